```python
import jax, jax.numpy as jnp
from jax import lax
import numpy as np

D_MODEL = 1024
BATCH = 32
SEQ = 2048
DEPTH = 1

GLA_HEADS = 4
GLA_V_W = D_MODEL // 2
GLA_DV = GLA_V_W // GLA_HEADS
GLA_DK = GLA_DV // 2
GLA_QK_W = GLA_HEADS * GLA_DK
GLA_GATE_RANK = 16
GLA_LOGIT_NORM = 16.0
GLA_CHUNK = 64
RWKV_HEAD = 64
RWKV_W = D_MODEL // 2
RWKV_HEADS = RWKV_W // RWKV_HEAD
RWKV_DECAY_LORA = 64
RWKV_AAA_LORA = 64
RWKV_GATE_LORA = 128
RWKV_GN_EPS = RWKV_HEAD * 1e-5
GLA_SPLITS = (GLA_QK_W, GLA_QK_W, GLA_V_W, GLA_V_W, GLA_GATE_RANK, GLA_GATE_RANK)
RWKV_SPLITS = (RWKV_W, RWKV_W, RWKV_W, RWKV_DECAY_LORA, RWKV_AAA_LORA, RWKV_GATE_LORA)
GLA_PROJ_W = sum(GLA_SPLITS)
RWKV_PROJ_W = sum(RWKV_SPLITS)
GATE_PROJ_W = 2 * D_MODEL
N_PROJ = GLA_PROJ_W + RWKV_PROJ_W + GATE_PROJ_W
D_FF = ((8 * D_MODEL // 3) + 63) // 64 * 64
CONV_W = 3
NORM_EPS = 1e-6
HEAD_NORM_EPS = 1e-5

kernel_name = "bidir_gla_rwkv7_gated_hybrid"


def _rmsnorm(x, g):
    xf = x.astype(jnp.float32)
    y = xf * lax.rsqrt(jnp.mean(xf * xf, axis=-1, keepdims=True) + NORM_EPS)
    return (y * g.astype(jnp.float32)).astype(x.dtype)


def _shift_prev(u):
    return jnp.pad(u[:, :-1], ((0, 0), (1, 0), (0, 0)))


def _shift_next(u):
    return jnp.pad(u[:, 1:], ((0, 0), (0, 1), (0, 0)))


def _split(t, sizes):
    return jnp.split(t, np.cumsum(sizes)[:-1].tolist(), axis=-1)


def _gla_chunked(q, k, v, log_a):
    f32 = jnp.float32
    B_, T, H, K = q.shape
    V = v.shape[-1]
    C = GLA_CHUNK
    n = T // C
    q = q.astype(f32).reshape(B_, n, C, H, K)
    k = k.astype(f32).reshape(B_, n, C, H, K)
    v = v.astype(f32).reshape(B_, n, C, H, V)
    b = jnp.cumsum(log_a.astype(f32).reshape(B_, n, C, H, K), axis=2)
    b_ref = b[:, :, C // 2:C // 2 + 1]
    qi = q * jnp.exp(b - b_ref)
    ki = k * jnp.exp(b_ref - b)
    A = jnp.einsum('bnchk,bnshk->bnhcs', qi, ki)
    A = jnp.where(jnp.tril(jnp.ones((C, C), dtype=bool)), A, 0.0)
    o_intra = jnp.einsum('bnhcs,bnshv->bnchv', A, v)
    b_last = b[:, :, -1:]
    kv = jnp.einsum('bnchk,bnchv->bnhkv', k * jnp.exp(b_last - b), v)
    decay = jnp.exp(b_last[:, :, 0])

    def step(S, inp):
        d, u = inp
        return S * d[..., None] + u, S

    _, S_prev = lax.scan(step, jnp.zeros((B_, H, K, V), f32),
                         (jnp.moveaxis(decay, 1, 0), jnp.moveaxis(kv, 1, 0)))
    S_prev = jnp.moveaxis(S_prev, 0, 1)
    o_inter = jnp.einsum('bnchk,bnhkv->bnchv', q * jnp.exp(b), S_prev)
    return (o_intra + o_inter).reshape(B_, T, H, V)


def _gla_branch(p, wa2_f, ba_f, wa2_b, ba_b, norm_g, proj):
    f32 = jnp.float32
    B_, T, _ = p.shape
    q, k, v, og, af, ab = _split(p, GLA_SPLITS)
    q = q.reshape(B_, T, GLA_HEADS, GLA_DK) * (GLA_DK ** -0.5)
    k = k.reshape(B_, T, GLA_HEADS, GLA_DK)
    v = v.reshape(B_, T, GLA_HEADS, GLA_DV)
    la_f = (jax.nn.log_sigmoid((af @ wa2_f + ba_f).astype(f32)) / GLA_LOGIT_NORM).reshape(B_, T, GLA_HEADS, GLA_DK)
    la_b = (jax.nn.log_sigmoid((ab @ wa2_b + ba_b).astype(f32)) / GLA_LOGIT_NORM).reshape(B_, T, GLA_HEADS, GLA_DK)
    flip = lambda t: jnp.flip(t, axis=1)
    o = _gla_chunked(q, k, v, la_f) + flip(_gla_chunked(flip(q), flip(k), flip(v), flip(la_b)))
    o = o * lax.rsqrt(jnp.mean(o * o, axis=-1, keepdims=True) + HEAD_NORM_EPS)
    o = (o.reshape(B_, T, GLA_V_W) * norm_g.astype(f32)).astype(p.dtype)
    o = o * jax.nn.silu(og)
    return o @ proj


def _rwkv7_step(S, inp):
    r, w, k, v, a, b = inp
    sa = jnp.einsum('bhvk,bhk->bhv', S, a)
    S = S * w[:, :, None, :] + sa[..., None] * b[:, :, None, :] + v[..., None] * k[:, :, None, :]
    y = jnp.einsum('bhvk,bhk->bhv', S, r)
    return S, y


def _rwkv_branch(p, mu_prev, mu_next, w0_f, w2_f, w0_b, w2_b, a0, a2, g2, k_k, k_a, r_k, ln_w, ln_b, proj):
    f32 = jnp.float32
    B_, T, _ = p.shape
    s = p + mu_prev * (_shift_prev(p) - p) + mu_next * (_shift_next(p) - p)
    r, k, v, wl, al, gl = _split(s, RWKV_SPLITS)
    tw = jnp.tanh(wl)

    def decay(w0, w2):
        w = -jax.nn.softplus(-(w0 + tw @ w2).astype(f32)) - 0.5
        return jnp.exp(-jnp.exp(w))

    a = jax.nn.sigmoid(a0 + al @ a2)
    g = jax.nn.sigmoid(gl) @ g2
    heads = lambda t: t.reshape(B_, T, RWKV_HEADS, RWKV_HEAD).astype(f32)
    kk = heads(k * k_k)
    kk = kk / jnp.maximum(jnp.sqrt(jnp.sum(kk * kk, axis=-1, keepdims=True)), 1e-12)
    k = k * (1.0 + (a - 1.0) * k_a)
    rh, kh, vh, ah = heads(r), heads(k), heads(v), heads(a)
    tm = lambda t: jnp.moveaxis(t, 1, 0)
    r_s, k_s, v_s, a_s, b_s = tm(rh), tm(kh), tm(vh), tm(-kk), tm(kk * ah)
    S0 = jnp.zeros((B_, RWKV_HEADS, RWKV_HEAD, RWKV_HEAD), f32)
    _, y_f = lax.scan(_rwkv7_step, S0, (r_s, tm(heads(decay(w0_f, w2_f))), k_s, v_s, a_s, b_s))
    _, y_b = lax.scan(_rwkv7_step, S0, (r_s, tm(heads(decay(w0_b, w2_b))), k_s, v_s, a_s, b_s), reverse=True)
    y = jnp.moveaxis(y_f + y_b, 0, 1)
    mu = jnp.mean(y, axis=-1, keepdims=True)
    var = jnp.mean(jnp.square(y - mu), axis=-1, keepdims=True)
    y = ((y - mu) * lax.rsqrt(var + RWKV_GN_EPS)).reshape(B_, T, RWKV_W) * ln_w.astype(f32) + ln_b.astype(f32)
    bonus = (jnp.sum(rh * kh * r_k.astype(f32), axis=-1, keepdims=True) * vh).reshape(B_, T, RWKV_W)
    o = (y + bonus).astype(p.dtype) * g
    return o @ proj


def _fwd_setup_inputs(seed: int = 0) -> dict:
    key = jax.random.key(seed)
    ks = jax.random.split(key, 32)
    L, D = DEPTH, D_MODEL
    f32 = jnp.float32
    nrm = lambda k, shape, scale: jax.random.normal(k, shape, f32) * scale
    uni = lambda k, shape: jax.random.uniform(k, shape, f32, 0.0, 0.5)
    centre = jnp.array([0.0, 1.0, 0.0], f32)[None, :, None]
    return {
        "x": nrm(ks[0], (BATCH, SEQ, D), 1.0),
        "norm1_g": 1.0 + nrm(ks[1], (L, D), 0.02),
        "w_in": nrm(ks[2], (L, D, N_PROJ), D ** -0.5),
        "gla_wa2_f": nrm(ks[3], (L, GLA_GATE_RANK, GLA_QK_W), GLA_GATE_RANK ** -0.5),
        "gla_ba_f": 1.0 + nrm(ks[4], (L, GLA_QK_W), 0.5),
        "gla_wa2_b": nrm(ks[5], (L, GLA_GATE_RANK, GLA_QK_W), GLA_GATE_RANK ** -0.5),
        "gla_ba_b": 1.0 + nrm(ks[6], (L, GLA_QK_W), 0.5),
        "gla_norm_g": 1.0 + nrm(ks[7], (L, GLA_V_W), 0.02),
        "gla_proj": nrm(ks[8], (L, GLA_V_W, D), GLA_V_W ** -0.5),
        "rwkv_mu_prev": uni(ks[9], (L, RWKV_PROJ_W)),
        "rwkv_mu_next": uni(ks[10], (L, RWKV_PROJ_W)),
        "rwkv_w0_f": -1.0 + nrm(ks[11], (L, RWKV_W), 0.5),
        "rwkv_w2_f": nrm(ks[12], (L, RWKV_DECAY_LORA, RWKV_W), RWKV_DECAY_LORA ** -0.5),
        "rwkv_w0_b": -1.0 + nrm(ks[13], (L, RWKV_W), 0.5),
        "rwkv_w2_b": nrm(ks[14], (L, RWKV_DECAY_LORA, RWKV_W), RWKV_DECAY_LORA ** -0.5),
        "rwkv_a0": nrm(ks[15], (L, RWKV_W), 0.1),
        "rwkv_a2": nrm(ks[16], (L, RWKV_AAA_LORA, RWKV_W), RWKV_AAA_LORA ** -0.5),
        "rwkv_g2": nrm(ks[17], (L, RWKV_GATE_LORA, RWKV_W), RWKV_GATE_LORA ** -0.5),
        "rwkv_k_k": 0.85 + nrm(ks[18], (L, RWKV_W), 0.05),
        "rwkv_k_a": 1.0 + nrm(ks[19], (L, RWKV_W), 0.05),
        "rwkv_r_k": nrm(ks[20], (L, RWKV_HEADS, RWKV_HEAD), 0.1),
        "rwkv_ln_w": 1.0 + nrm(ks[21], (L, RWKV_W), 0.02),
        "rwkv_ln_b": nrm(ks[22], (L, RWKV_W), 0.02),
        "rwkv_proj": nrm(ks[23], (L, RWKV_W, D), RWKV_W ** -0.5),
        "w_out": nrm(ks[24], (L, D, D), D ** -0.5),
        "norm2_g": 1.0 + nrm(ks[25], (L, D), 0.02),
        "ffn_up": nrm(ks[26], (L, D, 2 * D_FF), D ** -0.5),
        "ffn_conv_w": centre + nrm(ks[27], (L, CONV_W, 2 * D_FF), 0.2),
        "ffn_conv_b": nrm(ks[28], (L, 2 * D_FF), 0.02),
        "ffn_down": nrm(ks[29], (L, D_FF, D), D_FF ** -0.5),
        "norm_f_g": 1.0 + nrm(ks[30], (D,), 0.02),
    }


def _fwd_reference(x, norm1_g, w_in, gla_wa2_f, gla_ba_f, gla_wa2_b, gla_ba_b, gla_norm_g, gla_proj,
              rwkv_mu_prev, rwkv_mu_next, rwkv_w0_f, rwkv_w2_f, rwkv_w0_b, rwkv_w2_b, rwkv_a0, rwkv_a2,
              rwkv_g2, rwkv_k_k, rwkv_k_a, rwkv_r_k, rwkv_ln_w, rwkv_ln_b, rwkv_proj, w_out,
              norm2_g, ffn_up, ffn_conv_w, ffn_conv_b, ffn_down, norm_f_g):
    for l in range(DEPTH):
        h = _rmsnorm(x, norm1_g[l])
        p = h @ w_in[l]
        p_gla, p_rwkv, p_gate = jnp.split(p, [GLA_PROJ_W, GLA_PROJ_W + RWKV_PROJ_W], axis=-1)
        y_a = _gla_branch(p_gla, gla_wa2_f[l], gla_ba_f[l], gla_wa2_b[l], gla_ba_b[l],
                          gla_norm_g[l], gla_proj[l])
        y_b = _rwkv_branch(p_rwkv, rwkv_mu_prev[l], rwkv_mu_next[l], rwkv_w0_f[l], rwkv_w2_f[l],
                           rwkv_w0_b[l], rwkv_w2_b[l], rwkv_a0[l], rwkv_a2[l], rwkv_g2[l],
                           rwkv_k_k[l], rwkv_k_a[l], rwkv_r_k[l], rwkv_ln_w[l], rwkv_ln_b[l], rwkv_proj[l])
        gate_a, gate_b = jnp.split(p_gate, 2, axis=-1)
        merged = jax.nn.sigmoid(gate_a) * y_a + jax.nn.sigmoid(gate_b) * y_b
        x = x + merged @ w_out[l]
        h2 = _rmsnorm(x, norm2_g[l])
        u = h2 @ ffn_up[l]
        cw = ffn_conv_w[l]
        u = cw[0] * _shift_prev(u) + cw[1] * u + cw[2] * _shift_next(u) + ffn_conv_b[l]
        u_gate, u_val = jnp.split(u, 2, axis=-1)
        x = x + (jax.nn.silu(u_gate) * u_val) @ ffn_down[l]
    return _rmsnorm(x, norm_f_g)


import jax as _jax
import jax.numpy as _jnp

TWIN_FORMAT = 'train_step'
FWD_PARAMS = ['x', 'norm1_g', 'w_in', 'gla_wa2_f', 'gla_ba_f', 'gla_wa2_b', 'gla_ba_b', 'gla_norm_g', 'gla_proj', 'rwkv_mu_prev', 'rwkv_mu_next', 'rwkv_w0_f', 'rwkv_w2_f', 'rwkv_w0_b', 'rwkv_w2_b', 'rwkv_a0', 'rwkv_a2', 'rwkv_g2', 'rwkv_k_k', 'rwkv_k_a', 'rwkv_r_k', 'rwkv_ln_w', 'rwkv_ln_b', 'rwkv_proj', 'w_out', 'norm2_g', 'ffn_up', 'ffn_conv_w', 'ffn_conv_b', 'ffn_down', 'norm_f_g']
TWIN_WEIGHTS = ['norm1_g', 'w_in', 'gla_wa2_f', 'gla_ba_f', 'gla_wa2_b', 'gla_ba_b', 'gla_norm_g', 'gla_proj', 'rwkv_mu_prev', 'rwkv_mu_next', 'rwkv_w0_f', 'rwkv_w2_f', 'rwkv_w0_b', 'rwkv_w2_b', 'rwkv_a0', 'rwkv_a2', 'rwkv_g2', 'rwkv_k_k', 'rwkv_k_a', 'rwkv_r_k', 'rwkv_ln_w', 'rwkv_ln_b', 'rwkv_proj', 'w_out', 'norm2_g', 'ffn_up', 'ffn_conv_w', 'ffn_conv_b', 'ffn_down', 'norm_f_g']
TWIN_DIFF_INPUT = 'x'
TWIN_INPUTS = ['x', 'norm1_g', 'w_in', 'gla_wa2_f', 'gla_ba_f', 'gla_wa2_b', 'gla_ba_b', 'gla_norm_g', 'gla_proj', 'rwkv_mu_prev', 'rwkv_mu_next', 'rwkv_w0_f', 'rwkv_w2_f', 'rwkv_w0_b', 'rwkv_w2_b', 'rwkv_a0', 'rwkv_a2', 'rwkv_g2', 'rwkv_k_k', 'rwkv_k_a', 'rwkv_r_k', 'rwkv_ln_w', 'rwkv_ln_b', 'rwkv_proj', 'w_out', 'norm2_g', 'ffn_up', 'ffn_conv_w', 'ffn_conv_b', 'ffn_down', 'norm_f_g', 'loss_target', 'm_norm1_g', 'm_w_in', 'm_gla_wa2_f', 'm_gla_ba_f', 'm_gla_wa2_b', 'm_gla_ba_b', 'm_gla_norm_g', 'm_gla_proj', 'm_rwkv_mu_prev', 'm_rwkv_mu_next', 'm_rwkv_w0_f', 'm_rwkv_w2_f', 'm_rwkv_w0_b', 'm_rwkv_w2_b', 'm_rwkv_a0', 'm_rwkv_a2', 'm_rwkv_g2', 'm_rwkv_k_k', 'm_rwkv_k_a', 'm_rwkv_r_k', 'm_rwkv_ln_w', 'm_rwkv_ln_b', 'm_rwkv_proj', 'm_w_out', 'm_norm2_g', 'm_ffn_up', 'm_ffn_conv_w', 'm_ffn_conv_b', 'm_ffn_down', 'm_norm_f_g', 'v_norm1_g', 'v_w_in', 'v_gla_wa2_f', 'v_gla_ba_f', 'v_gla_wa2_b', 'v_gla_ba_b', 'v_gla_norm_g', 'v_gla_proj', 'v_rwkv_mu_prev', 'v_rwkv_mu_next', 'v_rwkv_w0_f', 'v_rwkv_w2_f', 'v_rwkv_w0_b', 'v_rwkv_w2_b', 'v_rwkv_a0', 'v_rwkv_a2', 'v_rwkv_g2', 'v_rwkv_k_k', 'v_rwkv_k_a', 'v_rwkv_r_k', 'v_rwkv_ln_w', 'v_rwkv_ln_b', 'v_rwkv_proj', 'v_w_out', 'v_norm2_g', 'v_ffn_up', 'v_ffn_conv_w', 'v_ffn_conv_b', 'v_ffn_down', 'v_norm_f_g']
TWIN_OUTPUTS = ['loss', 'grad_x', 'grad_norm1_g', 'grad_w_in', 'grad_gla_wa2_f', 'grad_gla_ba_f', 'grad_gla_wa2_b', 'grad_gla_ba_b', 'grad_gla_norm_g', 'grad_gla_proj', 'grad_rwkv_mu_prev', 'grad_rwkv_mu_next', 'grad_rwkv_w0_f', 'grad_rwkv_w2_f', 'grad_rwkv_w0_b', 'grad_rwkv_w2_b', 'grad_rwkv_a0', 'grad_rwkv_a2', 'grad_rwkv_g2', 'grad_rwkv_k_k', 'grad_rwkv_k_a', 'grad_rwkv_r_k', 'grad_rwkv_ln_w', 'grad_rwkv_ln_b', 'grad_rwkv_proj', 'grad_w_out', 'grad_norm2_g', 'grad_ffn_up', 'grad_ffn_conv_w', 'grad_ffn_conv_b', 'grad_ffn_down', 'grad_norm_f_g', 'delta_norm1_g', 'delta_w_in', 'delta_gla_wa2_f', 'delta_gla_ba_f', 'delta_gla_wa2_b', 'delta_gla_ba_b', 'delta_gla_norm_g', 'delta_gla_proj', 'delta_rwkv_mu_prev', 'delta_rwkv_mu_next', 'delta_rwkv_w0_f', 'delta_rwkv_w2_f', 'delta_rwkv_w0_b', 'delta_rwkv_w2_b', 'delta_rwkv_a0', 'delta_rwkv_a2', 'delta_rwkv_g2', 'delta_rwkv_k_k', 'delta_rwkv_k_a', 'delta_rwkv_r_k', 'delta_rwkv_ln_w', 'delta_rwkv_ln_b', 'delta_rwkv_proj', 'delta_w_out', 'delta_norm2_g', 'delta_ffn_up', 'delta_ffn_conv_w', 'delta_ffn_conv_b', 'delta_ffn_down', 'delta_norm_f_g', 'new_m_norm1_g', 'new_m_w_in', 'new_m_gla_wa2_f', 'new_m_gla_ba_f', 'new_m_gla_wa2_b', 'new_m_gla_ba_b', 'new_m_gla_norm_g', 'new_m_gla_proj', 'new_m_rwkv_mu_prev', 'new_m_rwkv_mu_next', 'new_m_rwkv_w0_f', 'new_m_rwkv_w2_f', 'new_m_rwkv_w0_b', 'new_m_rwkv_w2_b', 'new_m_rwkv_a0', 'new_m_rwkv_a2', 'new_m_rwkv_g2', 'new_m_rwkv_k_k', 'new_m_rwkv_k_a', 'new_m_rwkv_r_k', 'new_m_rwkv_ln_w', 'new_m_rwkv_ln_b', 'new_m_rwkv_proj', 'new_m_w_out', 'new_m_norm2_g', 'new_m_ffn_up', 'new_m_ffn_conv_w', 'new_m_ffn_conv_b', 'new_m_ffn_down', 'new_m_norm_f_g', 'new_v_norm1_g', 'new_v_w_in', 'new_v_gla_wa2_f', 'new_v_gla_ba_f', 'new_v_gla_wa2_b', 'new_v_gla_ba_b', 'new_v_gla_norm_g', 'new_v_gla_proj', 'new_v_rwkv_mu_prev', 'new_v_rwkv_mu_next', 'new_v_rwkv_w0_f', 'new_v_rwkv_w2_f', 'new_v_rwkv_w0_b', 'new_v_rwkv_w2_b', 'new_v_rwkv_a0', 'new_v_rwkv_a2', 'new_v_rwkv_g2', 'new_v_rwkv_k_k', 'new_v_rwkv_k_a', 'new_v_rwkv_r_k', 'new_v_rwkv_ln_w', 'new_v_rwkv_ln_b', 'new_v_rwkv_proj', 'new_v_w_out', 'new_v_norm2_g', 'new_v_ffn_up', 'new_v_ffn_conv_w', 'new_v_ffn_conv_b', 'new_v_ffn_down', 'new_v_norm_f_g']
TWIN_LEAF_KINDS = {'loss': 'loss', 'grad_x': 'grad_x', 'grad_norm1_g': 'grad_w', 'grad_w_in': 'grad_w', 'grad_gla_wa2_f': 'grad_w', 'grad_gla_ba_f': 'grad_w', 'grad_gla_wa2_b': 'grad_w', 'grad_gla_ba_b': 'grad_w', 'grad_gla_norm_g': 'grad_w', 'grad_gla_proj': 'grad_w', 'grad_rwkv_mu_prev': 'grad_w', 'grad_rwkv_mu_next': 'grad_w', 'grad_rwkv_w0_f': 'grad_w', 'grad_rwkv_w2_f': 'grad_w', 'grad_rwkv_w0_b': 'grad_w', 'grad_rwkv_w2_b': 'grad_w', 'grad_rwkv_a0': 'grad_w', 'grad_rwkv_a2': 'grad_w', 'grad_rwkv_g2': 'grad_w', 'grad_rwkv_k_k': 'grad_w', 'grad_rwkv_k_a': 'grad_w', 'grad_rwkv_r_k': 'grad_w', 'grad_rwkv_ln_w': 'grad_w', 'grad_rwkv_ln_b': 'grad_w', 'grad_rwkv_proj': 'grad_w', 'grad_w_out': 'grad_w', 'grad_norm2_g': 'grad_w', 'grad_ffn_up': 'grad_w', 'grad_ffn_conv_w': 'grad_w', 'grad_ffn_conv_b': 'grad_w', 'grad_ffn_down': 'grad_w', 'grad_norm_f_g': 'grad_w', 'delta_norm1_g': 'delta_w', 'delta_w_in': 'delta_w', 'delta_gla_wa2_f': 'delta_w', 'delta_gla_ba_f': 'delta_w', 'delta_gla_wa2_b': 'delta_w', 'delta_gla_ba_b': 'delta_w', 'delta_gla_norm_g': 'delta_w', 'delta_gla_proj': 'delta_w', 'delta_rwkv_mu_prev': 'delta_w', 'delta_rwkv_mu_next': 'delta_w', 'delta_rwkv_w0_f': 'delta_w', 'delta_rwkv_w2_f': 'delta_w', 'delta_rwkv_w0_b': 'delta_w', 'delta_rwkv_w2_b': 'delta_w', 'delta_rwkv_a0': 'delta_w', 'delta_rwkv_a2': 'delta_w', 'delta_rwkv_g2': 'delta_w', 'delta_rwkv_k_k': 'delta_w', 'delta_rwkv_k_a': 'delta_w', 'delta_rwkv_r_k': 'delta_w', 'delta_rwkv_ln_w': 'delta_w', 'delta_rwkv_ln_b': 'delta_w', 'delta_rwkv_proj': 'delta_w', 'delta_w_out': 'delta_w', 'delta_norm2_g': 'delta_w', 'delta_ffn_up': 'delta_w', 'delta_ffn_conv_w': 'delta_w', 'delta_ffn_conv_b': 'delta_w', 'delta_ffn_down': 'delta_w', 'delta_norm_f_g': 'delta_w', 'new_m_norm1_g': 'new_m', 'new_m_w_in': 'new_m', 'new_m_gla_wa2_f': 'new_m', 'new_m_gla_ba_f': 'new_m', 'new_m_gla_wa2_b': 'new_m', 'new_m_gla_ba_b': 'new_m', 'new_m_gla_norm_g': 'new_m', 'new_m_gla_proj': 'new_m', 'new_m_rwkv_mu_prev': 'new_m', 'new_m_rwkv_mu_next': 'new_m', 'new_m_rwkv_w0_f': 'new_m', 'new_m_rwkv_w2_f': 'new_m', 'new_m_rwkv_w0_b': 'new_m', 'new_m_rwkv_w2_b': 'new_m', 'new_m_rwkv_a0': 'new_m', 'new_m_rwkv_a2': 'new_m', 'new_m_rwkv_g2': 'new_m', 'new_m_rwkv_k_k': 'new_m', 'new_m_rwkv_k_a': 'new_m', 'new_m_rwkv_r_k': 'new_m', 'new_m_rwkv_ln_w': 'new_m', 'new_m_rwkv_ln_b': 'new_m', 'new_m_rwkv_proj': 'new_m', 'new_m_w_out': 'new_m', 'new_m_norm2_g': 'new_m', 'new_m_ffn_up': 'new_m', 'new_m_ffn_conv_w': 'new_m', 'new_m_ffn_conv_b': 'new_m', 'new_m_ffn_down': 'new_m', 'new_m_norm_f_g': 'new_m', 'new_v_norm1_g': 'new_v', 'new_v_w_in': 'new_v', 'new_v_gla_wa2_f': 'new_v', 'new_v_gla_ba_f': 'new_v', 'new_v_gla_wa2_b': 'new_v', 'new_v_gla_ba_b': 'new_v', 'new_v_gla_norm_g': 'new_v', 'new_v_gla_proj': 'new_v', 'new_v_rwkv_mu_prev': 'new_v', 'new_v_rwkv_mu_next': 'new_v', 'new_v_rwkv_w0_f': 'new_v', 'new_v_rwkv_w2_f': 'new_v', 'new_v_rwkv_w0_b': 'new_v', 'new_v_rwkv_w2_b': 'new_v', 'new_v_rwkv_a0': 'new_v', 'new_v_rwkv_a2': 'new_v', 'new_v_rwkv_g2': 'new_v', 'new_v_rwkv_k_k': 'new_v', 'new_v_rwkv_k_a': 'new_v', 'new_v_rwkv_r_k': 'new_v', 'new_v_rwkv_ln_w': 'new_v', 'new_v_rwkv_ln_b': 'new_v', 'new_v_rwkv_proj': 'new_v', 'new_v_w_out': 'new_v', 'new_v_norm2_g': 'new_v', 'new_v_ffn_up': 'new_v', 'new_v_ffn_conv_w': 'new_v', 'new_v_ffn_conv_b': 'new_v', 'new_v_ffn_down': 'new_v', 'new_v_norm_f_g': 'new_v'}


def _forward(args):
    return _fwd_reference(*[args[k] for k in FWD_PARAMS])


def _output_shape():
    out = _jax.eval_shape(lambda: _forward(_fwd_setup_inputs(0)))
    return out.shape, out.dtype

N_MICROBATCH = 1
ADAM_LR = 0.001
ADAM_B1 = 0.9
ADAM_B2 = 0.999
ADAM_EPS = 1e-08
ADAM_WD = 0.01
ADAM_STEP = 10
PER_EXAMPLE_BATCH_AXIS = {'x': 0, 'loss_target': 0}
SHARED_INPUTS = []
_WEIGHT_DTYPES = {'norm1_g': _jnp.float32, 'w_in': _jnp.float32, 'gla_wa2_f': _jnp.float32, 'gla_ba_f': _jnp.float32, 'gla_wa2_b': _jnp.float32, 'gla_ba_b': _jnp.float32, 'gla_norm_g': _jnp.float32, 'gla_proj': _jnp.float32, 'rwkv_mu_prev': _jnp.float32, 'rwkv_mu_next': _jnp.float32, 'rwkv_w0_f': _jnp.float32, 'rwkv_w2_f': _jnp.float32, 'rwkv_w0_b': _jnp.float32, 'rwkv_w2_b': _jnp.float32, 'rwkv_a0': _jnp.float32, 'rwkv_a2': _jnp.float32, 'rwkv_g2': _jnp.float32, 'rwkv_k_k': _jnp.float32, 'rwkv_k_a': _jnp.float32, 'rwkv_r_k': _jnp.float32, 'rwkv_ln_w': _jnp.float32, 'rwkv_ln_b': _jnp.float32, 'rwkv_proj': _jnp.float32, 'w_out': _jnp.float32, 'norm2_g': _jnp.float32, 'ffn_up': _jnp.float32, 'ffn_conv_w': _jnp.float32, 'ffn_conv_b': _jnp.float32, 'ffn_down': _jnp.float32, 'norm_f_g': _jnp.float32}
MOMENT_SCALE = {'norm1_g': 2.355629e-01, 'w_in': 1.003681e-01, 'gla_wa2_f': 1.803840e-02, 'gla_ba_f': 1.089280e-01, 'gla_wa2_b': 1.624368e-02, 'gla_ba_b': 8.066864e-02, 'gla_norm_g': 1.262318e-01, 'gla_proj': 8.455359e-02, 'rwkv_mu_prev': 2.173313e-01, 'rwkv_mu_next': 2.272838e-01, 'rwkv_w0_f': 3.700582e-02, 'rwkv_w2_f': 9.417542e-03, 'rwkv_w0_b': 3.975601e-02, 'rwkv_w2_b': 9.446213e-03, 'rwkv_a0': 5.121699e-02, 'rwkv_a2': 3.305286e-02, 'rwkv_g2': 1.104285e-01, 'rwkv_k_k': 5.380110e-02, 'rwkv_k_a': 1.410349e-01, 'rwkv_r_k': 1.462693e-01, 'rwkv_ln_w': 1.097657e-01, 'rwkv_ln_b': 1.182990e-01, 'rwkv_proj': 7.527800e-02, 'w_out': 1.132188e-01, 'norm2_g': 1.985462e-01, 'ffn_up': 8.246232e-02, 'ffn_conv_w': 8.240647e-02, 'ffn_conv_b': 7.722638e-02, 'ffn_down': 1.331144e-01, 'norm_f_g': 6.385574e+01}


def _to_microbatches(a, axis):
    t = _jnp.moveaxis(a, axis, 0)
    t = t.reshape((N_MICROBATCH, t.shape[0] // N_MICROBATCH) + t.shape[1:])
    return _jnp.moveaxis(t, 1, axis + 1)


def setup_inputs(seed: int = 0) -> dict:
    inp = _fwd_setup_inputs(seed)
    key = _jax.random.fold_in(_jax.random.key(seed), 7919)
    shape, _ = _output_shape()
    out = dict(inp)
    out["loss_target"] = _jax.random.normal(_jax.random.fold_in(key, 0), shape, _jnp.float32)
    for i, name in enumerate(TWIN_WEIGHTS):
        w = inp[name].astype(_jnp.float32)
        if MOMENT_SCALE is None:
            s = _jnp.sqrt(_jnp.mean(_jnp.square(w)) + 1e-30)
        else:
            s = MOMENT_SCALE[name]
        km, kv = _jax.random.split(_jax.random.fold_in(key, i + 1))
        out[name] = w
        out["m_" + name] = s * _jax.random.normal(km, w.shape, _jnp.float32)
        out["v_" + name] = (s * s) * _jax.random.uniform(kv, w.shape, _jnp.float32, 0.5, 1.5)
    if N_MICROBATCH > 1:
        for name, axis in PER_EXAMPLE_BATCH_AXIS.items():
            out[name] = _to_microbatches(out[name], axis)
    return {'x': out['x'], 'norm1_g': out['norm1_g'], 'w_in': out['w_in'], 'gla_wa2_f': out['gla_wa2_f'], 'gla_ba_f': out['gla_ba_f'], 'gla_wa2_b': out['gla_wa2_b'], 'gla_ba_b': out['gla_ba_b'], 'gla_norm_g': out['gla_norm_g'], 'gla_proj': out['gla_proj'], 'rwkv_mu_prev': out['rwkv_mu_prev'], 'rwkv_mu_next': out['rwkv_mu_next'], 'rwkv_w0_f': out['rwkv_w0_f'], 'rwkv_w2_f': out['rwkv_w2_f'], 'rwkv_w0_b': out['rwkv_w0_b'], 'rwkv_w2_b': out['rwkv_w2_b'], 'rwkv_a0': out['rwkv_a0'], 'rwkv_a2': out['rwkv_a2'], 'rwkv_g2': out['rwkv_g2'], 'rwkv_k_k': out['rwkv_k_k'], 'rwkv_k_a': out['rwkv_k_a'], 'rwkv_r_k': out['rwkv_r_k'], 'rwkv_ln_w': out['rwkv_ln_w'], 'rwkv_ln_b': out['rwkv_ln_b'], 'rwkv_proj': out['rwkv_proj'], 'w_out': out['w_out'], 'norm2_g': out['norm2_g'], 'ffn_up': out['ffn_up'], 'ffn_conv_w': out['ffn_conv_w'], 'ffn_conv_b': out['ffn_conv_b'], 'ffn_down': out['ffn_down'], 'norm_f_g': out['norm_f_g'], 'loss_target': out['loss_target'], 'm_norm1_g': out['m_norm1_g'], 'm_w_in': out['m_w_in'], 'm_gla_wa2_f': out['m_gla_wa2_f'], 'm_gla_ba_f': out['m_gla_ba_f'], 'm_gla_wa2_b': out['m_gla_wa2_b'], 'm_gla_ba_b': out['m_gla_ba_b'], 'm_gla_norm_g': out['m_gla_norm_g'], 'm_gla_proj': out['m_gla_proj'], 'm_rwkv_mu_prev': out['m_rwkv_mu_prev'], 'm_rwkv_mu_next': out['m_rwkv_mu_next'], 'm_rwkv_w0_f': out['m_rwkv_w0_f'], 'm_rwkv_w2_f': out['m_rwkv_w2_f'], 'm_rwkv_w0_b': out['m_rwkv_w0_b'], 'm_rwkv_w2_b': out['m_rwkv_w2_b'], 'm_rwkv_a0': out['m_rwkv_a0'], 'm_rwkv_a2': out['m_rwkv_a2'], 'm_rwkv_g2': out['m_rwkv_g2'], 'm_rwkv_k_k': out['m_rwkv_k_k'], 'm_rwkv_k_a': out['m_rwkv_k_a'], 'm_rwkv_r_k': out['m_rwkv_r_k'], 'm_rwkv_ln_w': out['m_rwkv_ln_w'], 'm_rwkv_ln_b': out['m_rwkv_ln_b'], 'm_rwkv_proj': out['m_rwkv_proj'], 'm_w_out': out['m_w_out'], 'm_norm2_g': out['m_norm2_g'], 'm_ffn_up': out['m_ffn_up'], 'm_ffn_conv_w': out['m_ffn_conv_w'], 'm_ffn_conv_b': out['m_ffn_conv_b'], 'm_ffn_down': out['m_ffn_down'], 'm_norm_f_g': out['m_norm_f_g'], 'v_norm1_g': out['v_norm1_g'], 'v_w_in': out['v_w_in'], 'v_gla_wa2_f': out['v_gla_wa2_f'], 'v_gla_ba_f': out['v_gla_ba_f'], 'v_gla_wa2_b': out['v_gla_wa2_b'], 'v_gla_ba_b': out['v_gla_ba_b'], 'v_gla_norm_g': out['v_gla_norm_g'], 'v_gla_proj': out['v_gla_proj'], 'v_rwkv_mu_prev': out['v_rwkv_mu_prev'], 'v_rwkv_mu_next': out['v_rwkv_mu_next'], 'v_rwkv_w0_f': out['v_rwkv_w0_f'], 'v_rwkv_w2_f': out['v_rwkv_w2_f'], 'v_rwkv_w0_b': out['v_rwkv_w0_b'], 'v_rwkv_w2_b': out['v_rwkv_w2_b'], 'v_rwkv_a0': out['v_rwkv_a0'], 'v_rwkv_a2': out['v_rwkv_a2'], 'v_rwkv_g2': out['v_rwkv_g2'], 'v_rwkv_k_k': out['v_rwkv_k_k'], 'v_rwkv_k_a': out['v_rwkv_k_a'], 'v_rwkv_r_k': out['v_rwkv_r_k'], 'v_rwkv_ln_w': out['v_rwkv_ln_w'], 'v_rwkv_ln_b': out['v_rwkv_ln_b'], 'v_rwkv_proj': out['v_rwkv_proj'], 'v_w_out': out['v_w_out'], 'v_norm2_g': out['v_norm2_g'], 'v_ffn_up': out['v_ffn_up'], 'v_ffn_conv_w': out['v_ffn_conv_w'], 'v_ffn_conv_b': out['v_ffn_conv_b'], 'v_ffn_down': out['v_ffn_down'], 'v_norm_f_g': out['v_norm_f_g']}


def _loss(weights, diff, rest, loss_target):
    with _jax.named_scope("forward"):
        args = {**rest, TWIN_DIFF_INPUT: diff, **{k: w.astype(_WEIGHT_DTYPES[k]) for k, w in weights.items()}}
        y = _forward(args)
    with _jax.named_scope("loss_head"):
        err = _jnp.square(y.astype(_jnp.float32) - loss_target)
        return 0.5 * _jnp.sum(_jnp.mean(err, axis=-1)) if err.ndim else 0.5 * err


def _adamw(w, g, m, v):
    m = ADAM_B1 * m + (1.0 - ADAM_B1) * g
    v = ADAM_B2 * v + (1.0 - ADAM_B2) * _jnp.square(g)
    m_hat = m / (1.0 - ADAM_B1 ** ADAM_STEP)
    v_hat = v / (1.0 - ADAM_B2 ** ADAM_STEP)
    delta = -ADAM_LR * (m_hat / (_jnp.sqrt(v_hat) + ADAM_EPS) + ADAM_WD * w)
    return delta, m, v


def reference(x, norm1_g, w_in, gla_wa2_f, gla_ba_f, gla_wa2_b, gla_ba_b, gla_norm_g, gla_proj, rwkv_mu_prev, rwkv_mu_next, rwkv_w0_f, rwkv_w2_f, rwkv_w0_b, rwkv_w2_b, rwkv_a0, rwkv_a2, rwkv_g2, rwkv_k_k, rwkv_k_a, rwkv_r_k, rwkv_ln_w, rwkv_ln_b, rwkv_proj, w_out, norm2_g, ffn_up, ffn_conv_w, ffn_conv_b, ffn_down, norm_f_g, loss_target, m_norm1_g, m_w_in, m_gla_wa2_f, m_gla_ba_f, m_gla_wa2_b, m_gla_ba_b, m_gla_norm_g, m_gla_proj, m_rwkv_mu_prev, m_rwkv_mu_next, m_rwkv_w0_f, m_rwkv_w2_f, m_rwkv_w0_b, m_rwkv_w2_b, m_rwkv_a0, m_rwkv_a2, m_rwkv_g2, m_rwkv_k_k, m_rwkv_k_a, m_rwkv_r_k, m_rwkv_ln_w, m_rwkv_ln_b, m_rwkv_proj, m_w_out, m_norm2_g, m_ffn_up, m_ffn_conv_w, m_ffn_conv_b, m_ffn_down, m_norm_f_g, v_norm1_g, v_w_in, v_gla_wa2_f, v_gla_ba_f, v_gla_wa2_b, v_gla_ba_b, v_gla_norm_g, v_gla_proj, v_rwkv_mu_prev, v_rwkv_mu_next, v_rwkv_w0_f, v_rwkv_w2_f, v_rwkv_w0_b, v_rwkv_w2_b, v_rwkv_a0, v_rwkv_a2, v_rwkv_g2, v_rwkv_k_k, v_rwkv_k_a, v_rwkv_r_k, v_rwkv_ln_w, v_rwkv_ln_b, v_rwkv_proj, v_w_out, v_norm2_g, v_ffn_up, v_ffn_conv_w, v_ffn_conv_b, v_ffn_down, v_norm_f_g):
    given = dict(x=x, norm1_g=norm1_g, w_in=w_in, gla_wa2_f=gla_wa2_f, gla_ba_f=gla_ba_f, gla_wa2_b=gla_wa2_b, gla_ba_b=gla_ba_b, gla_norm_g=gla_norm_g, gla_proj=gla_proj, rwkv_mu_prev=rwkv_mu_prev, rwkv_mu_next=rwkv_mu_next, rwkv_w0_f=rwkv_w0_f, rwkv_w2_f=rwkv_w2_f, rwkv_w0_b=rwkv_w0_b, rwkv_w2_b=rwkv_w2_b, rwkv_a0=rwkv_a0, rwkv_a2=rwkv_a2, rwkv_g2=rwkv_g2, rwkv_k_k=rwkv_k_k, rwkv_k_a=rwkv_k_a, rwkv_r_k=rwkv_r_k, rwkv_ln_w=rwkv_ln_w, rwkv_ln_b=rwkv_ln_b, rwkv_proj=rwkv_proj, w_out=w_out, norm2_g=norm2_g, ffn_up=ffn_up, ffn_conv_w=ffn_conv_w, ffn_conv_b=ffn_conv_b, ffn_down=ffn_down, norm_f_g=norm_f_g, loss_target=loss_target, m_norm1_g=m_norm1_g, m_w_in=m_w_in, m_gla_wa2_f=m_gla_wa2_f, m_gla_ba_f=m_gla_ba_f, m_gla_wa2_b=m_gla_wa2_b, m_gla_ba_b=m_gla_ba_b, m_gla_norm_g=m_gla_norm_g, m_gla_proj=m_gla_proj, m_rwkv_mu_prev=m_rwkv_mu_prev, m_rwkv_mu_next=m_rwkv_mu_next, m_rwkv_w0_f=m_rwkv_w0_f, m_rwkv_w2_f=m_rwkv_w2_f, m_rwkv_w0_b=m_rwkv_w0_b, m_rwkv_w2_b=m_rwkv_w2_b, m_rwkv_a0=m_rwkv_a0, m_rwkv_a2=m_rwkv_a2, m_rwkv_g2=m_rwkv_g2, m_rwkv_k_k=m_rwkv_k_k, m_rwkv_k_a=m_rwkv_k_a, m_rwkv_r_k=m_rwkv_r_k, m_rwkv_ln_w=m_rwkv_ln_w, m_rwkv_ln_b=m_rwkv_ln_b, m_rwkv_proj=m_rwkv_proj, m_w_out=m_w_out, m_norm2_g=m_norm2_g, m_ffn_up=m_ffn_up, m_ffn_conv_w=m_ffn_conv_w, m_ffn_conv_b=m_ffn_conv_b, m_ffn_down=m_ffn_down, m_norm_f_g=m_norm_f_g, v_norm1_g=v_norm1_g, v_w_in=v_w_in, v_gla_wa2_f=v_gla_wa2_f, v_gla_ba_f=v_gla_ba_f, v_gla_wa2_b=v_gla_wa2_b, v_gla_ba_b=v_gla_ba_b, v_gla_norm_g=v_gla_norm_g, v_gla_proj=v_gla_proj, v_rwkv_mu_prev=v_rwkv_mu_prev, v_rwkv_mu_next=v_rwkv_mu_next, v_rwkv_w0_f=v_rwkv_w0_f, v_rwkv_w2_f=v_rwkv_w2_f, v_rwkv_w0_b=v_rwkv_w0_b, v_rwkv_w2_b=v_rwkv_w2_b, v_rwkv_a0=v_rwkv_a0, v_rwkv_a2=v_rwkv_a2, v_rwkv_g2=v_rwkv_g2, v_rwkv_k_k=v_rwkv_k_k, v_rwkv_k_a=v_rwkv_k_a, v_rwkv_r_k=v_rwkv_r_k, v_rwkv_ln_w=v_rwkv_ln_w, v_rwkv_ln_b=v_rwkv_ln_b, v_rwkv_proj=v_rwkv_proj, v_w_out=v_w_out, v_norm2_g=v_norm2_g, v_ffn_up=v_ffn_up, v_ffn_conv_w=v_ffn_conv_w, v_ffn_conv_b=v_ffn_conv_b, v_ffn_down=v_ffn_down, v_norm_f_g=v_norm_f_g)
    weights = {n: given[n] for n in TWIN_WEIGHTS}
    shared = {n: given[n] for n in SHARED_INPUTS}
    per_example = {n: given[n] for n in ['x']}
    grad_fn = _jax.value_and_grad(_loss, argnums=(0, 1))

    def one_microbatch(ex, loss_target):
        ex = dict(ex)
        diff = ex.pop(TWIN_DIFF_INPUT)
        return grad_fn(weights, diff, {**shared, **ex}, loss_target)

    if N_MICROBATCH == 1:
        loss, (grad_w, grad_x) = one_microbatch(per_example, given["loss_target"])
    else:
        def body(carry, xs):
            loss_sum, grad_sum = carry
            l_k, (gw_k, gx_k) = one_microbatch(xs[0], xs[1])
            with _jax.named_scope("update"):
                return (loss_sum + l_k, _jax.tree.map(_jnp.add, grad_sum, gw_k)), gx_k

        init = (_jnp.zeros((), _jnp.float32), _jax.tree.map(_jnp.zeros_like, weights))
        (loss, grad_w), grad_x = _jax.lax.scan(body, init, (per_example, given["loss_target"]))
    with _jax.named_scope("update"):
        delta_w, new_m, new_v = {}, {}, {}
        for n in TWIN_WEIGHTS:
            delta_w[n], new_m[n], new_v[n] = _adamw(weights[n], grad_w[n], given["m_" + n], given["v_" + n])
    return (loss, grad_x, *[grad_w[n] for n in TWIN_WEIGHTS], *[delta_w[n] for n in TWIN_WEIGHTS],
            *[new_m[n] for n in TWIN_WEIGHTS], *[new_v[n] for n in TWIN_WEIGHTS])
```

```python
import functools

import jax
import jax.numpy as jnp
import numpy as np
from jax import lax
from jax.experimental import pallas as pl
from jax.experimental.pallas import tpu as pltpu

F32 = jnp.float32
MXU_DTYPE = jnp.bfloat16

D = 1024
SEQ = 2048
GLA_H, GLA_DK, GLA_DV, GLA_CHUNK = 4, 64, 128, 64
GLA_RANK = 16
GLA_LOGIT_NORM = 16.0
RW_H, RW_N = 8, 64
RW_W = 512
D_FF = 2752
NORM_EPS = 1e-6
HEAD_NORM_EPS = 1e-5
RW_GN_EPS = RW_N * 1e-5
N_DEV = 8
ADAM_LR, ADAM_B1, ADAM_B2, ADAM_EPS, ADAM_WD, ADAM_STEP = 0.001, 0.9, 0.999, 1e-08, 0.01, 10

C_GA, C_GB, C_Q, C_K, C_V, C_OG = 0, 1024, 2048, 2304, 2560, 3072
C_RW = 3584
C_R, C_RK, C_RV, C_WLAL, C_GL = 3584, 4096, 4608, 5120, 5248
C_AFAB = 5376
NP = 5632
RW_PW = 1792
FFP = 2816
LANE = 128
VMEM_LIMIT = 56 * 1024 * 1024


def _cparams(sem):
    return pltpu.CompilerParams(dimension_semantics=sem, vmem_limit_bytes=VMEM_LIMIT)


@jax.custom_vjp
def mm(a, b):
    return jnp.dot(a.astype(MXU_DTYPE), b.astype(MXU_DTYPE), preferred_element_type=F32)


def _mm_fwd(a, b):
    return mm(a, b), (a, b)


def _mm_bwd(res, g):
    a, b = res
    gb = g.astype(MXU_DTYPE)
    da = lax.dot_general(gb, b.astype(MXU_DTYPE), (((1,), (1,)), ((), ())), preferred_element_type=F32)
    db = lax.dot_general(a.astype(MXU_DTYPE), gb, (((0,), (0,)), ((), ())), preferred_element_type=F32)
    return da.astype(a.dtype), db.astype(b.dtype)


mm.defvjp(_mm_fwd, _mm_bwd)


@jax.custom_vjp
def mm_nt(a, b):
    return lax.dot_general(a.astype(MXU_DTYPE), b.astype(MXU_DTYPE), (((1,), (1,)), ((), ())), preferred_element_type=F32)


def _mm_nt_fwd(a, b):
    return mm_nt(a, b), (a, b)


def _mm_nt_bwd(res, g):
    a, b = res
    gb = g.astype(MXU_DTYPE)
    da = jnp.dot(gb, b.astype(MXU_DTYPE), preferred_element_type=F32)
    db = lax.dot_general(gb, a.astype(MXU_DTYPE), (((0,), (0,)), ((), ())), preferred_element_type=F32)
    return da.astype(a.dtype), db.astype(b.dtype)


mm_nt.defvjp(_mm_nt_fwd, _mm_nt_bwd)


@jax.custom_vjp
def mm_tn(a, b):
    return lax.dot_general(a.astype(MXU_DTYPE), b.astype(MXU_DTYPE), (((0,), (0,)), ((), ())), preferred_element_type=F32)


def _mm_tn_fwd(a, b):
    return mm_tn(a, b), (a, b)


def _mm_tn_bwd(res, g):
    a, b = res
    gb = g.astype(MXU_DTYPE)
    da = lax.dot_general(b.astype(MXU_DTYPE), gb, (((1,), (1,)), ((), ())), preferred_element_type=F32)
    db = jnp.dot(a.astype(MXU_DTYPE), gb, preferred_element_type=F32)
    return da.astype(a.dtype), db.astype(b.dtype)


mm_tn.defvjp(_mm_tn_fwd, _mm_tn_bwd)


def mm_exact(a, b):
    return jnp.dot(a, b, preferred_element_type=F32, precision=lax.Precision.HIGHEST)


def mm_tn_exact(a, b):
    return lax.dot_general(a, b, (((0,), (0,)), ((), ())), preferred_element_type=F32, precision=lax.Precision.HIGHEST)


def _softplus(x):
    return jnp.maximum(x, 0.0) + jnp.log(1.0 + jnp.exp(-jnp.abs(x)))


def _sigmoid(x):
    return 1.0 / (1.0 + jnp.exp(-x))


def _silu(x):
    return x * _sigmoid(x)


def _rmsnorm(x, g):
    return x * lax.rsqrt(jnp.mean(x * x, axis=-1, keepdims=True) + NORM_EPS) * g


def _segment_ones(width, seg):
    i = lax.broadcasted_iota(jnp.int32, (width, width), 0) // seg
    j = lax.broadcasted_iota(jnp.int32, (width, width), 1) // seg
    return (i == j).astype(F32)


def _row_spec(tm, width, cb):
    return pl.BlockSpec((tm, width), lambda i: (i, cb))


def _full_spec(shape):
    nd = len(shape)
    return pl.BlockSpec(tuple(shape), lambda i: (0,) * nd)


def rowwise_fwd(name, f, rows, params, outs, tm):
    n = rows[0][0].shape[0]
    nr, npar = len(rows), len(params)

    def body(*refs):
        rv = [r[...] for r in refs[:nr]]
        pv = [r[...] for r in refs[nr:nr + npar]]
        res = f(rv, pv)
        for o_ref, val in zip(refs[nr + npar:], res):
            o_ref[...] = val.astype(o_ref.dtype)

    return pl.pallas_call(
        body, name=name, grid=(n // tm,),
        in_specs=[_row_spec(tm, w, cb) for _, w, cb in rows] + [_full_spec(p.shape) for p in params],
        out_specs=[_row_spec(tm, w, 0) for w, _ in outs],
        out_shape=[jax.ShapeDtypeStruct((n, w), dt) for w, dt in outs],
        compiler_params=_cparams(("arbitrary",)),
    )(*[a for a, _, _ in rows], *params)


def rowwise_bwd(name, f, rows, params, douts, tm, adds=(), grad_rows=None):
    n = rows[0][0].shape[0]
    nr, npar = len(rows), len(params)
    grad_rows = list(range(nr)) if grad_rows is None else list(grad_rows)
    flat_d = [d for group in douts for d in group]
    nd, na, ng = len(flat_d), len(adds), len(grad_rows)

    def body(*refs):
        rv = [r[...] for r in refs[:nr]]
        pv = [r[...] for r in refs[nr:nr + npar]]
        dflat = [r[...].astype(F32) for r in refs[nr + npar:nr + npar + nd]]
        av = [r[...] for r in refs[nr + npar + nd:nr + npar + nd + na]]
        o = nr + npar + nd + na
        drow_refs, dpar_refs = refs[o:o + ng], refs[o + ng:o + ng + npar]
        dv, pos = [], 0
        for group in douts:
            dv.append(sum(dflat[pos + 1:pos + len(group)], dflat[pos]))
            pos += len(group)

        @pl.when(pl.program_id(0) == 0)
        def _():
            for r in dpar_refs:
                r[...] = jnp.zeros_like(r)

        def g(grows, pars):
            full = list(rv)
            for i, val in zip(grad_rows, grows):
                full[i] = val
            return f(full, pars)

        res, vjp = jax.vjp(g, [rv[i] for i in grad_rows], pv)
        drows, dpars = vjp([d.astype(r.dtype) for d, r in zip(dv, res)])
        drows = [d.astype(F32) for d in drows]
        for (idx, _), a in zip(adds, av):
            drows[idx] = drows[idx] + a.astype(F32)
        for r, d in zip(drow_refs, drows):
            r[...] = d
        for r, d in zip(dpar_refs, dpars):
            r[...] += d.astype(F32)

    res = pl.pallas_call(
        body, name=name, grid=(n // tm,),
        in_specs=[_row_spec(tm, w, cb) for _, w, cb in rows] + [_full_spec(p.shape) for p in params]
        + [_row_spec(tm, w, cb) for _, w, cb in flat_d] + [_row_spec(tm, w, cb) for _, (_, w, cb) in adds],
        out_specs=[_row_spec(tm, rows[i][1], 0) for i in grad_rows] + [_full_spec(p.shape) for p in params],
        out_shape=[jax.ShapeDtypeStruct((n, rows[i][1]), F32) for i in grad_rows]
        + [jax.ShapeDtypeStruct(p.shape, F32) for p in params],
        compiler_params=_cparams(("arbitrary",)),
    )(*[a for a, _, _ in rows], *params, *[a for a, _, _ in flat_d], *[a for _, (a, _, _) in adds])
    return res[:ng], res[ng:]


def matmul(name, a, b, mode, out_dtype, tm, tn, tk):
    if mode == "nn":
        (m, k), n = a.shape, b.shape[1]
        a_spec = pl.BlockSpec((tm, tk), lambda i, j, kk: (i, kk))
        b_spec = pl.BlockSpec((tk, tn), lambda i, j, kk: (kk, j))
        dims = (((1,), (0,)), ((), ()))
    elif mode == "nt":
        (m, k), n = a.shape, b.shape[0]
        a_spec = pl.BlockSpec((tm, tk), lambda i, j, kk: (i, kk))
        b_spec = pl.BlockSpec((tn, tk), lambda i, j, kk: (j, kk))
        dims = (((1,), (1,)), ((), ()))
    else:
        (k, m), n = a.shape, b.shape[1]
        a_spec = pl.BlockSpec((tk, tm), lambda i, j, kk: (kk, i))
        b_spec = pl.BlockSpec((tk, tn), lambda i, j, kk: (kk, j))
        dims = (((0,), (0,)), ((), ()))
    assert m % tm == 0 and n % tn == 0 and k % tk == 0, (name, a.shape, b.shape, tm, tn, tk)
    nk = k // tk

    def body(a_ref, b_ref, o_ref, acc_ref):
        kk = pl.program_id(2)

        @pl.when(kk == 0)
        def _():
            acc_ref[...] = jnp.zeros_like(acc_ref)

        acc_ref[...] += lax.dot_general(a_ref[...].astype(MXU_DTYPE), b_ref[...].astype(MXU_DTYPE), dims,
                                        preferred_element_type=F32)

        @pl.when(kk == nk - 1)
        def _():
            o_ref[...] = acc_ref[...].astype(o_ref.dtype)

    return pl.pallas_call(
        body, name=name, grid=(m // tm, n // tn, nk),
        in_specs=[a_spec, b_spec],
        out_specs=pl.BlockSpec((tm, tn), lambda i, j, kk: (i, j)),
        out_shape=jax.ShapeDtypeStruct((m, n), out_dtype),
        scratch_shapes=[pltpu.VMEM((tm, tn), F32)],
        compiler_params=_cparams(("arbitrary", "arbitrary", "arbitrary")),
    )(a, b)


def _prev(u, first):
    return jnp.where(first, 0.0, pltpu.roll(u, 1, 0))


def _next(u, last):
    return jnp.where(last, 0.0, pltpu.roll(u, u.shape[0] - 1, 0))


def _edge_masks(t, w):
    row = lax.broadcasted_iota(jnp.int32, (t, w), 0)
    return row == 0, row == t - 1


SHIFT_CW = 256


def shift_fwd(p, mu_prev, mu_next, nb, t):
    cw, c0 = SHIFT_CW, C_RW // SHIFT_CW

    def body(p_ref, mp_ref, mn_ref, s_ref):
        x = p_ref[...]
        first, last = _edge_masks(t, cw)
        s_ref[...] = x + mp_ref[...] * (_prev(x, first) - x) + mn_ref[...] * (_next(x, last) - x)

    return pl.pallas_call(
        body, name="rwkv_shift_fwd", grid=(nb, RW_PW // cw),
        in_specs=[pl.BlockSpec((t, cw), lambda b, j: (b, c0 + j)), pl.BlockSpec((1, cw), lambda b, j: (0, j)),
                  pl.BlockSpec((1, cw), lambda b, j: (0, j))],
        out_specs=pl.BlockSpec((t, cw), lambda b, j: (b, j)),
        out_shape=jax.ShapeDtypeStruct((nb * t, RW_PW), F32),
        compiler_params=_cparams(("arbitrary", "arbitrary")),
    )(p, mu_prev, mu_next)


def shift_bwd(p, ds, mu_prev, mu_next, nb, t):
    cw, c0 = SHIFT_CW, C_RW // SHIFT_CW

    def body(p_ref, ds_ref, mp_ref, mn_ref, dp_ref, dmp_ref, dmn_ref):
        @pl.when(pl.program_id(1) == 0)
        def _():
            dmp_ref[...] = jnp.zeros_like(dmp_ref)
            dmn_ref[...] = jnp.zeros_like(dmn_ref)

        x, g = p_ref[...], ds_ref[...]
        mp, mn = mp_ref[...], mn_ref[...]
        first, last = _edge_masks(t, cw)
        dp_ref[...] = g * (1.0 - mp - mn) + _next(mp * g, last) + _prev(mn * g, first)
        dmp_ref[...] += jnp.sum(g * (_prev(x, first) - x), axis=0, keepdims=True)
        dmn_ref[...] += jnp.sum(g * (_next(x, last) - x), axis=0, keepdims=True)

    return pl.pallas_call(
        body, name="rwkv_shift_bwd", grid=(RW_PW // cw, nb),
        in_specs=[pl.BlockSpec((t, cw), lambda j, b: (b, c0 + j)), pl.BlockSpec((t, cw), lambda j, b: (b, j)),
                  pl.BlockSpec((1, cw), lambda j, b: (0, j)), pl.BlockSpec((1, cw), lambda j, b: (0, j))],
        out_specs=[pl.BlockSpec((t, cw), lambda j, b: (b, j)), pl.BlockSpec((1, cw), lambda j, b: (0, j)),
                   pl.BlockSpec((1, cw), lambda j, b: (0, j))],
        out_shape=[jax.ShapeDtypeStruct((nb * t, RW_PW), F32), jax.ShapeDtypeStruct((1, RW_PW), F32),
                   jax.ShapeDtypeStruct((1, RW_PW), F32)],
        compiler_params=_cparams(("arbitrary", "arbitrary")),
    )(p, ds, mu_prev, mu_next)


def conv_glu_fwd(u, cw, cb, nb, t):
    def body(u_ref, w_ref, b_ref, z_ref):
        x, w = u_ref[...], w_ref[...]
        first, last = _edge_masks(t, 2 * LANE)
        c = w[0:1] * _prev(x, first) + w[1:2] * x + w[2:3] * _next(x, last) + b_ref[...]
        z_ref[...] = (_silu(c[:, :LANE]) * c[:, LANE:]).astype(z_ref.dtype)

    return pl.pallas_call(
        body, name="conv_glu_fwd", grid=(nb, FFP // LANE),
        in_specs=[pl.BlockSpec((t, 2 * LANE), lambda b, j: (b, j)), pl.BlockSpec((3, 2 * LANE), lambda b, j: (0, j)),
                  pl.BlockSpec((1, 2 * LANE), lambda b, j: (0, j))],
        out_specs=pl.BlockSpec((t, LANE), lambda b, j: (b, j)),
        out_shape=jax.ShapeDtypeStruct((nb * t, FFP), MXU_DTYPE),
        compiler_params=_cparams(("arbitrary", "arbitrary")),
    )(u, cw, cb)


def conv_glu_bwd(u, dz, cw, cb, nb, t):
    def body(u_ref, dz_ref, w_ref, b_ref, du_ref, dw_ref, db_ref):
        @pl.when(pl.program_id(1) == 0)
        def _():
            dw_ref[...] = jnp.zeros_like(dw_ref)
            db_ref[...] = jnp.zeros_like(db_ref)

        x, w, g = u_ref[...], w_ref[...], dz_ref[...]
        first, last = _edge_masks(t, 2 * LANE)
        xp, xn = _prev(x, first), _next(x, last)
        c = w[0:1] * xp + w[1:2] * x + w[2:3] * xn + b_ref[...]
        cg, cv = c[:, :LANE], c[:, LANE:]
        sg = _sigmoid(cg)
        dcg = g * cv * (sg * (1.0 + cg * (1.0 - sg)))
        dcv = g * (cg * sg)
        dc = jnp.concatenate([dcg, dcv], axis=1)
        du = w[1:2] * dc + _next(w[0:1] * dc, last) + _prev(w[2:3] * dc, first)
        du_ref[...] = du.astype(du_ref.dtype)
        dw_ref[0:1, :] += jnp.sum(dc * xp, axis=0, keepdims=True)
        dw_ref[1:2, :] += jnp.sum(dc * x, axis=0, keepdims=True)
        dw_ref[2:3, :] += jnp.sum(dc * xn, axis=0, keepdims=True)
        db_ref[...] += jnp.sum(dc, axis=0, keepdims=True)

    return pl.pallas_call(
        body, name="conv_glu_bwd", grid=(FFP // LANE, nb),
        in_specs=[pl.BlockSpec((t, 2 * LANE), lambda j, b: (b, j)), pl.BlockSpec((t, LANE), lambda j, b: (b, j)),
                  pl.BlockSpec((3, 2 * LANE), lambda j, b: (0, j)), pl.BlockSpec((1, 2 * LANE), lambda j, b: (0, j))],
        out_specs=[pl.BlockSpec((t, 2 * LANE), lambda j, b: (b, j)), pl.BlockSpec((3, 2 * LANE), lambda j, b: (0, j)),
                   pl.BlockSpec((1, 2 * LANE), lambda j, b: (0, j))],
        out_shape=[jax.ShapeDtypeStruct((nb * t, 2 * FFP), MXU_DTYPE), jax.ShapeDtypeStruct((3, 2 * FFP), F32),
                   jax.ShapeDtypeStruct((1, 2 * FFP), F32)],
        compiler_params=_cparams(("arbitrary", "arbitrary")),
    )(u, dz, cw, cb)


def _gla_chunk(q, k, v, afab, wa2p, ba, s_in, reverse):
    c = GLA_CHUNK
    ri = lax.broadcasted_iota(jnp.int32, (c, c), 0)
    ci = lax.broadcasted_iota(jnp.int32, (c, c), 1)
    keep = (ci >= ri) if reverse else (ci <= ri)
    i_ref = (c - 1 - c // 2) if reverse else (c // 2)
    pick_ref = (ci == i_ref).astype(F32)
    ones_cc = jnp.ones((c, c), F32)
    lane = lax.broadcasted_iota(jnp.int32, (1, LANE), 1)
    outs, states = [None] * GLA_H, [None] * GLA_H
    for pr in range(GLA_H // 2):
        la = -_softplus(-(mm(afab, wa2p[pr]) + ba[pr])) * (1.0 / GLA_LOGIT_NORM)
        b = mm_exact(keep.astype(F32), la)
        b_ref = mm_exact(pick_ref, b)
        b_last = mm_exact(ones_cc, la)
        qs = q[pr] * (GLA_DK ** -0.5)
        qi = qs * jnp.exp(b - b_ref)
        ki = k[pr] * jnp.exp(b_ref - b)
        kd = k[pr] * jnp.exp(b_last - b)
        qb = qs * jnp.exp(b)
        dec = jnp.exp(mm_tn_exact(la, jnp.ones((c, LANE), F32)))
        for h in (2 * pr, 2 * pr + 1):
            m = ((lane // GLA_DK) == (h % 2)).astype(F32)
            a = jnp.where(keep, mm_nt(qi * m, ki), 0.0)
            o_intra = mm(a, v[h])
            kv = mm_tn(kd * m, v[h])
            o_inter = mm(qb * m, s_in[h])
            outs[h] = o_intra + o_inter
            states[h] = s_in[h] * dec + kv
    return outs, states


def _gla_load(q_ref, k_ref, v_ref, w_ref, ba_ref, rows):
    q = [q_ref[rows, pr * LANE:(pr + 1) * LANE] for pr in range(GLA_H // 2)]
    k = [k_ref[rows, pr * LANE:(pr + 1) * LANE] for pr in range(GLA_H // 2)]
    v = [v_ref[rows, h * GLA_DV:(h + 1) * GLA_DV] for h in range(GLA_H)]
    w = [w_ref[:, pr * LANE:(pr + 1) * LANE] for pr in range(GLA_H // 2)]
    ba = [ba_ref[:, pr * LANE:(pr + 1) * LANE] for pr in range(GLA_H // 2)]
    return q, k, v, w, ba


GLA_TILE = 512


def _gla_specs(nb, t, tile, reverse):
    nt = t // tile
    rb = (lambda b, j: b * nt + (nt - 1 - j)) if reverse else (lambda b, j: b * nt + j)
    return nt, rb


def gla_fwd(p, wa2p, ba, o_add, nb, t, reverse):
    tile = min(GLA_TILE, t)
    cpt = tile // GLA_CHUNK
    nt, rb = _gla_specs(nb, t, tile, reverse)
    has_add = o_add is not None

    def body(*refs):
        if has_add:
            q_ref, k_ref, v_ref, af_ref, w_ref, ba_ref, add_ref, o_ref, hist_ref, s_ref = refs
        else:
            q_ref, k_ref, v_ref, af_ref, w_ref, ba_ref, o_ref, hist_ref, s_ref = refs

        @pl.when(pl.program_id(1) == 0)
        def _():
            s_ref[...] = jnp.zeros_like(s_ref)

        def step(i, carry):
            ci = (cpt - 1 - i) if reverse else i
            rows = pl.ds(pl.multiple_of(ci * GLA_CHUNK, GLA_CHUNK), GLA_CHUNK)
            s_in = [s_ref[h] for h in range(GLA_H)]
            for h in range(GLA_H):
                hist_ref[0, ci, h] = s_in[h]
            q, k, v, w, ba = _gla_load(q_ref, k_ref, v_ref, w_ref, ba_ref, rows)
            outs, states = _gla_chunk(q, k, v, af_ref[rows, :], w, ba, s_in, reverse)
            for h in range(GLA_H):
                oh = outs[h]
                if has_add:
                    oh = oh + add_ref[rows, h * GLA_DV:(h + 1) * GLA_DV]
                o_ref[rows, h * GLA_DV:(h + 1) * GLA_DV] = oh
                s_ref[h] = states[h]
            return carry

        lax.fori_loop(0, cpt, step, 0)

    hist_map = (lambda b, j: (b, nt - 1 - j, 0, 0, 0)) if reverse else (lambda b, j: (b, j, 0, 0, 0))
    in_specs = [pl.BlockSpec((tile, 256), lambda b, j: (rb(b, j), C_Q // 256)),
                pl.BlockSpec((tile, 256), lambda b, j: (rb(b, j), C_K // 256)),
                pl.BlockSpec((tile, 512), lambda b, j: (rb(b, j), C_V // 512)),
                pl.BlockSpec((tile, LANE), lambda b, j: (rb(b, j), C_AFAB // LANE)),
                pl.BlockSpec((LANE, 256), lambda b, j: (0, 0)), pl.BlockSpec((1, 256), lambda b, j: (0, 0))]
    args = [p, p, p, p, wa2p, ba]
    if has_add:
        in_specs.append(pl.BlockSpec((tile, 512), lambda b, j: (rb(b, j), 0)))
        args.append(o_add)
    return pl.pallas_call(
        body, name="gla_fwd_rev" if reverse else "gla_fwd", grid=(nb, nt),
        in_specs=in_specs,
        out_specs=[pl.BlockSpec((tile, 512), lambda b, j: (rb(b, j), 0)),
                   pl.BlockSpec((1, cpt, GLA_H, LANE, LANE), hist_map)],
        out_shape=[jax.ShapeDtypeStruct((nb * t, 512), F32),
                   jax.ShapeDtypeStruct((nb, t // GLA_CHUNK, GLA_H, LANE, LANE), F32)],
        scratch_shapes=[pltpu.VMEM((GLA_H, LANE, LANE), F32)],
        compiler_params=_cparams(("arbitrary", "arbitrary")),
    )(*args)


def gla_bwd(p, wa2p, ba, hist, do, dprev, nb, t, reverse):
    tile = min(GLA_TILE, t)
    cpt = tile // GLA_CHUNK
    nt, rb_f = _gla_specs(nb, t, tile, reverse)
    rb = lambda b, j: rb_f(b, nt - 1 - j)
    has_prev = dprev is not None

    def body(*refs):
        if has_prev:
            q_ref, k_ref, v_ref, af_ref, w_ref, ba_ref, hist_ref, do_ref, prev_ref, dqkv_ref, dw_ref, dba_ref, ds_ref = refs
        else:
            q_ref, k_ref, v_ref, af_ref, w_ref, ba_ref, hist_ref, do_ref, dqkv_ref, dw_ref, dba_ref, ds_ref = refs

        @pl.when((pl.program_id(0) == 0) & (pl.program_id(1) == 0))
        def _():
            dw_ref[...] = jnp.zeros_like(dw_ref)
            dba_ref[...] = jnp.zeros_like(dba_ref)

        @pl.when(pl.program_id(1) == 0)
        def _():
            ds_ref[...] = jnp.zeros_like(ds_ref)

        def step(i, carry):
            ci = i if reverse else (cpt - 1 - i)
            rows = pl.ds(pl.multiple_of(ci * GLA_CHUNK, GLA_CHUNK), GLA_CHUNK)
            s_in = [hist_ref[0, ci, h] for h in range(GLA_H)]
            fn = functools.partial(_gla_chunk, reverse=reverse)
            q, k, v, w, ba = _gla_load(q_ref, k_ref, v_ref, w_ref, ba_ref, rows)
            _, vjp = jax.vjp(fn, q, k, v, af_ref[rows, :], w, ba, s_in)
            d_o = [do_ref[rows, h * GLA_DV:(h + 1) * GLA_DV] for h in range(GLA_H)]
            d_s = [ds_ref[h] for h in range(GLA_H)]
            dq, dk, dv, daf, dw, dba, ds_in = vjp((d_o, d_s))
            pieces = [(pr * LANE, dq[pr]) for pr in range(2)] + [(256 + pr * LANE, dk[pr]) for pr in range(2)]
            pieces += [(512 + h * GLA_DV, dv[h]) for h in range(GLA_H)] + [(1024, daf)]
            for c0, val in pieces:
                if has_prev:
                    val = val + prev_ref[rows, c0:c0 + LANE]
                dqkv_ref[rows, c0:c0 + LANE] = val
            for pr in range(2):
                dw_ref[:, pr * LANE:(pr + 1) * LANE] += dw[pr]
                dba_ref[:, pr * LANE:(pr + 1) * LANE] += dba[pr]
            for h in range(GLA_H):
                ds_ref[h] = ds_in[h]
            return carry

        lax.fori_loop(0, cpt, step, 0)

    hist_map_f = (lambda b, j: (b, nt - 1 - j, 0, 0, 0)) if reverse else (lambda b, j: (b, j, 0, 0, 0))
    hist_map = lambda b, j: hist_map_f(b, nt - 1 - j)
    in_specs = [pl.BlockSpec((tile, 256), lambda b, j: (rb(b, j), C_Q // 256)),
                pl.BlockSpec((tile, 256), lambda b, j: (rb(b, j), C_K // 256)),
                pl.BlockSpec((tile, 512), lambda b, j: (rb(b, j), C_V // 512)),
                pl.BlockSpec((tile, LANE), lambda b, j: (rb(b, j), C_AFAB // LANE)),
                pl.BlockSpec((LANE, 256), lambda b, j: (0, 0)), pl.BlockSpec((1, 256), lambda b, j: (0, 0)),
                pl.BlockSpec((1, cpt, GLA_H, LANE, LANE), hist_map),
                pl.BlockSpec((tile, 512), lambda b, j: (rb(b, j), 0))]
    args = [p, p, p, p, wa2p, ba, hist, do]
    if has_prev:
        in_specs.append(pl.BlockSpec((tile, 1152), lambda b, j: (rb(b, j), 0)))
        args.append(dprev)
    return pl.pallas_call(
        body, name="gla_bwd_rev" if reverse else "gla_bwd", grid=(nb, nt),
        in_specs=in_specs,
        out_specs=[pl.BlockSpec((tile, 1152), lambda b, j: (rb(b, j), 0)),
                   pl.BlockSpec((LANE, 256), lambda b, j: (0, 0)), pl.BlockSpec((1, 256), lambda b, j: (0, 0))],
        out_shape=[jax.ShapeDtypeStruct((nb * t, 1152), F32), jax.ShapeDtypeStruct((LANE, 256), F32),
                   jax.ShapeDtypeStruct((1, 256), F32)],
        scratch_shapes=[pltpu.VMEM((GLA_H, LANE, LANE), F32)],
        compiler_params=_cparams(("arbitrary", "arbitrary")),
    )(*args)


SCAN_TB = 8
RW_VH = RW_N // 2


def rwkv_scan_fwd(r, w, k, a, b, v):
    t = r.shape[0]

    def body(r_ref, w_ref, k_ref, a_ref, b_ref, v_ref, y_ref, hist_ref, sa_ref, s_ref):
        @pl.when(pl.program_id(0) == 0)
        def _():
            s_ref[...] = jnp.zeros_like(s_ref)

        def step(tt, carry):
            rt, wt, kt, at, bt = r_ref[tt], w_ref[tt], k_ref[tt], a_ref[tt], b_ref[tt]
            for vi in range(RW_VH):
                sv = s_ref[vi]
                hist_ref[tt, vi] = sv
                sa = jnp.sum(sv * at, axis=0, keepdims=True)
                sn = sv * wt + sa * bt + v_ref[tt, vi:vi + 1, :] * kt
                s_ref[vi] = sn
                y_ref[tt, vi:vi + 1, :] = jnp.sum(sn * rt, axis=0, keepdims=True)
                sa_ref[tt, vi:vi + 1, :] = sa
            return carry

        lax.fori_loop(0, SCAN_TB, step, 0)

    kspec = pl.BlockSpec((SCAN_TB, RW_N, LANE), lambda i: (i, 0, 0))
    vspec = pl.BlockSpec((SCAN_TB, RW_VH, LANE), lambda i: (i, 0, 0))
    return pl.pallas_call(
        body, name="rwkv_scan_fwd", grid=(t // SCAN_TB,),
        in_specs=[kspec] * 5 + [vspec],
        out_specs=[vspec, pl.BlockSpec((SCAN_TB, RW_VH, RW_N, LANE), lambda i: (i, 0, 0, 0)), vspec],
        out_shape=[jax.ShapeDtypeStruct((t, RW_VH, LANE), F32), jax.ShapeDtypeStruct((t, RW_VH, RW_N, LANE), F32),
                   jax.ShapeDtypeStruct((t, RW_VH, LANE), F32)],
        scratch_shapes=[pltpu.VMEM((RW_VH, RW_N, LANE), F32)],
        compiler_params=_cparams(("arbitrary",)),
    )(r, w, k, a, b, v)


def rwkv_scan_bwd(r, w, k, a, b, v, hist, sa, dy):
    t = r.shape[0]
    nt = t // SCAN_TB

    def body(r_ref, w_ref, k_ref, a_ref, b_ref, v_ref, hist_ref, sa_ref, dy_ref,
             dr_ref, dw_ref, dk_ref, da_ref, db_ref, dv_ref, ds_ref):
        @pl.when(pl.program_id(0) == 0)
        def _():
            ds_ref[...] = jnp.zeros_like(ds_ref)

        def step(i, carry):
            tt = SCAN_TB - 1 - i
            rt, wt, kt, at, bt = r_ref[tt], w_ref[tt], k_ref[tt], a_ref[tt], b_ref[tt]
            zero = jnp.zeros((RW_N, LANE), F32)
            dr, dw, dk, da, db = zero, zero, zero, zero, zero
            for vi in range(RW_VH):
                sv = hist_ref[tt, vi]
                sa_row = sa_ref[tt, vi:vi + 1, :]
                v_row = v_ref[tt, vi:vi + 1, :]
                dy_row = dy_ref[tt, vi:vi + 1, :]
                sn = sv * wt + sa_row * bt + v_row * kt
                dsv = ds_ref[vi] + dy_row * rt
                dr = dr + sn * dy_row
                dsa = jnp.sum(dsv * bt, axis=0, keepdims=True)
                dw = dw + sv * dsv
                db = db + dsv * sa_row
                dk = dk + dsv * v_row
                dv_ref[tt, vi:vi + 1, :] = jnp.sum(dsv * kt, axis=0, keepdims=True)
                da = da + sv * dsa
                ds_ref[vi] = dsv * wt + dsa * at
            for ref, val in ((dr_ref, dr), (dw_ref, dw), (dk_ref, dk), (da_ref, da), (db_ref, db)):
                ref[tt] = val + pltpu.roll(val, LANE // 2, 1)
            return carry

        lax.fori_loop(0, SCAN_TB, step, 0)

    kspec = pl.BlockSpec((SCAN_TB, RW_N, LANE), lambda i: (nt - 1 - i, 0, 0))
    vspec = pl.BlockSpec((SCAN_TB, RW_VH, LANE), lambda i: (nt - 1 - i, 0, 0))
    kshape = jax.ShapeDtypeStruct((t, RW_N, LANE), F32)
    return pl.pallas_call(
        body, name="rwkv_scan_bwd", grid=(nt,),
        in_specs=[kspec] * 5 + [vspec, pl.BlockSpec((SCAN_TB, RW_VH, RW_N, LANE), lambda i: (nt - 1 - i, 0, 0, 0)), vspec, vspec],
        out_specs=[kspec] * 5 + [vspec],
        out_shape=[kshape] * 5 + [jax.ShapeDtypeStruct((t, RW_VH, LANE), F32)],
        scratch_shapes=[pltpu.VMEM((RW_VH, RW_N, LANE), F32)],
        compiler_params=_cparams(("arbitrary",)),
    )(r, w, k, a, b, v, hist, sa, dy)


def to_scan_k(xf, xb, nb, t):
    def one(x):
        return x.reshape(nb, t, RW_H, RW_N).transpose(1, 3, 0, 2).reshape(t, RW_N, nb * RW_H)
    f, b = one(xf), jnp.flip(one(xb), axis=0)
    return jnp.concatenate([f, b, f, b], axis=-1)


def to_scan_v(x, nb, t):
    y = x.reshape(nb, t, RW_H, 2, RW_VH).transpose(1, 4, 3, 0, 2).reshape(t, RW_VH, 2, nb * RW_H)
    return jnp.stack([y, jnp.flip(y, axis=0)], axis=3).reshape(t, RW_VH, LANE)


def from_scan_k(x, nb, t):
    def one(y):
        return y.reshape(t, RW_N, nb, RW_H).transpose(2, 0, 3, 1).reshape(nb * t, RW_W)
    q = nb * RW_H
    return one(x[:, :, 0:q]), one(jnp.flip(x[:, :, q:2 * q], axis=0))


def from_scan_v(x, nb, t):
    y = x.reshape(t, RW_VH, 2, 2, nb, RW_H)
    def one(z):
        return z.transpose(3, 0, 4, 2, 1).reshape(nb * t, RW_W)
    return one(y[:, :, :, 0]), one(jnp.flip(y[:, :, :, 1], axis=0))


def f_norm(rows, params):
    (x,), (g,) = rows, params
    return [_rmsnorm(x, g)]


def f_rwkv_pre(rows, params):
    k, wlal, gl = rows
    w0f, w2f, w0b, w2b, a0, a2, g2, k_k, k_a = params
    seg = _segment_ones(RW_W, RW_N)
    tw = jnp.tanh(wlal)

    def decay(w0, w2):
        return jnp.exp(-jnp.exp(-_softplus(-(w0 + mm(tw, w2))) - 0.5))

    lr = _sigmoid(a0 + mm(wlal, a2))
    gate = mm(_sigmoid(gl), g2)
    kk = k * k_k
    kk = kk / jnp.maximum(jnp.sqrt(mm_exact(kk * kk, seg)), 1e-12)
    kp = k * (1.0 + (lr - 1.0) * k_a)
    return [decay(w0f, w2f), decay(w0b, w2b), kp, -kk, kk * lr, gate]


def f_branch_post(rows, params):
    o, og, y, r, kp, v, g = rows
    gla_g, ln_w, ln_b, r_k = params
    seg_gla = _segment_ones(GLA_H * GLA_DV, GLA_DV)
    seg_rw = _segment_ones(RW_W, RW_N)
    on = o * lax.rsqrt(mm_exact(o * o, seg_gla) * (1.0 / GLA_DV) + HEAD_NORM_EPS)
    oa = on * gla_g * _silu(og)
    mu = mm_exact(y, seg_rw) * (1.0 / RW_N)
    yc = y - mu
    var = mm_exact(yc * yc, seg_rw) * (1.0 / RW_N)
    yn = yc * lax.rsqrt(var + RW_GN_EPS) * ln_w + ln_b
    bonus = mm_exact(r * kp * r_k, seg_rw) * v
    return [oa, (yn + bonus) * g]


def f_merge(rows, params):
    ga, gb, ya, yb = rows
    return [_sigmoid(ga) * ya + _sigmoid(gb) * yb]


def f_norm2(rows, params):
    (x, mo), (g,) = rows, params
    x1 = x + mo
    return [x1, _rmsnorm(x1, g)]


def loss_head(x1, ffo, tgt, gf, tm):
    n = x1.shape[0]

    def body(x1_ref, f_ref, t_ref, g_ref, loss_ref, dx_ref, dg_ref):
        @pl.when(pl.program_id(0) == 0)
        def _():
            loss_ref[...] = jnp.zeros_like(loss_ref)
            dg_ref[...] = jnp.zeros_like(dg_ref)

        tgt_v = t_ref[...]

        def f(x2, g):
            err = _rmsnorm(x2, g) - tgt_v
            return jnp.sum(jnp.sum(err * err, axis=-1, keepdims=True), axis=0, keepdims=True) * (0.5 / D)

        val, vjp = jax.vjp(f, x1_ref[...] + f_ref[...], g_ref[...])
        dx, dg = vjp(jnp.ones((1, 1), F32))
        loss_ref[...] += val
        dx_ref[...] = dx
        dg_ref[...] += dg

    return pl.pallas_call(
        body, name="loss_head", grid=(n // tm,),
        in_specs=[_row_spec(tm, D, 0)] * 3 + [_full_spec((1, D))],
        out_specs=[_full_spec((1, 1)), _row_spec(tm, D, 0), _full_spec((1, D))],
        out_shape=[jax.ShapeDtypeStruct((1, 1), F32), jax.ShapeDtypeStruct((n, D), F32), jax.ShapeDtypeStruct((1, D), F32)],
        compiler_params=_cparams(("arbitrary",)),
    )(x1, ffo, tgt, gf)


def _pad_cols(a, width):
    return jnp.pad(a, ((0, 0), (0, width - a.shape[1])))


def w_in_to_padded(w):
    return _pad_cols(jnp.concatenate([w[:, 3360:5408], w[:, 0:1536], w[:, 1568:3360], w[:, 1536:1568]], axis=1), NP)


def w_in_from_padded(wp):
    return jnp.concatenate([wp[:, 2048:3584], wp[:, 5376:5408], wp[:, 3584:5376], wp[:, 0:2048]], axis=1)


def ff_interleave(a):
    r = a.shape[0]
    halves = jnp.stack([_pad_cols(a[:, :D_FF], FFP), _pad_cols(a[:, D_FF:], FFP)], axis=1)
    return halves.reshape(r, 2, FFP // LANE, LANE).transpose(0, 2, 1, 3).reshape(r, 2 * FFP)


def ff_deinterleave(a):
    r = a.shape[0]
    halves = a.reshape(r, FFP // LANE, 2, LANE).transpose(0, 2, 1, 3).reshape(r, 2, FFP)
    return halves[:, :, :D_FF].reshape(r, 2 * D_FF)


def _rows_into(w, rows, off):
    return jnp.zeros((rows, w.shape[1]), w.dtype).at[off:off + w.shape[0]].set(w)


def local_step(x, tgt, w, nb, t):
    n = nb * t
    tm = min(n, 1024)
    tr = min(n, 256)
    vec = lambda a: a.reshape(1, -1)

    w_in_p = w_in_to_padded(w["w_in"])
    wa2_f, wa2_b = _rows_into(w["gla_wa2_f"], LANE, 0), _rows_into(w["gla_wa2_b"], LANE, GLA_RANK)
    w2f, w2b = _rows_into(w["rwkv_w2_f"], LANE, 0), _rows_into(w["rwkv_w2_b"], LANE, 0)
    a2 = _rows_into(w["rwkv_a2"], LANE, 64)
    ffn_up_p = ff_interleave(w["ffn_up"])
    conv_w_p, conv_b_p = ff_interleave(w["ffn_conv_w"]), ff_interleave(vec(w["ffn_conv_b"]))
    ffn_down_p = jnp.pad(w["ffn_down"], ((0, FFP - D_FF), (0, 0)))
    g1, g2n, gf = vec(w["norm1_g"]), vec(w["norm2_g"]), vec(w["norm_f_g"])
    mu_prev, mu_next = vec(w["rwkv_mu_prev"]), vec(w["rwkv_mu_next"])
    pre_params = [vec(w["rwkv_w0_f"]), w2f, vec(w["rwkv_w0_b"]), w2b, vec(w["rwkv_a0"]), a2, w["rwkv_g2"],
                  vec(w["rwkv_k_k"]), vec(w["rwkv_k_a"])]
    post_params = [vec(w["gla_norm_g"]), vec(w["rwkv_ln_w"]), vec(w["rwkv_ln_b"]), vec(w["rwkv_r_k"])]
    ba_f, ba_b = vec(w["gla_ba_f"]), vec(w["gla_ba_b"])

    (h1,) = rowwise_fwd("norm1_fwd", f_norm, [(x, D, 0)], [g1], [(D, MXU_DTYPE)], tr)
    p = matmul("proj_in", h1, w_in_p, "nn", F32, tm, 512, D)
    s = shift_fwd(p, mu_prev, mu_next, nb, t)
    pre_rows = [(s, 512, 1), (s, LANE, 1536 // LANE), (s, LANE, 1664 // LANE)]
    wf, wb, kp, a_s, b_s, g = rowwise_fwd("rwkv_pre_fwd", f_rwkv_pre, pre_rows, pre_params, [(RW_W, F32)] * 6, tr)
    s_r, s_v = s[:, 0:512], s[:, 1024:1536]
    sc = [to_scan_k(s_r, s_r, nb, t), to_scan_k(wf, wb, nb, t), to_scan_k(kp, kp, nb, t), to_scan_k(a_s, a_s, nb, t),
          to_scan_k(b_s, b_s, nb, t), to_scan_v(s_v, nb, t)]
    y_sc, hist_rw, sa_sc = rwkv_scan_fwd(*sc)
    y_f, y_b = from_scan_v(y_sc, nb, t)
    y = y_f + y_b
    o_f, hist_f = gla_fwd(p, wa2_f, ba_f, None, nb, t, False)
    o, hist_b = gla_fwd(p, wa2_b, ba_b, o_f, nb, t, True)
    post_rows = [(o, 512, 0), (p, 512, C_OG // 512), (y, 512, 0), (s, 512, 0), (kp, 512, 0), (s, 512, 2), (g, 512, 0)]
    oa, ob = rowwise_fwd("branch_post_fwd", f_branch_post, post_rows, post_params, [(512, MXU_DTYPE)] * 2, tr)
    ya = matmul("gla_proj", oa, w["gla_proj"], "nn", F32, tm, 512, 512)
    yb = matmul("rwkv_proj", ob, w["rwkv_proj"], "nn", F32, tm, 512, 512)
    merge_rows = [(p, D, 0), (p, D, 1), (ya, D, 0), (yb, D, 0)]
    (merged,) = rowwise_fwd("merge_fwd", f_merge, merge_rows, [], [(D, MXU_DTYPE)], tr)
    mo = matmul("w_out", merged, w["w_out"], "nn", F32, tm, 512, D)
    x1, h2 = rowwise_fwd("norm2_fwd", f_norm2, [(x, D, 0), (mo, D, 0)], [g2n], [(D, F32), (D, MXU_DTYPE)], tr)
    u = matmul("ffn_up", h2, ffn_up_p, "nn", F32, tm, 512, D)
    z = conv_glu_fwd(u, conv_w_p, conv_b_p, nb, t)
    ffo = matmul("ffn_down", z, ffn_down_p, "nn", F32, tm, 512, FFP // 2)
    loss, dx2, dgf = loss_head(x1, ffo, tgt, gf, tr)

    dz = matmul("ffn_down_dx", dx2, ffn_down_p, "nt", F32, tm, FFP // 2, D)
    d_ffn_down_p = matmul("ffn_down_dw", z, dx2, "tn", F32, FFP // 2, 512, tm)
    du, d_conv_w_p, d_conv_b_p = conv_glu_bwd(u, dz, conv_w_p, conv_b_p, nb, t)
    dh2 = matmul("ffn_up_dx", du, ffn_up_p, "nt", F32, tm, 512, 512)
    d_ffn_up_p = matmul("ffn_up_dw", h2, du, "tn", F32, 512, 512, tm)
    (dx1,), (dg2,) = rowwise_bwd("norm2_bwd", f_norm2, [(x, D, 0), (mo, D, 0)], [g2n],
                                 [[(dx2, D, 0)], [(dh2, D, 0)]], tr, grad_rows=[1])
    dmerged = matmul("w_out_dx", dx1, w["w_out"], "nt", F32, tm, 512, D)
    d_w_out = matmul("w_out_dw", merged, dx1, "tn", F32, 512, 512, tm)
    (dga, dgb, dya, dyb), _ = rowwise_bwd("merge_bwd", f_merge, merge_rows, [], [[(dmerged, D, 0)]], tr)
    d_oa = matmul("gla_proj_dx", dya, w["gla_proj"], "nt", F32, tm, 512, D)
    d_gla_proj = matmul("gla_proj_dw", oa, dya, "tn", F32, 512, 512, tm)
    d_ob = matmul("rwkv_proj_dx", dyb, w["rwkv_proj"], "nt", F32, tm, 512, D)
    d_rwkv_proj = matmul("rwkv_proj_dw", ob, dyb, "tn", F32, 512, 512, tm)
    (d_o, d_og, d_y, d_r_post, d_kp_post, d_v_post, d_g), d_post = rowwise_bwd(
        "branch_post_bwd", f_branch_post, post_rows, post_params, [[(d_oa, 512, 0)], [(d_ob, 512, 0)]], tr)
    dsc = rwkv_scan_bwd(*sc, hist_rw, sa_sc, to_scan_v(d_y, nb, t))
    d_r_f, d_r_b = from_scan_k(dsc[0], nb, t)
    d_wf, d_wb = from_scan_k(dsc[1], nb, t)
    d_kp_f, d_kp_b = from_scan_k(dsc[2], nb, t)
    d_a_f, d_a_b = from_scan_k(dsc[3], nb, t)
    d_b_f, d_b_b = from_scan_k(dsc[4], nb, t)
    d_v_f, d_v_b = from_scan_v(dsc[5], nb, t)
    (d_k, d_wlal, d_gl), d_pre = rowwise_bwd(
        "rwkv_pre_bwd", f_rwkv_pre, pre_rows, pre_params,
        [[(d_wf, 512, 0)], [(d_wb, 512, 0)], [(d_kp_f, 512, 0), (d_kp_b, 512, 0), (d_kp_post, 512, 0)],
         [(d_a_f, 512, 0), (d_a_b, 512, 0)], [(d_b_f, 512, 0), (d_b_b, 512, 0)], [(d_g, 512, 0)]], tr)
    ds = jnp.concatenate([d_r_f + d_r_b + d_r_post, d_k, d_v_f + d_v_b + d_v_post, d_wlal, d_gl], axis=1)
    dp_rw, d_mu_prev, d_mu_next = shift_bwd(p, ds, mu_prev, mu_next, nb, t)
    dqkv_f, d_wa2_f, d_ba_f = gla_bwd(p, wa2_f, ba_f, hist_f, d_o, None, nb, t, False)
    dqkv, d_wa2_b, d_ba_b = gla_bwd(p, wa2_b, ba_b, hist_b, d_o, dqkv_f, nb, t, True)
    dp = jnp.concatenate([dga, dgb, dqkv[:, 0:1024], d_og, dp_rw, dqkv[:, 1024:1152],
                          jnp.zeros((n, NP - C_AFAB - LANE), F32)], axis=1)
    dh1 = matmul("proj_in_dx", dp, w_in_p, "nt", F32, tm, 512, 512)
    d_w_in_p = matmul("proj_in_dw", h1, dp, "tn", F32, 512, 512, tm)
    (grad_x,), (dg1,) = rowwise_bwd("norm1_bwd", f_norm, [(x, D, 0)], [g1], [[(dh1, D, 0)]], tr, adds=[(0, (dx1, D, 0))])

    grads = {
        "norm1_g": dg1, "w_in": w_in_from_padded(d_w_in_p),
        "gla_wa2_f": d_wa2_f[0:GLA_RANK], "gla_ba_f": d_ba_f, "gla_wa2_b": d_wa2_b[GLA_RANK:2 * GLA_RANK], "gla_ba_b": d_ba_b,
        "gla_norm_g": d_post[0], "gla_proj": d_gla_proj, "rwkv_mu_prev": d_mu_prev, "rwkv_mu_next": d_mu_next,
        "rwkv_w0_f": d_pre[0], "rwkv_w2_f": d_pre[1][0:64], "rwkv_w0_b": d_pre[2], "rwkv_w2_b": d_pre[3][0:64],
        "rwkv_a0": d_pre[4], "rwkv_a2": d_pre[5][64:128], "rwkv_g2": d_pre[6], "rwkv_k_k": d_pre[7], "rwkv_k_a": d_pre[8],
        "rwkv_r_k": d_post[3], "rwkv_ln_w": d_post[1], "rwkv_ln_b": d_post[2], "rwkv_proj": d_rwkv_proj, "w_out": d_w_out,
        "norm2_g": dg2, "ffn_up": ff_deinterleave(d_ffn_up_p), "ffn_conv_w": ff_deinterleave(d_conv_w_p),
        "ffn_conv_b": ff_deinterleave(d_conv_b_p), "ffn_down": d_ffn_down_p[0:D_FF], "norm_f_g": dgf,
    }
    return loss, grad_x, grads


MESH = pl.DeviceIdType.MESH


def remote_exchange(name, items):
    n = len(items)

    def body(*refs):
        in_refs, out_refs = refs[:n], refs[n:2 * n]
        send_sems, recv_sems, local_sems = refs[2 * n:]
        x, y, c = lax.axis_index("x"), lax.axis_index("y"), lax.axis_index("c")
        me = 4 * x + 2 * y + c

        def peer(k):
            px = 1 - x if (k >> 2) & 1 else x
            py = 1 - y if (k >> 1) & 1 else y
            pc = 1 - c if k & 1 else c
            return (px, py, pc), 4 * px + 2 * py + pc

        copies = []
        for i, (_, scatter) in enumerate(items):
            src = in_refs[i].at[me] if scatter else in_refs[i]
            own = pltpu.make_async_copy(src, out_refs[i].at[me], local_sems.at[i])
            own.start()
            copies.append(own)
        sends = []
        for k in range(1, N_DEV):
            dev, slot = peer(k)
            for i, (_, scatter) in enumerate(items):
                src = in_refs[i].at[slot] if scatter else in_refs[i]
                cp = pltpu.make_async_remote_copy(src_ref=src, dst_ref=out_refs[i].at[me], send_sem=send_sems.at[i, k - 1],
                                                  recv_sem=recv_sems.at[i, k - 1], device_id=dev, device_id_type=MESH)
                cp.start()
                sends.append((cp, i, k, slot))
        for cp, i, k, slot in sends:
            recv = pltpu.make_async_remote_copy(src_ref=out_refs[i].at[slot], dst_ref=out_refs[i].at[slot],
                                                send_sem=send_sems.at[i, k - 1], recv_sem=recv_sems.at[i, k - 1],
                                                device_id=peer(k)[0], device_id_type=MESH)
            recv.wait_recv()
            cp.wait_send()
        for own in copies:
            own.wait()

    anyspec = pl.BlockSpec(memory_space=pl.ANY)
    out_shape = [jax.ShapeDtypeStruct((N_DEV,) + (a.shape[1:] if sc else a.shape), a.dtype) for a, sc in items]
    return pl.pallas_call(
        body, name=name,
        in_specs=[anyspec] * n, out_specs=[anyspec] * n, out_shape=out_shape,
        scratch_shapes=[pltpu.SemaphoreType.DMA((n, N_DEV - 1)), pltpu.SemaphoreType.DMA((n, N_DEV - 1)),
                        pltpu.SemaphoreType.DMA((n,))],
    )(*[a for a, _ in items])


def _adam_tiles(r, c):
    tc = 256 if (c % 256 == 0 and r * c > 128 * 1024) else c
    tr = 128 if (r % 128 == 0 and r > 128) else r
    return tr, tc


def adamw_reduce(name, parts, w, m, v):
    r, c = w.shape
    tr, tc = _adam_tiles(r, c)

    def body(p_ref, w_ref, m_ref, v_ref, g_ref, d_ref, nm_ref, nv_ref):
        g = p_ref[0]
        for d in range(1, N_DEV):
            g = g + p_ref[d]
        nm = ADAM_B1 * m_ref[...] + (1.0 - ADAM_B1) * g
        nv = ADAM_B2 * v_ref[...] + (1.0 - ADAM_B2) * (g * g)
        m_hat = nm / (1.0 - ADAM_B1 ** ADAM_STEP)
        v_hat = nv / (1.0 - ADAM_B2 ** ADAM_STEP)
        g_ref[...] = g
        d_ref[...] = -ADAM_LR * (m_hat / (jnp.sqrt(v_hat) + ADAM_EPS) + ADAM_WD * w_ref[...])
        nm_ref[...] = nm
        nv_ref[...] = nv

    spec = pl.BlockSpec((tr, tc), lambda i, j: (i, j))
    return pl.pallas_call(
        body, name=name, grid=(r // tr, c // tc),
        in_specs=[pl.BlockSpec((N_DEV, tr, tc), lambda i, j: (0, i, j)), spec, spec, spec],
        out_specs=[spec] * 4, out_shape=[jax.ShapeDtypeStruct((r, c), F32)] * 4,
        compiler_params=_cparams(("arbitrary", "arbitrary")),
    )(parts, w, m, v)


SHARDED = {"w_in": 1, "gla_wa2_f": 1, "gla_wa2_b": 1, "gla_proj": 1, "rwkv_w2_f": 1, "rwkv_w2_b": 1, "rwkv_a2": 1,
           "rwkv_g2": 1, "rwkv_proj": 1, "w_out": 0, "ffn_up": 1, "ffn_conv_w": 1, "ffn_down": 0}
BF16_GATHER = ("w_in", "gla_proj", "rwkv_proj", "w_out", "ffn_up", "ffn_down")
REPLICATED = ("norm1_g", "gla_ba_f", "gla_ba_b", "gla_norm_g", "rwkv_mu_prev", "rwkv_mu_next", "rwkv_w0_f", "rwkv_w0_b",
              "rwkv_a0", "rwkv_k_k", "rwkv_k_a", "rwkv_r_k", "rwkv_ln_w", "rwkv_ln_b", "norm2_g", "ffn_conv_b", "norm_f_g")
WEIGHTS = ("norm1_g", "w_in", "gla_wa2_f", "gla_ba_f", "gla_wa2_b", "gla_ba_b", "gla_norm_g", "gla_proj", "rwkv_mu_prev",
           "rwkv_mu_next", "rwkv_w0_f", "rwkv_w2_f", "rwkv_w0_b", "rwkv_w2_b", "rwkv_a0", "rwkv_a2", "rwkv_g2", "rwkv_k_k",
           "rwkv_k_a", "rwkv_r_k", "rwkv_ln_w", "rwkv_ln_b", "rwkv_proj", "w_out", "norm2_g", "ffn_up", "ffn_conv_w",
           "ffn_conv_b", "ffn_down", "norm_f_g")
PACK_ROWS = 144


def _gathered_to_full(g, axis):
    if axis == 0:
        return g.reshape(N_DEV * g.shape[1], g.shape[2])
    return g.transpose(1, 0, 2).reshape(g.shape[1], N_DEV * g.shape[2])


def _full_to_slices(a, axis):
    if axis == 0:
        return a.reshape(N_DEV, a.shape[0] // N_DEV, a.shape[1])
    return a.reshape(a.shape[0], N_DEV, a.shape[1] // N_DEV).transpose(1, 0, 2)


def _pack(d):
    flat = jnp.concatenate([d[k].reshape(-1).astype(F32) for k in REPLICATED])
    return jnp.pad(flat, (0, PACK_ROWS * LANE - flat.shape[0])).reshape(PACK_ROWS, LANE)


def _unpack(packed, shapes):
    flat, out, pos = packed.reshape(-1), {}, 0
    for k in REPLICATED:
        size = int(np.prod(shapes[k]))
        out[k] = flat[pos:pos + size].reshape(shapes[k])
        pos += size
    return out


def kernel(x, norm1_g, w_in, gla_wa2_f, gla_ba_f, gla_wa2_b, gla_ba_b, gla_norm_g, gla_proj, rwkv_mu_prev, rwkv_mu_next, rwkv_w0_f, rwkv_w2_f, rwkv_w0_b, rwkv_w2_b, rwkv_a0, rwkv_a2, rwkv_g2, rwkv_k_k, rwkv_k_a, rwkv_r_k, rwkv_ln_w, rwkv_ln_b, rwkv_proj, w_out, norm2_g, ffn_up, ffn_conv_w, ffn_conv_b, ffn_down, norm_f_g, loss_target, m_norm1_g, m_w_in, m_gla_wa2_f, m_gla_ba_f, m_gla_wa2_b, m_gla_ba_b, m_gla_norm_g, m_gla_proj, m_rwkv_mu_prev, m_rwkv_mu_next, m_rwkv_w0_f, m_rwkv_w2_f, m_rwkv_w0_b, m_rwkv_w2_b, m_rwkv_a0, m_rwkv_a2, m_rwkv_g2, m_rwkv_k_k, m_rwkv_k_a, m_rwkv_r_k, m_rwkv_ln_w, m_rwkv_ln_b, m_rwkv_proj, m_w_out, m_norm2_g, m_ffn_up, m_ffn_conv_w, m_ffn_conv_b, m_ffn_down, m_norm_f_g, v_norm1_g, v_w_in, v_gla_wa2_f, v_gla_ba_f, v_gla_wa2_b, v_gla_ba_b, v_gla_norm_g, v_gla_proj, v_rwkv_mu_prev, v_rwkv_mu_next, v_rwkv_w0_f, v_rwkv_w2_f, v_rwkv_w0_b, v_rwkv_w2_b, v_rwkv_a0, v_rwkv_a2, v_rwkv_g2, v_rwkv_k_k, v_rwkv_k_a, v_rwkv_r_k, v_rwkv_ln_w, v_rwkv_ln_b, v_rwkv_proj, v_w_out, v_norm2_g, v_ffn_up, v_ffn_conv_w, v_ffn_conv_b, v_ffn_down, v_norm_f_g):
    args = locals()
    wts = {k: args[k] for k in WEIGHTS}
    mom = {k: args["m_" + k] for k in WEIGHTS}
    var = {k: args["v_" + k] for k in WEIGHTS}
    shapes = {k: wts[k].shape for k in WEIGHTS}
    nb, t = x.shape[0], x.shape[1]
    mat = lambda a: a.reshape(a.shape[-2], a.shape[-1])

    names = list(SHARDED)
    blocks = [mat(wts[k]).astype(MXU_DTYPE) if k in BF16_GATHER else mat(wts[k]) for k in names]
    gathered = remote_exchange("gather_weights", [(b, False) for b in blocks])
    full = {k: _gathered_to_full(g, SHARDED[k]) for k, g in zip(names, gathered)}
    for k in REPLICATED:
        full[k] = wts[k].reshape(-1) if k in ("norm_f_g", "rwkv_r_k") else wts[k][0]

    loss, grad_x, grads = local_step(x.reshape(nb * t, D), loss_target.reshape(nb * t, D), full, nb, t)

    items = [(_full_to_slices(grads[k], SHARDED[k]), True) for k in names] + [(_pack(grads), False)]
    received = remote_exchange("exchange_grads", items)

    res = {}
    for k, parts in zip(names, received[:-1]):
        outs = adamw_reduce("adamw_" + k, parts, mat(wts[k]), mat(mom[k]), mat(var[k]))
        res[k] = [o.reshape(shapes[k]) for o in outs]
    packed = adamw_reduce("adamw_replicated", received[-1], _pack(wts), _pack(mom), _pack(var))
    unpacked = [_unpack(p, shapes) for p in packed]
    for k in REPLICATED:
        res[k] = [u[k] for u in unpacked]

    total = lax.psum(loss[0, 0], ("x", "y", "c"))
    out = [total, grad_x.reshape(x.shape)]
    for j in range(4):
        out += [res[k][j] for k in WEIGHTS]
    return tuple(out)
```

```python
import functools

import jax
import jax.numpy as jnp
import numpy as np
from jax import lax
from jax.experimental import pallas as pl
from jax.experimental.pallas import tpu as pltpu

F32 = jnp.float32
MXU_DTYPE = jnp.bfloat16

D = 1024
SEQ = 2048
GLA_H, GLA_DK, GLA_DV, GLA_CHUNK = 4, 64, 128, 64
GLA_RANK = 16
GLA_LOGIT_NORM = 16.0
RW_H, RW_N = 8, 64
RW_W = 512
D_FF = 2752
NORM_EPS = 1e-6
HEAD_NORM_EPS = 1e-5
RW_GN_EPS = RW_N * 1e-5
N_DEV = 8
ADAM_LR, ADAM_B1, ADAM_B2, ADAM_EPS, ADAM_WD, ADAM_STEP = 0.001, 0.9, 0.999, 1e-08, 0.01, 10

C_GA, C_GB, C_Q, C_K, C_V, C_OG = 0, 1024, 2048, 2304, 2560, 3072
C_RW = 3584
C_R, C_RK, C_RV, C_WLAL, C_GL = 3584, 4096, 4608, 5120, 5248
C_AFAB = 5376
NP = 5632
RW_PW = 1792
FFP = 2816
LANE = 128
VMEM_LIMIT = 56 * 1024 * 1024


def _cparams(sem):
    return pltpu.CompilerParams(dimension_semantics=sem, vmem_limit_bytes=VMEM_LIMIT)


@jax.custom_vjp
def mm(a, b):
    return jnp.dot(a.astype(MXU_DTYPE), b.astype(MXU_DTYPE), preferred_element_type=F32)


def _mm_fwd(a, b):
    return mm(a, b), (a, b)


def _mm_bwd(res, g):
    a, b = res
    gb = g.astype(MXU_DTYPE)
    da = lax.dot_general(gb, b.astype(MXU_DTYPE), (((1,), (1,)), ((), ())), preferred_element_type=F32)
    db = lax.dot_general(a.astype(MXU_DTYPE), gb, (((0,), (0,)), ((), ())), preferred_element_type=F32)
    return da.astype(a.dtype), db.astype(b.dtype)


mm.defvjp(_mm_fwd, _mm_bwd)


@jax.custom_vjp
def mm_nt(a, b):
    return lax.dot_general(a.astype(MXU_DTYPE), b.astype(MXU_DTYPE), (((1,), (1,)), ((), ())), preferred_element_type=F32)


def _mm_nt_fwd(a, b):
    return mm_nt(a, b), (a, b)


def _mm_nt_bwd(res, g):
    a, b = res
    gb = g.astype(MXU_DTYPE)
    da = jnp.dot(gb, b.astype(MXU_DTYPE), preferred_element_type=F32)
    db = lax.dot_general(gb, a.astype(MXU_DTYPE), (((0,), (0,)), ((), ())), preferred_element_type=F32)
    return da.astype(a.dtype), db.astype(b.dtype)


mm_nt.defvjp(_mm_nt_fwd, _mm_nt_bwd)


@jax.custom_vjp
def mm_tn(a, b):
    return lax.dot_general(a.astype(MXU_DTYPE), b.astype(MXU_DTYPE), (((0,), (0,)), ((), ())), preferred_element_type=F32)


def _mm_tn_fwd(a, b):
    return mm_tn(a, b), (a, b)


def _mm_tn_bwd(res, g):
    a, b = res
    gb = g.astype(MXU_DTYPE)
    da = lax.dot_general(b.astype(MXU_DTYPE), gb, (((1,), (1,)), ((), ())), preferred_element_type=F32)
    db = jnp.dot(a.astype(MXU_DTYPE), gb, preferred_element_type=F32)
    return da.astype(a.dtype), db.astype(b.dtype)


mm_tn.defvjp(_mm_tn_fwd, _mm_tn_bwd)


def mm_exact(a, b):
    return jnp.dot(a, b, preferred_element_type=F32, precision=lax.Precision.HIGHEST)


def mm_tn_exact(a, b):
    return lax.dot_general(a, b, (((0,), (0,)), ((), ())), preferred_element_type=F32, precision=lax.Precision.HIGHEST)


def _softplus(x):
    return jnp.maximum(x, 0.0) + jnp.log(1.0 + jnp.exp(-jnp.abs(x)))


def _sigmoid(x):
    return 1.0 / (1.0 + jnp.exp(-x))


def _silu(x):
    return x * _sigmoid(x)


def _rmsnorm(x, g):
    return x * lax.rsqrt(jnp.mean(x * x, axis=-1, keepdims=True) + NORM_EPS) * g


def _segment_ones(width, seg):
    i = lax.broadcasted_iota(jnp.int32, (width, width), 0) // seg
    j = lax.broadcasted_iota(jnp.int32, (width, width), 1) // seg
    return (i == j).astype(F32)


def _row_spec(tm, width, cb):
    return pl.BlockSpec((tm, width), lambda i: (i, cb))


def _full_spec(shape):
    nd = len(shape)
    return pl.BlockSpec(tuple(shape), lambda i: (0,) * nd)


def rowwise_fwd(name, f, rows, params, outs, tm):
    n = rows[0][0].shape[0]
    nr, npar = len(rows), len(params)

    def body(*refs):
        rv = [r[...] for r in refs[:nr]]
        pv = [r[...] for r in refs[nr:nr + npar]]
        res = f(rv, pv)
        for o_ref, val in zip(refs[nr + npar:], res):
            o_ref[...] = val.astype(o_ref.dtype)

    return pl.pallas_call(
        body, name=name, grid=(n // tm,),
        in_specs=[_row_spec(tm, w, cb) for _, w, cb in rows] + [_full_spec(p.shape) for p in params],
        out_specs=[_row_spec(tm, w, 0) for w, _ in outs],
        out_shape=[jax.ShapeDtypeStruct((n, w), dt) for w, dt in outs],
        compiler_params=_cparams(("arbitrary",)),
    )(*[a for a, _, _ in rows], *params)


def rowwise_bwd(name, f, rows, params, douts, tm, adds=(), grad_rows=None):
    n = rows[0][0].shape[0]
    nr, npar = len(rows), len(params)
    grad_rows = list(range(nr)) if grad_rows is None else list(grad_rows)
    flat_d = [d for group in douts for d in group]
    nd, na, ng = len(flat_d), len(adds), len(grad_rows)

    def body(*refs):
        rv = [r[...] for r in refs[:nr]]
        pv = [r[...] for r in refs[nr:nr + npar]]
        dflat = [r[...].astype(F32) for r in refs[nr + npar:nr + npar + nd]]
        av = [r[...] for r in refs[nr + npar + nd:nr + npar + nd + na]]
        o = nr + npar + nd + na
        drow_refs, dpar_refs = refs[o:o + ng], refs[o + ng:o + ng + npar]
        dv, pos = [], 0
        for group in douts:
            dv.append(sum(dflat[pos + 1:pos + len(group)], dflat[pos]))
            pos += len(group)

        @pl.when(pl.program_id(0) == 0)
        def _():
            for r in dpar_refs:
                r[...] = jnp.zeros_like(r)

        def g(grows, pars):
            full = list(rv)
            for i, val in zip(grad_rows, grows):
                full[i] = val
            return f(full, pars)

        res, vjp = jax.vjp(g, [rv[i] for i in grad_rows], pv)
        drows, dpars = vjp([d.astype(r.dtype) for d, r in zip(dv, res)])
        drows = [d.astype(F32) for d in drows]
        for (idx, _), a in zip(adds, av):
            drows[idx] = drows[idx] + a.astype(F32)
        for r, d in zip(drow_refs, drows):
            r[...] = d
        for r, d in zip(dpar_refs, dpars):
            r[...] += d.astype(F32)

    res = pl.pallas_call(
        body, name=name, grid=(n // tm,),
        in_specs=[_row_spec(tm, w, cb) for _, w, cb in rows] + [_full_spec(p.shape) for p in params]
        + [_row_spec(tm, w, cb) for _, w, cb in flat_d] + [_row_spec(tm, w, cb) for _, (_, w, cb) in adds],
        out_specs=[_row_spec(tm, rows[i][1], 0) for i in grad_rows] + [_full_spec(p.shape) for p in params],
        out_shape=[jax.ShapeDtypeStruct((n, rows[i][1]), F32) for i in grad_rows]
        + [jax.ShapeDtypeStruct(p.shape, F32) for p in params],
        compiler_params=_cparams(("arbitrary",)),
    )(*[a for a, _, _ in rows], *params, *[a for a, _, _ in flat_d], *[a for _, (a, _, _) in adds])
    return res[:ng], res[ng:]


def matmul(name, a, b, mode, out_dtype, tm, tn, tk):
    if mode == "nn":
        (m, k), n = a.shape, b.shape[1]
        a_spec = pl.BlockSpec((tm, tk), lambda i, j, kk: (i, kk))
        b_spec = pl.BlockSpec((tk, tn), lambda i, j, kk: (kk, j))
        dims = (((1,), (0,)), ((), ()))
    elif mode == "nt":
        (m, k), n = a.shape, b.shape[0]
        a_spec = pl.BlockSpec((tm, tk), lambda i, j, kk: (i, kk))
        b_spec = pl.BlockSpec((tn, tk), lambda i, j, kk: (j, kk))
        dims = (((1,), (1,)), ((), ()))
    else:
        (k, m), n = a.shape, b.shape[1]
        a_spec = pl.BlockSpec((tk, tm), lambda i, j, kk: (kk, i))
        b_spec = pl.BlockSpec((tk, tn), lambda i, j, kk: (kk, j))
        dims = (((0,), (0,)), ((), ()))
    assert m % tm == 0 and n % tn == 0 and k % tk == 0, (name, a.shape, b.shape, tm, tn, tk)
    nk = k // tk

    def body(a_ref, b_ref, o_ref, acc_ref):
        kk = pl.program_id(2)

        @pl.when(kk == 0)
        def _():
            acc_ref[...] = jnp.zeros_like(acc_ref)

        acc_ref[...] += lax.dot_general(a_ref[...].astype(MXU_DTYPE), b_ref[...].astype(MXU_DTYPE), dims,
                                        preferred_element_type=F32)

        @pl.when(kk == nk - 1)
        def _():
            o_ref[...] = acc_ref[...].astype(o_ref.dtype)

    return pl.pallas_call(
        body, name=name, grid=(m // tm, n // tn, nk),
        in_specs=[a_spec, b_spec],
        out_specs=pl.BlockSpec((tm, tn), lambda i, j, kk: (i, j)),
        out_shape=jax.ShapeDtypeStruct((m, n), out_dtype),
        scratch_shapes=[pltpu.VMEM((tm, tn), F32)],
        compiler_params=_cparams(("arbitrary", "arbitrary", "arbitrary")),
    )(a, b)


def _prev(u, first):
    return jnp.where(first, 0.0, pltpu.roll(u, 1, 0))


def _next(u, last):
    return jnp.where(last, 0.0, pltpu.roll(u, u.shape[0] - 1, 0))


def _edge_masks(t, w):
    row = lax.broadcasted_iota(jnp.int32, (t, w), 0)
    return row == 0, row == t - 1


SHIFT_CW = 256


def shift_fwd(p, mu_prev, mu_next, nb, t):
    cw, c0 = SHIFT_CW, C_RW // SHIFT_CW

    def body(p_ref, mp_ref, mn_ref, s_ref):
        x = p_ref[...]
        first, last = _edge_masks(t, cw)
        s_ref[...] = x + mp_ref[...] * (_prev(x, first) - x) + mn_ref[...] * (_next(x, last) - x)

    return pl.pallas_call(
        body, name="rwkv_shift_fwd", grid=(nb, RW_PW // cw),
        in_specs=[pl.BlockSpec((t, cw), lambda b, j: (b, c0 + j)), pl.BlockSpec((1, cw), lambda b, j: (0, j)),
                  pl.BlockSpec((1, cw), lambda b, j: (0, j))],
        out_specs=pl.BlockSpec((t, cw), lambda b, j: (b, j)),
        out_shape=jax.ShapeDtypeStruct((nb * t, RW_PW), F32),
        compiler_params=_cparams(("arbitrary", "arbitrary")),
    )(p, mu_prev, mu_next)


def shift_bwd(p, ds, mu_prev, mu_next, nb, t):
    cw, c0 = SHIFT_CW, C_RW // SHIFT_CW

    def body(p_ref, ds_ref, mp_ref, mn_ref, dp_ref, dmp_ref, dmn_ref):
        @pl.when(pl.program_id(1) == 0)
        def _():
            dmp_ref[...] = jnp.zeros_like(dmp_ref)
            dmn_ref[...] = jnp.zeros_like(dmn_ref)

        x, g = p_ref[...], ds_ref[...]
        mp, mn = mp_ref[...], mn_ref[...]
        first, last = _edge_masks(t, cw)
        dp_ref[...] = g * (1.0 - mp - mn) + _next(mp * g, last) + _prev(mn * g, first)
        dmp_ref[...] += jnp.sum(g * (_prev(x, first) - x), axis=0, keepdims=True)
        dmn_ref[...] += jnp.sum(g * (_next(x, last) - x), axis=0, keepdims=True)

    return pl.pallas_call(
        body, name="rwkv_shift_bwd", grid=(RW_PW // cw, nb),
        in_specs=[pl.BlockSpec((t, cw), lambda j, b: (b, c0 + j)), pl.BlockSpec((t, cw), lambda j, b: (b, j)),
                  pl.BlockSpec((1, cw), lambda j, b: (0, j)), pl.BlockSpec((1, cw), lambda j, b: (0, j))],
        out_specs=[pl.BlockSpec((t, cw), lambda j, b: (b, j)), pl.BlockSpec((1, cw), lambda j, b: (0, j)),
                   pl.BlockSpec((1, cw), lambda j, b: (0, j))],
        out_shape=[jax.ShapeDtypeStruct((nb * t, RW_PW), F32), jax.ShapeDtypeStruct((1, RW_PW), F32),
                   jax.ShapeDtypeStruct((1, RW_PW), F32)],
        compiler_params=_cparams(("arbitrary", "arbitrary")),
    )(p, ds, mu_prev, mu_next)


def conv_glu_fwd(u, cw, cb, nb, t):
    def body(u_ref, w_ref, b_ref, z_ref):
        x, w = u_ref[...], w_ref[...]
        first, last = _edge_masks(t, 2 * LANE)
        c = w[0:1] * _prev(x, first) + w[1:2] * x + w[2:3] * _next(x, last) + b_ref[...]
        z_ref[...] = (_silu(c[:, :LANE]) * c[:, LANE:]).astype(z_ref.dtype)

    return pl.pallas_call(
        body, name="conv_glu_fwd", grid=(nb, FFP // LANE),
        in_specs=[pl.BlockSpec((t, 2 * LANE), lambda b, j: (b, j)), pl.BlockSpec((3, 2 * LANE), lambda b, j: (0, j)),
                  pl.BlockSpec((1, 2 * LANE), lambda b, j: (0, j))],
        out_specs=pl.BlockSpec((t, LANE), lambda b, j: (b, j)),
        out_shape=jax.ShapeDtypeStruct((nb * t, FFP), MXU_DTYPE),
        compiler_params=_cparams(("arbitrary", "arbitrary")),
    )(u, cw, cb)


def conv_glu_bwd(u, dz, cw, cb, nb, t):
    def body(u_ref, dz_ref, w_ref, b_ref, du_ref, dw_ref, db_ref):
        @pl.when(pl.program_id(1) == 0)
        def _():
            dw_ref[...] = jnp.zeros_like(dw_ref)
            db_ref[...] = jnp.zeros_like(db_ref)

        x, w, g = u_ref[...], w_ref[...], dz_ref[...]
        first, last = _edge_masks(t, 2 * LANE)
        xp, xn = _prev(x, first), _next(x, last)
        c = w[0:1] * xp + w[1:2] * x + w[2:3] * xn + b_ref[...]
        cg, cv = c[:, :LANE], c[:, LANE:]
        sg = _sigmoid(cg)
        dcg = g * cv * (sg * (1.0 + cg * (1.0 - sg)))
        dcv = g * (cg * sg)
        dc = jnp.concatenate([dcg, dcv], axis=1)
        du = w[1:2] * dc + _next(w[0:1] * dc, last) + _prev(w[2:3] * dc, first)
        du_ref[...] = du.astype(du_ref.dtype)
        dw_ref[0:1, :] += jnp.sum(dc * xp, axis=0, keepdims=True)
        dw_ref[1:2, :] += jnp.sum(dc * x, axis=0, keepdims=True)
        dw_ref[2:3, :] += jnp.sum(dc * xn, axis=0, keepdims=True)
        db_ref[...] += jnp.sum(dc, axis=0, keepdims=True)

    return pl.pallas_call(
        body, name="conv_glu_bwd", grid=(FFP // LANE, nb),
        in_specs=[pl.BlockSpec((t, 2 * LANE), lambda j, b: (b, j)), pl.BlockSpec((t, LANE), lambda j, b: (b, j)),
                  pl.BlockSpec((3, 2 * LANE), lambda j, b: (0, j)), pl.BlockSpec((1, 2 * LANE), lambda j, b: (0, j))],
        out_specs=[pl.BlockSpec((t, 2 * LANE), lambda j, b: (b, j)), pl.BlockSpec((3, 2 * LANE), lambda j, b: (0, j)),
                   pl.BlockSpec((1, 2 * LANE), lambda j, b: (0, j))],
        out_shape=[jax.ShapeDtypeStruct((nb * t, 2 * FFP), MXU_DTYPE), jax.ShapeDtypeStruct((3, 2 * FFP), F32),
                   jax.ShapeDtypeStruct((1, 2 * FFP), F32)],
        compiler_params=_cparams(("arbitrary", "arbitrary")),
    )(u, dz, cw, cb)


def _gla_chunk(q, k, v, afab, wa2p, ba, s_in, reverse):
    c = GLA_CHUNK
    ri = lax.broadcasted_iota(jnp.int32, (c, c), 0)
    ci = lax.broadcasted_iota(jnp.int32, (c, c), 1)
    keep = (ci >= ri) if reverse else (ci <= ri)
    i_ref = (c - 1 - c // 2) if reverse else (c // 2)
    pick_ref = (ci == i_ref).astype(F32)
    ones_cc = jnp.ones((c, c), F32)
    lane = lax.broadcasted_iota(jnp.int32, (1, LANE), 1)
    outs, states = [None] * GLA_H, [None] * GLA_H
    for pr in range(GLA_H // 2):
        la = -_softplus(-(mm(afab, wa2p[pr]) + ba[pr])) * (1.0 / GLA_LOGIT_NORM)
        b = mm_exact(keep.astype(F32), la)
        b_ref = mm_exact(pick_ref, b)
        b_last = mm_exact(ones_cc, la)
        qs = q[pr] * (GLA_DK ** -0.5)
        qi = qs * jnp.exp(b - b_ref)
        ki = k[pr] * jnp.exp(b_ref - b)
        kd = k[pr] * jnp.exp(b_last - b)
        qb = qs * jnp.exp(b)
        dec = jnp.exp(mm_tn_exact(la, jnp.ones((c, LANE), F32)))
        for h in (2 * pr, 2 * pr + 1):
            m = ((lane // GLA_DK) == (h % 2)).astype(F32)
            a = jnp.where(keep, mm_nt(qi * m, ki), 0.0)
            o_intra = mm(a, v[h])
            kv = mm_tn(kd * m, v[h])
            o_inter = mm(qb * m, s_in[h])
            outs[h] = o_intra + o_inter
            states[h] = s_in[h] * dec + kv
    return outs, states


def _gla_load(q_ref, k_ref, v_ref, w_ref, ba_ref, rows):
    q = [q_ref[rows, pr * LANE:(pr + 1) * LANE] for pr in range(GLA_H // 2)]
    k = [k_ref[rows, pr * LANE:(pr + 1) * LANE] for pr in range(GLA_H // 2)]
    v = [v_ref[rows, h * GLA_DV:(h + 1) * GLA_DV] for h in range(GLA_H)]
    w = [w_ref[:, pr * LANE:(pr + 1) * LANE] for pr in range(GLA_H // 2)]
    ba = [ba_ref[:, pr * LANE:(pr + 1) * LANE] for pr in range(GLA_H // 2)]
    return q, k, v, w, ba


GLA_TILE = 512


def _gla_specs(nb, t, tile, reverse):
    nt = t // tile
    rb = (lambda b, j: b * nt + (nt - 1 - j)) if reverse else (lambda b, j: b * nt + j)
    return nt, rb


def gla_fwd(p, wa2p, ba, o_add, nb, t, reverse):
    tile = min(GLA_TILE, t)
    cpt = tile // GLA_CHUNK
    nt, rb = _gla_specs(nb, t, tile, reverse)
    has_add = o_add is not None

    def body(*refs):
        if has_add:
            q_ref, k_ref, v_ref, af_ref, w_ref, ba_ref, add_ref, o_ref, hist_ref, s_ref = refs
        else:
            q_ref, k_ref, v_ref, af_ref, w_ref, ba_ref, o_ref, hist_ref, s_ref = refs

        @pl.when(pl.program_id(1) == 0)
        def _():
            s_ref[...] = jnp.zeros_like(s_ref)

        def step(i, carry):
            ci = (cpt - 1 - i) if reverse else i
            rows = pl.ds(pl.multiple_of(ci * GLA_CHUNK, GLA_CHUNK), GLA_CHUNK)
            s_in = [s_ref[h] for h in range(GLA_H)]
            for h in range(GLA_H):
                hist_ref[0, ci, h] = s_in[h]
            q, k, v, w, ba = _gla_load(q_ref, k_ref, v_ref, w_ref, ba_ref, rows)
            outs, states = _gla_chunk(q, k, v, af_ref[rows, :], w, ba, s_in, reverse)
            for h in range(GLA_H):
                oh = outs[h]
                if has_add:
                    oh = oh + add_ref[rows, h * GLA_DV:(h + 1) * GLA_DV]
                o_ref[rows, h * GLA_DV:(h + 1) * GLA_DV] = oh
                s_ref[h] = states[h]
            return carry

        lax.fori_loop(0, cpt, step, 0)

    hist_map = (lambda b, j: (b, nt - 1 - j, 0, 0, 0)) if reverse else (lambda b, j: (b, j, 0, 0, 0))
    in_specs = [pl.BlockSpec((tile, 256), lambda b, j: (rb(b, j), C_Q // 256)),
                pl.BlockSpec((tile, 256), lambda b, j: (rb(b, j), C_K // 256)),
                pl.BlockSpec((tile, 512), lambda b, j: (rb(b, j), C_V // 512)),
                pl.BlockSpec((tile, LANE), lambda b, j: (rb(b, j), C_AFAB // LANE)),
                pl.BlockSpec((LANE, 256), lambda b, j: (0, 0)), pl.BlockSpec((1, 256), lambda b, j: (0, 0))]
    args = [p, p, p, p, wa2p, ba]
    if has_add:
        in_specs.append(pl.BlockSpec((tile, 512), lambda b, j: (rb(b, j), 0)))
        args.append(o_add)
    return pl.pallas_call(
        body, name="gla_fwd_rev" if reverse else "gla_fwd", grid=(nb, nt),
        in_specs=in_specs,
        out_specs=[pl.BlockSpec((tile, 512), lambda b, j: (rb(b, j), 0)),
                   pl.BlockSpec((1, cpt, GLA_H, LANE, LANE), hist_map)],
        out_shape=[jax.ShapeDtypeStruct((nb * t, 512), F32),
                   jax.ShapeDtypeStruct((nb, t // GLA_CHUNK, GLA_H, LANE, LANE), F32)],
        scratch_shapes=[pltpu.VMEM((GLA_H, LANE, LANE), F32)],
        compiler_params=_cparams(("arbitrary", "arbitrary")),
    )(*args)


def gla_bwd(p, wa2p, ba, hist, do, dprev, nb, t, reverse):
    tile = min(GLA_TILE, t)
    cpt = tile // GLA_CHUNK
    nt, rb_f = _gla_specs(nb, t, tile, reverse)
    rb = lambda b, j: rb_f(b, nt - 1 - j)
    has_prev = dprev is not None

    def body(*refs):
        if has_prev:
            q_ref, k_ref, v_ref, af_ref, w_ref, ba_ref, hist_ref, do_ref, prev_ref, dqkv_ref, dw_ref, dba_ref, ds_ref = refs
        else:
            q_ref, k_ref, v_ref, af_ref, w_ref, ba_ref, hist_ref, do_ref, dqkv_ref, dw_ref, dba_ref, ds_ref = refs

        @pl.when((pl.program_id(0) == 0) & (pl.program_id(1) == 0))
        def _():
            dw_ref[...] = jnp.zeros_like(dw_ref)
            dba_ref[...] = jnp.zeros_like(dba_ref)

        @pl.when(pl.program_id(1) == 0)
        def _():
            ds_ref[...] = jnp.zeros_like(ds_ref)

        def step(i, carry):
            ci = i if reverse else (cpt - 1 - i)
            rows = pl.ds(pl.multiple_of(ci * GLA_CHUNK, GLA_CHUNK), GLA_CHUNK)
            s_in = [hist_ref[0, ci, h] for h in range(GLA_H)]
            fn = functools.partial(_gla_chunk, reverse=reverse)
            q, k, v, w, ba = _gla_load(q_ref, k_ref, v_ref, w_ref, ba_ref, rows)
            _, vjp = jax.vjp(fn, q, k, v, af_ref[rows, :], w, ba, s_in)
            d_o = [do_ref[rows, h * GLA_DV:(h + 1) * GLA_DV] for h in range(GLA_H)]
            d_s = [ds_ref[h] for h in range(GLA_H)]
            dq, dk, dv, daf, dw, dba, ds_in = vjp((d_o, d_s))
            pieces = [(pr * LANE, dq[pr]) for pr in range(2)] + [(256 + pr * LANE, dk[pr]) for pr in range(2)]
            pieces += [(512 + h * GLA_DV, dv[h]) for h in range(GLA_H)] + [(1024, daf)]
            for c0, val in pieces:
                if has_prev:
                    val = val + prev_ref[rows, c0:c0 + LANE]
                dqkv_ref[rows, c0:c0 + LANE] = val
            for pr in range(2):
                dw_ref[:, pr * LANE:(pr + 1) * LANE] += dw[pr]
                dba_ref[:, pr * LANE:(pr + 1) * LANE] += dba[pr]
            for h in range(GLA_H):
                ds_ref[h] = ds_in[h]
            return carry

        lax.fori_loop(0, cpt, step, 0)

    hist_map_f = (lambda b, j: (b, nt - 1 - j, 0, 0, 0)) if reverse else (lambda b, j: (b, j, 0, 0, 0))
    hist_map = lambda b, j: hist_map_f(b, nt - 1 - j)
    in_specs = [pl.BlockSpec((tile, 256), lambda b, j: (rb(b, j), C_Q // 256)),
                pl.BlockSpec((tile, 256), lambda b, j: (rb(b, j), C_K // 256)),
                pl.BlockSpec((tile, 512), lambda b, j: (rb(b, j), C_V // 512)),
                pl.BlockSpec((tile, LANE), lambda b, j: (rb(b, j), C_AFAB // LANE)),
                pl.BlockSpec((LANE, 256), lambda b, j: (0, 0)), pl.BlockSpec((1, 256), lambda b, j: (0, 0)),
                pl.BlockSpec((1, cpt, GLA_H, LANE, LANE), hist_map),
                pl.BlockSpec((tile, 512), lambda b, j: (rb(b, j), 0))]
    args = [p, p, p, p, wa2p, ba, hist, do]
    if has_prev:
        in_specs.append(pl.BlockSpec((tile, 1152), lambda b, j: (rb(b, j), 0)))
        args.append(dprev)
    return pl.pallas_call(
        body, name="gla_bwd_rev" if reverse else "gla_bwd", grid=(nb, nt),
        in_specs=in_specs,
        out_specs=[pl.BlockSpec((tile, 1152), lambda b, j: (rb(b, j), 0)),
                   pl.BlockSpec((LANE, 256), lambda b, j: (0, 0)), pl.BlockSpec((1, 256), lambda b, j: (0, 0))],
        out_shape=[jax.ShapeDtypeStruct((nb * t, 1152), F32), jax.ShapeDtypeStruct((LANE, 256), F32),
                   jax.ShapeDtypeStruct((1, 256), F32)],
        scratch_shapes=[pltpu.VMEM((GLA_H, LANE, LANE), F32)],
        compiler_params=_cparams(("arbitrary", "arbitrary")),
    )(*args)


SCAN_TB = 8
RW_VH = RW_N // 2


def _bwd_lanes():
    lane = lax.broadcasted_iota(jnp.int32, (1, LANE), 1)
    return ((lane // (LANE // 4)) % 2) == 1


def _comm_specs(comm):
    anyspec = pl.BlockSpec(memory_space=pl.ANY)
    n = len(comm)
    shapes = [jax.ShapeDtypeStruct((N_DEV,) + (a.shape[1:] if sc else a.shape), a.dtype) for a, sc in comm]
    sems = [pltpu.SemaphoreType.DMA((n, N_DEV - 1)), pltpu.SemaphoreType.DMA((n, N_DEV - 1)), pltpu.SemaphoreType.DMA((n,))] if n else []
    return [a for a, _ in comm], [anyspec] * n, shapes, sems


def rwkv_scan_fwd(r, wf, wb, k, a, b, v, comm=()):
    t = r.shape[0]
    nt = t // SCAN_TB
    nc = len(comm)
    flags = [sc for _, sc in comm]

    def body(*refs):
        (rf, rm, kf, km, af, am, bf, bm, wf_ref, wb_ref, vf, vm), refs = refs[:12], refs[12:]
        c_in, refs = refs[:nc], refs[nc:]
        (yf_ref, ym_ref, hist_ref, sa_ref, fin_ref), refs = refs[:5], refs[5:]
        c_out, refs = refs[:nc], refs[nc:]
        s_ref, sems = refs[0], refs[1:]
        i = pl.program_id(0)
        if nc:
            start, wait = _exchange_plan(flags, c_in, c_out, *sems)

        @pl.when(i == 0)
        def _():
            s_ref[...] = jnp.zeros_like(s_ref)
            if nc:
                start()

        bwd = _bwd_lanes()

        def step(tt, carry):
            mt = SCAN_TB - 1 - tt
            pick = lambda f_ref, m_ref: jnp.where(bwd, m_ref[mt], f_ref[tt])
            rt, kt, at, bt, wt = pick(rf, rm), pick(kf, km), pick(af, am), pick(bf, bm), pick(wf_ref, wb_ref)
            for vi in range(RW_VH):
                sv = s_ref[vi]
                hist_ref[tt, vi] = sv
                sa = jnp.sum(sv * at, axis=0, keepdims=True)
                v_row = jnp.where(bwd, vm[mt, vi:vi + 1, :], vf[tt, vi:vi + 1, :])
                sn = sv * wt + sa * bt + v_row * kt
                s_ref[vi] = sn
                y_row = jnp.sum(sn * rt, axis=0, keepdims=True)
                yf_ref[tt, vi:vi + 1, :] = y_row
                ym_ref[mt, vi:vi + 1, :] = y_row
                sa_ref[tt, vi:vi + 1, :] = sa
            return carry

        lax.fori_loop(0, SCAN_TB, step, 0)

        @pl.when(i == nt - 1)
        def _():
            fin_ref[...] = s_ref[...]
            if nc:
                wait()

    fwd_map, mir_map = (lambda i: (i, 0, 0)), (lambda i: (nt - 1 - i, 0, 0))
    kf_spec, km_spec = pl.BlockSpec((SCAN_TB, RW_N, LANE), fwd_map), pl.BlockSpec((SCAN_TB, RW_N, LANE), mir_map)
    vf_spec, vm_spec = pl.BlockSpec((SCAN_TB, RW_VH, LANE), fwd_map), pl.BlockSpec((SCAN_TB, RW_VH, LANE), mir_map)
    c_args, c_specs, c_shapes, c_sems = _comm_specs(comm)
    vshape = jax.ShapeDtypeStruct((t, RW_VH, LANE), F32)
    return pl.pallas_call(
        body, name="rwkv_scan_fwd", grid=(nt,),
        in_specs=[kf_spec, km_spec] * 4 + [kf_spec, km_spec, vf_spec, vm_spec] + c_specs,
        out_specs=[vf_spec, vm_spec, pl.BlockSpec((SCAN_TB, RW_VH, RW_N, LANE), lambda i: (i, 0, 0, 0)), vf_spec,
                   pl.BlockSpec((RW_VH, RW_N, LANE), lambda i: (0, 0, 0))] + c_specs,
        out_shape=[vshape, vshape, jax.ShapeDtypeStruct((t, RW_VH, RW_N, LANE), F32), vshape,
                   jax.ShapeDtypeStruct((RW_VH, RW_N, LANE), F32)] + c_shapes,
        scratch_shapes=[pltpu.VMEM((RW_VH, RW_N, LANE), F32)] + c_sems,
        compiler_params=_cparams(("arbitrary",)),
    )(r, r, k, k, a, a, b, b, wf, wb, v, v, *c_args)


def rwkv_scan_bwd(r, wf, wb, k, a, b, v, hist, sa, fin, dy, comm=()):
    t = r.shape[0]
    nt = t // SCAN_TB
    nc = len(comm)
    flags = [sc for _, sc in comm]

    def body(*refs):
        (rf, rm, kf, km, af, am, bf, bm, wf_ref, wb_ref, vf, vm, hist_ref, sa_ref, fin_ref, dyf, dym), refs = refs[:17], refs[17:]
        c_in, refs = refs[:nc], refs[nc:]
        k_outs, (dvf_ref, dvm_ref), refs = refs[:10], refs[10:12], refs[12:]
        c_out, refs = refs[:nc], refs[nc:]
        ds_ref, snext_ref, sems = refs[0], refs[1], refs[2:]
        i = pl.program_id(0)
        if nc:
            start, wait = _exchange_plan(flags, c_in, c_out, *sems)

        @pl.when(i == 0)
        def _():
            ds_ref[...] = jnp.zeros_like(ds_ref)
            snext_ref[...] = fin_ref[...]
            if nc:
                start()

        bwd = _bwd_lanes()

        for tt in range(SCAN_TB - 1, -1, -1):
            mt = SCAN_TB - 1 - tt
            pick = lambda f_ref, m_ref: jnp.where(bwd, m_ref[mt], f_ref[tt])
            rt, kt, at, bt, wt = pick(rf, rm), pick(kf, km), pick(af, am), pick(bf, bm), pick(wf_ref, wb_ref)
            zero = jnp.zeros((RW_N, LANE), F32)
            dr, dw, dk, da, db = zero, zero, zero, zero, zero
            for vi in range(RW_VH):
                sv = hist_ref[tt, vi]
                sn = hist_ref[tt + 1, vi] if tt + 1 < SCAN_TB else snext_ref[vi]
                sa_row = sa_ref[tt, vi:vi + 1, :]
                v_row = jnp.where(bwd, vm[mt, vi:vi + 1, :], vf[tt, vi:vi + 1, :])
                dy_row = jnp.where(bwd, dym[mt, vi:vi + 1, :], dyf[tt, vi:vi + 1, :])
                dsv = ds_ref[vi] + dy_row * rt
                dr = dr + sn * dy_row
                dsa = jnp.sum(dsv * bt, axis=0, keepdims=True)
                dw = dw + sv * dsv
                db = db + dsv * sa_row
                dk = dk + dsv * v_row
                dv_row = jnp.sum(dsv * kt, axis=0, keepdims=True)
                dvf_ref[tt, vi:vi + 1, :] = dv_row
                dvm_ref[mt, vi:vi + 1, :] = dv_row
                da = da + sv * dsa
                ds_ref[vi] = dsv * wt + dsa * at
            for j, val in enumerate((dr, dw, dk, da, db)):
                val = val + pltpu.roll(val, LANE // 2, 1)
                k_outs[2 * j][tt] = val
                k_outs[2 * j + 1][mt] = val
        snext_ref[...] = hist_ref[0]

        if nc:
            @pl.when(i == nt - 1)
            def _():
                wait()

    fwd_map, mir_map = (lambda i: (nt - 1 - i, 0, 0)), (lambda i: (i, 0, 0))
    kf_spec, km_spec = pl.BlockSpec((SCAN_TB, RW_N, LANE), fwd_map), pl.BlockSpec((SCAN_TB, RW_N, LANE), mir_map)
    vf_spec, vm_spec = pl.BlockSpec((SCAN_TB, RW_VH, LANE), fwd_map), pl.BlockSpec((SCAN_TB, RW_VH, LANE), mir_map)
    state_spec = pl.BlockSpec((RW_VH, RW_N, LANE), lambda i: (0, 0, 0))
    c_args, c_specs, c_shapes, c_sems = _comm_specs(comm)
    kshape, vshape = jax.ShapeDtypeStruct((t, RW_N, LANE), F32), jax.ShapeDtypeStruct((t, RW_VH, LANE), F32)
    res = pl.pallas_call(
        body, name="rwkv_scan_bwd", grid=(nt,),
        in_specs=[kf_spec, km_spec] * 4 + [kf_spec, km_spec, vf_spec, vm_spec,
                                           pl.BlockSpec((SCAN_TB, RW_VH, RW_N, LANE), lambda i: (nt - 1 - i, 0, 0, 0)),
                                           vf_spec, state_spec, vf_spec, vm_spec] + c_specs,
        out_specs=[kf_spec, km_spec] * 5 + [vf_spec, vm_spec] + c_specs,
        out_shape=[kshape] * 10 + [vshape] * 2 + c_shapes,
        scratch_shapes=[pltpu.VMEM((RW_VH, RW_N, LANE), F32), pltpu.VMEM((RW_VH, RW_N, LANE), F32)] + c_sems,
        compiler_params=_cparams(("arbitrary",)),
    )(r, r, k, k, a, a, b, b, wf, wb, v, v, hist, sa, fin, dy, dy, *c_args)
    return res


def to_scan_k(x, nb, t):
    y = x.reshape(nb, t, RW_H, RW_N).transpose(1, 3, 0, 2).reshape(t, RW_N, nb * RW_H)
    return jnp.tile(y, (1, 1, 4))


def to_scan_v(x, nb, t):
    y = x.reshape(nb, t, RW_H, 2, RW_VH).transpose(1, 4, 3, 0, 2).reshape(t, RW_VH, 2, 1, nb * RW_H)
    return jnp.tile(y, (1, 1, 1, 2, 1)).reshape(t, RW_VH, LANE)


def from_scan_k(xf, xm, nb, t):
    def one(y):
        return y.reshape(t, RW_N, nb, RW_H).transpose(2, 0, 3, 1).reshape(nb * t, RW_W)
    q = nb * RW_H
    return one(xf[:, :, 0:q]), one(xm[:, :, q:2 * q])


def from_scan_v(xf, xm, nb, t):
    def one(z):
        return z.transpose(3, 0, 4, 2, 1).reshape(nb * t, RW_W)
    yf, ym = xf.reshape(t, RW_VH, 2, 2, nb, RW_H), xm.reshape(t, RW_VH, 2, 2, nb, RW_H)
    return one(yf[:, :, :, 0]), one(ym[:, :, :, 1])


def f_norm(rows, params):
    (x,), (g,) = rows, params
    return [_rmsnorm(x, g)]


def f_rwkv_pre(rows, params):
    k, wlal, gl = rows
    w0f, w2f, w0b, w2b, a0, a2, g2, k_k, k_a = params
    seg = _segment_ones(RW_W, RW_N)
    tw = jnp.tanh(wlal)

    def decay(w0, w2):
        return jnp.exp(-jnp.exp(-_softplus(-(w0 + mm(tw, w2))) - 0.5))

    lr = _sigmoid(a0 + mm(wlal, a2))
    gate = mm(_sigmoid(gl), g2)
    kk = k * k_k
    kk = kk / jnp.maximum(jnp.sqrt(mm_exact(kk * kk, seg)), 1e-12)
    kp = k * (1.0 + (lr - 1.0) * k_a)
    return [decay(w0f, w2f), decay(w0b, w2b), kp, -kk, kk * lr, gate]


def f_branch_post(rows, params):
    o, og, y, r, kp, v, g = rows
    gla_g, ln_w, ln_b, r_k = params
    seg_gla = _segment_ones(GLA_H * GLA_DV, GLA_DV)
    seg_rw = _segment_ones(RW_W, RW_N)
    on = o * lax.rsqrt(mm_exact(o * o, seg_gla) * (1.0 / GLA_DV) + HEAD_NORM_EPS)
    oa = on * gla_g * _silu(og)
    mu = mm_exact(y, seg_rw) * (1.0 / RW_N)
    yc = y - mu
    var = mm_exact(yc * yc, seg_rw) * (1.0 / RW_N)
    yn = yc * lax.rsqrt(var + RW_GN_EPS) * ln_w + ln_b
    bonus = mm_exact(r * kp * r_k, seg_rw) * v
    return [oa, (yn + bonus) * g]


def f_merge(rows, params):
    ga, gb, ya, yb = rows
    return [_sigmoid(ga) * ya + _sigmoid(gb) * yb]


def f_norm2(rows, params):
    (x, mo), (g,) = rows, params
    x1 = x + mo
    return [x1, _rmsnorm(x1, g)]


def loss_head(x1, ffo, tgt, gf, tm):
    n = x1.shape[0]

    def body(x1_ref, f_ref, t_ref, g_ref, loss_ref, dx_ref, dg_ref):
        @pl.when(pl.program_id(0) == 0)
        def _():
            loss_ref[...] = jnp.zeros_like(loss_ref)
            dg_ref[...] = jnp.zeros_like(dg_ref)

        tgt_v = t_ref[...]

        def f(x2, g):
            err = _rmsnorm(x2, g) - tgt_v
            return jnp.sum(jnp.sum(err * err, axis=-1, keepdims=True), axis=0, keepdims=True) * (0.5 / D)

        val, vjp = jax.vjp(f, x1_ref[...] + f_ref[...], g_ref[...])
        dx, dg = vjp(jnp.ones((1, 1), F32))
        loss_ref[...] += val
        dx_ref[...] = dx
        dg_ref[...] += dg

    return pl.pallas_call(
        body, name="loss_head", grid=(n // tm,),
        in_specs=[_row_spec(tm, D, 0)] * 3 + [_full_spec((1, D))],
        out_specs=[_full_spec((1, 1)), _row_spec(tm, D, 0), _full_spec((1, D))],
        out_shape=[jax.ShapeDtypeStruct((1, 1), F32), jax.ShapeDtypeStruct((n, D), F32), jax.ShapeDtypeStruct((1, D), F32)],
        compiler_params=_cparams(("arbitrary",)),
    )(x1, ffo, tgt, gf)


def _pad_cols(a, width):
    return jnp.pad(a, ((0, 0), (0, width - a.shape[1])))


def w_in_to_padded(w):
    return _pad_cols(jnp.concatenate([w[:, 3360:5408], w[:, 0:1536], w[:, 1568:3360], w[:, 1536:1568]], axis=1), NP)


def w_in_from_padded(wp):
    return jnp.concatenate([wp[:, 2048:3584], wp[:, 5376:5408], wp[:, 3584:5376], wp[:, 0:2048]], axis=1)


def ff_interleave(a):
    r = a.shape[0]
    halves = jnp.stack([_pad_cols(a[:, :D_FF], FFP), _pad_cols(a[:, D_FF:], FFP)], axis=1)
    return halves.reshape(r, 2, FFP // LANE, LANE).transpose(0, 2, 1, 3).reshape(r, 2 * FFP)


def ff_deinterleave(a):
    r = a.shape[0]
    halves = a.reshape(r, FFP // LANE, 2, LANE).transpose(0, 2, 1, 3).reshape(r, 2, FFP)
    return halves[:, :, :D_FF].reshape(r, 2 * D_FF)


def _rows_into(w, rows, off):
    return jnp.zeros((rows, w.shape[1]), w.dtype).at[off:off + w.shape[0]].set(w)


LATE = ("gla_proj", "rwkv_proj", "w_out", "ffn_up", "ffn_conv_w", "ffn_down")


def local_step(x, tgt, w, nb, t, late_blocks=None):
    n = nb * t
    tm = min(n, 1024)
    tr = min(n, 256)
    vec = lambda a: a.reshape(1, -1)
    w = dict(w)

    w_in_p = w_in_to_padded(w["w_in"])
    wa2_f, wa2_b = _rows_into(w["gla_wa2_f"], LANE, 0), _rows_into(w["gla_wa2_b"], LANE, GLA_RANK)
    w2f, w2b = _rows_into(w["rwkv_w2_f"], LANE, 0), _rows_into(w["rwkv_w2_b"], LANE, 0)
    a2 = _rows_into(w["rwkv_a2"], LANE, 64)
    g1, g2n, gf = vec(w["norm1_g"]), vec(w["norm2_g"]), vec(w["norm_f_g"])
    mu_prev, mu_next = vec(w["rwkv_mu_prev"]), vec(w["rwkv_mu_next"])
    pre_params = [vec(w["rwkv_w0_f"]), w2f, vec(w["rwkv_w0_b"]), w2b, vec(w["rwkv_a0"]), a2, w["rwkv_g2"],
                  vec(w["rwkv_k_k"]), vec(w["rwkv_k_a"])]
    post_params = [vec(w["gla_norm_g"]), vec(w["rwkv_ln_w"]), vec(w["rwkv_ln_b"]), vec(w["rwkv_r_k"])]
    ba_f, ba_b = vec(w["gla_ba_f"]), vec(w["gla_ba_b"])

    (h1,) = rowwise_fwd("norm1_fwd", f_norm, [(x, D, 0)], [g1], [(D, MXU_DTYPE)], tr)
    p = matmul("proj_in", h1, w_in_p, "nn", F32, tm, 512, D)
    s = shift_fwd(p, mu_prev, mu_next, nb, t)
    pre_rows = [(s, 512, 1), (s, LANE, 1536 // LANE), (s, LANE, 1664 // LANE)]
    wf, wb, kp, a_s, b_s, g = rowwise_fwd("rwkv_pre_fwd", f_rwkv_pre, pre_rows, pre_params, [(RW_W, F32)] * 6, tr)
    sc = [to_scan_k(s[:, 0:512], nb, t), to_scan_k(wf, nb, t), to_scan_k(wb, nb, t), to_scan_k(kp, nb, t),
          to_scan_k(a_s, nb, t), to_scan_k(b_s, nb, t), to_scan_v(s[:, 1024:1536], nb, t)]
    comm = [] if late_blocks is None else [(late_blocks[k], False) for k in LATE]
    y_scf, y_scm, hist_rw, sa_sc, fin_rw, *gathered = rwkv_scan_fwd(*sc, comm=comm)
    for k, g_k in zip(LATE, gathered):
        w[k] = _gathered_to_full(g_k, SHARDED[k])
    ffn_up_p = ff_interleave(w["ffn_up"])
    conv_w_p, conv_b_p = ff_interleave(w["ffn_conv_w"]), ff_interleave(vec(w["ffn_conv_b"]))
    ffn_down_p = jnp.pad(w["ffn_down"], ((0, FFP - D_FF), (0, 0)))
    y_f, y_b = from_scan_v(y_scf, y_scm, nb, t)
    y = y_f + y_b
    o_f, hist_f = gla_fwd(p, wa2_f, ba_f, None, nb, t, False)
    o, hist_b = gla_fwd(p, wa2_b, ba_b, o_f, nb, t, True)
    post_rows = [(o, 512, 0), (p, 512, C_OG // 512), (y, 512, 0), (s, 512, 0), (kp, 512, 0), (s, 512, 2), (g, 512, 0)]
    oa, ob = rowwise_fwd("branch_post_fwd", f_branch_post, post_rows, post_params, [(512, MXU_DTYPE)] * 2, tr)
    ya = matmul("gla_proj", oa, w["gla_proj"], "nn", F32, tm, 512, 512)
    yb = matmul("rwkv_proj", ob, w["rwkv_proj"], "nn", F32, tm, 512, 512)
    merge_rows = [(p, D, 0), (p, D, 1), (ya, D, 0), (yb, D, 0)]
    (merged,) = rowwise_fwd("merge_fwd", f_merge, merge_rows, [], [(D, MXU_DTYPE)], tr)
    mo = matmul("w_out", merged, w["w_out"], "nn", F32, tm, 512, D)
    x1, h2 = rowwise_fwd("norm2_fwd", f_norm2, [(x, D, 0), (mo, D, 0)], [g2n], [(D, F32), (D, MXU_DTYPE)], tr)
    u = matmul("ffn_up", h2, ffn_up_p, "nn", F32, tm, 512, D)
    z = conv_glu_fwd(u, conv_w_p, conv_b_p, nb, t)
    ffo = matmul("ffn_down", z, ffn_down_p, "nn", F32, tm, 512, FFP // 2)
    loss, dx2, dgf = loss_head(x1, ffo, tgt, gf, tr)

    dz = matmul("ffn_down_dx", dx2, ffn_down_p, "nt", F32, tm, FFP // 2, D)
    d_ffn_down_p = matmul("ffn_down_dw", z, dx2, "tn", F32, FFP // 2, 512, tm)
    du, d_conv_w_p, d_conv_b_p = conv_glu_bwd(u, dz, conv_w_p, conv_b_p, nb, t)
    dh2 = matmul("ffn_up_dx", du, ffn_up_p, "nt", F32, tm, 512, 512)
    d_ffn_up_p = matmul("ffn_up_dw", h2, du, "tn", F32, 512, 512, tm)
    (dx1,), (dg2,) = rowwise_bwd("norm2_bwd", f_norm2, [(x, D, 0), (mo, D, 0)], [g2n],
                                 [[(dx2, D, 0)], [(dh2, D, 0)]], tr, grad_rows=[1])
    dmerged = matmul("w_out_dx", dx1, w["w_out"], "nt", F32, tm, 512, D)
    d_w_out = matmul("w_out_dw", merged, dx1, "tn", F32, 512, 512, tm)
    (dga, dgb, dya, dyb), _ = rowwise_bwd("merge_bwd", f_merge, merge_rows, [], [[(dmerged, D, 0)]], tr)
    d_oa = matmul("gla_proj_dx", dya, w["gla_proj"], "nt", F32, tm, 512, D)
    d_gla_proj = matmul("gla_proj_dw", oa, dya, "tn", F32, 512, 512, tm)
    d_ob = matmul("rwkv_proj_dx", dyb, w["rwkv_proj"], "nt", F32, tm, 512, D)
    d_rwkv_proj = matmul("rwkv_proj_dw", ob, dyb, "tn", F32, 512, 512, tm)
    (d_o, d_og, d_y, d_r_post, d_kp_post, d_v_post, d_g), d_post = rowwise_bwd(
        "branch_post_bwd", f_branch_post, post_rows, post_params, [[(d_oa, 512, 0)], [(d_ob, 512, 0)]], tr)
    late_grads = {"gla_proj": d_gla_proj, "rwkv_proj": d_rwkv_proj, "w_out": d_w_out, "ffn_up": ff_deinterleave(d_ffn_up_p),
                  "ffn_conv_w": ff_deinterleave(d_conv_w_p), "ffn_down": d_ffn_down_p[0:D_FF]}
    comm = [] if late_blocks is None else [(_full_to_slices(late_grads[k], SHARDED[k]), True) for k in LATE]
    dsc = rwkv_scan_bwd(*sc, hist_rw, sa_sc, fin_rw, to_scan_v(d_y, nb, t), comm=comm)
    received = dict(zip(LATE, dsc[12:]))
    d_r_f, d_r_b = from_scan_k(dsc[0], dsc[1], nb, t)
    d_wf, d_wb = from_scan_k(dsc[2], dsc[3], nb, t)
    d_kp_f, d_kp_b = from_scan_k(dsc[4], dsc[5], nb, t)
    d_a_f, d_a_b = from_scan_k(dsc[6], dsc[7], nb, t)
    d_b_f, d_b_b = from_scan_k(dsc[8], dsc[9], nb, t)
    d_v_f, d_v_b = from_scan_v(dsc[10], dsc[11], nb, t)
    (d_k, d_wlal, d_gl), d_pre = rowwise_bwd(
        "rwkv_pre_bwd", f_rwkv_pre, pre_rows, pre_params,
        [[(d_wf, 512, 0)], [(d_wb, 512, 0)], [(d_kp_f, 512, 0), (d_kp_b, 512, 0), (d_kp_post, 512, 0)],
         [(d_a_f, 512, 0), (d_a_b, 512, 0)], [(d_b_f, 512, 0), (d_b_b, 512, 0)], [(d_g, 512, 0)]], tr)
    ds = jnp.concatenate([d_r_f + d_r_b + d_r_post, d_k, d_v_f + d_v_b + d_v_post, d_wlal, d_gl], axis=1)
    dp_rw, d_mu_prev, d_mu_next = shift_bwd(p, ds, mu_prev, mu_next, nb, t)
    dqkv_f, d_wa2_f, d_ba_f = gla_bwd(p, wa2_f, ba_f, hist_f, d_o, None, nb, t, False)
    dqkv, d_wa2_b, d_ba_b = gla_bwd(p, wa2_b, ba_b, hist_b, d_o, dqkv_f, nb, t, True)
    dp = jnp.concatenate([dga, dgb, dqkv[:, 0:1024], d_og, dp_rw, dqkv[:, 1024:1152],
                          jnp.zeros((n, NP - C_AFAB - LANE), F32)], axis=1)
    dh1 = matmul("proj_in_dx", dp, w_in_p, "nt", F32, tm, 512, 512)
    d_w_in_p = matmul("proj_in_dw", h1, dp, "tn", F32, 512, 512, tm)
    (grad_x,), (dg1,) = rowwise_bwd("norm1_bwd", f_norm, [(x, D, 0)], [g1], [[(dh1, D, 0)]], tr, adds=[(0, (dx1, D, 0))])

    grads = {
        "norm1_g": dg1, "w_in": w_in_from_padded(d_w_in_p),
        "gla_wa2_f": d_wa2_f[0:GLA_RANK], "gla_ba_f": d_ba_f, "gla_wa2_b": d_wa2_b[GLA_RANK:2 * GLA_RANK], "gla_ba_b": d_ba_b,
        "gla_norm_g": d_post[0], "rwkv_mu_prev": d_mu_prev, "rwkv_mu_next": d_mu_next,
        "rwkv_w0_f": d_pre[0], "rwkv_w2_f": d_pre[1][0:64], "rwkv_w0_b": d_pre[2], "rwkv_w2_b": d_pre[3][0:64],
        "rwkv_a0": d_pre[4], "rwkv_a2": d_pre[5][64:128], "rwkv_g2": d_pre[6], "rwkv_k_k": d_pre[7], "rwkv_k_a": d_pre[8],
        "rwkv_r_k": d_post[3], "rwkv_ln_w": d_post[1], "rwkv_ln_b": d_post[2],
        "norm2_g": dg2, "ffn_conv_b": ff_deinterleave(d_conv_b_p), "norm_f_g": dgf, **late_grads,
    }
    return loss, grad_x, grads, received


MESH = pl.DeviceIdType.MESH


def remote_exchange(name, items):
    n = len(items)

    def body(*refs):
        start, wait = _exchange_plan([sc for _, sc in items], refs[:n], refs[n:2 * n], *refs[2 * n:])
        start()
        wait()

    args, specs, shapes, sems = _comm_specs(items)
    return pl.pallas_call(body, name=name, in_specs=specs, out_specs=specs, out_shape=shapes, scratch_shapes=sems)(*args)


def _exchange_plan(flags, in_refs, out_refs, send_sems, recv_sems, local_sems):
    x, y, c = lax.axis_index("x"), lax.axis_index("y"), lax.axis_index("c")
    me = 4 * x + 2 * y + c

    def peer(k):
        px = 1 - x if (k >> 2) & 1 else x
        py = 1 - y if (k >> 1) & 1 else y
        pc = 1 - c if k & 1 else c
        return (px, py, pc), 4 * px + 2 * py + pc

    def copies():
        own, sends, recvs = [], [], []
        for i, scatter in enumerate(flags):
            src = in_refs[i].at[me] if scatter else in_refs[i]
            own.append(pltpu.make_async_copy(src, out_refs[i].at[me], local_sems.at[i]))
        for k in range(1, N_DEV):
            dev, slot = peer(k)
            for i, scatter in enumerate(flags):
                src = in_refs[i].at[slot] if scatter else in_refs[i]
                pair = dict(send_sem=send_sems.at[i, k - 1], recv_sem=recv_sems.at[i, k - 1], device_id=dev, device_id_type=MESH)
                sends.append(pltpu.make_async_remote_copy(src_ref=src, dst_ref=out_refs[i].at[me], **pair))
                recvs.append(pltpu.make_async_remote_copy(src_ref=out_refs[i].at[slot], dst_ref=out_refs[i].at[slot], **pair))
        return own, sends, recvs

    def start():
        own, sends, _ = copies()
        for cp in own + sends:
            cp.start()

    def wait():
        own, sends, recvs = copies()
        for send, recv in zip(sends, recvs):
            recv.wait_recv()
            send.wait_send()
        for cp in own:
            cp.wait()

    return start, wait


def _adam_tiles(r, c):
    tc = 256 if (c % 256 == 0 and r * c > 128 * 1024) else c
    tr = 128 if (r % 128 == 0 and r > 128) else r
    return tr, tc


def adamw_reduce(name, parts, w, m, v):
    r, c = w.shape
    tr, tc = _adam_tiles(r, c)

    def body(p_ref, w_ref, m_ref, v_ref, g_ref, d_ref, nm_ref, nv_ref):
        g = p_ref[0]
        for d in range(1, N_DEV):
            g = g + p_ref[d]
        nm = ADAM_B1 * m_ref[...] + (1.0 - ADAM_B1) * g
        nv = ADAM_B2 * v_ref[...] + (1.0 - ADAM_B2) * (g * g)
        m_hat = nm / (1.0 - ADAM_B1 ** ADAM_STEP)
        v_hat = nv / (1.0 - ADAM_B2 ** ADAM_STEP)
        g_ref[...] = g
        d_ref[...] = -ADAM_LR * (m_hat / (jnp.sqrt(v_hat) + ADAM_EPS) + ADAM_WD * w_ref[...])
        nm_ref[...] = nm
        nv_ref[...] = nv

    spec = pl.BlockSpec((tr, tc), lambda i, j: (i, j))
    return pl.pallas_call(
        body, name=name, grid=(r // tr, c // tc),
        in_specs=[pl.BlockSpec((N_DEV, tr, tc), lambda i, j: (0, i, j)), spec, spec, spec],
        out_specs=[spec] * 4, out_shape=[jax.ShapeDtypeStruct((r, c), F32)] * 4,
        compiler_params=_cparams(("arbitrary", "arbitrary")),
    )(parts, w, m, v)


SHARDED = {"w_in": 1, "gla_wa2_f": 1, "gla_wa2_b": 1, "gla_proj": 1, "rwkv_w2_f": 1, "rwkv_w2_b": 1, "rwkv_a2": 1,
           "rwkv_g2": 1, "rwkv_proj": 1, "w_out": 0, "ffn_up": 1, "ffn_conv_w": 1, "ffn_down": 0}
BF16_GATHER = ("w_in", "gla_proj", "rwkv_proj", "w_out", "ffn_up", "ffn_down")
REPLICATED = ("norm1_g", "gla_ba_f", "gla_ba_b", "gla_norm_g", "rwkv_mu_prev", "rwkv_mu_next", "rwkv_w0_f", "rwkv_w0_b",
              "rwkv_a0", "rwkv_k_k", "rwkv_k_a", "rwkv_r_k", "rwkv_ln_w", "rwkv_ln_b", "norm2_g", "ffn_conv_b", "norm_f_g")
WEIGHTS = ("norm1_g", "w_in", "gla_wa2_f", "gla_ba_f", "gla_wa2_b", "gla_ba_b", "gla_norm_g", "gla_proj", "rwkv_mu_prev",
           "rwkv_mu_next", "rwkv_w0_f", "rwkv_w2_f", "rwkv_w0_b", "rwkv_w2_b", "rwkv_a0", "rwkv_a2", "rwkv_g2", "rwkv_k_k",
           "rwkv_k_a", "rwkv_r_k", "rwkv_ln_w", "rwkv_ln_b", "rwkv_proj", "w_out", "norm2_g", "ffn_up", "ffn_conv_w",
           "ffn_conv_b", "ffn_down", "norm_f_g")
PACK_ROWS = 144


def _gathered_to_full(g, axis):
    if axis == 0:
        return g.reshape(N_DEV * g.shape[1], g.shape[2])
    return g.transpose(1, 0, 2).reshape(g.shape[1], N_DEV * g.shape[2])


def _full_to_slices(a, axis):
    if axis == 0:
        return a.reshape(N_DEV, a.shape[0] // N_DEV, a.shape[1])
    return a.reshape(a.shape[0], N_DEV, a.shape[1] // N_DEV).transpose(1, 0, 2)


def _pack(d):
    flat = jnp.concatenate([d[k].reshape(-1).astype(F32) for k in REPLICATED])
    return jnp.pad(flat, (0, PACK_ROWS * LANE - flat.shape[0])).reshape(PACK_ROWS, LANE)


def _unpack(packed, shapes):
    flat, out, pos = packed.reshape(-1), {}, 0
    for k in REPLICATED:
        size = int(np.prod(shapes[k]))
        out[k] = flat[pos:pos + size].reshape(shapes[k])
        pos += size
    return out


def kernel(x, norm1_g, w_in, gla_wa2_f, gla_ba_f, gla_wa2_b, gla_ba_b, gla_norm_g, gla_proj, rwkv_mu_prev, rwkv_mu_next, rwkv_w0_f, rwkv_w2_f, rwkv_w0_b, rwkv_w2_b, rwkv_a0, rwkv_a2, rwkv_g2, rwkv_k_k, rwkv_k_a, rwkv_r_k, rwkv_ln_w, rwkv_ln_b, rwkv_proj, w_out, norm2_g, ffn_up, ffn_conv_w, ffn_conv_b, ffn_down, norm_f_g, loss_target, m_norm1_g, m_w_in, m_gla_wa2_f, m_gla_ba_f, m_gla_wa2_b, m_gla_ba_b, m_gla_norm_g, m_gla_proj, m_rwkv_mu_prev, m_rwkv_mu_next, m_rwkv_w0_f, m_rwkv_w2_f, m_rwkv_w0_b, m_rwkv_w2_b, m_rwkv_a0, m_rwkv_a2, m_rwkv_g2, m_rwkv_k_k, m_rwkv_k_a, m_rwkv_r_k, m_rwkv_ln_w, m_rwkv_ln_b, m_rwkv_proj, m_w_out, m_norm2_g, m_ffn_up, m_ffn_conv_w, m_ffn_conv_b, m_ffn_down, m_norm_f_g, v_norm1_g, v_w_in, v_gla_wa2_f, v_gla_ba_f, v_gla_wa2_b, v_gla_ba_b, v_gla_norm_g, v_gla_proj, v_rwkv_mu_prev, v_rwkv_mu_next, v_rwkv_w0_f, v_rwkv_w2_f, v_rwkv_w0_b, v_rwkv_w2_b, v_rwkv_a0, v_rwkv_a2, v_rwkv_g2, v_rwkv_k_k, v_rwkv_k_a, v_rwkv_r_k, v_rwkv_ln_w, v_rwkv_ln_b, v_rwkv_proj, v_w_out, v_norm2_g, v_ffn_up, v_ffn_conv_w, v_ffn_conv_b, v_ffn_down, v_norm_f_g):
    args = locals()
    wts = {k: args[k] for k in WEIGHTS}
    mom = {k: args["m_" + k] for k in WEIGHTS}
    var = {k: args["v_" + k] for k in WEIGHTS}
    shapes = {k: wts[k].shape for k in WEIGHTS}
    nb, t = x.shape[0], x.shape[1]
    mat = lambda a: a.reshape(a.shape[-2], a.shape[-1])

    block = lambda k: mat(wts[k]).astype(MXU_DTYPE) if k in BF16_GATHER else mat(wts[k])
    early = [k for k in SHARDED if k not in LATE]
    gathered = remote_exchange("gather_weights", [(block(k), False) for k in early])
    full = {k: _gathered_to_full(g, SHARDED[k]) for k, g in zip(early, gathered)}
    for k in REPLICATED:
        full[k] = wts[k].reshape(-1) if k in ("norm_f_g", "rwkv_r_k") else wts[k][0]

    loss, grad_x, grads, received = local_step(x.reshape(nb * t, D), loss_target.reshape(nb * t, D), full, nb, t,
                                               late_blocks={k: block(k) for k in LATE})

    items = [(_full_to_slices(grads[k], SHARDED[k]), True) for k in early] + [(_pack(grads), False)]
    rest = remote_exchange("exchange_grads", items)
    received.update(zip(early, rest[:-1]))

    res = {}
    for k in SHARDED:
        outs = adamw_reduce("adamw_" + k, received[k], mat(wts[k]), mat(mom[k]), mat(var[k]))
        res[k] = [o.reshape(shapes[k]) for o in outs]
    packed = adamw_reduce("adamw_replicated", rest[-1], _pack(wts), _pack(mom), _pack(var))
    unpacked = [_unpack(p, shapes) for p in packed]
    for k in REPLICATED:
        res[k] = [u[k] for u in unpacked]

    total = lax.psum(loss[0, 0], ("x", "y", "c"))
    out = [total, grad_x.reshape(x.shape)]
    for j in range(4):
        out += [res[k][j] for k in WEIGHTS]
    return tuple(out)
```

```python
import functools

import jax
import jax.numpy as jnp
import numpy as np
from jax import lax
from jax.experimental import pallas as pl
from jax.experimental.pallas import tpu as pltpu

F32 = jnp.float32
MXU_DTYPE = jnp.bfloat16

D = 1024
SEQ = 2048
GLA_H, GLA_DK, GLA_DV, GLA_CHUNK = 4, 64, 128, 64
GLA_RANK = 16
GLA_LOGIT_NORM = 16.0
RW_H, RW_N = 8, 64
RW_W = 512
D_FF = 2752
NORM_EPS = 1e-6
HEAD_NORM_EPS = 1e-5
RW_GN_EPS = RW_N * 1e-5
N_DEV = 8
ADAM_LR, ADAM_B1, ADAM_B2, ADAM_EPS, ADAM_WD, ADAM_STEP = 0.001, 0.9, 0.999, 1e-08, 0.01, 10

C_GA, C_GB, C_Q, C_K, C_V, C_OG = 0, 1024, 2048, 2304, 2560, 3072
C_RW = 3584
C_R, C_RK, C_RV, C_WLAL, C_GL = 3584, 4096, 4608, 5120, 5248
C_AFAB = 5376
NP = 5632
RW_PW = 1792
FFP = 2816
LANE = 128
VMEM_LIMIT = 56 * 1024 * 1024


def _cparams(sem):
    return pltpu.CompilerParams(dimension_semantics=sem, vmem_limit_bytes=VMEM_LIMIT)


@jax.custom_vjp
def mm(a, b):
    return jnp.dot(a.astype(MXU_DTYPE), b.astype(MXU_DTYPE), preferred_element_type=F32)


def _mm_fwd(a, b):
    return mm(a, b), (a, b)


def _mm_bwd(res, g):
    a, b = res
    gb = g.astype(MXU_DTYPE)
    da = lax.dot_general(gb, b.astype(MXU_DTYPE), (((1,), (1,)), ((), ())), preferred_element_type=F32)
    db = lax.dot_general(a.astype(MXU_DTYPE), gb, (((0,), (0,)), ((), ())), preferred_element_type=F32)
    return da.astype(a.dtype), db.astype(b.dtype)


mm.defvjp(_mm_fwd, _mm_bwd)


@jax.custom_vjp
def mm_nt(a, b):
    return lax.dot_general(a.astype(MXU_DTYPE), b.astype(MXU_DTYPE), (((1,), (1,)), ((), ())), preferred_element_type=F32)


def _mm_nt_fwd(a, b):
    return mm_nt(a, b), (a, b)


def _mm_nt_bwd(res, g):
    a, b = res
    gb = g.astype(MXU_DTYPE)
    da = jnp.dot(gb, b.astype(MXU_DTYPE), preferred_element_type=F32)
    db = lax.dot_general(gb, a.astype(MXU_DTYPE), (((0,), (0,)), ((), ())), preferred_element_type=F32)
    return da.astype(a.dtype), db.astype(b.dtype)


mm_nt.defvjp(_mm_nt_fwd, _mm_nt_bwd)


@jax.custom_vjp
def mm_tn(a, b):
    return lax.dot_general(a.astype(MXU_DTYPE), b.astype(MXU_DTYPE), (((0,), (0,)), ((), ())), preferred_element_type=F32)


def _mm_tn_fwd(a, b):
    return mm_tn(a, b), (a, b)


def _mm_tn_bwd(res, g):
    a, b = res
    gb = g.astype(MXU_DTYPE)
    da = lax.dot_general(b.astype(MXU_DTYPE), gb, (((1,), (1,)), ((), ())), preferred_element_type=F32)
    db = jnp.dot(a.astype(MXU_DTYPE), gb, preferred_element_type=F32)
    return da.astype(a.dtype), db.astype(b.dtype)


mm_tn.defvjp(_mm_tn_fwd, _mm_tn_bwd)


def mm_exact(a, b):
    return jnp.dot(a, b, preferred_element_type=F32, precision=lax.Precision.HIGHEST)


def mm_tn_exact(a, b):
    return lax.dot_general(a, b, (((0,), (0,)), ((), ())), preferred_element_type=F32, precision=lax.Precision.HIGHEST)


def _softplus(x):
    return jnp.maximum(x, 0.0) + jnp.log(1.0 + jnp.exp(-jnp.abs(x)))


def _sigmoid(x):
    return 1.0 / (1.0 + jnp.exp(-x))


def _silu(x):
    return x * _sigmoid(x)


def _rmsnorm(x, g):
    return x * lax.rsqrt(jnp.mean(x * x, axis=-1, keepdims=True) + NORM_EPS) * g


def _segment_ones(width, seg):
    i = lax.broadcasted_iota(jnp.int32, (width, width), 0) // seg
    j = lax.broadcasted_iota(jnp.int32, (width, width), 1) // seg
    return (i == j).astype(F32)


def _row_spec(tm, width, cb):
    return pl.BlockSpec((tm, width), lambda i: (i, cb))


def _full_spec(shape):
    nd = len(shape)
    return pl.BlockSpec(tuple(shape), lambda i: (0,) * nd)


def rowwise_fwd(name, f, rows, params, outs, tm):
    n = rows[0][0].shape[0]
    nr, npar = len(rows), len(params)

    def body(*refs):
        rv = [r[...] for r in refs[:nr]]
        pv = [r[...] for r in refs[nr:nr + npar]]
        res = f(rv, pv)
        for o_ref, val in zip(refs[nr + npar:], res):
            o_ref[...] = val.astype(o_ref.dtype)

    return pl.pallas_call(
        body, name=name, grid=(n // tm,),
        in_specs=[_row_spec(tm, w, cb) for _, w, cb in rows] + [_full_spec(p.shape) for p in params],
        out_specs=[_row_spec(tm, w, 0) for w, _ in outs],
        out_shape=[jax.ShapeDtypeStruct((n, w), dt) for w, dt in outs],
        compiler_params=_cparams(("arbitrary",)),
    )(*[a for a, _, _ in rows], *params)


def rowwise_bwd(name, f, rows, params, douts, tm, adds=(), grad_rows=None):
    n = rows[0][0].shape[0]
    nr, npar = len(rows), len(params)
    grad_rows = list(range(nr)) if grad_rows is None else list(grad_rows)
    flat_d = [d for group in douts for d in group]
    nd, na, ng = len(flat_d), len(adds), len(grad_rows)

    def body(*refs):
        rv = [r[...] for r in refs[:nr]]
        pv = [r[...] for r in refs[nr:nr + npar]]
        dflat = [r[...].astype(F32) for r in refs[nr + npar:nr + npar + nd]]
        av = [r[...] for r in refs[nr + npar + nd:nr + npar + nd + na]]
        o = nr + npar + nd + na
        drow_refs, dpar_refs = refs[o:o + ng], refs[o + ng:o + ng + npar]
        dv, pos = [], 0
        for group in douts:
            dv.append(sum(dflat[pos + 1:pos + len(group)], dflat[pos]))
            pos += len(group)

        @pl.when(pl.program_id(0) == 0)
        def _():
            for r in dpar_refs:
                r[...] = jnp.zeros_like(r)

        def g(grows, pars):
            full = list(rv)
            for i, val in zip(grad_rows, grows):
                full[i] = val
            return f(full, pars)

        res, vjp = jax.vjp(g, [rv[i] for i in grad_rows], pv)
        drows, dpars = vjp([d.astype(r.dtype) for d, r in zip(dv, res)])
        drows = [d.astype(F32) for d in drows]
        for (idx, _), a in zip(adds, av):
            drows[idx] = drows[idx] + a.astype(F32)
        for r, d in zip(drow_refs, drows):
            r[...] = d
        for r, d in zip(dpar_refs, dpars):
            r[...] += d.astype(F32)

    res = pl.pallas_call(
        body, name=name, grid=(n // tm,),
        in_specs=[_row_spec(tm, w, cb) for _, w, cb in rows] + [_full_spec(p.shape) for p in params]
        + [_row_spec(tm, w, cb) for _, w, cb in flat_d] + [_row_spec(tm, w, cb) for _, (_, w, cb) in adds],
        out_specs=[_row_spec(tm, rows[i][1], 0) for i in grad_rows] + [_full_spec(p.shape) for p in params],
        out_shape=[jax.ShapeDtypeStruct((n, rows[i][1]), F32) for i in grad_rows]
        + [jax.ShapeDtypeStruct(p.shape, F32) for p in params],
        compiler_params=_cparams(("arbitrary",)),
    )(*[a for a, _, _ in rows], *params, *[a for a, _, _ in flat_d], *[a for _, (a, _, _) in adds])
    return res[:ng], res[ng:]


def matmul(name, a, b, mode, out_dtype, tm, tn, tk):
    if mode == "nn":
        (m, k), n = a.shape, b.shape[1]
        a_spec = pl.BlockSpec((tm, tk), lambda i, j, kk: (i, kk))
        b_spec = pl.BlockSpec((tk, tn), lambda i, j, kk: (kk, j))
        dims = (((1,), (0,)), ((), ()))
    elif mode == "nt":
        (m, k), n = a.shape, b.shape[0]
        a_spec = pl.BlockSpec((tm, tk), lambda i, j, kk: (i, kk))
        b_spec = pl.BlockSpec((tn, tk), lambda i, j, kk: (j, kk))
        dims = (((1,), (1,)), ((), ()))
    else:
        (k, m), n = a.shape, b.shape[1]
        a_spec = pl.BlockSpec((tk, tm), lambda i, j, kk: (kk, i))
        b_spec = pl.BlockSpec((tk, tn), lambda i, j, kk: (kk, j))
        dims = (((0,), (0,)), ((), ()))
    assert m % tm == 0 and n % tn == 0 and k % tk == 0, (name, a.shape, b.shape, tm, tn, tk)
    nk = k // tk

    def body(a_ref, b_ref, o_ref, acc_ref):
        kk = pl.program_id(2)

        @pl.when(kk == 0)
        def _():
            acc_ref[...] = jnp.zeros_like(acc_ref)

        acc_ref[...] += lax.dot_general(a_ref[...].astype(MXU_DTYPE), b_ref[...].astype(MXU_DTYPE), dims,
                                        preferred_element_type=F32)

        @pl.when(kk == nk - 1)
        def _():
            o_ref[...] = acc_ref[...].astype(o_ref.dtype)

    return pl.pallas_call(
        body, name=name, grid=(m // tm, n // tn, nk),
        in_specs=[a_spec, b_spec],
        out_specs=pl.BlockSpec((tm, tn), lambda i, j, kk: (i, j)),
        out_shape=jax.ShapeDtypeStruct((m, n), out_dtype),
        scratch_shapes=[pltpu.VMEM((tm, tn), F32)],
        compiler_params=_cparams(("arbitrary", "arbitrary", "arbitrary")),
    )(a, b)


def _prev(u, first):
    return jnp.where(first, 0.0, pltpu.roll(u, 1, 0))


def _next(u, last):
    return jnp.where(last, 0.0, pltpu.roll(u, u.shape[0] - 1, 0))


def _edge_masks(t, w):
    row = lax.broadcasted_iota(jnp.int32, (t, w), 0)
    return row == 0, row == t - 1


SHIFT_CW = 256


def shift_fwd(p, mu_prev, mu_next, nb, t):
    cw, c0 = SHIFT_CW, C_RW // SHIFT_CW

    def body(p_ref, mp_ref, mn_ref, s_ref):
        x = p_ref[...]
        first, last = _edge_masks(t, cw)
        s_ref[...] = x + mp_ref[...] * (_prev(x, first) - x) + mn_ref[...] * (_next(x, last) - x)

    return pl.pallas_call(
        body, name="rwkv_shift_fwd", grid=(nb, RW_PW // cw),
        in_specs=[pl.BlockSpec((t, cw), lambda b, j: (b, c0 + j)), pl.BlockSpec((1, cw), lambda b, j: (0, j)),
                  pl.BlockSpec((1, cw), lambda b, j: (0, j))],
        out_specs=pl.BlockSpec((t, cw), lambda b, j: (b, j)),
        out_shape=jax.ShapeDtypeStruct((nb * t, RW_PW), F32),
        compiler_params=_cparams(("arbitrary", "arbitrary")),
    )(p, mu_prev, mu_next)


def shift_bwd(p, ds, mu_prev, mu_next, nb, t):
    cw, c0 = SHIFT_CW, C_RW // SHIFT_CW

    def body(p_ref, ds_ref, mp_ref, mn_ref, dp_ref, dmp_ref, dmn_ref):
        @pl.when(pl.program_id(1) == 0)
        def _():
            dmp_ref[...] = jnp.zeros_like(dmp_ref)
            dmn_ref[...] = jnp.zeros_like(dmn_ref)

        x, g = p_ref[...], ds_ref[...]
        mp, mn = mp_ref[...], mn_ref[...]
        first, last = _edge_masks(t, cw)
        dp_ref[...] = g * (1.0 - mp - mn) + _next(mp * g, last) + _prev(mn * g, first)
        dmp_ref[...] += jnp.sum(g * (_prev(x, first) - x), axis=0, keepdims=True)
        dmn_ref[...] += jnp.sum(g * (_next(x, last) - x), axis=0, keepdims=True)

    return pl.pallas_call(
        body, name="rwkv_shift_bwd", grid=(RW_PW // cw, nb),
        in_specs=[pl.BlockSpec((t, cw), lambda j, b: (b, c0 + j)), pl.BlockSpec((t, cw), lambda j, b: (b, j)),
                  pl.BlockSpec((1, cw), lambda j, b: (0, j)), pl.BlockSpec((1, cw), lambda j, b: (0, j))],
        out_specs=[pl.BlockSpec((t, cw), lambda j, b: (b, j)), pl.BlockSpec((1, cw), lambda j, b: (0, j)),
                   pl.BlockSpec((1, cw), lambda j, b: (0, j))],
        out_shape=[jax.ShapeDtypeStruct((nb * t, RW_PW), F32), jax.ShapeDtypeStruct((1, RW_PW), F32),
                   jax.ShapeDtypeStruct((1, RW_PW), F32)],
        compiler_params=_cparams(("arbitrary", "arbitrary")),
    )(p, ds, mu_prev, mu_next)


def conv_glu_fwd(u, cw, cb, nb, t):
    def body(u_ref, w_ref, b_ref, z_ref):
        x, w = u_ref[...], w_ref[...]
        first, last = _edge_masks(t, 2 * LANE)
        c = w[0:1] * _prev(x, first) + w[1:2] * x + w[2:3] * _next(x, last) + b_ref[...]
        z_ref[...] = (_silu(c[:, :LANE]) * c[:, LANE:]).astype(z_ref.dtype)

    return pl.pallas_call(
        body, name="conv_glu_fwd", grid=(nb, FFP // LANE),
        in_specs=[pl.BlockSpec((t, 2 * LANE), lambda b, j: (b, j)), pl.BlockSpec((3, 2 * LANE), lambda b, j: (0, j)),
                  pl.BlockSpec((1, 2 * LANE), lambda b, j: (0, j))],
        out_specs=pl.BlockSpec((t, LANE), lambda b, j: (b, j)),
        out_shape=jax.ShapeDtypeStruct((nb * t, FFP), MXU_DTYPE),
        compiler_params=_cparams(("arbitrary", "arbitrary")),
    )(u, cw, cb)


def conv_glu_bwd(u, dz, cw, cb, nb, t):
    def body(u_ref, dz_ref, w_ref, b_ref, du_ref, dw_ref, db_ref):
        @pl.when(pl.program_id(1) == 0)
        def _():
            dw_ref[...] = jnp.zeros_like(dw_ref)
            db_ref[...] = jnp.zeros_like(db_ref)

        x, w, g = u_ref[...], w_ref[...], dz_ref[...]
        first, last = _edge_masks(t, 2 * LANE)
        xp, xn = _prev(x, first), _next(x, last)
        c = w[0:1] * xp + w[1:2] * x + w[2:3] * xn + b_ref[...]
        cg, cv = c[:, :LANE], c[:, LANE:]
        sg = _sigmoid(cg)
        dcg = g * cv * (sg * (1.0 + cg * (1.0 - sg)))
        dcv = g * (cg * sg)
        dc = jnp.concatenate([dcg, dcv], axis=1)
        du = w[1:2] * dc + _next(w[0:1] * dc, last) + _prev(w[2:3] * dc, first)
        du_ref[...] = du.astype(du_ref.dtype)
        dw_ref[0:1, :] += jnp.sum(dc * xp, axis=0, keepdims=True)
        dw_ref[1:2, :] += jnp.sum(dc * x, axis=0, keepdims=True)
        dw_ref[2:3, :] += jnp.sum(dc * xn, axis=0, keepdims=True)
        db_ref[...] += jnp.sum(dc, axis=0, keepdims=True)

    return pl.pallas_call(
        body, name="conv_glu_bwd", grid=(FFP // LANE, nb),
        in_specs=[pl.BlockSpec((t, 2 * LANE), lambda j, b: (b, j)), pl.BlockSpec((t, LANE), lambda j, b: (b, j)),
                  pl.BlockSpec((3, 2 * LANE), lambda j, b: (0, j)), pl.BlockSpec((1, 2 * LANE), lambda j, b: (0, j))],
        out_specs=[pl.BlockSpec((t, 2 * LANE), lambda j, b: (b, j)), pl.BlockSpec((3, 2 * LANE), lambda j, b: (0, j)),
                   pl.BlockSpec((1, 2 * LANE), lambda j, b: (0, j))],
        out_shape=[jax.ShapeDtypeStruct((nb * t, 2 * FFP), MXU_DTYPE), jax.ShapeDtypeStruct((3, 2 * FFP), F32),
                   jax.ShapeDtypeStruct((1, 2 * FFP), F32)],
        compiler_params=_cparams(("arbitrary", "arbitrary")),
    )(u, dz, cw, cb)


def _gla_chunk(q, k, v, afab, wa2p, ba, s_in, reverse):
    c = GLA_CHUNK
    ri = lax.broadcasted_iota(jnp.int32, (c, c), 0)
    ci = lax.broadcasted_iota(jnp.int32, (c, c), 1)
    keep = (ci >= ri) if reverse else (ci <= ri)
    i_ref = (c - 1 - c // 2) if reverse else (c // 2)
    pick_ref = (ci == i_ref).astype(F32)
    ones_cc = jnp.ones((c, c), F32)
    lane = lax.broadcasted_iota(jnp.int32, (1, LANE), 1)
    outs, states = [None] * GLA_H, [None] * GLA_H
    for pr in range(GLA_H // 2):
        la = -_softplus(-(mm(afab, wa2p[pr]) + ba[pr])) * (1.0 / GLA_LOGIT_NORM)
        b = mm_exact(keep.astype(F32), la)
        b_ref = mm_exact(pick_ref, b)
        b_last = mm_exact(ones_cc, la)
        qs = q[pr] * (GLA_DK ** -0.5)
        qi = qs * jnp.exp(b - b_ref)
        ki = k[pr] * jnp.exp(b_ref - b)
        kd = k[pr] * jnp.exp(b_last - b)
        qb = qs * jnp.exp(b)
        dec = jnp.exp(mm_tn_exact(la, jnp.ones((c, LANE), F32)))
        for h in (2 * pr, 2 * pr + 1):
            m = ((lane // GLA_DK) == (h % 2)).astype(F32)
            a = jnp.where(keep, mm_nt(qi * m, ki), 0.0)
            o_intra = mm(a, v[h])
            kv = mm_tn(kd * m, v[h])
            o_inter = mm(qb * m, s_in[h])
            outs[h] = o_intra + o_inter
            states[h] = s_in[h] * dec + kv
    return outs, states


def _gla_load(q_ref, k_ref, v_ref, w_ref, ba_ref, rows):
    q = [q_ref[rows, pr * LANE:(pr + 1) * LANE] for pr in range(GLA_H // 2)]
    k = [k_ref[rows, pr * LANE:(pr + 1) * LANE] for pr in range(GLA_H // 2)]
    v = [v_ref[rows, h * GLA_DV:(h + 1) * GLA_DV] for h in range(GLA_H)]
    w = [w_ref[:, pr * LANE:(pr + 1) * LANE] for pr in range(GLA_H // 2)]
    ba = [ba_ref[:, pr * LANE:(pr + 1) * LANE] for pr in range(GLA_H // 2)]
    return q, k, v, w, ba


GLA_TILE = 512


def _gla_specs(nb, t, tile, reverse):
    nt = t // tile
    rb = (lambda b, j: b * nt + (nt - 1 - j)) if reverse else (lambda b, j: b * nt + j)
    return nt, rb


def gla_fwd(p, wa2p, ba, o_add, nb, t, reverse):
    tile = min(GLA_TILE, t)
    cpt = tile // GLA_CHUNK
    nt, rb = _gla_specs(nb, t, tile, reverse)
    has_add = o_add is not None

    def body(*refs):
        if has_add:
            q_ref, k_ref, v_ref, af_ref, w_ref, ba_ref, add_ref, o_ref, hist_ref, s_ref = refs
        else:
            q_ref, k_ref, v_ref, af_ref, w_ref, ba_ref, o_ref, hist_ref, s_ref = refs

        @pl.when(pl.program_id(1) == 0)
        def _():
            s_ref[...] = jnp.zeros_like(s_ref)

        def step(i, carry):
            ci = (cpt - 1 - i) if reverse else i
            rows = pl.ds(pl.multiple_of(ci * GLA_CHUNK, GLA_CHUNK), GLA_CHUNK)
            s_in = [s_ref[h] for h in range(GLA_H)]
            for h in range(GLA_H):
                hist_ref[0, ci, h] = s_in[h]
            q, k, v, w, ba = _gla_load(q_ref, k_ref, v_ref, w_ref, ba_ref, rows)
            outs, states = _gla_chunk(q, k, v, af_ref[rows, :], w, ba, s_in, reverse)
            for h in range(GLA_H):
                oh = outs[h]
                if has_add:
                    oh = oh + add_ref[rows, h * GLA_DV:(h + 1) * GLA_DV]
                o_ref[rows, h * GLA_DV:(h + 1) * GLA_DV] = oh
                s_ref[h] = states[h]
            return carry

        lax.fori_loop(0, cpt, step, 0)

    hist_map = (lambda b, j: (b, nt - 1 - j, 0, 0, 0)) if reverse else (lambda b, j: (b, j, 0, 0, 0))
    in_specs = [pl.BlockSpec((tile, 256), lambda b, j: (rb(b, j), C_Q // 256)),
                pl.BlockSpec((tile, 256), lambda b, j: (rb(b, j), C_K // 256)),
                pl.BlockSpec((tile, 512), lambda b, j: (rb(b, j), C_V // 512)),
                pl.BlockSpec((tile, LANE), lambda b, j: (rb(b, j), C_AFAB // LANE)),
                pl.BlockSpec((LANE, 256), lambda b, j: (0, 0)), pl.BlockSpec((1, 256), lambda b, j: (0, 0))]
    args = [p, p, p, p, wa2p, ba]
    if has_add:
        in_specs.append(pl.BlockSpec((tile, 512), lambda b, j: (rb(b, j), 0)))
        args.append(o_add)
    return pl.pallas_call(
        body, name="gla_fwd_rev" if reverse else "gla_fwd", grid=(nb, nt),
        in_specs=in_specs,
        out_specs=[pl.BlockSpec((tile, 512), lambda b, j: (rb(b, j), 0)),
                   pl.BlockSpec((1, cpt, GLA_H, LANE, LANE), hist_map)],
        out_shape=[jax.ShapeDtypeStruct((nb * t, 512), F32),
                   jax.ShapeDtypeStruct((nb, t // GLA_CHUNK, GLA_H, LANE, LANE), F32)],
        scratch_shapes=[pltpu.VMEM((GLA_H, LANE, LANE), F32)],
        compiler_params=_cparams(("arbitrary", "arbitrary")),
    )(*args)


def gla_bwd(p, wa2p, ba, hist, do, dprev, nb, t, reverse):
    tile = min(GLA_TILE, t)
    cpt = tile // GLA_CHUNK
    nt, rb_f = _gla_specs(nb, t, tile, reverse)
    rb = lambda b, j: rb_f(b, nt - 1 - j)
    has_prev = dprev is not None

    def body(*refs):
        if has_prev:
            q_ref, k_ref, v_ref, af_ref, w_ref, ba_ref, hist_ref, do_ref, prev_ref, dqkv_ref, dw_ref, dba_ref, ds_ref = refs
        else:
            q_ref, k_ref, v_ref, af_ref, w_ref, ba_ref, hist_ref, do_ref, dqkv_ref, dw_ref, dba_ref, ds_ref = refs

        @pl.when((pl.program_id(0) == 0) & (pl.program_id(1) == 0))
        def _():
            dw_ref[...] = jnp.zeros_like(dw_ref)
            dba_ref[...] = jnp.zeros_like(dba_ref)

        @pl.when(pl.program_id(1) == 0)
        def _():
            ds_ref[...] = jnp.zeros_like(ds_ref)

        def step(i, carry):
            ci = i if reverse else (cpt - 1 - i)
            rows = pl.ds(pl.multiple_of(ci * GLA_CHUNK, GLA_CHUNK), GLA_CHUNK)
            s_in = [hist_ref[0, ci, h] for h in range(GLA_H)]
            fn = functools.partial(_gla_chunk, reverse=reverse)
            q, k, v, w, ba = _gla_load(q_ref, k_ref, v_ref, w_ref, ba_ref, rows)
            _, vjp = jax.vjp(fn, q, k, v, af_ref[rows, :], w, ba, s_in)
            d_o = [do_ref[rows, h * GLA_DV:(h + 1) * GLA_DV] for h in range(GLA_H)]
            d_s = [ds_ref[h] for h in range(GLA_H)]
            dq, dk, dv, daf, dw, dba, ds_in = vjp((d_o, d_s))
            pieces = [(pr * LANE, dq[pr]) for pr in range(2)] + [(256 + pr * LANE, dk[pr]) for pr in range(2)]
            pieces += [(512 + h * GLA_DV, dv[h]) for h in range(GLA_H)] + [(1024, daf)]
            for c0, val in pieces:
                if has_prev:
                    val = val + prev_ref[rows, c0:c0 + LANE]
                dqkv_ref[rows, c0:c0 + LANE] = val
            for pr in range(2):
                dw_ref[:, pr * LANE:(pr + 1) * LANE] += dw[pr]
                dba_ref[:, pr * LANE:(pr + 1) * LANE] += dba[pr]
            for h in range(GLA_H):
                ds_ref[h] = ds_in[h]
            return carry

        lax.fori_loop(0, cpt, step, 0)

    hist_map_f = (lambda b, j: (b, nt - 1 - j, 0, 0, 0)) if reverse else (lambda b, j: (b, j, 0, 0, 0))
    hist_map = lambda b, j: hist_map_f(b, nt - 1 - j)
    in_specs = [pl.BlockSpec((tile, 256), lambda b, j: (rb(b, j), C_Q // 256)),
                pl.BlockSpec((tile, 256), lambda b, j: (rb(b, j), C_K // 256)),
                pl.BlockSpec((tile, 512), lambda b, j: (rb(b, j), C_V // 512)),
                pl.BlockSpec((tile, LANE), lambda b, j: (rb(b, j), C_AFAB // LANE)),
                pl.BlockSpec((LANE, 256), lambda b, j: (0, 0)), pl.BlockSpec((1, 256), lambda b, j: (0, 0)),
                pl.BlockSpec((1, cpt, GLA_H, LANE, LANE), hist_map),
                pl.BlockSpec((tile, 512), lambda b, j: (rb(b, j), 0))]
    args = [p, p, p, p, wa2p, ba, hist, do]
    if has_prev:
        in_specs.append(pl.BlockSpec((tile, 1152), lambda b, j: (rb(b, j), 0)))
        args.append(dprev)
    return pl.pallas_call(
        body, name="gla_bwd_rev" if reverse else "gla_bwd", grid=(nb, nt),
        in_specs=in_specs,
        out_specs=[pl.BlockSpec((tile, 1152), lambda b, j: (rb(b, j), 0)),
                   pl.BlockSpec((LANE, 256), lambda b, j: (0, 0)), pl.BlockSpec((1, 256), lambda b, j: (0, 0))],
        out_shape=[jax.ShapeDtypeStruct((nb * t, 1152), F32), jax.ShapeDtypeStruct((LANE, 256), F32),
                   jax.ShapeDtypeStruct((1, 256), F32)],
        scratch_shapes=[pltpu.VMEM((GLA_H, LANE, LANE), F32)],
        compiler_params=_cparams(("arbitrary", "arbitrary")),
    )(*args)


SCAN_TB = 8
RW_VH = RW_N // 2


def _bwd_lanes():
    lane = lax.broadcasted_iota(jnp.int32, (1, LANE), 1)
    return ((lane // (LANE // 4)) % 2) == 1


def _comm_specs(comm):
    anyspec = pl.BlockSpec(memory_space=pl.ANY)
    n = len(comm)
    shapes = [jax.ShapeDtypeStruct((N_DEV,) + (a.shape[1:] if sc else a.shape), a.dtype) for a, sc in comm]
    sems = [pltpu.SemaphoreType.DMA((n, N_DEV - 1)), pltpu.SemaphoreType.DMA((n, N_DEV - 1)), pltpu.SemaphoreType.DMA((n,))] if n else []
    return [a for a, _ in comm], [anyspec] * n, shapes, sems


def rwkv_scan_fwd(r, wf, wb, k, a, b, v, comm=()):
    t = r.shape[0]
    nt = t // SCAN_TB
    nc = len(comm)
    flags = [sc for _, sc in comm]

    def body(*refs):
        (rf, rm, kf, km, af, am, bf, bm, wf_ref, wb_ref, vf, vm), refs = refs[:12], refs[12:]
        c_in, refs = refs[:nc], refs[nc:]
        (yf_ref, ym_ref, hist_ref, sa_ref, fin_ref), refs = refs[:5], refs[5:]
        c_out, refs = refs[:nc], refs[nc:]
        s_ref, sems = refs[0], refs[1:]
        i = pl.program_id(0)
        if nc:
            start, wait = _exchange_plan(flags, c_in, c_out, *sems)

        @pl.when(i == 0)
        def _():
            s_ref[...] = jnp.zeros_like(s_ref)
            if nc:
                start()

        bwd = _bwd_lanes()

        def step(tt, carry):
            mt = SCAN_TB - 1 - tt
            pick = lambda f_ref, m_ref: jnp.where(bwd, m_ref[mt], f_ref[tt])
            rt, kt, at, bt, wt = pick(rf, rm), pick(kf, km), pick(af, am), pick(bf, bm), pick(wf_ref, wb_ref)
            for vi in range(RW_VH):
                sv = s_ref[vi]
                hist_ref[tt, vi] = sv
                sa = jnp.sum(sv * at, axis=0, keepdims=True)
                v_row = jnp.where(bwd, vm[mt, vi:vi + 1, :], vf[tt, vi:vi + 1, :])
                sn = sv * wt + sa * bt + v_row * kt
                s_ref[vi] = sn
                y_row = jnp.sum(sn * rt, axis=0, keepdims=True)
                yf_ref[tt, vi:vi + 1, :] = y_row
                ym_ref[mt, vi:vi + 1, :] = y_row
                sa_ref[tt, vi:vi + 1, :] = sa
            return carry

        lax.fori_loop(0, SCAN_TB, step, 0)

        @pl.when(i == nt - 1)
        def _():
            fin_ref[...] = s_ref[...]
            if nc:
                wait()

    fwd_map, mir_map = (lambda i: (i, 0, 0)), (lambda i: (nt - 1 - i, 0, 0))
    kf_spec, km_spec = pl.BlockSpec((SCAN_TB, RW_N, LANE), fwd_map), pl.BlockSpec((SCAN_TB, RW_N, LANE), mir_map)
    vf_spec, vm_spec = pl.BlockSpec((SCAN_TB, RW_VH, LANE), fwd_map), pl.BlockSpec((SCAN_TB, RW_VH, LANE), mir_map)
    c_args, c_specs, c_shapes, c_sems = _comm_specs(comm)
    vshape = jax.ShapeDtypeStruct((t, RW_VH, LANE), F32)
    return pl.pallas_call(
        body, name="rwkv_scan_fwd", grid=(nt,),
        in_specs=[kf_spec, km_spec] * 4 + [kf_spec, km_spec, vf_spec, vm_spec] + c_specs,
        out_specs=[vf_spec, vm_spec, pl.BlockSpec((SCAN_TB, RW_VH, RW_N, LANE), lambda i: (i, 0, 0, 0)), vf_spec,
                   pl.BlockSpec((RW_VH, RW_N, LANE), lambda i: (0, 0, 0))] + c_specs,
        out_shape=[vshape, vshape, jax.ShapeDtypeStruct((t, RW_VH, RW_N, LANE), F32), vshape,
                   jax.ShapeDtypeStruct((RW_VH, RW_N, LANE), F32)] + c_shapes,
        scratch_shapes=[pltpu.VMEM((RW_VH, RW_N, LANE), F32)] + c_sems,
        compiler_params=_cparams(("arbitrary",)),
    )(r, r, k, k, a, a, b, b, wf, wb, v, v, *c_args)


def rwkv_scan_bwd(r, wf, wb, k, a, b, v, hist, sa, fin, dy, comm=()):
    t = r.shape[0]
    nt = t // SCAN_TB
    nc = len(comm)
    flags = [sc for _, sc in comm]

    def body(*refs):
        (rf, rm, kf, km, af, am, bf, bm, wf_ref, wb_ref, vf, vm, hist_ref, sa_ref, fin_ref, dyf, dym), refs = refs[:17], refs[17:]
        c_in, refs = refs[:nc], refs[nc:]
        k_outs, (dvf_ref, dvm_ref), refs = refs[:10], refs[10:12], refs[12:]
        c_out, refs = refs[:nc], refs[nc:]
        ds_ref, snext_ref, sems = refs[0], refs[1], refs[2:]
        i = pl.program_id(0)
        if nc:
            start, wait = _exchange_plan(flags, c_in, c_out, *sems)

        @pl.when(i == 0)
        def _():
            ds_ref[...] = jnp.zeros_like(ds_ref)
            snext_ref[...] = fin_ref[...]
            if nc:
                start()

        bwd = _bwd_lanes()

        for tt in range(SCAN_TB - 1, -1, -1):
            mt = SCAN_TB - 1 - tt
            pick = lambda f_ref, m_ref: jnp.where(bwd, m_ref[mt], f_ref[tt])
            rt, kt, at, bt, wt = pick(rf, rm), pick(kf, km), pick(af, am), pick(bf, bm), pick(wf_ref, wb_ref)
            zero = jnp.zeros((RW_N, LANE), F32)
            dr, dw, dk, da, db = zero, zero, zero, zero, zero
            for vi in range(RW_VH):
                sv = hist_ref[tt, vi]
                sn = hist_ref[tt + 1, vi] if tt + 1 < SCAN_TB else snext_ref[vi]
                sa_row = sa_ref[tt, vi:vi + 1, :]
                v_row = jnp.where(bwd, vm[mt, vi:vi + 1, :], vf[tt, vi:vi + 1, :])
                dy_row = jnp.where(bwd, dym[mt, vi:vi + 1, :], dyf[tt, vi:vi + 1, :])
                dsv = ds_ref[vi] + dy_row * rt
                dr = dr + sn * dy_row
                dsa = jnp.sum(dsv * bt, axis=0, keepdims=True)
                dw = dw + sv * dsv
                db = db + dsv * sa_row
                dk = dk + dsv * v_row
                dv_row = jnp.sum(dsv * kt, axis=0, keepdims=True)
                dvf_ref[tt, vi:vi + 1, :] = dv_row
                dvm_ref[mt, vi:vi + 1, :] = dv_row
                da = da + sv * dsa
                ds_ref[vi] = dsv * wt + dsa * at
            for j, val in enumerate((dr, dw, dk, da, db)):
                val = val + pltpu.roll(val, LANE // 2, 1)
                k_outs[2 * j][tt] = val
                k_outs[2 * j + 1][mt] = val
        snext_ref[...] = hist_ref[0]

        if nc:
            @pl.when(i == nt - 1)
            def _():
                wait()

    fwd_map, mir_map = (lambda i: (nt - 1 - i, 0, 0)), (lambda i: (i, 0, 0))
    kf_spec, km_spec = pl.BlockSpec((SCAN_TB, RW_N, LANE), fwd_map), pl.BlockSpec((SCAN_TB, RW_N, LANE), mir_map)
    vf_spec, vm_spec = pl.BlockSpec((SCAN_TB, RW_VH, LANE), fwd_map), pl.BlockSpec((SCAN_TB, RW_VH, LANE), mir_map)
    state_spec = pl.BlockSpec((RW_VH, RW_N, LANE), lambda i: (0, 0, 0))
    c_args, c_specs, c_shapes, c_sems = _comm_specs(comm)
    kshape, vshape = jax.ShapeDtypeStruct((t, RW_N, LANE), F32), jax.ShapeDtypeStruct((t, RW_VH, LANE), F32)
    res = pl.pallas_call(
        body, name="rwkv_scan_bwd", grid=(nt,),
        in_specs=[kf_spec, km_spec] * 4 + [kf_spec, km_spec, vf_spec, vm_spec,
                                           pl.BlockSpec((SCAN_TB, RW_VH, RW_N, LANE), lambda i: (nt - 1 - i, 0, 0, 0)),
                                           vf_spec, state_spec, vf_spec, vm_spec] + c_specs,
        out_specs=[kf_spec, km_spec] * 5 + [vf_spec, vm_spec] + c_specs,
        out_shape=[kshape] * 10 + [vshape] * 2 + c_shapes,
        scratch_shapes=[pltpu.VMEM((RW_VH, RW_N, LANE), F32), pltpu.VMEM((RW_VH, RW_N, LANE), F32)] + c_sems,
        compiler_params=_cparams(("arbitrary",)),
    )(r, r, k, k, a, a, b, b, wf, wb, v, v, hist, sa, fin, dy, dy, *c_args)
    return res


RELAYOUT_TB = 128
RW_Q = LANE // 4


def to_scan(x, cb, nb, t, value):
    tb = min(RELAYOUT_TB, t)
    rows_out = RW_VH if value else RW_N

    def body(x_ref, o_ref, scr):
        for b in range(nb):
            scr[b * RW_H:(b + 1) * RW_H] = x_ref[b].T.reshape(RW_H, RW_N, tb)
        for j in range(rows_out):
            lo = scr[:, j, :]
            hi = scr[:, j + RW_VH, :] if value else lo
            o_ref[:, j, :] = jnp.concatenate([lo, lo, hi, hi], axis=0).T

    return pl.pallas_call(
        body, name="to_scan_v" if value else "to_scan_k", grid=(t // tb,),
        in_specs=[pl.BlockSpec((nb, tb, RW_W), lambda i: (0, i, cb))],
        out_specs=pl.BlockSpec((tb, rows_out, LANE), lambda i: (i, 0, 0)),
        out_shape=jax.ShapeDtypeStruct((t, rows_out, LANE), F32),
        scratch_shapes=[pltpu.VMEM((nb * RW_H, RW_N, tb), F32)],
        compiler_params=_cparams(("arbitrary",)),
    )(x.reshape(nb, t, x.shape[1]))


def from_scan(xf, xm, nb, t, value, split=False):
    tb = min(RELAYOUT_TB, t)
    rows_in = RW_VH if value else RW_N
    q = RW_Q

    def body(f_ref, m_ref, *rest):
        outs, scrs = rest[:-2] if split else rest[:-1], rest[-2:] if split else rest[-1:]
        for j in range(rows_in):
            a, b = f_ref[:, j, :].T, m_ref[:, j, :].T
            if value:
                parts = [(j, a[0:q], b[q:2 * q]), (j + RW_VH, a[2 * q:3 * q], b[3 * q:4 * q])]
            else:
                parts = [(j, a[0:q], b[q:2 * q])]
            for row, pf, pm in parts:
                if split:
                    scrs[0][:, row, :] = pf
                    scrs[1][:, row, :] = pm
                else:
                    scrs[0][:, row, :] = pf + pm
        for o_ref, scr in zip(outs, scrs):
            for b in range(nb):
                o_ref[b] = scr[b * RW_H:(b + 1) * RW_H].reshape(RW_W, tb).T

    n_out = 2 if split else 1
    res = pl.pallas_call(
        body, name=("from_scan_v" if value else "from_scan_k") + ("_split" if split else ""), grid=(t // tb,),
        in_specs=[pl.BlockSpec((tb, rows_in, LANE), lambda i: (i, 0, 0))] * 2,
        out_specs=[pl.BlockSpec((nb, tb, RW_W), lambda i: (0, i, 0))] * n_out,
        out_shape=[jax.ShapeDtypeStruct((nb, t, RW_W), F32)] * n_out,
        scratch_shapes=[pltpu.VMEM((nb * RW_H, RW_N, tb), F32)] * n_out,
        compiler_params=_cparams(("arbitrary",)),
    )(xf, xm)
    return [r.reshape(nb * t, RW_W) for r in res]


def f_norm(rows, params):
    (x,), (g,) = rows, params
    return [_rmsnorm(x, g)]


def f_rwkv_pre(rows, params):
    k, wlal, gl = rows
    w0f, w2f, w0b, w2b, a0, a2, g2, k_k, k_a = params
    seg = _segment_ones(RW_W, RW_N)
    tw = jnp.tanh(wlal)

    def decay(w0, w2):
        return jnp.exp(-jnp.exp(-_softplus(-(w0 + mm(tw, w2))) - 0.5))

    lr = _sigmoid(a0 + mm(wlal, a2))
    gate = mm(_sigmoid(gl), g2)
    kk = k * k_k
    kk = kk / jnp.maximum(jnp.sqrt(mm_exact(kk * kk, seg)), 1e-12)
    kp = k * (1.0 + (lr - 1.0) * k_a)
    return [decay(w0f, w2f), decay(w0b, w2b), kp, -kk, kk * lr, gate]


def f_branch_post(rows, params):
    o, og, y, r, kp, v, g = rows
    gla_g, ln_w, ln_b, r_k = params
    seg_gla = _segment_ones(GLA_H * GLA_DV, GLA_DV)
    seg_rw = _segment_ones(RW_W, RW_N)
    on = o * lax.rsqrt(mm_exact(o * o, seg_gla) * (1.0 / GLA_DV) + HEAD_NORM_EPS)
    oa = on * gla_g * _silu(og)
    mu = mm_exact(y, seg_rw) * (1.0 / RW_N)
    yc = y - mu
    var = mm_exact(yc * yc, seg_rw) * (1.0 / RW_N)
    yn = yc * lax.rsqrt(var + RW_GN_EPS) * ln_w + ln_b
    bonus = mm_exact(r * kp * r_k, seg_rw) * v
    return [oa, (yn + bonus) * g]


def f_merge(rows, params):
    ga, gb, ya, yb = rows
    return [_sigmoid(ga) * ya + _sigmoid(gb) * yb]


def f_norm2(rows, params):
    (x, mo), (g,) = rows, params
    x1 = x + mo
    return [x1, _rmsnorm(x1, g)]


def loss_head(x1, ffo, tgt, gf, tm):
    n = x1.shape[0]

    def body(x1_ref, f_ref, t_ref, g_ref, loss_ref, dx_ref, dg_ref):
        @pl.when(pl.program_id(0) == 0)
        def _():
            loss_ref[...] = jnp.zeros_like(loss_ref)
            dg_ref[...] = jnp.zeros_like(dg_ref)

        tgt_v = t_ref[...]

        def f(x2, g):
            err = _rmsnorm(x2, g) - tgt_v
            return jnp.sum(jnp.sum(err * err, axis=-1, keepdims=True), axis=0, keepdims=True) * (0.5 / D)

        val, vjp = jax.vjp(f, x1_ref[...] + f_ref[...], g_ref[...])
        dx, dg = vjp(jnp.ones((1, 1), F32))
        loss_ref[...] += val
        dx_ref[...] = dx
        dg_ref[...] += dg

    return pl.pallas_call(
        body, name="loss_head", grid=(n // tm,),
        in_specs=[_row_spec(tm, D, 0)] * 3 + [_full_spec((1, D))],
        out_specs=[_full_spec((1, 1)), _row_spec(tm, D, 0), _full_spec((1, D))],
        out_shape=[jax.ShapeDtypeStruct((1, 1), F32), jax.ShapeDtypeStruct((n, D), F32), jax.ShapeDtypeStruct((1, D), F32)],
        compiler_params=_cparams(("arbitrary",)),
    )(x1, ffo, tgt, gf)


def _pad_cols(a, width):
    return jnp.pad(a, ((0, 0), (0, width - a.shape[1])))


def w_in_to_padded(w):
    return _pad_cols(jnp.concatenate([w[:, 3360:5408], w[:, 0:1536], w[:, 1568:3360], w[:, 1536:1568]], axis=1), NP)


def w_in_from_padded(wp):
    return jnp.concatenate([wp[:, 2048:3584], wp[:, 5376:5408], wp[:, 3584:5376], wp[:, 0:2048]], axis=1)


def ff_interleave(a):
    r = a.shape[0]
    halves = jnp.stack([_pad_cols(a[:, :D_FF], FFP), _pad_cols(a[:, D_FF:], FFP)], axis=1)
    return halves.reshape(r, 2, FFP // LANE, LANE).transpose(0, 2, 1, 3).reshape(r, 2 * FFP)


def ff_deinterleave(a):
    r = a.shape[0]
    halves = a.reshape(r, FFP // LANE, 2, LANE).transpose(0, 2, 1, 3).reshape(r, 2, FFP)
    return halves[:, :, :D_FF].reshape(r, 2 * D_FF)


def _rows_into(w, rows, off):
    return jnp.zeros((rows, w.shape[1]), w.dtype).at[off:off + w.shape[0]].set(w)


LATE = ("gla_proj", "rwkv_proj", "w_out", "ffn_up", "ffn_conv_w", "ffn_down")


def local_step(x, tgt, w, nb, t, late_blocks=None):
    n = nb * t
    tm = min(n, 1024)
    tr = min(n, 256)
    vec = lambda a: a.reshape(1, -1)
    w = dict(w)

    w_in_p = w_in_to_padded(w["w_in"])
    wa2_f, wa2_b = _rows_into(w["gla_wa2_f"], LANE, 0), _rows_into(w["gla_wa2_b"], LANE, GLA_RANK)
    w2f, w2b = _rows_into(w["rwkv_w2_f"], LANE, 0), _rows_into(w["rwkv_w2_b"], LANE, 0)
    a2 = _rows_into(w["rwkv_a2"], LANE, 64)
    g1, g2n, gf = vec(w["norm1_g"]), vec(w["norm2_g"]), vec(w["norm_f_g"])
    mu_prev, mu_next = vec(w["rwkv_mu_prev"]), vec(w["rwkv_mu_next"])
    pre_params = [vec(w["rwkv_w0_f"]), w2f, vec(w["rwkv_w0_b"]), w2b, vec(w["rwkv_a0"]), a2, w["rwkv_g2"],
                  vec(w["rwkv_k_k"]), vec(w["rwkv_k_a"])]
    post_params = [vec(w["gla_norm_g"]), vec(w["rwkv_ln_w"]), vec(w["rwkv_ln_b"]), vec(w["rwkv_r_k"])]
    ba_f, ba_b = vec(w["gla_ba_f"]), vec(w["gla_ba_b"])

    (h1,) = rowwise_fwd("norm1_fwd", f_norm, [(x, D, 0)], [g1], [(D, MXU_DTYPE)], tr)
    p = matmul("proj_in", h1, w_in_p, "nn", F32, tm, 512, D)
    s = shift_fwd(p, mu_prev, mu_next, nb, t)
    pre_rows = [(s, 512, 1), (s, LANE, 1536 // LANE), (s, LANE, 1664 // LANE)]
    wf, wb, kp, a_s, b_s, g = rowwise_fwd("rwkv_pre_fwd", f_rwkv_pre, pre_rows, pre_params, [(RW_W, F32)] * 6, tr)
    sc = [to_scan(s, 0, nb, t, False), to_scan(wf, 0, nb, t, False), to_scan(wb, 0, nb, t, False), to_scan(kp, 0, nb, t, False),
          to_scan(a_s, 0, nb, t, False), to_scan(b_s, 0, nb, t, False), to_scan(s, 2, nb, t, True)]
    comm = [] if late_blocks is None else [(late_blocks[k], False) for k in LATE]
    y_scf, y_scm, hist_rw, sa_sc, fin_rw, *gathered = rwkv_scan_fwd(*sc, comm=comm)
    for k, g_k in zip(LATE, gathered):
        w[k] = _gathered_to_full(g_k, SHARDED[k])
    ffn_up_p = ff_interleave(w["ffn_up"])
    conv_w_p, conv_b_p = ff_interleave(w["ffn_conv_w"]), ff_interleave(vec(w["ffn_conv_b"]))
    ffn_down_p = jnp.pad(w["ffn_down"], ((0, FFP - D_FF), (0, 0)))
    (y,) = from_scan(y_scf, y_scm, nb, t, True)
    o_f, hist_f = gla_fwd(p, wa2_f, ba_f, None, nb, t, False)
    o, hist_b = gla_fwd(p, wa2_b, ba_b, o_f, nb, t, True)
    post_rows = [(o, 512, 0), (p, 512, C_OG // 512), (y, 512, 0), (s, 512, 0), (kp, 512, 0), (s, 512, 2), (g, 512, 0)]
    oa, ob = rowwise_fwd("branch_post_fwd", f_branch_post, post_rows, post_params, [(512, MXU_DTYPE)] * 2, tr)
    ya = matmul("gla_proj", oa, w["gla_proj"], "nn", F32, tm, 512, 512)
    yb = matmul("rwkv_proj", ob, w["rwkv_proj"], "nn", F32, tm, 512, 512)
    merge_rows = [(p, D, 0), (p, D, 1), (ya, D, 0), (yb, D, 0)]
    (merged,) = rowwise_fwd("merge_fwd", f_merge, merge_rows, [], [(D, MXU_DTYPE)], tr)
    mo = matmul("w_out", merged, w["w_out"], "nn", F32, tm, 512, D)
    x1, h2 = rowwise_fwd("norm2_fwd", f_norm2, [(x, D, 0), (mo, D, 0)], [g2n], [(D, F32), (D, MXU_DTYPE)], tr)
    u = matmul("ffn_up", h2, ffn_up_p, "nn", F32, tm, 512, D)
    z = conv_glu_fwd(u, conv_w_p, conv_b_p, nb, t)
    ffo = matmul("ffn_down", z, ffn_down_p, "nn", F32, tm, 512, FFP // 2)
    loss, dx2, dgf = loss_head(x1, ffo, tgt, gf, tr)

    dz = matmul("ffn_down_dx", dx2, ffn_down_p, "nt", F32, tm, FFP // 2, D)
    d_ffn_down_p = matmul("ffn_down_dw", z, dx2, "tn", F32, FFP // 2, 512, tm)
    du, d_conv_w_p, d_conv_b_p = conv_glu_bwd(u, dz, conv_w_p, conv_b_p, nb, t)
    dh2 = matmul("ffn_up_dx", du, ffn_up_p, "nt", F32, tm, 512, 512)
    d_ffn_up_p = matmul("ffn_up_dw", h2, du, "tn", F32, 512, 512, tm)
    (dx1,), (dg2,) = rowwise_bwd("norm2_bwd", f_norm2, [(x, D, 0), (mo, D, 0)], [g2n],
                                 [[(dx2, D, 0)], [(dh2, D, 0)]], tr, grad_rows=[1])
    dmerged = matmul("w_out_dx", dx1, w["w_out"], "nt", F32, tm, 512, D)
    d_w_out = matmul("w_out_dw", merged, dx1, "tn", F32, 512, 512, tm)
    (dga, dgb, dya, dyb), _ = rowwise_bwd("merge_bwd", f_merge, merge_rows, [], [[(dmerged, D, 0)]], tr)
    d_oa = matmul("gla_proj_dx", dya, w["gla_proj"], "nt", F32, tm, 512, D)
    d_gla_proj = matmul("gla_proj_dw", oa, dya, "tn", F32, 512, 512, tm)
    d_ob = matmul("rwkv_proj_dx", dyb, w["rwkv_proj"], "nt", F32, tm, 512, D)
    d_rwkv_proj = matmul("rwkv_proj_dw", ob, dyb, "tn", F32, 512, 512, tm)
    (d_o, d_og, d_y, d_r_post, d_kp_post, d_v_post, d_g), d_post = rowwise_bwd(
        "branch_post_bwd", f_branch_post, post_rows, post_params, [[(d_oa, 512, 0)], [(d_ob, 512, 0)]], tr)
    late_grads = {"gla_proj": d_gla_proj, "rwkv_proj": d_rwkv_proj, "w_out": d_w_out, "ffn_up": ff_deinterleave(d_ffn_up_p),
                  "ffn_conv_w": ff_deinterleave(d_conv_w_p), "ffn_down": d_ffn_down_p[0:D_FF]}
    comm = [] if late_blocks is None else [(_full_to_slices(late_grads[k], SHARDED[k]), True) for k in LATE]
    dsc = rwkv_scan_bwd(*sc, hist_rw, sa_sc, fin_rw, to_scan(d_y, 0, nb, t, True), comm=comm)
    received = dict(zip(LATE, dsc[12:]))
    (d_r_scan,) = from_scan(dsc[0], dsc[1], nb, t, False)
    d_wf, d_wb = from_scan(dsc[2], dsc[3], nb, t, False, split=True)
    (d_kp_scan,) = from_scan(dsc[4], dsc[5], nb, t, False)
    (d_a_scan,) = from_scan(dsc[6], dsc[7], nb, t, False)
    (d_b_scan,) = from_scan(dsc[8], dsc[9], nb, t, False)
    (d_v_scan,) = from_scan(dsc[10], dsc[11], nb, t, True)
    (d_k, d_wlal, d_gl), d_pre = rowwise_bwd(
        "rwkv_pre_bwd", f_rwkv_pre, pre_rows, pre_params,
        [[(d_wf, 512, 0)], [(d_wb, 512, 0)], [(d_kp_scan, 512, 0), (d_kp_post, 512, 0)],
         [(d_a_scan, 512, 0)], [(d_b_scan, 512, 0)], [(d_g, 512, 0)]], tr)
    ds = jnp.concatenate([d_r_scan + d_r_post, d_k, d_v_scan + d_v_post, d_wlal, d_gl], axis=1)
    dp_rw, d_mu_prev, d_mu_next = shift_bwd(p, ds, mu_prev, mu_next, nb, t)
    dqkv_f, d_wa2_f, d_ba_f = gla_bwd(p, wa2_f, ba_f, hist_f, d_o, None, nb, t, False)
    dqkv, d_wa2_b, d_ba_b = gla_bwd(p, wa2_b, ba_b, hist_b, d_o, dqkv_f, nb, t, True)
    dp = jnp.concatenate([dga, dgb, dqkv[:, 0:1024], d_og, dp_rw, dqkv[:, 1024:1152],
                          jnp.zeros((n, NP - C_AFAB - LANE), F32)], axis=1)
    dh1 = matmul("proj_in_dx", dp, w_in_p, "nt", F32, tm, 512, 512)
    d_w_in_p = matmul("proj_in_dw", h1, dp, "tn", F32, 512, 512, tm)
    (grad_x,), (dg1,) = rowwise_bwd("norm1_bwd", f_norm, [(x, D, 0)], [g1], [[(dh1, D, 0)]], tr, adds=[(0, (dx1, D, 0))])

    grads = {
        "norm1_g": dg1, "w_in": w_in_from_padded(d_w_in_p),
        "gla_wa2_f": d_wa2_f[0:GLA_RANK], "gla_ba_f": d_ba_f, "gla_wa2_b": d_wa2_b[GLA_RANK:2 * GLA_RANK], "gla_ba_b": d_ba_b,
        "gla_norm_g": d_post[0], "rwkv_mu_prev": d_mu_prev, "rwkv_mu_next": d_mu_next,
        "rwkv_w0_f": d_pre[0], "rwkv_w2_f": d_pre[1][0:64], "rwkv_w0_b": d_pre[2], "rwkv_w2_b": d_pre[3][0:64],
        "rwkv_a0": d_pre[4], "rwkv_a2": d_pre[5][64:128], "rwkv_g2": d_pre[6], "rwkv_k_k": d_pre[7], "rwkv_k_a": d_pre[8],
        "rwkv_r_k": d_post[3], "rwkv_ln_w": d_post[1], "rwkv_ln_b": d_post[2],
        "norm2_g": dg2, "ffn_conv_b": ff_deinterleave(d_conv_b_p), "norm_f_g": dgf, **late_grads,
    }
    return loss, grad_x, grads, received


MESH = pl.DeviceIdType.MESH


def remote_exchange(name, items):
    n = len(items)

    def body(*refs):
        start, wait = _exchange_plan([sc for _, sc in items], refs[:n], refs[n:2 * n], *refs[2 * n:])
        start()
        wait()

    args, specs, shapes, sems = _comm_specs(items)
    return pl.pallas_call(body, name=name, in_specs=specs, out_specs=specs, out_shape=shapes, scratch_shapes=sems)(*args)


def _exchange_plan(flags, in_refs, out_refs, send_sems, recv_sems, local_sems):
    x, y, c = lax.axis_index("x"), lax.axis_index("y"), lax.axis_index("c")
    me = 4 * x + 2 * y + c

    def peer(k):
        px = 1 - x if (k >> 2) & 1 else x
        py = 1 - y if (k >> 1) & 1 else y
        pc = 1 - c if k & 1 else c
        return (px, py, pc), 4 * px + 2 * py + pc

    def copies():
        own, sends, recvs = [], [], []
        for i, scatter in enumerate(flags):
            src = in_refs[i].at[me] if scatter else in_refs[i]
            own.append(pltpu.make_async_copy(src, out_refs[i].at[me], local_sems.at[i]))
        for k in range(1, N_DEV):
            dev, slot = peer(k)
            for i, scatter in enumerate(flags):
                src = in_refs[i].at[slot] if scatter else in_refs[i]
                pair = dict(send_sem=send_sems.at[i, k - 1], recv_sem=recv_sems.at[i, k - 1], device_id=dev, device_id_type=MESH)
                sends.append(pltpu.make_async_remote_copy(src_ref=src, dst_ref=out_refs[i].at[me], **pair))
                recvs.append(pltpu.make_async_remote_copy(src_ref=out_refs[i].at[slot], dst_ref=out_refs[i].at[slot], **pair))
        return own, sends, recvs

    def start():
        own, sends, _ = copies()
        for cp in own + sends:
            cp.start()

    def wait():
        own, sends, recvs = copies()
        for send, recv in zip(sends, recvs):
            recv.wait_recv()
            send.wait_send()
        for cp in own:
            cp.wait()

    return start, wait


def _adam_tiles(r, c):
    tc = 256 if (c % 256 == 0 and r * c > 128 * 1024) else c
    tr = 128 if (r % 128 == 0 and r > 128) else r
    return tr, tc


def adamw_reduce(name, parts, w, m, v):
    r, c = w.shape
    tr, tc = _adam_tiles(r, c)

    def body(p_ref, w_ref, m_ref, v_ref, g_ref, d_ref, nm_ref, nv_ref):
        g = p_ref[0]
        for d in range(1, N_DEV):
            g = g + p_ref[d]
        nm = ADAM_B1 * m_ref[...] + (1.0 - ADAM_B1) * g
        nv = ADAM_B2 * v_ref[...] + (1.0 - ADAM_B2) * (g * g)
        m_hat = nm / (1.0 - ADAM_B1 ** ADAM_STEP)
        v_hat = nv / (1.0 - ADAM_B2 ** ADAM_STEP)
        g_ref[...] = g
        d_ref[...] = -ADAM_LR * (m_hat / (jnp.sqrt(v_hat) + ADAM_EPS) + ADAM_WD * w_ref[...])
        nm_ref[...] = nm
        nv_ref[...] = nv

    spec = pl.BlockSpec((tr, tc), lambda i, j: (i, j))
    return pl.pallas_call(
        body, name=name, grid=(r // tr, c // tc),
        in_specs=[pl.BlockSpec((N_DEV, tr, tc), lambda i, j: (0, i, j)), spec, spec, spec],
        out_specs=[spec] * 4, out_shape=[jax.ShapeDtypeStruct((r, c), F32)] * 4,
        compiler_params=_cparams(("arbitrary", "arbitrary")),
    )(parts, w, m, v)


SHARDED = {"w_in": 1, "gla_wa2_f": 1, "gla_wa2_b": 1, "gla_proj": 1, "rwkv_w2_f": 1, "rwkv_w2_b": 1, "rwkv_a2": 1,
           "rwkv_g2": 1, "rwkv_proj": 1, "w_out": 0, "ffn_up": 1, "ffn_conv_w": 1, "ffn_down": 0}
BF16_GATHER = ("w_in", "gla_proj", "rwkv_proj", "w_out", "ffn_up", "ffn_down")
REPLICATED = ("norm1_g", "gla_ba_f", "gla_ba_b", "gla_norm_g", "rwkv_mu_prev", "rwkv_mu_next", "rwkv_w0_f", "rwkv_w0_b",
              "rwkv_a0", "rwkv_k_k", "rwkv_k_a", "rwkv_r_k", "rwkv_ln_w", "rwkv_ln_b", "norm2_g", "ffn_conv_b", "norm_f_g")
WEIGHTS = ("norm1_g", "w_in", "gla_wa2_f", "gla_ba_f", "gla_wa2_b", "gla_ba_b", "gla_norm_g", "gla_proj", "rwkv_mu_prev",
           "rwkv_mu_next", "rwkv_w0_f", "rwkv_w2_f", "rwkv_w0_b", "rwkv_w2_b", "rwkv_a0", "rwkv_a2", "rwkv_g2", "rwkv_k_k",
           "rwkv_k_a", "rwkv_r_k", "rwkv_ln_w", "rwkv_ln_b", "rwkv_proj", "w_out", "norm2_g", "ffn_up", "ffn_conv_w",
           "ffn_conv_b", "ffn_down", "norm_f_g")
PACK_ROWS = 144


def _gathered_to_full(g, axis):
    if axis == 0:
        return g.reshape(N_DEV * g.shape[1], g.shape[2])
    return g.transpose(1, 0, 2).reshape(g.shape[1], N_DEV * g.shape[2])


def _full_to_slices(a, axis):
    if axis == 0:
        return a.reshape(N_DEV, a.shape[0] // N_DEV, a.shape[1])
    return a.reshape(a.shape[0], N_DEV, a.shape[1] // N_DEV).transpose(1, 0, 2)


def _pack(d):
    flat = jnp.concatenate([d[k].reshape(-1).astype(F32) for k in REPLICATED])
    return jnp.pad(flat, (0, PACK_ROWS * LANE - flat.shape[0])).reshape(PACK_ROWS, LANE)


def _unpack(packed, shapes):
    flat, out, pos = packed.reshape(-1), {}, 0
    for k in REPLICATED:
        size = int(np.prod(shapes[k]))
        out[k] = flat[pos:pos + size].reshape(shapes[k])
        pos += size
    return out


def kernel(x, norm1_g, w_in, gla_wa2_f, gla_ba_f, gla_wa2_b, gla_ba_b, gla_norm_g, gla_proj, rwkv_mu_prev, rwkv_mu_next, rwkv_w0_f, rwkv_w2_f, rwkv_w0_b, rwkv_w2_b, rwkv_a0, rwkv_a2, rwkv_g2, rwkv_k_k, rwkv_k_a, rwkv_r_k, rwkv_ln_w, rwkv_ln_b, rwkv_proj, w_out, norm2_g, ffn_up, ffn_conv_w, ffn_conv_b, ffn_down, norm_f_g, loss_target, m_norm1_g, m_w_in, m_gla_wa2_f, m_gla_ba_f, m_gla_wa2_b, m_gla_ba_b, m_gla_norm_g, m_gla_proj, m_rwkv_mu_prev, m_rwkv_mu_next, m_rwkv_w0_f, m_rwkv_w2_f, m_rwkv_w0_b, m_rwkv_w2_b, m_rwkv_a0, m_rwkv_a2, m_rwkv_g2, m_rwkv_k_k, m_rwkv_k_a, m_rwkv_r_k, m_rwkv_ln_w, m_rwkv_ln_b, m_rwkv_proj, m_w_out, m_norm2_g, m_ffn_up, m_ffn_conv_w, m_ffn_conv_b, m_ffn_down, m_norm_f_g, v_norm1_g, v_w_in, v_gla_wa2_f, v_gla_ba_f, v_gla_wa2_b, v_gla_ba_b, v_gla_norm_g, v_gla_proj, v_rwkv_mu_prev, v_rwkv_mu_next, v_rwkv_w0_f, v_rwkv_w2_f, v_rwkv_w0_b, v_rwkv_w2_b, v_rwkv_a0, v_rwkv_a2, v_rwkv_g2, v_rwkv_k_k, v_rwkv_k_a, v_rwkv_r_k, v_rwkv_ln_w, v_rwkv_ln_b, v_rwkv_proj, v_w_out, v_norm2_g, v_ffn_up, v_ffn_conv_w, v_ffn_conv_b, v_ffn_down, v_norm_f_g):
    args = locals()
    wts = {k: args[k] for k in WEIGHTS}
    mom = {k: args["m_" + k] for k in WEIGHTS}
    var = {k: args["v_" + k] for k in WEIGHTS}
    shapes = {k: wts[k].shape for k in WEIGHTS}
    nb, t = x.shape[0], x.shape[1]
    mat = lambda a: a.reshape(a.shape[-2], a.shape[-1])

    block = lambda k: mat(wts[k]).astype(MXU_DTYPE) if k in BF16_GATHER else mat(wts[k])
    early = [k for k in SHARDED if k not in LATE]
    gathered = remote_exchange("gather_weights", [(block(k), False) for k in early])
    full = {k: _gathered_to_full(g, SHARDED[k]) for k, g in zip(early, gathered)}
    for k in REPLICATED:
        full[k] = wts[k].reshape(-1) if k in ("norm_f_g", "rwkv_r_k") else wts[k][0]

    loss, grad_x, grads, received = local_step(x.reshape(nb * t, D), loss_target.reshape(nb * t, D), full, nb, t,
                                               late_blocks={k: block(k) for k in LATE})

    items = [(_full_to_slices(grads[k], SHARDED[k]), True) for k in early] + [(_pack(grads), False)]
    rest = remote_exchange("exchange_grads", items)
    received.update(zip(early, rest[:-1]))

    res = {}
    for k in SHARDED:
        outs = adamw_reduce("adamw_" + k, received[k], mat(wts[k]), mat(mom[k]), mat(var[k]))
        res[k] = [o.reshape(shapes[k]) for o in outs]
    packed = adamw_reduce("adamw_replicated", rest[-1], _pack(wts), _pack(mom), _pack(var))
    unpacked = [_unpack(p, shapes) for p in packed]
    for k in REPLICATED:
        res[k] = [u[k] for u in unpacked]

    total = lax.psum(loss[0, 0], ("x", "y", "c"))
    out = [total, grad_x.reshape(x.shape)]
    for j in range(4):
        out += [res[k][j] for k in WEIGHTS]
    return tuple(out)
```

```python
import functools

import jax
import jax.numpy as jnp
import numpy as np
from jax import lax
from jax.experimental import pallas as pl
from jax.experimental.pallas import tpu as pltpu

F32 = jnp.float32
MXU_DTYPE = jnp.bfloat16

D = 1024
SEQ = 2048
GLA_H, GLA_DK, GLA_DV, GLA_CHUNK = 4, 64, 128, 64
GLA_RANK = 16
GLA_LOGIT_NORM = 16.0
RW_H, RW_N = 8, 64
RW_W = 512
D_FF = 2752
NORM_EPS = 1e-6
HEAD_NORM_EPS = 1e-5
RW_GN_EPS = RW_N * 1e-5
N_DEV = 8
ADAM_LR, ADAM_B1, ADAM_B2, ADAM_EPS, ADAM_WD, ADAM_STEP = 0.001, 0.9, 0.999, 1e-08, 0.01, 10

C_GA, C_GB, C_Q, C_K, C_V, C_OG = 0, 1024, 2048, 2304, 2560, 3072
C_RW = 3584
C_R, C_RK, C_RV, C_WLAL, C_GL = 3584, 4096, 4608, 5120, 5248
C_AFAB = 5376
NP = 5632
RW_PW = 1792
FFP = 2816
LANE = 128
VMEM_LIMIT = 56 * 1024 * 1024


def _cparams(sem):
    return pltpu.CompilerParams(dimension_semantics=sem, vmem_limit_bytes=VMEM_LIMIT)


@jax.custom_vjp
def mm(a, b):
    return jnp.dot(a.astype(MXU_DTYPE), b.astype(MXU_DTYPE), preferred_element_type=F32)


def _mm_fwd(a, b):
    return mm(a, b), (a, b)


def _mm_bwd(res, g):
    a, b = res
    gb = g.astype(MXU_DTYPE)
    da = lax.dot_general(gb, b.astype(MXU_DTYPE), (((1,), (1,)), ((), ())), preferred_element_type=F32)
    db = lax.dot_general(a.astype(MXU_DTYPE), gb, (((0,), (0,)), ((), ())), preferred_element_type=F32)
    return da.astype(a.dtype), db.astype(b.dtype)


mm.defvjp(_mm_fwd, _mm_bwd)


@jax.custom_vjp
def mm_nt(a, b):
    return lax.dot_general(a.astype(MXU_DTYPE), b.astype(MXU_DTYPE), (((1,), (1,)), ((), ())), preferred_element_type=F32)


def _mm_nt_fwd(a, b):
    return mm_nt(a, b), (a, b)


def _mm_nt_bwd(res, g):
    a, b = res
    gb = g.astype(MXU_DTYPE)
    da = jnp.dot(gb, b.astype(MXU_DTYPE), preferred_element_type=F32)
    db = lax.dot_general(gb, a.astype(MXU_DTYPE), (((0,), (0,)), ((), ())), preferred_element_type=F32)
    return da.astype(a.dtype), db.astype(b.dtype)


mm_nt.defvjp(_mm_nt_fwd, _mm_nt_bwd)


@jax.custom_vjp
def mm_tn(a, b):
    return lax.dot_general(a.astype(MXU_DTYPE), b.astype(MXU_DTYPE), (((0,), (0,)), ((), ())), preferred_element_type=F32)


def _mm_tn_fwd(a, b):
    return mm_tn(a, b), (a, b)


def _mm_tn_bwd(res, g):
    a, b = res
    gb = g.astype(MXU_DTYPE)
    da = lax.dot_general(b.astype(MXU_DTYPE), gb, (((1,), (1,)), ((), ())), preferred_element_type=F32)
    db = jnp.dot(a.astype(MXU_DTYPE), gb, preferred_element_type=F32)
    return da.astype(a.dtype), db.astype(b.dtype)


mm_tn.defvjp(_mm_tn_fwd, _mm_tn_bwd)


@functools.partial(jax.custom_vjp, nondiff_argnums=(2, 3))
def sel_dot(x, s, dims, x_first):
    sb = s.astype(MXU_DTYPE)
    hi = x.astype(MXU_DTYPE)
    r1 = x - hi.astype(F32)
    mid = r1.astype(MXU_DTYPE)
    lo = (r1 - mid.astype(F32)).astype(MXU_DTYPE)
    out = None
    for part in (hi, mid, lo):
        ops = (part, sb) if x_first else (sb, part)
        d = lax.dot_general(*ops, (dims, ((), ())), preferred_element_type=F32)
        out = d if out is None else out + d
    return out


def _sel_dot_fwd(x, s, dims, x_first):
    return sel_dot(x, s, dims, x_first), s


def _sel_dot_bwd(dims, x_first, s, g):
    if x_first:
        (cx,), (cs,) = dims
        dx = sel_dot(g, s, ((1,), (1 - cs,)), True) if cx == 1 else sel_dot(g, s, ((1 - cs,), (1,)), False)
    else:
        (cs,), (cx,) = dims
        dx = sel_dot(g, s, ((1 - cs,), (0,)), False) if cx == 0 else sel_dot(g, s, ((0,), (1 - cs,)), True)
    return dx, jnp.zeros_like(s)


sel_dot.defvjp(_sel_dot_fwd, _sel_dot_bwd)


def mm_exact(a, b, b_is_01=True):
    return sel_dot(a, b, ((1,), (0,)), True) if b_is_01 else sel_dot(b, a, ((1,), (0,)), False)


def mm_tn_exact(a, b):
    return sel_dot(a, b, ((0,), (0,)), True)


def _softplus(x):
    return jnp.maximum(x, 0.0) + jnp.log(1.0 + jnp.exp(-jnp.abs(x)))


def _sigmoid(x):
    return jax.nn.sigmoid(x)


def _silu(x):
    return x * _sigmoid(x)


def _rmsnorm(x, g):
    return x * lax.rsqrt(jnp.mean(x * x, axis=-1, keepdims=True) + NORM_EPS) * g


def _segment_ones(width, seg):
    i = lax.broadcasted_iota(jnp.int32, (width, width), 0) // seg
    j = lax.broadcasted_iota(jnp.int32, (width, width), 1) // seg
    return (i == j).astype(F32)


def _row_spec(tm, width, cb):
    return pl.BlockSpec((tm, width), lambda i: (i, cb))


def _full_spec(shape):
    nd = len(shape)
    return pl.BlockSpec(tuple(shape), lambda i: (0,) * nd)


def rowwise_fwd(name, f, rows, params, outs, tm):
    n = rows[0][0].shape[0]
    nr, npar = len(rows), len(params)

    def body(*refs):
        rv = [r[...] for r in refs[:nr]]
        pv = [r[...] for r in refs[nr:nr + npar]]
        res = f(rv, pv)
        for o_ref, val in zip(refs[nr + npar:], res):
            o_ref[...] = val.astype(o_ref.dtype)

    return pl.pallas_call(
        body, name=name, grid=(n // tm,),
        in_specs=[_row_spec(tm, w, cb) for _, w, cb in rows] + [_full_spec(p.shape) for p in params],
        out_specs=[_row_spec(tm, w, 0) for w, _ in outs],
        out_shape=[jax.ShapeDtypeStruct((n, w), dt) for w, dt in outs],
        compiler_params=_cparams(("arbitrary",)),
    )(*[a for a, _, _ in rows], *params)


def rowwise_bwd(name, f, rows, params, douts, tm, adds=(), grad_rows=None):
    n = rows[0][0].shape[0]
    nr, npar = len(rows), len(params)
    grad_rows = list(range(nr)) if grad_rows is None else list(grad_rows)
    flat_d = [d for group in douts for d in group]
    nd, na, ng = len(flat_d), len(adds), len(grad_rows)

    def body(*refs):
        rv = [r[...] for r in refs[:nr]]
        pv = [r[...] for r in refs[nr:nr + npar]]
        dflat = [r[...].astype(F32) for r in refs[nr + npar:nr + npar + nd]]
        av = [r[...] for r in refs[nr + npar + nd:nr + npar + nd + na]]
        o = nr + npar + nd + na
        drow_refs, dpar_refs = refs[o:o + ng], refs[o + ng:o + ng + npar]
        dv, pos = [], 0
        for group in douts:
            dv.append(sum(dflat[pos + 1:pos + len(group)], dflat[pos]))
            pos += len(group)

        @pl.when(pl.program_id(0) == 0)
        def _():
            for r in dpar_refs:
                r[...] = jnp.zeros_like(r)

        def g(grows, pars):
            full = list(rv)
            for i, val in zip(grad_rows, grows):
                full[i] = val
            return f(full, pars)

        res, vjp = jax.vjp(g, [rv[i] for i in grad_rows], pv)
        drows, dpars = vjp([d.astype(r.dtype) for d, r in zip(dv, res)])
        drows = [d.astype(F32) for d in drows]
        for (idx, _), a in zip(adds, av):
            drows[idx] = drows[idx] + a.astype(F32)
        for r, d in zip(drow_refs, drows):
            r[...] = d
        for r, d in zip(dpar_refs, dpars):
            r[...] += d.astype(F32)

    res = pl.pallas_call(
        body, name=name, grid=(n // tm,),
        in_specs=[_row_spec(tm, w, cb) for _, w, cb in rows] + [_full_spec(p.shape) for p in params]
        + [_row_spec(tm, w, cb) for _, w, cb in flat_d] + [_row_spec(tm, w, cb) for _, (_, w, cb) in adds],
        out_specs=[_row_spec(tm, rows[i][1], 0) for i in grad_rows] + [_full_spec(p.shape) for p in params],
        out_shape=[jax.ShapeDtypeStruct((n, rows[i][1]), F32) for i in grad_rows]
        + [jax.ShapeDtypeStruct(p.shape, F32) for p in params],
        compiler_params=_cparams(("arbitrary",)),
    )(*[a for a, _, _ in rows], *params, *[a for a, _, _ in flat_d], *[a for _, (a, _, _) in adds])
    return res[:ng], res[ng:]


def matmul(name, a, b, mode, out_dtype, tm, tn, tk, comm=()):
    nc = len(comm)
    flags = [sc for _, sc in comm]
    if mode == "nn":
        (m, k), n = a.shape, b.shape[1]
        a_spec = pl.BlockSpec((tm, tk), lambda i, j, kk: (i, kk))
        b_spec = pl.BlockSpec((tk, tn), lambda i, j, kk: (kk, j))
        dims = (((1,), (0,)), ((), ()))
    elif mode == "nt":
        (m, k), n = a.shape, b.shape[0]
        a_spec = pl.BlockSpec((tm, tk), lambda i, j, kk: (i, kk))
        b_spec = pl.BlockSpec((tn, tk), lambda i, j, kk: (j, kk))
        dims = (((1,), (1,)), ((), ()))
    else:
        (k, m), n = a.shape, b.shape[1]
        a_spec = pl.BlockSpec((tk, tm), lambda i, j, kk: (kk, i))
        b_spec = pl.BlockSpec((tk, tn), lambda i, j, kk: (kk, j))
        dims = (((0,), (0,)), ((), ()))
    assert m % tm == 0 and n % tn == 0 and k % tk == 0, (name, a.shape, b.shape, tm, tn, tk)
    nk = k // tk
    grid = (m // tm, n // tn, nk)

    def body(*refs):
        a_ref, b_ref, c_in, o_ref = refs[0], refs[1], refs[2:2 + nc], refs[2 + nc]
        c_out, acc_ref, sems = refs[3 + nc:3 + 2 * nc], refs[3 + 2 * nc], refs[4 + 2 * nc:]
        kk = pl.program_id(2)
        step = (pl.program_id(0) * grid[1] + pl.program_id(1)) * nk + kk
        if nc:
            start, wait = _exchange_plan(flags, c_in, c_out, *sems)

            @pl.when(step == 0)
            def _():
                start()

        @pl.when(kk == 0)
        def _():
            acc_ref[...] = jnp.zeros_like(acc_ref)

        acc_ref[...] += lax.dot_general(a_ref[...].astype(MXU_DTYPE), b_ref[...].astype(MXU_DTYPE), dims,
                                        preferred_element_type=F32)

        @pl.when(kk == nk - 1)
        def _():
            o_ref[...] = acc_ref[...].astype(o_ref.dtype)

        if nc:
            @pl.when(step == grid[0] * grid[1] * nk - 1)
            def _():
                wait()

    c_args, c_specs, c_shapes, c_sems = _comm_specs(comm)
    res = pl.pallas_call(
        body, name=name, grid=grid,
        in_specs=[a_spec, b_spec] + c_specs,
        out_specs=[pl.BlockSpec((tm, tn), lambda i, j, kk: (i, j))] + c_specs,
        out_shape=[jax.ShapeDtypeStruct((m, n), out_dtype)] + c_shapes,
        scratch_shapes=[pltpu.VMEM((tm, tn), F32)] + c_sems,
        compiler_params=_cparams(("arbitrary", "arbitrary", "arbitrary")),
    )(a, b, *c_args)
    return res if nc else res[0]


def _prev(u, first):
    return jnp.where(first, 0.0, pltpu.roll(u, 1, 0))


def _next(u, last):
    return jnp.where(last, 0.0, pltpu.roll(u, u.shape[0] - 1, 0))


def _edge_masks(t, w):
    row = lax.broadcasted_iota(jnp.int32, (t, w), 0)
    return row == 0, row == t - 1


SHIFT_CW = 256


def shift_fwd(p, mu_prev, mu_next, nb, t):
    cw, c0 = SHIFT_CW, C_RW // SHIFT_CW

    def body(p_ref, mp_ref, mn_ref, s_ref):
        x = p_ref[...]
        first, last = _edge_masks(t, cw)
        s_ref[...] = x + mp_ref[...] * (_prev(x, first) - x) + mn_ref[...] * (_next(x, last) - x)

    return pl.pallas_call(
        body, name="rwkv_shift_fwd", grid=(nb, RW_PW // cw),
        in_specs=[pl.BlockSpec((t, cw), lambda b, j: (b, c0 + j)), pl.BlockSpec((1, cw), lambda b, j: (0, j)),
                  pl.BlockSpec((1, cw), lambda b, j: (0, j))],
        out_specs=pl.BlockSpec((t, cw), lambda b, j: (b, j)),
        out_shape=jax.ShapeDtypeStruct((nb * t, RW_PW), F32),
        compiler_params=_cparams(("arbitrary", "arbitrary")),
    )(p, mu_prev, mu_next)


def shift_bwd(p, ds, mu_prev, mu_next, nb, t):
    cw, c0 = SHIFT_CW, C_RW // SHIFT_CW

    def body(p_ref, ds_ref, mp_ref, mn_ref, dp_ref, dmp_ref, dmn_ref):
        @pl.when(pl.program_id(1) == 0)
        def _():
            dmp_ref[...] = jnp.zeros_like(dmp_ref)
            dmn_ref[...] = jnp.zeros_like(dmn_ref)

        x, g = p_ref[...], ds_ref[...]
        mp, mn = mp_ref[...], mn_ref[...]
        first, last = _edge_masks(t, cw)
        dp_ref[...] = g * (1.0 - mp - mn) + _next(mp * g, last) + _prev(mn * g, first)
        dmp_ref[...] += jnp.sum(g * (_prev(x, first) - x), axis=0, keepdims=True)
        dmn_ref[...] += jnp.sum(g * (_next(x, last) - x), axis=0, keepdims=True)

    return pl.pallas_call(
        body, name="rwkv_shift_bwd", grid=(RW_PW // cw, nb),
        in_specs=[pl.BlockSpec((t, cw), lambda j, b: (b, c0 + j)), pl.BlockSpec((t, cw), lambda j, b: (b, j)),
                  pl.BlockSpec((1, cw), lambda j, b: (0, j)), pl.BlockSpec((1, cw), lambda j, b: (0, j))],
        out_specs=[pl.BlockSpec((t, cw), lambda j, b: (b, j)), pl.BlockSpec((1, cw), lambda j, b: (0, j)),
                   pl.BlockSpec((1, cw), lambda j, b: (0, j))],
        out_shape=[jax.ShapeDtypeStruct((nb * t, RW_PW), F32), jax.ShapeDtypeStruct((1, RW_PW), F32),
                   jax.ShapeDtypeStruct((1, RW_PW), F32)],
        compiler_params=_cparams(("arbitrary", "arbitrary")),
    )(p, ds, mu_prev, mu_next)


def conv_glu_fwd(u, cw, cb, nb, t):
    def body(u_ref, w_ref, b_ref, z_ref):
        x, w = u_ref[...], w_ref[...]
        first, last = _edge_masks(t, 2 * LANE)
        c = w[0:1] * _prev(x, first) + w[1:2] * x + w[2:3] * _next(x, last) + b_ref[...]
        z_ref[...] = (_silu(c[:, :LANE]) * c[:, LANE:]).astype(z_ref.dtype)

    return pl.pallas_call(
        body, name="conv_glu_fwd", grid=(nb, FFP // LANE),
        in_specs=[pl.BlockSpec((t, 2 * LANE), lambda b, j: (b, j)), pl.BlockSpec((3, 2 * LANE), lambda b, j: (0, j)),
                  pl.BlockSpec((1, 2 * LANE), lambda b, j: (0, j))],
        out_specs=pl.BlockSpec((t, LANE), lambda b, j: (b, j)),
        out_shape=jax.ShapeDtypeStruct((nb * t, FFP), MXU_DTYPE),
        compiler_params=_cparams(("arbitrary", "arbitrary")),
    )(u, cw, cb)


def conv_glu_bwd(u, dz, cw, cb, nb, t):
    def body(u_ref, dz_ref, w_ref, b_ref, du_ref, dw_ref, db_ref):
        @pl.when(pl.program_id(1) == 0)
        def _():
            dw_ref[...] = jnp.zeros_like(dw_ref)
            db_ref[...] = jnp.zeros_like(db_ref)

        x, w, g = u_ref[...], w_ref[...], dz_ref[...]
        first, last = _edge_masks(t, 2 * LANE)
        xp, xn = _prev(x, first), _next(x, last)
        c = w[0:1] * xp + w[1:2] * x + w[2:3] * xn + b_ref[...]
        cg, cv = c[:, :LANE], c[:, LANE:]
        sg = _sigmoid(cg)
        dcg = g * cv * (sg * (1.0 + cg * (1.0 - sg)))
        dcv = g * (cg * sg)
        dc = jnp.concatenate([dcg, dcv], axis=1)
        du = w[1:2] * dc + _next(w[0:1] * dc, last) + _prev(w[2:3] * dc, first)
        du_ref[...] = du.astype(du_ref.dtype)
        dw_ref[0:1, :] += jnp.sum(dc * xp, axis=0, keepdims=True)
        dw_ref[1:2, :] += jnp.sum(dc * x, axis=0, keepdims=True)
        dw_ref[2:3, :] += jnp.sum(dc * xn, axis=0, keepdims=True)
        db_ref[...] += jnp.sum(dc, axis=0, keepdims=True)

    return pl.pallas_call(
        body, name="conv_glu_bwd", grid=(FFP // LANE, nb),
        in_specs=[pl.BlockSpec((t, 2 * LANE), lambda j, b: (b, j)), pl.BlockSpec((t, LANE), lambda j, b: (b, j)),
                  pl.BlockSpec((3, 2 * LANE), lambda j, b: (0, j)), pl.BlockSpec((1, 2 * LANE), lambda j, b: (0, j))],
        out_specs=[pl.BlockSpec((t, 2 * LANE), lambda j, b: (b, j)), pl.BlockSpec((3, 2 * LANE), lambda j, b: (0, j)),
                   pl.BlockSpec((1, 2 * LANE), lambda j, b: (0, j))],
        out_shape=[jax.ShapeDtypeStruct((nb * t, 2 * FFP), MXU_DTYPE), jax.ShapeDtypeStruct((3, 2 * FFP), F32),
                   jax.ShapeDtypeStruct((1, 2 * FFP), F32)],
        compiler_params=_cparams(("arbitrary", "arbitrary")),
    )(u, dz, cw, cb)


def _gla_chunk(q, k, v, afab, wa2p, ba, s_in, reverse):
    c = GLA_CHUNK
    ri = lax.broadcasted_iota(jnp.int32, (c, c), 0)
    ci = lax.broadcasted_iota(jnp.int32, (c, c), 1)
    keep = (ci >= ri) if reverse else (ci <= ri)
    i_ref = (c - 1 - c // 2) if reverse else (c // 2)
    pick_ref = (ci == i_ref).astype(F32)
    ones_cc = jnp.ones((c, c), F32)
    lane = lax.broadcasted_iota(jnp.int32, (1, LANE), 1)
    outs, states = [None] * GLA_H, [None] * GLA_H
    for pr in range(GLA_H // 2):
        la = -_softplus(-(mm(afab, wa2p[pr]) + ba[pr])) * (1.0 / GLA_LOGIT_NORM)
        b = mm_exact(keep.astype(F32), la, b_is_01=False)
        b_ref = mm_exact(pick_ref, b, b_is_01=False)
        b_last = mm_exact(ones_cc, la, b_is_01=False)
        qs = q[pr] * (GLA_DK ** -0.5)
        qi = qs * jnp.exp(b - b_ref)
        ki = k[pr] * jnp.exp(b_ref - b)
        kd = k[pr] * jnp.exp(b_last - b)
        qb = qs * jnp.exp(b)
        dec = jnp.exp(mm_tn_exact(la, jnp.ones((c, LANE), F32)))
        for h in (2 * pr, 2 * pr + 1):
            m = ((lane // GLA_DK) == (h % 2)).astype(F32)
            a = jnp.where(keep, mm_nt(qi * m, ki), 0.0)
            o_intra = mm(a, v[h])
            kv = mm_tn(kd * m, v[h])
            o_inter = mm(qb * m, s_in[h])
            outs[h] = o_intra + o_inter
            states[h] = s_in[h] * dec + kv
    return outs, states


def _gla_load(q_ref, k_ref, v_ref, w_ref, ba_ref, s, rows):
    q = [q_ref[s, rows, pr * LANE:(pr + 1) * LANE] for pr in range(GLA_H // 2)]
    k = [k_ref[s, rows, pr * LANE:(pr + 1) * LANE] for pr in range(GLA_H // 2)]
    v = [v_ref[s, rows, h * GLA_DV:(h + 1) * GLA_DV] for h in range(GLA_H)]
    w = [w_ref[:, pr * LANE:(pr + 1) * LANE] for pr in range(GLA_H // 2)]
    ba = [ba_ref[:, pr * LANE:(pr + 1) * LANE] for pr in range(GLA_H // 2)]
    return q, k, v, w, ba


GLA_TILE = 512
GLA_SB = 2


def _gla_specs(nb, t, reverse):
    tile = min(GLA_TILE, t)
    nt = t // tile
    sb = GLA_SB if nb % GLA_SB == 0 else 1
    return tile, tile // GLA_CHUNK, nt, sb, ((lambda j: nt - 1 - j) if reverse else (lambda j: j))


def gla_fwd(p, wa2p, ba, o_add, nb, t, reverse):
    tile, cpt, nt, sb, tj = _gla_specs(nb, t, reverse)
    has_add = o_add is not None

    def body(*refs):
        if has_add:
            q_ref, k_ref, v_ref, af_ref, w_ref, ba_ref, add_ref, o_ref, hist_ref, s_ref = refs
        else:
            q_ref, k_ref, v_ref, af_ref, w_ref, ba_ref, o_ref, hist_ref, s_ref = refs

        @pl.when(pl.program_id(1) == 0)
        def _():
            s_ref[...] = jnp.zeros_like(s_ref)

        def step(i, carry):
            ci = (cpt - 1 - i) if reverse else i
            rows = pl.ds(pl.multiple_of(ci * GLA_CHUNK, GLA_CHUNK), GLA_CHUNK)
            for s in range(sb):
                s_in = [s_ref[s, h] for h in range(GLA_H)]
                for h in range(GLA_H):
                    hist_ref[s, ci, h] = s_in[h]
                q, k, v, w, ba = _gla_load(q_ref, k_ref, v_ref, w_ref, ba_ref, s, rows)
                outs, states = _gla_chunk(q, k, v, af_ref[s, rows, :], w, ba, s_in, reverse)
                for h in range(GLA_H):
                    oh = outs[h]
                    if has_add:
                        oh = oh + add_ref[s, rows, h * GLA_DV:(h + 1) * GLA_DV]
                    o_ref[s, rows, h * GLA_DV:(h + 1) * GLA_DV] = oh
                    s_ref[s, h] = states[h]
            return carry

        lax.fori_loop(0, cpt, step, 0)

    col = lambda width, c0: pl.BlockSpec((sb, tile, width), lambda b, j: (b, tj(j), c0 // width))
    in_specs = [col(256, C_Q), col(256, C_K), col(512, C_V), col(LANE, C_AFAB),
                pl.BlockSpec((LANE, 256), lambda b, j: (0, 0)), pl.BlockSpec((1, 256), lambda b, j: (0, 0))]
    p3 = p.reshape(nb, t, p.shape[1])
    args = [p3, p3, p3, p3, wa2p, ba]
    if has_add:
        in_specs.append(col(512, 0))
        args.append(o_add.reshape(nb, t, 512))
    o, hist = pl.pallas_call(
        body, name="gla_fwd_rev" if reverse else "gla_fwd", grid=(nb // sb, nt),
        in_specs=in_specs,
        out_specs=[col(512, 0), pl.BlockSpec((sb, cpt, GLA_H, LANE, LANE), lambda b, j: (b, tj(j), 0, 0, 0))],
        out_shape=[jax.ShapeDtypeStruct((nb, t, 512), F32),
                   jax.ShapeDtypeStruct((nb, t // GLA_CHUNK, GLA_H, LANE, LANE), F32)],
        scratch_shapes=[pltpu.VMEM((sb, GLA_H, LANE, LANE), F32)],
        compiler_params=_cparams(("arbitrary", "arbitrary")),
    )(*args)
    return o.reshape(nb * t, 512), hist


def gla_bwd(p, wa2p, ba, hist, do, dprev, nb, t, reverse):
    tile, cpt, nt, sb, tj_f = _gla_specs(nb, t, reverse)
    tj = lambda j: tj_f(nt - 1 - j)
    has_prev = dprev is not None

    def body(*refs):
        if has_prev:
            q_ref, k_ref, v_ref, af_ref, w_ref, ba_ref, hist_ref, do_ref, prev_ref, dqkv_ref, dw_ref, dba_ref, ds_ref = refs
        else:
            q_ref, k_ref, v_ref, af_ref, w_ref, ba_ref, hist_ref, do_ref, dqkv_ref, dw_ref, dba_ref, ds_ref = refs

        @pl.when((pl.program_id(0) == 0) & (pl.program_id(1) == 0))
        def _():
            dw_ref[...] = jnp.zeros_like(dw_ref)
            dba_ref[...] = jnp.zeros_like(dba_ref)

        @pl.when(pl.program_id(1) == 0)
        def _():
            ds_ref[...] = jnp.zeros_like(ds_ref)

        def step(i, carry):
            ci = i if reverse else (cpt - 1 - i)
            rows = pl.ds(pl.multiple_of(ci * GLA_CHUNK, GLA_CHUNK), GLA_CHUNK)
            fn = functools.partial(_gla_chunk, reverse=reverse)
            for s in range(sb):
                s_in = [hist_ref[s, ci, h] for h in range(GLA_H)]
                q, k, v, w, ba = _gla_load(q_ref, k_ref, v_ref, w_ref, ba_ref, s, rows)
                _, vjp = jax.vjp(fn, q, k, v, af_ref[s, rows, :], w, ba, s_in)
                d_o = [do_ref[s, rows, h * GLA_DV:(h + 1) * GLA_DV] for h in range(GLA_H)]
                d_s = [ds_ref[s, h] for h in range(GLA_H)]
                dq, dk, dv, daf, dw, dba, ds_in = vjp((d_o, d_s))
                pieces = [(pr * LANE, dq[pr]) for pr in range(2)] + [(256 + pr * LANE, dk[pr]) for pr in range(2)]
                pieces += [(512 + h * GLA_DV, dv[h]) for h in range(GLA_H)] + [(1024, daf)]
                for c0, val in pieces:
                    if has_prev:
                        val = val + prev_ref[s, rows, c0:c0 + LANE]
                    dqkv_ref[s, rows, c0:c0 + LANE] = val
                for pr in range(2):
                    dw_ref[:, pr * LANE:(pr + 1) * LANE] += dw[pr]
                    dba_ref[:, pr * LANE:(pr + 1) * LANE] += dba[pr]
                for h in range(GLA_H):
                    ds_ref[s, h] = ds_in[h]
            return carry

        lax.fori_loop(0, cpt, step, 0)

    col = lambda width, c0: pl.BlockSpec((sb, tile, width), lambda b, j: (b, tj(j), c0 // width))
    in_specs = [col(256, C_Q), col(256, C_K), col(512, C_V), col(LANE, C_AFAB),
                pl.BlockSpec((LANE, 256), lambda b, j: (0, 0)), pl.BlockSpec((1, 256), lambda b, j: (0, 0)),
                pl.BlockSpec((sb, cpt, GLA_H, LANE, LANE), lambda b, j: (b, tj(j), 0, 0, 0)), col(512, 0)]
    p3 = p.reshape(nb, t, p.shape[1])
    args = [p3, p3, p3, p3, wa2p, ba, hist, do.reshape(nb, t, 512)]
    if has_prev:
        in_specs.append(col(1152, 0))
        args.append(dprev.reshape(nb, t, 1152))
    dqkv, dw, dba = pl.pallas_call(
        body, name="gla_bwd_rev" if reverse else "gla_bwd", grid=(nb // sb, nt),
        in_specs=in_specs,
        out_specs=[col(1152, 0), pl.BlockSpec((LANE, 256), lambda b, j: (0, 0)), pl.BlockSpec((1, 256), lambda b, j: (0, 0))],
        out_shape=[jax.ShapeDtypeStruct((nb, t, 1152), F32), jax.ShapeDtypeStruct((LANE, 256), F32),
                   jax.ShapeDtypeStruct((1, 256), F32)],
        scratch_shapes=[pltpu.VMEM((sb, GLA_H, LANE, LANE), F32)],
        compiler_params=_cparams(("arbitrary", "arbitrary")),
    )(*args)
    return dqkv.reshape(nb * t, 1152), dw, dba


SCAN_TB = 8
RW_VH = RW_N // 2


def _bwd_lanes():
    lane = lax.broadcasted_iota(jnp.int32, (1, LANE), 1)
    return ((lane // (LANE // 4)) % 2) == 1


def _comm_specs(comm):
    anyspec = pl.BlockSpec(memory_space=pl.ANY)
    n = len(comm)
    shapes = [jax.ShapeDtypeStruct((N_DEV,) + (a.shape[1:] if sc else a.shape), a.dtype) for a, sc in comm]
    sems = [pltpu.SemaphoreType.DMA((n, N_DEV - 1)), pltpu.SemaphoreType.DMA((n, N_DEV - 1)), pltpu.SemaphoreType.DMA((n,))] if n else []
    return [a for a, _ in comm], [anyspec] * n, shapes, sems


def rwkv_scan_fwd(r, wf, wb, k, a, b, v, comm=()):
    t = r.shape[0]
    nt = t // SCAN_TB
    nc = len(comm)
    flags = [sc for _, sc in comm]

    def body(*refs):
        (rf, rm, kf, km, af, am, bf, bm, wf_ref, wb_ref, vf, vm), refs = refs[:12], refs[12:]
        c_in, refs = refs[:nc], refs[nc:]
        (yf_ref, ym_ref, hist_ref, sa_ref, fin_ref), refs = refs[:5], refs[5:]
        c_out, refs = refs[:nc], refs[nc:]
        s_ref, sems = refs[0], refs[1:]
        i = pl.program_id(0)
        if nc:
            start, wait = _exchange_plan(flags, c_in, c_out, *sems)

        @pl.when(i == 0)
        def _():
            s_ref[...] = jnp.zeros_like(s_ref)
            if nc:
                start()

        bwd = _bwd_lanes()

        def step(tt, carry):
            mt = SCAN_TB - 1 - tt
            pick = lambda f_ref, m_ref: jnp.where(bwd, m_ref[mt], f_ref[tt])
            rt, kt, at, bt, wt = pick(rf, rm), pick(kf, km), pick(af, am), pick(bf, bm), pick(wf_ref, wb_ref)
            for vi in range(RW_VH):
                sv = s_ref[vi]
                hist_ref[tt, vi] = sv
                sa = jnp.sum(sv * at, axis=0, keepdims=True)
                v_row = jnp.where(bwd, vm[mt, vi:vi + 1, :], vf[tt, vi:vi + 1, :])
                sn = sv * wt + sa * bt + v_row * kt
                s_ref[vi] = sn
                y_row = jnp.sum(sn * rt, axis=0, keepdims=True)
                yf_ref[tt, vi:vi + 1, :] = y_row
                ym_ref[mt, vi:vi + 1, :] = y_row
                sa_ref[tt, vi:vi + 1, :] = sa
            return carry

        lax.fori_loop(0, SCAN_TB, step, 0)

        @pl.when(i == nt - 1)
        def _():
            fin_ref[...] = s_ref[...]
            if nc:
                wait()

    fwd_map, mir_map = (lambda i: (i, 0, 0)), (lambda i: (nt - 1 - i, 0, 0))
    kf_spec, km_spec = pl.BlockSpec((SCAN_TB, RW_N, LANE), fwd_map), pl.BlockSpec((SCAN_TB, RW_N, LANE), mir_map)
    vf_spec, vm_spec = pl.BlockSpec((SCAN_TB, RW_VH, LANE), fwd_map), pl.BlockSpec((SCAN_TB, RW_VH, LANE), mir_map)
    c_args, c_specs, c_shapes, c_sems = _comm_specs(comm)
    vshape = jax.ShapeDtypeStruct((t, RW_VH, LANE), F32)
    return pl.pallas_call(
        body, name="rwkv_scan_fwd", grid=(nt,),
        in_specs=[kf_spec, km_spec] * 4 + [kf_spec, km_spec, vf_spec, vm_spec] + c_specs,
        out_specs=[vf_spec, vm_spec, pl.BlockSpec((SCAN_TB, RW_VH, RW_N, LANE), lambda i: (i, 0, 0, 0)), vf_spec,
                   pl.BlockSpec((RW_VH, RW_N, LANE), lambda i: (0, 0, 0))] + c_specs,
        out_shape=[vshape, vshape, jax.ShapeDtypeStruct((t, RW_VH, RW_N, LANE), F32), vshape,
                   jax.ShapeDtypeStruct((RW_VH, RW_N, LANE), F32)] + c_shapes,
        scratch_shapes=[pltpu.VMEM((RW_VH, RW_N, LANE), F32)] + c_sems,
        compiler_params=_cparams(("arbitrary",)),
    )(r, r, k, k, a, a, b, b, wf, wb, v, v, *c_args)


def rwkv_scan_bwd(r, wf, wb, k, a, b, v, hist, sa, fin, dy, comm=()):
    t = r.shape[0]
    nt = t // SCAN_TB
    nc = len(comm)
    flags = [sc for _, sc in comm]

    def body(*refs):
        (rf, rm, kf, km, af, am, bf, bm, wf_ref, wb_ref, vf, vm, hist_ref, sa_ref, fin_ref, dyf, dym), refs = refs[:17], refs[17:]
        c_in, refs = refs[:nc], refs[nc:]
        k_outs, (dvf_ref, dvm_ref), refs = refs[:4], refs[4:6], refs[6:]
        c_out, refs = refs[:nc], refs[nc:]
        ds_ref, snext_ref, sems = refs[0], refs[1], refs[2:]
        i = pl.program_id(0)
        if nc:
            start, wait = _exchange_plan(flags, c_in, c_out, *sems)

        @pl.when(i == 0)
        def _():
            ds_ref[...] = jnp.zeros_like(ds_ref)
            snext_ref[...] = fin_ref[...]
            if nc:
                start()

        bwd = _bwd_lanes()
        group = lax.broadcasted_iota(jnp.int32, (1, LANE), 1) // RW_Q

        for tt in range(SCAN_TB - 1, -1, -1):
            mt = SCAN_TB - 1 - tt
            pick = lambda f_ref, m_ref: jnp.where(bwd, m_ref[mt], f_ref[tt])
            rt, kt, at, bt, wt = pick(rf, rm), pick(kf, km), pick(af, am), pick(bf, bm), pick(wf_ref, wb_ref)
            zero = jnp.zeros((RW_N, LANE), F32)
            dr, dw, dk, da, db = zero, zero, zero, zero, zero
            for vi in range(RW_VH):
                sv = hist_ref[tt, vi]
                sn = hist_ref[tt + 1, vi] if tt + 1 < SCAN_TB else snext_ref[vi]
                sa_row = sa_ref[tt, vi:vi + 1, :]
                v_row = jnp.where(bwd, vm[mt, vi:vi + 1, :], vf[tt, vi:vi + 1, :])
                dy_row = jnp.where(bwd, dym[mt, vi:vi + 1, :], dyf[tt, vi:vi + 1, :])
                dsv = ds_ref[vi] + dy_row * rt
                dr = dr + sn * dy_row
                dsa = jnp.sum(dsv * bt, axis=0, keepdims=True)
                dw = dw + sv * dsv
                db = db + dsv * sa_row
                dk = dk + dsv * v_row
                dv_row = jnp.sum(dsv * kt, axis=0, keepdims=True)
                dvf_ref[tt, vi:vi + 1, :] = dv_row
                dvm_ref[mt, vi:vi + 1, :] = dv_row
                da = da + sv * dsa
                ds_ref[vi] = dsv * wt + dsa * at
            dr, dw, dk, da, db = [val + pltpu.roll(val, LANE // 2, 1) for val in (dr, dw, dk, da, db)]
            up, down = (lambda val: pltpu.roll(val, RW_Q, 1)), (lambda val: pltpu.roll(val, LANE - RW_Q, 1))
            packed_f = jnp.where(group == 0, dr, jnp.where(group == 1, up(dk), jnp.where(group == 2, da, up(db))))
            packed_m = jnp.where(group == 0, down(dr), jnp.where(group == 1, dk, jnp.where(group == 2, down(da), db)))
            k_outs[0][tt] = packed_f
            k_outs[1][mt] = packed_m
            k_outs[2][tt] = dw
            k_outs[3][mt] = dw
        snext_ref[...] = hist_ref[0]

        if nc:
            @pl.when(i == nt - 1)
            def _():
                wait()

    fwd_map, mir_map = (lambda i: (nt - 1 - i, 0, 0)), (lambda i: (i, 0, 0))
    kf_spec, km_spec = pl.BlockSpec((SCAN_TB, RW_N, LANE), fwd_map), pl.BlockSpec((SCAN_TB, RW_N, LANE), mir_map)
    vf_spec, vm_spec = pl.BlockSpec((SCAN_TB, RW_VH, LANE), fwd_map), pl.BlockSpec((SCAN_TB, RW_VH, LANE), mir_map)
    state_spec = pl.BlockSpec((RW_VH, RW_N, LANE), lambda i: (0, 0, 0))
    c_args, c_specs, c_shapes, c_sems = _comm_specs(comm)
    kshape, vshape = jax.ShapeDtypeStruct((t, RW_N, LANE), F32), jax.ShapeDtypeStruct((t, RW_VH, LANE), F32)
    res = pl.pallas_call(
        body, name="rwkv_scan_bwd", grid=(nt,),
        in_specs=[kf_spec, km_spec] * 4 + [kf_spec, km_spec, vf_spec, vm_spec,
                                           pl.BlockSpec((SCAN_TB, RW_VH, RW_N, LANE), lambda i: (nt - 1 - i, 0, 0, 0)),
                                           vf_spec, state_spec, vf_spec, vm_spec] + c_specs,
        out_specs=[kf_spec, km_spec] * 2 + [vf_spec, vm_spec] + c_specs,
        out_shape=[kshape] * 4 + [vshape] * 2 + c_shapes,
        scratch_shapes=[pltpu.VMEM((RW_VH, RW_N, LANE), F32), pltpu.VMEM((RW_VH, RW_N, LANE), F32)] + c_sems,
        compiler_params=_cparams(("arbitrary",)),
    )(r, r, k, k, a, a, b, b, wf, wb, v, v, hist, sa, fin, dy, dy, *c_args)
    return res


RELAYOUT_TB = 128
RW_Q = LANE // 4


def to_scan(name, x, cb, nb, t, value):
    tb = min(RELAYOUT_TB, t)
    rows_out = RW_VH if value else RW_N

    def body(x_ref, o_ref, scr):
        for b in range(nb):
            scr[b * RW_H:(b + 1) * RW_H] = x_ref[b].T.reshape(RW_H, RW_N, tb)
        for j in range(rows_out):
            lo = scr[:, j, :]
            hi = scr[:, j + RW_VH, :] if value else lo
            o_ref[:, j, :] = jnp.concatenate([lo, lo, hi, hi], axis=0).T

    return pl.pallas_call(
        body, name=name, grid=(t // tb,),
        in_specs=[pl.BlockSpec((nb, tb, RW_W), lambda i: (0, i, cb))],
        out_specs=pl.BlockSpec((tb, rows_out, LANE), lambda i: (i, 0, 0)),
        out_shape=jax.ShapeDtypeStruct((t, rows_out, LANE), F32),
        scratch_shapes=[pltpu.VMEM((nb * RW_H, RW_N, tb), F32)],
        compiler_params=_cparams(("arbitrary",)),
    )(x.reshape(nb, t, x.shape[1]))


def from_scan(name, xf, xm, nb, t, value, picks=((0, 1),)):
    tb = min(RELAYOUT_TB, t)
    rows_in = RW_VH if value else RW_N
    n_out = 1 if value else len(picks)
    grp = lambda a, g: a[g * RW_Q:(g + 1) * RW_Q]

    def body(f_ref, m_ref, *rest):
        outs, scrs = rest[:n_out], rest[n_out:]
        for j in range(rows_in):
            a, b = f_ref[:, j, :].T, m_ref[:, j, :].T
            if value:
                scrs[0][:, j, :] = grp(a, 0) + grp(b, 1)
                scrs[0][:, j + RW_VH, :] = grp(a, 2) + grp(b, 3)
            else:
                for scr, (gf, gm) in zip(scrs, picks):
                    parts = ([grp(a, gf)] if gf is not None else []) + ([grp(b, gm)] if gm is not None else [])
                    scr[:, j, :] = parts[0] if len(parts) == 1 else parts[0] + parts[1]
        for o_ref, scr in zip(outs, scrs):
            for b in range(nb):
                o_ref[b] = scr[b * RW_H:(b + 1) * RW_H].reshape(RW_W, tb).T

    res = pl.pallas_call(
        body, name=name, grid=(t // tb,),
        in_specs=[pl.BlockSpec((tb, rows_in, LANE), lambda i: (i, 0, 0))] * 2,
        out_specs=[pl.BlockSpec((nb, tb, RW_W), lambda i: (0, i, 0))] * n_out,
        out_shape=[jax.ShapeDtypeStruct((nb, t, RW_W), F32)] * n_out,
        scratch_shapes=[pltpu.VMEM((nb * RW_H, RW_N, tb), F32)] * n_out,
        compiler_params=_cparams(("arbitrary",)),
    )(xf, xm)
    return [r.reshape(nb * t, RW_W) for r in res]


def f_norm(rows, params):
    (x,), (g,) = rows, params
    return [_rmsnorm(x, g)]


def f_rwkv_pre(rows, params):
    k, wlal, gl = rows
    w0f, w2f, w0b, w2b, a0, a2, g2, k_k, k_a = params
    seg = _segment_ones(RW_W, RW_N)
    tw = jnp.tanh(wlal)

    def decay(w0, w2):
        return jnp.exp(-jnp.exp(-_softplus(-(w0 + mm(tw, w2))) - 0.5))

    lr = _sigmoid(a0 + mm(wlal, a2))
    gate = mm(_sigmoid(gl), g2)
    kk = k * k_k
    kk = kk / jnp.maximum(jnp.sqrt(mm_exact(kk * kk, seg)), 1e-12)
    kp = k * (1.0 + (lr - 1.0) * k_a)
    return [decay(w0f, w2f), decay(w0b, w2b), kp, -kk, kk * lr, gate]


def f_branch_post(rows, params):
    o, og, y, r, kp, v, g = rows
    gla_g, ln_w, ln_b, r_k = params
    seg_gla = _segment_ones(GLA_H * GLA_DV, GLA_DV)
    seg_rw = _segment_ones(RW_W, RW_N)
    on = o * lax.rsqrt(mm_exact(o * o, seg_gla) * (1.0 / GLA_DV) + HEAD_NORM_EPS)
    oa = on * gla_g * _silu(og)
    mu = mm_exact(y, seg_rw) * (1.0 / RW_N)
    yc = y - mu
    var = mm_exact(yc * yc, seg_rw) * (1.0 / RW_N)
    yn = yc * lax.rsqrt(var + RW_GN_EPS) * ln_w + ln_b
    bonus = mm_exact(r * kp * r_k, seg_rw) * v
    return [oa, (yn + bonus) * g]


def f_merge(rows, params):
    ga, gb, ya, yb = rows
    return [_sigmoid(ga) * ya + _sigmoid(gb) * yb]


def f_norm2(rows, params):
    (x, mo), (g,) = rows, params
    x1 = x + mo
    return [x1, _rmsnorm(x1, g)]


def loss_head(x1, ffo, tgt, gf, tm):
    n = x1.shape[0]

    def body(x1_ref, f_ref, t_ref, g_ref, loss_ref, dx_ref, dg_ref):
        @pl.when(pl.program_id(0) == 0)
        def _():
            loss_ref[...] = jnp.zeros_like(loss_ref)
            dg_ref[...] = jnp.zeros_like(dg_ref)

        tgt_v = t_ref[...]

        def f(x2, g):
            err = _rmsnorm(x2, g) - tgt_v
            return jnp.sum(jnp.sum(err * err, axis=-1, keepdims=True), axis=0, keepdims=True) * (0.5 / D)

        val, vjp = jax.vjp(f, x1_ref[...] + f_ref[...], g_ref[...])
        dx, dg = vjp(jnp.ones((1, 1), F32))
        loss_ref[...] += val
        dx_ref[...] = dx
        dg_ref[...] += dg

    return pl.pallas_call(
        body, name="loss_head", grid=(n // tm,),
        in_specs=[_row_spec(tm, D, 0)] * 3 + [_full_spec((1, D))],
        out_specs=[_full_spec((1, 1)), _row_spec(tm, D, 0), _full_spec((1, D))],
        out_shape=[jax.ShapeDtypeStruct((1, 1), F32), jax.ShapeDtypeStruct((n, D), F32), jax.ShapeDtypeStruct((1, D), F32)],
        compiler_params=_cparams(("arbitrary",)),
    )(x1, ffo, tgt, gf)


def _pad_cols(a, width):
    return jnp.pad(a, ((0, 0), (0, width - a.shape[1])))


def w_in_to_padded(w):
    return _pad_cols(jnp.concatenate([w[:, 3360:5408], w[:, 0:1536], w[:, 1568:3360], w[:, 1536:1568]], axis=1), NP)


def w_in_from_padded(wp):
    return jnp.concatenate([wp[:, 2048:3584], wp[:, 5376:5408], wp[:, 3584:5376], wp[:, 0:2048]], axis=1)


def ff_interleave(a):
    r = a.shape[0]
    halves = jnp.stack([_pad_cols(a[:, :D_FF], FFP), _pad_cols(a[:, D_FF:], FFP)], axis=1)
    return halves.reshape(r, 2, FFP // LANE, LANE).transpose(0, 2, 1, 3).reshape(r, 2 * FFP)


def ff_deinterleave(a):
    r = a.shape[0]
    halves = a.reshape(r, FFP // LANE, 2, LANE).transpose(0, 2, 1, 3).reshape(r, 2, FFP)
    return halves[:, :, :D_FF].reshape(r, 2 * D_FF)


def _rows_into(w, rows, off):
    return jnp.zeros((rows, w.shape[1]), w.dtype).at[off:off + w.shape[0]].set(w)


LATE = ("gla_proj", "rwkv_proj", "w_out", "ffn_up", "ffn_conv_w", "ffn_down")


def local_step(x, tgt, w, nb, t, late_blocks=None):
    n = nb * t
    tm = min(n, 1024)
    tr = min(n, 256)
    vec = lambda a: a.reshape(1, -1)
    w = dict(w)

    w_in_p = w_in_to_padded(w["w_in"])
    wa2_f, wa2_b = _rows_into(w["gla_wa2_f"], LANE, 0), _rows_into(w["gla_wa2_b"], LANE, GLA_RANK)
    w2f, w2b = _rows_into(w["rwkv_w2_f"], LANE, 0), _rows_into(w["rwkv_w2_b"], LANE, 0)
    a2 = _rows_into(w["rwkv_a2"], LANE, 64)
    g1, g2n, gf = vec(w["norm1_g"]), vec(w["norm2_g"]), vec(w["norm_f_g"])
    mu_prev, mu_next = vec(w["rwkv_mu_prev"]), vec(w["rwkv_mu_next"])
    pre_params = [vec(w["rwkv_w0_f"]), w2f, vec(w["rwkv_w0_b"]), w2b, vec(w["rwkv_a0"]), a2, w["rwkv_g2"],
                  vec(w["rwkv_k_k"]), vec(w["rwkv_k_a"])]
    post_params = [vec(w["gla_norm_g"]), vec(w["rwkv_ln_w"]), vec(w["rwkv_ln_b"]), vec(w["rwkv_r_k"])]
    ba_f, ba_b = vec(w["gla_ba_f"]), vec(w["gla_ba_b"])

    (h1,) = rowwise_fwd("norm1_fwd", f_norm, [(x, D, 0)], [g1], [(D, MXU_DTYPE)], tr)
    p = matmul("proj_in", h1, w_in_p, "nn", F32, tm, 512, D)
    s = shift_fwd(p, mu_prev, mu_next, nb, t)
    pre_rows = [(s, 512, 1), (s, LANE, 1536 // LANE), (s, LANE, 1664 // LANE)]
    wf, wb, kp, a_s, b_s, g = rowwise_fwd("rwkv_pre_fwd", f_rwkv_pre, pre_rows, pre_params, [(RW_W, F32)] * 6, tr)
    sc = [to_scan("to_scan_" + nm, a, cb, nb, t, val) for nm, a, cb, val in
          (("r", s, 0, False), ("wf", wf, 0, False), ("wb", wb, 0, False), ("k", kp, 0, False), ("a", a_s, 0, False),
           ("b", b_s, 0, False), ("v", s, 2, True))]
    comm = [] if late_blocks is None else [(late_blocks[k], False) for k in LATE]
    y_scf, y_scm, hist_rw, sa_sc, fin_rw, *gathered = rwkv_scan_fwd(*sc, comm=comm)
    for k, g_k in zip(LATE, gathered):
        w[k] = _gathered_to_full(g_k, SHARDED[k])
    ffn_up_p = ff_interleave(w["ffn_up"])
    conv_w_p, conv_b_p = ff_interleave(w["ffn_conv_w"]), ff_interleave(vec(w["ffn_conv_b"]))
    ffn_down_p = jnp.pad(w["ffn_down"], ((0, FFP - D_FF), (0, 0)))
    (y,) = from_scan("from_scan_y", y_scf, y_scm, nb, t, True)
    o_f, hist_f = gla_fwd(p, wa2_f, ba_f, None, nb, t, False)
    o, hist_b = gla_fwd(p, wa2_b, ba_b, o_f, nb, t, True)
    post_rows = [(o, 512, 0), (p, 512, C_OG // 512), (y, 512, 0), (s, 512, 0), (kp, 512, 0), (s, 512, 2), (g, 512, 0)]
    oa, ob = rowwise_fwd("branch_post_fwd", f_branch_post, post_rows, post_params, [(512, MXU_DTYPE)] * 2, tr)
    ya = matmul("gla_proj", oa, w["gla_proj"], "nn", F32, tm, 512, 512)
    yb = matmul("rwkv_proj", ob, w["rwkv_proj"], "nn", F32, tm, 512, 512)
    merge_rows = [(p, D, 0), (p, D, 1), (ya, D, 0), (yb, D, 0)]
    (merged,) = rowwise_fwd("merge_fwd", f_merge, merge_rows, [], [(D, MXU_DTYPE)], tr)
    mo = matmul("w_out", merged, w["w_out"], "nn", F32, tm, 512, D)
    x1, h2 = rowwise_fwd("norm2_fwd", f_norm2, [(x, D, 0), (mo, D, 0)], [g2n], [(D, F32), (D, MXU_DTYPE)], tr)
    u = matmul("ffn_up", h2, ffn_up_p, "nn", F32, tm, 512, D)
    z = conv_glu_fwd(u, conv_w_p, conv_b_p, nb, t)
    ffo = matmul("ffn_down", z, ffn_down_p, "nn", F32, tm, 512, FFP // 2)
    loss, dx2, dgf = loss_head(x1, ffo, tgt, gf, tr)

    dz = matmul("ffn_down_dx", dx2, ffn_down_p, "nt", F32, tm, FFP // 2, D)
    d_ffn_down_p = matmul("ffn_down_dw", z, dx2, "tn", F32, FFP // 2, 512, tm)
    du, d_conv_w_p, d_conv_b_p = conv_glu_bwd(u, dz, conv_w_p, conv_b_p, nb, t)
    dh2 = matmul("ffn_up_dx", du, ffn_up_p, "nt", F32, tm, D, 512)
    d_ffn_up_p = matmul("ffn_up_dw", h2, du, "tn", F32, D, 512, tm)
    (dx1,), (dg2,) = rowwise_bwd("norm2_bwd", f_norm2, [(x, D, 0), (mo, D, 0)], [g2n],
                                 [[(dx2, D, 0)], [(dh2, D, 0)]], tr, grad_rows=[1])
    dmerged = matmul("w_out_dx", dx1, w["w_out"], "nt", F32, tm, D, 512)
    d_w_out = matmul("w_out_dw", merged, dx1, "tn", F32, D, 512, tm)
    (dga, dgb, dya, dyb), _ = rowwise_bwd("merge_bwd", f_merge, merge_rows, [], [[(dmerged, D, 0)]], tr)
    d_oa = matmul("gla_proj_dx", dya, w["gla_proj"], "nt", F32, tm, 512, D)
    d_gla_proj = matmul("gla_proj_dw", oa, dya, "tn", F32, 512, 512, tm)
    d_ob = matmul("rwkv_proj_dx", dyb, w["rwkv_proj"], "nt", F32, tm, 512, D)
    d_rwkv_proj = matmul("rwkv_proj_dw", ob, dyb, "tn", F32, 512, 512, tm)
    (d_o, d_og, d_y, d_r_post, d_kp_post, d_v_post, d_g), d_post = rowwise_bwd(
        "branch_post_bwd", f_branch_post, post_rows, post_params, [[(d_oa, 512, 0)], [(d_ob, 512, 0)]], tr)
    late_grads = {"gla_proj": d_gla_proj, "rwkv_proj": d_rwkv_proj, "w_out": d_w_out, "ffn_up": ff_deinterleave(d_ffn_up_p),
                  "ffn_conv_w": ff_deinterleave(d_conv_w_p), "ffn_down": d_ffn_down_p[0:D_FF]}
    comm = [] if late_blocks is None else [(_full_to_slices(late_grads[k], SHARDED[k]), True) for k in LATE]
    dsc = rwkv_scan_bwd(*sc, hist_rw, sa_sc, fin_rw, to_scan("to_scan_dy", d_y, 0, nb, t, True), comm=comm)
    received = dict(zip(LATE, dsc[6:]))
    d_r_scan, d_kp_scan, d_a_scan, d_b_scan = from_scan("from_scan_rkab", dsc[0], dsc[1], nb, t, False,
                                                        picks=((0, 0), (1, 1), (2, 2), (3, 3)))
    d_wf, d_wb = from_scan("from_scan_w", dsc[2], dsc[3], nb, t, False, picks=((0, None), (None, 1)))
    (d_v_scan,) = from_scan("from_scan_dv", dsc[4], dsc[5], nb, t, True)
    (d_k, d_wlal, d_gl), d_pre = rowwise_bwd(
        "rwkv_pre_bwd", f_rwkv_pre, pre_rows, pre_params,
        [[(d_wf, 512, 0)], [(d_wb, 512, 0)], [(d_kp_scan, 512, 0), (d_kp_post, 512, 0)],
         [(d_a_scan, 512, 0)], [(d_b_scan, 512, 0)], [(d_g, 512, 0)]], tr)
    ds = jnp.concatenate([d_r_scan + d_r_post, d_k, d_v_scan + d_v_post, d_wlal, d_gl], axis=1)
    dp_rw, d_mu_prev, d_mu_next = shift_bwd(p, ds, mu_prev, mu_next, nb, t)
    dqkv_f, d_wa2_f, d_ba_f = gla_bwd(p, wa2_f, ba_f, hist_f, d_o, None, nb, t, False)
    dqkv, d_wa2_b, d_ba_b = gla_bwd(p, wa2_b, ba_b, hist_b, d_o, dqkv_f, nb, t, True)
    dp = jnp.concatenate([dga, dgb, dqkv[:, 0:1024], d_og, dp_rw, dqkv[:, 1024:1152],
                          jnp.zeros((n, NP - C_AFAB - LANE), F32)], axis=1)
    d_w_in_p = matmul("proj_in_dw", h1, dp, "tn", F32, D, 512, tm)
    grads = {
        "w_in": w_in_from_padded(d_w_in_p),
        "gla_wa2_f": d_wa2_f[0:GLA_RANK], "gla_ba_f": d_ba_f, "gla_wa2_b": d_wa2_b[GLA_RANK:2 * GLA_RANK], "gla_ba_b": d_ba_b,
        "gla_norm_g": d_post[0], "rwkv_mu_prev": d_mu_prev, "rwkv_mu_next": d_mu_next,
        "rwkv_w0_f": d_pre[0], "rwkv_w2_f": d_pre[1][0:64], "rwkv_w0_b": d_pre[2], "rwkv_w2_b": d_pre[3][0:64],
        "rwkv_a0": d_pre[4], "rwkv_a2": d_pre[5][64:128], "rwkv_g2": d_pre[6], "rwkv_k_k": d_pre[7], "rwkv_k_a": d_pre[8],
        "rwkv_r_k": d_post[3], "rwkv_ln_w": d_post[1], "rwkv_ln_b": d_post[2],
        "norm2_g": dg2, "ffn_conv_b": ff_deinterleave(d_conv_b_p), "norm_f_g": dgf, **late_grads,
    }
    early = [k for k in SHARDED if k not in LATE]
    comm = [] if late_blocks is None else [(_full_to_slices(grads[k], SHARDED[k]), True) for k in early]
    dh1, *got = matmul("proj_in_dx", dp, w_in_p, "nt", F32, tm, D, 512, comm=comm) if comm else \
        [matmul("proj_in_dx", dp, w_in_p, "nt", F32, tm, D, 512)]
    received.update(zip(early, got))
    (grad_x,), (grads["norm1_g"],) = rowwise_bwd("norm1_bwd", f_norm, [(x, D, 0)], [g1], [[(dh1, D, 0)]], tr,
                                                 adds=[(0, (dx1, D, 0))])
    return loss, grad_x, grads, received


MESH = pl.DeviceIdType.MESH


def remote_exchange(name, items):
    n = len(items)

    def body(*refs):
        start, wait = _exchange_plan([sc for _, sc in items], refs[:n], refs[n:2 * n], *refs[2 * n:])
        start()
        wait()

    args, specs, shapes, sems = _comm_specs(items)
    return pl.pallas_call(body, name=name, in_specs=specs, out_specs=specs, out_shape=shapes, scratch_shapes=sems)(*args)


def _exchange_plan(flags, in_refs, out_refs, send_sems, recv_sems, local_sems):
    x, y, c = lax.axis_index("x"), lax.axis_index("y"), lax.axis_index("c")
    me = 4 * x + 2 * y + c

    def peer(k):
        px = 1 - x if (k >> 2) & 1 else x
        py = 1 - y if (k >> 1) & 1 else y
        pc = 1 - c if k & 1 else c
        return (px, py, pc), 4 * px + 2 * py + pc

    def copies():
        own, sends, recvs = [], [], []
        for i, scatter in enumerate(flags):
            src = in_refs[i].at[me] if scatter else in_refs[i]
            own.append(pltpu.make_async_copy(src, out_refs[i].at[me], local_sems.at[i]))
        for k in range(1, N_DEV):
            dev, slot = peer(k)
            for i, scatter in enumerate(flags):
                src = in_refs[i].at[slot] if scatter else in_refs[i]
                pair = dict(send_sem=send_sems.at[i, k - 1], recv_sem=recv_sems.at[i, k - 1], device_id=dev, device_id_type=MESH)
                sends.append(pltpu.make_async_remote_copy(src_ref=src, dst_ref=out_refs[i].at[me], **pair))
                recvs.append(pltpu.make_async_remote_copy(src_ref=out_refs[i].at[slot], dst_ref=out_refs[i].at[slot], **pair))
        return own, sends, recvs

    def start():
        own, sends, _ = copies()
        for cp in own + sends:
            cp.start()

    def wait():
        own, sends, recvs = copies()
        for send, recv in zip(sends, recvs):
            recv.wait_recv()
            send.wait_send()
        for cp in own:
            cp.wait()

    return start, wait


def _adam_tiles(r, c):
    tc = 256 if (c % 256 == 0 and r * c > 128 * 1024) else c
    tr = 128 if (r % 128 == 0 and r > 128) else r
    return tr, tc


def adamw_reduce(name, parts, w, m, v):
    r, c = w.shape
    tr, tc = _adam_tiles(r, c)

    def body(p_ref, w_ref, m_ref, v_ref, g_ref, d_ref, nm_ref, nv_ref):
        g = p_ref[0]
        for d in range(1, N_DEV):
            g = g + p_ref[d]
        nm = ADAM_B1 * m_ref[...] + (1.0 - ADAM_B1) * g
        nv = ADAM_B2 * v_ref[...] + (1.0 - ADAM_B2) * (g * g)
        m_hat = nm / (1.0 - ADAM_B1 ** ADAM_STEP)
        v_hat = nv / (1.0 - ADAM_B2 ** ADAM_STEP)
        g_ref[...] = g
        d_ref[...] = -ADAM_LR * (m_hat / (jnp.sqrt(v_hat) + ADAM_EPS) + ADAM_WD * w_ref[...])
        nm_ref[...] = nm
        nv_ref[...] = nv

    spec = pl.BlockSpec((tr, tc), lambda i, j: (i, j))
    return pl.pallas_call(
        body, name=name, grid=(r // tr, c // tc),
        in_specs=[pl.BlockSpec((N_DEV, tr, tc), lambda i, j: (0, i, j)), spec, spec, spec],
        out_specs=[spec] * 4, out_shape=[jax.ShapeDtypeStruct((r, c), F32)] * 4,
        compiler_params=_cparams(("arbitrary", "arbitrary")),
    )(parts, w, m, v)


SHARDED = {"w_in": 1, "gla_wa2_f": 1, "gla_wa2_b": 1, "gla_proj": 1, "rwkv_w2_f": 1, "rwkv_w2_b": 1, "rwkv_a2": 1,
           "rwkv_g2": 1, "rwkv_proj": 1, "w_out": 0, "ffn_up": 1, "ffn_conv_w": 1, "ffn_down": 0}
BF16_GATHER = ("w_in", "gla_proj", "rwkv_proj", "w_out", "ffn_up", "ffn_down")
REPLICATED = ("norm1_g", "gla_ba_f", "gla_ba_b", "gla_norm_g", "rwkv_mu_prev", "rwkv_mu_next", "rwkv_w0_f", "rwkv_w0_b",
              "rwkv_a0", "rwkv_k_k", "rwkv_k_a", "rwkv_r_k", "rwkv_ln_w", "rwkv_ln_b", "norm2_g", "ffn_conv_b", "norm_f_g")
WEIGHTS = ("norm1_g", "w_in", "gla_wa2_f", "gla_ba_f", "gla_wa2_b", "gla_ba_b", "gla_norm_g", "gla_proj", "rwkv_mu_prev",
           "rwkv_mu_next", "rwkv_w0_f", "rwkv_w2_f", "rwkv_w0_b", "rwkv_w2_b", "rwkv_a0", "rwkv_a2", "rwkv_g2", "rwkv_k_k",
           "rwkv_k_a", "rwkv_r_k", "rwkv_ln_w", "rwkv_ln_b", "rwkv_proj", "w_out", "norm2_g", "ffn_up", "ffn_conv_w",
           "ffn_conv_b", "ffn_down", "norm_f_g")


def _gathered_to_full(g, axis):
    if axis == 0:
        return g.reshape(N_DEV * g.shape[1], g.shape[2])
    return g.transpose(1, 0, 2).reshape(g.shape[1], N_DEV * g.shape[2])


def _full_to_slices(a, axis):
    if axis == 0:
        return a.reshape(N_DEV, a.shape[0] // N_DEV, a.shape[1])
    return a.reshape(a.shape[0], N_DEV, a.shape[1] // N_DEV).transpose(1, 0, 2)


def _pack_rows(size):
    return -(-size // (8 * LANE)) * 8


def _pack(d):
    parts = []
    for k in REPLICATED:
        rows = d[k].reshape(-1, LANE).astype(F32)
        parts.append(jnp.pad(rows, ((0, _pack_rows(rows.size) - rows.shape[0]), (0, 0))))
    return jnp.concatenate(parts, axis=0)


def _unpack(packed, shapes):
    out, pos = {}, 0
    for k in REPLICATED:
        size = int(np.prod(shapes[k]))
        out[k] = packed[pos:pos + size // LANE].reshape(shapes[k])
        pos += _pack_rows(size)
    return out


def kernel(x, norm1_g, w_in, gla_wa2_f, gla_ba_f, gla_wa2_b, gla_ba_b, gla_norm_g, gla_proj, rwkv_mu_prev, rwkv_mu_next, rwkv_w0_f, rwkv_w2_f, rwkv_w0_b, rwkv_w2_b, rwkv_a0, rwkv_a2, rwkv_g2, rwkv_k_k, rwkv_k_a, rwkv_r_k, rwkv_ln_w, rwkv_ln_b, rwkv_proj, w_out, norm2_g, ffn_up, ffn_conv_w, ffn_conv_b, ffn_down, norm_f_g, loss_target, m_norm1_g, m_w_in, m_gla_wa2_f, m_gla_ba_f, m_gla_wa2_b, m_gla_ba_b, m_gla_norm_g, m_gla_proj, m_rwkv_mu_prev, m_rwkv_mu_next, m_rwkv_w0_f, m_rwkv_w2_f, m_rwkv_w0_b, m_rwkv_w2_b, m_rwkv_a0, m_rwkv_a2, m_rwkv_g2, m_rwkv_k_k, m_rwkv_k_a, m_rwkv_r_k, m_rwkv_ln_w, m_rwkv_ln_b, m_rwkv_proj, m_w_out, m_norm2_g, m_ffn_up, m_ffn_conv_w, m_ffn_conv_b, m_ffn_down, m_norm_f_g, v_norm1_g, v_w_in, v_gla_wa2_f, v_gla_ba_f, v_gla_wa2_b, v_gla_ba_b, v_gla_norm_g, v_gla_proj, v_rwkv_mu_prev, v_rwkv_mu_next, v_rwkv_w0_f, v_rwkv_w2_f, v_rwkv_w0_b, v_rwkv_w2_b, v_rwkv_a0, v_rwkv_a2, v_rwkv_g2, v_rwkv_k_k, v_rwkv_k_a, v_rwkv_r_k, v_rwkv_ln_w, v_rwkv_ln_b, v_rwkv_proj, v_w_out, v_norm2_g, v_ffn_up, v_ffn_conv_w, v_ffn_conv_b, v_ffn_down, v_norm_f_g):
    args = locals()
    wts = {k: args[k] for k in WEIGHTS}
    mom = {k: args["m_" + k] for k in WEIGHTS}
    var = {k: args["v_" + k] for k in WEIGHTS}
    shapes = {k: wts[k].shape for k in WEIGHTS}
    nb, t = x.shape[0], x.shape[1]
    mat = lambda a: a.reshape(a.shape[-2], a.shape[-1])

    block = lambda k: mat(wts[k]).astype(MXU_DTYPE) if k in BF16_GATHER else mat(wts[k])
    early = [k for k in SHARDED if k not in LATE]
    gathered = remote_exchange("gather_weights", [(block(k), False) for k in early])
    full = {k: _gathered_to_full(g, SHARDED[k]) for k, g in zip(early, gathered)}
    for k in REPLICATED:
        full[k] = wts[k].reshape(-1) if k in ("norm_f_g", "rwkv_r_k") else wts[k][0]

    loss, grad_x, grads, received = local_step(x.reshape(nb * t, D), loss_target.reshape(nb * t, D), full, nb, t,
                                               late_blocks={k: block(k) for k in LATE})

    (rep_parts,) = remote_exchange("exchange_replicated", [(_pack(grads), False)])

    res = {}
    for k in SHARDED:
        outs = adamw_reduce("adamw_" + k, received[k], mat(wts[k]), mat(mom[k]), mat(var[k]))
        res[k] = [o.reshape(shapes[k]) for o in outs]
    packed = adamw_reduce("adamw_replicated", rep_parts, _pack(wts), _pack(mom), _pack(var))
    unpacked = [_unpack(p, shapes) for p in packed]
    for k in REPLICATED:
        res[k] = [u[k] for u in unpacked]

    total = lax.psum(loss[0, 0], ("x", "y", "c"))
    out = [total, grad_x.reshape(x.shape)]
    for j in range(4):
        out += [res[k][j] for k in WEIGHTS]
    return tuple(out)
```

```python
import functools

import jax
import jax.numpy as jnp
import numpy as np
from jax import lax
from jax.experimental import pallas as pl
from jax.experimental.pallas import tpu as pltpu

F32 = jnp.float32
MXU_DTYPE = jnp.bfloat16

D = 1024
SEQ = 2048
GLA_H, GLA_DK, GLA_DV, GLA_CHUNK = 4, 64, 128, 64
GLA_RANK = 16
GLA_LOGIT_NORM = 16.0
RW_H, RW_N = 8, 64
RW_W = 512
D_FF = 2752
NORM_EPS = 1e-6
HEAD_NORM_EPS = 1e-5
RW_GN_EPS = RW_N * 1e-5
N_DEV = 8
ADAM_LR, ADAM_B1, ADAM_B2, ADAM_EPS, ADAM_WD, ADAM_STEP = 0.001, 0.9, 0.999, 1e-08, 0.01, 10

C_GA, C_GB, C_Q, C_K, C_V, C_OG = 0, 1024, 2048, 2304, 2560, 3072
C_RW = 3584
C_R, C_RK, C_RV, C_WLAL, C_GL = 3584, 4096, 4608, 5120, 5248
C_AFAB = 5376
NP = 5632
RW_PW = 1792
FFP = 2816
LANE = 128
VMEM_LIMIT = 56 * 1024 * 1024


def _cparams(sem):
    return pltpu.CompilerParams(dimension_semantics=sem, vmem_limit_bytes=VMEM_LIMIT)


@jax.custom_vjp
def mm(a, b):
    return jnp.dot(a.astype(MXU_DTYPE), b.astype(MXU_DTYPE), preferred_element_type=F32)


def _mm_fwd(a, b):
    return mm(a, b), (a, b)


def _mm_bwd(res, g):
    a, b = res
    gb = g.astype(MXU_DTYPE)
    da = lax.dot_general(gb, b.astype(MXU_DTYPE), (((1,), (1,)), ((), ())), preferred_element_type=F32)
    db = lax.dot_general(a.astype(MXU_DTYPE), gb, (((0,), (0,)), ((), ())), preferred_element_type=F32)
    return da.astype(a.dtype), db.astype(b.dtype)


mm.defvjp(_mm_fwd, _mm_bwd)


@jax.custom_vjp
def mm_nt(a, b):
    return lax.dot_general(a.astype(MXU_DTYPE), b.astype(MXU_DTYPE), (((1,), (1,)), ((), ())), preferred_element_type=F32)


def _mm_nt_fwd(a, b):
    return mm_nt(a, b), (a, b)


def _mm_nt_bwd(res, g):
    a, b = res
    gb = g.astype(MXU_DTYPE)
    da = jnp.dot(gb, b.astype(MXU_DTYPE), preferred_element_type=F32)
    db = lax.dot_general(gb, a.astype(MXU_DTYPE), (((0,), (0,)), ((), ())), preferred_element_type=F32)
    return da.astype(a.dtype), db.astype(b.dtype)


mm_nt.defvjp(_mm_nt_fwd, _mm_nt_bwd)


@jax.custom_vjp
def mm_tn(a, b):
    return lax.dot_general(a.astype(MXU_DTYPE), b.astype(MXU_DTYPE), (((0,), (0,)), ((), ())), preferred_element_type=F32)


def _mm_tn_fwd(a, b):
    return mm_tn(a, b), (a, b)


def _mm_tn_bwd(res, g):
    a, b = res
    gb = g.astype(MXU_DTYPE)
    da = lax.dot_general(b.astype(MXU_DTYPE), gb, (((1,), (1,)), ((), ())), preferred_element_type=F32)
    db = jnp.dot(a.astype(MXU_DTYPE), gb, preferred_element_type=F32)
    return da.astype(a.dtype), db.astype(b.dtype)


mm_tn.defvjp(_mm_tn_fwd, _mm_tn_bwd)


@functools.partial(jax.custom_vjp, nondiff_argnums=(2, 3))
def sel_dot(x, s, dims, x_first):
    sb = s.astype(MXU_DTYPE)
    hi = x.astype(MXU_DTYPE)
    r1 = x - hi.astype(F32)
    mid = r1.astype(MXU_DTYPE)
    lo = (r1 - mid.astype(F32)).astype(MXU_DTYPE)
    out = None
    for part in (hi, mid, lo):
        ops = (part, sb) if x_first else (sb, part)
        d = lax.dot_general(*ops, (dims, ((), ())), preferred_element_type=F32)
        out = d if out is None else out + d
    return out


def _sel_dot_fwd(x, s, dims, x_first):
    return sel_dot(x, s, dims, x_first), s


def _sel_dot_bwd(dims, x_first, s, g):
    if x_first:
        (cx,), (cs,) = dims
        dx = sel_dot(g, s, ((1,), (1 - cs,)), True) if cx == 1 else sel_dot(g, s, ((1 - cs,), (1,)), False)
    else:
        (cs,), (cx,) = dims
        dx = sel_dot(g, s, ((1 - cs,), (0,)), False) if cx == 0 else sel_dot(g, s, ((0,), (1 - cs,)), True)
    return dx, jnp.zeros_like(s)


sel_dot.defvjp(_sel_dot_fwd, _sel_dot_bwd)


def mm_exact(a, b, b_is_01=True):
    return sel_dot(a, b, ((1,), (0,)), True) if b_is_01 else sel_dot(b, a, ((1,), (0,)), False)


def mm_tn_exact(a, b):
    return sel_dot(a, b, ((0,), (0,)), True)


def _softplus(x):
    return jnp.maximum(x, 0.0) + jnp.log(1.0 + jnp.exp(-jnp.abs(x)))


def _sigmoid(x):
    return jax.nn.sigmoid(x)


def _silu(x):
    return x * _sigmoid(x)


def _rmsnorm(x, g):
    return x * lax.rsqrt(jnp.mean(x * x, axis=-1, keepdims=True) + NORM_EPS) * g


def _segment_ones(width, seg):
    i = lax.broadcasted_iota(jnp.int32, (width, width), 0) // seg
    j = lax.broadcasted_iota(jnp.int32, (width, width), 1) // seg
    return (i == j).astype(F32)


def _row_spec(tm, width, cb):
    return pl.BlockSpec((tm, width), lambda i: (i, cb))


def _full_spec(shape):
    nd = len(shape)
    return pl.BlockSpec(tuple(shape), lambda i: (0,) * nd)


def rowwise_fwd(name, f, rows, params, outs, tm):
    n = rows[0][0].shape[0]
    nr, npar = len(rows), len(params)

    def body(*refs):
        rv = [r[...] for r in refs[:nr]]
        pv = [r[...] for r in refs[nr:nr + npar]]
        res = f(rv, pv)
        for o_ref, val in zip(refs[nr + npar:], res):
            o_ref[...] = val.astype(o_ref.dtype)

    return pl.pallas_call(
        body, name=name, grid=(n // tm,),
        in_specs=[_row_spec(tm, w, cb) for _, w, cb in rows] + [_full_spec(p.shape) for p in params],
        out_specs=[_row_spec(tm, w, 0) for w, _ in outs],
        out_shape=[jax.ShapeDtypeStruct((n, w), dt) for w, dt in outs],
        compiler_params=_cparams(("arbitrary",)),
    )(*[a for a, _, _ in rows], *params)


def rowwise_bwd(name, f, rows, params, douts, tm, adds=(), grad_rows=None):
    n = rows[0][0].shape[0]
    nr, npar = len(rows), len(params)
    grad_rows = list(range(nr)) if grad_rows is None else list(grad_rows)
    flat_d = [d for group in douts for d in group]
    nd, na, ng = len(flat_d), len(adds), len(grad_rows)

    def body(*refs):
        rv = [r[...] for r in refs[:nr]]
        pv = [r[...] for r in refs[nr:nr + npar]]
        dflat = [r[...].astype(F32) for r in refs[nr + npar:nr + npar + nd]]
        av = [r[...] for r in refs[nr + npar + nd:nr + npar + nd + na]]
        o = nr + npar + nd + na
        drow_refs, dpar_refs = refs[o:o + ng], refs[o + ng:o + ng + npar]
        dv, pos = [], 0
        for group in douts:
            dv.append(sum(dflat[pos + 1:pos + len(group)], dflat[pos]))
            pos += len(group)

        @pl.when(pl.program_id(0) == 0)
        def _():
            for r in dpar_refs:
                r[...] = jnp.zeros_like(r)

        def g(grows, pars):
            full = list(rv)
            for i, val in zip(grad_rows, grows):
                full[i] = val
            return f(full, pars)

        res, vjp = jax.vjp(g, [rv[i] for i in grad_rows], pv)
        drows, dpars = vjp([d.astype(r.dtype) for d, r in zip(dv, res)])
        drows = [d.astype(F32) for d in drows]
        for (idx, _), a in zip(adds, av):
            drows[idx] = drows[idx] + a.astype(F32)
        for r, d in zip(drow_refs, drows):
            r[...] = d
        for r, d in zip(dpar_refs, dpars):
            r[...] += d.astype(F32)

    res = pl.pallas_call(
        body, name=name, grid=(n // tm,),
        in_specs=[_row_spec(tm, w, cb) for _, w, cb in rows] + [_full_spec(p.shape) for p in params]
        + [_row_spec(tm, w, cb) for _, w, cb in flat_d] + [_row_spec(tm, w, cb) for _, (_, w, cb) in adds],
        out_specs=[_row_spec(tm, rows[i][1], 0) for i in grad_rows] + [_full_spec(p.shape) for p in params],
        out_shape=[jax.ShapeDtypeStruct((n, rows[i][1]), F32) for i in grad_rows]
        + [jax.ShapeDtypeStruct(p.shape, F32) for p in params],
        compiler_params=_cparams(("arbitrary",)),
    )(*[a for a, _, _ in rows], *params, *[a for a, _, _ in flat_d], *[a for _, (a, _, _) in adds])
    return res[:ng], res[ng:]


def matmul(name, a, b, mode, out_dtype, tm, tn, tk, comm=()):
    nc = len(comm)
    flags = [sc for _, sc in comm]
    if mode == "nn":
        (m, k), n = a.shape, b.shape[1]
        a_spec = pl.BlockSpec((tm, tk), lambda i, j, kk: (i, kk))
        b_spec = pl.BlockSpec((tk, tn), lambda i, j, kk: (kk, j))
        dims = (((1,), (0,)), ((), ()))
    elif mode == "nt":
        (m, k), n = a.shape, b.shape[0]
        a_spec = pl.BlockSpec((tm, tk), lambda i, j, kk: (i, kk))
        b_spec = pl.BlockSpec((tn, tk), lambda i, j, kk: (j, kk))
        dims = (((1,), (1,)), ((), ()))
    else:
        (k, m), n = a.shape, b.shape[1]
        a_spec = pl.BlockSpec((tk, tm), lambda i, j, kk: (kk, i))
        b_spec = pl.BlockSpec((tk, tn), lambda i, j, kk: (kk, j))
        dims = (((0,), (0,)), ((), ()))
    assert m % tm == 0 and n % tn == 0 and k % tk == 0, (name, a.shape, b.shape, tm, tn, tk)
    nk = k // tk
    grid = (m // tm, n // tn, nk)

    def body(*refs):
        a_ref, b_ref, c_in, o_ref = refs[0], refs[1], refs[2:2 + nc], refs[2 + nc]
        c_out, acc_ref, sems = refs[3 + nc:3 + 2 * nc], refs[3 + 2 * nc], refs[4 + 2 * nc:]
        kk = pl.program_id(2)
        step = (pl.program_id(0) * grid[1] + pl.program_id(1)) * nk + kk
        if nc:
            start, wait = _exchange_plan(flags, c_in, c_out, *sems)

            @pl.when(step == 0)
            def _():
                start()

        @pl.when(kk == 0)
        def _():
            acc_ref[...] = jnp.zeros_like(acc_ref)

        acc_ref[...] += lax.dot_general(a_ref[...].astype(MXU_DTYPE), b_ref[...].astype(MXU_DTYPE), dims,
                                        preferred_element_type=F32)

        @pl.when(kk == nk - 1)
        def _():
            o_ref[...] = acc_ref[...].astype(o_ref.dtype)

        if nc:
            @pl.when(step == grid[0] * grid[1] * nk - 1)
            def _():
                wait()

    c_args, c_specs, c_shapes, c_sems = _comm_specs(comm)
    res = pl.pallas_call(
        body, name=name, grid=grid,
        in_specs=[a_spec, b_spec] + c_specs,
        out_specs=[pl.BlockSpec((tm, tn), lambda i, j, kk: (i, j))] + c_specs,
        out_shape=[jax.ShapeDtypeStruct((m, n), out_dtype)] + c_shapes,
        scratch_shapes=[pltpu.VMEM((tm, tn), F32)] + c_sems,
        compiler_params=_cparams(("arbitrary", "arbitrary", "arbitrary")),
    )(a, b, *c_args)
    return res if nc else res[0]


def _prev(u, first):
    return jnp.where(first, 0.0, pltpu.roll(u, 1, 0))


def _next(u, last):
    return jnp.where(last, 0.0, pltpu.roll(u, u.shape[0] - 1, 0))


def _edge_masks(t, w):
    row = lax.broadcasted_iota(jnp.int32, (t, w), 0)
    return row == 0, row == t - 1


SHIFT_CW = 256


def shift_fwd(p, mu_prev, mu_next, nb, t):
    cw, c0 = SHIFT_CW, C_RW // SHIFT_CW

    def body(p_ref, mp_ref, mn_ref, s_ref):
        x = p_ref[...]
        first, last = _edge_masks(t, cw)
        s_ref[...] = x + mp_ref[...] * (_prev(x, first) - x) + mn_ref[...] * (_next(x, last) - x)

    return pl.pallas_call(
        body, name="rwkv_shift_fwd", grid=(nb, RW_PW // cw),
        in_specs=[pl.BlockSpec((t, cw), lambda b, j: (b, c0 + j)), pl.BlockSpec((1, cw), lambda b, j: (0, j)),
                  pl.BlockSpec((1, cw), lambda b, j: (0, j))],
        out_specs=pl.BlockSpec((t, cw), lambda b, j: (b, j)),
        out_shape=jax.ShapeDtypeStruct((nb * t, RW_PW), F32),
        compiler_params=_cparams(("arbitrary", "arbitrary")),
    )(p, mu_prev, mu_next)


def shift_bwd(p, ds, mu_prev, mu_next, nb, t):
    cw, c0 = SHIFT_CW, C_RW // SHIFT_CW

    def body(p_ref, ds_ref, mp_ref, mn_ref, dp_ref, dmp_ref, dmn_ref):
        @pl.when(pl.program_id(1) == 0)
        def _():
            dmp_ref[...] = jnp.zeros_like(dmp_ref)
            dmn_ref[...] = jnp.zeros_like(dmn_ref)

        x, g = p_ref[...], ds_ref[...]
        mp, mn = mp_ref[...], mn_ref[...]
        first, last = _edge_masks(t, cw)
        dp_ref[...] = g * (1.0 - mp - mn) + _next(mp * g, last) + _prev(mn * g, first)
        dmp_ref[...] += jnp.sum(g * (_prev(x, first) - x), axis=0, keepdims=True)
        dmn_ref[...] += jnp.sum(g * (_next(x, last) - x), axis=0, keepdims=True)

    return pl.pallas_call(
        body, name="rwkv_shift_bwd", grid=(RW_PW // cw, nb),
        in_specs=[pl.BlockSpec((t, cw), lambda j, b: (b, c0 + j)), pl.BlockSpec((t, cw), lambda j, b: (b, j)),
                  pl.BlockSpec((1, cw), lambda j, b: (0, j)), pl.BlockSpec((1, cw), lambda j, b: (0, j))],
        out_specs=[pl.BlockSpec((t, cw), lambda j, b: (b, j)), pl.BlockSpec((1, cw), lambda j, b: (0, j)),
                   pl.BlockSpec((1, cw), lambda j, b: (0, j))],
        out_shape=[jax.ShapeDtypeStruct((nb * t, RW_PW), F32), jax.ShapeDtypeStruct((1, RW_PW), F32),
                   jax.ShapeDtypeStruct((1, RW_PW), F32)],
        compiler_params=_cparams(("arbitrary", "arbitrary")),
    )(p, ds, mu_prev, mu_next)


def conv_glu_fwd(u, cw, cb, nb, t):
    def body(u_ref, w_ref, b_ref, z_ref):
        x, w = u_ref[...], w_ref[...]
        first, last = _edge_masks(t, 2 * LANE)
        c = w[0:1] * _prev(x, first) + w[1:2] * x + w[2:3] * _next(x, last) + b_ref[...]
        z_ref[...] = (_silu(c[:, :LANE]) * c[:, LANE:]).astype(z_ref.dtype)

    return pl.pallas_call(
        body, name="conv_glu_fwd", grid=(nb, FFP // LANE),
        in_specs=[pl.BlockSpec((t, 2 * LANE), lambda b, j: (b, j)), pl.BlockSpec((3, 2 * LANE), lambda b, j: (0, j)),
                  pl.BlockSpec((1, 2 * LANE), lambda b, j: (0, j))],
        out_specs=pl.BlockSpec((t, LANE), lambda b, j: (b, j)),
        out_shape=jax.ShapeDtypeStruct((nb * t, FFP), MXU_DTYPE),
        compiler_params=_cparams(("arbitrary", "arbitrary")),
    )(u, cw, cb)


def conv_glu_bwd(u, dz, cw, cb, nb, t):
    def body(u_ref, dz_ref, w_ref, b_ref, du_ref, dw_ref, db_ref):
        @pl.when(pl.program_id(1) == 0)
        def _():
            dw_ref[...] = jnp.zeros_like(dw_ref)
            db_ref[...] = jnp.zeros_like(db_ref)

        x, w, g = u_ref[...], w_ref[...], dz_ref[...]
        first, last = _edge_masks(t, 2 * LANE)
        xp, xn = _prev(x, first), _next(x, last)
        c = w[0:1] * xp + w[1:2] * x + w[2:3] * xn + b_ref[...]
        cg, cv = c[:, :LANE], c[:, LANE:]
        sg = _sigmoid(cg)
        dcg = g * cv * (sg * (1.0 + cg * (1.0 - sg)))
        dcv = g * (cg * sg)
        dc = jnp.concatenate([dcg, dcv], axis=1)
        du = w[1:2] * dc + _next(w[0:1] * dc, last) + _prev(w[2:3] * dc, first)
        du_ref[...] = du.astype(du_ref.dtype)
        dw_ref[0:1, :] += jnp.sum(dc * xp, axis=0, keepdims=True)
        dw_ref[1:2, :] += jnp.sum(dc * x, axis=0, keepdims=True)
        dw_ref[2:3, :] += jnp.sum(dc * xn, axis=0, keepdims=True)
        db_ref[...] += jnp.sum(dc, axis=0, keepdims=True)

    return pl.pallas_call(
        body, name="conv_glu_bwd", grid=(FFP // LANE, nb),
        in_specs=[pl.BlockSpec((t, 2 * LANE), lambda j, b: (b, j)), pl.BlockSpec((t, LANE), lambda j, b: (b, j)),
                  pl.BlockSpec((3, 2 * LANE), lambda j, b: (0, j)), pl.BlockSpec((1, 2 * LANE), lambda j, b: (0, j))],
        out_specs=[pl.BlockSpec((t, 2 * LANE), lambda j, b: (b, j)), pl.BlockSpec((3, 2 * LANE), lambda j, b: (0, j)),
                   pl.BlockSpec((1, 2 * LANE), lambda j, b: (0, j))],
        out_shape=[jax.ShapeDtypeStruct((nb * t, 2 * FFP), MXU_DTYPE), jax.ShapeDtypeStruct((3, 2 * FFP), F32),
                   jax.ShapeDtypeStruct((1, 2 * FFP), F32)],
        compiler_params=_cparams(("arbitrary", "arbitrary")),
    )(u, dz, cw, cb)


def _gla_chunk(q, k, v, afab, wa2p, ba, s_in, reverse):
    c = GLA_CHUNK
    ri = lax.broadcasted_iota(jnp.int32, (c, c), 0)
    ci = lax.broadcasted_iota(jnp.int32, (c, c), 1)
    keep = (ci >= ri) if reverse else (ci <= ri)
    i_ref = (c - 1 - c // 2) if reverse else (c // 2)
    pick_ref = (ci == i_ref).astype(F32)
    ones_cc = jnp.ones((c, c), F32)
    lane = lax.broadcasted_iota(jnp.int32, (1, LANE), 1)
    outs, states = [None] * GLA_H, [None] * GLA_H
    for pr in range(GLA_H // 2):
        la = -_softplus(-(mm(afab, wa2p[pr]) + ba[pr])) * (1.0 / GLA_LOGIT_NORM)
        b = mm_exact(keep.astype(F32), la, b_is_01=False)
        b_ref = mm_exact(pick_ref, b, b_is_01=False)
        b_last = mm_exact(ones_cc, la, b_is_01=False)
        qs = q[pr] * (GLA_DK ** -0.5)
        qi = qs * jnp.exp(b - b_ref)
        ki = k[pr] * jnp.exp(b_ref - b)
        kd = k[pr] * jnp.exp(b_last - b)
        qb = qs * jnp.exp(b)
        dec = jnp.exp(mm_tn_exact(la, jnp.ones((c, LANE), F32)))
        for h in (2 * pr, 2 * pr + 1):
            m = ((lane // GLA_DK) == (h % 2)).astype(F32)
            a = jnp.where(keep, mm_nt(qi * m, ki), 0.0)
            o_intra = mm(a, v[h])
            kv = mm_tn(kd * m, v[h])
            o_inter = mm(qb * m, s_in[h])
            outs[h] = o_intra + o_inter
            states[h] = s_in[h] * dec + kv
    return outs, states


def _gla_load(q_ref, k_ref, v_ref, w_ref, ba_ref, s, rows):
    q = [q_ref[s, rows, pr * LANE:(pr + 1) * LANE] for pr in range(GLA_H // 2)]
    k = [k_ref[s, rows, pr * LANE:(pr + 1) * LANE] for pr in range(GLA_H // 2)]
    v = [v_ref[s, rows, h * GLA_DV:(h + 1) * GLA_DV] for h in range(GLA_H)]
    w = [w_ref[:, pr * LANE:(pr + 1) * LANE] for pr in range(GLA_H // 2)]
    ba = [ba_ref[:, pr * LANE:(pr + 1) * LANE] for pr in range(GLA_H // 2)]
    return q, k, v, w, ba


GLA_TILE = 256
GLA_SB = 4


def _gla_specs(nb, t, reverse):
    tile = min(GLA_TILE, t)
    nt = t // tile
    sb = GLA_SB if nb % GLA_SB == 0 else 1
    return tile, tile // GLA_CHUNK, nt, sb, ((lambda j: nt - 1 - j) if reverse else (lambda j: j))


def gla_fwd(p, wa2p, ba, o_add, nb, t, reverse):
    tile, cpt, nt, sb, tj = _gla_specs(nb, t, reverse)
    has_add = o_add is not None

    def body(*refs):
        if has_add:
            q_ref, k_ref, v_ref, af_ref, w_ref, ba_ref, add_ref, o_ref, hist_ref, s_ref = refs
        else:
            q_ref, k_ref, v_ref, af_ref, w_ref, ba_ref, o_ref, hist_ref, s_ref = refs

        @pl.when(pl.program_id(1) == 0)
        def _():
            s_ref[...] = jnp.zeros_like(s_ref)

        def step(i, carry):
            ci = (cpt - 1 - i) if reverse else i
            rows = pl.ds(pl.multiple_of(ci * GLA_CHUNK, GLA_CHUNK), GLA_CHUNK)
            for s in range(sb):
                s_in = [s_ref[s, h] for h in range(GLA_H)]
                for h in range(GLA_H):
                    hist_ref[s, ci, h] = s_in[h]
                q, k, v, w, ba = _gla_load(q_ref, k_ref, v_ref, w_ref, ba_ref, s, rows)
                outs, states = _gla_chunk(q, k, v, af_ref[s, rows, :], w, ba, s_in, reverse)
                for h in range(GLA_H):
                    oh = outs[h]
                    if has_add:
                        oh = oh + add_ref[s, rows, h * GLA_DV:(h + 1) * GLA_DV]
                    o_ref[s, rows, h * GLA_DV:(h + 1) * GLA_DV] = oh
                    s_ref[s, h] = states[h]
            return carry

        lax.fori_loop(0, cpt, step, 0)

    col = lambda width, c0: pl.BlockSpec((sb, tile, width), lambda b, j: (b, tj(j), c0 // width))
    in_specs = [col(256, C_Q), col(256, C_K), col(512, C_V), col(LANE, C_AFAB),
                pl.BlockSpec((LANE, 256), lambda b, j: (0, 0)), pl.BlockSpec((1, 256), lambda b, j: (0, 0))]
    p3 = p.reshape(nb, t, p.shape[1])
    args = [p3, p3, p3, p3, wa2p, ba]
    if has_add:
        in_specs.append(col(512, 0))
        args.append(o_add.reshape(nb, t, 512))
    o, hist = pl.pallas_call(
        body, name="gla_fwd_rev" if reverse else "gla_fwd", grid=(nb // sb, nt),
        in_specs=in_specs,
        out_specs=[col(512, 0), pl.BlockSpec((sb, cpt, GLA_H, LANE, LANE), lambda b, j: (b, tj(j), 0, 0, 0))],
        out_shape=[jax.ShapeDtypeStruct((nb, t, 512), F32),
                   jax.ShapeDtypeStruct((nb, t // GLA_CHUNK, GLA_H, LANE, LANE), F32)],
        scratch_shapes=[pltpu.VMEM((sb, GLA_H, LANE, LANE), F32)],
        compiler_params=_cparams(("arbitrary", "arbitrary")),
    )(*args)
    return o.reshape(nb * t, 512), hist


def gla_bwd(p, wa2p, ba, hist, do, dprev, nb, t, reverse):
    tile, cpt, nt, sb, tj_f = _gla_specs(nb, t, reverse)
    tj = lambda j: tj_f(nt - 1 - j)
    has_prev = dprev is not None

    def body(*refs):
        if has_prev:
            q_ref, k_ref, v_ref, af_ref, w_ref, ba_ref, hist_ref, do_ref, prev_ref, dqkv_ref, dw_ref, dba_ref, ds_ref = refs
        else:
            q_ref, k_ref, v_ref, af_ref, w_ref, ba_ref, hist_ref, do_ref, dqkv_ref, dw_ref, dba_ref, ds_ref = refs

        @pl.when((pl.program_id(0) == 0) & (pl.program_id(1) == 0))
        def _():
            dw_ref[...] = jnp.zeros_like(dw_ref)
            dba_ref[...] = jnp.zeros_like(dba_ref)

        @pl.when(pl.program_id(1) == 0)
        def _():
            ds_ref[...] = jnp.zeros_like(ds_ref)

        def step(i, carry):
            ci = i if reverse else (cpt - 1 - i)
            rows = pl.ds(pl.multiple_of(ci * GLA_CHUNK, GLA_CHUNK), GLA_CHUNK)
            fn = functools.partial(_gla_chunk, reverse=reverse)
            for s in range(sb):
                s_in = [hist_ref[s, ci, h] for h in range(GLA_H)]
                q, k, v, w, ba = _gla_load(q_ref, k_ref, v_ref, w_ref, ba_ref, s, rows)
                _, vjp = jax.vjp(fn, q, k, v, af_ref[s, rows, :], w, ba, s_in)
                d_o = [do_ref[s, rows, h * GLA_DV:(h + 1) * GLA_DV] for h in range(GLA_H)]
                d_s = [ds_ref[s, h] for h in range(GLA_H)]
                dq, dk, dv, daf, dw, dba, ds_in = vjp((d_o, d_s))
                pieces = [(pr * LANE, dq[pr]) for pr in range(2)] + [(256 + pr * LANE, dk[pr]) for pr in range(2)]
                pieces += [(512 + h * GLA_DV, dv[h]) for h in range(GLA_H)] + [(1024, daf)]
                for c0, val in pieces:
                    if has_prev:
                        val = val + prev_ref[s, rows, c0:c0 + LANE]
                    dqkv_ref[s, rows, c0:c0 + LANE] = val
                for pr in range(2):
                    dw_ref[:, pr * LANE:(pr + 1) * LANE] += dw[pr]
                    dba_ref[:, pr * LANE:(pr + 1) * LANE] += dba[pr]
                for h in range(GLA_H):
                    ds_ref[s, h] = ds_in[h]
            return carry

        lax.fori_loop(0, cpt, step, 0)

    col = lambda width, c0: pl.BlockSpec((sb, tile, width), lambda b, j: (b, tj(j), c0 // width))
    in_specs = [col(256, C_Q), col(256, C_K), col(512, C_V), col(LANE, C_AFAB),
                pl.BlockSpec((LANE, 256), lambda b, j: (0, 0)), pl.BlockSpec((1, 256), lambda b, j: (0, 0)),
                pl.BlockSpec((sb, cpt, GLA_H, LANE, LANE), lambda b, j: (b, tj(j), 0, 0, 0)), col(512, 0)]
    p3 = p.reshape(nb, t, p.shape[1])
    args = [p3, p3, p3, p3, wa2p, ba, hist, do.reshape(nb, t, 512)]
    if has_prev:
        in_specs.append(col(1152, 0))
        args.append(dprev.reshape(nb, t, 1152))
    dqkv, dw, dba = pl.pallas_call(
        body, name="gla_bwd_rev" if reverse else "gla_bwd", grid=(nb // sb, nt),
        in_specs=in_specs,
        out_specs=[col(1152, 0), pl.BlockSpec((LANE, 256), lambda b, j: (0, 0)), pl.BlockSpec((1, 256), lambda b, j: (0, 0))],
        out_shape=[jax.ShapeDtypeStruct((nb, t, 1152), F32), jax.ShapeDtypeStruct((LANE, 256), F32),
                   jax.ShapeDtypeStruct((1, 256), F32)],
        scratch_shapes=[pltpu.VMEM((sb, GLA_H, LANE, LANE), F32)],
        compiler_params=_cparams(("arbitrary", "arbitrary")),
    )(*args)
    return dqkv.reshape(nb * t, 1152), dw, dba


SCAN_TB = 8
RW_VH = RW_N // 2


def _bwd_lanes():
    lane = lax.broadcasted_iota(jnp.int32, (1, LANE), 1)
    return ((lane // (LANE // 4)) % 2) == 1


def _comm_specs(comm):
    anyspec = pl.BlockSpec(memory_space=pl.ANY)
    n = len(comm)
    shapes = [jax.ShapeDtypeStruct((N_DEV,) + (a.shape[1:] if sc else a.shape), a.dtype) for a, sc in comm]
    sems = [pltpu.SemaphoreType.DMA((n, N_DEV - 1)), pltpu.SemaphoreType.DMA((n, N_DEV - 1)), pltpu.SemaphoreType.DMA((n,))] if n else []
    return [a for a, _ in comm], [anyspec] * n, shapes, sems


def rwkv_scan_fwd(r, w, k, a, b, v, comm=()):
    t = r.shape[0]
    nt = t // SCAN_TB
    nc = len(comm)
    flags = [sc for _, sc in comm]

    def body(*refs):
        (rf, rm, kf, km, af, am, bf, bm, wf_ref, wm_ref, vf, vm), refs = refs[:12], refs[12:]
        c_in, refs = refs[:nc], refs[nc:]
        (yf_ref, ym_ref, hist_ref, sa_ref), refs = refs[:4], refs[4:]
        c_out, refs = refs[:nc], refs[nc:]
        s_ref, sems = refs[0], refs[1:]
        i = pl.program_id(0)
        if nc:
            start, wait = _exchange_plan(flags, c_in, c_out, *sems)

        @pl.when(i == 0)
        def _():
            s_ref[...] = jnp.zeros_like(s_ref)
            if nc:
                start()

        bwd = _bwd_lanes()

        for tt in range(SCAN_TB):
            mt = SCAN_TB - 1 - tt
            pick = lambda f_ref, m_ref: jnp.where(bwd, m_ref[mt], f_ref[tt])
            rt, kt, at, bt, wt = pick(rf, rm), pick(kf, km), pick(af, am), pick(bf, bm), pick(wf_ref, wm_ref)
            for vi in range(RW_VH):
                sv = s_ref[vi] if tt == 0 else hist_ref[tt - 1, vi]
                sa = jnp.sum(sv * at, axis=0, keepdims=True)
                v_row = jnp.where(bwd, vm[mt, vi:vi + 1, :], vf[tt, vi:vi + 1, :])
                sn = sv * wt + sa * bt + v_row * kt
                hist_ref[tt, vi] = sn
                y_row = jnp.sum(sn * rt, axis=0, keepdims=True)
                yf_ref[tt, vi:vi + 1, :] = y_row
                ym_ref[mt, vi:vi + 1, :] = y_row
                sa_ref[tt, vi:vi + 1, :] = sa
        s_ref[...] = hist_ref[SCAN_TB - 1]

        if nc:
            @pl.when(i == nt - 1)
            def _():
                wait()

    fwd_map, mir_map = (lambda i: (i, 0, 0)), (lambda i: (nt - 1 - i, 0, 0))
    kf_spec, km_spec = pl.BlockSpec((SCAN_TB, RW_N, LANE), fwd_map), pl.BlockSpec((SCAN_TB, RW_N, LANE), mir_map)
    vf_spec, vm_spec = pl.BlockSpec((SCAN_TB, RW_VH, LANE), fwd_map), pl.BlockSpec((SCAN_TB, RW_VH, LANE), mir_map)
    c_args, c_specs, c_shapes, c_sems = _comm_specs(comm)
    vshape = jax.ShapeDtypeStruct((t, RW_VH, LANE), F32)
    return pl.pallas_call(
        body, name="rwkv_scan_fwd", grid=(nt,),
        in_specs=[kf_spec, km_spec] * 5 + [vf_spec, vm_spec] + c_specs,
        out_specs=[vf_spec, vm_spec, pl.BlockSpec((SCAN_TB, RW_VH, RW_N, LANE), lambda i: (i, 0, 0, 0)), vf_spec] + c_specs,
        out_shape=[vshape, vshape, jax.ShapeDtypeStruct((t, RW_VH, RW_N, LANE), F32), vshape] + c_shapes,
        scratch_shapes=[pltpu.VMEM((RW_VH, RW_N, LANE), F32)] + c_sems,
        compiler_params=_cparams(("arbitrary",)),
    )(r, r, k, k, a, a, b, b, w, w, v, v, *c_args)


def rwkv_scan_bwd(r, w, k, a, b, v, hist, sa, dy, comm=()):
    t = r.shape[0]
    nt = t // SCAN_TB
    nc = len(comm)
    flags = [sc for _, sc in comm]

    def body(*refs):
        (rf, rm, kf, km, af, am, bf, bm, wf_ref, wm_ref, vf, vm, hist_ref, prev_ref, sa_ref, dyf, dym), refs = refs[:17], refs[17:]
        c_in, refs = refs[:nc], refs[nc:]
        k_outs, (dvf_ref, dvm_ref), refs = refs[:4], refs[4:6], refs[6:]
        c_out, refs = refs[:nc], refs[nc:]
        ds_ref, sems = refs[0], refs[1:]
        i = pl.program_id(0)
        if nc:
            start, wait = _exchange_plan(flags, c_in, c_out, *sems)

        @pl.when(i == 0)
        def _():
            ds_ref[...] = jnp.zeros_like(ds_ref)
            if nc:
                start()

        bwd = _bwd_lanes()
        group = lax.broadcasted_iota(jnp.int32, (1, LANE), 1) // RW_Q
        first_block = i == nt - 1

        for tt in range(SCAN_TB - 1, -1, -1):
            mt = SCAN_TB - 1 - tt
            pick = lambda f_ref, m_ref: jnp.where(bwd, m_ref[mt], f_ref[tt])
            rt, kt, at, bt, wt = pick(rf, rm), pick(kf, km), pick(af, am), pick(bf, bm), pick(wf_ref, wm_ref)
            zero = jnp.zeros((RW_N, LANE), F32)
            dr, dw, dk, da, db = zero, zero, zero, zero, zero
            for vi in range(RW_VH):
                sn = hist_ref[tt, vi]
                sv = hist_ref[tt - 1, vi] if tt > 0 else jnp.where(first_block, 0.0, prev_ref[0, vi])
                sa_row = sa_ref[tt, vi:vi + 1, :]
                v_row = jnp.where(bwd, vm[mt, vi:vi + 1, :], vf[tt, vi:vi + 1, :])
                dy_row = jnp.where(bwd, dym[mt, vi:vi + 1, :], dyf[tt, vi:vi + 1, :])
                dsv = ds_ref[vi] + dy_row * rt
                dr = dr + sn * dy_row
                dsa = jnp.sum(dsv * bt, axis=0, keepdims=True)
                dw = dw + sv * dsv
                db = db + dsv * sa_row
                dk = dk + dsv * v_row
                dv_row = jnp.sum(dsv * kt, axis=0, keepdims=True)
                dvf_ref[tt, vi:vi + 1, :] = dv_row
                dvm_ref[mt, vi:vi + 1, :] = dv_row
                da = da + sv * dsa
                ds_ref[vi] = dsv * wt + dsa * at
            dr, dw, dk, da, db = [val + pltpu.roll(val, LANE // 2, 1) for val in (dr, dw, dk, da, db)]
            up, down = (lambda val: pltpu.roll(val, RW_Q, 1)), (lambda val: pltpu.roll(val, LANE - RW_Q, 1))
            packed_f = jnp.where(group == 0, dr, jnp.where(group == 1, up(dk), jnp.where(group == 2, da, up(db))))
            packed_m = jnp.where(group == 0, down(dr), jnp.where(group == 1, dk, jnp.where(group == 2, down(da), db)))
            k_outs[0][tt] = packed_f
            k_outs[1][mt] = packed_m
            k_outs[2][tt] = dw
            k_outs[3][mt] = dw

        if nc:
            @pl.when(i == nt - 1)
            def _():
                wait()

    fwd_map, mir_map = (lambda i: (nt - 1 - i, 0, 0)), (lambda i: (i, 0, 0))
    kf_spec, km_spec = pl.BlockSpec((SCAN_TB, RW_N, LANE), fwd_map), pl.BlockSpec((SCAN_TB, RW_N, LANE), mir_map)
    vf_spec, vm_spec = pl.BlockSpec((SCAN_TB, RW_VH, LANE), fwd_map), pl.BlockSpec((SCAN_TB, RW_VH, LANE), mir_map)
    prev_spec = pl.BlockSpec((1, RW_VH, RW_N, LANE), lambda i: (jnp.maximum((nt - 1 - i) * SCAN_TB - 1, 0), 0, 0, 0))
    c_args, c_specs, c_shapes, c_sems = _comm_specs(comm)
    kshape, vshape = jax.ShapeDtypeStruct((t, RW_N, LANE), F32), jax.ShapeDtypeStruct((t, RW_VH, LANE), F32)
    return pl.pallas_call(
        body, name="rwkv_scan_bwd", grid=(nt,),
        in_specs=[kf_spec, km_spec] * 5 + [vf_spec, vm_spec,
                                           pl.BlockSpec((SCAN_TB, RW_VH, RW_N, LANE), lambda i: (nt - 1 - i, 0, 0, 0)),
                                           prev_spec, vf_spec, vf_spec, vm_spec] + c_specs,
        out_specs=[kf_spec, km_spec] * 2 + [vf_spec, vm_spec] + c_specs,
        out_shape=[kshape] * 4 + [vshape] * 2 + c_shapes,
        scratch_shapes=[pltpu.VMEM((RW_VH, RW_N, LANE), F32)] + c_sems,
        compiler_params=_cparams(("arbitrary",)),
    )(r, r, k, k, a, a, b, b, w, w, v, v, hist, hist, sa, dy, dy, *c_args)


RELAYOUT_TB = 128
RW_Q = LANE // 4


def to_scan(name, x, cb, nb, t, value, x_bwd=None):
    tb = min(RELAYOUT_TB, t)
    rows_out = RW_VH if value else RW_N
    ins = [x] if x_bwd is None else [x, x_bwd]

    def body(*refs):
        x_refs, o_ref, scrs = refs[:len(ins)], refs[len(ins)], refs[len(ins) + 1:]
        for x_ref, scr in zip(x_refs, scrs):
            for b in range(nb):
                scr[b * RW_H:(b + 1) * RW_H] = x_ref[b].T.reshape(RW_H, RW_N, tb)
        for j in range(rows_out):
            lo = scrs[0][:, j, :]
            if value:
                hi = scrs[0][:, j + RW_VH, :]
                blk = [lo, lo, hi, hi]
            else:
                other = lo if x_bwd is None else scrs[1][:, j, :]
                blk = [lo, other, lo, other]
            o_ref[:, j, :] = jnp.concatenate(blk, axis=0).T

    return pl.pallas_call(
        body, name=name, grid=(t // tb,),
        in_specs=[pl.BlockSpec((nb, tb, RW_W), lambda i: (0, i, cb))] + [pl.BlockSpec((nb, tb, RW_W), lambda i: (0, i, 0))] * (len(ins) - 1),
        out_specs=pl.BlockSpec((tb, rows_out, LANE), lambda i: (i, 0, 0)),
        out_shape=jax.ShapeDtypeStruct((t, rows_out, LANE), F32),
        scratch_shapes=[pltpu.VMEM((nb * RW_H, RW_N, tb), F32)] * len(ins),
        compiler_params=_cparams(("arbitrary",)),
    )(*[a.reshape(nb, t, a.shape[1]) for a in ins])


def from_scan(name, xf, xm, nb, t, value, picks=((0, 1),)):
    tb = min(RELAYOUT_TB, t)
    rows_in = RW_VH if value else RW_N
    n_out = 1 if value else len(picks)
    grp = lambda a, g: a[g * RW_Q:(g + 1) * RW_Q]

    def body(f_ref, m_ref, *rest):
        outs, scrs = rest[:n_out], rest[n_out:]
        for j in range(rows_in):
            a, b = f_ref[:, j, :].T, m_ref[:, j, :].T
            if value:
                scrs[0][:, j, :] = grp(a, 0) + grp(b, 1)
                scrs[0][:, j + RW_VH, :] = grp(a, 2) + grp(b, 3)
            else:
                for scr, (gf, gm) in zip(scrs, picks):
                    parts = ([grp(a, gf)] if gf is not None else []) + ([grp(b, gm)] if gm is not None else [])
                    scr[:, j, :] = parts[0] if len(parts) == 1 else parts[0] + parts[1]
        for o_ref, scr in zip(outs, scrs):
            for b in range(nb):
                o_ref[b] = scr[b * RW_H:(b + 1) * RW_H].reshape(RW_W, tb).T

    res = pl.pallas_call(
        body, name=name, grid=(t // tb,),
        in_specs=[pl.BlockSpec((tb, rows_in, LANE), lambda i: (i, 0, 0))] * 2,
        out_specs=[pl.BlockSpec((nb, tb, RW_W), lambda i: (0, i, 0))] * n_out,
        out_shape=[jax.ShapeDtypeStruct((nb, t, RW_W), F32)] * n_out,
        scratch_shapes=[pltpu.VMEM((nb * RW_H, RW_N, tb), F32)] * n_out,
        compiler_params=_cparams(("arbitrary",)),
    )(xf, xm)
    return [r.reshape(nb * t, RW_W) for r in res]


def f_norm(rows, params):
    (x,), (g,) = rows, params
    return [_rmsnorm(x, g)]


def f_rwkv_pre(rows, params):
    k, wlal, gl = rows
    w0f, w2f, w0b, w2b, a0, a2, g2, k_k, k_a = params
    seg = _segment_ones(RW_W, RW_N)
    tw = jnp.tanh(wlal)

    def decay(w0, w2):
        return jnp.exp(-jnp.exp(-_softplus(-(w0 + mm(tw, w2))) - 0.5))

    lr = _sigmoid(a0 + mm(wlal, a2))
    gate = mm(_sigmoid(gl), g2)
    kk = k * k_k
    kk = kk / jnp.maximum(jnp.sqrt(mm_exact(kk * kk, seg)), 1e-12)
    kp = k * (1.0 + (lr - 1.0) * k_a)
    return [decay(w0f, w2f), decay(w0b, w2b), kp, -kk, kk * lr, gate]


def f_branch_post(rows, params):
    o, og, y, r, kp, v, g = rows
    gla_g, ln_w, ln_b, r_k = params
    seg_gla = _segment_ones(GLA_H * GLA_DV, GLA_DV)
    seg_rw = _segment_ones(RW_W, RW_N)
    on = o * lax.rsqrt(mm_exact(o * o, seg_gla) * (1.0 / GLA_DV) + HEAD_NORM_EPS)
    oa = on * gla_g * _silu(og)
    mu = mm_exact(y, seg_rw) * (1.0 / RW_N)
    yc = y - mu
    var = mm_exact(yc * yc, seg_rw) * (1.0 / RW_N)
    yn = yc * lax.rsqrt(var + RW_GN_EPS) * ln_w + ln_b
    bonus = mm_exact(r * kp * r_k, seg_rw) * v
    return [oa, (yn + bonus) * g]


def f_merge(rows, params):
    ga, gb, ya, yb = rows
    return [_sigmoid(ga) * ya + _sigmoid(gb) * yb]


def f_norm2(rows, params):
    (x, mo), (g,) = rows, params
    x1 = x + mo
    return [x1, _rmsnorm(x1, g)]


def loss_head(x1, ffo, tgt, gf, tm):
    n = x1.shape[0]

    def body(x1_ref, f_ref, t_ref, g_ref, loss_ref, dx_ref, dg_ref):
        @pl.when(pl.program_id(0) == 0)
        def _():
            loss_ref[...] = jnp.zeros_like(loss_ref)
            dg_ref[...] = jnp.zeros_like(dg_ref)

        tgt_v = t_ref[...]

        def f(x2, g):
            err = _rmsnorm(x2, g) - tgt_v
            return jnp.sum(jnp.sum(err * err, axis=-1, keepdims=True), axis=0, keepdims=True) * (0.5 / D)

        val, vjp = jax.vjp(f, x1_ref[...] + f_ref[...], g_ref[...])
        dx, dg = vjp(jnp.ones((1, 1), F32))
        loss_ref[...] += val
        dx_ref[...] = dx
        dg_ref[...] += dg

    return pl.pallas_call(
        body, name="loss_head", grid=(n // tm,),
        in_specs=[_row_spec(tm, D, 0)] * 3 + [_full_spec((1, D))],
        out_specs=[_full_spec((1, 1)), _row_spec(tm, D, 0), _full_spec((1, D))],
        out_shape=[jax.ShapeDtypeStruct((1, 1), F32), jax.ShapeDtypeStruct((n, D), F32), jax.ShapeDtypeStruct((1, D), F32)],
        compiler_params=_cparams(("arbitrary",)),
    )(x1, ffo, tgt, gf)


def _pad_cols(a, width):
    return jnp.pad(a, ((0, 0), (0, width - a.shape[1])))


def w_in_to_padded(w):
    return _pad_cols(jnp.concatenate([w[:, 3360:5408], w[:, 0:1536], w[:, 1568:3360], w[:, 1536:1568]], axis=1), NP)


def w_in_from_padded(wp):
    return jnp.concatenate([wp[:, 2048:3584], wp[:, 5376:5408], wp[:, 3584:5376], wp[:, 0:2048]], axis=1)


def ff_interleave(a):
    r = a.shape[0]
    halves = jnp.stack([_pad_cols(a[:, :D_FF], FFP), _pad_cols(a[:, D_FF:], FFP)], axis=1)
    return halves.reshape(r, 2, FFP // LANE, LANE).transpose(0, 2, 1, 3).reshape(r, 2 * FFP)


def ff_deinterleave(a):
    r = a.shape[0]
    halves = a.reshape(r, FFP // LANE, 2, LANE).transpose(0, 2, 1, 3).reshape(r, 2, FFP)
    return halves[:, :, :D_FF].reshape(r, 2 * D_FF)


def _rows_into(w, rows, off):
    return jnp.zeros((rows, w.shape[1]), w.dtype).at[off:off + w.shape[0]].set(w)


LATE = ("gla_proj", "rwkv_proj", "w_out", "ffn_up", "ffn_conv_w", "ffn_down")


def local_step(x, tgt, w, nb, t, late_blocks=None):
    n = nb * t
    tm = min(n, 1024)
    tr = min(n, 256)
    vec = lambda a: a.reshape(1, -1)
    w = dict(w)

    w_in_p = w_in_to_padded(w["w_in"])
    wa2_f, wa2_b = _rows_into(w["gla_wa2_f"], LANE, 0), _rows_into(w["gla_wa2_b"], LANE, GLA_RANK)
    w2f, w2b = _rows_into(w["rwkv_w2_f"], LANE, 0), _rows_into(w["rwkv_w2_b"], LANE, 0)
    a2 = _rows_into(w["rwkv_a2"], LANE, 64)
    g1, g2n, gf = vec(w["norm1_g"]), vec(w["norm2_g"]), vec(w["norm_f_g"])
    mu_prev, mu_next = vec(w["rwkv_mu_prev"]), vec(w["rwkv_mu_next"])
    pre_params = [vec(w["rwkv_w0_f"]), w2f, vec(w["rwkv_w0_b"]), w2b, vec(w["rwkv_a0"]), a2, w["rwkv_g2"],
                  vec(w["rwkv_k_k"]), vec(w["rwkv_k_a"])]
    post_params = [vec(w["gla_norm_g"]), vec(w["rwkv_ln_w"]), vec(w["rwkv_ln_b"]), vec(w["rwkv_r_k"])]
    ba_f, ba_b = vec(w["gla_ba_f"]), vec(w["gla_ba_b"])

    (h1,) = rowwise_fwd("norm1_fwd", f_norm, [(x, D, 0)], [g1], [(D, MXU_DTYPE)], tr)
    p = matmul("proj_in", h1, w_in_p, "nn", F32, tm, 512, D)
    s = shift_fwd(p, mu_prev, mu_next, nb, t)
    pre_rows = [(s, 512, 1), (s, LANE, 1536 // LANE), (s, LANE, 1664 // LANE)]
    wf, wb, kp, a_s, b_s, g = rowwise_fwd("rwkv_pre_fwd", f_rwkv_pre, pre_rows, pre_params, [(RW_W, F32)] * 6, tr)
    sc = [to_scan("to_scan_r", s, 0, nb, t, False), to_scan("to_scan_w", wf, 0, nb, t, False, x_bwd=wb),
          to_scan("to_scan_k", kp, 0, nb, t, False), to_scan("to_scan_a", a_s, 0, nb, t, False),
          to_scan("to_scan_b", b_s, 0, nb, t, False), to_scan("to_scan_v", s, 2, nb, t, True)]
    comm = [] if late_blocks is None else [(late_blocks[k], False) for k in LATE]
    y_scf, y_scm, hist_rw, sa_sc, *gathered = rwkv_scan_fwd(*sc, comm=comm)
    for k, g_k in zip(LATE, gathered):
        w[k] = _gathered_to_full(g_k, SHARDED[k])
    ffn_up_p = ff_interleave(w["ffn_up"])
    conv_w_p, conv_b_p = ff_interleave(w["ffn_conv_w"]), ff_interleave(vec(w["ffn_conv_b"]))
    ffn_down_p = jnp.pad(w["ffn_down"], ((0, FFP - D_FF), (0, 0)))
    (y,) = from_scan("from_scan_y", y_scf, y_scm, nb, t, True)
    o_f, hist_f = gla_fwd(p, wa2_f, ba_f, None, nb, t, False)
    o, hist_b = gla_fwd(p, wa2_b, ba_b, o_f, nb, t, True)
    post_rows = [(o, 512, 0), (p, 512, C_OG // 512), (y, 512, 0), (s, 512, 0), (kp, 512, 0), (s, 512, 2), (g, 512, 0)]
    oa, ob = rowwise_fwd("branch_post_fwd", f_branch_post, post_rows, post_params, [(512, MXU_DTYPE)] * 2, tr)
    ya = matmul("gla_proj", oa, w["gla_proj"], "nn", F32, tm, 512, 512)
    yb = matmul("rwkv_proj", ob, w["rwkv_proj"], "nn", F32, tm, 512, 512)
    merge_rows = [(p, D, 0), (p, D, 1), (ya, D, 0), (yb, D, 0)]
    (merged,) = rowwise_fwd("merge_fwd", f_merge, merge_rows, [], [(D, MXU_DTYPE)], tr)
    mo = matmul("w_out", merged, w["w_out"], "nn", F32, tm, 512, D)
    x1, h2 = rowwise_fwd("norm2_fwd", f_norm2, [(x, D, 0), (mo, D, 0)], [g2n], [(D, F32), (D, MXU_DTYPE)], tr)
    u = matmul("ffn_up", h2, ffn_up_p, "nn", F32, tm, 512, D)
    z = conv_glu_fwd(u, conv_w_p, conv_b_p, nb, t)
    ffo = matmul("ffn_down", z, ffn_down_p, "nn", F32, tm, 512, FFP // 2)
    loss, dx2, dgf = loss_head(x1, ffo, tgt, gf, tr)

    dz = matmul("ffn_down_dx", dx2, ffn_down_p, "nt", F32, tm, FFP // 2, D)
    d_ffn_down_p = matmul("ffn_down_dw", z, dx2, "tn", F32, FFP // 2, 512, tm)
    du, d_conv_w_p, d_conv_b_p = conv_glu_bwd(u, dz, conv_w_p, conv_b_p, nb, t)
    dh2 = matmul("ffn_up_dx", du, ffn_up_p, "nt", F32, tm, D, 512)
    d_ffn_up_p = matmul("ffn_up_dw", h2, du, "tn", F32, D, 512, tm)
    (dx1,), (dg2,) = rowwise_bwd("norm2_bwd", f_norm2, [(x, D, 0), (mo, D, 0)], [g2n],
                                 [[(dx2, D, 0)], [(dh2, D, 0)]], tr, grad_rows=[1])
    dmerged = matmul("w_out_dx", dx1, w["w_out"], "nt", F32, tm, D, 512)
    d_w_out = matmul("w_out_dw", merged, dx1, "tn", F32, D, 512, tm)
    (dga, dgb, dya, dyb), _ = rowwise_bwd("merge_bwd", f_merge, merge_rows, [], [[(dmerged, D, 0)]], tr)
    d_oa = matmul("gla_proj_dx", dya, w["gla_proj"], "nt", F32, tm, 512, D)
    d_gla_proj = matmul("gla_proj_dw", oa, dya, "tn", F32, 512, 512, tm)
    d_ob = matmul("rwkv_proj_dx", dyb, w["rwkv_proj"], "nt", F32, tm, 512, D)
    d_rwkv_proj = matmul("rwkv_proj_dw", ob, dyb, "tn", F32, 512, 512, tm)
    (d_o, d_og, d_y, d_r_post, d_kp_post, d_v_post, d_g), d_post = rowwise_bwd(
        "branch_post_bwd", f_branch_post, post_rows, post_params, [[(d_oa, 512, 0)], [(d_ob, 512, 0)]], tr)
    late_grads = {"gla_proj": d_gla_proj, "rwkv_proj": d_rwkv_proj, "w_out": d_w_out, "ffn_up": ff_deinterleave(d_ffn_up_p),
                  "ffn_conv_w": ff_deinterleave(d_conv_w_p), "ffn_down": d_ffn_down_p[0:D_FF]}
    comm = [] if late_blocks is None else [(_full_to_slices(late_grads[k], SHARDED[k]), True) for k in LATE]
    dsc = rwkv_scan_bwd(*sc, hist_rw, sa_sc, to_scan("to_scan_dy", d_y, 0, nb, t, True), comm=comm)
    received = dict(zip(LATE, dsc[6:]))
    d_r_scan, d_kp_scan, d_a_scan, d_b_scan = from_scan("from_scan_rkab", dsc[0], dsc[1], nb, t, False,
                                                        picks=((0, 0), (1, 1), (2, 2), (3, 3)))
    d_wf, d_wb = from_scan("from_scan_w", dsc[2], dsc[3], nb, t, False, picks=((0, None), (None, 1)))
    (d_v_scan,) = from_scan("from_scan_dv", dsc[4], dsc[5], nb, t, True)
    (d_k, d_wlal, d_gl), d_pre = rowwise_bwd(
        "rwkv_pre_bwd", f_rwkv_pre, pre_rows, pre_params,
        [[(d_wf, 512, 0)], [(d_wb, 512, 0)], [(d_kp_scan, 512, 0), (d_kp_post, 512, 0)],
         [(d_a_scan, 512, 0)], [(d_b_scan, 512, 0)], [(d_g, 512, 0)]], tr)
    ds = jnp.concatenate([d_r_scan + d_r_post, d_k, d_v_scan + d_v_post, d_wlal, d_gl], axis=1)
    dp_rw, d_mu_prev, d_mu_next = shift_bwd(p, ds, mu_prev, mu_next, nb, t)
    dqkv_f, d_wa2_f, d_ba_f = gla_bwd(p, wa2_f, ba_f, hist_f, d_o, None, nb, t, False)
    dqkv, d_wa2_b, d_ba_b = gla_bwd(p, wa2_b, ba_b, hist_b, d_o, dqkv_f, nb, t, True)
    dp = jnp.concatenate([dga, dgb, dqkv[:, 0:1024], d_og, dp_rw, dqkv[:, 1024:1152],
                          jnp.zeros((n, NP - C_AFAB - LANE), F32)], axis=1)
    d_w_in_p = matmul("proj_in_dw", h1, dp, "tn", F32, D, 512, tm)
    grads = {
        "w_in": w_in_from_padded(d_w_in_p),
        "gla_wa2_f": d_wa2_f[0:GLA_RANK], "gla_ba_f": d_ba_f, "gla_wa2_b": d_wa2_b[GLA_RANK:2 * GLA_RANK], "gla_ba_b": d_ba_b,
        "gla_norm_g": d_post[0], "rwkv_mu_prev": d_mu_prev, "rwkv_mu_next": d_mu_next,
        "rwkv_w0_f": d_pre[0], "rwkv_w2_f": d_pre[1][0:64], "rwkv_w0_b": d_pre[2], "rwkv_w2_b": d_pre[3][0:64],
        "rwkv_a0": d_pre[4], "rwkv_a2": d_pre[5][64:128], "rwkv_g2": d_pre[6], "rwkv_k_k": d_pre[7], "rwkv_k_a": d_pre[8],
        "rwkv_r_k": d_post[3], "rwkv_ln_w": d_post[1], "rwkv_ln_b": d_post[2],
        "norm2_g": dg2, "ffn_conv_b": ff_deinterleave(d_conv_b_p), "norm_f_g": dgf, **late_grads,
    }
    early = [k for k in SHARDED if k not in LATE]
    comm = [] if late_blocks is None else [(_full_to_slices(grads[k], SHARDED[k]), True) for k in early]
    dh1, *got = matmul("proj_in_dx", dp, w_in_p, "nt", F32, tm, D, 512, comm=comm) if comm else \
        [matmul("proj_in_dx", dp, w_in_p, "nt", F32, tm, D, 512)]
    received.update(zip(early, got))
    (grad_x,), (grads["norm1_g"],) = rowwise_bwd("norm1_bwd", f_norm, [(x, D, 0)], [g1], [[(dh1, D, 0)]], tr,
                                                 adds=[(0, (dx1, D, 0))])
    return loss, grad_x, grads, received


MESH = pl.DeviceIdType.MESH


def remote_exchange(name, items):
    n = len(items)

    def body(*refs):
        start, wait = _exchange_plan([sc for _, sc in items], refs[:n], refs[n:2 * n], *refs[2 * n:])
        start()
        wait()

    args, specs, shapes, sems = _comm_specs(items)
    return pl.pallas_call(body, name=name, in_specs=specs, out_specs=specs, out_shape=shapes, scratch_shapes=sems)(*args)


def _exchange_plan(flags, in_refs, out_refs, send_sems, recv_sems, local_sems):
    x, y, c = lax.axis_index("x"), lax.axis_index("y"), lax.axis_index("c")
    me = 4 * x + 2 * y + c

    def peer(k):
        px = 1 - x if (k >> 2) & 1 else x
        py = 1 - y if (k >> 1) & 1 else y
        pc = 1 - c if k & 1 else c
        return (px, py, pc), 4 * px + 2 * py + pc

    def copies():
        own, sends, recvs = [], [], []
        for i, scatter in enumerate(flags):
            src = in_refs[i].at[me] if scatter else in_refs[i]
            own.append(pltpu.make_async_copy(src, out_refs[i].at[me], local_sems.at[i]))
        for k in range(1, N_DEV):
            dev, slot = peer(k)
            for i, scatter in enumerate(flags):
                src = in_refs[i].at[slot] if scatter else in_refs[i]
                pair = dict(send_sem=send_sems.at[i, k - 1], recv_sem=recv_sems.at[i, k - 1], device_id=dev, device_id_type=MESH)
                sends.append(pltpu.make_async_remote_copy(src_ref=src, dst_ref=out_refs[i].at[me], **pair))
                recvs.append(pltpu.make_async_remote_copy(src_ref=out_refs[i].at[slot], dst_ref=out_refs[i].at[slot], **pair))
        return own, sends, recvs

    def start():
        own, sends, _ = copies()
        for cp in own + sends:
            cp.start()

    def wait():
        own, sends, recvs = copies()
        for send, recv in zip(sends, recvs):
            recv.wait_recv()
            send.wait_send()
        for cp in own:
            cp.wait()

    return start, wait


def _adam_tiles(r, c):
    tc = 256 if (c % 256 == 0 and r * c > 128 * 1024) else c
    tr = 128 if (r % 128 == 0 and r > 128) else r
    return tr, tc


def adamw_reduce(name, parts, w, m, v):
    r, c = w.shape
    tr, tc = _adam_tiles(r, c)

    def body(p_ref, w_ref, m_ref, v_ref, g_ref, d_ref, nm_ref, nv_ref):
        g = p_ref[0]
        for d in range(1, N_DEV):
            g = g + p_ref[d]
        nm = ADAM_B1 * m_ref[...] + (1.0 - ADAM_B1) * g
        nv = ADAM_B2 * v_ref[...] + (1.0 - ADAM_B2) * (g * g)
        m_hat = nm / (1.0 - ADAM_B1 ** ADAM_STEP)
        v_hat = nv / (1.0 - ADAM_B2 ** ADAM_STEP)
        g_ref[...] = g
        d_ref[...] = -ADAM_LR * (m_hat / (jnp.sqrt(v_hat) + ADAM_EPS) + ADAM_WD * w_ref[...])
        nm_ref[...] = nm
        nv_ref[...] = nv

    spec = pl.BlockSpec((tr, tc), lambda i, j: (i, j))
    return pl.pallas_call(
        body, name=name, grid=(r // tr, c // tc),
        in_specs=[pl.BlockSpec((N_DEV, tr, tc), lambda i, j: (0, i, j)), spec, spec, spec],
        out_specs=[spec] * 4, out_shape=[jax.ShapeDtypeStruct((r, c), F32)] * 4,
        compiler_params=_cparams(("arbitrary", "arbitrary")),
    )(parts, w, m, v)


SHARDED = {"w_in": 1, "gla_wa2_f": 1, "gla_wa2_b": 1, "gla_proj": 1, "rwkv_w2_f": 1, "rwkv_w2_b": 1, "rwkv_a2": 1,
           "rwkv_g2": 1, "rwkv_proj": 1, "w_out": 0, "ffn_up": 1, "ffn_conv_w": 1, "ffn_down": 0}
BF16_GATHER = ("w_in", "gla_proj", "rwkv_proj", "w_out", "ffn_up", "ffn_down")
REPLICATED = ("norm1_g", "gla_ba_f", "gla_ba_b", "gla_norm_g", "rwkv_mu_prev", "rwkv_mu_next", "rwkv_w0_f", "rwkv_w0_b",
              "rwkv_a0", "rwkv_k_k", "rwkv_k_a", "rwkv_r_k", "rwkv_ln_w", "rwkv_ln_b", "norm2_g", "ffn_conv_b", "norm_f_g")
WEIGHTS = ("norm1_g", "w_in", "gla_wa2_f", "gla_ba_f", "gla_wa2_b", "gla_ba_b", "gla_norm_g", "gla_proj", "rwkv_mu_prev",
           "rwkv_mu_next", "rwkv_w0_f", "rwkv_w2_f", "rwkv_w0_b", "rwkv_w2_b", "rwkv_a0", "rwkv_a2", "rwkv_g2", "rwkv_k_k",
           "rwkv_k_a", "rwkv_r_k", "rwkv_ln_w", "rwkv_ln_b", "rwkv_proj", "w_out", "norm2_g", "ffn_up", "ffn_conv_w",
           "ffn_conv_b", "ffn_down", "norm_f_g")


def _gathered_to_full(g, axis):
    if axis == 0:
        return g.reshape(N_DEV * g.shape[1], g.shape[2])
    return g.transpose(1, 0, 2).reshape(g.shape[1], N_DEV * g.shape[2])


def _full_to_slices(a, axis):
    if axis == 0:
        return a.reshape(N_DEV, a.shape[0] // N_DEV, a.shape[1])
    return a.reshape(a.shape[0], N_DEV, a.shape[1] // N_DEV).transpose(1, 0, 2)


def _pack_rows(size):
    return -(-size // (8 * LANE)) * 8


def _pack(d):
    parts = []
    for k in REPLICATED:
        rows = d[k].reshape(-1, LANE).astype(F32)
        parts.append(jnp.pad(rows, ((0, _pack_rows(rows.size) - rows.shape[0]), (0, 0))))
    return jnp.concatenate(parts, axis=0)


def _unpack(packed, shapes):
    out, pos = {}, 0
    for k in REPLICATED:
        size = int(np.prod(shapes[k]))
        out[k] = packed[pos:pos + size // LANE].reshape(shapes[k])
        pos += _pack_rows(size)
    return out


def kernel(x, norm1_g, w_in, gla_wa2_f, gla_ba_f, gla_wa2_b, gla_ba_b, gla_norm_g, gla_proj, rwkv_mu_prev, rwkv_mu_next, rwkv_w0_f, rwkv_w2_f, rwkv_w0_b, rwkv_w2_b, rwkv_a0, rwkv_a2, rwkv_g2, rwkv_k_k, rwkv_k_a, rwkv_r_k, rwkv_ln_w, rwkv_ln_b, rwkv_proj, w_out, norm2_g, ffn_up, ffn_conv_w, ffn_conv_b, ffn_down, norm_f_g, loss_target, m_norm1_g, m_w_in, m_gla_wa2_f, m_gla_ba_f, m_gla_wa2_b, m_gla_ba_b, m_gla_norm_g, m_gla_proj, m_rwkv_mu_prev, m_rwkv_mu_next, m_rwkv_w0_f, m_rwkv_w2_f, m_rwkv_w0_b, m_rwkv_w2_b, m_rwkv_a0, m_rwkv_a2, m_rwkv_g2, m_rwkv_k_k, m_rwkv_k_a, m_rwkv_r_k, m_rwkv_ln_w, m_rwkv_ln_b, m_rwkv_proj, m_w_out, m_norm2_g, m_ffn_up, m_ffn_conv_w, m_ffn_conv_b, m_ffn_down, m_norm_f_g, v_norm1_g, v_w_in, v_gla_wa2_f, v_gla_ba_f, v_gla_wa2_b, v_gla_ba_b, v_gla_norm_g, v_gla_proj, v_rwkv_mu_prev, v_rwkv_mu_next, v_rwkv_w0_f, v_rwkv_w2_f, v_rwkv_w0_b, v_rwkv_w2_b, v_rwkv_a0, v_rwkv_a2, v_rwkv_g2, v_rwkv_k_k, v_rwkv_k_a, v_rwkv_r_k, v_rwkv_ln_w, v_rwkv_ln_b, v_rwkv_proj, v_w_out, v_norm2_g, v_ffn_up, v_ffn_conv_w, v_ffn_conv_b, v_ffn_down, v_norm_f_g):
    args = locals()
    wts = {k: args[k] for k in WEIGHTS}
    mom = {k: args["m_" + k] for k in WEIGHTS}
    var = {k: args["v_" + k] for k in WEIGHTS}
    shapes = {k: wts[k].shape for k in WEIGHTS}
    nb, t = x.shape[0], x.shape[1]
    mat = lambda a: a.reshape(a.shape[-2], a.shape[-1])

    block = lambda k: mat(wts[k]).astype(MXU_DTYPE) if k in BF16_GATHER else mat(wts[k])
    early = [k for k in SHARDED if k not in LATE]
    gathered = remote_exchange("gather_weights", [(block(k), False) for k in early])
    full = {k: _gathered_to_full(g, SHARDED[k]) for k, g in zip(early, gathered)}
    for k in REPLICATED:
        full[k] = wts[k].reshape(-1) if k in ("norm_f_g", "rwkv_r_k") else wts[k][0]

    loss, grad_x, grads, received = local_step(x.reshape(nb * t, D), loss_target.reshape(nb * t, D), full, nb, t,
                                               late_blocks={k: block(k) for k in LATE})

    (rep_parts,) = remote_exchange("exchange_replicated", [(_pack(grads), False)])

    res = {}
    for k in SHARDED:
        outs = adamw_reduce("adamw_" + k, received[k], mat(wts[k]), mat(mom[k]), mat(var[k]))
        res[k] = [o.reshape(shapes[k]) for o in outs]
    packed = adamw_reduce("adamw_replicated", rep_parts, _pack(wts), _pack(mom), _pack(var))
    unpacked = [_unpack(p, shapes) for p in packed]
    for k in REPLICATED:
        res[k] = [u[k] for u in unpacked]

    total = lax.psum(loss[0, 0], ("x", "y", "c"))
    out = [total, grad_x.reshape(x.shape)]
    for j in range(4):
        out += [res[k][j] for k in WEIGHTS]
    return tuple(out)
```

```python
import functools

import jax
import jax.numpy as jnp
import numpy as np
from jax import lax
from jax.experimental import pallas as pl
from jax.experimental.pallas import tpu as pltpu

F32 = jnp.float32
MXU_DTYPE = jnp.bfloat16

D = 1024
SEQ = 2048
GLA_H, GLA_DK, GLA_DV, GLA_CHUNK = 4, 64, 128, 64
GLA_RANK = 16
GLA_LOGIT_NORM = 16.0
RW_H, RW_N = 8, 64
RW_W = 512
D_FF = 2752
NORM_EPS = 1e-6
HEAD_NORM_EPS = 1e-5
RW_GN_EPS = RW_N * 1e-5
N_DEV = 8
ADAM_LR, ADAM_B1, ADAM_B2, ADAM_EPS, ADAM_WD, ADAM_STEP = 0.001, 0.9, 0.999, 1e-08, 0.01, 10

C_GA, C_GB, C_Q, C_K, C_V, C_OG = 0, 1024, 2048, 2304, 2560, 3072
C_RW = 3584
C_R, C_RK, C_RV, C_WLAL, C_GL = 3584, 4096, 4608, 5120, 5248
C_AFAB = 5376
NP = 5632
RW_PW = 1792
FFP = 2816
LANE = 128
VMEM_LIMIT = 56 * 1024 * 1024


def _cparams(sem):
    return pltpu.CompilerParams(dimension_semantics=sem, vmem_limit_bytes=VMEM_LIMIT)


@jax.custom_vjp
def mm(a, b):
    return jnp.dot(a.astype(MXU_DTYPE), b.astype(MXU_DTYPE), preferred_element_type=F32)


def _mm_fwd(a, b):
    return mm(a, b), (a, b)


def _mm_bwd(res, g):
    a, b = res
    gb = g.astype(MXU_DTYPE)
    da = lax.dot_general(gb, b.astype(MXU_DTYPE), (((1,), (1,)), ((), ())), preferred_element_type=F32)
    db = lax.dot_general(a.astype(MXU_DTYPE), gb, (((0,), (0,)), ((), ())), preferred_element_type=F32)
    return da.astype(a.dtype), db.astype(b.dtype)


mm.defvjp(_mm_fwd, _mm_bwd)


@jax.custom_vjp
def mm_nt(a, b):
    return lax.dot_general(a.astype(MXU_DTYPE), b.astype(MXU_DTYPE), (((1,), (1,)), ((), ())), preferred_element_type=F32)


def _mm_nt_fwd(a, b):
    return mm_nt(a, b), (a, b)


def _mm_nt_bwd(res, g):
    a, b = res
    gb = g.astype(MXU_DTYPE)
    da = jnp.dot(gb, b.astype(MXU_DTYPE), preferred_element_type=F32)
    db = lax.dot_general(gb, a.astype(MXU_DTYPE), (((0,), (0,)), ((), ())), preferred_element_type=F32)
    return da.astype(a.dtype), db.astype(b.dtype)


mm_nt.defvjp(_mm_nt_fwd, _mm_nt_bwd)


@jax.custom_vjp
def mm_tn(a, b):
    return lax.dot_general(a.astype(MXU_DTYPE), b.astype(MXU_DTYPE), (((0,), (0,)), ((), ())), preferred_element_type=F32)


def _mm_tn_fwd(a, b):
    return mm_tn(a, b), (a, b)


def _mm_tn_bwd(res, g):
    a, b = res
    gb = g.astype(MXU_DTYPE)
    da = lax.dot_general(b.astype(MXU_DTYPE), gb, (((1,), (1,)), ((), ())), preferred_element_type=F32)
    db = jnp.dot(a.astype(MXU_DTYPE), gb, preferred_element_type=F32)
    return da.astype(a.dtype), db.astype(b.dtype)


mm_tn.defvjp(_mm_tn_fwd, _mm_tn_bwd)


@functools.partial(jax.custom_vjp, nondiff_argnums=(2, 3))
def sel_dot(x, s, dims, x_first):
    sb = s.astype(MXU_DTYPE)
    hi = x.astype(MXU_DTYPE)
    r1 = x - hi.astype(F32)
    mid = r1.astype(MXU_DTYPE)
    lo = (r1 - mid.astype(F32)).astype(MXU_DTYPE)
    out = None
    for part in (hi, mid, lo):
        ops = (part, sb) if x_first else (sb, part)
        d = lax.dot_general(*ops, (dims, ((), ())), preferred_element_type=F32)
        out = d if out is None else out + d
    return out


def _sel_dot_fwd(x, s, dims, x_first):
    return sel_dot(x, s, dims, x_first), s


def _sel_dot_bwd(dims, x_first, s, g):
    if x_first:
        (cx,), (cs,) = dims
        dx = sel_dot(g, s, ((1,), (1 - cs,)), True) if cx == 1 else sel_dot(g, s, ((1 - cs,), (1,)), False)
    else:
        (cs,), (cx,) = dims
        dx = sel_dot(g, s, ((1 - cs,), (0,)), False) if cx == 0 else sel_dot(g, s, ((0,), (1 - cs,)), True)
    return dx, jnp.zeros_like(s)


sel_dot.defvjp(_sel_dot_fwd, _sel_dot_bwd)


def mm_exact(a, b, b_is_01=True):
    return sel_dot(a, b, ((1,), (0,)), True) if b_is_01 else sel_dot(b, a, ((1,), (0,)), False)


def mm_tn_exact(a, b):
    return sel_dot(a, b, ((0,), (0,)), True)


def _softplus(x):
    return jnp.maximum(x, 0.0) + jnp.log(1.0 + jnp.exp(-jnp.abs(x)))


def _sigmoid(x):
    return jax.nn.sigmoid(x)


def _silu(x):
    return x * _sigmoid(x)


def _rmsnorm(x, g):
    return x * lax.rsqrt(jnp.mean(x * x, axis=-1, keepdims=True) + NORM_EPS) * g


def _segment_ones(width, seg):
    i = lax.broadcasted_iota(jnp.int32, (width, width), 0) // seg
    j = lax.broadcasted_iota(jnp.int32, (width, width), 1) // seg
    return (i == j).astype(F32)


def _row_spec(tm, width, cb):
    return pl.BlockSpec((tm, width), lambda i: (i, cb))


def _full_spec(shape):
    nd = len(shape)
    return pl.BlockSpec(tuple(shape), lambda i: (0,) * nd)


def rowwise_fwd(name, f, rows, params, outs, tm):
    n = rows[0][0].shape[0]
    nr, npar = len(rows), len(params)

    def body(*refs):
        rv = [r[...] for r in refs[:nr]]
        pv = [r[...] for r in refs[nr:nr + npar]]
        res = f(rv, pv)
        for o_ref, val in zip(refs[nr + npar:], res):
            o_ref[...] = val.astype(o_ref.dtype)

    return pl.pallas_call(
        body, name=name, grid=(n // tm,),
        in_specs=[_row_spec(tm, w, cb) for _, w, cb in rows] + [_full_spec(p.shape) for p in params],
        out_specs=[_row_spec(tm, w, 0) for w, _ in outs],
        out_shape=[jax.ShapeDtypeStruct((n, w), dt) for w, dt in outs],
        compiler_params=_cparams(("arbitrary",)),
    )(*[a for a, _, _ in rows], *params)


def rowwise_bwd(name, f, rows, params, douts, tm, adds=(), grad_rows=None):
    n = rows[0][0].shape[0]
    nr, npar = len(rows), len(params)
    grad_rows = list(range(nr)) if grad_rows is None else list(grad_rows)
    flat_d = [d for group in douts for d in group]
    nd, na, ng = len(flat_d), len(adds), len(grad_rows)

    def body(*refs):
        rv = [r[...] for r in refs[:nr]]
        pv = [r[...] for r in refs[nr:nr + npar]]
        dflat = [r[...].astype(F32) for r in refs[nr + npar:nr + npar + nd]]
        av = [r[...] for r in refs[nr + npar + nd:nr + npar + nd + na]]
        o = nr + npar + nd + na
        drow_refs, dpar_refs = refs[o:o + ng], refs[o + ng:o + ng + npar]
        dv, pos = [], 0
        for group in douts:
            dv.append(sum(dflat[pos + 1:pos + len(group)], dflat[pos]))
            pos += len(group)

        @pl.when(pl.program_id(0) == 0)
        def _():
            for r in dpar_refs:
                r[...] = jnp.zeros_like(r)

        def g(grows, pars):
            full = list(rv)
            for i, val in zip(grad_rows, grows):
                full[i] = val
            return f(full, pars)

        res, vjp = jax.vjp(g, [rv[i] for i in grad_rows], pv)
        drows, dpars = vjp([d.astype(r.dtype) for d, r in zip(dv, res)])
        drows = [d.astype(F32) for d in drows]
        for (idx, _), a in zip(adds, av):
            drows[idx] = drows[idx] + a.astype(F32)
        for r, d in zip(drow_refs, drows):
            r[...] = d
        for r, d in zip(dpar_refs, dpars):
            r[...] += d.astype(F32)

    res = pl.pallas_call(
        body, name=name, grid=(n // tm,),
        in_specs=[_row_spec(tm, w, cb) for _, w, cb in rows] + [_full_spec(p.shape) for p in params]
        + [_row_spec(tm, w, cb) for _, w, cb in flat_d] + [_row_spec(tm, w, cb) for _, (_, w, cb) in adds],
        out_specs=[_row_spec(tm, rows[i][1], 0) for i in grad_rows] + [_full_spec(p.shape) for p in params],
        out_shape=[jax.ShapeDtypeStruct((n, rows[i][1]), F32) for i in grad_rows]
        + [jax.ShapeDtypeStruct(p.shape, F32) for p in params],
        compiler_params=_cparams(("arbitrary",)),
    )(*[a for a, _, _ in rows], *params, *[a for a, _, _ in flat_d], *[a for _, (a, _, _) in adds])
    return res[:ng], res[ng:]


def matmul(name, a, b, mode, out_dtype, tm, tn, tk, comm=()):
    nc = len(comm)
    flags = [sc for _, sc in comm]
    if mode == "nn":
        (m, k), n = a.shape, b.shape[1]
        a_spec = pl.BlockSpec((tm, tk), lambda i, j, kk: (i, kk))
        b_spec = pl.BlockSpec((tk, tn), lambda i, j, kk: (kk, j))
        dims = (((1,), (0,)), ((), ()))
    elif mode == "nt":
        (m, k), n = a.shape, b.shape[0]
        a_spec = pl.BlockSpec((tm, tk), lambda i, j, kk: (i, kk))
        b_spec = pl.BlockSpec((tn, tk), lambda i, j, kk: (j, kk))
        dims = (((1,), (1,)), ((), ()))
    else:
        (k, m), n = a.shape, b.shape[1]
        a_spec = pl.BlockSpec((tk, tm), lambda i, j, kk: (kk, i))
        b_spec = pl.BlockSpec((tk, tn), lambda i, j, kk: (kk, j))
        dims = (((0,), (0,)), ((), ()))
    assert m % tm == 0 and n % tn == 0 and k % tk == 0, (name, a.shape, b.shape, tm, tn, tk)
    nk = k // tk
    grid = (m // tm, n // tn, nk)

    def body(*refs):
        a_ref, b_ref, c_in, o_ref = refs[0], refs[1], refs[2:2 + nc], refs[2 + nc]
        c_out, acc_ref, sems = refs[3 + nc:3 + 2 * nc], refs[3 + 2 * nc], refs[4 + 2 * nc:]
        kk = pl.program_id(2)
        step = (pl.program_id(0) * grid[1] + pl.program_id(1)) * nk + kk
        if nc:
            start, wait = _exchange_plan(flags, c_in, c_out, *sems)

            @pl.when(step == 0)
            def _():
                start()

        @pl.when(kk == 0)
        def _():
            acc_ref[...] = jnp.zeros_like(acc_ref)

        acc_ref[...] += lax.dot_general(a_ref[...].astype(MXU_DTYPE), b_ref[...].astype(MXU_DTYPE), dims,
                                        preferred_element_type=F32)

        @pl.when(kk == nk - 1)
        def _():
            o_ref[...] = acc_ref[...].astype(o_ref.dtype)

        if nc:
            @pl.when(step == grid[0] * grid[1] * nk - 1)
            def _():
                wait()

    c_args, c_specs, c_shapes, c_sems = _comm_specs(comm)
    res = pl.pallas_call(
        body, name=name, grid=grid,
        in_specs=[a_spec, b_spec] + c_specs,
        out_specs=[pl.BlockSpec((tm, tn), lambda i, j, kk: (i, j))] + c_specs,
        out_shape=[jax.ShapeDtypeStruct((m, n), out_dtype)] + c_shapes,
        scratch_shapes=[pltpu.VMEM((tm, tn), F32)] + c_sems,
        compiler_params=_cparams(("arbitrary", "arbitrary", "arbitrary")),
    )(a, b, *c_args)
    return res if nc else res[0]


def _prev(u, first):
    return jnp.where(first, 0.0, pltpu.roll(u, 1, 0))


def _next(u, last):
    return jnp.where(last, 0.0, pltpu.roll(u, u.shape[0] - 1, 0))


def _edge_masks(t, w):
    row = lax.broadcasted_iota(jnp.int32, (t, w), 0)
    return row == 0, row == t - 1


SHIFT_CW = 256


def shift_fwd(p, mu_prev, mu_next, nb, t):
    cw, c0 = SHIFT_CW, C_RW // SHIFT_CW

    def body(p_ref, mp_ref, mn_ref, s_ref):
        x = p_ref[...]
        first, last = _edge_masks(t, cw)
        s_ref[...] = x + mp_ref[...] * (_prev(x, first) - x) + mn_ref[...] * (_next(x, last) - x)

    return pl.pallas_call(
        body, name="rwkv_shift_fwd", grid=(nb, RW_PW // cw),
        in_specs=[pl.BlockSpec((t, cw), lambda b, j: (b, c0 + j)), pl.BlockSpec((1, cw), lambda b, j: (0, j)),
                  pl.BlockSpec((1, cw), lambda b, j: (0, j))],
        out_specs=pl.BlockSpec((t, cw), lambda b, j: (b, j)),
        out_shape=jax.ShapeDtypeStruct((nb * t, RW_PW), F32),
        compiler_params=_cparams(("arbitrary", "arbitrary")),
    )(p, mu_prev, mu_next)


def shift_bwd(p, ds, mu_prev, mu_next, nb, t):
    cw, c0 = SHIFT_CW, C_RW // SHIFT_CW

    def body(p_ref, ds_ref, mp_ref, mn_ref, dp_ref, dmp_ref, dmn_ref):
        @pl.when(pl.program_id(1) == 0)
        def _():
            dmp_ref[...] = jnp.zeros_like(dmp_ref)
            dmn_ref[...] = jnp.zeros_like(dmn_ref)

        x, g = p_ref[...], ds_ref[...]
        mp, mn = mp_ref[...], mn_ref[...]
        first, last = _edge_masks(t, cw)
        dp_ref[...] = g * (1.0 - mp - mn) + _next(mp * g, last) + _prev(mn * g, first)
        dmp_ref[...] += jnp.sum(g * (_prev(x, first) - x), axis=0, keepdims=True)
        dmn_ref[...] += jnp.sum(g * (_next(x, last) - x), axis=0, keepdims=True)

    return pl.pallas_call(
        body, name="rwkv_shift_bwd", grid=(RW_PW // cw, nb),
        in_specs=[pl.BlockSpec((t, cw), lambda j, b: (b, c0 + j)), pl.BlockSpec((t, cw), lambda j, b: (b, j)),
                  pl.BlockSpec((1, cw), lambda j, b: (0, j)), pl.BlockSpec((1, cw), lambda j, b: (0, j))],
        out_specs=[pl.BlockSpec((t, cw), lambda j, b: (b, j)), pl.BlockSpec((1, cw), lambda j, b: (0, j)),
                   pl.BlockSpec((1, cw), lambda j, b: (0, j))],
        out_shape=[jax.ShapeDtypeStruct((nb * t, RW_PW), F32), jax.ShapeDtypeStruct((1, RW_PW), F32),
                   jax.ShapeDtypeStruct((1, RW_PW), F32)],
        compiler_params=_cparams(("arbitrary", "arbitrary")),
    )(p, ds, mu_prev, mu_next)


def conv_glu_fwd(u, cw, cb, nb, t):
    def body(u_ref, w_ref, b_ref, z_ref):
        x, w = u_ref[...], w_ref[...]
        first, last = _edge_masks(t, 2 * LANE)
        c = w[0:1] * _prev(x, first) + w[1:2] * x + w[2:3] * _next(x, last) + b_ref[...]
        z_ref[...] = (_silu(c[:, :LANE]) * c[:, LANE:]).astype(z_ref.dtype)

    return pl.pallas_call(
        body, name="conv_glu_fwd", grid=(nb, FFP // LANE),
        in_specs=[pl.BlockSpec((t, 2 * LANE), lambda b, j: (b, j)), pl.BlockSpec((3, 2 * LANE), lambda b, j: (0, j)),
                  pl.BlockSpec((1, 2 * LANE), lambda b, j: (0, j))],
        out_specs=pl.BlockSpec((t, LANE), lambda b, j: (b, j)),
        out_shape=jax.ShapeDtypeStruct((nb * t, FFP), MXU_DTYPE),
        compiler_params=_cparams(("arbitrary", "arbitrary")),
    )(u, cw, cb)


def conv_glu_bwd(u, dz, cw, cb, nb, t):
    def body(u_ref, dz_ref, w_ref, b_ref, du_ref, dw_ref, db_ref):
        @pl.when(pl.program_id(1) == 0)
        def _():
            dw_ref[...] = jnp.zeros_like(dw_ref)
            db_ref[...] = jnp.zeros_like(db_ref)

        x, w, g = u_ref[...], w_ref[...], dz_ref[...]
        first, last = _edge_masks(t, 2 * LANE)
        xp, xn = _prev(x, first), _next(x, last)
        c = w[0:1] * xp + w[1:2] * x + w[2:3] * xn + b_ref[...]
        cg, cv = c[:, :LANE], c[:, LANE:]
        sg = _sigmoid(cg)
        dcg = g * cv * (sg * (1.0 + cg * (1.0 - sg)))
        dcv = g * (cg * sg)
        dc = jnp.concatenate([dcg, dcv], axis=1)
        du = w[1:2] * dc + _next(w[0:1] * dc, last) + _prev(w[2:3] * dc, first)
        du_ref[...] = du.astype(du_ref.dtype)
        dw_ref[0:1, :] += jnp.sum(dc * xp, axis=0, keepdims=True)
        dw_ref[1:2, :] += jnp.sum(dc * x, axis=0, keepdims=True)
        dw_ref[2:3, :] += jnp.sum(dc * xn, axis=0, keepdims=True)
        db_ref[...] += jnp.sum(dc, axis=0, keepdims=True)

    return pl.pallas_call(
        body, name="conv_glu_bwd", grid=(FFP // LANE, nb),
        in_specs=[pl.BlockSpec((t, 2 * LANE), lambda j, b: (b, j)), pl.BlockSpec((t, LANE), lambda j, b: (b, j)),
                  pl.BlockSpec((3, 2 * LANE), lambda j, b: (0, j)), pl.BlockSpec((1, 2 * LANE), lambda j, b: (0, j))],
        out_specs=[pl.BlockSpec((t, 2 * LANE), lambda j, b: (b, j)), pl.BlockSpec((3, 2 * LANE), lambda j, b: (0, j)),
                   pl.BlockSpec((1, 2 * LANE), lambda j, b: (0, j))],
        out_shape=[jax.ShapeDtypeStruct((nb * t, 2 * FFP), MXU_DTYPE), jax.ShapeDtypeStruct((3, 2 * FFP), F32),
                   jax.ShapeDtypeStruct((1, 2 * FFP), F32)],
        compiler_params=_cparams(("arbitrary", "arbitrary")),
    )(u, dz, cw, cb)


def _gla_chunk(q, k, v, afab, wa2p, ba, s_in, reverse, sb):
    c = GLA_CHUNK
    r = sb * c
    ri = lax.broadcasted_iota(jnp.int32, (r, r), 0)
    ci = lax.broadcasted_iota(jnp.int32, (r, r), 1)
    same = (ri // c) == (ci // c)
    keep = same & ((ci >= ri) if reverse else (ci <= ri))
    i_ref = (c - 1 - c // 2) if reverse else (c // 2)
    pick_ref = (ci == (ri // c) * c + i_ref).astype(F32)
    seq_cols = (lax.broadcasted_iota(jnp.int32, (r, sb * LANE), 0) // c) == (lax.broadcasted_iota(jnp.int32, (r, sb * LANE), 1) // LANE)
    expand = lambda x: jnp.where(seq_cols, jnp.concatenate([x] * sb, axis=1), 0.0)
    lane = lax.broadcasted_iota(jnp.int32, (1, LANE), 1)
    outs, states = [None] * GLA_H, [None] * GLA_H
    for pr in range(GLA_H // 2):
        la = -_softplus(-(mm(afab, wa2p[pr]) + ba[pr])) * (1.0 / GLA_LOGIT_NORM)
        b = mm_exact(keep.astype(F32), la, b_is_01=False)
        b_ref = mm_exact(pick_ref, b, b_is_01=False)
        b_last = mm_exact(same.astype(F32), la, b_is_01=False)
        qs = q[pr] * (GLA_DK ** -0.5)
        qi = qs * jnp.exp(b - b_ref)
        ki = k[pr] * jnp.exp(b_ref - b)
        kd = k[pr] * jnp.exp(b_last - b)
        qb = qs * jnp.exp(b)
        dec = jnp.exp(mm_tn_exact(expand(la), jnp.ones((r, LANE), F32)))
        for h in (2 * pr, 2 * pr + 1):
            m = ((lane // GLA_DK) == (h % 2)).astype(F32)
            a = jnp.where(keep, mm_nt(qi * m, ki), 0.0)
            o_intra = mm(a, v[h])
            kv = mm_tn(expand(kd * m), v[h])
            o_inter = mm(expand(qb * m), s_in[h])
            outs[h] = o_intra + o_inter
            states[h] = s_in[h] * dec + kv
    return outs, states


def _gla_load(q_ref, k_ref, v_ref, af_ref, w_ref, ba_ref, sb, rows):
    stack = lambda ref, c0: jnp.concatenate([ref[s, rows, c0:c0 + LANE] for s in range(sb)], axis=0)
    q = [stack(q_ref, pr * LANE) for pr in range(GLA_H // 2)]
    k = [stack(k_ref, pr * LANE) for pr in range(GLA_H // 2)]
    v = [stack(v_ref, h * GLA_DV) for h in range(GLA_H)]
    w = [w_ref[:, pr * LANE:(pr + 1) * LANE] for pr in range(GLA_H // 2)]
    ba = [ba_ref[:, pr * LANE:(pr + 1) * LANE] for pr in range(GLA_H // 2)]
    return q, k, v, stack(af_ref, 0), w, ba


GLA_TILE = 256
GLA_SB = 4


def _gla_specs(nb, t, reverse):
    tile = min(GLA_TILE, t)
    nt = t // tile
    sb = GLA_SB if nb % GLA_SB == 0 else 1
    return tile, tile // GLA_CHUNK, nt, sb, ((lambda j: nt - 1 - j) if reverse else (lambda j: j))


def gla_fwd(p, wa2p, ba, o_add, nb, t, reverse):
    tile, cpt, nt, sb, tj = _gla_specs(nb, t, reverse)
    has_add = o_add is not None

    def body(*refs):
        if has_add:
            q_ref, k_ref, v_ref, af_ref, w_ref, ba_ref, add_ref, o_ref, hist_ref, s_ref = refs
        else:
            q_ref, k_ref, v_ref, af_ref, w_ref, ba_ref, o_ref, hist_ref, s_ref = refs

        @pl.when(pl.program_id(1) == 0)
        def _():
            s_ref[...] = jnp.zeros_like(s_ref)

        def step(i, carry):
            ci = (cpt - 1 - i) if reverse else i
            rows = pl.ds(pl.multiple_of(ci * GLA_CHUNK, GLA_CHUNK), GLA_CHUNK)
            s_in = [s_ref[h] for h in range(GLA_H)]
            for h in range(GLA_H):
                for s in range(sb):
                    hist_ref[s, ci, h] = s_in[h][s * LANE:(s + 1) * LANE]
            q, k, v, af, w, ba = _gla_load(q_ref, k_ref, v_ref, af_ref, w_ref, ba_ref, sb, rows)
            outs, states = _gla_chunk(q, k, v, af, w, ba, s_in, reverse, sb)
            for h in range(GLA_H):
                for s in range(sb):
                    oh = outs[h][s * GLA_CHUNK:(s + 1) * GLA_CHUNK]
                    if has_add:
                        oh = oh + add_ref[s, rows, h * GLA_DV:(h + 1) * GLA_DV]
                    o_ref[s, rows, h * GLA_DV:(h + 1) * GLA_DV] = oh
                s_ref[h] = states[h]
            return carry

        lax.fori_loop(0, cpt, step, 0)

    col = lambda width, c0: pl.BlockSpec((sb, tile, width), lambda b, j: (b, tj(j), c0 // width))
    in_specs = [col(256, C_Q), col(256, C_K), col(512, C_V), col(LANE, C_AFAB),
                pl.BlockSpec((LANE, 256), lambda b, j: (0, 0)), pl.BlockSpec((1, 256), lambda b, j: (0, 0))]
    p3 = p.reshape(nb, t, p.shape[1])
    args = [p3, p3, p3, p3, wa2p, ba]
    if has_add:
        in_specs.append(col(512, 0))
        args.append(o_add.reshape(nb, t, 512))
    o, hist = pl.pallas_call(
        body, name="gla_fwd_rev" if reverse else "gla_fwd", grid=(nb // sb, nt),
        in_specs=in_specs,
        out_specs=[col(512, 0), pl.BlockSpec((sb, cpt, GLA_H, LANE, LANE), lambda b, j: (b, tj(j), 0, 0, 0))],
        out_shape=[jax.ShapeDtypeStruct((nb, t, 512), F32),
                   jax.ShapeDtypeStruct((nb, t // GLA_CHUNK, GLA_H, LANE, LANE), F32)],
        scratch_shapes=[pltpu.VMEM((GLA_H, sb * LANE, LANE), F32)],
        compiler_params=_cparams(("arbitrary", "arbitrary")),
    )(*args)
    return o.reshape(nb * t, 512), hist


def gla_bwd(p, wa2p, ba, hist, do, dprev, nb, t, reverse):
    tile, cpt, nt, sb, tj_f = _gla_specs(nb, t, reverse)
    tj = lambda j: tj_f(nt - 1 - j)
    has_prev = dprev is not None

    def body(*refs):
        if has_prev:
            q_ref, k_ref, v_ref, af_ref, w_ref, ba_ref, hist_ref, do_ref, prev_ref, dqkv_ref, dw_ref, dba_ref, ds_ref = refs
        else:
            q_ref, k_ref, v_ref, af_ref, w_ref, ba_ref, hist_ref, do_ref, dqkv_ref, dw_ref, dba_ref, ds_ref = refs

        @pl.when((pl.program_id(0) == 0) & (pl.program_id(1) == 0))
        def _():
            dw_ref[...] = jnp.zeros_like(dw_ref)
            dba_ref[...] = jnp.zeros_like(dba_ref)

        @pl.when(pl.program_id(1) == 0)
        def _():
            ds_ref[...] = jnp.zeros_like(ds_ref)

        def step(i, carry):
            ci = i if reverse else (cpt - 1 - i)
            rows = pl.ds(pl.multiple_of(ci * GLA_CHUNK, GLA_CHUNK), GLA_CHUNK)
            fn = functools.partial(_gla_chunk, reverse=reverse, sb=sb)
            seqs = lambda get: jnp.concatenate([get(s) for s in range(sb)], axis=0)
            s_in = [seqs(lambda s: hist_ref[s, ci, h]) for h in range(GLA_H)]
            q, k, v, af, w, ba = _gla_load(q_ref, k_ref, v_ref, af_ref, w_ref, ba_ref, sb, rows)
            _, vjp = jax.vjp(fn, q, k, v, af, w, ba, s_in)
            d_o = [seqs(lambda s: do_ref[s, rows, h * GLA_DV:(h + 1) * GLA_DV]) for h in range(GLA_H)]
            d_s = [ds_ref[h] for h in range(GLA_H)]
            dq, dk, dv, daf, dw, dba, ds_in = vjp((d_o, d_s))
            pieces = [(pr * LANE, dq[pr]) for pr in range(2)] + [(256 + pr * LANE, dk[pr]) for pr in range(2)]
            pieces += [(512 + h * GLA_DV, dv[h]) for h in range(GLA_H)] + [(1024, daf)]
            for c0, val in pieces:
                for s in range(sb):
                    part = val[s * GLA_CHUNK:(s + 1) * GLA_CHUNK]
                    if has_prev:
                        part = part + prev_ref[s, rows, c0:c0 + LANE]
                    dqkv_ref[s, rows, c0:c0 + LANE] = part
            for pr in range(2):
                dw_ref[:, pr * LANE:(pr + 1) * LANE] += dw[pr]
                dba_ref[:, pr * LANE:(pr + 1) * LANE] += dba[pr]
            for h in range(GLA_H):
                ds_ref[h] = ds_in[h]
            return carry

        lax.fori_loop(0, cpt, step, 0)

    col = lambda width, c0: pl.BlockSpec((sb, tile, width), lambda b, j: (b, tj(j), c0 // width))
    in_specs = [col(256, C_Q), col(256, C_K), col(512, C_V), col(LANE, C_AFAB),
                pl.BlockSpec((LANE, 256), lambda b, j: (0, 0)), pl.BlockSpec((1, 256), lambda b, j: (0, 0)),
                pl.BlockSpec((sb, cpt, GLA_H, LANE, LANE), lambda b, j: (b, tj(j), 0, 0, 0)), col(512, 0)]
    p3 = p.reshape(nb, t, p.shape[1])
    args = [p3, p3, p3, p3, wa2p, ba, hist, do.reshape(nb, t, 512)]
    if has_prev:
        in_specs.append(col(1152, 0))
        args.append(dprev.reshape(nb, t, 1152))
    dqkv, dw, dba = pl.pallas_call(
        body, name="gla_bwd_rev" if reverse else "gla_bwd", grid=(nb // sb, nt),
        in_specs=in_specs,
        out_specs=[col(1152, 0), pl.BlockSpec((LANE, 256), lambda b, j: (0, 0)), pl.BlockSpec((1, 256), lambda b, j: (0, 0))],
        out_shape=[jax.ShapeDtypeStruct((nb, t, 1152), F32), jax.ShapeDtypeStruct((LANE, 256), F32),
                   jax.ShapeDtypeStruct((1, 256), F32)],
        scratch_shapes=[pltpu.VMEM((GLA_H, sb * LANE, LANE), F32)],
        compiler_params=_cparams(("arbitrary", "arbitrary")),
    )(*args)
    return dqkv.reshape(nb * t, 1152), dw, dba


SCAN_TB = 8
RW_VH = RW_N // 2


def _bwd_lanes():
    lane = lax.broadcasted_iota(jnp.int32, (1, LANE), 1)
    return ((lane // (LANE // 4)) % 2) == 1


def _comm_specs(comm):
    anyspec = pl.BlockSpec(memory_space=pl.ANY)
    n = len(comm)
    shapes = [jax.ShapeDtypeStruct((N_DEV,) + (a.shape[1:] if sc else a.shape), a.dtype) for a, sc in comm]
    sems = [pltpu.SemaphoreType.DMA((n, N_DEV - 1)), pltpu.SemaphoreType.DMA((n, N_DEV - 1)), pltpu.SemaphoreType.DMA((n,))] if n else []
    return [a for a, _ in comm], [anyspec] * n, shapes, sems


def rwkv_scan_fwd(r, w, k, a, b, v, comm=()):
    t = r.shape[0]
    nt = t // SCAN_TB
    nc = len(comm)
    flags = [sc for _, sc in comm]

    def body(*refs):
        (rf, rm, kf, km, af, am, bf, bm, wf_ref, wm_ref, vf, vm), refs = refs[:12], refs[12:]
        c_in, refs = refs[:nc], refs[nc:]
        (yf_ref, ym_ref, hist_ref, sa_ref), refs = refs[:4], refs[4:]
        c_out, refs = refs[:nc], refs[nc:]
        s_ref, sems = refs[0], refs[1:]
        i = pl.program_id(0)
        if nc:
            start, wait = _exchange_plan(flags, c_in, c_out, *sems)

        @pl.when(i == 0)
        def _():
            s_ref[...] = jnp.zeros_like(s_ref)
            if nc:
                start()

        bwd = _bwd_lanes()

        for tt in range(SCAN_TB):
            mt = SCAN_TB - 1 - tt
            pick = lambda f_ref, m_ref: jnp.where(bwd, m_ref[mt], f_ref[tt])
            rt, kt, at, bt, wt = pick(rf, rm), pick(kf, km), pick(af, am), pick(bf, bm), pick(wf_ref, wm_ref)
            for vi in range(RW_VH):
                sv = s_ref[vi] if tt == 0 else hist_ref[tt - 1, vi]
                sa = jnp.sum(sv * at, axis=0, keepdims=True)
                v_row = jnp.where(bwd, vm[mt, vi:vi + 1, :], vf[tt, vi:vi + 1, :])
                sn = sv * wt + sa * bt + v_row * kt
                hist_ref[tt, vi] = sn
                y_row = jnp.sum(sn * rt, axis=0, keepdims=True)
                yf_ref[tt, vi:vi + 1, :] = y_row
                ym_ref[mt, vi:vi + 1, :] = y_row
                sa_ref[tt, vi:vi + 1, :] = sa
        s_ref[...] = hist_ref[SCAN_TB - 1]

        if nc:
            @pl.when(i == nt - 1)
            def _():
                wait()

    fwd_map, mir_map = (lambda i: (i, 0, 0)), (lambda i: (nt - 1 - i, 0, 0))
    kf_spec, km_spec = pl.BlockSpec((SCAN_TB, RW_N, LANE), fwd_map), pl.BlockSpec((SCAN_TB, RW_N, LANE), mir_map)
    vf_spec, vm_spec = pl.BlockSpec((SCAN_TB, RW_VH, LANE), fwd_map), pl.BlockSpec((SCAN_TB, RW_VH, LANE), mir_map)
    c_args, c_specs, c_shapes, c_sems = _comm_specs(comm)
    vshape = jax.ShapeDtypeStruct((t, RW_VH, LANE), F32)
    return pl.pallas_call(
        body, name="rwkv_scan_fwd", grid=(nt,),
        in_specs=[kf_spec, km_spec] * 5 + [vf_spec, vm_spec] + c_specs,
        out_specs=[vf_spec, vm_spec, pl.BlockSpec((SCAN_TB, RW_VH, RW_N, LANE), lambda i: (i, 0, 0, 0)), vf_spec] + c_specs,
        out_shape=[vshape, vshape, jax.ShapeDtypeStruct((t, RW_VH, RW_N, LANE), F32), vshape] + c_shapes,
        scratch_shapes=[pltpu.VMEM((RW_VH, RW_N, LANE), F32)] + c_sems,
        compiler_params=_cparams(("arbitrary",)),
    )(r, r, k, k, a, a, b, b, w, w, v, v, *c_args)


def rwkv_scan_bwd(r, w, k, a, b, v, hist, sa, dy, comm=()):
    t = r.shape[0]
    nt = t // SCAN_TB
    nc = len(comm)
    flags = [sc for _, sc in comm]

    def body(*refs):
        (rf, rm, kf, km, af, am, bf, bm, wf_ref, wm_ref, vf, vm, hist_ref, prev_ref, sa_ref, dyf, dym), refs = refs[:17], refs[17:]
        c_in, refs = refs[:nc], refs[nc:]
        k_outs, (dvf_ref, dvm_ref), refs = refs[:4], refs[4:6], refs[6:]
        c_out, refs = refs[:nc], refs[nc:]
        ds_ref, sems = refs[0], refs[1:]
        i = pl.program_id(0)
        if nc:
            start, wait = _exchange_plan(flags, c_in, c_out, *sems)

        @pl.when(i == 0)
        def _():
            ds_ref[...] = jnp.zeros_like(ds_ref)
            if nc:
                start()

        bwd = _bwd_lanes()
        group = lax.broadcasted_iota(jnp.int32, (1, LANE), 1) // RW_Q
        first_block = i == nt - 1

        for tt in range(SCAN_TB - 1, -1, -1):
            mt = SCAN_TB - 1 - tt
            pick = lambda f_ref, m_ref: jnp.where(bwd, m_ref[mt], f_ref[tt])
            rt, kt, at, bt, wt = pick(rf, rm), pick(kf, km), pick(af, am), pick(bf, bm), pick(wf_ref, wm_ref)
            zero = jnp.zeros((RW_N, LANE), F32)
            dr, dw, dk, da, db = zero, zero, zero, zero, zero
            for vi in range(RW_VH):
                sn = hist_ref[tt, vi]
                sv = hist_ref[tt - 1, vi] if tt > 0 else jnp.where(first_block, 0.0, prev_ref[0, vi])
                sa_row = sa_ref[tt, vi:vi + 1, :]
                v_row = jnp.where(bwd, vm[mt, vi:vi + 1, :], vf[tt, vi:vi + 1, :])
                dy_row = jnp.where(bwd, dym[mt, vi:vi + 1, :], dyf[tt, vi:vi + 1, :])
                dsv = ds_ref[vi] + dy_row * rt
                dr = dr + sn * dy_row
                dsa = jnp.sum(dsv * bt, axis=0, keepdims=True)
                dw = dw + sv * dsv
                db = db + dsv * sa_row
                dk = dk + dsv * v_row
                dv_row = jnp.sum(dsv * kt, axis=0, keepdims=True)
                dvf_ref[tt, vi:vi + 1, :] = dv_row
                dvm_ref[mt, vi:vi + 1, :] = dv_row
                da = da + sv * dsa
                ds_ref[vi] = dsv * wt + dsa * at
            dr, dw, dk, da, db = [val + pltpu.roll(val, LANE // 2, 1) for val in (dr, dw, dk, da, db)]
            up, down = (lambda val: pltpu.roll(val, RW_Q, 1)), (lambda val: pltpu.roll(val, LANE - RW_Q, 1))
            packed_f = jnp.where(group == 0, dr, jnp.where(group == 1, up(dk), jnp.where(group == 2, da, up(db))))
            packed_m = jnp.where(group == 0, down(dr), jnp.where(group == 1, dk, jnp.where(group == 2, down(da), db)))
            k_outs[0][tt] = packed_f
            k_outs[1][mt] = packed_m
            k_outs[2][tt] = dw
            k_outs[3][mt] = dw

        if nc:
            @pl.when(i == nt - 1)
            def _():
                wait()

    fwd_map, mir_map = (lambda i: (nt - 1 - i, 0, 0)), (lambda i: (i, 0, 0))
    kf_spec, km_spec = pl.BlockSpec((SCAN_TB, RW_N, LANE), fwd_map), pl.BlockSpec((SCAN_TB, RW_N, LANE), mir_map)
    vf_spec, vm_spec = pl.BlockSpec((SCAN_TB, RW_VH, LANE), fwd_map), pl.BlockSpec((SCAN_TB, RW_VH, LANE), mir_map)
    prev_spec = pl.BlockSpec((1, RW_VH, RW_N, LANE), lambda i: (jnp.maximum((nt - 1 - i) * SCAN_TB - 1, 0), 0, 0, 0))
    c_args, c_specs, c_shapes, c_sems = _comm_specs(comm)
    kshape, vshape = jax.ShapeDtypeStruct((t, RW_N, LANE), F32), jax.ShapeDtypeStruct((t, RW_VH, LANE), F32)
    return pl.pallas_call(
        body, name="rwkv_scan_bwd", grid=(nt,),
        in_specs=[kf_spec, km_spec] * 5 + [vf_spec, vm_spec,
                                           pl.BlockSpec((SCAN_TB, RW_VH, RW_N, LANE), lambda i: (nt - 1 - i, 0, 0, 0)),
                                           prev_spec, vf_spec, vf_spec, vm_spec] + c_specs,
        out_specs=[kf_spec, km_spec] * 2 + [vf_spec, vm_spec] + c_specs,
        out_shape=[kshape] * 4 + [vshape] * 2 + c_shapes,
        scratch_shapes=[pltpu.VMEM((RW_VH, RW_N, LANE), F32)] + c_sems,
        compiler_params=_cparams(("arbitrary",)),
    )(r, r, k, k, a, a, b, b, w, w, v, v, hist, hist, sa, dy, dy, *c_args)


RELAYOUT_TB = 128
RW_Q = LANE // 4


def to_scan(name, x, cb, nb, t, value, x_bwd=None):
    tb = min(RELAYOUT_TB, t)
    rows_out = RW_VH if value else RW_N
    ins = [x] if x_bwd is None else [x, x_bwd]

    def body(*refs):
        x_refs, o_ref, scrs = refs[:len(ins)], refs[len(ins)], refs[len(ins) + 1:]
        for x_ref, scr in zip(x_refs, scrs):
            for b in range(nb):
                scr[b * RW_H:(b + 1) * RW_H] = x_ref[b].T.reshape(RW_H, RW_N, tb)
        for j in range(rows_out):
            lo = scrs[0][:, j, :]
            if value:
                hi = scrs[0][:, j + RW_VH, :]
                blk = [lo, lo, hi, hi]
            else:
                other = lo if x_bwd is None else scrs[1][:, j, :]
                blk = [lo, other, lo, other]
            o_ref[:, j, :] = jnp.concatenate(blk, axis=0).T

    return pl.pallas_call(
        body, name=name, grid=(t // tb,),
        in_specs=[pl.BlockSpec((nb, tb, RW_W), lambda i: (0, i, cb))] + [pl.BlockSpec((nb, tb, RW_W), lambda i: (0, i, 0))] * (len(ins) - 1),
        out_specs=pl.BlockSpec((tb, rows_out, LANE), lambda i: (i, 0, 0)),
        out_shape=jax.ShapeDtypeStruct((t, rows_out, LANE), F32),
        scratch_shapes=[pltpu.VMEM((nb * RW_H, RW_N, tb), F32)] * len(ins),
        compiler_params=_cparams(("arbitrary",)),
    )(*[a.reshape(nb, t, a.shape[1]) for a in ins])


def from_scan(name, xf, xm, nb, t, value, picks=((0, 1),)):
    tb = min(RELAYOUT_TB, t)
    rows_in = RW_VH if value else RW_N
    n_out = 1 if value else len(picks)
    grp = lambda a, g: a[g * RW_Q:(g + 1) * RW_Q]

    def body(f_ref, m_ref, *rest):
        outs, scrs = rest[:n_out], rest[n_out:]
        for j in range(rows_in):
            a, b = f_ref[:, j, :].T, m_ref[:, j, :].T
            if value:
                scrs[0][:, j, :] = grp(a, 0) + grp(b, 1)
                scrs[0][:, j + RW_VH, :] = grp(a, 2) + grp(b, 3)
            else:
                for scr, (gf, gm) in zip(scrs, picks):
                    parts = ([grp(a, gf)] if gf is not None else []) + ([grp(b, gm)] if gm is not None else [])
                    scr[:, j, :] = parts[0] if len(parts) == 1 else parts[0] + parts[1]
        for o_ref, scr in zip(outs, scrs):
            for b in range(nb):
                o_ref[b] = scr[b * RW_H:(b + 1) * RW_H].reshape(RW_W, tb).T

    res = pl.pallas_call(
        body, name=name, grid=(t // tb,),
        in_specs=[pl.BlockSpec((tb, rows_in, LANE), lambda i: (i, 0, 0))] * 2,
        out_specs=[pl.BlockSpec((nb, tb, RW_W), lambda i: (0, i, 0))] * n_out,
        out_shape=[jax.ShapeDtypeStruct((nb, t, RW_W), F32)] * n_out,
        scratch_shapes=[pltpu.VMEM((nb * RW_H, RW_N, tb), F32)] * n_out,
        compiler_params=_cparams(("arbitrary",)),
    )(xf, xm)
    return [r.reshape(nb * t, RW_W) for r in res]


def f_norm(rows, params):
    (x,), (g,) = rows, params
    return [_rmsnorm(x, g)]


def f_rwkv_pre(rows, params):
    k, wlal, gl = rows
    w0f, w2f, w0b, w2b, a0, a2, g2, k_k, k_a = params
    seg = _segment_ones(RW_W, RW_N)
    tw = jnp.tanh(wlal)

    def decay(w0, w2):
        return jnp.exp(-jnp.exp(-_softplus(-(w0 + mm(tw, w2))) - 0.5))

    lr = _sigmoid(a0 + mm(wlal, a2))
    gate = mm(_sigmoid(gl), g2)
    kk = k * k_k
    kk = kk / jnp.maximum(jnp.sqrt(mm_exact(kk * kk, seg)), 1e-12)
    kp = k * (1.0 + (lr - 1.0) * k_a)
    return [decay(w0f, w2f), decay(w0b, w2b), kp, -kk, kk * lr, gate]


def f_branch_post(rows, params):
    o, og, y, r, kp, v, g = rows
    gla_g, ln_w, ln_b, r_k = params
    seg_gla = _segment_ones(GLA_H * GLA_DV, GLA_DV)
    seg_rw = _segment_ones(RW_W, RW_N)
    on = o * lax.rsqrt(mm_exact(o * o, seg_gla) * (1.0 / GLA_DV) + HEAD_NORM_EPS)
    oa = on * gla_g * _silu(og)
    mu = mm_exact(y, seg_rw) * (1.0 / RW_N)
    yc = y - mu
    var = mm_exact(yc * yc, seg_rw) * (1.0 / RW_N)
    yn = yc * lax.rsqrt(var + RW_GN_EPS) * ln_w + ln_b
    bonus = mm_exact(r * kp * r_k, seg_rw) * v
    return [oa, (yn + bonus) * g]


def f_merge(rows, params):
    ga, gb, ya, yb = rows
    return [_sigmoid(ga) * ya + _sigmoid(gb) * yb]


def f_norm2(rows, params):
    (x, mo), (g,) = rows, params
    x1 = x + mo
    return [x1, _rmsnorm(x1, g)]


def loss_head(x1, ffo, tgt, gf, tm):
    n = x1.shape[0]

    def body(x1_ref, f_ref, t_ref, g_ref, loss_ref, dx_ref, dg_ref):
        @pl.when(pl.program_id(0) == 0)
        def _():
            loss_ref[...] = jnp.zeros_like(loss_ref)
            dg_ref[...] = jnp.zeros_like(dg_ref)

        tgt_v = t_ref[...]

        def f(x2, g):
            err = _rmsnorm(x2, g) - tgt_v
            return jnp.sum(jnp.sum(err * err, axis=-1, keepdims=True), axis=0, keepdims=True) * (0.5 / D)

        val, vjp = jax.vjp(f, x1_ref[...] + f_ref[...], g_ref[...])
        dx, dg = vjp(jnp.ones((1, 1), F32))
        loss_ref[...] += val
        dx_ref[...] = dx
        dg_ref[...] += dg

    return pl.pallas_call(
        body, name="loss_head", grid=(n // tm,),
        in_specs=[_row_spec(tm, D, 0)] * 3 + [_full_spec((1, D))],
        out_specs=[_full_spec((1, 1)), _row_spec(tm, D, 0), _full_spec((1, D))],
        out_shape=[jax.ShapeDtypeStruct((1, 1), F32), jax.ShapeDtypeStruct((n, D), F32), jax.ShapeDtypeStruct((1, D), F32)],
        compiler_params=_cparams(("arbitrary",)),
    )(x1, ffo, tgt, gf)


def _pad_cols(a, width):
    return jnp.pad(a, ((0, 0), (0, width - a.shape[1])))


def w_in_to_padded(w):
    return _pad_cols(jnp.concatenate([w[:, 3360:5408], w[:, 0:1536], w[:, 1568:3360], w[:, 1536:1568]], axis=1), NP)


def w_in_from_padded(wp):
    return jnp.concatenate([wp[:, 2048:3584], wp[:, 5376:5408], wp[:, 3584:5376], wp[:, 0:2048]], axis=1)


def ff_interleave(a):
    r = a.shape[0]
    halves = jnp.stack([_pad_cols(a[:, :D_FF], FFP), _pad_cols(a[:, D_FF:], FFP)], axis=1)
    return halves.reshape(r, 2, FFP // LANE, LANE).transpose(0, 2, 1, 3).reshape(r, 2 * FFP)


def ff_deinterleave(a):
    r = a.shape[0]
    halves = a.reshape(r, FFP // LANE, 2, LANE).transpose(0, 2, 1, 3).reshape(r, 2, FFP)
    return halves[:, :, :D_FF].reshape(r, 2 * D_FF)


def _rows_into(w, rows, off):
    return jnp.zeros((rows, w.shape[1]), w.dtype).at[off:off + w.shape[0]].set(w)


LATE = ("gla_proj", "rwkv_proj", "w_out", "ffn_up", "ffn_conv_w", "ffn_down")


def local_step(x, tgt, w, nb, t, late_blocks=None):
    n = nb * t
    tm = min(n, 1024)
    tr = min(n, 256)
    vec = lambda a: a.reshape(1, -1)
    w = dict(w)

    w_in_p = w_in_to_padded(w["w_in"])
    wa2_f, wa2_b = _rows_into(w["gla_wa2_f"], LANE, 0), _rows_into(w["gla_wa2_b"], LANE, GLA_RANK)
    w2f, w2b = _rows_into(w["rwkv_w2_f"], LANE, 0), _rows_into(w["rwkv_w2_b"], LANE, 0)
    a2 = _rows_into(w["rwkv_a2"], LANE, 64)
    g1, g2n, gf = vec(w["norm1_g"]), vec(w["norm2_g"]), vec(w["norm_f_g"])
    mu_prev, mu_next = vec(w["rwkv_mu_prev"]), vec(w["rwkv_mu_next"])
    pre_params = [vec(w["rwkv_w0_f"]), w2f, vec(w["rwkv_w0_b"]), w2b, vec(w["rwkv_a0"]), a2, w["rwkv_g2"],
                  vec(w["rwkv_k_k"]), vec(w["rwkv_k_a"])]
    post_params = [vec(w["gla_norm_g"]), vec(w["rwkv_ln_w"]), vec(w["rwkv_ln_b"]), vec(w["rwkv_r_k"])]
    ba_f, ba_b = vec(w["gla_ba_f"]), vec(w["gla_ba_b"])

    (h1,) = rowwise_fwd("norm1_fwd", f_norm, [(x, D, 0)], [g1], [(D, MXU_DTYPE)], tr)
    p = matmul("proj_in", h1, w_in_p, "nn", F32, tm, 512, D)
    s = shift_fwd(p, mu_prev, mu_next, nb, t)
    pre_rows = [(s, 512, 1), (s, LANE, 1536 // LANE), (s, LANE, 1664 // LANE)]
    wf, wb, kp, a_s, b_s, g = rowwise_fwd("rwkv_pre_fwd", f_rwkv_pre, pre_rows, pre_params, [(RW_W, F32)] * 6, tr)
    sc = [to_scan("to_scan_r", s, 0, nb, t, False), to_scan("to_scan_w", wf, 0, nb, t, False, x_bwd=wb),
          to_scan("to_scan_k", kp, 0, nb, t, False), to_scan("to_scan_a", a_s, 0, nb, t, False),
          to_scan("to_scan_b", b_s, 0, nb, t, False), to_scan("to_scan_v", s, 2, nb, t, True)]
    comm = [] if late_blocks is None else [(late_blocks[k], False) for k in LATE]
    y_scf, y_scm, hist_rw, sa_sc, *gathered = rwkv_scan_fwd(*sc, comm=comm)
    for k, g_k in zip(LATE, gathered):
        w[k] = _gathered_to_full(g_k, SHARDED[k])
    ffn_up_p = ff_interleave(w["ffn_up"])
    conv_w_p, conv_b_p = ff_interleave(w["ffn_conv_w"]), ff_interleave(vec(w["ffn_conv_b"]))
    ffn_down_p = jnp.pad(w["ffn_down"], ((0, FFP - D_FF), (0, 0)))
    (y,) = from_scan("from_scan_y", y_scf, y_scm, nb, t, True)
    o_f, hist_f = gla_fwd(p, wa2_f, ba_f, None, nb, t, False)
    o, hist_b = gla_fwd(p, wa2_b, ba_b, o_f, nb, t, True)
    post_rows = [(o, 512, 0), (p, 512, C_OG // 512), (y, 512, 0), (s, 512, 0), (kp, 512, 0), (s, 512, 2), (g, 512, 0)]
    oa, ob = rowwise_fwd("branch_post_fwd", f_branch_post, post_rows, post_params, [(512, MXU_DTYPE)] * 2, tr)
    ya = matmul("gla_proj", oa, w["gla_proj"], "nn", F32, tm, 512, 512)
    yb = matmul("rwkv_proj", ob, w["rwkv_proj"], "nn", F32, tm, 512, 512)
    merge_rows = [(p, D, 0), (p, D, 1), (ya, D, 0), (yb, D, 0)]
    (merged,) = rowwise_fwd("merge_fwd", f_merge, merge_rows, [], [(D, MXU_DTYPE)], tr)
    mo = matmul("w_out", merged, w["w_out"], "nn", F32, tm, 512, D)
    x1, h2 = rowwise_fwd("norm2_fwd", f_norm2, [(x, D, 0), (mo, D, 0)], [g2n], [(D, F32), (D, MXU_DTYPE)], tr)
    u = matmul("ffn_up", h2, ffn_up_p, "nn", F32, tm, 512, D)
    z = conv_glu_fwd(u, conv_w_p, conv_b_p, nb, t)
    ffo = matmul("ffn_down", z, ffn_down_p, "nn", F32, tm, 512, FFP // 2)
    loss, dx2, dgf = loss_head(x1, ffo, tgt, gf, tr)

    dz = matmul("ffn_down_dx", dx2, ffn_down_p, "nt", F32, tm, FFP // 2, D)
    d_ffn_down_p = matmul("ffn_down_dw", z, dx2, "tn", F32, FFP // 2, 512, tm)
    du, d_conv_w_p, d_conv_b_p = conv_glu_bwd(u, dz, conv_w_p, conv_b_p, nb, t)
    dh2 = matmul("ffn_up_dx", du, ffn_up_p, "nt", F32, tm, D, 512)
    d_ffn_up_p = matmul("ffn_up_dw", h2, du, "tn", F32, D, 512, tm)
    (dx1,), (dg2,) = rowwise_bwd("norm2_bwd", f_norm2, [(x, D, 0), (mo, D, 0)], [g2n],
                                 [[(dx2, D, 0)], [(dh2, D, 0)]], tr, grad_rows=[1])
    dmerged = matmul("w_out_dx", dx1, w["w_out"], "nt", F32, tm, D, 512)
    d_w_out = matmul("w_out_dw", merged, dx1, "tn", F32, D, 512, tm)
    (dga, dgb, dya, dyb), _ = rowwise_bwd("merge_bwd", f_merge, merge_rows, [], [[(dmerged, D, 0)]], tr)
    d_oa = matmul("gla_proj_dx", dya, w["gla_proj"], "nt", F32, tm, 512, D)
    d_gla_proj = matmul("gla_proj_dw", oa, dya, "tn", F32, 512, 512, tm)
    d_ob = matmul("rwkv_proj_dx", dyb, w["rwkv_proj"], "nt", F32, tm, 512, D)
    d_rwkv_proj = matmul("rwkv_proj_dw", ob, dyb, "tn", F32, 512, 512, tm)
    (d_o, d_og, d_y, d_r_post, d_kp_post, d_v_post, d_g), d_post = rowwise_bwd(
        "branch_post_bwd", f_branch_post, post_rows, post_params, [[(d_oa, 512, 0)], [(d_ob, 512, 0)]], tr)
    late_grads = {"gla_proj": d_gla_proj, "rwkv_proj": d_rwkv_proj, "w_out": d_w_out, "ffn_up": ff_deinterleave(d_ffn_up_p),
                  "ffn_conv_w": ff_deinterleave(d_conv_w_p), "ffn_down": d_ffn_down_p[0:D_FF]}
    comm = [] if late_blocks is None else [(_full_to_slices(late_grads[k], SHARDED[k]), True) for k in LATE]
    dsc = rwkv_scan_bwd(*sc, hist_rw, sa_sc, to_scan("to_scan_dy", d_y, 0, nb, t, True), comm=comm)
    received = dict(zip(LATE, dsc[6:]))
    d_r_scan, d_kp_scan, d_a_scan, d_b_scan = from_scan("from_scan_rkab", dsc[0], dsc[1], nb, t, False,
                                                        picks=((0, 0), (1, 1), (2, 2), (3, 3)))
    d_wf, d_wb = from_scan("from_scan_w", dsc[2], dsc[3], nb, t, False, picks=((0, None), (None, 1)))
    (d_v_scan,) = from_scan("from_scan_dv", dsc[4], dsc[5], nb, t, True)
    (d_k, d_wlal, d_gl), d_pre = rowwise_bwd(
        "rwkv_pre_bwd", f_rwkv_pre, pre_rows, pre_params,
        [[(d_wf, 512, 0)], [(d_wb, 512, 0)], [(d_kp_scan, 512, 0), (d_kp_post, 512, 0)],
         [(d_a_scan, 512, 0)], [(d_b_scan, 512, 0)], [(d_g, 512, 0)]], tr)
    ds = jnp.concatenate([d_r_scan + d_r_post, d_k, d_v_scan + d_v_post, d_wlal, d_gl], axis=1)
    dp_rw, d_mu_prev, d_mu_next = shift_bwd(p, ds, mu_prev, mu_next, nb, t)
    dqkv_f, d_wa2_f, d_ba_f = gla_bwd(p, wa2_f, ba_f, hist_f, d_o, None, nb, t, False)
    dqkv, d_wa2_b, d_ba_b = gla_bwd(p, wa2_b, ba_b, hist_b, d_o, dqkv_f, nb, t, True)
    dp = jnp.concatenate([dga, dgb, dqkv[:, 0:1024], d_og, dp_rw, dqkv[:, 1024:1152],
                          jnp.zeros((n, NP - C_AFAB - LANE), F32)], axis=1).astype(MXU_DTYPE)
    d_w_in_p = matmul("proj_in_dw", h1, dp, "tn", F32, D, 512, tm)
    grads = {
        "w_in": w_in_from_padded(d_w_in_p),
        "gla_wa2_f": d_wa2_f[0:GLA_RANK], "gla_ba_f": d_ba_f, "gla_wa2_b": d_wa2_b[GLA_RANK:2 * GLA_RANK], "gla_ba_b": d_ba_b,
        "gla_norm_g": d_post[0], "rwkv_mu_prev": d_mu_prev, "rwkv_mu_next": d_mu_next,
        "rwkv_w0_f": d_pre[0], "rwkv_w2_f": d_pre[1][0:64], "rwkv_w0_b": d_pre[2], "rwkv_w2_b": d_pre[3][0:64],
        "rwkv_a0": d_pre[4], "rwkv_a2": d_pre[5][64:128], "rwkv_g2": d_pre[6], "rwkv_k_k": d_pre[7], "rwkv_k_a": d_pre[8],
        "rwkv_r_k": d_post[3], "rwkv_ln_w": d_post[1], "rwkv_ln_b": d_post[2],
        "norm2_g": dg2, "ffn_conv_b": ff_deinterleave(d_conv_b_p), "norm_f_g": dgf, **late_grads,
    }
    early = [k for k in SHARDED if k not in LATE]
    comm = [] if late_blocks is None else [(_full_to_slices(grads[k], SHARDED[k]), True) for k in early]
    dh1, *got = matmul("proj_in_dx", dp, w_in_p, "nt", F32, tm, D, 512, comm=comm) if comm else \
        [matmul("proj_in_dx", dp, w_in_p, "nt", F32, tm, D, 512)]
    received.update(zip(early, got))
    (grad_x,), (grads["norm1_g"],) = rowwise_bwd("norm1_bwd", f_norm, [(x, D, 0)], [g1], [[(dh1, D, 0)]], tr,
                                                 adds=[(0, (dx1, D, 0))])
    return loss, grad_x, grads, received


MESH = pl.DeviceIdType.MESH


def remote_exchange(name, items):
    n = len(items)

    def body(*refs):
        start, wait = _exchange_plan([sc for _, sc in items], refs[:n], refs[n:2 * n], *refs[2 * n:])
        start()
        wait()

    args, specs, shapes, sems = _comm_specs(items)
    return pl.pallas_call(body, name=name, in_specs=specs, out_specs=specs, out_shape=shapes, scratch_shapes=sems)(*args)


def _exchange_plan(flags, in_refs, out_refs, send_sems, recv_sems, local_sems):
    x, y, c = lax.axis_index("x"), lax.axis_index("y"), lax.axis_index("c")
    me = 4 * x + 2 * y + c

    def peer(k):
        px = 1 - x if (k >> 2) & 1 else x
        py = 1 - y if (k >> 1) & 1 else y
        pc = 1 - c if k & 1 else c
        return (px, py, pc), 4 * px + 2 * py + pc

    def copies(with_arrivals):
        own, sends, recvs = [], [], []
        for i, scatter in enumerate(flags):
            src = in_refs[i].at[me] if scatter else in_refs[i]
            own.append(pltpu.make_async_copy(src, out_refs[i].at[me], local_sems.at[i]))
        for k in range(1, N_DEV):
            dev, slot = peer(k)
            for i, scatter in enumerate(flags):
                src = in_refs[i].at[slot] if scatter else in_refs[i]
                pair = dict(send_sem=send_sems.at[i, k - 1], recv_sem=recv_sems.at[i, k - 1], device_id=dev, device_id_type=MESH)
                sends.append(pltpu.make_async_remote_copy(src_ref=src, dst_ref=out_refs[i].at[me], **pair))
                if with_arrivals:
                    recvs.append(pltpu.make_async_remote_copy(src_ref=out_refs[i].at[slot], dst_ref=out_refs[i].at[slot], **pair))
        return own, sends, recvs

    def start():
        own, sends, _ = copies(False)
        for cp in own + sends:
            cp.start()

    def wait():
        own, sends, recvs = copies(True)
        for send, recv in zip(sends, recvs):
            recv.wait_recv()
            send.wait_send()
        for cp in own:
            cp.wait()

    return start, wait


def _adam_tiles(r, c):
    tc = 256 if (c % 256 == 0 and r * c > 128 * 1024) else c
    tr = 128 if (r % 128 == 0 and r > 128) else r
    return tr, tc


def adamw_reduce(name, parts, w, m, v):
    r, c = w.shape
    tr, tc = _adam_tiles(r, c)

    def body(p_ref, w_ref, m_ref, v_ref, g_ref, d_ref, nm_ref, nv_ref):
        g = p_ref[0]
        for d in range(1, N_DEV):
            g = g + p_ref[d]
        nm = ADAM_B1 * m_ref[...] + (1.0 - ADAM_B1) * g
        nv = ADAM_B2 * v_ref[...] + (1.0 - ADAM_B2) * (g * g)
        m_hat = nm / (1.0 - ADAM_B1 ** ADAM_STEP)
        v_hat = nv / (1.0 - ADAM_B2 ** ADAM_STEP)
        g_ref[...] = g
        d_ref[...] = -ADAM_LR * (m_hat / (jnp.sqrt(v_hat) + ADAM_EPS) + ADAM_WD * w_ref[...])
        nm_ref[...] = nm
        nv_ref[...] = nv

    spec = pl.BlockSpec((tr, tc), lambda i, j: (i, j))
    return pl.pallas_call(
        body, name=name, grid=(r // tr, c // tc),
        in_specs=[pl.BlockSpec((N_DEV, tr, tc), lambda i, j: (0, i, j)), spec, spec, spec],
        out_specs=[spec] * 4, out_shape=[jax.ShapeDtypeStruct((r, c), F32)] * 4,
        compiler_params=_cparams(("arbitrary", "arbitrary")),
    )(parts, w, m, v)


SHARDED = {"w_in": 1, "gla_wa2_f": 1, "gla_wa2_b": 1, "gla_proj": 1, "rwkv_w2_f": 1, "rwkv_w2_b": 1, "rwkv_a2": 1,
           "rwkv_g2": 1, "rwkv_proj": 1, "w_out": 0, "ffn_up": 1, "ffn_conv_w": 1, "ffn_down": 0}
BF16_GATHER = ("w_in", "gla_proj", "rwkv_proj", "w_out", "ffn_up", "ffn_down")
REPLICATED = ("norm1_g", "gla_ba_f", "gla_ba_b", "gla_norm_g", "rwkv_mu_prev", "rwkv_mu_next", "rwkv_w0_f", "rwkv_w0_b",
              "rwkv_a0", "rwkv_k_k", "rwkv_k_a", "rwkv_r_k", "rwkv_ln_w", "rwkv_ln_b", "norm2_g", "ffn_conv_b", "norm_f_g")
WEIGHTS = ("norm1_g", "w_in", "gla_wa2_f", "gla_ba_f", "gla_wa2_b", "gla_ba_b", "gla_norm_g", "gla_proj", "rwkv_mu_prev",
           "rwkv_mu_next", "rwkv_w0_f", "rwkv_w2_f", "rwkv_w0_b", "rwkv_w2_b", "rwkv_a0", "rwkv_a2", "rwkv_g2", "rwkv_k_k",
           "rwkv_k_a", "rwkv_r_k", "rwkv_ln_w", "rwkv_ln_b", "rwkv_proj", "w_out", "norm2_g", "ffn_up", "ffn_conv_w",
           "ffn_conv_b", "ffn_down", "norm_f_g")


def _gathered_to_full(g, axis):
    if axis == 0:
        return g.reshape(N_DEV * g.shape[1], g.shape[2])
    return g.transpose(1, 0, 2).reshape(g.shape[1], N_DEV * g.shape[2])


def _full_to_slices(a, axis):
    if axis == 0:
        return a.reshape(N_DEV, a.shape[0] // N_DEV, a.shape[1])
    return a.reshape(a.shape[0], N_DEV, a.shape[1] // N_DEV).transpose(1, 0, 2)


def _pack_rows(size):
    return -(-size // (8 * LANE)) * 8


def _pack(d):
    parts = []
    for k in REPLICATED:
        rows = d[k].reshape(-1, LANE).astype(F32)
        parts.append(jnp.pad(rows, ((0, _pack_rows(rows.size) - rows.shape[0]), (0, 0))))
    return jnp.concatenate(parts, axis=0)


def _unpack(packed, shapes):
    out, pos = {}, 0
    for k in REPLICATED:
        size = int(np.prod(shapes[k]))
        out[k] = packed[pos:pos + size // LANE].reshape(shapes[k])
        pos += _pack_rows(size)
    return out


def kernel(x, norm1_g, w_in, gla_wa2_f, gla_ba_f, gla_wa2_b, gla_ba_b, gla_norm_g, gla_proj, rwkv_mu_prev, rwkv_mu_next, rwkv_w0_f, rwkv_w2_f, rwkv_w0_b, rwkv_w2_b, rwkv_a0, rwkv_a2, rwkv_g2, rwkv_k_k, rwkv_k_a, rwkv_r_k, rwkv_ln_w, rwkv_ln_b, rwkv_proj, w_out, norm2_g, ffn_up, ffn_conv_w, ffn_conv_b, ffn_down, norm_f_g, loss_target, m_norm1_g, m_w_in, m_gla_wa2_f, m_gla_ba_f, m_gla_wa2_b, m_gla_ba_b, m_gla_norm_g, m_gla_proj, m_rwkv_mu_prev, m_rwkv_mu_next, m_rwkv_w0_f, m_rwkv_w2_f, m_rwkv_w0_b, m_rwkv_w2_b, m_rwkv_a0, m_rwkv_a2, m_rwkv_g2, m_rwkv_k_k, m_rwkv_k_a, m_rwkv_r_k, m_rwkv_ln_w, m_rwkv_ln_b, m_rwkv_proj, m_w_out, m_norm2_g, m_ffn_up, m_ffn_conv_w, m_ffn_conv_b, m_ffn_down, m_norm_f_g, v_norm1_g, v_w_in, v_gla_wa2_f, v_gla_ba_f, v_gla_wa2_b, v_gla_ba_b, v_gla_norm_g, v_gla_proj, v_rwkv_mu_prev, v_rwkv_mu_next, v_rwkv_w0_f, v_rwkv_w2_f, v_rwkv_w0_b, v_rwkv_w2_b, v_rwkv_a0, v_rwkv_a2, v_rwkv_g2, v_rwkv_k_k, v_rwkv_k_a, v_rwkv_r_k, v_rwkv_ln_w, v_rwkv_ln_b, v_rwkv_proj, v_w_out, v_norm2_g, v_ffn_up, v_ffn_conv_w, v_ffn_conv_b, v_ffn_down, v_norm_f_g):
    args = locals()
    wts = {k: args[k] for k in WEIGHTS}
    mom = {k: args["m_" + k] for k in WEIGHTS}
    var = {k: args["v_" + k] for k in WEIGHTS}
    shapes = {k: wts[k].shape for k in WEIGHTS}
    nb, t = x.shape[0], x.shape[1]
    mat = lambda a: a.reshape(a.shape[-2], a.shape[-1])

    block = lambda k: mat(wts[k]).astype(MXU_DTYPE) if k in BF16_GATHER else mat(wts[k])
    early = [k for k in SHARDED if k not in LATE]
    gathered = remote_exchange("gather_weights", [(block(k), False) for k in early])
    full = {k: _gathered_to_full(g, SHARDED[k]) for k, g in zip(early, gathered)}
    for k in REPLICATED:
        full[k] = wts[k].reshape(-1) if k in ("norm_f_g", "rwkv_r_k") else wts[k][0]

    loss, grad_x, grads, received = local_step(x.reshape(nb * t, D), loss_target.reshape(nb * t, D), full, nb, t,
                                               late_blocks={k: block(k) for k in LATE})

    (rep_parts,) = remote_exchange("exchange_replicated", [(_pack(grads), False)])

    res = {}
    for k in SHARDED:
        outs = adamw_reduce("adamw_" + k, received[k], mat(wts[k]), mat(mom[k]), mat(var[k]))
        res[k] = [o.reshape(shapes[k]) for o in outs]
    packed = adamw_reduce("adamw_replicated", rep_parts, _pack(wts), _pack(mom), _pack(var))
    unpacked = [_unpack(p, shapes) for p in packed]
    for k in REPLICATED:
        res[k] = [u[k] for u in unpacked]

    total = lax.psum(loss[0, 0], ("x", "y", "c"))
    out = [total, grad_x.reshape(x.shape)]
    for j in range(4):
        out += [res[k][j] for k in WEIGHTS]
    return tuple(out)
```

```python
import functools

import jax
import jax.numpy as jnp
import numpy as np
from jax import lax
from jax.experimental import pallas as pl
from jax.experimental.pallas import tpu as pltpu

F32 = jnp.float32
MXU_DTYPE = jnp.bfloat16

D = 1024
SEQ = 2048
GLA_H, GLA_DK, GLA_DV, GLA_CHUNK = 4, 64, 128, 64
GLA_RANK = 16
GLA_LOGIT_NORM = 16.0
RW_H, RW_N = 8, 64
RW_W = 512
D_FF = 2752
NORM_EPS = 1e-6
HEAD_NORM_EPS = 1e-5
RW_GN_EPS = RW_N * 1e-5
N_DEV = 8
ADAM_LR, ADAM_B1, ADAM_B2, ADAM_EPS, ADAM_WD, ADAM_STEP = 0.001, 0.9, 0.999, 1e-08, 0.01, 10

C_GA, C_GB, C_Q, C_K, C_V, C_OG = 0, 1024, 2048, 2304, 2560, 3072
C_RW = 3584
C_R, C_RK, C_RV, C_WLAL, C_GL = 3584, 4096, 4608, 5120, 5248
C_AFAB = 5376
NP = 5632
RW_PW = 1792
FFP = 2816
LANE = 128
VMEM_LIMIT = 56 * 1024 * 1024


def _cparams(sem):
    return pltpu.CompilerParams(dimension_semantics=sem, vmem_limit_bytes=VMEM_LIMIT)


@jax.custom_vjp
def mm(a, b):
    return jnp.dot(a.astype(MXU_DTYPE), b.astype(MXU_DTYPE), preferred_element_type=F32)


def _mm_fwd(a, b):
    return mm(a, b), (a, b)


def _mm_bwd(res, g):
    a, b = res
    gb = g.astype(MXU_DTYPE)
    da = lax.dot_general(gb, b.astype(MXU_DTYPE), (((1,), (1,)), ((), ())), preferred_element_type=F32)
    db = lax.dot_general(a.astype(MXU_DTYPE), gb, (((0,), (0,)), ((), ())), preferred_element_type=F32)
    return da.astype(a.dtype), db.astype(b.dtype)


mm.defvjp(_mm_fwd, _mm_bwd)


@jax.custom_vjp
def mm_nt(a, b):
    return lax.dot_general(a.astype(MXU_DTYPE), b.astype(MXU_DTYPE), (((1,), (1,)), ((), ())), preferred_element_type=F32)


def _mm_nt_fwd(a, b):
    return mm_nt(a, b), (a, b)


def _mm_nt_bwd(res, g):
    a, b = res
    gb = g.astype(MXU_DTYPE)
    da = jnp.dot(gb, b.astype(MXU_DTYPE), preferred_element_type=F32)
    db = lax.dot_general(gb, a.astype(MXU_DTYPE), (((0,), (0,)), ((), ())), preferred_element_type=F32)
    return da.astype(a.dtype), db.astype(b.dtype)


mm_nt.defvjp(_mm_nt_fwd, _mm_nt_bwd)


@jax.custom_vjp
def mm_tn(a, b):
    return lax.dot_general(a.astype(MXU_DTYPE), b.astype(MXU_DTYPE), (((0,), (0,)), ((), ())), preferred_element_type=F32)


def _mm_tn_fwd(a, b):
    return mm_tn(a, b), (a, b)


def _mm_tn_bwd(res, g):
    a, b = res
    gb = g.astype(MXU_DTYPE)
    da = lax.dot_general(b.astype(MXU_DTYPE), gb, (((1,), (1,)), ((), ())), preferred_element_type=F32)
    db = jnp.dot(a.astype(MXU_DTYPE), gb, preferred_element_type=F32)
    return da.astype(a.dtype), db.astype(b.dtype)


mm_tn.defvjp(_mm_tn_fwd, _mm_tn_bwd)


@functools.partial(jax.custom_vjp, nondiff_argnums=(2, 3))
def sel_dot(x, s, dims, x_first):
    sb = s.astype(MXU_DTYPE)
    hi = x.astype(MXU_DTYPE)
    r1 = x - hi.astype(F32)
    mid = r1.astype(MXU_DTYPE)
    lo = (r1 - mid.astype(F32)).astype(MXU_DTYPE)
    out = None
    for part in (hi, mid, lo):
        ops = (part, sb) if x_first else (sb, part)
        d = lax.dot_general(*ops, (dims, ((), ())), preferred_element_type=F32)
        out = d if out is None else out + d
    return out


def _sel_dot_fwd(x, s, dims, x_first):
    return sel_dot(x, s, dims, x_first), s


def _sel_dot_bwd(dims, x_first, s, g):
    if x_first:
        (cx,), (cs,) = dims
        dx = sel_dot(g, s, ((1,), (1 - cs,)), True) if cx == 1 else sel_dot(g, s, ((1 - cs,), (1,)), False)
    else:
        (cs,), (cx,) = dims
        dx = sel_dot(g, s, ((1 - cs,), (0,)), False) if cx == 0 else sel_dot(g, s, ((0,), (1 - cs,)), True)
    return dx, jnp.zeros_like(s)


sel_dot.defvjp(_sel_dot_fwd, _sel_dot_bwd)


def mm_exact(a, b, b_is_01=True):
    return sel_dot(a, b, ((1,), (0,)), True) if b_is_01 else sel_dot(b, a, ((1,), (0,)), False)


def mm_tn_exact(a, b):
    return sel_dot(a, b, ((0,), (0,)), True)


def _softplus(x):
    return jnp.maximum(x, 0.0) + jnp.log(1.0 + jnp.exp(-jnp.abs(x)))


def _sigmoid(x):
    return jax.nn.sigmoid(x)


def _silu(x):
    return x * _sigmoid(x)


def _rmsnorm(x, g):
    return x * lax.rsqrt(jnp.mean(x * x, axis=-1, keepdims=True) + NORM_EPS) * g


def _segment_ones(width, seg):
    i = lax.broadcasted_iota(jnp.int32, (width, width), 0) // seg
    j = lax.broadcasted_iota(jnp.int32, (width, width), 1) // seg
    return (i == j).astype(F32)


def _row_spec(tm, width, cb):
    return pl.BlockSpec((tm, width), lambda i: (i, cb))


def _full_spec(shape):
    nd = len(shape)
    return pl.BlockSpec(tuple(shape), lambda i: (0,) * nd)


def rowwise_fwd(name, f, rows, params, outs, tm):
    n = rows[0][0].shape[0]
    nr, npar = len(rows), len(params)

    def body(*refs):
        rv = [r[...] for r in refs[:nr]]
        pv = [r[...] for r in refs[nr:nr + npar]]
        res = f(rv, pv)
        for o_ref, val in zip(refs[nr + npar:], res):
            o_ref[...] = val.astype(o_ref.dtype)

    return pl.pallas_call(
        body, name=name, grid=(n // tm,),
        in_specs=[_row_spec(tm, w, cb) for _, w, cb in rows] + [_full_spec(p.shape) for p in params],
        out_specs=[_row_spec(tm, w, 0) for w, _ in outs],
        out_shape=[jax.ShapeDtypeStruct((n, w), dt) for w, dt in outs],
        compiler_params=_cparams(("arbitrary",)),
    )(*[a for a, _, _ in rows], *params)


def rowwise_bwd(name, f, rows, params, douts, tm, adds=(), grad_rows=None):
    n = rows[0][0].shape[0]
    nr, npar = len(rows), len(params)
    grad_rows = list(range(nr)) if grad_rows is None else list(grad_rows)
    flat_d = [d for group in douts for d in group]
    nd, na, ng = len(flat_d), len(adds), len(grad_rows)

    def body(*refs):
        rv = [r[...] for r in refs[:nr]]
        pv = [r[...] for r in refs[nr:nr + npar]]
        dflat = [r[...].astype(F32) for r in refs[nr + npar:nr + npar + nd]]
        av = [r[...] for r in refs[nr + npar + nd:nr + npar + nd + na]]
        o = nr + npar + nd + na
        drow_refs, dpar_refs = refs[o:o + ng], refs[o + ng:o + ng + npar]
        dv, pos = [], 0
        for group in douts:
            dv.append(sum(dflat[pos + 1:pos + len(group)], dflat[pos]))
            pos += len(group)

        @pl.when(pl.program_id(0) == 0)
        def _():
            for r in dpar_refs:
                r[...] = jnp.zeros_like(r)

        def g(grows, pars):
            full = list(rv)
            for i, val in zip(grad_rows, grows):
                full[i] = val
            return f(full, pars)

        res, vjp = jax.vjp(g, [rv[i] for i in grad_rows], pv)
        drows, dpars = vjp([d.astype(r.dtype) for d, r in zip(dv, res)])
        drows = [d.astype(F32) for d in drows]
        for (idx, _), a in zip(adds, av):
            drows[idx] = drows[idx] + a.astype(F32)
        for r, d in zip(drow_refs, drows):
            r[...] = d
        for r, d in zip(dpar_refs, dpars):
            r[...] += d.astype(F32)

    res = pl.pallas_call(
        body, name=name, grid=(n // tm,),
        in_specs=[_row_spec(tm, w, cb) for _, w, cb in rows] + [_full_spec(p.shape) for p in params]
        + [_row_spec(tm, w, cb) for _, w, cb in flat_d] + [_row_spec(tm, w, cb) for _, (_, w, cb) in adds],
        out_specs=[_row_spec(tm, rows[i][1], 0) for i in grad_rows] + [_full_spec(p.shape) for p in params],
        out_shape=[jax.ShapeDtypeStruct((n, rows[i][1]), F32) for i in grad_rows]
        + [jax.ShapeDtypeStruct(p.shape, F32) for p in params],
        compiler_params=_cparams(("arbitrary",)),
    )(*[a for a, _, _ in rows], *params, *[a for a, _, _ in flat_d], *[a for _, (a, _, _) in adds])
    return res[:ng], res[ng:]


def matmul(name, a, b, mode, out_dtype, tm, tn, tk, comm=()):
    nc = len(comm)
    flags = [sc for _, sc in comm]
    if mode == "nn":
        (m, k), n = a.shape, b.shape[1]
        a_spec = pl.BlockSpec((tm, tk), lambda i, j, kk: (i, kk))
        b_spec = pl.BlockSpec((tk, tn), lambda i, j, kk: (kk, j))
        dims = (((1,), (0,)), ((), ()))
    elif mode == "nt":
        (m, k), n = a.shape, b.shape[0]
        a_spec = pl.BlockSpec((tm, tk), lambda i, j, kk: (i, kk))
        b_spec = pl.BlockSpec((tn, tk), lambda i, j, kk: (j, kk))
        dims = (((1,), (1,)), ((), ()))
    else:
        (k, m), n = a.shape, b.shape[1]
        a_spec = pl.BlockSpec((tk, tm), lambda i, j, kk: (kk, i))
        b_spec = pl.BlockSpec((tk, tn), lambda i, j, kk: (kk, j))
        dims = (((0,), (0,)), ((), ()))
    assert m % tm == 0 and n % tn == 0 and k % tk == 0, (name, a.shape, b.shape, tm, tn, tk)
    nk = k // tk
    grid = (m // tm, n // tn, nk)

    def body(*refs):
        a_ref, b_ref, c_in, o_ref = refs[0], refs[1], refs[2:2 + nc], refs[2 + nc]
        c_out, acc_ref, sems = refs[3 + nc:3 + 2 * nc], refs[3 + 2 * nc], refs[4 + 2 * nc:]
        kk = pl.program_id(2)
        step = (pl.program_id(0) * grid[1] + pl.program_id(1)) * nk + kk
        if nc:
            start, wait = _exchange_plan(flags, c_in, c_out, *sems)

            @pl.when(step == 0)
            def _():
                start()

        @pl.when(kk == 0)
        def _():
            acc_ref[...] = jnp.zeros_like(acc_ref)

        acc_ref[...] += lax.dot_general(a_ref[...].astype(MXU_DTYPE), b_ref[...].astype(MXU_DTYPE), dims,
                                        preferred_element_type=F32)

        @pl.when(kk == nk - 1)
        def _():
            o_ref[...] = acc_ref[...].astype(o_ref.dtype)

        if nc:
            @pl.when(step == grid[0] * grid[1] * nk - 1)
            def _():
                wait()

    c_args, c_specs, c_shapes, c_sems = _comm_specs(comm)
    res = pl.pallas_call(
        body, name=name, grid=grid,
        in_specs=[a_spec, b_spec] + c_specs,
        out_specs=[pl.BlockSpec((tm, tn), lambda i, j, kk: (i, j))] + c_specs,
        out_shape=[jax.ShapeDtypeStruct((m, n), out_dtype)] + c_shapes,
        scratch_shapes=[pltpu.VMEM((tm, tn), F32)] + c_sems,
        compiler_params=_cparams(("arbitrary", "arbitrary", "arbitrary")),
    )(a, b, *c_args)
    return res if nc else res[0]


def _prev(u, first):
    return jnp.where(first, 0.0, pltpu.roll(u, 1, 0))


def _next(u, last):
    return jnp.where(last, 0.0, pltpu.roll(u, u.shape[0] - 1, 0))


def _edge_masks(t, w):
    row = lax.broadcasted_iota(jnp.int32, (t, w), 0)
    return row == 0, row == t - 1


WIN, HALO = 64, 8
MID = slice(HALO, HALO + WIN)


def _window(ref, i, t):
    r0 = pl.multiple_of(i * WIN, WIN)
    before = ref[pl.ds(pl.multiple_of(jnp.maximum(r0 - HALO, 0), HALO), HALO), :]
    after = ref[pl.ds(pl.multiple_of(jnp.minimum(r0 + WIN, t - HALO), HALO), HALO), :]
    before = jnp.where(i == 0, 0.0, before.astype(F32))
    after = jnp.where(i == t // WIN - 1, 0.0, after.astype(F32))
    return jnp.concatenate([before, ref[pl.ds(r0, WIN), :].astype(F32), after], axis=0)


def _wprev(u):
    return pltpu.roll(u, 1, 0)


def _wnext(u):
    return pltpu.roll(u, u.shape[0] - 1, 0)


def _mid_rows(i):
    return pl.ds(pl.multiple_of(i * WIN, WIN), WIN)


def _colsum(x):
    return jnp.sum(x[MID], axis=0, keepdims=True)


SHIFT_CW = 256


def shift_fwd(p, mu_prev, mu_next, nb, t):
    cw, c0 = SHIFT_CW, C_RW // SHIFT_CW

    def body(p_ref, mp_ref, mn_ref, s_ref):
        x = p_ref[...]
        first, last = _edge_masks(t, cw)
        s_ref[...] = x + mp_ref[...] * (_prev(x, first) - x) + mn_ref[...] * (_next(x, last) - x)

    return pl.pallas_call(
        body, name="rwkv_shift_fwd", grid=(nb, RW_PW // cw),
        in_specs=[pl.BlockSpec((t, cw), lambda b, j: (b, c0 + j)), pl.BlockSpec((1, cw), lambda b, j: (0, j)),
                  pl.BlockSpec((1, cw), lambda b, j: (0, j))],
        out_specs=pl.BlockSpec((t, cw), lambda b, j: (b, j)),
        out_shape=jax.ShapeDtypeStruct((nb * t, RW_PW), F32),
        compiler_params=_cparams(("arbitrary", "arbitrary")),
    )(p, mu_prev, mu_next)


def shift_bwd(p, ds, mu_prev, mu_next, nb, t):
    cw, c0 = SHIFT_CW, C_RW // SHIFT_CW

    def body(p_ref, ds_ref, mp_ref, mn_ref, dp_ref, dmp_ref, dmn_ref):
        @pl.when(pl.program_id(1) == 0)
        def _():
            dmp_ref[...] = jnp.zeros_like(dmp_ref)
            dmn_ref[...] = jnp.zeros_like(dmn_ref)

        mp, mn = mp_ref[...], mn_ref[...]

        def step(i, carry):
            dmp, dmn = carry
            x, g = _window(p_ref, i, t), _window(ds_ref, i, t)
            dp = g * (1.0 - mp - mn) + _wnext(mp * g) + _wprev(mn * g)
            dp_ref[_mid_rows(i), :] = dp[MID]
            return dmp + _colsum(g * (_wprev(x) - x)), dmn + _colsum(g * (_wnext(x) - x))

        zero = jnp.zeros((1, cw), F32)
        dmp, dmn = lax.fori_loop(0, t // WIN, step, (zero, zero))
        dmp_ref[...] += dmp
        dmn_ref[...] += dmn

    return pl.pallas_call(
        body, name="rwkv_shift_bwd", grid=(RW_PW // cw, nb),
        in_specs=[pl.BlockSpec((t, cw), lambda j, b: (b, c0 + j)), pl.BlockSpec((t, cw), lambda j, b: (b, j)),
                  pl.BlockSpec((1, cw), lambda j, b: (0, j)), pl.BlockSpec((1, cw), lambda j, b: (0, j))],
        out_specs=[pl.BlockSpec((t, cw), lambda j, b: (b, j)), pl.BlockSpec((1, cw), lambda j, b: (0, j)),
                   pl.BlockSpec((1, cw), lambda j, b: (0, j))],
        out_shape=[jax.ShapeDtypeStruct((nb * t, RW_PW), F32), jax.ShapeDtypeStruct((1, RW_PW), F32),
                   jax.ShapeDtypeStruct((1, RW_PW), F32)],
        compiler_params=_cparams(("arbitrary", "arbitrary")),
    )(p, ds, mu_prev, mu_next)


def conv_glu_fwd(u, cw, cb, nb, t):
    def body(u_ref, w_ref, b_ref, z_ref):
        x, w = u_ref[...], w_ref[...]
        first, last = _edge_masks(t, 2 * LANE)
        c = w[0:1] * _prev(x, first) + w[1:2] * x + w[2:3] * _next(x, last) + b_ref[...]
        z_ref[...] = (_silu(c[:, :LANE]) * c[:, LANE:]).astype(z_ref.dtype)

    return pl.pallas_call(
        body, name="conv_glu_fwd", grid=(nb, FFP // LANE),
        in_specs=[pl.BlockSpec((t, 2 * LANE), lambda b, j: (b, j)), pl.BlockSpec((3, 2 * LANE), lambda b, j: (0, j)),
                  pl.BlockSpec((1, 2 * LANE), lambda b, j: (0, j))],
        out_specs=pl.BlockSpec((t, LANE), lambda b, j: (b, j)),
        out_shape=jax.ShapeDtypeStruct((nb * t, FFP), MXU_DTYPE),
        compiler_params=_cparams(("arbitrary", "arbitrary")),
    )(u, cw, cb)


def conv_glu_bwd(u, dz, cw, cb, nb, t):
    def body(u_ref, dz_ref, w_ref, b_ref, du_ref, dw_ref, db_ref):
        @pl.when(pl.program_id(1) == 0)
        def _():
            dw_ref[...] = jnp.zeros_like(dw_ref)
            db_ref[...] = jnp.zeros_like(db_ref)

        w, bias = w_ref[...], b_ref[...]

        def step(i, carry):
            x, g = _window(u_ref, i, t), _window(dz_ref, i, t)
            xp, xn = _wprev(x), _wnext(x)
            c = w[0:1] * xp + w[1:2] * x + w[2:3] * xn + bias
            cg, cv = c[:, :LANE], c[:, LANE:]
            sg = _sigmoid(cg)
            dcg = g * cv * (sg * (1.0 + cg * (1.0 - sg)))
            dcv = g * (cg * sg)
            dc = jnp.concatenate([dcg, dcv], axis=1)
            du = w[1:2] * dc + _wnext(w[0:1] * dc) + _wprev(w[2:3] * dc)
            du_ref[_mid_rows(i), :] = du[MID].astype(du_ref.dtype)
            return tuple(acc + _colsum(val) for acc, val in zip(carry, (dc * xp, dc * x, dc * xn, dc)))

        zero = jnp.zeros((1, 2 * LANE), F32)
        sums = lax.fori_loop(0, t // WIN, step, (zero, zero, zero, zero))
        for row in range(3):
            dw_ref[row:row + 1, :] += sums[row]
        db_ref[...] += sums[3]

    return pl.pallas_call(
        body, name="conv_glu_bwd", grid=(FFP // LANE, nb),
        in_specs=[pl.BlockSpec((t, 2 * LANE), lambda j, b: (b, j)), pl.BlockSpec((t, LANE), lambda j, b: (b, j)),
                  pl.BlockSpec((3, 2 * LANE), lambda j, b: (0, j)), pl.BlockSpec((1, 2 * LANE), lambda j, b: (0, j))],
        out_specs=[pl.BlockSpec((t, 2 * LANE), lambda j, b: (b, j)), pl.BlockSpec((3, 2 * LANE), lambda j, b: (0, j)),
                   pl.BlockSpec((1, 2 * LANE), lambda j, b: (0, j))],
        out_shape=[jax.ShapeDtypeStruct((nb * t, 2 * FFP), MXU_DTYPE), jax.ShapeDtypeStruct((3, 2 * FFP), F32),
                   jax.ShapeDtypeStruct((1, 2 * FFP), F32)],
        compiler_params=_cparams(("arbitrary", "arbitrary")),
    )(u, dz, cw, cb)


def _gla_chunk(q, k, v, afab, wa2p, ba, s_in, reverse, sb):
    c = GLA_CHUNK
    r = sb * c
    ri = lax.broadcasted_iota(jnp.int32, (r, r), 0)
    ci = lax.broadcasted_iota(jnp.int32, (r, r), 1)
    same = (ri // c) == (ci // c)
    keep = same & ((ci >= ri) if reverse else (ci <= ri))
    i_ref = (c - 1 - c // 2) if reverse else (c // 2)
    pick_ref = (ci == (ri // c) * c + i_ref).astype(F32)
    seq_cols = (lax.broadcasted_iota(jnp.int32, (r, sb * LANE), 0) // c) == (lax.broadcasted_iota(jnp.int32, (r, sb * LANE), 1) // LANE)
    expand = lambda x: jnp.where(seq_cols, jnp.concatenate([x] * sb, axis=1), 0.0)
    lane = lax.broadcasted_iota(jnp.int32, (1, LANE), 1)
    outs, states = [None] * GLA_H, [None] * GLA_H
    for pr in range(GLA_H // 2):
        la = -_softplus(-(mm(afab, wa2p[pr]) + ba[pr])) * (1.0 / GLA_LOGIT_NORM)
        b = mm_exact(keep.astype(F32), la, b_is_01=False)
        b_ref = mm_exact(pick_ref, b, b_is_01=False)
        b_last = mm_exact(same.astype(F32), la, b_is_01=False)
        qs = q[pr] * (GLA_DK ** -0.5)
        qi = qs * jnp.exp(b - b_ref)
        ki = k[pr] * jnp.exp(b_ref - b)
        kd = k[pr] * jnp.exp(b_last - b)
        qb = qs * jnp.exp(b)
        dec = jnp.exp(mm_tn_exact(expand(la), jnp.ones((r, LANE), F32)))
        for h in (2 * pr, 2 * pr + 1):
            m = ((lane // GLA_DK) == (h % 2)).astype(F32)
            a = jnp.where(keep, mm_nt(qi * m, ki), 0.0)
            o_intra = mm(a, v[h])
            kv = mm_tn(expand(kd * m), v[h])
            o_inter = mm(expand(qb * m), s_in[h])
            outs[h] = o_intra + o_inter
            states[h] = s_in[h] * dec + kv
    return outs, states


def _gla_load(q_ref, k_ref, v_ref, af_ref, w_ref, ba_ref, sb, rows):
    stack = lambda ref, c0: jnp.concatenate([ref[s, rows, c0:c0 + LANE] for s in range(sb)], axis=0)
    q = [stack(q_ref, pr * LANE) for pr in range(GLA_H // 2)]
    k = [stack(k_ref, pr * LANE) for pr in range(GLA_H // 2)]
    v = [stack(v_ref, h * GLA_DV) for h in range(GLA_H)]
    w = [w_ref[:, pr * LANE:(pr + 1) * LANE] for pr in range(GLA_H // 2)]
    ba = [ba_ref[:, pr * LANE:(pr + 1) * LANE] for pr in range(GLA_H // 2)]
    return q, k, v, stack(af_ref, 0), w, ba


GLA_TILE = 256
GLA_SB = 4


def _gla_specs(nb, t, reverse):
    tile = min(GLA_TILE, t)
    nt = t // tile
    sb = GLA_SB if nb % GLA_SB == 0 else 1
    return tile, tile // GLA_CHUNK, nt, sb, ((lambda j: nt - 1 - j) if reverse else (lambda j: j))


def gla_fwd(p, wa2p, ba, o_add, nb, t, reverse):
    tile, cpt, nt, sb, tj = _gla_specs(nb, t, reverse)
    has_add = o_add is not None

    def body(*refs):
        if has_add:
            q_ref, k_ref, v_ref, af_ref, w_ref, ba_ref, add_ref, o_ref, hist_ref, s_ref = refs
        else:
            q_ref, k_ref, v_ref, af_ref, w_ref, ba_ref, o_ref, hist_ref, s_ref = refs

        @pl.when(pl.program_id(1) == 0)
        def _():
            s_ref[...] = jnp.zeros_like(s_ref)

        def step(i, carry):
            ci = (cpt - 1 - i) if reverse else i
            rows = pl.ds(pl.multiple_of(ci * GLA_CHUNK, GLA_CHUNK), GLA_CHUNK)
            s_in = [s_ref[h] for h in range(GLA_H)]
            for h in range(GLA_H):
                for s in range(sb):
                    hist_ref[s, ci, h] = s_in[h][s * LANE:(s + 1) * LANE]
            q, k, v, af, w, ba = _gla_load(q_ref, k_ref, v_ref, af_ref, w_ref, ba_ref, sb, rows)
            outs, states = _gla_chunk(q, k, v, af, w, ba, s_in, reverse, sb)
            for h in range(GLA_H):
                for s in range(sb):
                    oh = outs[h][s * GLA_CHUNK:(s + 1) * GLA_CHUNK]
                    if has_add:
                        oh = oh + add_ref[s, rows, h * GLA_DV:(h + 1) * GLA_DV]
                    o_ref[s, rows, h * GLA_DV:(h + 1) * GLA_DV] = oh
                s_ref[h] = states[h]
            return carry

        lax.fori_loop(0, cpt, step, 0)

    col = lambda width, c0: pl.BlockSpec((sb, tile, width), lambda b, j: (b, tj(j), c0 // width))
    in_specs = [col(256, C_Q), col(256, C_K), col(512, C_V), col(LANE, C_AFAB),
                pl.BlockSpec((LANE, 256), lambda b, j: (0, 0)), pl.BlockSpec((1, 256), lambda b, j: (0, 0))]
    p3 = p.reshape(nb, t, p.shape[1])
    args = [p3, p3, p3, p3, wa2p, ba]
    if has_add:
        in_specs.append(col(512, 0))
        args.append(o_add.reshape(nb, t, 512))
    o, hist = pl.pallas_call(
        body, name="gla_fwd_rev" if reverse else "gla_fwd", grid=(nb // sb, nt),
        in_specs=in_specs,
        out_specs=[col(512, 0), pl.BlockSpec((sb, cpt, GLA_H, LANE, LANE), lambda b, j: (b, tj(j), 0, 0, 0))],
        out_shape=[jax.ShapeDtypeStruct((nb, t, 512), F32),
                   jax.ShapeDtypeStruct((nb, t // GLA_CHUNK, GLA_H, LANE, LANE), F32)],
        scratch_shapes=[pltpu.VMEM((GLA_H, sb * LANE, LANE), F32)],
        compiler_params=_cparams(("arbitrary", "arbitrary")),
    )(*args)
    return o.reshape(nb * t, 512), hist


def gla_bwd(p, wa2p, ba, hist, do, dprev, nb, t, reverse):
    tile, cpt, nt, sb, tj_f = _gla_specs(nb, t, reverse)
    tj = lambda j: tj_f(nt - 1 - j)
    has_prev = dprev is not None

    def body(*refs):
        if has_prev:
            q_ref, k_ref, v_ref, af_ref, w_ref, ba_ref, hist_ref, do_ref, prev_ref, dqkv_ref, dw_ref, dba_ref, ds_ref = refs
        else:
            q_ref, k_ref, v_ref, af_ref, w_ref, ba_ref, hist_ref, do_ref, dqkv_ref, dw_ref, dba_ref, ds_ref = refs

        @pl.when((pl.program_id(0) == 0) & (pl.program_id(1) == 0))
        def _():
            dw_ref[...] = jnp.zeros_like(dw_ref)
            dba_ref[...] = jnp.zeros_like(dba_ref)

        @pl.when(pl.program_id(1) == 0)
        def _():
            ds_ref[...] = jnp.zeros_like(ds_ref)

        def step(i, carry):
            ci = i if reverse else (cpt - 1 - i)
            rows = pl.ds(pl.multiple_of(ci * GLA_CHUNK, GLA_CHUNK), GLA_CHUNK)
            fn = functools.partial(_gla_chunk, reverse=reverse, sb=sb)
            seqs = lambda get: jnp.concatenate([get(s) for s in range(sb)], axis=0)
            s_in = [seqs(lambda s: hist_ref[s, ci, h]) for h in range(GLA_H)]
            q, k, v, af, w, ba = _gla_load(q_ref, k_ref, v_ref, af_ref, w_ref, ba_ref, sb, rows)
            _, vjp = jax.vjp(fn, q, k, v, af, w, ba, s_in)
            d_o = [seqs(lambda s: do_ref[s, rows, h * GLA_DV:(h + 1) * GLA_DV]) for h in range(GLA_H)]
            d_s = [ds_ref[h] for h in range(GLA_H)]
            dq, dk, dv, daf, dw, dba, ds_in = vjp((d_o, d_s))
            pieces = [(pr * LANE, dq[pr]) for pr in range(2)] + [(256 + pr * LANE, dk[pr]) for pr in range(2)]
            pieces += [(512 + h * GLA_DV, dv[h]) for h in range(GLA_H)] + [(1024, daf)]
            for c0, val in pieces:
                for s in range(sb):
                    part = val[s * GLA_CHUNK:(s + 1) * GLA_CHUNK]
                    if has_prev:
                        part = part + prev_ref[s, rows, c0:c0 + LANE]
                    dqkv_ref[s, rows, c0:c0 + LANE] = part
            for pr in range(2):
                dw_ref[:, pr * LANE:(pr + 1) * LANE] += dw[pr]
                dba_ref[:, pr * LANE:(pr + 1) * LANE] += dba[pr]
            for h in range(GLA_H):
                ds_ref[h] = ds_in[h]
            return carry

        lax.fori_loop(0, cpt, step, 0)

    col = lambda width, c0: pl.BlockSpec((sb, tile, width), lambda b, j: (b, tj(j), c0 // width))
    in_specs = [col(256, C_Q), col(256, C_K), col(512, C_V), col(LANE, C_AFAB),
                pl.BlockSpec((LANE, 256), lambda b, j: (0, 0)), pl.BlockSpec((1, 256), lambda b, j: (0, 0)),
                pl.BlockSpec((sb, cpt, GLA_H, LANE, LANE), lambda b, j: (b, tj(j), 0, 0, 0)), col(512, 0)]
    p3 = p.reshape(nb, t, p.shape[1])
    args = [p3, p3, p3, p3, wa2p, ba, hist, do.reshape(nb, t, 512)]
    if has_prev:
        in_specs.append(col(1152, 0))
        args.append(dprev.reshape(nb, t, 1152))
    dqkv, dw, dba = pl.pallas_call(
        body, name="gla_bwd_rev" if reverse else "gla_bwd", grid=(nb // sb, nt),
        in_specs=in_specs,
        out_specs=[col(1152, 0), pl.BlockSpec((LANE, 256), lambda b, j: (0, 0)), pl.BlockSpec((1, 256), lambda b, j: (0, 0))],
        out_shape=[jax.ShapeDtypeStruct((nb, t, 1152), F32), jax.ShapeDtypeStruct((LANE, 256), F32),
                   jax.ShapeDtypeStruct((1, 256), F32)],
        scratch_shapes=[pltpu.VMEM((GLA_H, sb * LANE, LANE), F32)],
        compiler_params=_cparams(("arbitrary", "arbitrary")),
    )(*args)
    return dqkv.reshape(nb * t, 1152), dw, dba


SCAN_TB = 8
RW_VH = RW_N // 2


def _bwd_lanes():
    lane = lax.broadcasted_iota(jnp.int32, (1, LANE), 1)
    return ((lane // (LANE // 4)) % 2) == 1


def _comm_specs(comm):
    anyspec = pl.BlockSpec(memory_space=pl.ANY)
    n = len(comm)
    shapes = [jax.ShapeDtypeStruct((N_DEV,) + (a.shape[1:] if sc else a.shape), a.dtype) for a, sc in comm]
    sems = [pltpu.SemaphoreType.DMA((n, N_DEV - 1)), pltpu.SemaphoreType.DMA((n, N_DEV - 1)), pltpu.SemaphoreType.DMA((n,))] if n else []
    return [a for a, _ in comm], [anyspec] * n, shapes, sems


def rwkv_scan_fwd(r, w, k, a, b, v, comm=()):
    t = r.shape[0]
    nt = t // SCAN_TB
    nc = len(comm)
    flags = [sc for _, sc in comm]

    def body(*refs):
        (rf, rm, kf, km, af, am, bf, bm, wf_ref, wm_ref, vf, vm), refs = refs[:12], refs[12:]
        c_in, refs = refs[:nc], refs[nc:]
        (yf_ref, ym_ref, hist_ref, sa_ref), refs = refs[:4], refs[4:]
        c_out, refs = refs[:nc], refs[nc:]
        s_ref, sems = refs[0], refs[1:]
        i = pl.program_id(0)
        if nc:
            start, wait = _exchange_plan(flags, c_in, c_out, *sems)

        @pl.when(i == 0)
        def _():
            s_ref[...] = jnp.zeros_like(s_ref)
            if nc:
                start()

        bwd = _bwd_lanes()

        for tt in range(SCAN_TB):
            mt = SCAN_TB - 1 - tt
            pick = lambda f_ref, m_ref: jnp.where(bwd, m_ref[mt], f_ref[tt])
            rt, kt, at, bt, wt = pick(rf, rm), pick(kf, km), pick(af, am), pick(bf, bm), pick(wf_ref, wm_ref)
            for vi in range(RW_VH):
                sv = s_ref[vi] if tt == 0 else hist_ref[tt - 1, vi]
                sa = jnp.sum(sv * at, axis=0, keepdims=True)
                v_row = jnp.where(bwd, vm[mt, vi:vi + 1, :], vf[tt, vi:vi + 1, :])
                sn = sv * wt + sa * bt + v_row * kt
                hist_ref[tt, vi] = sn
                y_row = jnp.sum(sn * rt, axis=0, keepdims=True)
                yf_ref[tt, vi:vi + 1, :] = y_row
                ym_ref[mt, vi:vi + 1, :] = y_row
                sa_ref[tt, vi:vi + 1, :] = sa
        s_ref[...] = hist_ref[SCAN_TB - 1]

        if nc:
            @pl.when(i == nt - 1)
            def _():
                wait()

    fwd_map, mir_map = (lambda i: (i, 0, 0)), (lambda i: (nt - 1 - i, 0, 0))
    kf_spec, km_spec = pl.BlockSpec((SCAN_TB, RW_N, LANE), fwd_map), pl.BlockSpec((SCAN_TB, RW_N, LANE), mir_map)
    vf_spec, vm_spec = pl.BlockSpec((SCAN_TB, RW_VH, LANE), fwd_map), pl.BlockSpec((SCAN_TB, RW_VH, LANE), mir_map)
    c_args, c_specs, c_shapes, c_sems = _comm_specs(comm)
    vshape = jax.ShapeDtypeStruct((t, RW_VH, LANE), F32)
    return pl.pallas_call(
        body, name="rwkv_scan_fwd", grid=(nt,),
        in_specs=[kf_spec, km_spec] * 5 + [vf_spec, vm_spec] + c_specs,
        out_specs=[vf_spec, vm_spec, pl.BlockSpec((SCAN_TB, RW_VH, RW_N, LANE), lambda i: (i, 0, 0, 0)), vf_spec] + c_specs,
        out_shape=[vshape, vshape, jax.ShapeDtypeStruct((t, RW_VH, RW_N, LANE), F32), vshape] + c_shapes,
        scratch_shapes=[pltpu.VMEM((RW_VH, RW_N, LANE), F32)] + c_sems,
        compiler_params=_cparams(("arbitrary",)),
    )(r, r, k, k, a, a, b, b, w, w, v, v, *c_args)


def rwkv_scan_bwd(r, w, k, a, b, v, hist, sa, dy, comm=()):
    t = r.shape[0]
    nt = t // SCAN_TB
    nc = len(comm)
    flags = [sc for _, sc in comm]

    def body(*refs):
        (rf, rm, kf, km, af, am, bf, bm, wf_ref, wm_ref, vf, vm, hist_ref, prev_ref, sa_ref, dyf, dym), refs = refs[:17], refs[17:]
        c_in, refs = refs[:nc], refs[nc:]
        k_outs, (dvf_ref, dvm_ref), refs = refs[:4], refs[4:6], refs[6:]
        c_out, refs = refs[:nc], refs[nc:]
        ds_ref, sems = refs[0], refs[1:]
        i = pl.program_id(0)
        if nc:
            start, wait = _exchange_plan(flags, c_in, c_out, *sems)

        @pl.when(i == 0)
        def _():
            ds_ref[...] = jnp.zeros_like(ds_ref)
            if nc:
                start()

        bwd = _bwd_lanes()
        group = lax.broadcasted_iota(jnp.int32, (1, LANE), 1) // RW_Q
        first_block = i == nt - 1

        for tt in range(SCAN_TB - 1, -1, -1):
            mt = SCAN_TB - 1 - tt
            pick = lambda f_ref, m_ref: jnp.where(bwd, m_ref[mt], f_ref[tt])
            rt, kt, at, bt, wt = pick(rf, rm), pick(kf, km), pick(af, am), pick(bf, bm), pick(wf_ref, wm_ref)
            zero = jnp.zeros((RW_N, LANE), F32)
            dr, dw, dk, da, db = zero, zero, zero, zero, zero
            for vi in range(RW_VH):
                sn = hist_ref[tt, vi]
                sv = hist_ref[tt - 1, vi] if tt > 0 else jnp.where(first_block, 0.0, prev_ref[0, vi])
                sa_row = sa_ref[tt, vi:vi + 1, :]
                v_row = jnp.where(bwd, vm[mt, vi:vi + 1, :], vf[tt, vi:vi + 1, :])
                dy_row = jnp.where(bwd, dym[mt, vi:vi + 1, :], dyf[tt, vi:vi + 1, :])
                dsv = ds_ref[vi] + dy_row * rt
                dr = dr + sn * dy_row
                dsa = jnp.sum(dsv * bt, axis=0, keepdims=True)
                dw = dw + sv * dsv
                db = db + dsv * sa_row
                dk = dk + dsv * v_row
                dv_row = jnp.sum(dsv * kt, axis=0, keepdims=True)
                dvf_ref[tt, vi:vi + 1, :] = dv_row
                dvm_ref[mt, vi:vi + 1, :] = dv_row
                da = da + sv * dsa
                ds_ref[vi] = dsv * wt + dsa * at
            dr, dw, dk, da, db = [val + pltpu.roll(val, LANE // 2, 1) for val in (dr, dw, dk, da, db)]
            up, down = (lambda val: pltpu.roll(val, RW_Q, 1)), (lambda val: pltpu.roll(val, LANE - RW_Q, 1))
            packed_f = jnp.where(group == 0, dr, jnp.where(group == 1, up(dk), jnp.where(group == 2, da, up(db))))
            packed_m = jnp.where(group == 0, down(dr), jnp.where(group == 1, dk, jnp.where(group == 2, down(da), db)))
            k_outs[0][tt] = packed_f
            k_outs[1][mt] = packed_m
            k_outs[2][tt] = dw
            k_outs[3][mt] = dw

        if nc:
            @pl.when(i == nt - 1)
            def _():
                wait()

    fwd_map, mir_map = (lambda i: (nt - 1 - i, 0, 0)), (lambda i: (i, 0, 0))
    kf_spec, km_spec = pl.BlockSpec((SCAN_TB, RW_N, LANE), fwd_map), pl.BlockSpec((SCAN_TB, RW_N, LANE), mir_map)
    vf_spec, vm_spec = pl.BlockSpec((SCAN_TB, RW_VH, LANE), fwd_map), pl.BlockSpec((SCAN_TB, RW_VH, LANE), mir_map)
    prev_spec = pl.BlockSpec((1, RW_VH, RW_N, LANE), lambda i: (jnp.maximum((nt - 1 - i) * SCAN_TB - 1, 0), 0, 0, 0))
    c_args, c_specs, c_shapes, c_sems = _comm_specs(comm)
    kshape, vshape = jax.ShapeDtypeStruct((t, RW_N, LANE), F32), jax.ShapeDtypeStruct((t, RW_VH, LANE), F32)
    return pl.pallas_call(
        body, name="rwkv_scan_bwd", grid=(nt,),
        in_specs=[kf_spec, km_spec] * 5 + [vf_spec, vm_spec,
                                           pl.BlockSpec((SCAN_TB, RW_VH, RW_N, LANE), lambda i: (nt - 1 - i, 0, 0, 0)),
                                           prev_spec, vf_spec, vf_spec, vm_spec] + c_specs,
        out_specs=[kf_spec, km_spec] * 2 + [vf_spec, vm_spec] + c_specs,
        out_shape=[kshape] * 4 + [vshape] * 2 + c_shapes,
        scratch_shapes=[pltpu.VMEM((RW_VH, RW_N, LANE), F32)] + c_sems,
        compiler_params=_cparams(("arbitrary",)),
    )(r, r, k, k, a, a, b, b, w, w, v, v, hist, hist, sa, dy, dy, *c_args)


RELAYOUT_TB = 128
RW_Q = LANE // 4


def to_scan(name, x, cb, nb, t, value, x_bwd=None):
    tb = min(RELAYOUT_TB, t)
    rows_out = RW_VH if value else RW_N
    ins = [x] if x_bwd is None else [x, x_bwd]

    def body(*refs):
        x_refs, o_ref, scrs = refs[:len(ins)], refs[len(ins)], refs[len(ins) + 1:]
        for x_ref, scr in zip(x_refs, scrs):
            for b in range(nb):
                scr[b * RW_H:(b + 1) * RW_H] = x_ref[b].T.reshape(RW_H, RW_N, tb)
        for j in range(rows_out):
            lo = scrs[0][:, j, :]
            if value:
                hi = scrs[0][:, j + RW_VH, :]
                blk = [lo, lo, hi, hi]
            else:
                other = lo if x_bwd is None else scrs[1][:, j, :]
                blk = [lo, other, lo, other]
            o_ref[:, j, :] = jnp.concatenate(blk, axis=0).T

    return pl.pallas_call(
        body, name=name, grid=(t // tb,),
        in_specs=[pl.BlockSpec((nb, tb, RW_W), lambda i: (0, i, cb))] + [pl.BlockSpec((nb, tb, RW_W), lambda i: (0, i, 0))] * (len(ins) - 1),
        out_specs=pl.BlockSpec((tb, rows_out, LANE), lambda i: (i, 0, 0)),
        out_shape=jax.ShapeDtypeStruct((t, rows_out, LANE), F32),
        scratch_shapes=[pltpu.VMEM((nb * RW_H, RW_N, tb), F32)] * len(ins),
        compiler_params=_cparams(("arbitrary",)),
    )(*[a.reshape(nb, t, a.shape[1]) for a in ins])


def from_scan(name, xf, xm, nb, t, value, groups=None):
    tb = min(RELAYOUT_TB, t)
    rows_in = RW_VH if value else RW_N
    n_out = 1 if value else (4 if groups is None else 2)
    grp = lambda a, g: a[g * RW_Q:(g + 1) * RW_Q]

    def body(f_ref, m_ref, *rest):
        outs, scrs = rest[:n_out], rest[n_out:]
        lane_group = lax.broadcasted_iota(jnp.int32, (1, LANE), 1) // RW_Q
        for j in range(rows_in):
            f, m = f_ref[:, j, :], m_ref[:, j, :]
            if value:
                c = jnp.where(_bwd_lanes(), m, f).T
                scrs[0][:, j, :] = grp(c, 0) + grp(c, 1)
                scrs[0][:, j + RW_VH, :] = grp(c, 2) + grp(c, 3)
            elif groups is None:
                c = (f + m).T
                for q, scr in enumerate(scrs):
                    scr[:, j, :] = grp(c, q)
            else:
                c = jnp.where(lane_group == groups[1], m, f).T
                scrs[0][:, j, :] = grp(c, groups[0])
                scrs[1][:, j, :] = grp(c, groups[1])
        for o_ref, scr in zip(outs, scrs):
            for b in range(nb):
                o_ref[b] = scr[b * RW_H:(b + 1) * RW_H].reshape(RW_W, tb).T

    res = pl.pallas_call(
        body, name=name, grid=(t // tb,),
        in_specs=[pl.BlockSpec((tb, rows_in, LANE), lambda i: (i, 0, 0))] * 2,
        out_specs=[pl.BlockSpec((nb, tb, RW_W), lambda i: (0, i, 0))] * n_out,
        out_shape=[jax.ShapeDtypeStruct((nb, t, RW_W), F32)] * n_out,
        scratch_shapes=[pltpu.VMEM((nb * RW_H, RW_N, tb), F32)] * n_out,
        compiler_params=_cparams(("arbitrary",)),
    )(xf, xm)
    return [r.reshape(nb * t, RW_W) for r in res]


def f_norm(rows, params):
    (x,), (g,) = rows, params
    return [_rmsnorm(x, g)]


def f_rwkv_pre(rows, params):
    k, wlal, gl = rows
    w0f, w2f, w0b, w2b, a0, a2, g2, k_k, k_a = params
    seg = _segment_ones(RW_W, RW_N)
    tw = jnp.tanh(wlal)

    def decay(w0, w2):
        return jnp.exp(-jnp.exp(-_softplus(-(w0 + mm(tw, w2))) - 0.5))

    lr = _sigmoid(a0 + mm(wlal, a2))
    gate = mm(_sigmoid(gl), g2)
    kk = k * k_k
    kk = kk / jnp.maximum(jnp.sqrt(mm_exact(kk * kk, seg)), 1e-12)
    kp = k * (1.0 + (lr - 1.0) * k_a)
    return [decay(w0f, w2f), decay(w0b, w2b), kp, -kk, kk * lr, gate]


def f_branch_post(rows, params):
    o, og, y, r, kp, v, g = rows
    gla_g, ln_w, ln_b, r_k = params
    seg_gla = _segment_ones(GLA_H * GLA_DV, GLA_DV)
    seg_rw = _segment_ones(RW_W, RW_N)
    on = o * lax.rsqrt(mm_exact(o * o, seg_gla) * (1.0 / GLA_DV) + HEAD_NORM_EPS)
    oa = on * gla_g * _silu(og)
    mu = mm_exact(y, seg_rw) * (1.0 / RW_N)
    yc = y - mu
    var = mm_exact(yc * yc, seg_rw) * (1.0 / RW_N)
    yn = yc * lax.rsqrt(var + RW_GN_EPS) * ln_w + ln_b
    bonus = mm_exact(r * kp * r_k, seg_rw) * v
    return [oa, (yn + bonus) * g]


def f_merge(rows, params):
    ga, gb, ya, yb = rows
    return [_sigmoid(ga) * ya + _sigmoid(gb) * yb]


def f_norm2(rows, params):
    (x, mo), (g,) = rows, params
    x1 = x + mo
    return [x1, _rmsnorm(x1, g)]


def loss_head(x1, ffo, tgt, gf, tm):
    n = x1.shape[0]

    def body(x1_ref, f_ref, t_ref, g_ref, loss_ref, dx_ref, dg_ref):
        @pl.when(pl.program_id(0) == 0)
        def _():
            loss_ref[...] = jnp.zeros_like(loss_ref)
            dg_ref[...] = jnp.zeros_like(dg_ref)

        tgt_v = t_ref[...]

        def f(x2, g):
            err = _rmsnorm(x2, g) - tgt_v
            return jnp.sum(jnp.sum(err * err, axis=-1, keepdims=True), axis=0, keepdims=True) * (0.5 / D)

        val, vjp = jax.vjp(f, x1_ref[...] + f_ref[...], g_ref[...])
        dx, dg = vjp(jnp.ones((1, 1), F32))
        loss_ref[...] += val
        dx_ref[...] = dx
        dg_ref[...] += dg

    return pl.pallas_call(
        body, name="loss_head", grid=(n // tm,),
        in_specs=[_row_spec(tm, D, 0)] * 3 + [_full_spec((1, D))],
        out_specs=[_full_spec((1, 1)), _row_spec(tm, D, 0), _full_spec((1, D))],
        out_shape=[jax.ShapeDtypeStruct((1, 1), F32), jax.ShapeDtypeStruct((n, D), F32), jax.ShapeDtypeStruct((1, D), F32)],
        compiler_params=_cparams(("arbitrary",)),
    )(x1, ffo, tgt, gf)


def _pad_cols(a, width):
    return jnp.pad(a, ((0, 0), (0, width - a.shape[1])))


def w_in_to_padded(w):
    return _pad_cols(jnp.concatenate([w[:, 3360:5408], w[:, 0:1536], w[:, 1568:3360], w[:, 1536:1568]], axis=1), NP)


def w_in_from_padded(wp):
    return jnp.concatenate([wp[:, 2048:3584], wp[:, 5376:5408], wp[:, 3584:5376], wp[:, 0:2048]], axis=1)


def ff_interleave(a):
    r = a.shape[0]
    halves = jnp.stack([_pad_cols(a[:, :D_FF], FFP), _pad_cols(a[:, D_FF:], FFP)], axis=1)
    return halves.reshape(r, 2, FFP // LANE, LANE).transpose(0, 2, 1, 3).reshape(r, 2 * FFP)


def ff_deinterleave(a):
    r = a.shape[0]
    halves = a.reshape(r, FFP // LANE, 2, LANE).transpose(0, 2, 1, 3).reshape(r, 2, FFP)
    return halves[:, :, :D_FF].reshape(r, 2 * D_FF)


def _rows_into(w, rows, off):
    return jnp.zeros((rows, w.shape[1]), w.dtype).at[off:off + w.shape[0]].set(w)


LATE = ("gla_proj", "rwkv_proj", "w_out", "ffn_up", "ffn_conv_w", "ffn_down")


def local_step(x, tgt, w, nb, t, late_blocks=None):
    n = nb * t
    tm = min(n, 1024)
    tr = min(n, 256)
    vec = lambda a: a.reshape(1, -1)
    w = dict(w)

    w_in_p = w_in_to_padded(w["w_in"])
    wa2_f, wa2_b = _rows_into(w["gla_wa2_f"], LANE, 0), _rows_into(w["gla_wa2_b"], LANE, GLA_RANK)
    w2f, w2b = _rows_into(w["rwkv_w2_f"], LANE, 0), _rows_into(w["rwkv_w2_b"], LANE, 0)
    a2 = _rows_into(w["rwkv_a2"], LANE, 64)
    g1, g2n, gf = vec(w["norm1_g"]), vec(w["norm2_g"]), vec(w["norm_f_g"])
    mu_prev, mu_next = vec(w["rwkv_mu_prev"]), vec(w["rwkv_mu_next"])
    pre_params = [vec(w["rwkv_w0_f"]), w2f, vec(w["rwkv_w0_b"]), w2b, vec(w["rwkv_a0"]), a2, w["rwkv_g2"],
                  vec(w["rwkv_k_k"]), vec(w["rwkv_k_a"])]
    post_params = [vec(w["gla_norm_g"]), vec(w["rwkv_ln_w"]), vec(w["rwkv_ln_b"]), vec(w["rwkv_r_k"])]
    ba_f, ba_b = vec(w["gla_ba_f"]), vec(w["gla_ba_b"])

    (h1,) = rowwise_fwd("norm1_fwd", f_norm, [(x, D, 0)], [g1], [(D, MXU_DTYPE)], tr)
    p = matmul("proj_in", h1, w_in_p, "nn", F32, tm, 512, D)
    s = shift_fwd(p, mu_prev, mu_next, nb, t)
    pre_rows = [(s, 512, 1), (s, LANE, 1536 // LANE), (s, LANE, 1664 // LANE)]
    wf, wb, kp, a_s, b_s, g = rowwise_fwd("rwkv_pre_fwd", f_rwkv_pre, pre_rows, pre_params, [(RW_W, F32)] * 6, tr)
    sc = [to_scan("to_scan_r", s, 0, nb, t, False), to_scan("to_scan_w", wf, 0, nb, t, False, x_bwd=wb),
          to_scan("to_scan_k", kp, 0, nb, t, False), to_scan("to_scan_a", a_s, 0, nb, t, False),
          to_scan("to_scan_b", b_s, 0, nb, t, False), to_scan("to_scan_v", s, 2, nb, t, True)]
    comm = [] if late_blocks is None else [(late_blocks[k], False) for k in LATE]
    y_scf, y_scm, hist_rw, sa_sc, *gathered = rwkv_scan_fwd(*sc, comm=comm)
    for k, g_k in zip(LATE, gathered):
        w[k] = _gathered_to_full(g_k, SHARDED[k])
    ffn_up_p = ff_interleave(w["ffn_up"])
    conv_w_p, conv_b_p = ff_interleave(w["ffn_conv_w"]), ff_interleave(vec(w["ffn_conv_b"]))
    ffn_down_p = jnp.pad(w["ffn_down"], ((0, FFP - D_FF), (0, 0)))
    (y,) = from_scan("from_scan_y", y_scf, y_scm, nb, t, True)
    o_f, hist_f = gla_fwd(p, wa2_f, ba_f, None, nb, t, False)
    o, hist_b = gla_fwd(p, wa2_b, ba_b, o_f, nb, t, True)
    post_rows = [(o, 512, 0), (p, 512, C_OG // 512), (y, 512, 0), (s, 512, 0), (kp, 512, 0), (s, 512, 2), (g, 512, 0)]
    oa, ob = rowwise_fwd("branch_post_fwd", f_branch_post, post_rows, post_params, [(512, MXU_DTYPE)] * 2, tr)
    ya = matmul("gla_proj", oa, w["gla_proj"], "nn", F32, tm, 512, 512)
    yb = matmul("rwkv_proj", ob, w["rwkv_proj"], "nn", F32, tm, 512, 512)
    merge_rows = [(p, D, 0), (p, D, 1), (ya, D, 0), (yb, D, 0)]
    (merged,) = rowwise_fwd("merge_fwd", f_merge, merge_rows, [], [(D, MXU_DTYPE)], tr)
    mo = matmul("w_out", merged, w["w_out"], "nn", F32, tm, 512, D)
    x1, h2 = rowwise_fwd("norm2_fwd", f_norm2, [(x, D, 0), (mo, D, 0)], [g2n], [(D, F32), (D, MXU_DTYPE)], tr)
    u = matmul("ffn_up", h2, ffn_up_p, "nn", F32, tm, 512, D)
    z = conv_glu_fwd(u, conv_w_p, conv_b_p, nb, t)
    ffo = matmul("ffn_down", z, ffn_down_p, "nn", F32, tm, 512, FFP // 2)
    loss, dx2, dgf = loss_head(x1, ffo, tgt, gf, tr)

    dz = matmul("ffn_down_dx", dx2, ffn_down_p, "nt", F32, tm, FFP // 2, D)
    d_ffn_down_p = matmul("ffn_down_dw", z, dx2, "tn", F32, FFP // 2, 512, tm)
    du, d_conv_w_p, d_conv_b_p = conv_glu_bwd(u, dz, conv_w_p, conv_b_p, nb, t)
    dh2 = matmul("ffn_up_dx", du, ffn_up_p, "nt", F32, tm, D, 512)
    d_ffn_up_p = matmul("ffn_up_dw", h2, du, "tn", F32, D, 512, tm)
    (dx1,), (dg2,) = rowwise_bwd("norm2_bwd", f_norm2, [(x, D, 0), (mo, D, 0)], [g2n],
                                 [[(dx2, D, 0)], [(dh2, D, 0)]], tr, grad_rows=[1])
    dmerged = matmul("w_out_dx", dx1, w["w_out"], "nt", F32, tm, D, 512)
    d_w_out = matmul("w_out_dw", merged, dx1, "tn", F32, D, 512, tm)
    (dga, dgb, dya, dyb), _ = rowwise_bwd("merge_bwd", f_merge, merge_rows, [], [[(dmerged, D, 0)]], tr)
    d_oa = matmul("gla_proj_dx", dya, w["gla_proj"], "nt", F32, tm, 512, D)
    d_gla_proj = matmul("gla_proj_dw", oa, dya, "tn", F32, 512, 512, tm)
    d_ob = matmul("rwkv_proj_dx", dyb, w["rwkv_proj"], "nt", F32, tm, 512, D)
    d_rwkv_proj = matmul("rwkv_proj_dw", ob, dyb, "tn", F32, 512, 512, tm)
    (d_o, d_og, d_y, d_r_post, d_kp_post, d_v_post, d_g), d_post = rowwise_bwd(
        "branch_post_bwd", f_branch_post, post_rows, post_params, [[(d_oa, 512, 0)], [(d_ob, 512, 0)]], tr)
    late_grads = {"gla_proj": d_gla_proj, "rwkv_proj": d_rwkv_proj, "w_out": d_w_out, "ffn_up": ff_deinterleave(d_ffn_up_p),
                  "ffn_conv_w": ff_deinterleave(d_conv_w_p), "ffn_down": d_ffn_down_p[0:D_FF]}
    comm = [] if late_blocks is None else [(_full_to_slices(late_grads[k], SHARDED[k]), True) for k in LATE]
    dsc = rwkv_scan_bwd(*sc, hist_rw, sa_sc, to_scan("to_scan_dy", d_y, 0, nb, t, True), comm=comm)
    received = dict(zip(LATE, dsc[6:]))
    d_r_scan, d_kp_scan, d_a_scan, d_b_scan = from_scan("from_scan_rkab", dsc[0], dsc[1], nb, t, False)
    d_wf, d_wb = from_scan("from_scan_w", dsc[2], dsc[3], nb, t, False, groups=(0, 1))
    (d_v_scan,) = from_scan("from_scan_dv", dsc[4], dsc[5], nb, t, True)
    (d_k, d_wlal, d_gl), d_pre = rowwise_bwd(
        "rwkv_pre_bwd", f_rwkv_pre, pre_rows, pre_params,
        [[(d_wf, 512, 0)], [(d_wb, 512, 0)], [(d_kp_scan, 512, 0), (d_kp_post, 512, 0)],
         [(d_a_scan, 512, 0)], [(d_b_scan, 512, 0)], [(d_g, 512, 0)]], tr)
    ds = jnp.concatenate([d_r_scan + d_r_post, d_k, d_v_scan + d_v_post, d_wlal, d_gl], axis=1)
    dp_rw, d_mu_prev, d_mu_next = shift_bwd(p, ds, mu_prev, mu_next, nb, t)
    dqkv_f, d_wa2_f, d_ba_f = gla_bwd(p, wa2_f, ba_f, hist_f, d_o, None, nb, t, False)
    dqkv, d_wa2_b, d_ba_b = gla_bwd(p, wa2_b, ba_b, hist_b, d_o, dqkv_f, nb, t, True)
    dp = jnp.concatenate([dga, dgb, dqkv[:, 0:1024], d_og, dp_rw, dqkv[:, 1024:1152],
                          jnp.zeros((n, NP - C_AFAB - LANE), F32)], axis=1).astype(MXU_DTYPE)
    d_w_in_p = matmul("proj_in_dw", h1, dp, "tn", F32, D, 512, tm)
    grads = {
        "w_in": w_in_from_padded(d_w_in_p),
        "gla_wa2_f": d_wa2_f[0:GLA_RANK], "gla_ba_f": d_ba_f, "gla_wa2_b": d_wa2_b[GLA_RANK:2 * GLA_RANK], "gla_ba_b": d_ba_b,
        "gla_norm_g": d_post[0], "rwkv_mu_prev": d_mu_prev, "rwkv_mu_next": d_mu_next,
        "rwkv_w0_f": d_pre[0], "rwkv_w2_f": d_pre[1][0:64], "rwkv_w0_b": d_pre[2], "rwkv_w2_b": d_pre[3][0:64],
        "rwkv_a0": d_pre[4], "rwkv_a2": d_pre[5][64:128], "rwkv_g2": d_pre[6], "rwkv_k_k": d_pre[7], "rwkv_k_a": d_pre[8],
        "rwkv_r_k": d_post[3], "rwkv_ln_w": d_post[1], "rwkv_ln_b": d_post[2],
        "norm2_g": dg2, "ffn_conv_b": ff_deinterleave(d_conv_b_p), "norm_f_g": dgf, **late_grads,
    }
    early = [k for k in SHARDED if k not in LATE]
    comm = [] if late_blocks is None else [(_full_to_slices(grads[k], SHARDED[k]), True) for k in early]
    dh1, *got = matmul("proj_in_dx", dp, w_in_p, "nt", F32, tm, D, 512, comm=comm) if comm else \
        [matmul("proj_in_dx", dp, w_in_p, "nt", F32, tm, D, 512)]
    received.update(zip(early, got))
    (grad_x,), (grads["norm1_g"],) = rowwise_bwd("norm1_bwd", f_norm, [(x, D, 0)], [g1], [[(dh1, D, 0)]], tr,
                                                 adds=[(0, (dx1, D, 0))])
    return loss, grad_x, grads, received


MESH = pl.DeviceIdType.MESH


def remote_exchange(name, items):
    n = len(items)

    def body(*refs):
        start, wait = _exchange_plan([sc for _, sc in items], refs[:n], refs[n:2 * n], *refs[2 * n:])
        start()
        wait()

    args, specs, shapes, sems = _comm_specs(items)
    return pl.pallas_call(body, name=name, in_specs=specs, out_specs=specs, out_shape=shapes, scratch_shapes=sems)(*args)


def _exchange_plan(flags, in_refs, out_refs, send_sems, recv_sems, local_sems):
    x, y, c = lax.axis_index("x"), lax.axis_index("y"), lax.axis_index("c")
    me = 4 * x + 2 * y + c

    def peer(k):
        px = 1 - x if (k >> 2) & 1 else x
        py = 1 - y if (k >> 1) & 1 else y
        pc = 1 - c if k & 1 else c
        return (px, py, pc), 4 * px + 2 * py + pc

    def copies(with_arrivals):
        own, sends, recvs = [], [], []
        for i, scatter in enumerate(flags):
            src = in_refs[i].at[me] if scatter else in_refs[i]
            own.append(pltpu.make_async_copy(src, out_refs[i].at[me], local_sems.at[i]))
        for k in range(1, N_DEV):
            dev, slot = peer(k)
            for i, scatter in enumerate(flags):
                src = in_refs[i].at[slot] if scatter else in_refs[i]
                pair = dict(send_sem=send_sems.at[i, k - 1], recv_sem=recv_sems.at[i, k - 1], device_id=dev, device_id_type=MESH)
                sends.append(pltpu.make_async_remote_copy(src_ref=src, dst_ref=out_refs[i].at[me], **pair))
                if with_arrivals:
                    recvs.append(pltpu.make_async_remote_copy(src_ref=out_refs[i].at[slot], dst_ref=out_refs[i].at[slot], **pair))
        return own, sends, recvs

    def start():
        own, sends, _ = copies(False)
        for cp in own + sends:
            cp.start()

    def wait():
        own, sends, recvs = copies(True)
        for send, recv in zip(sends, recvs):
            recv.wait_recv()
            send.wait_send()
        for cp in own:
            cp.wait()

    return start, wait


def _adam_tiles(r, c):
    tc = 256 if (c % 256 == 0 and r * c > 128 * 1024) else c
    tr = 128 if (r % 128 == 0 and r > 128) else r
    return tr, tc


def adamw_reduce(name, parts, w, m, v):
    r, c = w.shape
    tr, tc = _adam_tiles(r, c)

    def body(p_ref, w_ref, m_ref, v_ref, g_ref, d_ref, nm_ref, nv_ref):
        g = p_ref[0]
        for d in range(1, N_DEV):
            g = g + p_ref[d]
        nm = ADAM_B1 * m_ref[...] + (1.0 - ADAM_B1) * g
        nv = ADAM_B2 * v_ref[...] + (1.0 - ADAM_B2) * (g * g)
        m_hat = nm / (1.0 - ADAM_B1 ** ADAM_STEP)
        v_hat = nv / (1.0 - ADAM_B2 ** ADAM_STEP)
        g_ref[...] = g
        d_ref[...] = -ADAM_LR * (m_hat / (jnp.sqrt(v_hat) + ADAM_EPS) + ADAM_WD * w_ref[...])
        nm_ref[...] = nm
        nv_ref[...] = nv

    spec = pl.BlockSpec((tr, tc), lambda i, j: (i, j))
    return pl.pallas_call(
        body, name=name, grid=(r // tr, c // tc),
        in_specs=[pl.BlockSpec((N_DEV, tr, tc), lambda i, j: (0, i, j)), spec, spec, spec],
        out_specs=[spec] * 4, out_shape=[jax.ShapeDtypeStruct((r, c), F32)] * 4,
        compiler_params=_cparams(("arbitrary", "arbitrary")),
    )(parts, w, m, v)


SHARDED = {"w_in": 1, "gla_wa2_f": 1, "gla_wa2_b": 1, "gla_proj": 1, "rwkv_w2_f": 1, "rwkv_w2_b": 1, "rwkv_a2": 1,
           "rwkv_g2": 1, "rwkv_proj": 1, "w_out": 0, "ffn_up": 1, "ffn_conv_w": 1, "ffn_down": 0}
BF16_GATHER = ("w_in", "gla_proj", "rwkv_proj", "w_out", "ffn_up", "ffn_down")
REPLICATED = ("norm1_g", "gla_ba_f", "gla_ba_b", "gla_norm_g", "rwkv_mu_prev", "rwkv_mu_next", "rwkv_w0_f", "rwkv_w0_b",
              "rwkv_a0", "rwkv_k_k", "rwkv_k_a", "rwkv_r_k", "rwkv_ln_w", "rwkv_ln_b", "norm2_g", "ffn_conv_b", "norm_f_g")
WEIGHTS = ("norm1_g", "w_in", "gla_wa2_f", "gla_ba_f", "gla_wa2_b", "gla_ba_b", "gla_norm_g", "gla_proj", "rwkv_mu_prev",
           "rwkv_mu_next", "rwkv_w0_f", "rwkv_w2_f", "rwkv_w0_b", "rwkv_w2_b", "rwkv_a0", "rwkv_a2", "rwkv_g2", "rwkv_k_k",
           "rwkv_k_a", "rwkv_r_k", "rwkv_ln_w", "rwkv_ln_b", "rwkv_proj", "w_out", "norm2_g", "ffn_up", "ffn_conv_w",
           "ffn_conv_b", "ffn_down", "norm_f_g")


def _gathered_to_full(g, axis):
    if axis == 0:
        return g.reshape(N_DEV * g.shape[1], g.shape[2])
    return g.transpose(1, 0, 2).reshape(g.shape[1], N_DEV * g.shape[2])


def _full_to_slices(a, axis):
    if axis == 0:
        return a.reshape(N_DEV, a.shape[0] // N_DEV, a.shape[1])
    return a.reshape(a.shape[0], N_DEV, a.shape[1] // N_DEV).transpose(1, 0, 2)


def _pack_rows(size):
    return -(-size // (8 * LANE)) * 8


def _pack(d):
    parts = []
    for k in REPLICATED:
        rows = d[k].reshape(-1, LANE).astype(F32)
        parts.append(jnp.pad(rows, ((0, _pack_rows(rows.size) - rows.shape[0]), (0, 0))))
    return jnp.concatenate(parts, axis=0)


def _unpack(packed, shapes):
    out, pos = {}, 0
    for k in REPLICATED:
        size = int(np.prod(shapes[k]))
        out[k] = packed[pos:pos + size // LANE].reshape(shapes[k])
        pos += _pack_rows(size)
    return out


def kernel(x, norm1_g, w_in, gla_wa2_f, gla_ba_f, gla_wa2_b, gla_ba_b, gla_norm_g, gla_proj, rwkv_mu_prev, rwkv_mu_next, rwkv_w0_f, rwkv_w2_f, rwkv_w0_b, rwkv_w2_b, rwkv_a0, rwkv_a2, rwkv_g2, rwkv_k_k, rwkv_k_a, rwkv_r_k, rwkv_ln_w, rwkv_ln_b, rwkv_proj, w_out, norm2_g, ffn_up, ffn_conv_w, ffn_conv_b, ffn_down, norm_f_g, loss_target, m_norm1_g, m_w_in, m_gla_wa2_f, m_gla_ba_f, m_gla_wa2_b, m_gla_ba_b, m_gla_norm_g, m_gla_proj, m_rwkv_mu_prev, m_rwkv_mu_next, m_rwkv_w0_f, m_rwkv_w2_f, m_rwkv_w0_b, m_rwkv_w2_b, m_rwkv_a0, m_rwkv_a2, m_rwkv_g2, m_rwkv_k_k, m_rwkv_k_a, m_rwkv_r_k, m_rwkv_ln_w, m_rwkv_ln_b, m_rwkv_proj, m_w_out, m_norm2_g, m_ffn_up, m_ffn_conv_w, m_ffn_conv_b, m_ffn_down, m_norm_f_g, v_norm1_g, v_w_in, v_gla_wa2_f, v_gla_ba_f, v_gla_wa2_b, v_gla_ba_b, v_gla_norm_g, v_gla_proj, v_rwkv_mu_prev, v_rwkv_mu_next, v_rwkv_w0_f, v_rwkv_w2_f, v_rwkv_w0_b, v_rwkv_w2_b, v_rwkv_a0, v_rwkv_a2, v_rwkv_g2, v_rwkv_k_k, v_rwkv_k_a, v_rwkv_r_k, v_rwkv_ln_w, v_rwkv_ln_b, v_rwkv_proj, v_w_out, v_norm2_g, v_ffn_up, v_ffn_conv_w, v_ffn_conv_b, v_ffn_down, v_norm_f_g):
    args = locals()
    wts = {k: args[k] for k in WEIGHTS}
    mom = {k: args["m_" + k] for k in WEIGHTS}
    var = {k: args["v_" + k] for k in WEIGHTS}
    shapes = {k: wts[k].shape for k in WEIGHTS}
    nb, t = x.shape[0], x.shape[1]
    mat = lambda a: a.reshape(a.shape[-2], a.shape[-1])

    block = lambda k: mat(wts[k]).astype(MXU_DTYPE) if k in BF16_GATHER else mat(wts[k])
    early = [k for k in SHARDED if k not in LATE]
    gathered = remote_exchange("gather_weights", [(block(k), False) for k in early])
    full = {k: _gathered_to_full(g, SHARDED[k]) for k, g in zip(early, gathered)}
    for k in REPLICATED:
        full[k] = wts[k].reshape(-1) if k in ("norm_f_g", "rwkv_r_k") else wts[k][0]

    loss, grad_x, grads, received = local_step(x.reshape(nb * t, D), loss_target.reshape(nb * t, D), full, nb, t,
                                               late_blocks={k: block(k) for k in LATE})

    (rep_parts,) = remote_exchange("exchange_replicated", [(_pack(grads), False)])

    res = {}
    for k in SHARDED:
        outs = adamw_reduce("adamw_" + k, received[k], mat(wts[k]), mat(mom[k]), mat(var[k]))
        res[k] = [o.reshape(shapes[k]) for o in outs]
    packed = adamw_reduce("adamw_replicated", rep_parts, _pack(wts), _pack(mom), _pack(var))
    unpacked = [_unpack(p, shapes) for p in packed]
    for k in REPLICATED:
        res[k] = [u[k] for u in unpacked]

    total = lax.psum(loss[0, 0], ("x", "y", "c"))
    out = [total, grad_x.reshape(x.shape)]
    for j in range(4):
        out += [res[k][j] for k in WEIGHTS]
    return tuple(out)
```

```python
import functools

import jax
import jax.numpy as jnp
import numpy as np
from jax import lax
from jax.experimental import pallas as pl
from jax.experimental.pallas import tpu as pltpu

F32 = jnp.float32
MXU_DTYPE = jnp.bfloat16

D = 1024
SEQ = 2048
GLA_H, GLA_DK, GLA_DV, GLA_CHUNK = 4, 64, 128, 64
GLA_RANK = 16
GLA_LOGIT_NORM = 16.0
RW_H, RW_N = 8, 64
RW_W = 512
D_FF = 2752
NORM_EPS = 1e-6
HEAD_NORM_EPS = 1e-5
RW_GN_EPS = RW_N * 1e-5
N_DEV = 8
ADAM_LR, ADAM_B1, ADAM_B2, ADAM_EPS, ADAM_WD, ADAM_STEP = 0.001, 0.9, 0.999, 1e-08, 0.01, 10

C_GA, C_GB, C_Q, C_K, C_V, C_OG = 0, 1024, 2048, 2304, 2560, 3072
C_RW = 3584
C_R, C_RK, C_RV, C_WLAL, C_GL = 3584, 4096, 4608, 5120, 5248
C_AFAB = 5376
NP = 5632
RW_PW = 1792
FFP = 2816
LANE = 128
VMEM_LIMIT = 56 * 1024 * 1024


def _cparams(sem):
    return pltpu.CompilerParams(dimension_semantics=sem, vmem_limit_bytes=VMEM_LIMIT)


@jax.custom_vjp
def mm(a, b):
    return jnp.dot(a.astype(MXU_DTYPE), b.astype(MXU_DTYPE), preferred_element_type=F32)


def _mm_fwd(a, b):
    return mm(a, b), (a, b)


def _mm_bwd(res, g):
    a, b = res
    gb = g.astype(MXU_DTYPE)
    da = lax.dot_general(gb, b.astype(MXU_DTYPE), (((1,), (1,)), ((), ())), preferred_element_type=F32)
    db = lax.dot_general(a.astype(MXU_DTYPE), gb, (((0,), (0,)), ((), ())), preferred_element_type=F32)
    return da.astype(a.dtype), db.astype(b.dtype)


mm.defvjp(_mm_fwd, _mm_bwd)


@jax.custom_vjp
def mm_nt(a, b):
    return lax.dot_general(a.astype(MXU_DTYPE), b.astype(MXU_DTYPE), (((1,), (1,)), ((), ())), preferred_element_type=F32)


def _mm_nt_fwd(a, b):
    return mm_nt(a, b), (a, b)


def _mm_nt_bwd(res, g):
    a, b = res
    gb = g.astype(MXU_DTYPE)
    da = jnp.dot(gb, b.astype(MXU_DTYPE), preferred_element_type=F32)
    db = lax.dot_general(gb, a.astype(MXU_DTYPE), (((0,), (0,)), ((), ())), preferred_element_type=F32)
    return da.astype(a.dtype), db.astype(b.dtype)


mm_nt.defvjp(_mm_nt_fwd, _mm_nt_bwd)


@jax.custom_vjp
def mm_tn(a, b):
    return lax.dot_general(a.astype(MXU_DTYPE), b.astype(MXU_DTYPE), (((0,), (0,)), ((), ())), preferred_element_type=F32)


def _mm_tn_fwd(a, b):
    return mm_tn(a, b), (a, b)


def _mm_tn_bwd(res, g):
    a, b = res
    gb = g.astype(MXU_DTYPE)
    da = lax.dot_general(b.astype(MXU_DTYPE), gb, (((1,), (1,)), ((), ())), preferred_element_type=F32)
    db = jnp.dot(a.astype(MXU_DTYPE), gb, preferred_element_type=F32)
    return da.astype(a.dtype), db.astype(b.dtype)


mm_tn.defvjp(_mm_tn_fwd, _mm_tn_bwd)


@functools.partial(jax.custom_vjp, nondiff_argnums=(2, 3))
def sel_dot(x, s, dims, x_first):
    sb = s.astype(MXU_DTYPE)
    hi = x.astype(MXU_DTYPE)
    r1 = x - hi.astype(F32)
    mid = r1.astype(MXU_DTYPE)
    lo = (r1 - mid.astype(F32)).astype(MXU_DTYPE)
    out = None
    for part in (hi, mid, lo):
        ops = (part, sb) if x_first else (sb, part)
        d = lax.dot_general(*ops, (dims, ((), ())), preferred_element_type=F32)
        out = d if out is None else out + d
    return out


def _sel_dot_fwd(x, s, dims, x_first):
    return sel_dot(x, s, dims, x_first), s


def _sel_dot_bwd(dims, x_first, s, g):
    if x_first:
        (cx,), (cs,) = dims
        dx = sel_dot(g, s, ((1,), (1 - cs,)), True) if cx == 1 else sel_dot(g, s, ((1 - cs,), (1,)), False)
    else:
        (cs,), (cx,) = dims
        dx = sel_dot(g, s, ((1 - cs,), (0,)), False) if cx == 0 else sel_dot(g, s, ((0,), (1 - cs,)), True)
    return dx, jnp.zeros_like(s)


sel_dot.defvjp(_sel_dot_fwd, _sel_dot_bwd)


def mm_exact(a, b, b_is_01=True):
    return sel_dot(a, b, ((1,), (0,)), True) if b_is_01 else sel_dot(b, a, ((1,), (0,)), False)


def mm_tn_exact(a, b):
    return sel_dot(a, b, ((0,), (0,)), True)


def _softplus(x):
    return jnp.maximum(x, 0.0) + jnp.log(1.0 + jnp.exp(-jnp.abs(x)))


def _sigmoid(x):
    return jax.nn.sigmoid(x)


def _silu(x):
    return x * _sigmoid(x)


def _rmsnorm(x, g):
    return x * lax.rsqrt(jnp.mean(x * x, axis=-1, keepdims=True) + NORM_EPS) * g


def _segment_sum(x, seg):
    width = x.shape[1]
    reduce = (lax.broadcasted_iota(jnp.int32, (width, LANE), 0) // seg) == lax.broadcasted_iota(jnp.int32, (width, LANE), 1)
    expand = lax.broadcasted_iota(jnp.int32, (LANE, width), 0) == (lax.broadcasted_iota(jnp.int32, (LANE, width), 1) // seg)
    return mm_exact(mm_exact(x, reduce.astype(F32)), expand.astype(F32))


def _row_spec(tm, width, cb):
    return pl.BlockSpec((tm, width), lambda i: (i, cb))


def _full_spec(shape):
    nd = len(shape)
    return pl.BlockSpec(tuple(shape), lambda i: (0,) * nd)


def rowwise_fwd(name, f, rows, params, outs, tm):
    n = rows[0][0].shape[0]
    nr, npar = len(rows), len(params)

    def body(*refs):
        rv = [r[...] for r in refs[:nr]]
        pv = [r[...] for r in refs[nr:nr + npar]]
        res = f(rv, pv)
        for o_ref, val in zip(refs[nr + npar:], res):
            o_ref[...] = val.astype(o_ref.dtype)

    return pl.pallas_call(
        body, name=name, grid=(n // tm,),
        in_specs=[_row_spec(tm, w, cb) for _, w, cb in rows] + [_full_spec(p.shape) for p in params],
        out_specs=[_row_spec(tm, w, 0) for w, _ in outs],
        out_shape=[jax.ShapeDtypeStruct((n, w), dt) for w, dt in outs],
        compiler_params=_cparams(("arbitrary",)),
    )(*[a for a, _, _ in rows], *params)


def rowwise_bwd(name, f, rows, params, douts, tm, adds=(), grad_rows=None):
    n = rows[0][0].shape[0]
    nr, npar = len(rows), len(params)
    grad_rows = list(range(nr)) if grad_rows is None else list(grad_rows)
    flat_d = [d for group in douts for d in group]
    nd, na, ng = len(flat_d), len(adds), len(grad_rows)

    def body(*refs):
        rv = [r[...] for r in refs[:nr]]
        pv = [r[...] for r in refs[nr:nr + npar]]
        dflat = [r[...].astype(F32) for r in refs[nr + npar:nr + npar + nd]]
        av = [r[...] for r in refs[nr + npar + nd:nr + npar + nd + na]]
        o = nr + npar + nd + na
        drow_refs, dpar_refs = refs[o:o + ng], refs[o + ng:o + ng + npar]
        dv, pos = [], 0
        for group in douts:
            dv.append(sum(dflat[pos + 1:pos + len(group)], dflat[pos]))
            pos += len(group)

        @pl.when(pl.program_id(0) == 0)
        def _():
            for r in dpar_refs:
                r[...] = jnp.zeros_like(r)

        def g(grows, pars):
            full = list(rv)
            for i, val in zip(grad_rows, grows):
                full[i] = val
            return f(full, pars)

        res, vjp = jax.vjp(g, [rv[i] for i in grad_rows], pv)
        drows, dpars = vjp([d.astype(r.dtype) for d, r in zip(dv, res)])
        drows = [d.astype(F32) for d in drows]
        for (idx, _), a in zip(adds, av):
            drows[idx] = drows[idx] + a.astype(F32)
        for r, d in zip(drow_refs, drows):
            r[...] = d
        for r, d in zip(dpar_refs, dpars):
            r[...] += d.astype(F32)

    res = pl.pallas_call(
        body, name=name, grid=(n // tm,),
        in_specs=[_row_spec(tm, w, cb) for _, w, cb in rows] + [_full_spec(p.shape) for p in params]
        + [_row_spec(tm, w, cb) for _, w, cb in flat_d] + [_row_spec(tm, w, cb) for _, (_, w, cb) in adds],
        out_specs=[_row_spec(tm, rows[i][1], 0) for i in grad_rows] + [_full_spec(p.shape) for p in params],
        out_shape=[jax.ShapeDtypeStruct((n, rows[i][1]), F32) for i in grad_rows]
        + [jax.ShapeDtypeStruct(p.shape, F32) for p in params],
        compiler_params=_cparams(("arbitrary",)),
    )(*[a for a, _, _ in rows], *params, *[a for a, _, _ in flat_d], *[a for _, (a, _, _) in adds])
    return res[:ng], res[ng:]


def matmul(name, a, b, mode, out_dtype, tm, tn, tk, comm=()):
    nc = len(comm)
    flags = [sc for _, sc in comm]
    if mode == "nn":
        (m, k), n = a.shape, b.shape[1]
        a_spec = pl.BlockSpec((tm, tk), lambda i, j, kk: (i, kk))
        b_spec = pl.BlockSpec((tk, tn), lambda i, j, kk: (kk, j))
        dims = (((1,), (0,)), ((), ()))
    elif mode == "nt":
        (m, k), n = a.shape, b.shape[0]
        a_spec = pl.BlockSpec((tm, tk), lambda i, j, kk: (i, kk))
        b_spec = pl.BlockSpec((tn, tk), lambda i, j, kk: (j, kk))
        dims = (((1,), (1,)), ((), ()))
    else:
        (k, m), n = a.shape, b.shape[1]
        a_spec = pl.BlockSpec((tk, tm), lambda i, j, kk: (kk, i))
        b_spec = pl.BlockSpec((tk, tn), lambda i, j, kk: (kk, j))
        dims = (((0,), (0,)), ((), ()))
    assert m % tm == 0 and n % tn == 0 and k % tk == 0, (name, a.shape, b.shape, tm, tn, tk)
    nk = k // tk
    grid = (m // tm, n // tn, nk)

    def body(*refs):
        a_ref, b_ref, c_in, o_ref = refs[0], refs[1], refs[2:2 + nc], refs[2 + nc]
        c_out, acc_ref, sems = refs[3 + nc:3 + 2 * nc], refs[3 + 2 * nc], refs[4 + 2 * nc:]
        kk = pl.program_id(2)
        step = (pl.program_id(0) * grid[1] + pl.program_id(1)) * nk + kk
        if nc:
            start, wait = _exchange_plan(flags, c_in, c_out, *sems)

            @pl.when(step == 0)
            def _():
                start()

        part = lax.dot_general(a_ref[...].astype(MXU_DTYPE), b_ref[...].astype(MXU_DTYPE), dims, preferred_element_type=F32)
        if nk == 1:
            o_ref[...] = part.astype(o_ref.dtype)
        else:
            @pl.when(kk == 0)
            def _():
                acc_ref[...] = part

            @pl.when((kk > 0) & (kk < nk - 1))
            def _():
                acc_ref[...] += part

            @pl.when(kk == nk - 1)
            def _():
                o_ref[...] = (acc_ref[...] + part).astype(o_ref.dtype)

        if nc:
            @pl.when(step == grid[0] * grid[1] * nk - 1)
            def _():
                wait()

    c_args, c_specs, c_shapes, c_sems = _comm_specs(comm)
    res = pl.pallas_call(
        body, name=name, grid=grid,
        in_specs=[a_spec, b_spec] + c_specs,
        out_specs=[pl.BlockSpec((tm, tn), lambda i, j, kk: (i, j))] + c_specs,
        out_shape=[jax.ShapeDtypeStruct((m, n), out_dtype)] + c_shapes,
        scratch_shapes=[pltpu.VMEM((tm, tn) if nk > 1 else (8, LANE), F32)] + c_sems,
        compiler_params=_cparams(("arbitrary", "arbitrary", "arbitrary")),
    )(a, b, *c_args)
    return res if nc else res[0]


def _prev(u, first):
    return jnp.where(first, 0.0, pltpu.roll(u, 1, 0))


def _next(u, last):
    return jnp.where(last, 0.0, pltpu.roll(u, u.shape[0] - 1, 0))


def _edge_masks(t, w):
    row = lax.broadcasted_iota(jnp.int32, (t, w), 0)
    return row == 0, row == t - 1


WIN, HALO = 64, 8
MID = slice(HALO, HALO + WIN)


def _window(ref, i, t):
    r0 = pl.multiple_of(i * WIN, WIN)
    before = ref[pl.ds(pl.multiple_of(jnp.maximum(r0 - HALO, 0), HALO), HALO), :]
    after = ref[pl.ds(pl.multiple_of(jnp.minimum(r0 + WIN, t - HALO), HALO), HALO), :]
    before = jnp.where(i == 0, 0.0, before.astype(F32))
    after = jnp.where(i == t // WIN - 1, 0.0, after.astype(F32))
    return jnp.concatenate([before, ref[pl.ds(r0, WIN), :].astype(F32), after], axis=0)


def _wprev(u):
    return pltpu.roll(u, 1, 0)


def _wnext(u):
    return pltpu.roll(u, u.shape[0] - 1, 0)


def _mid_rows(i):
    return pl.ds(pl.multiple_of(i * WIN, WIN), WIN)


def _colsum(x):
    return jnp.sum(x[MID], axis=0, keepdims=True)


SHIFT_CW = 256


def shift_fwd(p, mu_prev, mu_next, nb, t):
    cw, c0 = SHIFT_CW, C_RW // SHIFT_CW

    def body(p_ref, mp_ref, mn_ref, s_ref):
        x = p_ref[...]
        first, last = _edge_masks(t, cw)
        s_ref[...] = x + mp_ref[...] * (_prev(x, first) - x) + mn_ref[...] * (_next(x, last) - x)

    return pl.pallas_call(
        body, name="rwkv_shift_fwd", grid=(nb, RW_PW // cw),
        in_specs=[pl.BlockSpec((t, cw), lambda b, j: (b, c0 + j)), pl.BlockSpec((1, cw), lambda b, j: (0, j)),
                  pl.BlockSpec((1, cw), lambda b, j: (0, j))],
        out_specs=pl.BlockSpec((t, cw), lambda b, j: (b, j)),
        out_shape=jax.ShapeDtypeStruct((nb * t, RW_PW), F32),
        compiler_params=_cparams(("arbitrary", "arbitrary")),
    )(p, mu_prev, mu_next)


def shift_bwd(p, ds, mu_prev, mu_next, nb, t):
    cw, c0 = SHIFT_CW, C_RW // SHIFT_CW

    def body(p_ref, ds_ref, mp_ref, mn_ref, dp_ref, dmp_ref, dmn_ref):
        @pl.when(pl.program_id(1) == 0)
        def _():
            dmp_ref[...] = jnp.zeros_like(dmp_ref)
            dmn_ref[...] = jnp.zeros_like(dmn_ref)

        mp, mn = mp_ref[...], mn_ref[...]

        def step(i, carry):
            dmp, dmn = carry
            x, g = _window(p_ref, i, t), _window(ds_ref, i, t)
            dp = g * (1.0 - mp - mn) + _wnext(mp * g) + _wprev(mn * g)
            dp_ref[_mid_rows(i), :] = dp[MID]
            return dmp + _colsum(g * (_wprev(x) - x)), dmn + _colsum(g * (_wnext(x) - x))

        zero = jnp.zeros((1, cw), F32)
        dmp, dmn = lax.fori_loop(0, t // WIN, step, (zero, zero))
        dmp_ref[...] += dmp
        dmn_ref[...] += dmn

    return pl.pallas_call(
        body, name="rwkv_shift_bwd", grid=(RW_PW // cw, nb),
        in_specs=[pl.BlockSpec((t, cw), lambda j, b: (b, c0 + j)), pl.BlockSpec((t, cw), lambda j, b: (b, j)),
                  pl.BlockSpec((1, cw), lambda j, b: (0, j)), pl.BlockSpec((1, cw), lambda j, b: (0, j))],
        out_specs=[pl.BlockSpec((t, cw), lambda j, b: (b, j)), pl.BlockSpec((1, cw), lambda j, b: (0, j)),
                   pl.BlockSpec((1, cw), lambda j, b: (0, j))],
        out_shape=[jax.ShapeDtypeStruct((nb * t, RW_PW), F32), jax.ShapeDtypeStruct((1, RW_PW), F32),
                   jax.ShapeDtypeStruct((1, RW_PW), F32)],
        compiler_params=_cparams(("arbitrary", "arbitrary")),
    )(p, ds, mu_prev, mu_next)


def conv_glu_fwd(u, cw, cb, nb, t):
    def body(u_ref, w_ref, b_ref, z_ref):
        x, w = u_ref[...], w_ref[...]
        first, last = _edge_masks(t, 2 * LANE)
        c = w[0:1] * _prev(x, first) + w[1:2] * x + w[2:3] * _next(x, last) + b_ref[...]
        z_ref[...] = (_silu(c[:, :LANE]) * c[:, LANE:]).astype(z_ref.dtype)

    return pl.pallas_call(
        body, name="conv_glu_fwd", grid=(nb, FFP // LANE),
        in_specs=[pl.BlockSpec((t, 2 * LANE), lambda b, j: (b, j)), pl.BlockSpec((3, 2 * LANE), lambda b, j: (0, j)),
                  pl.BlockSpec((1, 2 * LANE), lambda b, j: (0, j))],
        out_specs=pl.BlockSpec((t, LANE), lambda b, j: (b, j)),
        out_shape=jax.ShapeDtypeStruct((nb * t, FFP), MXU_DTYPE),
        compiler_params=_cparams(("arbitrary", "arbitrary")),
    )(u, cw, cb)


def conv_glu_bwd(u, dz, cw, cb, nb, t):
    def body(u_ref, dz_ref, w_ref, b_ref, du_ref, dw_ref, db_ref):
        @pl.when(pl.program_id(1) == 0)
        def _():
            dw_ref[...] = jnp.zeros_like(dw_ref)
            db_ref[...] = jnp.zeros_like(db_ref)

        w, bias = w_ref[...], b_ref[...]

        def step(i, carry):
            x, g = _window(u_ref, i, t), _window(dz_ref, i, t)
            xp, xn = _wprev(x), _wnext(x)
            c = w[0:1] * xp + w[1:2] * x + w[2:3] * xn + bias
            cg, cv = c[:, :LANE], c[:, LANE:]
            sg = _sigmoid(cg)
            dcg = g * cv * (sg * (1.0 + cg * (1.0 - sg)))
            dcv = g * (cg * sg)
            dc = jnp.concatenate([dcg, dcv], axis=1)
            du = w[1:2] * dc + _wnext(w[0:1] * dc) + _wprev(w[2:3] * dc)
            du_ref[_mid_rows(i), :] = du[MID].astype(du_ref.dtype)
            return tuple(acc + _colsum(val) for acc, val in zip(carry, (dc * xp, dc * x, dc * xn, dc)))

        zero = jnp.zeros((1, 2 * LANE), F32)
        sums = lax.fori_loop(0, t // WIN, step, (zero, zero, zero, zero))
        for row in range(3):
            dw_ref[row:row + 1, :] += sums[row]
        db_ref[...] += sums[3]

    return pl.pallas_call(
        body, name="conv_glu_bwd", grid=(FFP // LANE, nb),
        in_specs=[pl.BlockSpec((t, 2 * LANE), lambda j, b: (b, j)), pl.BlockSpec((t, LANE), lambda j, b: (b, j)),
                  pl.BlockSpec((3, 2 * LANE), lambda j, b: (0, j)), pl.BlockSpec((1, 2 * LANE), lambda j, b: (0, j))],
        out_specs=[pl.BlockSpec((t, 2 * LANE), lambda j, b: (b, j)), pl.BlockSpec((3, 2 * LANE), lambda j, b: (0, j)),
                   pl.BlockSpec((1, 2 * LANE), lambda j, b: (0, j))],
        out_shape=[jax.ShapeDtypeStruct((nb * t, 2 * FFP), MXU_DTYPE), jax.ShapeDtypeStruct((3, 2 * FFP), F32),
                   jax.ShapeDtypeStruct((1, 2 * FFP), F32)],
        compiler_params=_cparams(("arbitrary", "arbitrary")),
    )(u, dz, cw, cb)


def _gla_chunk(q, k, v, afab, wa2p, ba, s_in, reverse, sb):
    c = GLA_CHUNK
    r = sb * c
    ri = lax.broadcasted_iota(jnp.int32, (r, r), 0)
    ci = lax.broadcasted_iota(jnp.int32, (r, r), 1)
    same = (ri // c) == (ci // c)
    keep = same & ((ci >= ri) if reverse else (ci <= ri))
    i_ref = (c - 1 - c // 2) if reverse else (c // 2)
    pick_ref = (ci == (ri // c) * c + i_ref).astype(F32)
    seq_cols = (lax.broadcasted_iota(jnp.int32, (r, sb * LANE), 0) // c) == (lax.broadcasted_iota(jnp.int32, (r, sb * LANE), 1) // LANE)
    expand = lambda x: jnp.where(seq_cols, jnp.concatenate([x] * sb, axis=1), 0.0)
    lane = lax.broadcasted_iota(jnp.int32, (1, LANE), 1)
    outs, states = [None] * GLA_H, [None] * GLA_H
    for pr in range(GLA_H // 2):
        la = -_softplus(-(mm(afab, wa2p[pr]) + ba[pr])) * (1.0 / GLA_LOGIT_NORM)
        b = mm_exact(keep.astype(F32), la, b_is_01=False)
        b_ref = mm_exact(pick_ref, b, b_is_01=False)
        b_last = mm_exact(same.astype(F32), la, b_is_01=False)
        qs = q[pr] * (GLA_DK ** -0.5)
        qi = qs * jnp.exp(b - b_ref)
        ki = k[pr] * jnp.exp(b_ref - b)
        kd = k[pr] * jnp.exp(b_last - b)
        qb = qs * jnp.exp(b)
        dec = jnp.exp(mm_tn_exact(expand(la), jnp.ones((r, LANE), F32)))
        for h in (2 * pr, 2 * pr + 1):
            m = ((lane // GLA_DK) == (h % 2)).astype(F32)
            a = jnp.where(keep, mm_nt(qi * m, ki), 0.0)
            o_intra = mm(a, v[h])
            kv = mm_tn(expand(kd * m), v[h])
            o_inter = mm(expand(qb * m), s_in[h])
            outs[h] = o_intra + o_inter
            states[h] = s_in[h] * dec + kv
    return outs, states


def _gla_load(q_ref, k_ref, v_ref, af_ref, w_ref, ba_ref, sb, rows):
    stack = lambda ref, c0: jnp.concatenate([ref[s, rows, c0:c0 + LANE] for s in range(sb)], axis=0)
    q = [stack(q_ref, pr * LANE) for pr in range(GLA_H // 2)]
    k = [stack(k_ref, pr * LANE) for pr in range(GLA_H // 2)]
    v = [stack(v_ref, h * GLA_DV) for h in range(GLA_H)]
    w = [w_ref[:, pr * LANE:(pr + 1) * LANE] for pr in range(GLA_H // 2)]
    ba = [ba_ref[:, pr * LANE:(pr + 1) * LANE] for pr in range(GLA_H // 2)]
    return q, k, v, stack(af_ref, 0), w, ba


GLA_TILE = 256
GLA_SB = 4


def _gla_specs(nb, t, reverse):
    tile = min(GLA_TILE, t)
    nt = t // tile
    sb = GLA_SB if nb % GLA_SB == 0 else 1
    return tile, tile // GLA_CHUNK, nt, sb, ((lambda j: nt - 1 - j) if reverse else (lambda j: j))


def gla_fwd(p, wa2p, ba, o_add, nb, t, reverse):
    tile, cpt, nt, sb, tj = _gla_specs(nb, t, reverse)
    has_add = o_add is not None

    def body(*refs):
        if has_add:
            q_ref, k_ref, v_ref, af_ref, w_ref, ba_ref, add_ref, o_ref, hist_ref, s_ref = refs
        else:
            q_ref, k_ref, v_ref, af_ref, w_ref, ba_ref, o_ref, hist_ref, s_ref = refs

        @pl.when(pl.program_id(1) == 0)
        def _():
            s_ref[...] = jnp.zeros_like(s_ref)

        def step(i, carry):
            ci = (cpt - 1 - i) if reverse else i
            rows = pl.ds(pl.multiple_of(ci * GLA_CHUNK, GLA_CHUNK), GLA_CHUNK)
            s_in = [s_ref[h] for h in range(GLA_H)]
            for h in range(GLA_H):
                for s in range(sb):
                    hist_ref[s, ci, h] = s_in[h][s * LANE:(s + 1) * LANE]
            q, k, v, af, w, ba = _gla_load(q_ref, k_ref, v_ref, af_ref, w_ref, ba_ref, sb, rows)
            outs, states = _gla_chunk(q, k, v, af, w, ba, s_in, reverse, sb)
            for h in range(GLA_H):
                for s in range(sb):
                    oh = outs[h][s * GLA_CHUNK:(s + 1) * GLA_CHUNK]
                    if has_add:
                        oh = oh + add_ref[s, rows, h * GLA_DV:(h + 1) * GLA_DV]
                    o_ref[s, rows, h * GLA_DV:(h + 1) * GLA_DV] = oh
                s_ref[h] = states[h]
            return carry

        lax.fori_loop(0, cpt, step, 0)

    col = lambda width, c0: pl.BlockSpec((sb, tile, width), lambda b, j: (b, tj(j), c0 // width))
    in_specs = [col(256, C_Q), col(256, C_K), col(512, C_V), col(LANE, C_AFAB),
                pl.BlockSpec((LANE, 256), lambda b, j: (0, 0)), pl.BlockSpec((1, 256), lambda b, j: (0, 0))]
    p3 = p.reshape(nb, t, p.shape[1])
    args = [p3, p3, p3, p3, wa2p, ba]
    if has_add:
        in_specs.append(col(512, 0))
        args.append(o_add.reshape(nb, t, 512))
    o, hist = pl.pallas_call(
        body, name="gla_fwd_rev" if reverse else "gla_fwd", grid=(nb // sb, nt),
        in_specs=in_specs,
        out_specs=[col(512, 0), pl.BlockSpec((sb, cpt, GLA_H, LANE, LANE), lambda b, j: (b, tj(j), 0, 0, 0))],
        out_shape=[jax.ShapeDtypeStruct((nb, t, 512), F32),
                   jax.ShapeDtypeStruct((nb, t // GLA_CHUNK, GLA_H, LANE, LANE), F32)],
        scratch_shapes=[pltpu.VMEM((GLA_H, sb * LANE, LANE), F32)],
        compiler_params=_cparams(("arbitrary", "arbitrary")),
    )(*args)
    return o.reshape(nb * t, 512), hist


def gla_bwd(p, wa2p, ba, hist, do, dprev, nb, t, reverse):
    tile, cpt, nt, sb, tj_f = _gla_specs(nb, t, reverse)
    tj = lambda j: tj_f(nt - 1 - j)
    has_prev = dprev is not None

    def body(*refs):
        if has_prev:
            q_ref, k_ref, v_ref, af_ref, w_ref, ba_ref, hist_ref, do_ref, prev_ref, dqkv_ref, dw_ref, dba_ref, ds_ref = refs
        else:
            q_ref, k_ref, v_ref, af_ref, w_ref, ba_ref, hist_ref, do_ref, dqkv_ref, dw_ref, dba_ref, ds_ref = refs

        @pl.when((pl.program_id(0) == 0) & (pl.program_id(1) == 0))
        def _():
            dw_ref[...] = jnp.zeros_like(dw_ref)
            dba_ref[...] = jnp.zeros_like(dba_ref)

        @pl.when(pl.program_id(1) == 0)
        def _():
            ds_ref[...] = jnp.zeros_like(ds_ref)

        def step(i, carry):
            ci = i if reverse else (cpt - 1 - i)
            rows = pl.ds(pl.multiple_of(ci * GLA_CHUNK, GLA_CHUNK), GLA_CHUNK)
            fn = functools.partial(_gla_chunk, reverse=reverse, sb=sb)
            seqs = lambda get: jnp.concatenate([get(s) for s in range(sb)], axis=0)
            s_in = [seqs(lambda s: hist_ref[s, ci, h]) for h in range(GLA_H)]
            q, k, v, af, w, ba = _gla_load(q_ref, k_ref, v_ref, af_ref, w_ref, ba_ref, sb, rows)
            _, vjp = jax.vjp(fn, q, k, v, af, w, ba, s_in)
            d_o = [seqs(lambda s: do_ref[s, rows, h * GLA_DV:(h + 1) * GLA_DV]) for h in range(GLA_H)]
            d_s = [ds_ref[h] for h in range(GLA_H)]
            dq, dk, dv, daf, dw, dba, ds_in = vjp((d_o, d_s))
            pieces = [(pr * LANE, dq[pr]) for pr in range(2)] + [(256 + pr * LANE, dk[pr]) for pr in range(2)]
            pieces += [(512 + h * GLA_DV, dv[h]) for h in range(GLA_H)] + [(1024, daf)]
            for c0, val in pieces:
                for s in range(sb):
                    part = val[s * GLA_CHUNK:(s + 1) * GLA_CHUNK]
                    if has_prev:
                        part = part + prev_ref[s, rows, c0:c0 + LANE]
                    dqkv_ref[s, rows, c0:c0 + LANE] = part
            for pr in range(2):
                dw_ref[:, pr * LANE:(pr + 1) * LANE] += dw[pr]
                dba_ref[:, pr * LANE:(pr + 1) * LANE] += dba[pr]
            for h in range(GLA_H):
                ds_ref[h] = ds_in[h]
            return carry

        lax.fori_loop(0, cpt, step, 0)

    col = lambda width, c0: pl.BlockSpec((sb, tile, width), lambda b, j: (b, tj(j), c0 // width))
    in_specs = [col(256, C_Q), col(256, C_K), col(512, C_V), col(LANE, C_AFAB),
                pl.BlockSpec((LANE, 256), lambda b, j: (0, 0)), pl.BlockSpec((1, 256), lambda b, j: (0, 0)),
                pl.BlockSpec((sb, cpt, GLA_H, LANE, LANE), lambda b, j: (b, tj(j), 0, 0, 0)), col(512, 0)]
    p3 = p.reshape(nb, t, p.shape[1])
    args = [p3, p3, p3, p3, wa2p, ba, hist, do.reshape(nb, t, 512)]
    if has_prev:
        in_specs.append(col(1152, 0))
        args.append(dprev.reshape(nb, t, 1152))
    dqkv, dw, dba = pl.pallas_call(
        body, name="gla_bwd_rev" if reverse else "gla_bwd", grid=(nb // sb, nt),
        in_specs=in_specs,
        out_specs=[col(1152, 0), pl.BlockSpec((LANE, 256), lambda b, j: (0, 0)), pl.BlockSpec((1, 256), lambda b, j: (0, 0))],
        out_shape=[jax.ShapeDtypeStruct((nb, t, 1152), F32), jax.ShapeDtypeStruct((LANE, 256), F32),
                   jax.ShapeDtypeStruct((1, 256), F32)],
        scratch_shapes=[pltpu.VMEM((GLA_H, sb * LANE, LANE), F32)],
        compiler_params=_cparams(("arbitrary", "arbitrary")),
    )(*args)
    return dqkv.reshape(nb * t, 1152), dw, dba


SCAN_TB = 8
RW_VH = RW_N // 2


def _bwd_lanes():
    lane = lax.broadcasted_iota(jnp.int32, (1, LANE), 1)
    return ((lane // (LANE // 4)) % 2) == 1


def _comm_specs(comm):
    anyspec = pl.BlockSpec(memory_space=pl.ANY)
    n = len(comm)
    shapes = [jax.ShapeDtypeStruct((N_DEV,) + (a.shape[1:] if sc else a.shape), a.dtype) for a, sc in comm]
    sems = [pltpu.SemaphoreType.DMA((n, N_DEV - 1)), pltpu.SemaphoreType.DMA((n, N_DEV - 1)), pltpu.SemaphoreType.DMA((n,))] if n else []
    return [a for a, _ in comm], [anyspec] * n, shapes, sems


def rwkv_scan_fwd(r, w, k, a, b, v, comm=()):
    t = r.shape[0]
    nt = t // SCAN_TB
    nc = len(comm)
    flags = [sc for _, sc in comm]

    def body(*refs):
        (rf, rm, kf, km, af, am, bf, bm, wf_ref, wm_ref, vf, vm), refs = refs[:12], refs[12:]
        c_in, refs = refs[:nc], refs[nc:]
        (yf_ref, ym_ref, hist_ref, sa_ref), refs = refs[:4], refs[4:]
        c_out, refs = refs[:nc], refs[nc:]
        s_ref, sems = refs[0], refs[1:]
        i = pl.program_id(0)
        if nc:
            start, wait = _exchange_plan(flags, c_in, c_out, *sems)

        @pl.when(i == 0)
        def _():
            s_ref[...] = jnp.zeros_like(s_ref)
            if nc:
                start()

        bwd = _bwd_lanes()

        for tt in range(SCAN_TB):
            mt = SCAN_TB - 1 - tt
            pick = lambda f_ref, m_ref: jnp.where(bwd, m_ref[mt], f_ref[tt])
            rt, kt, at, bt, wt = pick(rf, rm), pick(kf, km), pick(af, am), pick(bf, bm), pick(wf_ref, wm_ref)
            for vi in range(RW_VH):
                sv = s_ref[vi] if tt == 0 else hist_ref[tt - 1, vi]
                sa = jnp.sum(sv * at, axis=0, keepdims=True)
                v_row = jnp.where(bwd, vm[mt, vi:vi + 1, :], vf[tt, vi:vi + 1, :])
                sn = sv * wt + sa * bt + v_row * kt
                hist_ref[tt, vi] = sn
                y_row = jnp.sum(sn * rt, axis=0, keepdims=True)
                yf_ref[tt, vi:vi + 1, :] = y_row
                ym_ref[mt, vi:vi + 1, :] = y_row
                sa_ref[tt, vi:vi + 1, :] = sa
        s_ref[...] = hist_ref[SCAN_TB - 1]

        if nc:
            @pl.when(i == nt - 1)
            def _():
                wait()

    fwd_map, mir_map = (lambda i: (i, 0, 0)), (lambda i: (nt - 1 - i, 0, 0))
    kf_spec, km_spec = pl.BlockSpec((SCAN_TB, RW_N, LANE), fwd_map), pl.BlockSpec((SCAN_TB, RW_N, LANE), mir_map)
    vf_spec, vm_spec = pl.BlockSpec((SCAN_TB, RW_VH, LANE), fwd_map), pl.BlockSpec((SCAN_TB, RW_VH, LANE), mir_map)
    c_args, c_specs, c_shapes, c_sems = _comm_specs(comm)
    vshape = jax.ShapeDtypeStruct((t, RW_VH, LANE), F32)
    return pl.pallas_call(
        body, name="rwkv_scan_fwd", grid=(nt,),
        in_specs=[kf_spec, km_spec] * 5 + [vf_spec, vm_spec] + c_specs,
        out_specs=[vf_spec, vm_spec, pl.BlockSpec((SCAN_TB, RW_VH, RW_N, LANE), lambda i: (i, 0, 0, 0)), vf_spec] + c_specs,
        out_shape=[vshape, vshape, jax.ShapeDtypeStruct((t, RW_VH, RW_N, LANE), F32), vshape] + c_shapes,
        scratch_shapes=[pltpu.VMEM((RW_VH, RW_N, LANE), F32)] + c_sems,
        compiler_params=_cparams(("arbitrary",)),
    )(r, r, k, k, a, a, b, b, w, w, v, v, *c_args)


def rwkv_scan_bwd(r, w, k, a, b, v, hist, sa, dy, comm=()):
    t = r.shape[0]
    nt = t // SCAN_TB
    nc = len(comm)
    flags = [sc for _, sc in comm]

    def body(*refs):
        (rf, rm, kf, km, af, am, bf, bm, wf_ref, wm_ref, vf, vm, hist_ref, prev_ref, sa_ref, dyf, dym), refs = refs[:17], refs[17:]
        c_in, refs = refs[:nc], refs[nc:]
        k_outs, (dvf_ref, dvm_ref), refs = refs[:4], refs[4:6], refs[6:]
        c_out, refs = refs[:nc], refs[nc:]
        ds_ref, sems = refs[0], refs[1:]
        i = pl.program_id(0)
        if nc:
            start, wait = _exchange_plan(flags, c_in, c_out, *sems)

        @pl.when(i == 0)
        def _():
            ds_ref[...] = jnp.zeros_like(ds_ref)
            if nc:
                start()

        bwd = _bwd_lanes()
        group = lax.broadcasted_iota(jnp.int32, (1, LANE), 1) // RW_Q
        first_block = i == nt - 1

        for tt in range(SCAN_TB - 1, -1, -1):
            mt = SCAN_TB - 1 - tt
            pick = lambda f_ref, m_ref: jnp.where(bwd, m_ref[mt], f_ref[tt])
            rt, kt, at, bt, wt = pick(rf, rm), pick(kf, km), pick(af, am), pick(bf, bm), pick(wf_ref, wm_ref)
            zero = jnp.zeros((RW_N, LANE), F32)
            dr, dw, dk, da, db = zero, zero, zero, zero, zero
            for vi in range(RW_VH):
                sn = hist_ref[tt, vi]
                sv = hist_ref[tt - 1, vi] if tt > 0 else jnp.where(first_block, 0.0, prev_ref[0, vi])
                sa_row = sa_ref[tt, vi:vi + 1, :]
                v_row = jnp.where(bwd, vm[mt, vi:vi + 1, :], vf[tt, vi:vi + 1, :])
                dy_row = jnp.where(bwd, dym[mt, vi:vi + 1, :], dyf[tt, vi:vi + 1, :])
                dsv = ds_ref[vi] + dy_row * rt
                dr = dr + sn * dy_row
                dsa = jnp.sum(dsv * bt, axis=0, keepdims=True)
                dw = dw + sv * dsv
                db = db + dsv * sa_row
                dk = dk + dsv * v_row
                dv_row = jnp.sum(dsv * kt, axis=0, keepdims=True)
                dvf_ref[tt, vi:vi + 1, :] = dv_row
                dvm_ref[mt, vi:vi + 1, :] = dv_row
                da = da + sv * dsa
                ds_ref[vi] = dsv * wt + dsa * at
            dr, dw, dk, da, db = [val + pltpu.roll(val, LANE // 2, 1) for val in (dr, dw, dk, da, db)]
            up, down = (lambda val: pltpu.roll(val, RW_Q, 1)), (lambda val: pltpu.roll(val, LANE - RW_Q, 1))
            packed_f = jnp.where(group == 0, dr, jnp.where(group == 1, up(dk), jnp.where(group == 2, da, up(db))))
            packed_m = jnp.where(group == 0, down(dr), jnp.where(group == 1, dk, jnp.where(group == 2, down(da), db)))
            k_outs[0][tt] = packed_f
            k_outs[1][mt] = packed_m
            k_outs[2][tt] = dw
            k_outs[3][mt] = dw

        if nc:
            @pl.when(i == nt - 1)
            def _():
                wait()

    fwd_map, mir_map = (lambda i: (nt - 1 - i, 0, 0)), (lambda i: (i, 0, 0))
    kf_spec, km_spec = pl.BlockSpec((SCAN_TB, RW_N, LANE), fwd_map), pl.BlockSpec((SCAN_TB, RW_N, LANE), mir_map)
    vf_spec, vm_spec = pl.BlockSpec((SCAN_TB, RW_VH, LANE), fwd_map), pl.BlockSpec((SCAN_TB, RW_VH, LANE), mir_map)
    prev_spec = pl.BlockSpec((1, RW_VH, RW_N, LANE), lambda i: (jnp.maximum((nt - 1 - i) * SCAN_TB - 1, 0), 0, 0, 0))
    c_args, c_specs, c_shapes, c_sems = _comm_specs(comm)
    kshape, vshape = jax.ShapeDtypeStruct((t, RW_N, LANE), F32), jax.ShapeDtypeStruct((t, RW_VH, LANE), F32)
    return pl.pallas_call(
        body, name="rwkv_scan_bwd", grid=(nt,),
        in_specs=[kf_spec, km_spec] * 5 + [vf_spec, vm_spec,
                                           pl.BlockSpec((SCAN_TB, RW_VH, RW_N, LANE), lambda i: (nt - 1 - i, 0, 0, 0)),
                                           prev_spec, vf_spec, vf_spec, vm_spec] + c_specs,
        out_specs=[kf_spec, km_spec] * 2 + [vf_spec, vm_spec] + c_specs,
        out_shape=[kshape] * 4 + [vshape] * 2 + c_shapes,
        scratch_shapes=[pltpu.VMEM((RW_VH, RW_N, LANE), F32)] + c_sems,
        compiler_params=_cparams(("arbitrary",)),
    )(r, r, k, k, a, a, b, b, w, w, v, v, hist, hist, sa, dy, dy, *c_args)


RELAYOUT_TB = 128
RW_Q = LANE // 4


def to_scan(name, x, cb, nb, t, value, x_bwd=None):
    tb = min(RELAYOUT_TB, t)
    rows_out = RW_VH if value else RW_N
    ins = [x] if x_bwd is None else [x, x_bwd]

    def body(*refs):
        x_refs, o_ref, scrs = refs[:len(ins)], refs[len(ins)], refs[len(ins) + 1:]
        for x_ref, scr in zip(x_refs, scrs):
            for b in range(nb):
                scr[b * RW_H:(b + 1) * RW_H] = x_ref[b].T.reshape(RW_H, RW_N, tb)
        for j in range(rows_out):
            lo = scrs[0][:, j, :]
            if value:
                hi = scrs[0][:, j + RW_VH, :]
                blk = [lo, lo, hi, hi]
            else:
                other = lo if x_bwd is None else scrs[1][:, j, :]
                blk = [lo, other, lo, other]
            o_ref[:, j, :] = jnp.concatenate(blk, axis=0).T

    return pl.pallas_call(
        body, name=name, grid=(t // tb,),
        in_specs=[pl.BlockSpec((nb, tb, RW_W), lambda i: (0, i, cb))] + [pl.BlockSpec((nb, tb, RW_W), lambda i: (0, i, 0))] * (len(ins) - 1),
        out_specs=pl.BlockSpec((tb, rows_out, LANE), lambda i: (i, 0, 0)),
        out_shape=jax.ShapeDtypeStruct((t, rows_out, LANE), F32),
        scratch_shapes=[pltpu.VMEM((nb * RW_H, RW_N, tb), F32)] * len(ins),
        compiler_params=_cparams(("arbitrary",)),
    )(*[a.reshape(nb, t, a.shape[1]) for a in ins])


def from_scan(name, xf, xm, nb, t, value, groups=None):
    tb = min(RELAYOUT_TB, t)
    rows_in = RW_VH if value else RW_N
    n_out = 1 if value else (4 if groups is None else 2)
    grp = lambda a, g: a[g * RW_Q:(g + 1) * RW_Q]

    def body(f_ref, m_ref, *rest):
        outs, scrs = rest[:n_out], rest[n_out:]
        lane_group = lax.broadcasted_iota(jnp.int32, (1, LANE), 1) // RW_Q
        for j in range(rows_in):
            f, m = f_ref[:, j, :], m_ref[:, j, :]
            if value:
                c = jnp.where(_bwd_lanes(), m, f).T
                scrs[0][:, j, :] = grp(c, 0) + grp(c, 1)
                scrs[0][:, j + RW_VH, :] = grp(c, 2) + grp(c, 3)
            elif groups is None:
                c = (f + m).T
                for q, scr in enumerate(scrs):
                    scr[:, j, :] = grp(c, q)
            else:
                c = jnp.where(lane_group == groups[1], m, f).T
                scrs[0][:, j, :] = grp(c, groups[0])
                scrs[1][:, j, :] = grp(c, groups[1])
        for o_ref, scr in zip(outs, scrs):
            for b in range(nb):
                o_ref[b] = scr[b * RW_H:(b + 1) * RW_H].reshape(RW_W, tb).T

    res = pl.pallas_call(
        body, name=name, grid=(t // tb,),
        in_specs=[pl.BlockSpec((tb, rows_in, LANE), lambda i: (i, 0, 0))] * 2,
        out_specs=[pl.BlockSpec((nb, tb, RW_W), lambda i: (0, i, 0))] * n_out,
        out_shape=[jax.ShapeDtypeStruct((nb, t, RW_W), F32)] * n_out,
        scratch_shapes=[pltpu.VMEM((nb * RW_H, RW_N, tb), F32)] * n_out,
        compiler_params=_cparams(("arbitrary",)),
    )(xf, xm)
    return [r.reshape(nb * t, RW_W) for r in res]


def f_norm(rows, params):
    (x,), (g,) = rows, params
    return [_rmsnorm(x, g)]


def f_rwkv_pre(rows, params):
    k, wlal, gl = rows
    w0f, w2f, w0b, w2b, a0, a2, g2, k_k, k_a = params
    tw = jnp.tanh(wlal)

    def decay(w0, w2):
        return jnp.exp(-jnp.exp(-_softplus(-(w0 + mm(tw, w2))) - 0.5))

    lr = _sigmoid(a0 + mm(wlal, a2))
    gate = mm(_sigmoid(gl), g2)
    kk = k * k_k
    kk = kk / jnp.maximum(jnp.sqrt(_segment_sum(kk * kk, RW_N)), 1e-12)
    kp = k * (1.0 + (lr - 1.0) * k_a)
    return [decay(w0f, w2f), decay(w0b, w2b), kp, -kk, kk * lr, gate]


def f_branch_post(rows, params):
    o, og, y, r, kp, v, g = rows
    gla_g, ln_w, ln_b, r_k = params
    on = o * lax.rsqrt(_segment_sum(o * o, GLA_DV) * (1.0 / GLA_DV) + HEAD_NORM_EPS)
    oa = on * gla_g * _silu(og)
    mu = _segment_sum(y, RW_N) * (1.0 / RW_N)
    yc = y - mu
    var = _segment_sum(yc * yc, RW_N) * (1.0 / RW_N)
    yn = yc * lax.rsqrt(var + RW_GN_EPS) * ln_w + ln_b
    bonus = _segment_sum(r * kp * r_k, RW_N) * v
    return [oa, (yn + bonus) * g]


def f_merge(rows, params):
    ga, gb, ya, yb = rows
    return [_sigmoid(ga) * ya + _sigmoid(gb) * yb]


def f_norm2(rows, params):
    (x, mo), (g,) = rows, params
    x1 = x + mo
    return [x1, _rmsnorm(x1, g)]


def loss_head(x1, ffo, tgt, gf, tm):
    n = x1.shape[0]

    def body(x1_ref, f_ref, t_ref, g_ref, loss_ref, dx_ref, dg_ref):
        @pl.when(pl.program_id(0) == 0)
        def _():
            loss_ref[...] = jnp.zeros_like(loss_ref)
            dg_ref[...] = jnp.zeros_like(dg_ref)

        tgt_v = t_ref[...]

        def f(x2, g):
            err = _rmsnorm(x2, g) - tgt_v
            return jnp.sum(jnp.sum(err * err, axis=-1, keepdims=True), axis=0, keepdims=True) * (0.5 / D)

        val, vjp = jax.vjp(f, x1_ref[...] + f_ref[...], g_ref[...])
        dx, dg = vjp(jnp.ones((1, 1), F32))
        loss_ref[...] += val
        dx_ref[...] = dx
        dg_ref[...] += dg

    return pl.pallas_call(
        body, name="loss_head", grid=(n // tm,),
        in_specs=[_row_spec(tm, D, 0)] * 3 + [_full_spec((1, D))],
        out_specs=[_full_spec((1, 1)), _row_spec(tm, D, 0), _full_spec((1, D))],
        out_shape=[jax.ShapeDtypeStruct((1, 1), F32), jax.ShapeDtypeStruct((n, D), F32), jax.ShapeDtypeStruct((1, D), F32)],
        compiler_params=_cparams(("arbitrary",)),
    )(x1, ffo, tgt, gf)


def _pad_cols(a, width):
    return jnp.pad(a, ((0, 0), (0, width - a.shape[1])))


def w_in_to_padded(w):
    return _pad_cols(jnp.concatenate([w[:, 3360:5408], w[:, 0:1536], w[:, 1568:3360], w[:, 1536:1568]], axis=1), NP)


def w_in_from_padded(wp):
    return jnp.concatenate([wp[:, 2048:3584], wp[:, 5376:5408], wp[:, 3584:5376], wp[:, 0:2048]], axis=1)


def ff_interleave(a):
    r = a.shape[0]
    halves = jnp.stack([_pad_cols(a[:, :D_FF], FFP), _pad_cols(a[:, D_FF:], FFP)], axis=1)
    return halves.reshape(r, 2, FFP // LANE, LANE).transpose(0, 2, 1, 3).reshape(r, 2 * FFP)


def ff_deinterleave(a):
    r = a.shape[0]
    halves = a.reshape(r, FFP // LANE, 2, LANE).transpose(0, 2, 1, 3).reshape(r, 2, FFP)
    return halves[:, :, :D_FF].reshape(r, 2 * D_FF)


def _rows_into(w, rows, off):
    return jnp.zeros((rows, w.shape[1]), w.dtype).at[off:off + w.shape[0]].set(w)


LATE = ("gla_proj", "rwkv_proj", "w_out", "ffn_up", "ffn_conv_w", "ffn_down")


def local_step(x, tgt, w, nb, t, late_blocks=None):
    n = nb * t
    tm = min(n, 1024)
    tkt = min(n, 2048)
    tr = min(n, 256)
    vec = lambda a: a.reshape(1, -1)
    w = dict(w)

    w_in_p = w_in_to_padded(w["w_in"])
    wa2_f, wa2_b = _rows_into(w["gla_wa2_f"], LANE, 0), _rows_into(w["gla_wa2_b"], LANE, GLA_RANK)
    w2f, w2b = _rows_into(w["rwkv_w2_f"], LANE, 0), _rows_into(w["rwkv_w2_b"], LANE, 0)
    a2 = _rows_into(w["rwkv_a2"], LANE, 64)
    g1, g2n, gf = vec(w["norm1_g"]), vec(w["norm2_g"]), vec(w["norm_f_g"])
    mu_prev, mu_next = vec(w["rwkv_mu_prev"]), vec(w["rwkv_mu_next"])
    pre_params = [vec(w["rwkv_w0_f"]), w2f, vec(w["rwkv_w0_b"]), w2b, vec(w["rwkv_a0"]), a2, w["rwkv_g2"],
                  vec(w["rwkv_k_k"]), vec(w["rwkv_k_a"])]
    post_params = [vec(w["gla_norm_g"]), vec(w["rwkv_ln_w"]), vec(w["rwkv_ln_b"]), vec(w["rwkv_r_k"])]
    ba_f, ba_b = vec(w["gla_ba_f"]), vec(w["gla_ba_b"])

    (h1,) = rowwise_fwd("norm1_fwd", f_norm, [(x, D, 0)], [g1], [(D, MXU_DTYPE)], tr)
    p = matmul("proj_in", h1, w_in_p, "nn", F32, tm, 512, D)
    s = shift_fwd(p, mu_prev, mu_next, nb, t)
    pre_rows = [(s, 512, 1), (s, LANE, 1536 // LANE), (s, LANE, 1664 // LANE)]
    wf, wb, kp, a_s, b_s, g = rowwise_fwd("rwkv_pre_fwd", f_rwkv_pre, pre_rows, pre_params, [(RW_W, F32)] * 6, tr)
    sc = [to_scan("to_scan_r", s, 0, nb, t, False), to_scan("to_scan_w", wf, 0, nb, t, False, x_bwd=wb),
          to_scan("to_scan_k", kp, 0, nb, t, False), to_scan("to_scan_a", a_s, 0, nb, t, False),
          to_scan("to_scan_b", b_s, 0, nb, t, False), to_scan("to_scan_v", s, 2, nb, t, True)]
    comm = [] if late_blocks is None else [(late_blocks[k], False) for k in LATE]
    y_scf, y_scm, hist_rw, sa_sc, *gathered = rwkv_scan_fwd(*sc, comm=comm)
    for k, g_k in zip(LATE, gathered):
        w[k] = _gathered_to_full(g_k, SHARDED[k])
    ffn_up_p = ff_interleave(w["ffn_up"])
    conv_w_p, conv_b_p = ff_interleave(w["ffn_conv_w"]), ff_interleave(vec(w["ffn_conv_b"]))
    ffn_down_p = jnp.pad(w["ffn_down"], ((0, FFP - D_FF), (0, 0)))
    (y,) = from_scan("from_scan_y", y_scf, y_scm, nb, t, True)
    o_f, hist_f = gla_fwd(p, wa2_f, ba_f, None, nb, t, False)
    o, hist_b = gla_fwd(p, wa2_b, ba_b, o_f, nb, t, True)
    post_rows = [(o, 512, 0), (p, 512, C_OG // 512), (y, 512, 0), (s, 512, 0), (kp, 512, 0), (s, 512, 2), (g, 512, 0)]
    oa, ob = rowwise_fwd("branch_post_fwd", f_branch_post, post_rows, post_params, [(512, MXU_DTYPE)] * 2, tr)
    ya = matmul("gla_proj", oa, w["gla_proj"], "nn", F32, tm, 512, 512)
    yb = matmul("rwkv_proj", ob, w["rwkv_proj"], "nn", F32, tm, 512, 512)
    merge_rows = [(p, D, 0), (p, D, 1), (ya, D, 0), (yb, D, 0)]
    (merged,) = rowwise_fwd("merge_fwd", f_merge, merge_rows, [], [(D, MXU_DTYPE)], tr)
    mo = matmul("w_out", merged, w["w_out"], "nn", F32, tm, 512, D)
    x1, h2 = rowwise_fwd("norm2_fwd", f_norm2, [(x, D, 0), (mo, D, 0)], [g2n], [(D, F32), (D, MXU_DTYPE)], tr)
    u = matmul("ffn_up", h2, ffn_up_p, "nn", F32, tm, 512, D)
    z = conv_glu_fwd(u, conv_w_p, conv_b_p, nb, t)
    ffo = matmul("ffn_down", z, ffn_down_p, "nn", F32, tm, 512, FFP // 2)
    loss, dx2, dgf = loss_head(x1, ffo, tgt, gf, tr)

    dz = matmul("ffn_down_dx", dx2, ffn_down_p, "nt", F32, tm, FFP // 2, D)
    d_ffn_down_p = matmul("ffn_down_dw", z, dx2, "tn", F32, FFP // 2, 512, tkt)
    du, d_conv_w_p, d_conv_b_p = conv_glu_bwd(u, dz, conv_w_p, conv_b_p, nb, t)
    dh2 = matmul("ffn_up_dx", du, ffn_up_p, "nt", F32, tm, D, FFP // 2)
    d_ffn_up_p = matmul("ffn_up_dw", h2, du, "tn", F32, D, 512, tkt)
    (dx1,), (dg2,) = rowwise_bwd("norm2_bwd", f_norm2, [(x, D, 0), (mo, D, 0)], [g2n],
                                 [[(dx2, D, 0)], [(dh2, D, 0)]], tr, grad_rows=[1])
    dmerged = matmul("w_out_dx", dx1, w["w_out"], "nt", F32, tm, D, D)
    d_w_out = matmul("w_out_dw", merged, dx1, "tn", F32, D, 512, tkt)
    (dga, dgb, dya, dyb), _ = rowwise_bwd("merge_bwd", f_merge, merge_rows, [], [[(dmerged, D, 0)]], tr)
    d_oa = matmul("gla_proj_dx", dya, w["gla_proj"], "nt", F32, tm, 512, D)
    d_gla_proj = matmul("gla_proj_dw", oa, dya, "tn", F32, 512, 512, tkt)
    d_ob = matmul("rwkv_proj_dx", dyb, w["rwkv_proj"], "nt", F32, tm, 512, D)
    d_rwkv_proj = matmul("rwkv_proj_dw", ob, dyb, "tn", F32, 512, 512, tkt)
    (d_o, d_og, d_y, d_r_post, d_kp_post, d_v_post, d_g), d_post = rowwise_bwd(
        "branch_post_bwd", f_branch_post, post_rows, post_params, [[(d_oa, 512, 0)], [(d_ob, 512, 0)]], tr)
    late_grads = {"gla_proj": d_gla_proj, "rwkv_proj": d_rwkv_proj, "w_out": d_w_out, "ffn_up": ff_deinterleave(d_ffn_up_p),
                  "ffn_conv_w": ff_deinterleave(d_conv_w_p), "ffn_down": d_ffn_down_p[0:D_FF]}
    comm = [] if late_blocks is None else [(_full_to_slices(late_grads[k], SHARDED[k]), True) for k in LATE]
    dsc = rwkv_scan_bwd(*sc, hist_rw, sa_sc, to_scan("to_scan_dy", d_y, 0, nb, t, True), comm=comm)
    received = dict(zip(LATE, dsc[6:]))
    d_r_scan, d_kp_scan, d_a_scan, d_b_scan = from_scan("from_scan_rkab", dsc[0], dsc[1], nb, t, False)
    d_wf, d_wb = from_scan("from_scan_w", dsc[2], dsc[3], nb, t, False, groups=(0, 1))
    (d_v_scan,) = from_scan("from_scan_dv", dsc[4], dsc[5], nb, t, True)
    (d_k, d_wlal, d_gl), d_pre = rowwise_bwd(
        "rwkv_pre_bwd", f_rwkv_pre, pre_rows, pre_params,
        [[(d_wf, 512, 0)], [(d_wb, 512, 0)], [(d_kp_scan, 512, 0), (d_kp_post, 512, 0)],
         [(d_a_scan, 512, 0)], [(d_b_scan, 512, 0)], [(d_g, 512, 0)]], tr)
    ds = jnp.concatenate([d_r_scan + d_r_post, d_k, d_v_scan + d_v_post, d_wlal, d_gl], axis=1)
    dp_rw, d_mu_prev, d_mu_next = shift_bwd(p, ds, mu_prev, mu_next, nb, t)
    dqkv_f, d_wa2_f, d_ba_f = gla_bwd(p, wa2_f, ba_f, hist_f, d_o, None, nb, t, False)
    dqkv, d_wa2_b, d_ba_b = gla_bwd(p, wa2_b, ba_b, hist_b, d_o, dqkv_f, nb, t, True)
    dp = jnp.concatenate([dga, dgb, dqkv[:, 0:1024], d_og, dp_rw, dqkv[:, 1024:1152],
                          jnp.zeros((n, NP - C_AFAB - LANE), F32)], axis=1).astype(MXU_DTYPE)
    d_w_in_p = matmul("proj_in_dw", h1, dp, "tn", F32, D, 512, tkt)
    grads = {
        "w_in": w_in_from_padded(d_w_in_p),
        "gla_wa2_f": d_wa2_f[0:GLA_RANK], "gla_ba_f": d_ba_f, "gla_wa2_b": d_wa2_b[GLA_RANK:2 * GLA_RANK], "gla_ba_b": d_ba_b,
        "gla_norm_g": d_post[0], "rwkv_mu_prev": d_mu_prev, "rwkv_mu_next": d_mu_next,
        "rwkv_w0_f": d_pre[0], "rwkv_w2_f": d_pre[1][0:64], "rwkv_w0_b": d_pre[2], "rwkv_w2_b": d_pre[3][0:64],
        "rwkv_a0": d_pre[4], "rwkv_a2": d_pre[5][64:128], "rwkv_g2": d_pre[6], "rwkv_k_k": d_pre[7], "rwkv_k_a": d_pre[8],
        "rwkv_r_k": d_post[3], "rwkv_ln_w": d_post[1], "rwkv_ln_b": d_post[2],
        "norm2_g": dg2, "ffn_conv_b": ff_deinterleave(d_conv_b_p), "norm_f_g": dgf, **late_grads,
    }
    early = [k for k in SHARDED if k not in LATE]
    comm = [] if late_blocks is None else [(_full_to_slices(grads[k], SHARDED[k]), True) for k in early]
    dh1, *got = matmul("proj_in_dx", dp, w_in_p, "nt", F32, tm, D, FFP // 2, comm=comm) if comm else \
        [matmul("proj_in_dx", dp, w_in_p, "nt", F32, tm, D, FFP // 2)]
    received.update(zip(early, got))
    (grad_x,), (grads["norm1_g"],) = rowwise_bwd("norm1_bwd", f_norm, [(x, D, 0)], [g1], [[(dh1, D, 0)]], tr,
                                                 adds=[(0, (dx1, D, 0))])
    return loss, grad_x, grads, received


MESH = pl.DeviceIdType.MESH


def remote_exchange(name, items):
    n = len(items)

    def body(*refs):
        start, wait = _exchange_plan([sc for _, sc in items], refs[:n], refs[n:2 * n], *refs[2 * n:])
        start()
        wait()

    args, specs, shapes, sems = _comm_specs(items)
    return pl.pallas_call(body, name=name, in_specs=specs, out_specs=specs, out_shape=shapes, scratch_shapes=sems)(*args)


def gather_two_level(name, blocks):
    n = len(blocks)

    def body(*refs):
        in_refs, out_refs = refs[:n], refs[n:2 * n]
        send_sems, recv_sems, local_sems = refs[2 * n:]
        x, y, c = lax.axis_index("x"), lax.axis_index("y"), lax.axis_index("c")
        me, sibling = (x, y, c), (x, y, 1 - c)
        chips = [(1 - x, y), (x, 1 - y), (1 - x, 1 - y)]

        def copy(i, k, block, to, src=None):
            rows = out_refs[i].at[4 * block[0] + 2 * block[1] + block[2]]
            return pltpu.make_async_remote_copy(src_ref=rows if src is None else src, dst_ref=rows, send_sem=send_sems.at[i, k],
                                                recv_sem=recv_sems.at[i, k], device_id=to, device_id_type=MESH)

        own = [pltpu.make_async_copy(in_refs[i], out_refs[i].at[4 * x + 2 * y + c], local_sems.at[i]) for i in range(n)]
        first = [copy(i, 0, me, sibling, src=in_refs[i]) for i in range(n)]
        first += [copy(i, 1 + j, me, (*chip, c), src=in_refs[i]) for j, chip in enumerate(chips) for i in range(n)]
        for cp in own + first:
            cp.start()
        passed = []
        for j, chip in enumerate(chips):
            for i in range(n):
                copy(i, 1 + j, (*chip, c), me).wait_recv()
                onward = copy(i, 4 + j, (*chip, c), sibling)
                onward.start()
                passed.append(onward)
        for i in range(n):
            copy(i, 0, sibling, me).wait_recv()
        for j, chip in enumerate(chips):
            for i in range(n):
                copy(i, 4 + j, (*chip, 1 - c), me).wait_recv()
        for cp in first + passed:
            cp.wait_send()
        for cp in own:
            cp.wait()

    args, specs, shapes, sems = _comm_specs([(b, False) for b in blocks])
    return pl.pallas_call(body, name=name, in_specs=specs, out_specs=specs, out_shape=shapes, scratch_shapes=sems)(*args)


def _exchange_plan(flags, in_refs, out_refs, send_sems, recv_sems, local_sems):
    x, y, c = lax.axis_index("x"), lax.axis_index("y"), lax.axis_index("c")
    me = 4 * x + 2 * y + c

    def peer(k):
        px = 1 - x if (k >> 2) & 1 else x
        py = 1 - y if (k >> 1) & 1 else y
        pc = 1 - c if k & 1 else c
        return (px, py, pc), 4 * px + 2 * py + pc

    def copies(with_arrivals):
        own, sends, recvs = [], [], []
        for i, scatter in enumerate(flags):
            src = in_refs[i].at[me] if scatter else in_refs[i]
            own.append(pltpu.make_async_copy(src, out_refs[i].at[me], local_sems.at[i]))
        for k in range(1, N_DEV):
            dev, slot = peer(k)
            for i, scatter in enumerate(flags):
                src = in_refs[i].at[slot] if scatter else in_refs[i]
                pair = dict(send_sem=send_sems.at[i, k - 1], recv_sem=recv_sems.at[i, k - 1], device_id=dev, device_id_type=MESH)
                sends.append(pltpu.make_async_remote_copy(src_ref=src, dst_ref=out_refs[i].at[me], **pair))
                if with_arrivals:
                    recvs.append(pltpu.make_async_remote_copy(src_ref=out_refs[i].at[slot], dst_ref=out_refs[i].at[slot], **pair))
        return own, sends, recvs

    def start():
        own, sends, _ = copies(False)
        for cp in own + sends:
            cp.start()

    def wait():
        own, sends, recvs = copies(True)
        for send, recv in zip(sends, recvs):
            recv.wait_recv()
            send.wait_send()
        for cp in own:
            cp.wait()

    return start, wait


def _adam_tiles(r, c):
    tc = 256 if (c % 256 == 0 and r * c > 128 * 1024) else c
    tr = 128 if (r % 128 == 0 and r > 128) else r
    return tr, tc


def adamw_reduce(name, parts, w, m, v):
    r, c = w.shape
    tr, tc = _adam_tiles(r, c)

    def body(p_ref, w_ref, m_ref, v_ref, g_ref, d_ref, nm_ref, nv_ref):
        g = p_ref[0]
        for d in range(1, N_DEV):
            g = g + p_ref[d]
        nm = ADAM_B1 * m_ref[...] + (1.0 - ADAM_B1) * g
        nv = ADAM_B2 * v_ref[...] + (1.0 - ADAM_B2) * (g * g)
        m_hat = nm / (1.0 - ADAM_B1 ** ADAM_STEP)
        v_hat = nv / (1.0 - ADAM_B2 ** ADAM_STEP)
        g_ref[...] = g
        d_ref[...] = -ADAM_LR * (m_hat / (jnp.sqrt(v_hat) + ADAM_EPS) + ADAM_WD * w_ref[...])
        nm_ref[...] = nm
        nv_ref[...] = nv

    spec = pl.BlockSpec((tr, tc), lambda i, j: (i, j))
    return pl.pallas_call(
        body, name=name, grid=(r // tr, c // tc),
        in_specs=[pl.BlockSpec((N_DEV, tr, tc), lambda i, j: (0, i, j)), spec, spec, spec],
        out_specs=[spec] * 4, out_shape=[jax.ShapeDtypeStruct((r, c), F32)] * 4,
        compiler_params=_cparams(("arbitrary", "arbitrary")),
    )(parts, w, m, v)


SHARDED = {"w_in": 1, "gla_wa2_f": 1, "gla_wa2_b": 1, "gla_proj": 1, "rwkv_w2_f": 1, "rwkv_w2_b": 1, "rwkv_a2": 1,
           "rwkv_g2": 1, "rwkv_proj": 1, "w_out": 0, "ffn_up": 1, "ffn_conv_w": 1, "ffn_down": 0}
BF16_GATHER = ("w_in", "gla_proj", "rwkv_proj", "w_out", "ffn_up", "ffn_down")
REPLICATED = ("norm1_g", "gla_ba_f", "gla_ba_b", "gla_norm_g", "rwkv_mu_prev", "rwkv_mu_next", "rwkv_w0_f", "rwkv_w0_b",
              "rwkv_a0", "rwkv_k_k", "rwkv_k_a", "rwkv_r_k", "rwkv_ln_w", "rwkv_ln_b", "norm2_g", "ffn_conv_b", "norm_f_g")
WEIGHTS = ("norm1_g", "w_in", "gla_wa2_f", "gla_ba_f", "gla_wa2_b", "gla_ba_b", "gla_norm_g", "gla_proj", "rwkv_mu_prev",
           "rwkv_mu_next", "rwkv_w0_f", "rwkv_w2_f", "rwkv_w0_b", "rwkv_w2_b", "rwkv_a0", "rwkv_a2", "rwkv_g2", "rwkv_k_k",
           "rwkv_k_a", "rwkv_r_k", "rwkv_ln_w", "rwkv_ln_b", "rwkv_proj", "w_out", "norm2_g", "ffn_up", "ffn_conv_w",
           "ffn_conv_b", "ffn_down", "norm_f_g")


def _gathered_to_full(g, axis):
    if axis == 0:
        return g.reshape(N_DEV * g.shape[1], g.shape[2])
    return g.transpose(1, 0, 2).reshape(g.shape[1], N_DEV * g.shape[2])


def _full_to_slices(a, axis):
    if axis == 0:
        return a.reshape(N_DEV, a.shape[0] // N_DEV, a.shape[1])
    return a.reshape(a.shape[0], N_DEV, a.shape[1] // N_DEV).transpose(1, 0, 2)


def _pack_rows(size):
    return -(-size // (8 * LANE)) * 8


def _pack(d):
    parts = []
    for k in REPLICATED:
        rows = d[k].reshape(-1, LANE).astype(F32)
        parts.append(jnp.pad(rows, ((0, _pack_rows(rows.size) - rows.shape[0]), (0, 0))))
    return jnp.concatenate(parts, axis=0)


def _unpack(packed, shapes):
    out, pos = {}, 0
    for k in REPLICATED:
        size = int(np.prod(shapes[k]))
        out[k] = packed[pos:pos + size // LANE].reshape(shapes[k])
        pos += _pack_rows(size)
    return out


def kernel(x, norm1_g, w_in, gla_wa2_f, gla_ba_f, gla_wa2_b, gla_ba_b, gla_norm_g, gla_proj, rwkv_mu_prev, rwkv_mu_next, rwkv_w0_f, rwkv_w2_f, rwkv_w0_b, rwkv_w2_b, rwkv_a0, rwkv_a2, rwkv_g2, rwkv_k_k, rwkv_k_a, rwkv_r_k, rwkv_ln_w, rwkv_ln_b, rwkv_proj, w_out, norm2_g, ffn_up, ffn_conv_w, ffn_conv_b, ffn_down, norm_f_g, loss_target, m_norm1_g, m_w_in, m_gla_wa2_f, m_gla_ba_f, m_gla_wa2_b, m_gla_ba_b, m_gla_norm_g, m_gla_proj, m_rwkv_mu_prev, m_rwkv_mu_next, m_rwkv_w0_f, m_rwkv_w2_f, m_rwkv_w0_b, m_rwkv_w2_b, m_rwkv_a0, m_rwkv_a2, m_rwkv_g2, m_rwkv_k_k, m_rwkv_k_a, m_rwkv_r_k, m_rwkv_ln_w, m_rwkv_ln_b, m_rwkv_proj, m_w_out, m_norm2_g, m_ffn_up, m_ffn_conv_w, m_ffn_conv_b, m_ffn_down, m_norm_f_g, v_norm1_g, v_w_in, v_gla_wa2_f, v_gla_ba_f, v_gla_wa2_b, v_gla_ba_b, v_gla_norm_g, v_gla_proj, v_rwkv_mu_prev, v_rwkv_mu_next, v_rwkv_w0_f, v_rwkv_w2_f, v_rwkv_w0_b, v_rwkv_w2_b, v_rwkv_a0, v_rwkv_a2, v_rwkv_g2, v_rwkv_k_k, v_rwkv_k_a, v_rwkv_r_k, v_rwkv_ln_w, v_rwkv_ln_b, v_rwkv_proj, v_w_out, v_norm2_g, v_ffn_up, v_ffn_conv_w, v_ffn_conv_b, v_ffn_down, v_norm_f_g):
    args = locals()
    wts = {k: args[k] for k in WEIGHTS}
    mom = {k: args["m_" + k] for k in WEIGHTS}
    var = {k: args["v_" + k] for k in WEIGHTS}
    shapes = {k: wts[k].shape for k in WEIGHTS}
    nb, t = x.shape[0], x.shape[1]
    mat = lambda a: a.reshape(a.shape[-2], a.shape[-1])

    block = lambda k: mat(wts[k]).astype(MXU_DTYPE) if k in BF16_GATHER else mat(wts[k])
    early = [k for k in SHARDED if k not in LATE]
    gathered = gather_two_level("gather_weights", [block(k) for k in early])
    full = {k: _gathered_to_full(g, SHARDED[k]) for k, g in zip(early, gathered)}
    for k in REPLICATED:
        full[k] = wts[k].reshape(-1) if k in ("norm_f_g", "rwkv_r_k") else wts[k][0]

    loss, grad_x, grads, received = local_step(x.reshape(nb * t, D), loss_target.reshape(nb * t, D), full, nb, t,
                                               late_blocks={k: block(k) for k in LATE})

    (rep_parts,) = remote_exchange("exchange_replicated", [(_pack(grads), False)])

    res = {}
    for k in SHARDED:
        outs = adamw_reduce("adamw_" + k, received[k], mat(wts[k]), mat(mom[k]), mat(var[k]))
        res[k] = [o.reshape(shapes[k]) for o in outs]
    packed = adamw_reduce("adamw_replicated", rep_parts, _pack(wts), _pack(mom), _pack(var))
    unpacked = [_unpack(p, shapes) for p in packed]
    for k in REPLICATED:
        res[k] = [u[k] for u in unpacked]

    total = lax.psum(loss[0, 0], ("x", "y", "c"))
    out = [total, grad_x.reshape(x.shape)]
    for j in range(4):
        out += [res[k][j] for k in WEIGHTS]
    return tuple(out)
```

```python
import functools

import jax
import jax.numpy as jnp
import numpy as np
from jax import lax
from jax.experimental import pallas as pl
from jax.experimental.pallas import tpu as pltpu

F32 = jnp.float32
MXU_DTYPE = jnp.bfloat16

D = 1024
SEQ = 2048
GLA_H, GLA_DK, GLA_DV, GLA_CHUNK = 4, 64, 128, 64
GLA_RANK = 16
GLA_LOGIT_NORM = 16.0
RW_H, RW_N = 8, 64
RW_W = 512
D_FF = 2752
NORM_EPS = 1e-6
HEAD_NORM_EPS = 1e-5
RW_GN_EPS = RW_N * 1e-5
N_DEV = 8
ADAM_LR, ADAM_B1, ADAM_B2, ADAM_EPS, ADAM_WD, ADAM_STEP = 0.001, 0.9, 0.999, 1e-08, 0.01, 10

C_GA, C_GB, C_Q, C_K, C_V, C_OG = 0, 1024, 2048, 2304, 2560, 3072
C_RW = 3584
C_R, C_RK, C_RV, C_WLAL, C_GL = 3584, 4096, 4608, 5120, 5248
C_AFAB = 5376
NP = 5632
RW_PW = 1792
FFP = 2816
LANE = 128
VMEM_LIMIT = 56 * 1024 * 1024


def _cparams(sem):
    return pltpu.CompilerParams(dimension_semantics=sem, vmem_limit_bytes=VMEM_LIMIT)


@jax.custom_vjp
def mm(a, b):
    return jnp.dot(a.astype(MXU_DTYPE), b.astype(MXU_DTYPE), preferred_element_type=F32)


def _mm_fwd(a, b):
    return mm(a, b), (a, b)


def _mm_bwd(res, g):
    a, b = res
    gb = g.astype(MXU_DTYPE)
    da = lax.dot_general(gb, b.astype(MXU_DTYPE), (((1,), (1,)), ((), ())), preferred_element_type=F32)
    db = lax.dot_general(a.astype(MXU_DTYPE), gb, (((0,), (0,)), ((), ())), preferred_element_type=F32)
    return da.astype(a.dtype), db.astype(b.dtype)


mm.defvjp(_mm_fwd, _mm_bwd)


@jax.custom_vjp
def mm_nt(a, b):
    return lax.dot_general(a.astype(MXU_DTYPE), b.astype(MXU_DTYPE), (((1,), (1,)), ((), ())), preferred_element_type=F32)


def _mm_nt_fwd(a, b):
    return mm_nt(a, b), (a, b)


def _mm_nt_bwd(res, g):
    a, b = res
    gb = g.astype(MXU_DTYPE)
    da = jnp.dot(gb, b.astype(MXU_DTYPE), preferred_element_type=F32)
    db = lax.dot_general(gb, a.astype(MXU_DTYPE), (((0,), (0,)), ((), ())), preferred_element_type=F32)
    return da.astype(a.dtype), db.astype(b.dtype)


mm_nt.defvjp(_mm_nt_fwd, _mm_nt_bwd)


@jax.custom_vjp
def mm_tn(a, b):
    return lax.dot_general(a.astype(MXU_DTYPE), b.astype(MXU_DTYPE), (((0,), (0,)), ((), ())), preferred_element_type=F32)


def _mm_tn_fwd(a, b):
    return mm_tn(a, b), (a, b)


def _mm_tn_bwd(res, g):
    a, b = res
    gb = g.astype(MXU_DTYPE)
    da = lax.dot_general(b.astype(MXU_DTYPE), gb, (((1,), (1,)), ((), ())), preferred_element_type=F32)
    db = jnp.dot(a.astype(MXU_DTYPE), gb, preferred_element_type=F32)
    return da.astype(a.dtype), db.astype(b.dtype)


mm_tn.defvjp(_mm_tn_fwd, _mm_tn_bwd)


@functools.partial(jax.custom_vjp, nondiff_argnums=(2, 3))
def sel_dot(x, s, dims, x_first):
    sb = s.astype(MXU_DTYPE)
    hi = x.astype(MXU_DTYPE)
    r1 = x - hi.astype(F32)
    mid = r1.astype(MXU_DTYPE)
    lo = (r1 - mid.astype(F32)).astype(MXU_DTYPE)
    out = None
    for part in (hi, mid, lo):
        ops = (part, sb) if x_first else (sb, part)
        d = lax.dot_general(*ops, (dims, ((), ())), preferred_element_type=F32)
        out = d if out is None else out + d
    return out


def _sel_dot_fwd(x, s, dims, x_first):
    return sel_dot(x, s, dims, x_first), s


def _sel_dot_bwd(dims, x_first, s, g):
    if x_first:
        (cx,), (cs,) = dims
        dx = sel_dot(g, s, ((1,), (1 - cs,)), True) if cx == 1 else sel_dot(g, s, ((1 - cs,), (1,)), False)
    else:
        (cs,), (cx,) = dims
        dx = sel_dot(g, s, ((1 - cs,), (0,)), False) if cx == 0 else sel_dot(g, s, ((0,), (1 - cs,)), True)
    return dx, jnp.zeros_like(s)


sel_dot.defvjp(_sel_dot_fwd, _sel_dot_bwd)


def mm_exact(a, b, b_is_01=True):
    return sel_dot(a, b, ((1,), (0,)), True) if b_is_01 else sel_dot(b, a, ((1,), (0,)), False)


def mm_tn_exact(a, b):
    return sel_dot(a, b, ((0,), (0,)), True)


def _softplus(x):
    return jnp.maximum(x, 0.0) + jnp.log(1.0 + jnp.exp(-jnp.abs(x)))


def _sigmoid(x):
    return jax.nn.sigmoid(x)


def _silu(x):
    return x * _sigmoid(x)


def _rmsnorm(x, g):
    return x * lax.rsqrt(jnp.mean(x * x, axis=-1, keepdims=True) + NORM_EPS) * g


def _segment_sum(x, seg):
    width = x.shape[1]
    i = lax.broadcasted_iota(jnp.int32, (width, width), 0) // seg
    j = lax.broadcasted_iota(jnp.int32, (width, width), 1) // seg
    return mm_exact(x, (i == j).astype(F32))


def _row_spec(tm, width, cb):
    return pl.BlockSpec((tm, width), lambda i: (i, cb))


def _full_spec(shape):
    nd = len(shape)
    return pl.BlockSpec(tuple(shape), lambda i: (0,) * nd)


def rowwise_fwd(name, f, rows, params, outs, tm):
    n = rows[0][0].shape[0]
    nr, npar = len(rows), len(params)

    def body(*refs):
        rv = [r[...] for r in refs[:nr]]
        pv = [r[...] for r in refs[nr:nr + npar]]
        res = f(rv, pv)
        for o_ref, val in zip(refs[nr + npar:], res):
            o_ref[...] = val.astype(o_ref.dtype)

    return pl.pallas_call(
        body, name=name, grid=(n // tm,),
        in_specs=[_row_spec(tm, w, cb) for _, w, cb in rows] + [_full_spec(p.shape) for p in params],
        out_specs=[_row_spec(tm, w, 0) for w, _ in outs],
        out_shape=[jax.ShapeDtypeStruct((n, w), dt) for w, dt in outs],
        compiler_params=_cparams(("arbitrary",)),
    )(*[a for a, _, _ in rows], *params)


def rowwise_bwd(name, f, rows, params, douts, tm, adds=(), grad_rows=None):
    n = rows[0][0].shape[0]
    nr, npar = len(rows), len(params)
    grad_rows = list(range(nr)) if grad_rows is None else list(grad_rows)
    flat_d = [d for group in douts for d in group]
    nd, na, ng = len(flat_d), len(adds), len(grad_rows)

    def body(*refs):
        rv = [r[...] for r in refs[:nr]]
        pv = [r[...] for r in refs[nr:nr + npar]]
        dflat = [r[...].astype(F32) for r in refs[nr + npar:nr + npar + nd]]
        av = [r[...] for r in refs[nr + npar + nd:nr + npar + nd + na]]
        o = nr + npar + nd + na
        drow_refs, dpar_refs = refs[o:o + ng], refs[o + ng:o + ng + npar]
        dv, pos = [], 0
        for group in douts:
            dv.append(sum(dflat[pos + 1:pos + len(group)], dflat[pos]))
            pos += len(group)

        @pl.when(pl.program_id(0) == 0)
        def _():
            for r in dpar_refs:
                r[...] = jnp.zeros_like(r)

        def g(grows, pars):
            full = list(rv)
            for i, val in zip(grad_rows, grows):
                full[i] = val
            return f(full, pars)

        res, vjp = jax.vjp(g, [rv[i] for i in grad_rows], pv)
        drows, dpars = vjp([d.astype(r.dtype) for d, r in zip(dv, res)])
        drows = [d.astype(F32) for d in drows]
        for (idx, _), a in zip(adds, av):
            drows[idx] = drows[idx] + a.astype(F32)
        for r, d in zip(drow_refs, drows):
            r[...] = d
        for r, d in zip(dpar_refs, dpars):
            r[...] += d.astype(F32)

    res = pl.pallas_call(
        body, name=name, grid=(n // tm,),
        in_specs=[_row_spec(tm, w, cb) for _, w, cb in rows] + [_full_spec(p.shape) for p in params]
        + [_row_spec(tm, w, cb) for _, w, cb in flat_d] + [_row_spec(tm, w, cb) for _, (_, w, cb) in adds],
        out_specs=[_row_spec(tm, rows[i][1], 0) for i in grad_rows] + [_full_spec(p.shape) for p in params],
        out_shape=[jax.ShapeDtypeStruct((n, rows[i][1]), F32) for i in grad_rows]
        + [jax.ShapeDtypeStruct(p.shape, F32) for p in params],
        compiler_params=_cparams(("arbitrary",)),
    )(*[a for a, _, _ in rows], *params, *[a for a, _, _ in flat_d], *[a for _, (a, _, _) in adds])
    return res[:ng], res[ng:]


def matmul(name, a, b, mode, out_dtype, tm, tn, tk, comm=()):
    nc = len(comm)
    flags = [sc for _, sc in comm]
    if mode == "nn":
        (m, k), n = a.shape, b.shape[1]
        a_spec = pl.BlockSpec((tm, tk), lambda i, j, kk: (i, kk))
        b_spec = pl.BlockSpec((tk, tn), lambda i, j, kk: (kk, j))
        dims = (((1,), (0,)), ((), ()))
    elif mode == "nt":
        (m, k), n = a.shape, b.shape[0]
        a_spec = pl.BlockSpec((tm, tk), lambda i, j, kk: (i, kk))
        b_spec = pl.BlockSpec((tn, tk), lambda i, j, kk: (j, kk))
        dims = (((1,), (1,)), ((), ()))
    else:
        (k, m), n = a.shape, b.shape[1]
        a_spec = pl.BlockSpec((tk, tm), lambda i, j, kk: (kk, i))
        b_spec = pl.BlockSpec((tk, tn), lambda i, j, kk: (kk, j))
        dims = (((0,), (0,)), ((), ()))
    assert m % tm == 0 and n % tn == 0 and k % tk == 0, (name, a.shape, b.shape, tm, tn, tk)
    nk = k // tk
    grid = (m // tm, n // tn, nk)

    def body(*refs):
        a_ref, b_ref, c_in, o_ref = refs[0], refs[1], refs[2:2 + nc], refs[2 + nc]
        c_out, acc_ref, sems = refs[3 + nc:3 + 2 * nc], refs[3 + 2 * nc], refs[4 + 2 * nc:]
        kk = pl.program_id(2)
        step = (pl.program_id(0) * grid[1] + pl.program_id(1)) * nk + kk
        if nc:
            start, wait = _exchange_plan(flags, c_in, c_out, *sems)

            @pl.when(step == 0)
            def _():
                start()

        part = lax.dot_general(a_ref[...].astype(MXU_DTYPE), b_ref[...].astype(MXU_DTYPE), dims, preferred_element_type=F32)
        if nk == 1:
            o_ref[...] = part.astype(o_ref.dtype)
        else:
            @pl.when(kk == 0)
            def _():
                acc_ref[...] = part

            @pl.when((kk > 0) & (kk < nk - 1))
            def _():
                acc_ref[...] += part

            @pl.when(kk == nk - 1)
            def _():
                o_ref[...] = (acc_ref[...] + part).astype(o_ref.dtype)

        if nc:
            @pl.when(step == grid[0] * grid[1] * nk - 1)
            def _():
                wait()

    c_args, c_specs, c_shapes, c_sems = _comm_specs(comm)
    res = pl.pallas_call(
        body, name=name, grid=grid,
        in_specs=[a_spec, b_spec] + c_specs,
        out_specs=[pl.BlockSpec((tm, tn), lambda i, j, kk: (i, j))] + c_specs,
        out_shape=[jax.ShapeDtypeStruct((m, n), out_dtype)] + c_shapes,
        scratch_shapes=[pltpu.VMEM((tm, tn) if nk > 1 else (8, LANE), F32)] + c_sems,
        compiler_params=_cparams(("arbitrary", "arbitrary", "arbitrary")),
    )(a, b, *c_args)
    return res if nc else res[0]


def _prev(u, first):
    return jnp.where(first, 0.0, pltpu.roll(u, 1, 0))


def _next(u, last):
    return jnp.where(last, 0.0, pltpu.roll(u, u.shape[0] - 1, 0))


def _edge_masks(t, w):
    row = lax.broadcasted_iota(jnp.int32, (t, w), 0)
    return row == 0, row == t - 1


WIN, HALO = 64, 8
MID = slice(HALO, HALO + WIN)


def _window(ref, i, t):
    r0 = pl.multiple_of(i * WIN, WIN)
    before = ref[pl.ds(pl.multiple_of(jnp.maximum(r0 - HALO, 0), HALO), HALO), :]
    after = ref[pl.ds(pl.multiple_of(jnp.minimum(r0 + WIN, t - HALO), HALO), HALO), :]
    before = jnp.where(i == 0, 0.0, before.astype(F32))
    after = jnp.where(i == t // WIN - 1, 0.0, after.astype(F32))
    return jnp.concatenate([before, ref[pl.ds(r0, WIN), :].astype(F32), after], axis=0)


def _wprev(u):
    return pltpu.roll(u, 1, 0)


def _wnext(u):
    return pltpu.roll(u, u.shape[0] - 1, 0)


def _mid_rows(i):
    return pl.ds(pl.multiple_of(i * WIN, WIN), WIN)


def _colsum(x):
    return jnp.sum(x[MID], axis=0, keepdims=True)


SHIFT_CW = 256


def shift_fwd(p, mu_prev, mu_next, nb, t):
    cw, c0 = SHIFT_CW, C_RW // SHIFT_CW

    def body(p_ref, mp_ref, mn_ref, s_ref):
        x = p_ref[...]
        first, last = _edge_masks(t, cw)
        s_ref[...] = x + mp_ref[...] * (_prev(x, first) - x) + mn_ref[...] * (_next(x, last) - x)

    return pl.pallas_call(
        body, name="rwkv_shift_fwd", grid=(nb, RW_PW // cw),
        in_specs=[pl.BlockSpec((t, cw), lambda b, j: (b, c0 + j)), pl.BlockSpec((1, cw), lambda b, j: (0, j)),
                  pl.BlockSpec((1, cw), lambda b, j: (0, j))],
        out_specs=pl.BlockSpec((t, cw), lambda b, j: (b, j)),
        out_shape=jax.ShapeDtypeStruct((nb * t, RW_PW), F32),
        compiler_params=_cparams(("arbitrary", "arbitrary")),
    )(p, mu_prev, mu_next)


def shift_bwd(p, ds, mu_prev, mu_next, nb, t):
    cw, c0 = SHIFT_CW, C_RW // SHIFT_CW

    def body(p_ref, ds_ref, mp_ref, mn_ref, dp_ref, dmp_ref, dmn_ref):
        @pl.when(pl.program_id(1) == 0)
        def _():
            dmp_ref[...] = jnp.zeros_like(dmp_ref)
            dmn_ref[...] = jnp.zeros_like(dmn_ref)

        mp, mn = mp_ref[...], mn_ref[...]

        def step(i, carry):
            dmp, dmn = carry
            x, g = _window(p_ref, i, t), _window(ds_ref, i, t)
            dp = g * (1.0 - mp - mn) + _wnext(mp * g) + _wprev(mn * g)
            dp_ref[_mid_rows(i), :] = dp[MID]
            return dmp + _colsum(g * (_wprev(x) - x)), dmn + _colsum(g * (_wnext(x) - x))

        zero = jnp.zeros((1, cw), F32)
        dmp, dmn = lax.fori_loop(0, t // WIN, step, (zero, zero))
        dmp_ref[...] += dmp
        dmn_ref[...] += dmn

    return pl.pallas_call(
        body, name="rwkv_shift_bwd", grid=(RW_PW // cw, nb),
        in_specs=[pl.BlockSpec((t, cw), lambda j, b: (b, c0 + j)), pl.BlockSpec((t, cw), lambda j, b: (b, j)),
                  pl.BlockSpec((1, cw), lambda j, b: (0, j)), pl.BlockSpec((1, cw), lambda j, b: (0, j))],
        out_specs=[pl.BlockSpec((t, cw), lambda j, b: (b, j)), pl.BlockSpec((1, cw), lambda j, b: (0, j)),
                   pl.BlockSpec((1, cw), lambda j, b: (0, j))],
        out_shape=[jax.ShapeDtypeStruct((nb * t, RW_PW), F32), jax.ShapeDtypeStruct((1, RW_PW), F32),
                   jax.ShapeDtypeStruct((1, RW_PW), F32)],
        compiler_params=_cparams(("arbitrary", "arbitrary")),
    )(p, ds, mu_prev, mu_next)


def conv_glu_fwd(u, cw, cb, nb, t):
    def body(u_ref, w_ref, b_ref, z_ref):
        x, w = u_ref[...], w_ref[...]
        first, last = _edge_masks(t, 2 * LANE)
        c = w[0:1] * _prev(x, first) + w[1:2] * x + w[2:3] * _next(x, last) + b_ref[...]
        z_ref[...] = (_silu(c[:, :LANE]) * c[:, LANE:]).astype(z_ref.dtype)

    return pl.pallas_call(
        body, name="conv_glu_fwd", grid=(nb, FFP // LANE),
        in_specs=[pl.BlockSpec((t, 2 * LANE), lambda b, j: (b, j)), pl.BlockSpec((3, 2 * LANE), lambda b, j: (0, j)),
                  pl.BlockSpec((1, 2 * LANE), lambda b, j: (0, j))],
        out_specs=pl.BlockSpec((t, LANE), lambda b, j: (b, j)),
        out_shape=jax.ShapeDtypeStruct((nb * t, FFP), MXU_DTYPE),
        compiler_params=_cparams(("arbitrary", "arbitrary")),
    )(u, cw, cb)


def conv_glu_bwd(u, dz, cw, cb, nb, t):
    def body(u_ref, dz_ref, w_ref, b_ref, du_ref, dw_ref, db_ref):
        @pl.when(pl.program_id(1) == 0)
        def _():
            dw_ref[...] = jnp.zeros_like(dw_ref)
            db_ref[...] = jnp.zeros_like(db_ref)

        w, bias = w_ref[...], b_ref[...]

        def step(i, carry):
            x, g = _window(u_ref, i, t), _window(dz_ref, i, t)
            xp, xn = _wprev(x), _wnext(x)
            c = w[0:1] * xp + w[1:2] * x + w[2:3] * xn + bias
            cg, cv = c[:, :LANE], c[:, LANE:]
            sg = _sigmoid(cg)
            dcg = g * cv * (sg * (1.0 + cg * (1.0 - sg)))
            dcv = g * (cg * sg)
            dc = jnp.concatenate([dcg, dcv], axis=1)
            du = w[1:2] * dc + _wnext(w[0:1] * dc) + _wprev(w[2:3] * dc)
            du_ref[_mid_rows(i), :] = du[MID].astype(du_ref.dtype)
            return tuple(acc + _colsum(val) for acc, val in zip(carry, (dc * xp, dc * x, dc * xn, dc)))

        zero = jnp.zeros((1, 2 * LANE), F32)
        sums = lax.fori_loop(0, t // WIN, step, (zero, zero, zero, zero))
        for row in range(3):
            dw_ref[row:row + 1, :] += sums[row]
        db_ref[...] += sums[3]

    return pl.pallas_call(
        body, name="conv_glu_bwd", grid=(FFP // LANE, nb),
        in_specs=[pl.BlockSpec((t, 2 * LANE), lambda j, b: (b, j)), pl.BlockSpec((t, LANE), lambda j, b: (b, j)),
                  pl.BlockSpec((3, 2 * LANE), lambda j, b: (0, j)), pl.BlockSpec((1, 2 * LANE), lambda j, b: (0, j))],
        out_specs=[pl.BlockSpec((t, 2 * LANE), lambda j, b: (b, j)), pl.BlockSpec((3, 2 * LANE), lambda j, b: (0, j)),
                   pl.BlockSpec((1, 2 * LANE), lambda j, b: (0, j))],
        out_shape=[jax.ShapeDtypeStruct((nb * t, 2 * FFP), MXU_DTYPE), jax.ShapeDtypeStruct((3, 2 * FFP), F32),
                   jax.ShapeDtypeStruct((1, 2 * FFP), F32)],
        compiler_params=_cparams(("arbitrary", "arbitrary")),
    )(u, dz, cw, cb)


def _gla_chunk(q, k, v, afab, wa2p, ba, s_in, reverse, sb):
    c = GLA_CHUNK
    r = sb * c
    ri = lax.broadcasted_iota(jnp.int32, (r, r), 0)
    ci = lax.broadcasted_iota(jnp.int32, (r, r), 1)
    same = (ri // c) == (ci // c)
    keep = same & ((ci >= ri) if reverse else (ci <= ri))
    i_ref = (c - 1 - c // 2) if reverse else (c // 2)
    pick_ref = (ci == (ri // c) * c + i_ref).astype(F32)
    seq_cols = (lax.broadcasted_iota(jnp.int32, (r, sb * LANE), 0) // c) == (lax.broadcasted_iota(jnp.int32, (r, sb * LANE), 1) // LANE)
    expand = lambda x: jnp.where(seq_cols, jnp.concatenate([x] * sb, axis=1), 0.0)
    lane = lax.broadcasted_iota(jnp.int32, (1, LANE), 1)
    outs, states = [None] * GLA_H, [None] * GLA_H
    for pr in range(GLA_H // 2):
        la = -_softplus(-(mm(afab, wa2p[pr]) + ba[pr])) * (1.0 / GLA_LOGIT_NORM)
        b = mm_exact(keep.astype(F32), la, b_is_01=False)
        b_ref = mm_exact(pick_ref, b, b_is_01=False)
        b_last = mm_exact(same.astype(F32), la, b_is_01=False)
        qs = q[pr] * (GLA_DK ** -0.5)
        qi = qs * jnp.exp(b - b_ref)
        ki = k[pr] * jnp.exp(b_ref - b)
        kd = k[pr] * jnp.exp(b_last - b)
        qb = qs * jnp.exp(b)
        dec = jnp.exp(mm_tn_exact(expand(la), jnp.ones((r, LANE), F32)))
        for h in (2 * pr, 2 * pr + 1):
            m = ((lane // GLA_DK) == (h % 2)).astype(F32)
            a = jnp.where(keep, mm_nt(qi * m, ki), 0.0)
            o_intra = mm(a, v[h])
            kv = mm_tn(expand(kd * m), v[h])
            o_inter = mm(expand(qb * m), s_in[h])
            outs[h] = o_intra + o_inter
            states[h] = s_in[h] * dec + kv
    return outs, states


def _gla_load(q_ref, k_ref, v_ref, af_ref, w_ref, ba_ref, sb, rows):
    stack = lambda ref, c0: jnp.concatenate([ref[s, rows, c0:c0 + LANE] for s in range(sb)], axis=0)
    q = [stack(q_ref, pr * LANE) for pr in range(GLA_H // 2)]
    k = [stack(k_ref, pr * LANE) for pr in range(GLA_H // 2)]
    v = [stack(v_ref, h * GLA_DV) for h in range(GLA_H)]
    w = [w_ref[:, pr * LANE:(pr + 1) * LANE] for pr in range(GLA_H // 2)]
    ba = [ba_ref[:, pr * LANE:(pr + 1) * LANE] for pr in range(GLA_H // 2)]
    return q, k, v, stack(af_ref, 0), w, ba


GLA_TILE = 256
GLA_SB = 4


def _gla_specs(nb, t, reverse):
    tile = min(GLA_TILE, t)
    nt = t // tile
    sb = GLA_SB if nb % GLA_SB == 0 else 1
    return tile, tile // GLA_CHUNK, nt, sb, ((lambda j: nt - 1 - j) if reverse else (lambda j: j))


def gla_fwd(p, wa2p, ba, o_add, nb, t, reverse):
    tile, cpt, nt, sb, tj = _gla_specs(nb, t, reverse)
    has_add = o_add is not None

    def body(*refs):
        if has_add:
            q_ref, k_ref, v_ref, af_ref, w_ref, ba_ref, add_ref, o_ref, hist_ref, s_ref = refs
        else:
            q_ref, k_ref, v_ref, af_ref, w_ref, ba_ref, o_ref, hist_ref, s_ref = refs

        @pl.when(pl.program_id(1) == 0)
        def _():
            s_ref[...] = jnp.zeros_like(s_ref)

        def step(i, carry):
            ci = (cpt - 1 - i) if reverse else i
            rows = pl.ds(pl.multiple_of(ci * GLA_CHUNK, GLA_CHUNK), GLA_CHUNK)
            s_in = [s_ref[h] for h in range(GLA_H)]
            for h in range(GLA_H):
                for s in range(sb):
                    hist_ref[s, ci, h] = s_in[h][s * LANE:(s + 1) * LANE]
            q, k, v, af, w, ba = _gla_load(q_ref, k_ref, v_ref, af_ref, w_ref, ba_ref, sb, rows)
            outs, states = _gla_chunk(q, k, v, af, w, ba, s_in, reverse, sb)
            for h in range(GLA_H):
                for s in range(sb):
                    oh = outs[h][s * GLA_CHUNK:(s + 1) * GLA_CHUNK]
                    if has_add:
                        oh = oh + add_ref[s, rows, h * GLA_DV:(h + 1) * GLA_DV]
                    o_ref[s, rows, h * GLA_DV:(h + 1) * GLA_DV] = oh
                s_ref[h] = states[h]
            return carry

        lax.fori_loop(0, cpt, step, 0)

    col = lambda width, c0: pl.BlockSpec((sb, tile, width), lambda b, j: (b, tj(j), c0 // width))
    in_specs = [col(256, C_Q), col(256, C_K), col(512, C_V), col(LANE, C_AFAB),
                pl.BlockSpec((LANE, 256), lambda b, j: (0, 0)), pl.BlockSpec((1, 256), lambda b, j: (0, 0))]
    p3 = p.reshape(nb, t, p.shape[1])
    args = [p3, p3, p3, p3, wa2p, ba]
    if has_add:
        in_specs.append(col(512, 0))
        args.append(o_add.reshape(nb, t, 512))
    o, hist = pl.pallas_call(
        body, name="gla_fwd_rev" if reverse else "gla_fwd", grid=(nb // sb, nt),
        in_specs=in_specs,
        out_specs=[col(512, 0), pl.BlockSpec((sb, cpt, GLA_H, LANE, LANE), lambda b, j: (b, tj(j), 0, 0, 0))],
        out_shape=[jax.ShapeDtypeStruct((nb, t, 512), F32),
                   jax.ShapeDtypeStruct((nb, t // GLA_CHUNK, GLA_H, LANE, LANE), F32)],
        scratch_shapes=[pltpu.VMEM((GLA_H, sb * LANE, LANE), F32)],
        compiler_params=_cparams(("arbitrary", "arbitrary")),
    )(*args)
    return o.reshape(nb * t, 512), hist


def gla_bwd(p, wa2p, ba, hist, do, dprev, nb, t, reverse):
    tile, cpt, nt, sb, tj_f = _gla_specs(nb, t, reverse)
    tj = lambda j: tj_f(nt - 1 - j)
    has_prev = dprev is not None

    def body(*refs):
        if has_prev:
            q_ref, k_ref, v_ref, af_ref, w_ref, ba_ref, hist_ref, do_ref, prev_ref, dqkv_ref, dw_ref, dba_ref, ds_ref = refs
        else:
            q_ref, k_ref, v_ref, af_ref, w_ref, ba_ref, hist_ref, do_ref, dqkv_ref, dw_ref, dba_ref, ds_ref = refs

        @pl.when((pl.program_id(0) == 0) & (pl.program_id(1) == 0))
        def _():
            dw_ref[...] = jnp.zeros_like(dw_ref)
            dba_ref[...] = jnp.zeros_like(dba_ref)

        @pl.when(pl.program_id(1) == 0)
        def _():
            ds_ref[...] = jnp.zeros_like(ds_ref)

        def step(i, carry):
            ci = i if reverse else (cpt - 1 - i)
            rows = pl.ds(pl.multiple_of(ci * GLA_CHUNK, GLA_CHUNK), GLA_CHUNK)
            fn = functools.partial(_gla_chunk, reverse=reverse, sb=sb)
            seqs = lambda get: jnp.concatenate([get(s) for s in range(sb)], axis=0)
            s_in = [seqs(lambda s: hist_ref[s, ci, h]) for h in range(GLA_H)]
            q, k, v, af, w, ba = _gla_load(q_ref, k_ref, v_ref, af_ref, w_ref, ba_ref, sb, rows)
            _, vjp = jax.vjp(fn, q, k, v, af, w, ba, s_in)
            d_o = [seqs(lambda s: do_ref[s, rows, h * GLA_DV:(h + 1) * GLA_DV]) for h in range(GLA_H)]
            d_s = [ds_ref[h] for h in range(GLA_H)]
            dq, dk, dv, daf, dw, dba, ds_in = vjp((d_o, d_s))
            pieces = [(pr * LANE, dq[pr]) for pr in range(2)] + [(256 + pr * LANE, dk[pr]) for pr in range(2)]
            pieces += [(512 + h * GLA_DV, dv[h]) for h in range(GLA_H)] + [(1024, daf)]
            for c0, val in pieces:
                for s in range(sb):
                    part = val[s * GLA_CHUNK:(s + 1) * GLA_CHUNK]
                    if has_prev:
                        part = part + prev_ref[s, rows, c0:c0 + LANE]
                    dqkv_ref[s, rows, c0:c0 + LANE] = part
            for pr in range(2):
                dw_ref[:, pr * LANE:(pr + 1) * LANE] += dw[pr]
                dba_ref[:, pr * LANE:(pr + 1) * LANE] += dba[pr]
            for h in range(GLA_H):
                ds_ref[h] = ds_in[h]
            return carry

        lax.fori_loop(0, cpt, step, 0)

    col = lambda width, c0: pl.BlockSpec((sb, tile, width), lambda b, j: (b, tj(j), c0 // width))
    in_specs = [col(256, C_Q), col(256, C_K), col(512, C_V), col(LANE, C_AFAB),
                pl.BlockSpec((LANE, 256), lambda b, j: (0, 0)), pl.BlockSpec((1, 256), lambda b, j: (0, 0)),
                pl.BlockSpec((sb, cpt, GLA_H, LANE, LANE), lambda b, j: (b, tj(j), 0, 0, 0)), col(512, 0)]
    p3 = p.reshape(nb, t, p.shape[1])
    args = [p3, p3, p3, p3, wa2p, ba, hist, do.reshape(nb, t, 512)]
    if has_prev:
        in_specs.append(col(1152, 0))
        args.append(dprev.reshape(nb, t, 1152))
    dqkv, dw, dba = pl.pallas_call(
        body, name="gla_bwd_rev" if reverse else "gla_bwd", grid=(nb // sb, nt),
        in_specs=in_specs,
        out_specs=[col(1152, 0), pl.BlockSpec((LANE, 256), lambda b, j: (0, 0)), pl.BlockSpec((1, 256), lambda b, j: (0, 0))],
        out_shape=[jax.ShapeDtypeStruct((nb, t, 1152), F32), jax.ShapeDtypeStruct((LANE, 256), F32),
                   jax.ShapeDtypeStruct((1, 256), F32)],
        scratch_shapes=[pltpu.VMEM((GLA_H, sb * LANE, LANE), F32)],
        compiler_params=_cparams(("arbitrary", "arbitrary")),
    )(*args)
    return dqkv.reshape(nb * t, 1152), dw, dba


SCAN_TB = 8
RW_VH = RW_N // 2


def _bwd_lanes():
    lane = lax.broadcasted_iota(jnp.int32, (1, LANE), 1)
    return ((lane // (LANE // 4)) % 2) == 1


def _comm_specs(comm):
    anyspec = pl.BlockSpec(memory_space=pl.ANY)
    n = len(comm)
    shapes = [jax.ShapeDtypeStruct((N_DEV,) + (a.shape[1:] if sc else a.shape), a.dtype) for a, sc in comm]
    sems = [pltpu.SemaphoreType.DMA((n, N_DEV - 1)), pltpu.SemaphoreType.DMA((n, N_DEV - 1)), pltpu.SemaphoreType.DMA((n,))] if n else []
    return [a for a, _ in comm], [anyspec] * n, shapes, sems


def rwkv_scan_fwd(r, w, k, a, b, v, comm=()):
    t = r.shape[0]
    nt = t // SCAN_TB
    nc = len(comm)
    flags = [sc for _, sc in comm]

    def body(*refs):
        (rf, rm, kf, km, af, am, bf, bm, wf_ref, wm_ref, vf, vm), refs = refs[:12], refs[12:]
        c_in, refs = refs[:nc], refs[nc:]
        (yf_ref, ym_ref, hist_ref, sa_ref), refs = refs[:4], refs[4:]
        c_out, refs = refs[:nc], refs[nc:]
        s_ref, sems = refs[0], refs[1:]
        i = pl.program_id(0)
        if nc:
            start, wait = _exchange_plan(flags, c_in, c_out, *sems)

        @pl.when(i == 0)
        def _():
            s_ref[...] = jnp.zeros_like(s_ref)
            if nc:
                start()

        bwd = _bwd_lanes()

        for tt in range(SCAN_TB):
            mt = SCAN_TB - 1 - tt
            pick = lambda f_ref, m_ref: jnp.where(bwd, m_ref[mt], f_ref[tt])
            rt, kt, at, bt, wt = pick(rf, rm), pick(kf, km), pick(af, am), pick(bf, bm), pick(wf_ref, wm_ref)
            for vi in range(RW_VH):
                sv = s_ref[vi] if tt == 0 else hist_ref[tt - 1, vi]
                sa = jnp.sum(sv * at, axis=0, keepdims=True)
                v_row = jnp.where(bwd, vm[mt, vi:vi + 1, :], vf[tt, vi:vi + 1, :])
                sn = sv * wt + sa * bt + v_row * kt
                hist_ref[tt, vi] = sn
                y_row = jnp.sum(sn * rt, axis=0, keepdims=True)
                yf_ref[tt, vi:vi + 1, :] = y_row
                ym_ref[mt, vi:vi + 1, :] = y_row
                sa_ref[tt, vi:vi + 1, :] = sa
        s_ref[...] = hist_ref[SCAN_TB - 1]

        if nc:
            @pl.when(i == nt - 1)
            def _():
                wait()

    fwd_map, mir_map = (lambda i: (i, 0, 0)), (lambda i: (nt - 1 - i, 0, 0))
    kf_spec, km_spec = pl.BlockSpec((SCAN_TB, RW_N, LANE), fwd_map), pl.BlockSpec((SCAN_TB, RW_N, LANE), mir_map)
    vf_spec, vm_spec = pl.BlockSpec((SCAN_TB, RW_VH, LANE), fwd_map), pl.BlockSpec((SCAN_TB, RW_VH, LANE), mir_map)
    c_args, c_specs, c_shapes, c_sems = _comm_specs(comm)
    vshape = jax.ShapeDtypeStruct((t, RW_VH, LANE), F32)
    return pl.pallas_call(
        body, name="rwkv_scan_fwd", grid=(nt,),
        in_specs=[kf_spec, km_spec] * 5 + [vf_spec, vm_spec] + c_specs,
        out_specs=[vf_spec, vm_spec, pl.BlockSpec((SCAN_TB, RW_VH, RW_N, LANE), lambda i: (i, 0, 0, 0)), vf_spec] + c_specs,
        out_shape=[vshape, vshape, jax.ShapeDtypeStruct((t, RW_VH, RW_N, LANE), F32), vshape] + c_shapes,
        scratch_shapes=[pltpu.VMEM((RW_VH, RW_N, LANE), F32)] + c_sems,
        compiler_params=_cparams(("arbitrary",)),
    )(r, r, k, k, a, a, b, b, w, w, v, v, *c_args)


def rwkv_scan_bwd(r, w, k, a, b, v, hist, sa, dy, comm=()):
    t = r.shape[0]
    nt = t // SCAN_TB
    nc = len(comm)
    flags = [sc for _, sc in comm]

    def body(*refs):
        (rf, rm, kf, km, af, am, bf, bm, wf_ref, wm_ref, vf, vm, hist_ref, prev_ref, sa_ref, dyf, dym), refs = refs[:17], refs[17:]
        c_in, refs = refs[:nc], refs[nc:]
        k_outs, (dvf_ref, dvm_ref), refs = refs[:4], refs[4:6], refs[6:]
        c_out, refs = refs[:nc], refs[nc:]
        ds_ref, sems = refs[0], refs[1:]
        i = pl.program_id(0)
        if nc:
            start, wait = _exchange_plan(flags, c_in, c_out, *sems)

        @pl.when(i == 0)
        def _():
            ds_ref[...] = jnp.zeros_like(ds_ref)
            if nc:
                start()

        bwd = _bwd_lanes()
        group = lax.broadcasted_iota(jnp.int32, (1, LANE), 1) // RW_Q
        first_block = i == nt - 1

        for tt in range(SCAN_TB - 1, -1, -1):
            mt = SCAN_TB - 1 - tt
            pick = lambda f_ref, m_ref: jnp.where(bwd, m_ref[mt], f_ref[tt])
            rt, kt, at, bt, wt = pick(rf, rm), pick(kf, km), pick(af, am), pick(bf, bm), pick(wf_ref, wm_ref)
            zero = jnp.zeros((RW_N, LANE), F32)
            dr, dw, dk, da, db = zero, zero, zero, zero, zero
            for vi in range(RW_VH):
                sn = hist_ref[tt, vi]
                sv = hist_ref[tt - 1, vi] if tt > 0 else jnp.where(first_block, 0.0, prev_ref[0, vi])
                sa_row = sa_ref[tt, vi:vi + 1, :]
                v_row = jnp.where(bwd, vm[mt, vi:vi + 1, :], vf[tt, vi:vi + 1, :])
                dy_row = jnp.where(bwd, dym[mt, vi:vi + 1, :], dyf[tt, vi:vi + 1, :])
                dsv = ds_ref[vi] + dy_row * rt
                dr = dr + sn * dy_row
                dsa = jnp.sum(dsv * bt, axis=0, keepdims=True)
                dw = dw + sv * dsv
                db = db + dsv * sa_row
                dk = dk + dsv * v_row
                dv_row = jnp.sum(dsv * kt, axis=0, keepdims=True)
                dvf_ref[tt, vi:vi + 1, :] = dv_row
                dvm_ref[mt, vi:vi + 1, :] = dv_row
                da = da + sv * dsa
                ds_ref[vi] = dsv * wt + dsa * at
            dr, dw, dk, da, db = [val + pltpu.roll(val, LANE // 2, 1) for val in (dr, dw, dk, da, db)]
            up, down = (lambda val: pltpu.roll(val, RW_Q, 1)), (lambda val: pltpu.roll(val, LANE - RW_Q, 1))
            packed_f = jnp.where(group == 0, dr, jnp.where(group == 1, up(dk), jnp.where(group == 2, da, up(db))))
            packed_m = jnp.where(group == 0, down(dr), jnp.where(group == 1, dk, jnp.where(group == 2, down(da), db)))
            k_outs[0][tt] = packed_f
            k_outs[1][mt] = packed_m
            k_outs[2][tt] = dw
            k_outs[3][mt] = dw

        if nc:
            @pl.when(i == nt - 1)
            def _():
                wait()

    fwd_map, mir_map = (lambda i: (nt - 1 - i, 0, 0)), (lambda i: (i, 0, 0))
    kf_spec, km_spec = pl.BlockSpec((SCAN_TB, RW_N, LANE), fwd_map), pl.BlockSpec((SCAN_TB, RW_N, LANE), mir_map)
    vf_spec, vm_spec = pl.BlockSpec((SCAN_TB, RW_VH, LANE), fwd_map), pl.BlockSpec((SCAN_TB, RW_VH, LANE), mir_map)
    prev_spec = pl.BlockSpec((1, RW_VH, RW_N, LANE), lambda i: (jnp.maximum((nt - 1 - i) * SCAN_TB - 1, 0), 0, 0, 0))
    c_args, c_specs, c_shapes, c_sems = _comm_specs(comm)
    kshape, vshape = jax.ShapeDtypeStruct((t, RW_N, LANE), F32), jax.ShapeDtypeStruct((t, RW_VH, LANE), F32)
    return pl.pallas_call(
        body, name="rwkv_scan_bwd", grid=(nt,),
        in_specs=[kf_spec, km_spec] * 5 + [vf_spec, vm_spec,
                                           pl.BlockSpec((SCAN_TB, RW_VH, RW_N, LANE), lambda i: (nt - 1 - i, 0, 0, 0)),
                                           prev_spec, vf_spec, vf_spec, vm_spec] + c_specs,
        out_specs=[kf_spec, km_spec] * 2 + [vf_spec, vm_spec] + c_specs,
        out_shape=[kshape] * 4 + [vshape] * 2 + c_shapes,
        scratch_shapes=[pltpu.VMEM((RW_VH, RW_N, LANE), F32)] + c_sems,
        compiler_params=_cparams(("arbitrary",)),
    )(r, r, k, k, a, a, b, b, w, w, v, v, hist, hist, sa, dy, dy, *c_args)


RELAYOUT_TB = 128
RW_Q = LANE // 4


def to_scan(name, x, cb, nb, t, value, x_bwd=None):
    tb = min(RELAYOUT_TB, t)
    rows_out = RW_VH if value else RW_N
    ins = [x] if x_bwd is None else [x, x_bwd]

    def body(*refs):
        x_refs, o_ref, scrs = refs[:len(ins)], refs[len(ins)], refs[len(ins) + 1:]
        for x_ref, scr in zip(x_refs, scrs):
            for b in range(nb):
                scr[b * RW_H:(b + 1) * RW_H] = x_ref[b].T.reshape(RW_H, RW_N, tb)
        for j in range(rows_out):
            lo = scrs[0][:, j, :]
            if value:
                hi = scrs[0][:, j + RW_VH, :]
                blk = [lo, lo, hi, hi]
            else:
                other = lo if x_bwd is None else scrs[1][:, j, :]
                blk = [lo, other, lo, other]
            o_ref[:, j, :] = jnp.concatenate(blk, axis=0).T

    return pl.pallas_call(
        body, name=name, grid=(t // tb,),
        in_specs=[pl.BlockSpec((nb, tb, RW_W), lambda i: (0, i, cb))] + [pl.BlockSpec((nb, tb, RW_W), lambda i: (0, i, 0))] * (len(ins) - 1),
        out_specs=pl.BlockSpec((tb, rows_out, LANE), lambda i: (i, 0, 0)),
        out_shape=jax.ShapeDtypeStruct((t, rows_out, LANE), F32),
        scratch_shapes=[pltpu.VMEM((nb * RW_H, RW_N, tb), F32)] * len(ins),
        compiler_params=_cparams(("arbitrary",)),
    )(*[a.reshape(nb, t, a.shape[1]) for a in ins])


def from_scan(name, xf, xm, nb, t, value, groups=None):
    tb = min(RELAYOUT_TB, t)
    rows_in = RW_VH if value else RW_N
    n_out = 1 if value else (4 if groups is None else 2)
    grp = lambda a, g: a[g * RW_Q:(g + 1) * RW_Q]

    def body(f_ref, m_ref, *rest):
        outs, scrs = rest[:n_out], rest[n_out:]
        lane_group = lax.broadcasted_iota(jnp.int32, (1, LANE), 1) // RW_Q
        for j in range(rows_in):
            f, m = f_ref[:, j, :], m_ref[:, j, :]
            if value:
                c = jnp.where(_bwd_lanes(), m, f).T
                scrs[0][:, j, :] = grp(c, 0) + grp(c, 1)
                scrs[0][:, j + RW_VH, :] = grp(c, 2) + grp(c, 3)
            elif groups is None:
                c = (f + m).T
                for q, scr in enumerate(scrs):
                    scr[:, j, :] = grp(c, q)
            else:
                c = jnp.where(lane_group == groups[1], m, f).T
                scrs[0][:, j, :] = grp(c, groups[0])
                scrs[1][:, j, :] = grp(c, groups[1])
        for o_ref, scr in zip(outs, scrs):
            for b in range(nb):
                o_ref[b] = scr[b * RW_H:(b + 1) * RW_H].reshape(RW_W, tb).T

    res = pl.pallas_call(
        body, name=name, grid=(t // tb,),
        in_specs=[pl.BlockSpec((tb, rows_in, LANE), lambda i: (i, 0, 0))] * 2,
        out_specs=[pl.BlockSpec((nb, tb, RW_W), lambda i: (0, i, 0))] * n_out,
        out_shape=[jax.ShapeDtypeStruct((nb, t, RW_W), F32)] * n_out,
        scratch_shapes=[pltpu.VMEM((nb * RW_H, RW_N, tb), F32)] * n_out,
        compiler_params=_cparams(("arbitrary",)),
    )(xf, xm)
    return [r.reshape(nb * t, RW_W) for r in res]


def f_norm(rows, params):
    (x,), (g,) = rows, params
    return [_rmsnorm(x, g)]


def f_rwkv_pre(rows, params):
    k, wlal, gl = rows
    w0f, w2f, w0b, w2b, a0, a2, g2, k_k, k_a = params
    tw = jnp.tanh(wlal)

    def decay(w0, w2):
        return jnp.exp(-jnp.exp(-_softplus(-(w0 + mm(tw, w2))) - 0.5))

    lr = _sigmoid(a0 + mm(wlal, a2))
    gate = mm(_sigmoid(gl), g2)
    kk = k * k_k
    kk = kk / jnp.maximum(jnp.sqrt(_segment_sum(kk * kk, RW_N)), 1e-12)
    kp = k * (1.0 + (lr - 1.0) * k_a)
    return [decay(w0f, w2f), decay(w0b, w2b), kp, -kk, kk * lr, gate]


def f_branch_post(rows, params):
    o, og, y, r, kp, v, g = rows
    gla_g, ln_w, ln_b, r_k = params
    on = o * lax.rsqrt(_segment_sum(o * o, GLA_DV) * (1.0 / GLA_DV) + HEAD_NORM_EPS)
    oa = on * gla_g * _silu(og)
    mu = _segment_sum(y, RW_N) * (1.0 / RW_N)
    yc = y - mu
    var = _segment_sum(yc * yc, RW_N) * (1.0 / RW_N)
    yn = yc * lax.rsqrt(var + RW_GN_EPS) * ln_w + ln_b
    bonus = _segment_sum(r * kp * r_k, RW_N) * v
    return [oa, (yn + bonus) * g]


def f_merge(rows, params):
    ga, gb, ya, yb = rows
    return [_sigmoid(ga) * ya + _sigmoid(gb) * yb]


def f_norm2(rows, params):
    (x, mo), (g,) = rows, params
    x1 = x + mo
    return [x1, _rmsnorm(x1, g)]


def loss_head(x1, ffo, tgt, gf, tm):
    n = x1.shape[0]

    def body(x1_ref, f_ref, t_ref, g_ref, loss_ref, dx_ref, dg_ref):
        @pl.when(pl.program_id(0) == 0)
        def _():
            loss_ref[...] = jnp.zeros_like(loss_ref)
            dg_ref[...] = jnp.zeros_like(dg_ref)

        tgt_v = t_ref[...]

        def f(x2, g):
            err = _rmsnorm(x2, g) - tgt_v
            return jnp.sum(jnp.sum(err * err, axis=-1, keepdims=True), axis=0, keepdims=True) * (0.5 / D)

        val, vjp = jax.vjp(f, x1_ref[...] + f_ref[...], g_ref[...])
        dx, dg = vjp(jnp.ones((1, 1), F32))
        loss_ref[...] += val
        dx_ref[...] = dx
        dg_ref[...] += dg

    return pl.pallas_call(
        body, name="loss_head", grid=(n // tm,),
        in_specs=[_row_spec(tm, D, 0)] * 3 + [_full_spec((1, D))],
        out_specs=[_full_spec((1, 1)), _row_spec(tm, D, 0), _full_spec((1, D))],
        out_shape=[jax.ShapeDtypeStruct((1, 1), F32), jax.ShapeDtypeStruct((n, D), F32), jax.ShapeDtypeStruct((1, D), F32)],
        compiler_params=_cparams(("arbitrary",)),
    )(x1, ffo, tgt, gf)


def _pad_cols(a, width):
    return jnp.pad(a, ((0, 0), (0, width - a.shape[1])))


def w_in_to_padded(w):
    return _pad_cols(jnp.concatenate([w[:, 3360:5408], w[:, 0:1536], w[:, 1568:3360], w[:, 1536:1568]], axis=1), NP)


def w_in_from_padded(wp):
    return jnp.concatenate([wp[:, 2048:3584], wp[:, 5376:5408], wp[:, 3584:5376], wp[:, 0:2048]], axis=1)


def ff_interleave(a):
    r = a.shape[0]
    halves = jnp.stack([_pad_cols(a[:, :D_FF], FFP), _pad_cols(a[:, D_FF:], FFP)], axis=1)
    return halves.reshape(r, 2, FFP // LANE, LANE).transpose(0, 2, 1, 3).reshape(r, 2 * FFP)


def ff_deinterleave(a):
    r = a.shape[0]
    halves = a.reshape(r, FFP // LANE, 2, LANE).transpose(0, 2, 1, 3).reshape(r, 2, FFP)
    return halves[:, :, :D_FF].reshape(r, 2 * D_FF)


def _rows_into(w, rows, off):
    return jnp.zeros((rows, w.shape[1]), w.dtype).at[off:off + w.shape[0]].set(w)


LATE = ("gla_proj", "rwkv_proj", "w_out", "ffn_up", "ffn_conv_w", "ffn_down")


def local_step(x, tgt, w, nb, t, late_blocks=None):
    n = nb * t
    tm = min(n, 1024)
    tkt = min(n, 2048)
    tr = min(n, 256)
    vec = lambda a: a.reshape(1, -1)
    w = dict(w)

    w_in_p = w_in_to_padded(w["w_in"])
    wa2_f, wa2_b = _rows_into(w["gla_wa2_f"], LANE, 0), _rows_into(w["gla_wa2_b"], LANE, GLA_RANK)
    w2f, w2b = _rows_into(w["rwkv_w2_f"], LANE, 0), _rows_into(w["rwkv_w2_b"], LANE, 0)
    a2 = _rows_into(w["rwkv_a2"], LANE, 64)
    g1, g2n, gf = vec(w["norm1_g"]), vec(w["norm2_g"]), vec(w["norm_f_g"])
    mu_prev, mu_next = vec(w["rwkv_mu_prev"]), vec(w["rwkv_mu_next"])
    pre_params = [vec(w["rwkv_w0_f"]), w2f, vec(w["rwkv_w0_b"]), w2b, vec(w["rwkv_a0"]), a2, w["rwkv_g2"],
                  vec(w["rwkv_k_k"]), vec(w["rwkv_k_a"])]
    post_params = [vec(w["gla_norm_g"]), vec(w["rwkv_ln_w"]), vec(w["rwkv_ln_b"]), vec(w["rwkv_r_k"])]
    ba_f, ba_b = vec(w["gla_ba_f"]), vec(w["gla_ba_b"])

    (h1,) = rowwise_fwd("norm1_fwd", f_norm, [(x, D, 0)], [g1], [(D, MXU_DTYPE)], tr)
    p = matmul("proj_in", h1, w_in_p, "nn", F32, tm, FFP // 2, D)
    s = shift_fwd(p, mu_prev, mu_next, nb, t)
    pre_rows = [(s, 512, 1), (s, LANE, 1536 // LANE), (s, LANE, 1664 // LANE)]
    wf, wb, kp, a_s, b_s, g = rowwise_fwd("rwkv_pre_fwd", f_rwkv_pre, pre_rows, pre_params, [(RW_W, F32)] * 6, tr)
    sc = [to_scan("to_scan_r", s, 0, nb, t, False), to_scan("to_scan_w", wf, 0, nb, t, False, x_bwd=wb),
          to_scan("to_scan_k", kp, 0, nb, t, False), to_scan("to_scan_a", a_s, 0, nb, t, False),
          to_scan("to_scan_b", b_s, 0, nb, t, False), to_scan("to_scan_v", s, 2, nb, t, True)]
    comm = [] if late_blocks is None else [(late_blocks[k], False) for k in LATE]
    y_scf, y_scm, hist_rw, sa_sc, *gathered = rwkv_scan_fwd(*sc, comm=comm)
    for k, g_k in zip(LATE, gathered):
        w[k] = _gathered_to_full(g_k, SHARDED[k])
    ffn_up_p = ff_interleave(w["ffn_up"])
    conv_w_p, conv_b_p = ff_interleave(w["ffn_conv_w"]), ff_interleave(vec(w["ffn_conv_b"]))
    ffn_down_p = jnp.pad(w["ffn_down"], ((0, FFP - D_FF), (0, 0)))
    (y,) = from_scan("from_scan_y", y_scf, y_scm, nb, t, True)
    o_f, hist_f = gla_fwd(p, wa2_f, ba_f, None, nb, t, False)
    o, hist_b = gla_fwd(p, wa2_b, ba_b, o_f, nb, t, True)
    post_rows = [(o, 512, 0), (p, 512, C_OG // 512), (y, 512, 0), (s, 512, 0), (kp, 512, 0), (s, 512, 2), (g, 512, 0)]
    oa, ob = rowwise_fwd("branch_post_fwd", f_branch_post, post_rows, post_params, [(512, MXU_DTYPE)] * 2, tr)
    ya = matmul("gla_proj", oa, w["gla_proj"], "nn", F32, tm, 512, 512)
    yb = matmul("rwkv_proj", ob, w["rwkv_proj"], "nn", F32, tm, 512, 512)
    merge_rows = [(p, D, 0), (p, D, 1), (ya, D, 0), (yb, D, 0)]
    (merged,) = rowwise_fwd("merge_fwd", f_merge, merge_rows, [], [(D, MXU_DTYPE)], tr)
    mo = matmul("w_out", merged, w["w_out"], "nn", F32, tm, 512, D)
    x1, h2 = rowwise_fwd("norm2_fwd", f_norm2, [(x, D, 0), (mo, D, 0)], [g2n], [(D, F32), (D, MXU_DTYPE)], tr)
    u = matmul("ffn_up", h2, ffn_up_p, "nn", F32, tm, FFP // 2, D)
    z = conv_glu_fwd(u, conv_w_p, conv_b_p, nb, t)
    ffo = matmul("ffn_down", z, ffn_down_p, "nn", F32, tm, 512, FFP // 2)
    loss, dx2, dgf = loss_head(x1, ffo, tgt, gf, tr)

    dz = matmul("ffn_down_dx", dx2, ffn_down_p, "nt", F32, tm, FFP // 2, D)
    d_ffn_down_p = matmul("ffn_down_dw", z, dx2, "tn", F32, FFP // 2, 512, tkt)
    du, d_conv_w_p, d_conv_b_p = conv_glu_bwd(u, dz, conv_w_p, conv_b_p, nb, t)
    dh2 = matmul("ffn_up_dx", du, ffn_up_p, "nt", F32, tm, D, FFP // 2)
    d_ffn_up_p = matmul("ffn_up_dw", h2, du, "tn", F32, D, 512, tkt)
    (dx1,), (dg2,) = rowwise_bwd("norm2_bwd", f_norm2, [(x, D, 0), (mo, D, 0)], [g2n],
                                 [[(dx2, D, 0)], [(dh2, D, 0)]], tr, grad_rows=[1])
    dmerged = matmul("w_out_dx", dx1, w["w_out"], "nt", F32, tm, D, D)
    d_w_out = matmul("w_out_dw", merged, dx1, "tn", F32, D, 512, tkt)
    (dga, dgb, dya, dyb), _ = rowwise_bwd("merge_bwd", f_merge, merge_rows, [], [[(dmerged, D, 0)]], tr)
    d_oa = matmul("gla_proj_dx", dya, w["gla_proj"], "nt", F32, tm, 512, D)
    d_gla_proj = matmul("gla_proj_dw", oa, dya, "tn", F32, 512, 512, tkt)
    d_ob = matmul("rwkv_proj_dx", dyb, w["rwkv_proj"], "nt", F32, tm, 512, D)
    d_rwkv_proj = matmul("rwkv_proj_dw", ob, dyb, "tn", F32, 512, 512, tkt)
    (d_o, d_og, d_y, d_r_post, d_kp_post, d_v_post, d_g), d_post = rowwise_bwd(
        "branch_post_bwd", f_branch_post, post_rows, post_params, [[(d_oa, 512, 0)], [(d_ob, 512, 0)]], tr)
    late_grads = {"gla_proj": d_gla_proj, "rwkv_proj": d_rwkv_proj, "w_out": d_w_out, "ffn_up": ff_deinterleave(d_ffn_up_p),
                  "ffn_conv_w": ff_deinterleave(d_conv_w_p), "ffn_down": d_ffn_down_p[0:D_FF]}
    comm = [] if late_blocks is None else [(_full_to_slices(late_grads[k], SHARDED[k]), True) for k in LATE]
    dsc = rwkv_scan_bwd(*sc, hist_rw, sa_sc, to_scan("to_scan_dy", d_y, 0, nb, t, True), comm=comm)
    received = dict(zip(LATE, dsc[6:]))
    d_r_scan, d_kp_scan, d_a_scan, d_b_scan = from_scan("from_scan_rkab", dsc[0], dsc[1], nb, t, False)
    d_wf, d_wb = from_scan("from_scan_w", dsc[2], dsc[3], nb, t, False, groups=(0, 1))
    (d_v_scan,) = from_scan("from_scan_dv", dsc[4], dsc[5], nb, t, True)
    (d_k, d_wlal, d_gl), d_pre = rowwise_bwd(
        "rwkv_pre_bwd", f_rwkv_pre, pre_rows, pre_params,
        [[(d_wf, 512, 0)], [(d_wb, 512, 0)], [(d_kp_scan, 512, 0), (d_kp_post, 512, 0)],
         [(d_a_scan, 512, 0)], [(d_b_scan, 512, 0)], [(d_g, 512, 0)]], tr)
    ds = jnp.concatenate([d_r_scan + d_r_post, d_k, d_v_scan + d_v_post, d_wlal, d_gl], axis=1)
    dp_rw, d_mu_prev, d_mu_next = shift_bwd(p, ds, mu_prev, mu_next, nb, t)
    dqkv_f, d_wa2_f, d_ba_f = gla_bwd(p, wa2_f, ba_f, hist_f, d_o, None, nb, t, False)
    dqkv, d_wa2_b, d_ba_b = gla_bwd(p, wa2_b, ba_b, hist_b, d_o, dqkv_f, nb, t, True)
    dp = jnp.concatenate([dga, dgb, dqkv[:, 0:1024], d_og, dp_rw, dqkv[:, 1024:1152],
                          jnp.zeros((n, NP - C_AFAB - LANE), F32)], axis=1).astype(MXU_DTYPE)
    d_w_in_p = matmul("proj_in_dw", h1, dp, "tn", F32, D, 512, tkt)
    grads = {
        "w_in": w_in_from_padded(d_w_in_p),
        "gla_wa2_f": d_wa2_f[0:GLA_RANK], "gla_ba_f": d_ba_f, "gla_wa2_b": d_wa2_b[GLA_RANK:2 * GLA_RANK], "gla_ba_b": d_ba_b,
        "gla_norm_g": d_post[0], "rwkv_mu_prev": d_mu_prev, "rwkv_mu_next": d_mu_next,
        "rwkv_w0_f": d_pre[0], "rwkv_w2_f": d_pre[1][0:64], "rwkv_w0_b": d_pre[2], "rwkv_w2_b": d_pre[3][0:64],
        "rwkv_a0": d_pre[4], "rwkv_a2": d_pre[5][64:128], "rwkv_g2": d_pre[6], "rwkv_k_k": d_pre[7], "rwkv_k_a": d_pre[8],
        "rwkv_r_k": d_post[3], "rwkv_ln_w": d_post[1], "rwkv_ln_b": d_post[2],
        "norm2_g": dg2, "ffn_conv_b": ff_deinterleave(d_conv_b_p), "norm_f_g": dgf, **late_grads,
    }
    early = [k for k in SHARDED if k not in LATE]
    payload = lambda k: _full_to_slices(grads[k], SHARDED[k]).astype(MXU_DTYPE if k == "w_in" else F32)
    comm = [] if late_blocks is None else [(payload(k), True) for k in early]
    dh1, *got = matmul("proj_in_dx", dp, w_in_p, "nt", F32, tm, D, FFP // 2, comm=comm) if comm else \
        [matmul("proj_in_dx", dp, w_in_p, "nt", F32, tm, D, FFP // 2)]
    received.update(zip(early, got))
    (grad_x,), (grads["norm1_g"],) = rowwise_bwd("norm1_bwd", f_norm, [(x, D, 0)], [g1], [[(dh1, D, 0)]], tr,
                                                 adds=[(0, (dx1, D, 0))])
    return loss, grad_x, grads, received


MESH = pl.DeviceIdType.MESH


def remote_exchange(name, items):
    n = len(items)

    def body(*refs):
        start, wait = _exchange_plan([sc for _, sc in items], refs[:n], refs[n:2 * n], *refs[2 * n:])
        start()
        wait()

    args, specs, shapes, sems = _comm_specs(items)
    return pl.pallas_call(body, name=name, in_specs=specs, out_specs=specs, out_shape=shapes, scratch_shapes=sems)(*args)


def gather_two_level(name, blocks):
    n = len(blocks)

    def body(*refs):
        in_refs, out_refs = refs[:n], refs[n:2 * n]
        send_sems, recv_sems, local_sems = refs[2 * n:]
        x, y, c = lax.axis_index("x"), lax.axis_index("y"), lax.axis_index("c")
        me, sibling = (x, y, c), (x, y, 1 - c)
        chips = [(1 - x, y), (x, 1 - y), (1 - x, 1 - y)]

        def copy(i, k, block, to, src=None):
            rows = out_refs[i].at[4 * block[0] + 2 * block[1] + block[2]]
            return pltpu.make_async_remote_copy(src_ref=rows if src is None else src, dst_ref=rows, send_sem=send_sems.at[i, k],
                                                recv_sem=recv_sems.at[i, k], device_id=to, device_id_type=MESH)

        own = [pltpu.make_async_copy(in_refs[i], out_refs[i].at[4 * x + 2 * y + c], local_sems.at[i]) for i in range(n)]
        first = [copy(i, 0, me, sibling, src=in_refs[i]) for i in range(n)]
        first += [copy(i, 1 + j, me, (*chip, c), src=in_refs[i]) for j, chip in enumerate(chips) for i in range(n)]
        for cp in own + first:
            cp.start()
        passed = []
        for j, chip in enumerate(chips):
            for i in range(n):
                copy(i, 1 + j, (*chip, c), me).wait_recv()
                onward = copy(i, 4 + j, (*chip, c), sibling)
                onward.start()
                passed.append(onward)
        for i in range(n):
            copy(i, 0, sibling, me).wait_recv()
        for j, chip in enumerate(chips):
            for i in range(n):
                copy(i, 4 + j, (*chip, 1 - c), me).wait_recv()
        for cp in first + passed:
            cp.wait_send()
        for cp in own:
            cp.wait()

    args, specs, shapes, sems = _comm_specs([(b, False) for b in blocks])
    return pl.pallas_call(body, name=name, in_specs=specs, out_specs=specs, out_shape=shapes, scratch_shapes=sems)(*args)


def _exchange_plan(flags, in_refs, out_refs, send_sems, recv_sems, local_sems):
    x, y, c = lax.axis_index("x"), lax.axis_index("y"), lax.axis_index("c")
    me = 4 * x + 2 * y + c

    def peer(k):
        px = 1 - x if (k >> 2) & 1 else x
        py = 1 - y if (k >> 1) & 1 else y
        pc = 1 - c if k & 1 else c
        return (px, py, pc), 4 * px + 2 * py + pc

    def copies(with_arrivals):
        own, sends, recvs = [], [], []
        for i, scatter in enumerate(flags):
            src = in_refs[i].at[me] if scatter else in_refs[i]
            own.append(pltpu.make_async_copy(src, out_refs[i].at[me], local_sems.at[i]))
        for k in range(1, N_DEV):
            dev, slot = peer(k)
            for i, scatter in enumerate(flags):
                src = in_refs[i].at[slot] if scatter else in_refs[i]
                pair = dict(send_sem=send_sems.at[i, k - 1], recv_sem=recv_sems.at[i, k - 1], device_id=dev, device_id_type=MESH)
                sends.append(pltpu.make_async_remote_copy(src_ref=src, dst_ref=out_refs[i].at[me], **pair))
                if with_arrivals:
                    recvs.append(pltpu.make_async_remote_copy(src_ref=out_refs[i].at[slot], dst_ref=out_refs[i].at[slot], **pair))
        return own, sends, recvs

    def start():
        own, sends, _ = copies(False)
        for cp in own + sends:
            cp.start()

    def wait():
        own, sends, recvs = copies(True)
        for send, recv in zip(sends, recvs):
            recv.wait_recv()
            send.wait_send()
        for cp in own:
            cp.wait()

    return start, wait


def _adam_tiles(r, c):
    tc = 256 if (c % 256 == 0 and r * c > 128 * 1024) else c
    tr = 128 if (r % 128 == 0 and r > 128) else r
    return tr, tc


def adamw_reduce(name, parts, w, m, v):
    r, c = w.shape
    tr, tc = _adam_tiles(r, c)

    def body(p_ref, w_ref, m_ref, v_ref, g_ref, d_ref, nm_ref, nv_ref):
        g = p_ref[0].astype(F32)
        for d in range(1, N_DEV):
            g = g + p_ref[d].astype(F32)
        nm = ADAM_B1 * m_ref[...] + (1.0 - ADAM_B1) * g
        nv = ADAM_B2 * v_ref[...] + (1.0 - ADAM_B2) * (g * g)
        m_hat = nm / (1.0 - ADAM_B1 ** ADAM_STEP)
        v_hat = nv / (1.0 - ADAM_B2 ** ADAM_STEP)
        g_ref[...] = g
        d_ref[...] = -ADAM_LR * (m_hat / (jnp.sqrt(v_hat) + ADAM_EPS) + ADAM_WD * w_ref[...])
        nm_ref[...] = nm
        nv_ref[...] = nv

    spec = pl.BlockSpec((tr, tc), lambda i, j: (i, j))
    return pl.pallas_call(
        body, name=name, grid=(r // tr, c // tc),
        in_specs=[pl.BlockSpec((N_DEV, tr, tc), lambda i, j: (0, i, j)), spec, spec, spec],
        out_specs=[spec] * 4, out_shape=[jax.ShapeDtypeStruct((r, c), F32)] * 4,
        compiler_params=_cparams(("arbitrary", "arbitrary")),
    )(parts, w, m, v)


SHARDED = {"w_in": 1, "gla_wa2_f": 1, "gla_wa2_b": 1, "gla_proj": 1, "rwkv_w2_f": 1, "rwkv_w2_b": 1, "rwkv_a2": 1,
           "rwkv_g2": 1, "rwkv_proj": 1, "w_out": 0, "ffn_up": 1, "ffn_conv_w": 1, "ffn_down": 0}
BF16_GATHER = ("w_in", "gla_proj", "rwkv_proj", "w_out", "ffn_up", "ffn_down")
REPLICATED = ("norm1_g", "gla_ba_f", "gla_ba_b", "gla_norm_g", "rwkv_mu_prev", "rwkv_mu_next", "rwkv_w0_f", "rwkv_w0_b",
              "rwkv_a0", "rwkv_k_k", "rwkv_k_a", "rwkv_r_k", "rwkv_ln_w", "rwkv_ln_b", "norm2_g", "ffn_conv_b", "norm_f_g")
WEIGHTS = ("norm1_g", "w_in", "gla_wa2_f", "gla_ba_f", "gla_wa2_b", "gla_ba_b", "gla_norm_g", "gla_proj", "rwkv_mu_prev",
           "rwkv_mu_next", "rwkv_w0_f", "rwkv_w2_f", "rwkv_w0_b", "rwkv_w2_b", "rwkv_a0", "rwkv_a2", "rwkv_g2", "rwkv_k_k",
           "rwkv_k_a", "rwkv_r_k", "rwkv_ln_w", "rwkv_ln_b", "rwkv_proj", "w_out", "norm2_g", "ffn_up", "ffn_conv_w",
           "ffn_conv_b", "ffn_down", "norm_f_g")


def _gathered_to_full(g, axis):
    if axis == 0:
        return g.reshape(N_DEV * g.shape[1], g.shape[2])
    return g.transpose(1, 0, 2).reshape(g.shape[1], N_DEV * g.shape[2])


def _full_to_slices(a, axis):
    if axis == 0:
        return a.reshape(N_DEV, a.shape[0] // N_DEV, a.shape[1])
    return a.reshape(a.shape[0], N_DEV, a.shape[1] // N_DEV).transpose(1, 0, 2)


def _pack_rows(size):
    return -(-size // (8 * LANE)) * 8


def _pack(d):
    parts = []
    for k in REPLICATED:
        rows = d[k].reshape(-1, LANE).astype(F32)
        parts.append(jnp.pad(rows, ((0, _pack_rows(rows.size) - rows.shape[0]), (0, 0))))
    return jnp.concatenate(parts, axis=0)


def _unpack(packed, shapes):
    out, pos = {}, 0
    for k in REPLICATED:
        size = int(np.prod(shapes[k]))
        out[k] = packed[pos:pos + size // LANE].reshape(shapes[k])
        pos += _pack_rows(size)
    return out


def kernel(x, norm1_g, w_in, gla_wa2_f, gla_ba_f, gla_wa2_b, gla_ba_b, gla_norm_g, gla_proj, rwkv_mu_prev, rwkv_mu_next, rwkv_w0_f, rwkv_w2_f, rwkv_w0_b, rwkv_w2_b, rwkv_a0, rwkv_a2, rwkv_g2, rwkv_k_k, rwkv_k_a, rwkv_r_k, rwkv_ln_w, rwkv_ln_b, rwkv_proj, w_out, norm2_g, ffn_up, ffn_conv_w, ffn_conv_b, ffn_down, norm_f_g, loss_target, m_norm1_g, m_w_in, m_gla_wa2_f, m_gla_ba_f, m_gla_wa2_b, m_gla_ba_b, m_gla_norm_g, m_gla_proj, m_rwkv_mu_prev, m_rwkv_mu_next, m_rwkv_w0_f, m_rwkv_w2_f, m_rwkv_w0_b, m_rwkv_w2_b, m_rwkv_a0, m_rwkv_a2, m_rwkv_g2, m_rwkv_k_k, m_rwkv_k_a, m_rwkv_r_k, m_rwkv_ln_w, m_rwkv_ln_b, m_rwkv_proj, m_w_out, m_norm2_g, m_ffn_up, m_ffn_conv_w, m_ffn_conv_b, m_ffn_down, m_norm_f_g, v_norm1_g, v_w_in, v_gla_wa2_f, v_gla_ba_f, v_gla_wa2_b, v_gla_ba_b, v_gla_norm_g, v_gla_proj, v_rwkv_mu_prev, v_rwkv_mu_next, v_rwkv_w0_f, v_rwkv_w2_f, v_rwkv_w0_b, v_rwkv_w2_b, v_rwkv_a0, v_rwkv_a2, v_rwkv_g2, v_rwkv_k_k, v_rwkv_k_a, v_rwkv_r_k, v_rwkv_ln_w, v_rwkv_ln_b, v_rwkv_proj, v_w_out, v_norm2_g, v_ffn_up, v_ffn_conv_w, v_ffn_conv_b, v_ffn_down, v_norm_f_g):
    args = locals()
    wts = {k: args[k] for k in WEIGHTS}
    mom = {k: args["m_" + k] for k in WEIGHTS}
    var = {k: args["v_" + k] for k in WEIGHTS}
    shapes = {k: wts[k].shape for k in WEIGHTS}
    nb, t = x.shape[0], x.shape[1]
    mat = lambda a: a.reshape(a.shape[-2], a.shape[-1])

    block = lambda k: mat(wts[k]).astype(MXU_DTYPE) if k in BF16_GATHER else mat(wts[k])
    early = [k for k in SHARDED if k not in LATE]
    gathered = gather_two_level("gather_weights", [block(k) for k in early])
    full = {k: _gathered_to_full(g, SHARDED[k]) for k, g in zip(early, gathered)}
    for k in REPLICATED:
        full[k] = wts[k].reshape(-1) if k in ("norm_f_g", "rwkv_r_k") else wts[k][0]

    loss, grad_x, grads, received = local_step(x.reshape(nb * t, D), loss_target.reshape(nb * t, D), full, nb, t,
                                               late_blocks={k: block(k) for k in LATE})

    (rep_parts,) = remote_exchange("exchange_replicated", [(_pack(grads), False)])

    res = {}
    for k in SHARDED:
        outs = adamw_reduce("adamw_" + k, received[k], mat(wts[k]), mat(mom[k]), mat(var[k]))
        res[k] = [o.reshape(shapes[k]) for o in outs]
    packed = adamw_reduce("adamw_replicated", rep_parts, _pack(wts), _pack(mom), _pack(var))
    unpacked = [_unpack(p, shapes) for p in packed]
    for k in REPLICATED:
        res[k] = [u[k] for u in unpacked]

    total = lax.psum(loss[0, 0], ("x", "y", "c"))
    out = [total, grad_x.reshape(x.shape)]
    for j in range(4):
        out += [res[k][j] for k in WEIGHTS]
    return tuple(out)
```

```python
import functools

import jax
import jax.numpy as jnp
import numpy as np
from jax import lax
from jax.experimental import pallas as pl
from jax.experimental.pallas import tpu as pltpu

F32 = jnp.float32
MXU_DTYPE = jnp.bfloat16

D = 1024
SEQ = 2048
GLA_H, GLA_DK, GLA_DV, GLA_CHUNK = 4, 64, 128, 64
GLA_RANK = 16
GLA_LOGIT_NORM = 16.0
RW_H, RW_N = 8, 64
RW_W = 512
D_FF = 2752
NORM_EPS = 1e-6
HEAD_NORM_EPS = 1e-5
RW_GN_EPS = RW_N * 1e-5
N_DEV = 8
ADAM_LR, ADAM_B1, ADAM_B2, ADAM_EPS, ADAM_WD, ADAM_STEP = 0.001, 0.9, 0.999, 1e-08, 0.01, 10

C_GA, C_GB, C_Q, C_K, C_V, C_OG = 0, 1024, 2048, 2304, 2560, 3072
C_RW = 3584
C_R, C_RK, C_RV, C_WLAL, C_GL = 3584, 4096, 4608, 5120, 5248
C_AFAB = 5376
NP = 5632
RW_PW = 1792
FFP = 2816
LANE = 128
VMEM_LIMIT = 56 * 1024 * 1024


def _cparams(sem):
    return pltpu.CompilerParams(dimension_semantics=sem, vmem_limit_bytes=VMEM_LIMIT)


@jax.custom_vjp
def mm(a, b):
    return jnp.dot(a.astype(MXU_DTYPE), b.astype(MXU_DTYPE), preferred_element_type=F32)


def _mm_fwd(a, b):
    return mm(a, b), (a, b)


def _mm_bwd(res, g):
    a, b = res
    gb = g.astype(MXU_DTYPE)
    da = lax.dot_general(gb, b.astype(MXU_DTYPE), (((1,), (1,)), ((), ())), preferred_element_type=F32)
    db = lax.dot_general(a.astype(MXU_DTYPE), gb, (((0,), (0,)), ((), ())), preferred_element_type=F32)
    return da.astype(a.dtype), db.astype(b.dtype)


mm.defvjp(_mm_fwd, _mm_bwd)


@jax.custom_vjp
def mm_nt(a, b):
    return lax.dot_general(a.astype(MXU_DTYPE), b.astype(MXU_DTYPE), (((1,), (1,)), ((), ())), preferred_element_type=F32)


def _mm_nt_fwd(a, b):
    return mm_nt(a, b), (a, b)


def _mm_nt_bwd(res, g):
    a, b = res
    gb = g.astype(MXU_DTYPE)
    da = jnp.dot(gb, b.astype(MXU_DTYPE), preferred_element_type=F32)
    db = lax.dot_general(gb, a.astype(MXU_DTYPE), (((0,), (0,)), ((), ())), preferred_element_type=F32)
    return da.astype(a.dtype), db.astype(b.dtype)


mm_nt.defvjp(_mm_nt_fwd, _mm_nt_bwd)


@jax.custom_vjp
def mm_tn(a, b):
    return lax.dot_general(a.astype(MXU_DTYPE), b.astype(MXU_DTYPE), (((0,), (0,)), ((), ())), preferred_element_type=F32)


def _mm_tn_fwd(a, b):
    return mm_tn(a, b), (a, b)


def _mm_tn_bwd(res, g):
    a, b = res
    gb = g.astype(MXU_DTYPE)
    da = lax.dot_general(b.astype(MXU_DTYPE), gb, (((1,), (1,)), ((), ())), preferred_element_type=F32)
    db = jnp.dot(a.astype(MXU_DTYPE), gb, preferred_element_type=F32)
    return da.astype(a.dtype), db.astype(b.dtype)


mm_tn.defvjp(_mm_tn_fwd, _mm_tn_bwd)


@functools.partial(jax.custom_vjp, nondiff_argnums=(2, 3))
def sel_dot(x, s, dims, x_first):
    sb = s.astype(MXU_DTYPE)
    hi = x.astype(MXU_DTYPE)
    r1 = x - hi.astype(F32)
    mid = r1.astype(MXU_DTYPE)
    lo = (r1 - mid.astype(F32)).astype(MXU_DTYPE)
    out = None
    for part in (hi, mid, lo):
        ops = (part, sb) if x_first else (sb, part)
        d = lax.dot_general(*ops, (dims, ((), ())), preferred_element_type=F32)
        out = d if out is None else out + d
    return out


def _sel_dot_fwd(x, s, dims, x_first):
    return sel_dot(x, s, dims, x_first), s


def _sel_dot_bwd(dims, x_first, s, g):
    if x_first:
        (cx,), (cs,) = dims
        dx = sel_dot(g, s, ((1,), (1 - cs,)), True) if cx == 1 else sel_dot(g, s, ((1 - cs,), (1,)), False)
    else:
        (cs,), (cx,) = dims
        dx = sel_dot(g, s, ((1 - cs,), (0,)), False) if cx == 0 else sel_dot(g, s, ((0,), (1 - cs,)), True)
    return dx, jnp.zeros_like(s)


sel_dot.defvjp(_sel_dot_fwd, _sel_dot_bwd)


def mm_exact(a, b, b_is_01=True):
    return sel_dot(a, b, ((1,), (0,)), True) if b_is_01 else sel_dot(b, a, ((1,), (0,)), False)


def mm_tn_exact(a, b):
    return sel_dot(a, b, ((0,), (0,)), True)


def _softplus(x):
    return jnp.maximum(x, 0.0) + jnp.log(1.0 + jnp.exp(-jnp.abs(x)))


def _sigmoid(x):
    return jax.nn.sigmoid(x)


def _silu(x):
    return x * _sigmoid(x)


def _rmsnorm(x, g):
    return x * lax.rsqrt(jnp.mean(x * x, axis=-1, keepdims=True) + NORM_EPS) * g


def _segment_sum(x, seg):
    width = x.shape[1]
    i = lax.broadcasted_iota(jnp.int32, (width, width), 0) // seg
    j = lax.broadcasted_iota(jnp.int32, (width, width), 1) // seg
    return mm_exact(x, (i == j).astype(F32))


def _row_spec(tm, width, cb):
    return pl.BlockSpec((tm, width), lambda i: (i, cb))


def _full_spec(shape):
    nd = len(shape)
    return pl.BlockSpec(tuple(shape), lambda i: (0,) * nd)


def rowwise_fwd(name, f, rows, params, outs, tm):
    n = rows[0][0].shape[0]
    nr, npar = len(rows), len(params)

    def body(*refs):
        rv = [r[...] for r in refs[:nr]]
        pv = [r[...] for r in refs[nr:nr + npar]]
        res = f(rv, pv)
        for o_ref, val in zip(refs[nr + npar:], res):
            o_ref[...] = val.astype(o_ref.dtype)

    return pl.pallas_call(
        body, name=name, grid=(n // tm,),
        in_specs=[_row_spec(tm, w, cb) for _, w, cb in rows] + [_full_spec(p.shape) for p in params],
        out_specs=[_row_spec(tm, w, 0) for w, _ in outs],
        out_shape=[jax.ShapeDtypeStruct((n, w), dt) for w, dt in outs],
        compiler_params=_cparams(("arbitrary",)),
    )(*[a for a, _, _ in rows], *params)


def rowwise_bwd(name, f, rows, params, douts, tm, adds=(), grad_rows=None):
    n = rows[0][0].shape[0]
    nr, npar = len(rows), len(params)
    grad_rows = list(range(nr)) if grad_rows is None else list(grad_rows)
    flat_d = [d for group in douts for d in group]
    nd, na, ng = len(flat_d), len(adds), len(grad_rows)

    def body(*refs):
        rv = [r[...] for r in refs[:nr]]
        pv = [r[...] for r in refs[nr:nr + npar]]
        dflat = [r[...].astype(F32) for r in refs[nr + npar:nr + npar + nd]]
        av = [r[...] for r in refs[nr + npar + nd:nr + npar + nd + na]]
        o = nr + npar + nd + na
        drow_refs, dpar_refs = refs[o:o + ng], refs[o + ng:o + ng + npar]
        dv, pos = [], 0
        for group in douts:
            dv.append(sum(dflat[pos + 1:pos + len(group)], dflat[pos]))
            pos += len(group)

        @pl.when(pl.program_id(0) == 0)
        def _():
            for r in dpar_refs:
                r[...] = jnp.zeros_like(r)

        def g(grows, pars):
            full = list(rv)
            for i, val in zip(grad_rows, grows):
                full[i] = val
            return f(full, pars)

        res, vjp = jax.vjp(g, [rv[i] for i in grad_rows], pv)
        drows, dpars = vjp([d.astype(r.dtype) for d, r in zip(dv, res)])
        drows = [d.astype(F32) for d in drows]
        for (idx, _), a in zip(adds, av):
            drows[idx] = drows[idx] + a.astype(F32)
        for r, d in zip(drow_refs, drows):
            r[...] = d
        for r, d in zip(dpar_refs, dpars):
            r[...] += d.astype(F32)

    res = pl.pallas_call(
        body, name=name, grid=(n // tm,),
        in_specs=[_row_spec(tm, w, cb) for _, w, cb in rows] + [_full_spec(p.shape) for p in params]
        + [_row_spec(tm, w, cb) for _, w, cb in flat_d] + [_row_spec(tm, w, cb) for _, (_, w, cb) in adds],
        out_specs=[_row_spec(tm, rows[i][1], 0) for i in grad_rows] + [_full_spec(p.shape) for p in params],
        out_shape=[jax.ShapeDtypeStruct((n, rows[i][1]), F32) for i in grad_rows]
        + [jax.ShapeDtypeStruct(p.shape, F32) for p in params],
        compiler_params=_cparams(("arbitrary",)),
    )(*[a for a, _, _ in rows], *params, *[a for a, _, _ in flat_d], *[a for _, (a, _, _) in adds])
    return res[:ng], res[ng:]


def matmul(name, a, b, mode, out_dtype, tm, tn, tk, comm=()):
    nc = len(comm)
    flags = [sc for _, sc in comm]
    if mode == "nn":
        (m, k), n = a.shape, b.shape[1]
        a_spec = pl.BlockSpec((tm, tk), lambda i, j, kk: (i, kk))
        b_spec = pl.BlockSpec((tk, tn), lambda i, j, kk: (kk, j))
        dims = (((1,), (0,)), ((), ()))
    elif mode == "nt":
        (m, k), n = a.shape, b.shape[0]
        a_spec = pl.BlockSpec((tm, tk), lambda i, j, kk: (i, kk))
        b_spec = pl.BlockSpec((tn, tk), lambda i, j, kk: (j, kk))
        dims = (((1,), (1,)), ((), ()))
    else:
        (k, m), n = a.shape, b.shape[1]
        a_spec = pl.BlockSpec((tk, tm), lambda i, j, kk: (kk, i))
        b_spec = pl.BlockSpec((tk, tn), lambda i, j, kk: (kk, j))
        dims = (((0,), (0,)), ((), ()))
    assert m % tm == 0 and n % tn == 0 and k % tk == 0, (name, a.shape, b.shape, tm, tn, tk)
    nk = k // tk
    grid = (m // tm, n // tn, nk)

    def body(*refs):
        a_ref, b_ref, c_in, o_ref = refs[0], refs[1], refs[2:2 + nc], refs[2 + nc]
        c_out, acc_ref, sems = refs[3 + nc:3 + 2 * nc], refs[3 + 2 * nc], refs[4 + 2 * nc:]
        kk = pl.program_id(2)
        step = (pl.program_id(0) * grid[1] + pl.program_id(1)) * nk + kk
        if nc:
            start, wait = _exchange_plan(flags, c_in, c_out, *sems)

            @pl.when(step == 0)
            def _():
                start()

        part = lax.dot_general(a_ref[...].astype(MXU_DTYPE), b_ref[...].astype(MXU_DTYPE), dims, preferred_element_type=F32)
        if nk == 1:
            o_ref[...] = part.astype(o_ref.dtype)
        else:
            @pl.when(kk == 0)
            def _():
                acc_ref[...] = part

            @pl.when((kk > 0) & (kk < nk - 1))
            def _():
                acc_ref[...] += part

            @pl.when(kk == nk - 1)
            def _():
                o_ref[...] = (acc_ref[...] + part).astype(o_ref.dtype)

        if nc:
            @pl.when(step == grid[0] * grid[1] * nk - 1)
            def _():
                wait()

    c_args, c_specs, c_shapes, c_sems = _comm_specs(comm)
    res = pl.pallas_call(
        body, name=name, grid=grid,
        in_specs=[a_spec, b_spec] + c_specs,
        out_specs=[pl.BlockSpec((tm, tn), lambda i, j, kk: (i, j))] + c_specs,
        out_shape=[jax.ShapeDtypeStruct((m, n), out_dtype)] + c_shapes,
        scratch_shapes=[pltpu.VMEM((tm, tn) if nk > 1 else (8, LANE), F32)] + c_sems,
        compiler_params=_cparams(("arbitrary", "arbitrary", "arbitrary")),
    )(a, b, *c_args)
    return res if nc else res[0]


def _prev(u, first):
    return jnp.where(first, 0.0, pltpu.roll(u, 1, 0))


def _next(u, last):
    return jnp.where(last, 0.0, pltpu.roll(u, u.shape[0] - 1, 0))


def _edge_masks(t, w):
    row = lax.broadcasted_iota(jnp.int32, (t, w), 0)
    return row == 0, row == t - 1


WIN, HALO = 64, 8
MID = slice(HALO, HALO + WIN)


def _window(ref, i, t):
    r0 = pl.multiple_of(i * WIN, WIN)
    before = ref[pl.ds(pl.multiple_of(jnp.maximum(r0 - HALO, 0), HALO), HALO), :]
    after = ref[pl.ds(pl.multiple_of(jnp.minimum(r0 + WIN, t - HALO), HALO), HALO), :]
    before = jnp.where(i == 0, 0.0, before.astype(F32))
    after = jnp.where(i == t // WIN - 1, 0.0, after.astype(F32))
    return jnp.concatenate([before, ref[pl.ds(r0, WIN), :].astype(F32), after], axis=0)


def _wprev(u):
    return pltpu.roll(u, 1, 0)


def _wnext(u):
    return pltpu.roll(u, u.shape[0] - 1, 0)


def _mid_rows(i):
    return pl.ds(pl.multiple_of(i * WIN, WIN), WIN)


def _colsum(x):
    return jnp.sum(x[MID], axis=0, keepdims=True)


SHIFT_CW = 256


def shift_fwd(p, mu_prev, mu_next, nb, t):
    cw, c0 = SHIFT_CW, C_RW // SHIFT_CW

    def body(p_ref, mp_ref, mn_ref, s_ref):
        x = p_ref[...]
        first, last = _edge_masks(t, cw)
        s_ref[...] = x + mp_ref[...] * (_prev(x, first) - x) + mn_ref[...] * (_next(x, last) - x)

    return pl.pallas_call(
        body, name="rwkv_shift_fwd", grid=(nb, RW_PW // cw),
        in_specs=[pl.BlockSpec((t, cw), lambda b, j: (b, c0 + j)), pl.BlockSpec((1, cw), lambda b, j: (0, j)),
                  pl.BlockSpec((1, cw), lambda b, j: (0, j))],
        out_specs=pl.BlockSpec((t, cw), lambda b, j: (b, j)),
        out_shape=jax.ShapeDtypeStruct((nb * t, RW_PW), F32),
        compiler_params=_cparams(("arbitrary", "arbitrary")),
    )(p, mu_prev, mu_next)


def shift_bwd(p, ds, mu_prev, mu_next, nb, t):
    cw, c0 = SHIFT_CW, C_RW // SHIFT_CW

    def body(p_ref, ds_ref, mp_ref, mn_ref, dp_ref, dmp_ref, dmn_ref):
        @pl.when(pl.program_id(1) == 0)
        def _():
            dmp_ref[...] = jnp.zeros_like(dmp_ref)
            dmn_ref[...] = jnp.zeros_like(dmn_ref)

        mp, mn = mp_ref[...], mn_ref[...]

        def step(i, carry):
            dmp, dmn = carry
            x, g = _window(p_ref, i, t), _window(ds_ref, i, t)
            dp = g * (1.0 - mp - mn) + _wnext(mp * g) + _wprev(mn * g)
            dp_ref[_mid_rows(i), :] = dp[MID]
            return dmp + _colsum(g * (_wprev(x) - x)), dmn + _colsum(g * (_wnext(x) - x))

        zero = jnp.zeros((1, cw), F32)
        dmp, dmn = lax.fori_loop(0, t // WIN, step, (zero, zero))
        dmp_ref[...] += dmp
        dmn_ref[...] += dmn

    return pl.pallas_call(
        body, name="rwkv_shift_bwd", grid=(RW_PW // cw, nb),
        in_specs=[pl.BlockSpec((t, cw), lambda j, b: (b, c0 + j)), pl.BlockSpec((t, cw), lambda j, b: (b, j)),
                  pl.BlockSpec((1, cw), lambda j, b: (0, j)), pl.BlockSpec((1, cw), lambda j, b: (0, j))],
        out_specs=[pl.BlockSpec((t, cw), lambda j, b: (b, j)), pl.BlockSpec((1, cw), lambda j, b: (0, j)),
                   pl.BlockSpec((1, cw), lambda j, b: (0, j))],
        out_shape=[jax.ShapeDtypeStruct((nb * t, RW_PW), F32), jax.ShapeDtypeStruct((1, RW_PW), F32),
                   jax.ShapeDtypeStruct((1, RW_PW), F32)],
        compiler_params=_cparams(("arbitrary", "arbitrary")),
    )(p, ds, mu_prev, mu_next)


def conv_glu_fwd(u, cw, cb, nb, t):
    def body(u_ref, w_ref, b_ref, z_ref):
        x, w = u_ref[...], w_ref[...]
        first, last = _edge_masks(t, 2 * LANE)
        c = w[0:1] * _prev(x, first) + w[1:2] * x + w[2:3] * _next(x, last) + b_ref[...]
        z_ref[...] = (_silu(c[:, :LANE]) * c[:, LANE:]).astype(z_ref.dtype)

    return pl.pallas_call(
        body, name="conv_glu_fwd", grid=(nb, FFP // LANE),
        in_specs=[pl.BlockSpec((t, 2 * LANE), lambda b, j: (b, j)), pl.BlockSpec((3, 2 * LANE), lambda b, j: (0, j)),
                  pl.BlockSpec((1, 2 * LANE), lambda b, j: (0, j))],
        out_specs=pl.BlockSpec((t, LANE), lambda b, j: (b, j)),
        out_shape=jax.ShapeDtypeStruct((nb * t, FFP), MXU_DTYPE),
        compiler_params=_cparams(("arbitrary", "arbitrary")),
    )(u, cw, cb)


def conv_glu_bwd(u, dz, cw, cb, nb, t):
    def body(u_ref, dz_ref, w_ref, b_ref, du_ref, dw_ref, db_ref):
        @pl.when(pl.program_id(1) == 0)
        def _():
            dw_ref[...] = jnp.zeros_like(dw_ref)
            db_ref[...] = jnp.zeros_like(db_ref)

        w, bias = w_ref[...], b_ref[...]

        def step(i, carry):
            x, g = _window(u_ref, i, t), _window(dz_ref, i, t)
            xp, xn = _wprev(x), _wnext(x)
            c = w[0:1] * xp + w[1:2] * x + w[2:3] * xn + bias
            cg, cv = c[:, :LANE], c[:, LANE:]
            sg = _sigmoid(cg)
            dcg = g * cv * (sg * (1.0 + cg * (1.0 - sg)))
            dcv = g * (cg * sg)
            dc = jnp.concatenate([dcg, dcv], axis=1)
            du = w[1:2] * dc + _wnext(w[0:1] * dc) + _wprev(w[2:3] * dc)
            du_ref[_mid_rows(i), :] = du[MID].astype(du_ref.dtype)
            return tuple(acc + _colsum(val) for acc, val in zip(carry, (dc * xp, dc * x, dc * xn, dc)))

        zero = jnp.zeros((1, 2 * LANE), F32)
        sums = lax.fori_loop(0, t // WIN, step, (zero, zero, zero, zero))
        for row in range(3):
            dw_ref[row:row + 1, :] += sums[row]
        db_ref[...] += sums[3]

    return pl.pallas_call(
        body, name="conv_glu_bwd", grid=(FFP // LANE, nb),
        in_specs=[pl.BlockSpec((t, 2 * LANE), lambda j, b: (b, j)), pl.BlockSpec((t, LANE), lambda j, b: (b, j)),
                  pl.BlockSpec((3, 2 * LANE), lambda j, b: (0, j)), pl.BlockSpec((1, 2 * LANE), lambda j, b: (0, j))],
        out_specs=[pl.BlockSpec((t, 2 * LANE), lambda j, b: (b, j)), pl.BlockSpec((3, 2 * LANE), lambda j, b: (0, j)),
                   pl.BlockSpec((1, 2 * LANE), lambda j, b: (0, j))],
        out_shape=[jax.ShapeDtypeStruct((nb * t, 2 * FFP), MXU_DTYPE), jax.ShapeDtypeStruct((3, 2 * FFP), F32),
                   jax.ShapeDtypeStruct((1, 2 * FFP), F32)],
        compiler_params=_cparams(("arbitrary", "arbitrary")),
    )(u, dz, cw, cb)


def _gla_chunk(q, k, v, afab, wa2p, ba, s_in, reverse, sb):
    c = GLA_CHUNK
    r = sb * c
    ri = lax.broadcasted_iota(jnp.int32, (r, r), 0)
    ci = lax.broadcasted_iota(jnp.int32, (r, r), 1)
    same = (ri // c) == (ci // c)
    keep = same & ((ci >= ri) if reverse else (ci <= ri))
    i_ref = (c - 1 - c // 2) if reverse else (c // 2)
    pick_ref = (ci == (ri // c) * c + i_ref).astype(F32)
    seq_cols = (lax.broadcasted_iota(jnp.int32, (r, sb * LANE), 0) // c) == (lax.broadcasted_iota(jnp.int32, (r, sb * LANE), 1) // LANE)
    expand = lambda x: jnp.where(seq_cols, jnp.concatenate([x] * sb, axis=1), 0.0)
    lane = lax.broadcasted_iota(jnp.int32, (1, LANE), 1)
    outs, states = [None] * GLA_H, [None] * GLA_H
    for pr in range(GLA_H // 2):
        la = -_softplus(-(mm(afab, wa2p[pr]) + ba[pr])) * (1.0 / GLA_LOGIT_NORM)
        b = mm_exact(keep.astype(F32), la, b_is_01=False)
        b_ref = mm_exact(pick_ref, b, b_is_01=False)
        b_last = mm_exact(same.astype(F32), la, b_is_01=False)
        qs = q[pr] * (GLA_DK ** -0.5)
        qi = qs * jnp.exp(b - b_ref)
        ki = k[pr] * jnp.exp(b_ref - b)
        kd = k[pr] * jnp.exp(b_last - b)
        qb = qs * jnp.exp(b)
        dec = jnp.exp(mm_tn_exact(expand(la), jnp.ones((r, LANE), F32)))
        for h in (2 * pr, 2 * pr + 1):
            m = ((lane // GLA_DK) == (h % 2)).astype(F32)
            a = jnp.where(keep, mm_nt(qi * m, ki), 0.0)
            o_intra = mm(a, v[h])
            kv = mm_tn(expand(kd * m), v[h])
            o_inter = mm(expand(qb * m), s_in[h])
            outs[h] = o_intra + o_inter
            states[h] = s_in[h] * dec + kv
    return outs, states


def _gla_load(q_ref, k_ref, v_ref, af_ref, w_ref, ba_ref, sb, rows):
    stack = lambda ref, c0: jnp.concatenate([ref[s, rows, c0:c0 + LANE] for s in range(sb)], axis=0)
    q = [stack(q_ref, pr * LANE) for pr in range(GLA_H // 2)]
    k = [stack(k_ref, pr * LANE) for pr in range(GLA_H // 2)]
    v = [stack(v_ref, h * GLA_DV) for h in range(GLA_H)]
    w = [w_ref[:, pr * LANE:(pr + 1) * LANE] for pr in range(GLA_H // 2)]
    ba = [ba_ref[:, pr * LANE:(pr + 1) * LANE] for pr in range(GLA_H // 2)]
    return q, k, v, stack(af_ref, 0), w, ba


GLA_TILE = 256
GLA_SB = 4


def _gla_specs(nb, t, reverse):
    tile = min(GLA_TILE, t)
    nt = t // tile
    sb = GLA_SB if nb % GLA_SB == 0 else 1
    return tile, tile // GLA_CHUNK, nt, sb, ((lambda j: nt - 1 - j) if reverse else (lambda j: j))


def gla_fwd(p, wa2p, ba, o_add, nb, t, reverse):
    tile, cpt, nt, sb, tj = _gla_specs(nb, t, reverse)
    has_add = o_add is not None

    def body(*refs):
        if has_add:
            q_ref, k_ref, v_ref, af_ref, w_ref, ba_ref, add_ref, o_ref, hist_ref, s_ref = refs
        else:
            q_ref, k_ref, v_ref, af_ref, w_ref, ba_ref, o_ref, hist_ref, s_ref = refs

        @pl.when(pl.program_id(1) == 0)
        def _():
            s_ref[...] = jnp.zeros_like(s_ref)

        def step(i, carry):
            ci = (cpt - 1 - i) if reverse else i
            rows = pl.ds(pl.multiple_of(ci * GLA_CHUNK, GLA_CHUNK), GLA_CHUNK)
            s_in = [s_ref[h] for h in range(GLA_H)]
            for h in range(GLA_H):
                for s in range(sb):
                    hist_ref[s, ci, h] = s_in[h][s * LANE:(s + 1) * LANE]
            q, k, v, af, w, ba = _gla_load(q_ref, k_ref, v_ref, af_ref, w_ref, ba_ref, sb, rows)
            outs, states = _gla_chunk(q, k, v, af, w, ba, s_in, reverse, sb)
            for h in range(GLA_H):
                for s in range(sb):
                    oh = outs[h][s * GLA_CHUNK:(s + 1) * GLA_CHUNK]
                    if has_add:
                        oh = oh + add_ref[s, rows, h * GLA_DV:(h + 1) * GLA_DV]
                    o_ref[s, rows, h * GLA_DV:(h + 1) * GLA_DV] = oh
                s_ref[h] = states[h]
            return carry

        lax.fori_loop(0, cpt, step, 0)

    col = lambda width, c0: pl.BlockSpec((sb, tile, width), lambda b, j: (b, tj(j), c0 // width))
    in_specs = [col(256, C_Q), col(256, C_K), col(512, C_V), col(LANE, C_AFAB),
                pl.BlockSpec((LANE, 256), lambda b, j: (0, 0)), pl.BlockSpec((1, 256), lambda b, j: (0, 0))]
    p3 = p.reshape(nb, t, p.shape[1])
    args = [p3, p3, p3, p3, wa2p, ba]
    if has_add:
        in_specs.append(col(512, 0))
        args.append(o_add.reshape(nb, t, 512))
    o, hist = pl.pallas_call(
        body, name="gla_fwd_rev" if reverse else "gla_fwd", grid=(nb // sb, nt),
        in_specs=in_specs,
        out_specs=[col(512, 0), pl.BlockSpec((sb, cpt, GLA_H, LANE, LANE), lambda b, j: (b, tj(j), 0, 0, 0))],
        out_shape=[jax.ShapeDtypeStruct((nb, t, 512), F32),
                   jax.ShapeDtypeStruct((nb, t // GLA_CHUNK, GLA_H, LANE, LANE), F32)],
        scratch_shapes=[pltpu.VMEM((GLA_H, sb * LANE, LANE), F32)],
        compiler_params=_cparams(("arbitrary", "arbitrary")),
    )(*args)
    return o.reshape(nb * t, 512), hist


def gla_bwd(p, wa2p, ba, hist, do, dprev, nb, t, reverse):
    tile, cpt, nt, sb, tj_f = _gla_specs(nb, t, reverse)
    tj = lambda j: tj_f(nt - 1 - j)
    has_prev = dprev is not None

    def body(*refs):
        if has_prev:
            q_ref, k_ref, v_ref, af_ref, w_ref, ba_ref, hist_ref, do_ref, prev_ref, dqkv_ref, dw_ref, dba_ref, ds_ref = refs
        else:
            q_ref, k_ref, v_ref, af_ref, w_ref, ba_ref, hist_ref, do_ref, dqkv_ref, dw_ref, dba_ref, ds_ref = refs

        @pl.when((pl.program_id(0) == 0) & (pl.program_id(1) == 0))
        def _():
            dw_ref[...] = jnp.zeros_like(dw_ref)
            dba_ref[...] = jnp.zeros_like(dba_ref)

        @pl.when(pl.program_id(1) == 0)
        def _():
            ds_ref[...] = jnp.zeros_like(ds_ref)

        def step(i, carry):
            ci = i if reverse else (cpt - 1 - i)
            rows = pl.ds(pl.multiple_of(ci * GLA_CHUNK, GLA_CHUNK), GLA_CHUNK)
            fn = functools.partial(_gla_chunk, reverse=reverse, sb=sb)
            seqs = lambda get: jnp.concatenate([get(s) for s in range(sb)], axis=0)
            s_in = [seqs(lambda s: hist_ref[s, ci, h]) for h in range(GLA_H)]
            q, k, v, af, w, ba = _gla_load(q_ref, k_ref, v_ref, af_ref, w_ref, ba_ref, sb, rows)
            _, vjp = jax.vjp(fn, q, k, v, af, w, ba, s_in)
            d_o = [seqs(lambda s: do_ref[s, rows, h * GLA_DV:(h + 1) * GLA_DV]) for h in range(GLA_H)]
            d_s = [ds_ref[h] for h in range(GLA_H)]
            dq, dk, dv, daf, dw, dba, ds_in = vjp((d_o, d_s))
            pieces = [(pr * LANE, dq[pr]) for pr in range(2)] + [(256 + pr * LANE, dk[pr]) for pr in range(2)]
            pieces += [(512 + h * GLA_DV, dv[h]) for h in range(GLA_H)] + [(1024, daf)]
            for c0, val in pieces:
                for s in range(sb):
                    part = val[s * GLA_CHUNK:(s + 1) * GLA_CHUNK]
                    if has_prev:
                        part = part + prev_ref[s, rows, c0:c0 + LANE]
                    dqkv_ref[s, rows, c0:c0 + LANE] = part
            for pr in range(2):
                dw_ref[:, pr * LANE:(pr + 1) * LANE] += dw[pr]
                dba_ref[:, pr * LANE:(pr + 1) * LANE] += dba[pr]
            for h in range(GLA_H):
                ds_ref[h] = ds_in[h]
            return carry

        lax.fori_loop(0, cpt, step, 0)

    col = lambda width, c0: pl.BlockSpec((sb, tile, width), lambda b, j: (b, tj(j), c0 // width))
    in_specs = [col(256, C_Q), col(256, C_K), col(512, C_V), col(LANE, C_AFAB),
                pl.BlockSpec((LANE, 256), lambda b, j: (0, 0)), pl.BlockSpec((1, 256), lambda b, j: (0, 0)),
                pl.BlockSpec((sb, cpt, GLA_H, LANE, LANE), lambda b, j: (b, tj(j), 0, 0, 0)), col(512, 0)]
    p3 = p.reshape(nb, t, p.shape[1])
    args = [p3, p3, p3, p3, wa2p, ba, hist, do.reshape(nb, t, 512)]
    if has_prev:
        in_specs.append(col(1152, 0))
        args.append(dprev.reshape(nb, t, 1152))
    dqkv, dw, dba = pl.pallas_call(
        body, name="gla_bwd_rev" if reverse else "gla_bwd", grid=(nb // sb, nt),
        in_specs=in_specs,
        out_specs=[col(1152, 0), pl.BlockSpec((LANE, 256), lambda b, j: (0, 0)), pl.BlockSpec((1, 256), lambda b, j: (0, 0))],
        out_shape=[jax.ShapeDtypeStruct((nb, t, 1152), F32), jax.ShapeDtypeStruct((LANE, 256), F32),
                   jax.ShapeDtypeStruct((1, 256), F32)],
        scratch_shapes=[pltpu.VMEM((GLA_H, sb * LANE, LANE), F32)],
        compiler_params=_cparams(("arbitrary", "arbitrary")),
    )(*args)
    return dqkv.reshape(nb * t, 1152), dw, dba


SCAN_TB = 8
RW_VH = RW_N // 2


def _bwd_lanes():
    lane = lax.broadcasted_iota(jnp.int32, (1, LANE), 1)
    return ((lane // (LANE // 4)) % 2) == 1


def _comm_specs(comm):
    anyspec = pl.BlockSpec(memory_space=pl.ANY)
    n = len(comm)
    shapes = [jax.ShapeDtypeStruct((N_DEV,) + (a.shape[1:] if sc else a.shape), a.dtype) for a, sc in comm]
    sems = [pltpu.SemaphoreType.DMA((n, N_DEV - 1)), pltpu.SemaphoreType.DMA((n, N_DEV - 1)), pltpu.SemaphoreType.DMA((n,))] if n else []
    return [a for a, _ in comm], [anyspec] * n, shapes, sems


def rwkv_scan_fwd(r, w, k, a, b, v, comm=()):
    t = r.shape[0]
    nt = t // SCAN_TB
    nc = len(comm)
    flags = [sc for _, sc in comm]

    def body(*refs):
        (rf, rm, kf, km, af, am, bf, bm, wf_ref, wm_ref, vf, vm), refs = refs[:12], refs[12:]
        c_in, refs = refs[:nc], refs[nc:]
        (yf_ref, ym_ref, hist_ref, sa_ref), refs = refs[:4], refs[4:]
        c_out, refs = refs[:nc], refs[nc:]
        s_ref, sems = refs[0], refs[1:]
        i = pl.program_id(0)
        if nc:
            start, wait = _exchange_plan(flags, c_in, c_out, *sems)

        @pl.when(i == 0)
        def _():
            s_ref[...] = jnp.zeros_like(s_ref)
            if nc:
                start()

        bwd = _bwd_lanes()

        for tt in range(SCAN_TB):
            mt = SCAN_TB - 1 - tt
            pick = lambda f_ref, m_ref: jnp.where(bwd, m_ref[mt], f_ref[tt])
            rt, kt, at, bt, wt = pick(rf, rm), pick(kf, km), pick(af, am), pick(bf, bm), pick(wf_ref, wm_ref)
            for vi in range(RW_VH):
                sv = s_ref[vi] if tt == 0 else hist_ref[tt - 1, vi]
                sa = jnp.sum(sv * at, axis=0, keepdims=True)
                v_row = jnp.where(bwd, vm[mt, vi:vi + 1, :], vf[tt, vi:vi + 1, :])
                sn = sv * wt + sa * bt + v_row * kt
                hist_ref[tt, vi] = sn
                y_row = jnp.sum(sn * rt, axis=0, keepdims=True)
                yf_ref[tt, vi:vi + 1, :] = y_row
                ym_ref[mt, vi:vi + 1, :] = y_row
                sa_ref[tt, vi:vi + 1, :] = sa
        s_ref[...] = hist_ref[SCAN_TB - 1]

        if nc:
            @pl.when(i == nt - 1)
            def _():
                wait()

    fwd_map, mir_map = (lambda i: (i, 0, 0)), (lambda i: (nt - 1 - i, 0, 0))
    kf_spec, km_spec = pl.BlockSpec((SCAN_TB, RW_N, LANE), fwd_map), pl.BlockSpec((SCAN_TB, RW_N, LANE), mir_map)
    vf_spec, vm_spec = pl.BlockSpec((SCAN_TB, RW_VH, LANE), fwd_map), pl.BlockSpec((SCAN_TB, RW_VH, LANE), mir_map)
    c_args, c_specs, c_shapes, c_sems = _comm_specs(comm)
    vshape = jax.ShapeDtypeStruct((t, RW_VH, LANE), F32)
    return pl.pallas_call(
        body, name="rwkv_scan_fwd", grid=(nt,),
        in_specs=[kf_spec, km_spec] * 5 + [vf_spec, vm_spec] + c_specs,
        out_specs=[vf_spec, vm_spec, pl.BlockSpec((SCAN_TB, RW_VH, RW_N, LANE), lambda i: (i, 0, 0, 0)), vf_spec] + c_specs,
        out_shape=[vshape, vshape, jax.ShapeDtypeStruct((t, RW_VH, RW_N, LANE), F32), vshape] + c_shapes,
        scratch_shapes=[pltpu.VMEM((RW_VH, RW_N, LANE), F32)] + c_sems,
        compiler_params=_cparams(("arbitrary",)),
    )(r, r, k, k, a, a, b, b, w, w, v, v, *c_args)


def rwkv_scan_bwd(r, w, k, a, b, v, hist, sa, dy, comm=()):
    t = r.shape[0]
    nt = t // SCAN_TB
    nc = len(comm)
    flags = [sc for _, sc in comm]

    def body(*refs):
        (rf, rm, kf, km, af, am, bf, bm, wf_ref, wm_ref, vf, vm, hist_ref, prev_ref, sa_ref, dyf, dym), refs = refs[:17], refs[17:]
        c_in, refs = refs[:nc], refs[nc:]
        k_outs, (dvf_ref, dvm_ref), refs = refs[:4], refs[4:6], refs[6:]
        c_out, refs = refs[:nc], refs[nc:]
        ds_ref, sems = refs[0], refs[1:]
        i = pl.program_id(0)
        if nc:
            start, wait = _exchange_plan(flags, c_in, c_out, *sems)

        @pl.when(i == 0)
        def _():
            ds_ref[...] = jnp.zeros_like(ds_ref)
            if nc:
                start()

        bwd = _bwd_lanes()
        group = lax.broadcasted_iota(jnp.int32, (1, LANE), 1) // RW_Q
        first_block = i == nt - 1

        for tt in range(SCAN_TB - 1, -1, -1):
            mt = SCAN_TB - 1 - tt
            pick = lambda f_ref, m_ref: jnp.where(bwd, m_ref[mt], f_ref[tt])
            rt, kt, at, bt, wt = pick(rf, rm), pick(kf, km), pick(af, am), pick(bf, bm), pick(wf_ref, wm_ref)
            zero = jnp.zeros((RW_N, LANE), F32)
            dr, dw, dk, da, db = zero, zero, zero, zero, zero
            for vi in range(RW_VH):
                sn = hist_ref[tt, vi]
                sv = hist_ref[tt - 1, vi] if tt > 0 else jnp.where(first_block, 0.0, prev_ref[0, vi])
                sa_row = sa_ref[tt, vi:vi + 1, :]
                v_row = jnp.where(bwd, vm[mt, vi:vi + 1, :], vf[tt, vi:vi + 1, :])
                dy_row = jnp.where(bwd, dym[mt, vi:vi + 1, :], dyf[tt, vi:vi + 1, :])
                dsv = ds_ref[vi] + dy_row * rt
                dr = dr + sn * dy_row
                dsa = jnp.sum(dsv * bt, axis=0, keepdims=True)
                dw = dw + sv * dsv
                db = db + dsv * sa_row
                dk = dk + dsv * v_row
                dv_row = jnp.sum(dsv * kt, axis=0, keepdims=True)
                dvf_ref[tt, vi:vi + 1, :] = dv_row
                dvm_ref[mt, vi:vi + 1, :] = dv_row
                da = da + sv * dsa
                ds_ref[vi] = dsv * wt + dsa * at
            dr, dw, dk, da, db = [val + pltpu.roll(val, LANE // 2, 1) for val in (dr, dw, dk, da, db)]
            up, down = (lambda val: pltpu.roll(val, RW_Q, 1)), (lambda val: pltpu.roll(val, LANE - RW_Q, 1))
            packed_f = jnp.where(group == 0, dr, jnp.where(group == 1, up(dk), jnp.where(group == 2, da, up(db))))
            packed_m = jnp.where(group == 0, down(dr), jnp.where(group == 1, dk, jnp.where(group == 2, down(da), db)))
            k_outs[0][tt] = packed_f
            k_outs[1][mt] = packed_m
            k_outs[2][tt] = dw
            k_outs[3][mt] = dw

        if nc:
            @pl.when(i == nt - 1)
            def _():
                wait()

    fwd_map, mir_map = (lambda i: (nt - 1 - i, 0, 0)), (lambda i: (i, 0, 0))
    kf_spec, km_spec = pl.BlockSpec((SCAN_TB, RW_N, LANE), fwd_map), pl.BlockSpec((SCAN_TB, RW_N, LANE), mir_map)
    vf_spec, vm_spec = pl.BlockSpec((SCAN_TB, RW_VH, LANE), fwd_map), pl.BlockSpec((SCAN_TB, RW_VH, LANE), mir_map)
    prev_spec = pl.BlockSpec((1, RW_VH, RW_N, LANE), lambda i: (jnp.maximum((nt - 1 - i) * SCAN_TB - 1, 0), 0, 0, 0))
    c_args, c_specs, c_shapes, c_sems = _comm_specs(comm)
    kshape, vshape = jax.ShapeDtypeStruct((t, RW_N, LANE), F32), jax.ShapeDtypeStruct((t, RW_VH, LANE), F32)
    return pl.pallas_call(
        body, name="rwkv_scan_bwd", grid=(nt,),
        in_specs=[kf_spec, km_spec] * 5 + [vf_spec, vm_spec,
                                           pl.BlockSpec((SCAN_TB, RW_VH, RW_N, LANE), lambda i: (nt - 1 - i, 0, 0, 0)),
                                           prev_spec, vf_spec, vf_spec, vm_spec] + c_specs,
        out_specs=[kf_spec, km_spec] * 2 + [vf_spec, vm_spec] + c_specs,
        out_shape=[kshape] * 4 + [vshape] * 2 + c_shapes,
        scratch_shapes=[pltpu.VMEM((RW_VH, RW_N, LANE), F32)] + c_sems,
        compiler_params=_cparams(("arbitrary",)),
    )(r, r, k, k, a, a, b, b, w, w, v, v, hist, hist, sa, dy, dy, *c_args)


RELAYOUT_TB = 128
RW_Q = LANE // 4


def to_scan(name, x, cb, nb, t, value, x_bwd=None):
    tb = min(RELAYOUT_TB, t)
    rows_out = RW_VH if value else RW_N
    ins = [x] if x_bwd is None else [x, x_bwd]

    def body(*refs):
        x_refs, o_ref, scrs = refs[:len(ins)], refs[len(ins)], refs[len(ins) + 1:]
        for x_ref, scr in zip(x_refs, scrs):
            for b in range(nb):
                scr[b * RW_H:(b + 1) * RW_H] = x_ref[b].T.reshape(RW_H, RW_N, tb)
        for j in range(rows_out):
            lo = scrs[0][:, j, :]
            if value:
                hi = scrs[0][:, j + RW_VH, :]
                blk = [lo, lo, hi, hi]
            else:
                other = lo if x_bwd is None else scrs[1][:, j, :]
                blk = [lo, other, lo, other]
            o_ref[:, j, :] = jnp.concatenate(blk, axis=0).T

    return pl.pallas_call(
        body, name=name, grid=(t // tb,),
        in_specs=[pl.BlockSpec((nb, tb, RW_W), lambda i: (0, i, cb))] + [pl.BlockSpec((nb, tb, RW_W), lambda i: (0, i, 0))] * (len(ins) - 1),
        out_specs=pl.BlockSpec((tb, rows_out, LANE), lambda i: (i, 0, 0)),
        out_shape=jax.ShapeDtypeStruct((t, rows_out, LANE), F32),
        scratch_shapes=[pltpu.VMEM((nb * RW_H, RW_N, tb), F32)] * len(ins),
        compiler_params=_cparams(("arbitrary",)),
    )(*[a.reshape(nb, t, a.shape[1]) for a in ins])


def from_scan(name, xf, xm, nb, t, value, groups=None):
    tb = min(RELAYOUT_TB, t)
    rows_in = RW_VH if value else RW_N
    n_out = 1 if value else (4 if groups is None else 2)
    grp = lambda a, g: a[g * RW_Q:(g + 1) * RW_Q]

    def body(f_ref, m_ref, *rest):
        outs, scrs = rest[:n_out], rest[n_out:]
        lane_group = lax.broadcasted_iota(jnp.int32, (1, LANE), 1) // RW_Q
        for j in range(rows_in):
            f, m = f_ref[:, j, :], m_ref[:, j, :]
            if value:
                c = jnp.where(_bwd_lanes(), m, f).T
                scrs[0][:, j, :] = grp(c, 0) + grp(c, 1)
                scrs[0][:, j + RW_VH, :] = grp(c, 2) + grp(c, 3)
            elif groups is None:
                c = (f + m).T
                for q, scr in enumerate(scrs):
                    scr[:, j, :] = grp(c, q)
            else:
                c = jnp.where(lane_group == groups[1], m, f).T
                scrs[0][:, j, :] = grp(c, groups[0])
                scrs[1][:, j, :] = grp(c, groups[1])
        for o_ref, scr in zip(outs, scrs):
            for b in range(nb):
                o_ref[b] = scr[b * RW_H:(b + 1) * RW_H].reshape(RW_W, tb).T

    res = pl.pallas_call(
        body, name=name, grid=(t // tb,),
        in_specs=[pl.BlockSpec((tb, rows_in, LANE), lambda i: (i, 0, 0))] * 2,
        out_specs=[pl.BlockSpec((nb, tb, RW_W), lambda i: (0, i, 0))] * n_out,
        out_shape=[jax.ShapeDtypeStruct((nb, t, RW_W), F32)] * n_out,
        scratch_shapes=[pltpu.VMEM((nb * RW_H, RW_N, tb), F32)] * n_out,
        compiler_params=_cparams(("arbitrary",)),
    )(xf, xm)
    return [r.reshape(nb * t, RW_W) for r in res]


def f_norm(rows, params):
    (x,), (g,) = rows, params
    return [_rmsnorm(x, g)]


def f_rwkv_pre(rows, params):
    k, wlal, gl = rows
    w0f, w2f, w0b, w2b, a0, a2, g2, k_k, k_a = params
    tw = jnp.tanh(wlal)

    def decay(w0, w2):
        return jnp.exp(-jnp.exp(-_softplus(-(w0 + mm(tw, w2))) - 0.5))

    lr = _sigmoid(a0 + mm(wlal, a2))
    gate = mm(_sigmoid(gl), g2)
    kk = k * k_k
    kk = kk / jnp.maximum(jnp.sqrt(_segment_sum(kk * kk, RW_N)), 1e-12)
    kp = k * (1.0 + (lr - 1.0) * k_a)
    return [decay(w0f, w2f), decay(w0b, w2b), kp, -kk, kk * lr, gate]


def f_branch_post(rows, params):
    o, og, y, r, kp, v, g = rows
    gla_g, ln_w, ln_b, r_k = params
    on = o * lax.rsqrt(_segment_sum(o * o, GLA_DV) * (1.0 / GLA_DV) + HEAD_NORM_EPS)
    oa = on * gla_g * _silu(og)
    mu = _segment_sum(y, RW_N) * (1.0 / RW_N)
    yc = y - mu
    var = _segment_sum(yc * yc, RW_N) * (1.0 / RW_N)
    yn = yc * lax.rsqrt(var + RW_GN_EPS) * ln_w + ln_b
    bonus = _segment_sum(r * kp * r_k, RW_N) * v
    return [oa, (yn + bonus) * g]


def f_merge(rows, params):
    ga, gb, ya, yb = rows
    return [_sigmoid(ga) * ya + _sigmoid(gb) * yb]


def f_norm2(rows, params):
    (x, mo), (g,) = rows, params
    x1 = x + mo
    return [x1, _rmsnorm(x1, g)]


def loss_head(x1, ffo, tgt, gf, tm):
    n = x1.shape[0]

    def body(x1_ref, f_ref, t_ref, g_ref, loss_ref, dx_ref, dg_ref):
        @pl.when(pl.program_id(0) == 0)
        def _():
            loss_ref[...] = jnp.zeros_like(loss_ref)
            dg_ref[...] = jnp.zeros_like(dg_ref)

        tgt_v = t_ref[...]

        def f(x2, g):
            err = _rmsnorm(x2, g) - tgt_v
            return jnp.sum(jnp.sum(err * err, axis=-1, keepdims=True), axis=0, keepdims=True) * (0.5 / D)

        val, vjp = jax.vjp(f, x1_ref[...] + f_ref[...], g_ref[...])
        dx, dg = vjp(jnp.ones((1, 1), F32))
        loss_ref[...] += val
        dx_ref[...] = dx
        dg_ref[...] += dg

    return pl.pallas_call(
        body, name="loss_head", grid=(n // tm,),
        in_specs=[_row_spec(tm, D, 0)] * 3 + [_full_spec((1, D))],
        out_specs=[_full_spec((1, 1)), _row_spec(tm, D, 0), _full_spec((1, D))],
        out_shape=[jax.ShapeDtypeStruct((1, 1), F32), jax.ShapeDtypeStruct((n, D), F32), jax.ShapeDtypeStruct((1, D), F32)],
        compiler_params=_cparams(("arbitrary",)),
    )(x1, ffo, tgt, gf)


def _pad_cols(a, width):
    return jnp.pad(a, ((0, 0), (0, width - a.shape[1])))


def w_in_to_padded(w):
    return _pad_cols(jnp.concatenate([w[:, 3360:5408], w[:, 0:1536], w[:, 1568:3360], w[:, 1536:1568]], axis=1), NP)


def w_in_from_padded(wp):
    return jnp.concatenate([wp[:, 2048:3584], wp[:, 5376:5408], wp[:, 3584:5376], wp[:, 0:2048]], axis=1)


def ff_interleave(a):
    r = a.shape[0]
    halves = jnp.stack([_pad_cols(a[:, :D_FF], FFP), _pad_cols(a[:, D_FF:], FFP)], axis=1)
    return halves.reshape(r, 2, FFP // LANE, LANE).transpose(0, 2, 1, 3).reshape(r, 2 * FFP)


def ff_deinterleave(a):
    r = a.shape[0]
    halves = a.reshape(r, FFP // LANE, 2, LANE).transpose(0, 2, 1, 3).reshape(r, 2, FFP)
    return halves[:, :, :D_FF].reshape(r, 2 * D_FF)


def _rows_into(w, rows, off):
    return jnp.zeros((rows, w.shape[1]), w.dtype).at[off:off + w.shape[0]].set(w)


LATE = ("gla_proj", "rwkv_proj", "w_out", "ffn_up", "ffn_conv_w", "ffn_down")


def local_step(x, tgt, w, nb, t, late_blocks=None):
    n = nb * t
    tm = min(n, 1024)
    tkt = min(n, 2048)
    tr = min(n, 256)
    vec = lambda a: a.reshape(1, -1)
    w = dict(w)

    w_in_p = w_in_to_padded(w["w_in"])
    wa2_f, wa2_b = _rows_into(w["gla_wa2_f"], LANE, 0), _rows_into(w["gla_wa2_b"], LANE, GLA_RANK)
    w2f, w2b = _rows_into(w["rwkv_w2_f"], LANE, 0), _rows_into(w["rwkv_w2_b"], LANE, 0)
    a2 = _rows_into(w["rwkv_a2"], LANE, 64)
    g1, g2n, gf = vec(w["norm1_g"]), vec(w["norm2_g"]), vec(w["norm_f_g"])
    mu_prev, mu_next = vec(w["rwkv_mu_prev"]), vec(w["rwkv_mu_next"])
    pre_params = [vec(w["rwkv_w0_f"]), w2f, vec(w["rwkv_w0_b"]), w2b, vec(w["rwkv_a0"]), a2, w["rwkv_g2"],
                  vec(w["rwkv_k_k"]), vec(w["rwkv_k_a"])]
    post_params = [vec(w["gla_norm_g"]), vec(w["rwkv_ln_w"]), vec(w["rwkv_ln_b"]), vec(w["rwkv_r_k"])]
    ba_f, ba_b = vec(w["gla_ba_f"]), vec(w["gla_ba_b"])

    (h1,) = rowwise_fwd("norm1_fwd", f_norm, [(x, D, 0)], [g1], [(D, MXU_DTYPE)], tr)
    p = matmul("proj_in", h1, w_in_p, "nn", F32, tm, FFP // 2, D)
    s = shift_fwd(p, mu_prev, mu_next, nb, t)
    pre_rows = [(s, 512, 1), (s, LANE, 1536 // LANE), (s, LANE, 1664 // LANE)]
    wf, wb, kp, a_s, b_s, g = rowwise_fwd("rwkv_pre_fwd", f_rwkv_pre, pre_rows, pre_params, [(RW_W, F32)] * 6, tr)
    sc = [to_scan("to_scan_r", s, 0, nb, t, False), to_scan("to_scan_w", wf, 0, nb, t, False, x_bwd=wb),
          to_scan("to_scan_k", kp, 0, nb, t, False), to_scan("to_scan_a", a_s, 0, nb, t, False),
          to_scan("to_scan_b", b_s, 0, nb, t, False), to_scan("to_scan_v", s, 2, nb, t, True)]
    comm = [] if late_blocks is None else [(late_blocks[k], False) for k in LATE]
    y_scf, y_scm, hist_rw, sa_sc, *gathered = rwkv_scan_fwd(*sc, comm=comm)
    for k, g_k in zip(LATE, gathered):
        w[k] = _gathered_to_full(g_k, SHARDED[k])
    ffn_up_p = ff_interleave(w["ffn_up"])
    conv_w_p, conv_b_p = ff_interleave(w["ffn_conv_w"]), ff_interleave(vec(w["ffn_conv_b"]))
    ffn_down_p = jnp.pad(w["ffn_down"], ((0, FFP - D_FF), (0, 0)))
    (y,) = from_scan("from_scan_y", y_scf, y_scm, nb, t, True)
    o_f, hist_f = gla_fwd(p, wa2_f, ba_f, None, nb, t, False)
    o, hist_b = gla_fwd(p, wa2_b, ba_b, o_f, nb, t, True)
    post_rows = [(o, 512, 0), (p, 512, C_OG // 512), (y, 512, 0), (s, 512, 0), (kp, 512, 0), (s, 512, 2), (g, 512, 0)]
    oa, ob = rowwise_fwd("branch_post_fwd", f_branch_post, post_rows, post_params, [(512, MXU_DTYPE)] * 2, tr)
    ya = matmul("gla_proj", oa, w["gla_proj"], "nn", F32, tm, 512, 512)
    yb = matmul("rwkv_proj", ob, w["rwkv_proj"], "nn", F32, tm, 512, 512)
    merge_rows = [(p, D, 0), (p, D, 1), (ya, D, 0), (yb, D, 0)]
    (merged,) = rowwise_fwd("merge_fwd", f_merge, merge_rows, [], [(D, MXU_DTYPE)], tr)
    mo = matmul("w_out", merged, w["w_out"], "nn", F32, tm, 512, D)
    x1, h2 = rowwise_fwd("norm2_fwd", f_norm2, [(x, D, 0), (mo, D, 0)], [g2n], [(D, F32), (D, MXU_DTYPE)], tr)
    u = matmul("ffn_up", h2, ffn_up_p, "nn", F32, tm, FFP // 2, D)
    z = conv_glu_fwd(u, conv_w_p, conv_b_p, nb, t)
    ffo = matmul("ffn_down", z, ffn_down_p, "nn", F32, tm, D, FFP // 2)
    loss, dx2, dgf = loss_head(x1, ffo, tgt, gf, tr)

    dz = matmul("ffn_down_dx", dx2, ffn_down_p, "nt", F32, tm, FFP // 2, D)
    d_ffn_down_p = matmul("ffn_down_dw", z, dx2, "tn", F32, FFP // 2, 512, tkt)
    du, d_conv_w_p, d_conv_b_p = conv_glu_bwd(u, dz, conv_w_p, conv_b_p, nb, t)
    dh2 = matmul("ffn_up_dx", du, ffn_up_p, "nt", F32, tm, D, FFP // 2)
    d_ffn_up_p = matmul("ffn_up_dw", h2, du, "tn", F32, D, 512, tkt)
    (dx1,), (dg2,) = rowwise_bwd("norm2_bwd", f_norm2, [(x, D, 0), (mo, D, 0)], [g2n],
                                 [[(dx2, D, 0)], [(dh2, D, 0)]], tr, grad_rows=[1])
    dmerged = matmul("w_out_dx", dx1, w["w_out"], "nt", F32, tm, D, D)
    d_w_out = matmul("w_out_dw", merged, dx1, "tn", F32, D, 512, tkt)
    (dga, dgb, dya, dyb), _ = rowwise_bwd("merge_bwd", f_merge, merge_rows, [], [[(dmerged, D, 0)]], tr)
    d_oa = matmul("gla_proj_dx", dya, w["gla_proj"], "nt", F32, tm, 512, D)
    d_gla_proj = matmul("gla_proj_dw", oa, dya, "tn", F32, 512, 512, tkt)
    d_ob = matmul("rwkv_proj_dx", dyb, w["rwkv_proj"], "nt", F32, tm, 512, D)
    d_rwkv_proj = matmul("rwkv_proj_dw", ob, dyb, "tn", F32, 512, 512, tkt)
    (d_o, d_og, d_y, d_r_post, d_kp_post, d_v_post, d_g), d_post = rowwise_bwd(
        "branch_post_bwd", f_branch_post, post_rows, post_params, [[(d_oa, 512, 0)], [(d_ob, 512, 0)]], tr)
    late_grads = {"gla_proj": d_gla_proj, "rwkv_proj": d_rwkv_proj, "w_out": d_w_out, "ffn_up": ff_deinterleave(d_ffn_up_p),
                  "ffn_conv_w": ff_deinterleave(d_conv_w_p), "ffn_down": d_ffn_down_p[0:D_FF]}
    comm = [] if late_blocks is None else [(_full_to_slices(late_grads[k], SHARDED[k]), True) for k in LATE]
    dsc = rwkv_scan_bwd(*sc, hist_rw, sa_sc, to_scan("to_scan_dy", d_y, 0, nb, t, True), comm=comm)
    received = dict(zip(LATE, dsc[6:]))
    d_r_scan, d_kp_scan, d_a_scan, d_b_scan = from_scan("from_scan_rkab", dsc[0], dsc[1], nb, t, False)
    d_wf, d_wb = from_scan("from_scan_w", dsc[2], dsc[3], nb, t, False, groups=(0, 1))
    (d_v_scan,) = from_scan("from_scan_dv", dsc[4], dsc[5], nb, t, True)
    (d_k, d_wlal, d_gl), d_pre = rowwise_bwd(
        "rwkv_pre_bwd", f_rwkv_pre, pre_rows, pre_params,
        [[(d_wf, 512, 0)], [(d_wb, 512, 0)], [(d_kp_scan, 512, 0), (d_kp_post, 512, 0)],
         [(d_a_scan, 512, 0)], [(d_b_scan, 512, 0)], [(d_g, 512, 0)]], tr)
    ds = jnp.concatenate([d_r_scan + d_r_post, d_k, d_v_scan + d_v_post, d_wlal, d_gl], axis=1)
    dp_rw, d_mu_prev, d_mu_next = shift_bwd(p, ds, mu_prev, mu_next, nb, t)
    dqkv_f, d_wa2_f, d_ba_f = gla_bwd(p, wa2_f, ba_f, hist_f, d_o, None, nb, t, False)
    dqkv, d_wa2_b, d_ba_b = gla_bwd(p, wa2_b, ba_b, hist_b, d_o, dqkv_f, nb, t, True)
    dp = jnp.concatenate([dga, dgb, dqkv[:, 0:1024], d_og, dp_rw, dqkv[:, 1024:1152],
                          jnp.zeros((n, NP - C_AFAB - LANE), F32)], axis=1).astype(MXU_DTYPE)
    d_w_in_p = matmul("proj_in_dw", h1, dp, "tn", F32, D, 512, tkt)
    grads = {
        "w_in": w_in_from_padded(d_w_in_p),
        "gla_wa2_f": d_wa2_f[0:GLA_RANK], "gla_ba_f": d_ba_f, "gla_wa2_b": d_wa2_b[GLA_RANK:2 * GLA_RANK], "gla_ba_b": d_ba_b,
        "gla_norm_g": d_post[0], "rwkv_mu_prev": d_mu_prev, "rwkv_mu_next": d_mu_next,
        "rwkv_w0_f": d_pre[0], "rwkv_w2_f": d_pre[1][0:64], "rwkv_w0_b": d_pre[2], "rwkv_w2_b": d_pre[3][0:64],
        "rwkv_a0": d_pre[4], "rwkv_a2": d_pre[5][64:128], "rwkv_g2": d_pre[6], "rwkv_k_k": d_pre[7], "rwkv_k_a": d_pre[8],
        "rwkv_r_k": d_post[3], "rwkv_ln_w": d_post[1], "rwkv_ln_b": d_post[2],
        "norm2_g": dg2, "ffn_conv_b": ff_deinterleave(d_conv_b_p), "norm_f_g": dgf, **late_grads,
    }
    early = [k for k in SHARDED if k not in LATE]
    payload = lambda k: _full_to_slices(grads[k], SHARDED[k]).astype(MXU_DTYPE if k == "w_in" else F32)
    comm = [] if late_blocks is None else [(payload(k), True) for k in early]
    dh1, *got = matmul("proj_in_dx", dp, w_in_p, "nt", F32, tm, D, FFP // 2, comm=comm) if comm else \
        [matmul("proj_in_dx", dp, w_in_p, "nt", F32, tm, D, FFP // 2)]
    received.update(zip(early, got))
    (grad_x,), (grads["norm1_g"],) = rowwise_bwd("norm1_bwd", f_norm, [(x, D, 0)], [g1], [[(dh1, D, 0)]], tr,
                                                 adds=[(0, (dx1, D, 0))])
    return loss, grad_x, grads, received


MESH = pl.DeviceIdType.MESH


def remote_exchange(name, items):
    n = len(items)

    def body(*refs):
        start, wait = _exchange_plan([sc for _, sc in items], refs[:n], refs[n:2 * n], *refs[2 * n:])
        start()
        wait()

    args, specs, shapes, sems = _comm_specs(items)
    return pl.pallas_call(body, name=name, in_specs=specs, out_specs=specs, out_shape=shapes, scratch_shapes=sems)(*args)


def gather_two_level(name, blocks):
    n = len(blocks)

    def body(*refs):
        in_refs, out_refs = refs[:n], refs[n:2 * n]
        send_sems, recv_sems, local_sems = refs[2 * n:]
        x, y, c = lax.axis_index("x"), lax.axis_index("y"), lax.axis_index("c")
        me, sibling = (x, y, c), (x, y, 1 - c)
        chips = [(1 - x, y), (x, 1 - y), (1 - x, 1 - y)]

        def copy(i, k, block, to, src=None):
            rows = out_refs[i].at[4 * block[0] + 2 * block[1] + block[2]]
            return pltpu.make_async_remote_copy(src_ref=rows if src is None else src, dst_ref=rows, send_sem=send_sems.at[i, k],
                                                recv_sem=recv_sems.at[i, k], device_id=to, device_id_type=MESH)

        own = [pltpu.make_async_copy(in_refs[i], out_refs[i].at[4 * x + 2 * y + c], local_sems.at[i]) for i in range(n)]
        first = [copy(i, 0, me, sibling, src=in_refs[i]) for i in range(n)]
        first += [copy(i, 1 + j, me, (*chip, c), src=in_refs[i]) for j, chip in enumerate(chips) for i in range(n)]
        for cp in own + first:
            cp.start()
        passed = []
        for j, chip in enumerate(chips):
            for i in range(n):
                copy(i, 1 + j, (*chip, c), me).wait_recv()
                onward = copy(i, 4 + j, (*chip, c), sibling)
                onward.start()
                passed.append(onward)
        for i in range(n):
            copy(i, 0, sibling, me).wait_recv()
        for j, chip in enumerate(chips):
            for i in range(n):
                copy(i, 4 + j, (*chip, 1 - c), me).wait_recv()
        for cp in first + passed:
            cp.wait_send()
        for cp in own:
            cp.wait()

    args, specs, shapes, sems = _comm_specs([(b, False) for b in blocks])
    return pl.pallas_call(body, name=name, in_specs=specs, out_specs=specs, out_shape=shapes, scratch_shapes=sems)(*args)


def _exchange_plan(flags, in_refs, out_refs, send_sems, recv_sems, local_sems):
    x, y, c = lax.axis_index("x"), lax.axis_index("y"), lax.axis_index("c")
    me = 4 * x + 2 * y + c

    def peer(k):
        px = 1 - x if (k >> 2) & 1 else x
        py = 1 - y if (k >> 1) & 1 else y
        pc = 1 - c if k & 1 else c
        return (px, py, pc), 4 * px + 2 * py + pc

    def copies(with_arrivals):
        own, sends, recvs = [], [], []
        for i, scatter in enumerate(flags):
            src = in_refs[i].at[me] if scatter else in_refs[i]
            own.append(pltpu.make_async_copy(src, out_refs[i].at[me], local_sems.at[i]))
        for k in range(1, N_DEV):
            dev, slot = peer(k)
            for i, scatter in enumerate(flags):
                src = in_refs[i].at[slot] if scatter else in_refs[i]
                pair = dict(send_sem=send_sems.at[i, k - 1], recv_sem=recv_sems.at[i, k - 1], device_id=dev, device_id_type=MESH)
                sends.append(pltpu.make_async_remote_copy(src_ref=src, dst_ref=out_refs[i].at[me], **pair))
                if with_arrivals:
                    recvs.append(pltpu.make_async_remote_copy(src_ref=out_refs[i].at[slot], dst_ref=out_refs[i].at[slot], **pair))
        return own, sends, recvs

    def start():
        own, sends, _ = copies(False)
        for cp in own + sends:
            cp.start()

    def wait():
        own, sends, recvs = copies(True)
        for send, recv in zip(sends, recvs):
            recv.wait_recv()
            send.wait_send()
        for cp in own:
            cp.wait()

    return start, wait


def _adam_tiles(r, c):
    tc = 256 if (c % 256 == 0 and r * c > 128 * 1024) else c
    tr = 128 if (r % 128 == 0 and r > 128) else r
    return tr, tc


def adamw_reduce(name, parts, w, m, v):
    lead = w.ndim - 2
    r, c = w.shape[lead:]
    tr, tc = _adam_tiles(r, c)

    def body(p_ref, w_ref, m_ref, v_ref, g_ref, d_ref, nm_ref, nv_ref):
        g = p_ref[0].astype(F32)
        for d in range(1, N_DEV):
            g = g + p_ref[d].astype(F32)
        at = (0,) * lead + (slice(None), slice(None))
        nm = ADAM_B1 * m_ref[at] + (1.0 - ADAM_B1) * g
        nv = ADAM_B2 * v_ref[at] + (1.0 - ADAM_B2) * (g * g)
        m_hat = nm / (1.0 - ADAM_B1 ** ADAM_STEP)
        v_hat = nv / (1.0 - ADAM_B2 ** ADAM_STEP)
        g_ref[at] = g
        d_ref[at] = -ADAM_LR * (m_hat / (jnp.sqrt(v_hat) + ADAM_EPS) + ADAM_WD * w_ref[at])
        nm_ref[at] = nm
        nv_ref[at] = nv

    spec = pl.BlockSpec((1,) * lead + (tr, tc), lambda i, j: (0,) * lead + (i, j))
    return pl.pallas_call(
        body, name=name, grid=(r // tr, c // tc),
        in_specs=[pl.BlockSpec((N_DEV, tr, tc), lambda i, j: (0, i, j)), spec, spec, spec],
        out_specs=[spec] * 4, out_shape=[jax.ShapeDtypeStruct(w.shape, F32)] * 4,
        compiler_params=_cparams(("arbitrary", "arbitrary")),
    )(parts, w, m, v)


SHARDED = {"w_in": 1, "gla_wa2_f": 1, "gla_wa2_b": 1, "gla_proj": 1, "rwkv_w2_f": 1, "rwkv_w2_b": 1, "rwkv_a2": 1,
           "rwkv_g2": 1, "rwkv_proj": 1, "w_out": 0, "ffn_up": 1, "ffn_conv_w": 1, "ffn_down": 0}
BF16_GATHER = ("w_in", "gla_proj", "rwkv_proj", "w_out", "ffn_up", "ffn_down")
REPLICATED = ("norm1_g", "gla_ba_f", "gla_ba_b", "gla_norm_g", "rwkv_mu_prev", "rwkv_mu_next", "rwkv_w0_f", "rwkv_w0_b",
              "rwkv_a0", "rwkv_k_k", "rwkv_k_a", "rwkv_r_k", "rwkv_ln_w", "rwkv_ln_b", "norm2_g", "ffn_conv_b", "norm_f_g")
WEIGHTS = ("norm1_g", "w_in", "gla_wa2_f", "gla_ba_f", "gla_wa2_b", "gla_ba_b", "gla_norm_g", "gla_proj", "rwkv_mu_prev",
           "rwkv_mu_next", "rwkv_w0_f", "rwkv_w2_f", "rwkv_w0_b", "rwkv_w2_b", "rwkv_a0", "rwkv_a2", "rwkv_g2", "rwkv_k_k",
           "rwkv_k_a", "rwkv_r_k", "rwkv_ln_w", "rwkv_ln_b", "rwkv_proj", "w_out", "norm2_g", "ffn_up", "ffn_conv_w",
           "ffn_conv_b", "ffn_down", "norm_f_g")


def _gathered_to_full(g, axis):
    if axis == 0:
        return g.reshape(N_DEV * g.shape[1], g.shape[2])
    return g.transpose(1, 0, 2).reshape(g.shape[1], N_DEV * g.shape[2])


def _full_to_slices(a, axis):
    if axis == 0:
        return a.reshape(N_DEV, a.shape[0] // N_DEV, a.shape[1])
    return a.reshape(a.shape[0], N_DEV, a.shape[1] // N_DEV).transpose(1, 0, 2)


def _pack_rows(size):
    return -(-size // (8 * LANE)) * 8


def _pack(d):
    parts = []
    for k in REPLICATED:
        rows = d[k].reshape(-1, LANE).astype(F32)
        parts.append(jnp.pad(rows, ((0, _pack_rows(rows.size) - rows.shape[0]), (0, 0))))
    return jnp.concatenate(parts, axis=0)


def _unpack(packed, shapes):
    out, pos = {}, 0
    for k in REPLICATED:
        size = int(np.prod(shapes[k]))
        out[k] = packed[pos:pos + size // LANE].reshape(shapes[k])
        pos += _pack_rows(size)
    return out


def kernel(x, norm1_g, w_in, gla_wa2_f, gla_ba_f, gla_wa2_b, gla_ba_b, gla_norm_g, gla_proj, rwkv_mu_prev, rwkv_mu_next, rwkv_w0_f, rwkv_w2_f, rwkv_w0_b, rwkv_w2_b, rwkv_a0, rwkv_a2, rwkv_g2, rwkv_k_k, rwkv_k_a, rwkv_r_k, rwkv_ln_w, rwkv_ln_b, rwkv_proj, w_out, norm2_g, ffn_up, ffn_conv_w, ffn_conv_b, ffn_down, norm_f_g, loss_target, m_norm1_g, m_w_in, m_gla_wa2_f, m_gla_ba_f, m_gla_wa2_b, m_gla_ba_b, m_gla_norm_g, m_gla_proj, m_rwkv_mu_prev, m_rwkv_mu_next, m_rwkv_w0_f, m_rwkv_w2_f, m_rwkv_w0_b, m_rwkv_w2_b, m_rwkv_a0, m_rwkv_a2, m_rwkv_g2, m_rwkv_k_k, m_rwkv_k_a, m_rwkv_r_k, m_rwkv_ln_w, m_rwkv_ln_b, m_rwkv_proj, m_w_out, m_norm2_g, m_ffn_up, m_ffn_conv_w, m_ffn_conv_b, m_ffn_down, m_norm_f_g, v_norm1_g, v_w_in, v_gla_wa2_f, v_gla_ba_f, v_gla_wa2_b, v_gla_ba_b, v_gla_norm_g, v_gla_proj, v_rwkv_mu_prev, v_rwkv_mu_next, v_rwkv_w0_f, v_rwkv_w2_f, v_rwkv_w0_b, v_rwkv_w2_b, v_rwkv_a0, v_rwkv_a2, v_rwkv_g2, v_rwkv_k_k, v_rwkv_k_a, v_rwkv_r_k, v_rwkv_ln_w, v_rwkv_ln_b, v_rwkv_proj, v_w_out, v_norm2_g, v_ffn_up, v_ffn_conv_w, v_ffn_conv_b, v_ffn_down, v_norm_f_g):
    args = locals()
    wts = {k: args[k] for k in WEIGHTS}
    mom = {k: args["m_" + k] for k in WEIGHTS}
    var = {k: args["v_" + k] for k in WEIGHTS}
    shapes = {k: wts[k].shape for k in WEIGHTS}
    nb, t = x.shape[0], x.shape[1]
    mat = lambda a: a.reshape(a.shape[-2], a.shape[-1])

    block = lambda k: mat(wts[k]).astype(MXU_DTYPE) if k in BF16_GATHER else mat(wts[k])
    early = [k for k in SHARDED if k not in LATE]
    gathered = gather_two_level("gather_weights", [block(k) for k in early])
    full = {k: _gathered_to_full(g, SHARDED[k]) for k, g in zip(early, gathered)}
    for k in REPLICATED:
        full[k] = wts[k].reshape(-1) if k in ("norm_f_g", "rwkv_r_k") else wts[k][0]

    loss, grad_x, grads, received = local_step(x.reshape(nb * t, D), loss_target.reshape(nb * t, D), full, nb, t,
                                               late_blocks={k: block(k) for k in LATE})

    (rep_parts,) = remote_exchange("exchange_replicated", [(_pack(grads), False)])

    res = {}
    for k in SHARDED:
        res[k] = adamw_reduce("adamw_" + k, received[k], wts[k], mom[k], var[k])
    packed = adamw_reduce("adamw_replicated", rep_parts, _pack(wts), _pack(mom), _pack(var))
    unpacked = [_unpack(p, shapes) for p in packed]
    for k in REPLICATED:
        res[k] = [u[k] for u in unpacked]

    total = lax.psum(loss[0, 0], ("x", "y", "c"))
    out = [total, grad_x.reshape(x.shape)]
    for j in range(4):
        out += [res[k][j] for k in WEIGHTS]
    return tuple(out)
```

```python
import functools

import jax
import jax.numpy as jnp
import numpy as np
from jax import lax
from jax.experimental import pallas as pl
from jax.experimental.pallas import tpu as pltpu

F32 = jnp.float32
MXU_DTYPE = jnp.bfloat16

D = 1024
SEQ = 2048
GLA_H, GLA_DK, GLA_DV, GLA_CHUNK = 4, 64, 128, 64
GLA_RANK = 16
GLA_LOGIT_NORM = 16.0
RW_H, RW_N = 8, 64
RW_W = 512
D_FF = 2752
NORM_EPS = 1e-6
HEAD_NORM_EPS = 1e-5
RW_GN_EPS = RW_N * 1e-5
N_DEV = 8
ADAM_LR, ADAM_B1, ADAM_B2, ADAM_EPS, ADAM_WD, ADAM_STEP = 0.001, 0.9, 0.999, 1e-08, 0.01, 10

C_GA, C_GB, C_Q, C_K, C_V, C_OG = 0, 1024, 2048, 2304, 2560, 3072
C_RW = 3584
C_R, C_RK, C_RV, C_WLAL, C_GL = 3584, 4096, 4608, 5120, 5248
C_AFAB = 5376
NP = 5632
RW_PW = 1792
FFP = 2816
LANE = 128
VMEM_LIMIT = 56 * 1024 * 1024


def _cparams(sem):
    return pltpu.CompilerParams(dimension_semantics=sem, vmem_limit_bytes=VMEM_LIMIT)


@jax.custom_vjp
def mm(a, b):
    return jnp.dot(a.astype(MXU_DTYPE), b.astype(MXU_DTYPE), preferred_element_type=F32)


def _mm_fwd(a, b):
    return mm(a, b), (a, b)


def _mm_bwd(res, g):
    a, b = res
    gb = g.astype(MXU_DTYPE)
    da = lax.dot_general(gb, b.astype(MXU_DTYPE), (((1,), (1,)), ((), ())), preferred_element_type=F32)
    db = lax.dot_general(a.astype(MXU_DTYPE), gb, (((0,), (0,)), ((), ())), preferred_element_type=F32)
    return da.astype(a.dtype), db.astype(b.dtype)


mm.defvjp(_mm_fwd, _mm_bwd)


@jax.custom_vjp
def mm_nt(a, b):
    return lax.dot_general(a.astype(MXU_DTYPE), b.astype(MXU_DTYPE), (((1,), (1,)), ((), ())), preferred_element_type=F32)


def _mm_nt_fwd(a, b):
    return mm_nt(a, b), (a, b)


def _mm_nt_bwd(res, g):
    a, b = res
    gb = g.astype(MXU_DTYPE)
    da = jnp.dot(gb, b.astype(MXU_DTYPE), preferred_element_type=F32)
    db = lax.dot_general(gb, a.astype(MXU_DTYPE), (((0,), (0,)), ((), ())), preferred_element_type=F32)
    return da.astype(a.dtype), db.astype(b.dtype)


mm_nt.defvjp(_mm_nt_fwd, _mm_nt_bwd)


@jax.custom_vjp
def mm_tn(a, b):
    return lax.dot_general(a.astype(MXU_DTYPE), b.astype(MXU_DTYPE), (((0,), (0,)), ((), ())), preferred_element_type=F32)


def _mm_tn_fwd(a, b):
    return mm_tn(a, b), (a, b)


def _mm_tn_bwd(res, g):
    a, b = res
    gb = g.astype(MXU_DTYPE)
    da = lax.dot_general(b.astype(MXU_DTYPE), gb, (((1,), (1,)), ((), ())), preferred_element_type=F32)
    db = jnp.dot(a.astype(MXU_DTYPE), gb, preferred_element_type=F32)
    return da.astype(a.dtype), db.astype(b.dtype)


mm_tn.defvjp(_mm_tn_fwd, _mm_tn_bwd)


@functools.partial(jax.custom_vjp, nondiff_argnums=(2, 3))
def sel_dot(x, s, dims, x_first):
    sb = s.astype(MXU_DTYPE)
    hi = x.astype(MXU_DTYPE)
    r1 = x - hi.astype(F32)
    mid = r1.astype(MXU_DTYPE)
    lo = (r1 - mid.astype(F32)).astype(MXU_DTYPE)
    out = None
    for part in (hi, mid, lo):
        ops = (part, sb) if x_first else (sb, part)
        d = lax.dot_general(*ops, (dims, ((), ())), preferred_element_type=F32)
        out = d if out is None else out + d
    return out


def _sel_dot_fwd(x, s, dims, x_first):
    return sel_dot(x, s, dims, x_first), s


def _sel_dot_bwd(dims, x_first, s, g):
    if x_first:
        (cx,), (cs,) = dims
        dx = sel_dot(g, s, ((1,), (1 - cs,)), True) if cx == 1 else sel_dot(g, s, ((1 - cs,), (1,)), False)
    else:
        (cs,), (cx,) = dims
        dx = sel_dot(g, s, ((1 - cs,), (0,)), False) if cx == 0 else sel_dot(g, s, ((0,), (1 - cs,)), True)
    return dx, jnp.zeros_like(s)


sel_dot.defvjp(_sel_dot_fwd, _sel_dot_bwd)


def mm_exact(a, b, b_is_01=True):
    return sel_dot(a, b, ((1,), (0,)), True) if b_is_01 else sel_dot(b, a, ((1,), (0,)), False)


def mm_tn_exact(a, b):
    return sel_dot(a, b, ((0,), (0,)), True)


def _softplus(x):
    return jnp.maximum(x, 0.0) + jnp.log(1.0 + jnp.exp(-jnp.abs(x)))


def _sigmoid(x):
    return jax.nn.sigmoid(x)


def _silu(x):
    return x * _sigmoid(x)


def _rmsnorm(x, g):
    return x * lax.rsqrt(jnp.mean(x * x, axis=-1, keepdims=True) + NORM_EPS) * g


def _segment_sum(x, seg):
    width = x.shape[1]
    i = lax.broadcasted_iota(jnp.int32, (width, width), 0) // seg
    j = lax.broadcasted_iota(jnp.int32, (width, width), 1) // seg
    return mm_exact(x, (i == j).astype(F32))


def _row_spec(tm, width, cb):
    return pl.BlockSpec((tm, width), lambda i: (i, cb))


def _full_spec(shape):
    nd = len(shape)
    return pl.BlockSpec(tuple(shape), lambda i: (0,) * nd)


def rowwise_fwd(name, f, rows, params, outs, tm):
    n = rows[0][0].shape[0]
    nr, npar = len(rows), len(params)

    def body(*refs):
        rv = [r[...] for r in refs[:nr]]
        pv = [r[...] for r in refs[nr:nr + npar]]
        res = f(rv, pv)
        for o_ref, val in zip(refs[nr + npar:], res):
            o_ref[...] = val.astype(o_ref.dtype)

    return pl.pallas_call(
        body, name=name, grid=(n // tm,),
        in_specs=[_row_spec(tm, w, cb) for _, w, cb in rows] + [_full_spec(p.shape) for p in params],
        out_specs=[_row_spec(tm, w, 0) for w, _ in outs],
        out_shape=[jax.ShapeDtypeStruct((n, w), dt) for w, dt in outs],
        compiler_params=_cparams(("arbitrary",)),
    )(*[a for a, _, _ in rows], *params)


def rowwise_bwd(name, f, rows, params, douts, tm, adds=(), grad_rows=None):
    n = rows[0][0].shape[0]
    nr, npar = len(rows), len(params)
    grad_rows = list(range(nr)) if grad_rows is None else list(grad_rows)
    flat_d = [d for group in douts for d in group]
    nd, na, ng = len(flat_d), len(adds), len(grad_rows)

    def body(*refs):
        rv = [r[...] for r in refs[:nr]]
        pv = [r[...] for r in refs[nr:nr + npar]]
        dflat = [r[...].astype(F32) for r in refs[nr + npar:nr + npar + nd]]
        av = [r[...] for r in refs[nr + npar + nd:nr + npar + nd + na]]
        o = nr + npar + nd + na
        drow_refs, dpar_refs = refs[o:o + ng], refs[o + ng:o + ng + npar]
        dv, pos = [], 0
        for group in douts:
            dv.append(sum(dflat[pos + 1:pos + len(group)], dflat[pos]))
            pos += len(group)

        @pl.when(pl.program_id(0) == 0)
        def _():
            for r in dpar_refs:
                r[...] = jnp.zeros_like(r)

        def g(grows, pars):
            full = list(rv)
            for i, val in zip(grad_rows, grows):
                full[i] = val
            return f(full, pars)

        res, vjp = jax.vjp(g, [rv[i] for i in grad_rows], pv)
        drows, dpars = vjp([d.astype(r.dtype) for d, r in zip(dv, res)])
        drows = [d.astype(F32) for d in drows]
        for (idx, _), a in zip(adds, av):
            drows[idx] = drows[idx] + a.astype(F32)
        for r, d in zip(drow_refs, drows):
            r[...] = d
        for r, d in zip(dpar_refs, dpars):
            r[...] += d.astype(F32)

    res = pl.pallas_call(
        body, name=name, grid=(n // tm,),
        in_specs=[_row_spec(tm, w, cb) for _, w, cb in rows] + [_full_spec(p.shape) for p in params]
        + [_row_spec(tm, w, cb) for _, w, cb in flat_d] + [_row_spec(tm, w, cb) for _, (_, w, cb) in adds],
        out_specs=[_row_spec(tm, rows[i][1], 0) for i in grad_rows] + [_full_spec(p.shape) for p in params],
        out_shape=[jax.ShapeDtypeStruct((n, rows[i][1]), F32) for i in grad_rows]
        + [jax.ShapeDtypeStruct(p.shape, F32) for p in params],
        compiler_params=_cparams(("arbitrary",)),
    )(*[a for a, _, _ in rows], *params, *[a for a, _, _ in flat_d], *[a for _, (a, _, _) in adds])
    return res[:ng], res[ng:]


def matmul(name, a, b, mode, out_dtype, tm, tn, tk, comm=()):
    nc = len(comm)
    flags = [sc for _, sc in comm]
    if mode == "nn":
        (m, k), n = a.shape, b.shape[1]
        a_spec = pl.BlockSpec((tm, tk), lambda i, j, kk: (i, kk))
        b_spec = pl.BlockSpec((tk, tn), lambda i, j, kk: (kk, j))
        dims = (((1,), (0,)), ((), ()))
    elif mode == "nt":
        (m, k), n = a.shape, b.shape[0]
        a_spec = pl.BlockSpec((tm, tk), lambda i, j, kk: (i, kk))
        b_spec = pl.BlockSpec((tn, tk), lambda i, j, kk: (j, kk))
        dims = (((1,), (1,)), ((), ()))
    else:
        (k, m), n = a.shape, b.shape[1]
        a_spec = pl.BlockSpec((tk, tm), lambda i, j, kk: (kk, i))
        b_spec = pl.BlockSpec((tk, tn), lambda i, j, kk: (kk, j))
        dims = (((0,), (0,)), ((), ()))
    assert m % tm == 0 and n % tn == 0 and k % tk == 0, (name, a.shape, b.shape, tm, tn, tk)
    nk = k // tk
    grid = (m // tm, n // tn, nk)

    def body(*refs):
        a_ref, b_ref, c_in, o_ref = refs[0], refs[1], refs[2:2 + nc], refs[2 + nc]
        c_out, acc_ref, sems = refs[3 + nc:3 + 2 * nc], refs[3 + 2 * nc], refs[4 + 2 * nc:]
        kk = pl.program_id(2)
        step = (pl.program_id(0) * grid[1] + pl.program_id(1)) * nk + kk
        if nc:
            start, wait = _exchange_plan(flags, c_in, c_out, *sems)

            @pl.when(step == 0)
            def _():
                start()

        part = lax.dot_general(a_ref[...].astype(MXU_DTYPE), b_ref[...].astype(MXU_DTYPE), dims, preferred_element_type=F32)
        if nk == 1:
            o_ref[...] = part.astype(o_ref.dtype)
        else:
            @pl.when(kk == 0)
            def _():
                acc_ref[...] = part

            @pl.when((kk > 0) & (kk < nk - 1))
            def _():
                acc_ref[...] += part

            @pl.when(kk == nk - 1)
            def _():
                o_ref[...] = (acc_ref[...] + part).astype(o_ref.dtype)

        if nc:
            @pl.when(step == grid[0] * grid[1] * nk - 1)
            def _():
                wait()

    c_args, c_specs, c_shapes, c_sems = _comm_specs(comm)
    res = pl.pallas_call(
        body, name=name, grid=grid,
        in_specs=[a_spec, b_spec] + c_specs,
        out_specs=[pl.BlockSpec((tm, tn), lambda i, j, kk: (i, j))] + c_specs,
        out_shape=[jax.ShapeDtypeStruct((m, n), out_dtype)] + c_shapes,
        scratch_shapes=[pltpu.VMEM((tm, tn) if nk > 1 else (8, LANE), F32)] + c_sems,
        compiler_params=_cparams(("arbitrary", "arbitrary", "arbitrary")),
    )(a, b, *c_args)
    return res if nc else res[0]


def _prev(u, first):
    return jnp.where(first, 0.0, pltpu.roll(u, 1, 0))


def _next(u, last):
    return jnp.where(last, 0.0, pltpu.roll(u, u.shape[0] - 1, 0))


def _edge_masks(t, w):
    row = lax.broadcasted_iota(jnp.int32, (t, w), 0)
    return row == 0, row == t - 1


WIN, HALO = 64, 8
MID = slice(HALO, HALO + WIN)


def _window(ref, i, t):
    r0 = pl.multiple_of(i * WIN, WIN)
    before = ref[pl.ds(pl.multiple_of(jnp.maximum(r0 - HALO, 0), HALO), HALO), :]
    after = ref[pl.ds(pl.multiple_of(jnp.minimum(r0 + WIN, t - HALO), HALO), HALO), :]
    before = jnp.where(i == 0, 0.0, before.astype(F32))
    after = jnp.where(i == t // WIN - 1, 0.0, after.astype(F32))
    return jnp.concatenate([before, ref[pl.ds(r0, WIN), :].astype(F32), after], axis=0)


def _wprev(u):
    return pltpu.roll(u, 1, 0)


def _wnext(u):
    return pltpu.roll(u, u.shape[0] - 1, 0)


def _mid_rows(i):
    return pl.ds(pl.multiple_of(i * WIN, WIN), WIN)


def _colsum(x):
    return jnp.sum(x[MID], axis=0, keepdims=True)


SHIFT_CW = 256


def shift_fwd(p, mu_prev, mu_next, nb, t):
    cw, c0 = SHIFT_CW, C_RW // SHIFT_CW

    def body(p_ref, mp_ref, mn_ref, s_ref):
        x = p_ref[...]
        first, last = _edge_masks(t, cw)
        s_ref[...] = x + mp_ref[...] * (_prev(x, first) - x) + mn_ref[...] * (_next(x, last) - x)

    return pl.pallas_call(
        body, name="rwkv_shift_fwd", grid=(nb, RW_PW // cw),
        in_specs=[pl.BlockSpec((t, cw), lambda b, j: (b, c0 + j)), pl.BlockSpec((1, cw), lambda b, j: (0, j)),
                  pl.BlockSpec((1, cw), lambda b, j: (0, j))],
        out_specs=pl.BlockSpec((t, cw), lambda b, j: (b, j)),
        out_shape=jax.ShapeDtypeStruct((nb * t, RW_PW), F32),
        compiler_params=_cparams(("arbitrary", "arbitrary")),
    )(p, mu_prev, mu_next)


def shift_bwd(p, ds, mu_prev, mu_next, nb, t):
    cw, c0 = SHIFT_CW, C_RW // SHIFT_CW

    def body(p_ref, ds_ref, mp_ref, mn_ref, dp_ref, dmp_ref, dmn_ref):
        @pl.when(pl.program_id(1) == 0)
        def _():
            dmp_ref[...] = jnp.zeros_like(dmp_ref)
            dmn_ref[...] = jnp.zeros_like(dmn_ref)

        mp, mn = mp_ref[...], mn_ref[...]

        def step(i, carry):
            dmp, dmn = carry
            x, g = _window(p_ref, i, t), _window(ds_ref, i, t)
            dp = g * (1.0 - mp - mn) + _wnext(mp * g) + _wprev(mn * g)
            dp_ref[_mid_rows(i), :] = dp[MID]
            return dmp + _colsum(g * (_wprev(x) - x)), dmn + _colsum(g * (_wnext(x) - x))

        zero = jnp.zeros((1, cw), F32)
        dmp, dmn = lax.fori_loop(0, t // WIN, step, (zero, zero))
        dmp_ref[...] += dmp
        dmn_ref[...] += dmn

    return pl.pallas_call(
        body, name="rwkv_shift_bwd", grid=(RW_PW // cw, nb),
        in_specs=[pl.BlockSpec((t, cw), lambda j, b: (b, c0 + j)), pl.BlockSpec((t, cw), lambda j, b: (b, j)),
                  pl.BlockSpec((1, cw), lambda j, b: (0, j)), pl.BlockSpec((1, cw), lambda j, b: (0, j))],
        out_specs=[pl.BlockSpec((t, cw), lambda j, b: (b, j)), pl.BlockSpec((1, cw), lambda j, b: (0, j)),
                   pl.BlockSpec((1, cw), lambda j, b: (0, j))],
        out_shape=[jax.ShapeDtypeStruct((nb * t, RW_PW), F32), jax.ShapeDtypeStruct((1, RW_PW), F32),
                   jax.ShapeDtypeStruct((1, RW_PW), F32)],
        compiler_params=_cparams(("arbitrary", "arbitrary")),
    )(p, ds, mu_prev, mu_next)


def conv_glu_fwd(u, cw, cb, nb, t):
    def body(u_ref, w_ref, b_ref, z_ref):
        x, w = u_ref[...], w_ref[...]
        first, last = _edge_masks(t, 2 * LANE)
        c = w[0:1] * _prev(x, first) + w[1:2] * x + w[2:3] * _next(x, last) + b_ref[...]
        z_ref[...] = (_silu(c[:, :LANE]) * c[:, LANE:]).astype(z_ref.dtype)

    return pl.pallas_call(
        body, name="conv_glu_fwd", grid=(nb, FFP // LANE),
        in_specs=[pl.BlockSpec((t, 2 * LANE), lambda b, j: (b, j)), pl.BlockSpec((3, 2 * LANE), lambda b, j: (0, j)),
                  pl.BlockSpec((1, 2 * LANE), lambda b, j: (0, j))],
        out_specs=pl.BlockSpec((t, LANE), lambda b, j: (b, j)),
        out_shape=jax.ShapeDtypeStruct((nb * t, FFP), MXU_DTYPE),
        compiler_params=_cparams(("arbitrary", "arbitrary")),
    )(u, cw, cb)


def conv_glu_bwd(u, dz, cw, cb, nb, t):
    def body(u_ref, dz_ref, w_ref, b_ref, du_ref, dw_ref, db_ref):
        @pl.when(pl.program_id(1) == 0)
        def _():
            dw_ref[...] = jnp.zeros_like(dw_ref)
            db_ref[...] = jnp.zeros_like(db_ref)

        w, bias = w_ref[...], b_ref[...]

        def step(i, carry):
            x, g = _window(u_ref, i, t), _window(dz_ref, i, t)
            xp, xn = _wprev(x), _wnext(x)
            c = w[0:1] * xp + w[1:2] * x + w[2:3] * xn + bias
            cg, cv = c[:, :LANE], c[:, LANE:]
            sg = _sigmoid(cg)
            dcg = g * cv * (sg * (1.0 + cg * (1.0 - sg)))
            dcv = g * (cg * sg)
            dc = jnp.concatenate([dcg, dcv], axis=1)
            du = w[1:2] * dc + _wnext(w[0:1] * dc) + _wprev(w[2:3] * dc)
            du_ref[_mid_rows(i), :] = du[MID].astype(du_ref.dtype)
            return tuple(acc + _colsum(val) for acc, val in zip(carry, (dc * xp, dc * x, dc * xn, dc)))

        zero = jnp.zeros((1, 2 * LANE), F32)
        sums = lax.fori_loop(0, t // WIN, step, (zero, zero, zero, zero))
        for row in range(3):
            dw_ref[row:row + 1, :] += sums[row]
        db_ref[...] += sums[3]

    return pl.pallas_call(
        body, name="conv_glu_bwd", grid=(FFP // LANE, nb),
        in_specs=[pl.BlockSpec((t, 2 * LANE), lambda j, b: (b, j)), pl.BlockSpec((t, LANE), lambda j, b: (b, j)),
                  pl.BlockSpec((3, 2 * LANE), lambda j, b: (0, j)), pl.BlockSpec((1, 2 * LANE), lambda j, b: (0, j))],
        out_specs=[pl.BlockSpec((t, 2 * LANE), lambda j, b: (b, j)), pl.BlockSpec((3, 2 * LANE), lambda j, b: (0, j)),
                   pl.BlockSpec((1, 2 * LANE), lambda j, b: (0, j))],
        out_shape=[jax.ShapeDtypeStruct((nb * t, 2 * FFP), MXU_DTYPE), jax.ShapeDtypeStruct((3, 2 * FFP), F32),
                   jax.ShapeDtypeStruct((1, 2 * FFP), F32)],
        compiler_params=_cparams(("arbitrary", "arbitrary")),
    )(u, dz, cw, cb)


def _gla_chunk(q, k, v, afab, wa2p, ba, s_in, reverse, sb):
    c = GLA_CHUNK
    r = sb * c
    ri = lax.broadcasted_iota(jnp.int32, (r, r), 0)
    ci = lax.broadcasted_iota(jnp.int32, (r, r), 1)
    same = (ri // c) == (ci // c)
    keep = same & ((ci >= ri) if reverse else (ci <= ri))
    i_ref = (c - 1 - c // 2) if reverse else (c // 2)
    pick_ref = (ci == (ri // c) * c + i_ref).astype(F32)
    seq_cols = (lax.broadcasted_iota(jnp.int32, (r, sb * LANE), 0) // c) == (lax.broadcasted_iota(jnp.int32, (r, sb * LANE), 1) // LANE)
    expand = lambda x: jnp.where(seq_cols, jnp.concatenate([x] * sb, axis=1), 0.0)
    lane = lax.broadcasted_iota(jnp.int32, (1, LANE), 1)
    outs, states = [None] * GLA_H, [None] * GLA_H
    for pr in range(GLA_H // 2):
        la = -_softplus(-(mm(afab, wa2p[pr]) + ba[pr])) * (1.0 / GLA_LOGIT_NORM)
        b = mm_exact(keep.astype(F32), la, b_is_01=False)
        b_ref = mm_exact(pick_ref, b, b_is_01=False)
        b_last = mm_exact(same.astype(F32), la, b_is_01=False)
        qs = q[pr] * (GLA_DK ** -0.5)
        qi = qs * jnp.exp(b - b_ref)
        ki = k[pr] * jnp.exp(b_ref - b)
        kd = k[pr] * jnp.exp(b_last - b)
        qb = qs * jnp.exp(b)
        dec = jnp.exp(mm_tn_exact(expand(la), jnp.ones((r, LANE), F32)))
        for h in (2 * pr, 2 * pr + 1):
            m = ((lane // GLA_DK) == (h % 2)).astype(F32)
            a = jnp.where(keep, mm_nt(qi * m, ki), 0.0)
            o_intra = mm(a, v[h])
            kv = mm_tn(expand(kd * m), v[h])
            o_inter = mm(expand(qb * m), s_in[h])
            outs[h] = o_intra + o_inter
            states[h] = s_in[h] * dec + kv
    return outs, states


def _gla_load(q_ref, k_ref, v_ref, af_ref, w_ref, ba_ref, sb, rows):
    stack = lambda ref, c0: jnp.concatenate([ref[s, rows, c0:c0 + LANE] for s in range(sb)], axis=0)
    q = [stack(q_ref, pr * LANE) for pr in range(GLA_H // 2)]
    k = [stack(k_ref, pr * LANE) for pr in range(GLA_H // 2)]
    v = [stack(v_ref, h * GLA_DV) for h in range(GLA_H)]
    w = [w_ref[:, pr * LANE:(pr + 1) * LANE] for pr in range(GLA_H // 2)]
    ba = [ba_ref[:, pr * LANE:(pr + 1) * LANE] for pr in range(GLA_H // 2)]
    return q, k, v, stack(af_ref, 0), w, ba


GLA_TILE = 256
GLA_SB = 4


def _gla_specs(nb, t, reverse):
    tile = min(GLA_TILE, t)
    nt = t // tile
    sb = GLA_SB if nb % GLA_SB == 0 else 1
    return tile, tile // GLA_CHUNK, nt, sb, ((lambda j: nt - 1 - j) if reverse else (lambda j: j))


def gla_fwd(p, wa2p, ba, o_add, nb, t, reverse):
    tile, cpt, nt, sb, tj = _gla_specs(nb, t, reverse)
    has_add = o_add is not None

    def body(*refs):
        if has_add:
            q_ref, k_ref, v_ref, af_ref, w_ref, ba_ref, add_ref, o_ref, hist_ref, s_ref = refs
        else:
            q_ref, k_ref, v_ref, af_ref, w_ref, ba_ref, o_ref, hist_ref, s_ref = refs

        @pl.when(pl.program_id(1) == 0)
        def _():
            s_ref[...] = jnp.zeros_like(s_ref)

        def step(i, carry):
            ci = (cpt - 1 - i) if reverse else i
            rows = pl.ds(pl.multiple_of(ci * GLA_CHUNK, GLA_CHUNK), GLA_CHUNK)
            s_in = [s_ref[h] for h in range(GLA_H)]
            for h in range(GLA_H):
                for s in range(sb):
                    hist_ref[s, ci, h] = s_in[h][s * LANE:(s + 1) * LANE]
            q, k, v, af, w, ba = _gla_load(q_ref, k_ref, v_ref, af_ref, w_ref, ba_ref, sb, rows)
            outs, states = _gla_chunk(q, k, v, af, w, ba, s_in, reverse, sb)
            for h in range(GLA_H):
                for s in range(sb):
                    oh = outs[h][s * GLA_CHUNK:(s + 1) * GLA_CHUNK]
                    if has_add:
                        oh = oh + add_ref[s, rows, h * GLA_DV:(h + 1) * GLA_DV]
                    o_ref[s, rows, h * GLA_DV:(h + 1) * GLA_DV] = oh
                s_ref[h] = states[h]
            return carry

        lax.fori_loop(0, cpt, step, 0)

    col = lambda width, c0: pl.BlockSpec((sb, tile, width), lambda b, j: (b, tj(j), c0 // width))
    in_specs = [col(256, C_Q), col(256, C_K), col(512, C_V), col(LANE, C_AFAB),
                pl.BlockSpec((LANE, 256), lambda b, j: (0, 0)), pl.BlockSpec((1, 256), lambda b, j: (0, 0))]
    p3 = p.reshape(nb, t, p.shape[1])
    args = [p3, p3, p3, p3, wa2p, ba]
    if has_add:
        in_specs.append(col(512, 0))
        args.append(o_add.reshape(nb, t, 512))
    o, hist = pl.pallas_call(
        body, name="gla_fwd_rev" if reverse else "gla_fwd", grid=(nb // sb, nt),
        in_specs=in_specs,
        out_specs=[col(512, 0), pl.BlockSpec((sb, cpt, GLA_H, LANE, LANE), lambda b, j: (b, tj(j), 0, 0, 0))],
        out_shape=[jax.ShapeDtypeStruct((nb, t, 512), F32),
                   jax.ShapeDtypeStruct((nb, t // GLA_CHUNK, GLA_H, LANE, LANE), F32)],
        scratch_shapes=[pltpu.VMEM((GLA_H, sb * LANE, LANE), F32)],
        compiler_params=_cparams(("arbitrary", "arbitrary")),
    )(*args)
    return o.reshape(nb * t, 512), hist


def gla_bwd(p, wa2p, ba, hist, do, dprev, nb, t, reverse):
    tile, cpt, nt, sb, tj_f = _gla_specs(nb, t, reverse)
    tj = lambda j: tj_f(nt - 1 - j)
    has_prev = dprev is not None

    def body(*refs):
        if has_prev:
            q_ref, k_ref, v_ref, af_ref, w_ref, ba_ref, hist_ref, do_ref, prev_ref, dqkv_ref, dw_ref, dba_ref, ds_ref = refs
        else:
            q_ref, k_ref, v_ref, af_ref, w_ref, ba_ref, hist_ref, do_ref, dqkv_ref, dw_ref, dba_ref, ds_ref = refs

        @pl.when((pl.program_id(0) == 0) & (pl.program_id(1) == 0))
        def _():
            dw_ref[...] = jnp.zeros_like(dw_ref)
            dba_ref[...] = jnp.zeros_like(dba_ref)

        @pl.when(pl.program_id(1) == 0)
        def _():
            ds_ref[...] = jnp.zeros_like(ds_ref)

        def step(i, carry):
            ci = i if reverse else (cpt - 1 - i)
            rows = pl.ds(pl.multiple_of(ci * GLA_CHUNK, GLA_CHUNK), GLA_CHUNK)
            fn = functools.partial(_gla_chunk, reverse=reverse, sb=sb)
            seqs = lambda get: jnp.concatenate([get(s) for s in range(sb)], axis=0)
            s_in = [seqs(lambda s: hist_ref[s, ci, h]) for h in range(GLA_H)]
            q, k, v, af, w, ba = _gla_load(q_ref, k_ref, v_ref, af_ref, w_ref, ba_ref, sb, rows)
            _, vjp = jax.vjp(fn, q, k, v, af, w, ba, s_in)
            d_o = [seqs(lambda s: do_ref[s, rows, h * GLA_DV:(h + 1) * GLA_DV]) for h in range(GLA_H)]
            d_s = [ds_ref[h] for h in range(GLA_H)]
            dq, dk, dv, daf, dw, dba, ds_in = vjp((d_o, d_s))
            pieces = [(pr * LANE, dq[pr]) for pr in range(2)] + [(256 + pr * LANE, dk[pr]) for pr in range(2)]
            pieces += [(512 + h * GLA_DV, dv[h]) for h in range(GLA_H)] + [(1024, daf)]
            for c0, val in pieces:
                for s in range(sb):
                    part = val[s * GLA_CHUNK:(s + 1) * GLA_CHUNK]
                    if has_prev:
                        part = part + prev_ref[s, rows, c0:c0 + LANE]
                    dqkv_ref[s, rows, c0:c0 + LANE] = part
            for pr in range(2):
                dw_ref[:, pr * LANE:(pr + 1) * LANE] += dw[pr]
                dba_ref[:, pr * LANE:(pr + 1) * LANE] += dba[pr]
            for h in range(GLA_H):
                ds_ref[h] = ds_in[h]
            return carry

        lax.fori_loop(0, cpt, step, 0)

    col = lambda width, c0: pl.BlockSpec((sb, tile, width), lambda b, j: (b, tj(j), c0 // width))
    in_specs = [col(256, C_Q), col(256, C_K), col(512, C_V), col(LANE, C_AFAB),
                pl.BlockSpec((LANE, 256), lambda b, j: (0, 0)), pl.BlockSpec((1, 256), lambda b, j: (0, 0)),
                pl.BlockSpec((sb, cpt, GLA_H, LANE, LANE), lambda b, j: (b, tj(j), 0, 0, 0)), col(512, 0)]
    p3 = p.reshape(nb, t, p.shape[1])
    args = [p3, p3, p3, p3, wa2p, ba, hist, do.reshape(nb, t, 512)]
    if has_prev:
        in_specs.append(col(1152, 0))
        args.append(dprev.reshape(nb, t, 1152))
    dqkv, dw, dba = pl.pallas_call(
        body, name="gla_bwd_rev" if reverse else "gla_bwd", grid=(nb // sb, nt),
        in_specs=in_specs,
        out_specs=[col(1152, 0), pl.BlockSpec((LANE, 256), lambda b, j: (0, 0)), pl.BlockSpec((1, 256), lambda b, j: (0, 0))],
        out_shape=[jax.ShapeDtypeStruct((nb, t, 1152), F32), jax.ShapeDtypeStruct((LANE, 256), F32),
                   jax.ShapeDtypeStruct((1, 256), F32)],
        scratch_shapes=[pltpu.VMEM((GLA_H, sb * LANE, LANE), F32)],
        compiler_params=_cparams(("arbitrary", "arbitrary")),
    )(*args)
    return dqkv.reshape(nb * t, 1152), dw, dba


SCAN_TB = 16
RW_VH = RW_N // 2


def _bwd_lanes():
    lane = lax.broadcasted_iota(jnp.int32, (1, LANE), 1)
    return ((lane // (LANE // 4)) % 2) == 1


def _comm_specs(comm):
    anyspec = pl.BlockSpec(memory_space=pl.ANY)
    n = len(comm)
    shapes = [jax.ShapeDtypeStruct((N_DEV,) + (a.shape[1:] if sc else a.shape), a.dtype) for a, sc in comm]
    sems = [pltpu.SemaphoreType.DMA((n, N_DEV - 1)), pltpu.SemaphoreType.DMA((n, N_DEV - 1)), pltpu.SemaphoreType.DMA((n,))] if n else []
    return [a for a, _ in comm], [anyspec] * n, shapes, sems


def rwkv_scan_fwd(r, w, k, a, b, v, comm=()):
    t = r.shape[0]
    nt = t // SCAN_TB
    nc = len(comm)
    flags = [sc for _, sc in comm]

    def body(*refs):
        (rf, rm, kf, km, af, am, bf, bm, wf_ref, wm_ref, vf, vm), refs = refs[:12], refs[12:]
        c_in, refs = refs[:nc], refs[nc:]
        (yf_ref, ym_ref, hist_ref, sa_ref), refs = refs[:4], refs[4:]
        c_out, refs = refs[:nc], refs[nc:]
        s_ref, sems = refs[0], refs[1:]
        i = pl.program_id(0)
        if nc:
            start, wait = _exchange_plan(flags, c_in, c_out, *sems)

        @pl.when(i == 0)
        def _():
            s_ref[...] = jnp.zeros_like(s_ref)
            if nc:
                start()

        bwd = _bwd_lanes()

        for tt in range(SCAN_TB):
            mt = SCAN_TB - 1 - tt
            pick = lambda f_ref, m_ref: jnp.where(bwd, m_ref[mt], f_ref[tt])
            rt, kt, at, bt, wt = pick(rf, rm), pick(kf, km), pick(af, am), pick(bf, bm), pick(wf_ref, wm_ref)
            for vi in range(RW_VH):
                sv = s_ref[vi] if tt == 0 else hist_ref[tt - 1, vi]
                sa = jnp.sum(sv * at, axis=0, keepdims=True)
                v_row = jnp.where(bwd, vm[mt, vi:vi + 1, :], vf[tt, vi:vi + 1, :])
                sn = sv * wt + sa * bt + v_row * kt
                hist_ref[tt, vi] = sn
                y_row = jnp.sum(sn * rt, axis=0, keepdims=True)
                yf_ref[tt, vi:vi + 1, :] = y_row
                ym_ref[mt, vi:vi + 1, :] = y_row
                sa_ref[tt, vi:vi + 1, :] = sa
        s_ref[...] = hist_ref[SCAN_TB - 1]

        if nc:
            @pl.when(i == nt - 1)
            def _():
                wait()

    fwd_map, mir_map = (lambda i: (i, 0, 0)), (lambda i: (nt - 1 - i, 0, 0))
    kf_spec, km_spec = pl.BlockSpec((SCAN_TB, RW_N, LANE), fwd_map), pl.BlockSpec((SCAN_TB, RW_N, LANE), mir_map)
    vf_spec, vm_spec = pl.BlockSpec((SCAN_TB, RW_VH, LANE), fwd_map), pl.BlockSpec((SCAN_TB, RW_VH, LANE), mir_map)
    c_args, c_specs, c_shapes, c_sems = _comm_specs(comm)
    vshape = jax.ShapeDtypeStruct((t, RW_VH, LANE), F32)
    return pl.pallas_call(
        body, name="rwkv_scan_fwd", grid=(nt,),
        in_specs=[kf_spec, km_spec] * 5 + [vf_spec, vm_spec] + c_specs,
        out_specs=[vf_spec, vm_spec, pl.BlockSpec((SCAN_TB, RW_VH, RW_N, LANE), lambda i: (i, 0, 0, 0)), vf_spec] + c_specs,
        out_shape=[vshape, vshape, jax.ShapeDtypeStruct((t, RW_VH, RW_N, LANE), F32), vshape] + c_shapes,
        scratch_shapes=[pltpu.VMEM((RW_VH, RW_N, LANE), F32)] + c_sems,
        compiler_params=_cparams(("arbitrary",)),
    )(r, r, k, k, a, a, b, b, w, w, v, v, *c_args)


def rwkv_scan_bwd(r, w, k, a, b, v, hist, sa, dy, comm=()):
    t = r.shape[0]
    nt = t // SCAN_TB
    nc = len(comm)
    flags = [sc for _, sc in comm]

    def body(*refs):
        (rf, rm, kf, km, af, am, bf, bm, wf_ref, wm_ref, vf, vm, hist_ref, prev_ref, sa_ref, dyf, dym), refs = refs[:17], refs[17:]
        c_in, refs = refs[:nc], refs[nc:]
        k_outs, (dvf_ref, dvm_ref), refs = refs[:4], refs[4:6], refs[6:]
        c_out, refs = refs[:nc], refs[nc:]
        ds_ref, sems = refs[0], refs[1:]
        i = pl.program_id(0)
        if nc:
            start, wait = _exchange_plan(flags, c_in, c_out, *sems)

        @pl.when(i == 0)
        def _():
            ds_ref[...] = jnp.zeros_like(ds_ref)
            if nc:
                start()

        bwd = _bwd_lanes()
        group = lax.broadcasted_iota(jnp.int32, (1, LANE), 1) // RW_Q
        first_block = i == nt - 1

        for tt in range(SCAN_TB - 1, -1, -1):
            mt = SCAN_TB - 1 - tt
            pick = lambda f_ref, m_ref: jnp.where(bwd, m_ref[mt], f_ref[tt])
            rt, kt, at, bt, wt = pick(rf, rm), pick(kf, km), pick(af, am), pick(bf, bm), pick(wf_ref, wm_ref)
            zero = jnp.zeros((RW_N, LANE), F32)
            dr, dw, dk, da, db = zero, zero, zero, zero, zero
            for vi in range(RW_VH):
                sn = hist_ref[tt, vi]
                sv = hist_ref[tt - 1, vi] if tt > 0 else jnp.where(first_block, 0.0, prev_ref[0, vi])
                sa_row = sa_ref[tt, vi:vi + 1, :]
                v_row = jnp.where(bwd, vm[mt, vi:vi + 1, :], vf[tt, vi:vi + 1, :])
                dy_row = jnp.where(bwd, dym[mt, vi:vi + 1, :], dyf[tt, vi:vi + 1, :])
                dsv = ds_ref[vi] + dy_row * rt
                dr = dr + sn * dy_row
                dsa = jnp.sum(dsv * bt, axis=0, keepdims=True)
                dw = dw + sv * dsv
                db = db + dsv * sa_row
                dk = dk + dsv * v_row
                dv_row = jnp.sum(dsv * kt, axis=0, keepdims=True)
                dvf_ref[tt, vi:vi + 1, :] = dv_row
                dvm_ref[mt, vi:vi + 1, :] = dv_row
                da = da + sv * dsa
                ds_ref[vi] = dsv * wt + dsa * at
            dr, dw, dk, da, db = [val + pltpu.roll(val, LANE // 2, 1) for val in (dr, dw, dk, da, db)]
            up, down = (lambda val: pltpu.roll(val, RW_Q, 1)), (lambda val: pltpu.roll(val, LANE - RW_Q, 1))
            packed_f = jnp.where(group == 0, dr, jnp.where(group == 1, up(dk), jnp.where(group == 2, da, up(db))))
            packed_m = jnp.where(group == 0, down(dr), jnp.where(group == 1, dk, jnp.where(group == 2, down(da), db)))
            k_outs[0][tt] = packed_f
            k_outs[1][mt] = packed_m
            k_outs[2][tt] = dw
            k_outs[3][mt] = dw

        if nc:
            @pl.when(i == nt - 1)
            def _():
                wait()

    fwd_map, mir_map = (lambda i: (nt - 1 - i, 0, 0)), (lambda i: (i, 0, 0))
    kf_spec, km_spec = pl.BlockSpec((SCAN_TB, RW_N, LANE), fwd_map), pl.BlockSpec((SCAN_TB, RW_N, LANE), mir_map)
    vf_spec, vm_spec = pl.BlockSpec((SCAN_TB, RW_VH, LANE), fwd_map), pl.BlockSpec((SCAN_TB, RW_VH, LANE), mir_map)
    prev_spec = pl.BlockSpec((1, RW_VH, RW_N, LANE), lambda i: (jnp.maximum((nt - 1 - i) * SCAN_TB - 1, 0), 0, 0, 0))
    c_args, c_specs, c_shapes, c_sems = _comm_specs(comm)
    kshape, vshape = jax.ShapeDtypeStruct((t, RW_N, LANE), F32), jax.ShapeDtypeStruct((t, RW_VH, LANE), F32)
    return pl.pallas_call(
        body, name="rwkv_scan_bwd", grid=(nt,),
        in_specs=[kf_spec, km_spec] * 5 + [vf_spec, vm_spec,
                                           pl.BlockSpec((SCAN_TB, RW_VH, RW_N, LANE), lambda i: (nt - 1 - i, 0, 0, 0)),
                                           prev_spec, vf_spec, vf_spec, vm_spec] + c_specs,
        out_specs=[kf_spec, km_spec] * 2 + [vf_spec, vm_spec] + c_specs,
        out_shape=[kshape] * 4 + [vshape] * 2 + c_shapes,
        scratch_shapes=[pltpu.VMEM((RW_VH, RW_N, LANE), F32)] + c_sems,
        compiler_params=_cparams(("arbitrary",)),
    )(r, r, k, k, a, a, b, b, w, w, v, v, hist, hist, sa, dy, dy, *c_args)


RELAYOUT_TB = 128
RW_Q = LANE // 4


def to_scan(name, x, cb, nb, t, value, x_bwd=None):
    tb = min(RELAYOUT_TB, t)
    rows_out = RW_VH if value else RW_N
    ins = [x] if x_bwd is None else [x, x_bwd]

    def body(*refs):
        x_refs, o_ref, scrs = refs[:len(ins)], refs[len(ins)], refs[len(ins) + 1:]
        for x_ref, scr in zip(x_refs, scrs):
            for b in range(nb):
                scr[b * RW_H:(b + 1) * RW_H] = x_ref[b].T.reshape(RW_H, RW_N, tb)
        for j in range(rows_out):
            lo = scrs[0][:, j, :]
            if value:
                hi = scrs[0][:, j + RW_VH, :]
                blk = [lo, lo, hi, hi]
            else:
                other = lo if x_bwd is None else scrs[1][:, j, :]
                blk = [lo, other, lo, other]
            o_ref[:, j, :] = jnp.concatenate(blk, axis=0).T

    return pl.pallas_call(
        body, name=name, grid=(t // tb,),
        in_specs=[pl.BlockSpec((nb, tb, RW_W), lambda i: (0, i, cb))] + [pl.BlockSpec((nb, tb, RW_W), lambda i: (0, i, 0))] * (len(ins) - 1),
        out_specs=pl.BlockSpec((tb, rows_out, LANE), lambda i: (i, 0, 0)),
        out_shape=jax.ShapeDtypeStruct((t, rows_out, LANE), F32),
        scratch_shapes=[pltpu.VMEM((nb * RW_H, RW_N, tb), F32)] * len(ins),
        compiler_params=_cparams(("arbitrary",)),
    )(*[a.reshape(nb, t, a.shape[1]) for a in ins])


def from_scan(name, xf, xm, nb, t, value, groups=None):
    tb = min(RELAYOUT_TB, t)
    rows_in = RW_VH if value else RW_N
    n_out = 1 if value else (4 if groups is None else 2)
    grp = lambda a, g: a[g * RW_Q:(g + 1) * RW_Q]

    def body(f_ref, m_ref, *rest):
        outs, scrs = rest[:n_out], rest[n_out:]
        lane_group = lax.broadcasted_iota(jnp.int32, (1, LANE), 1) // RW_Q
        for j in range(rows_in):
            f, m = f_ref[:, j, :], m_ref[:, j, :]
            if value:
                c = jnp.where(_bwd_lanes(), m, f).T
                scrs[0][:, j, :] = grp(c, 0) + grp(c, 1)
                scrs[0][:, j + RW_VH, :] = grp(c, 2) + grp(c, 3)
            elif groups is None:
                c = (f + m).T
                for q, scr in enumerate(scrs):
                    scr[:, j, :] = grp(c, q)
            else:
                c = jnp.where(lane_group == groups[1], m, f).T
                scrs[0][:, j, :] = grp(c, groups[0])
                scrs[1][:, j, :] = grp(c, groups[1])
        for o_ref, scr in zip(outs, scrs):
            for b in range(nb):
                o_ref[b] = scr[b * RW_H:(b + 1) * RW_H].reshape(RW_W, tb).T

    res = pl.pallas_call(
        body, name=name, grid=(t // tb,),
        in_specs=[pl.BlockSpec((tb, rows_in, LANE), lambda i: (i, 0, 0))] * 2,
        out_specs=[pl.BlockSpec((nb, tb, RW_W), lambda i: (0, i, 0))] * n_out,
        out_shape=[jax.ShapeDtypeStruct((nb, t, RW_W), F32)] * n_out,
        scratch_shapes=[pltpu.VMEM((nb * RW_H, RW_N, tb), F32)] * n_out,
        compiler_params=_cparams(("arbitrary",)),
    )(xf, xm)
    return [r.reshape(nb * t, RW_W) for r in res]


def f_norm(rows, params):
    (x,), (g,) = rows, params
    return [_rmsnorm(x, g)]


def f_rwkv_pre(rows, params):
    k, wlal, gl = rows
    w0f, w2f, w0b, w2b, a0, a2, g2, k_k, k_a = params
    tw = jnp.tanh(wlal)

    def decay(w0, w2):
        return jnp.exp(-jnp.exp(-_softplus(-(w0 + mm(tw, w2))) - 0.5))

    lr = _sigmoid(a0 + mm(wlal, a2))
    gate = mm(_sigmoid(gl), g2)
    kk = k * k_k
    kk = kk / jnp.maximum(jnp.sqrt(_segment_sum(kk * kk, RW_N)), 1e-12)
    kp = k * (1.0 + (lr - 1.0) * k_a)
    return [decay(w0f, w2f), decay(w0b, w2b), kp, -kk, kk * lr, gate]


def f_branch_post(rows, params):
    o, og, y, r, kp, v, g = rows
    gla_g, ln_w, ln_b, r_k = params
    on = o * lax.rsqrt(_segment_sum(o * o, GLA_DV) * (1.0 / GLA_DV) + HEAD_NORM_EPS)
    oa = on * gla_g * _silu(og)
    mu = _segment_sum(y, RW_N) * (1.0 / RW_N)
    yc = y - mu
    var = _segment_sum(yc * yc, RW_N) * (1.0 / RW_N)
    yn = yc * lax.rsqrt(var + RW_GN_EPS) * ln_w + ln_b
    bonus = _segment_sum(r * kp * r_k, RW_N) * v
    return [oa, (yn + bonus) * g]


def f_merge(rows, params):
    ga, gb, ya, yb = rows
    return [_sigmoid(ga) * ya + _sigmoid(gb) * yb]


def f_norm2(rows, params):
    (x, mo), (g,) = rows, params
    x1 = x + mo
    return [x1, _rmsnorm(x1, g)]


def loss_head(x1, ffo, tgt, gf, tm):
    n = x1.shape[0]

    def body(x1_ref, f_ref, t_ref, g_ref, loss_ref, dx_ref, dg_ref):
        @pl.when(pl.program_id(0) == 0)
        def _():
            loss_ref[...] = jnp.zeros_like(loss_ref)
            dg_ref[...] = jnp.zeros_like(dg_ref)

        tgt_v = t_ref[...]

        def f(x2, g):
            err = _rmsnorm(x2, g) - tgt_v
            return jnp.sum(jnp.sum(err * err, axis=-1, keepdims=True), axis=0, keepdims=True) * (0.5 / D)

        val, vjp = jax.vjp(f, x1_ref[...] + f_ref[...], g_ref[...])
        dx, dg = vjp(jnp.ones((1, 1), F32))
        loss_ref[...] += val
        dx_ref[...] = dx
        dg_ref[...] += dg

    return pl.pallas_call(
        body, name="loss_head", grid=(n // tm,),
        in_specs=[_row_spec(tm, D, 0)] * 3 + [_full_spec((1, D))],
        out_specs=[_full_spec((1, 1)), _row_spec(tm, D, 0), _full_spec((1, D))],
        out_shape=[jax.ShapeDtypeStruct((1, 1), F32), jax.ShapeDtypeStruct((n, D), F32), jax.ShapeDtypeStruct((1, D), F32)],
        compiler_params=_cparams(("arbitrary",)),
    )(x1, ffo, tgt, gf)


def _pad_cols(a, width):
    return jnp.pad(a, ((0, 0), (0, width - a.shape[1])))


def w_in_to_padded(w):
    return _pad_cols(jnp.concatenate([w[:, 3360:5408], w[:, 0:1536], w[:, 1568:3360], w[:, 1536:1568]], axis=1), NP)


def w_in_from_padded(wp):
    return jnp.concatenate([wp[:, 2048:3584], wp[:, 5376:5408], wp[:, 3584:5376], wp[:, 0:2048]], axis=1)


def ff_interleave(a):
    r = a.shape[0]
    halves = jnp.stack([_pad_cols(a[:, :D_FF], FFP), _pad_cols(a[:, D_FF:], FFP)], axis=1)
    return halves.reshape(r, 2, FFP // LANE, LANE).transpose(0, 2, 1, 3).reshape(r, 2 * FFP)


def ff_deinterleave(a):
    r = a.shape[0]
    halves = a.reshape(r, FFP // LANE, 2, LANE).transpose(0, 2, 1, 3).reshape(r, 2, FFP)
    return halves[:, :, :D_FF].reshape(r, 2 * D_FF)


def _rows_into(w, rows, off):
    return jnp.zeros((rows, w.shape[1]), w.dtype).at[off:off + w.shape[0]].set(w)


LATE = ("gla_proj", "rwkv_proj", "w_out", "ffn_up", "ffn_conv_w", "ffn_down")


def local_step(x, tgt, w, nb, t, late_blocks=None):
    n = nb * t
    tm = min(n, 1024)
    tkt = min(n, 2048)
    tr = min(n, 256)
    vec = lambda a: a.reshape(1, -1)
    w = dict(w)

    w_in_p = w_in_to_padded(w["w_in"])
    wa2_f, wa2_b = _rows_into(w["gla_wa2_f"], LANE, 0), _rows_into(w["gla_wa2_b"], LANE, GLA_RANK)
    w2f, w2b = _rows_into(w["rwkv_w2_f"], LANE, 0), _rows_into(w["rwkv_w2_b"], LANE, 0)
    a2 = _rows_into(w["rwkv_a2"], LANE, 64)
    g1, g2n, gf = vec(w["norm1_g"]), vec(w["norm2_g"]), vec(w["norm_f_g"])
    mu_prev, mu_next = vec(w["rwkv_mu_prev"]), vec(w["rwkv_mu_next"])
    pre_params = [vec(w["rwkv_w0_f"]), w2f, vec(w["rwkv_w0_b"]), w2b, vec(w["rwkv_a0"]), a2, w["rwkv_g2"],
                  vec(w["rwkv_k_k"]), vec(w["rwkv_k_a"])]
    post_params = [vec(w["gla_norm_g"]), vec(w["rwkv_ln_w"]), vec(w["rwkv_ln_b"]), vec(w["rwkv_r_k"])]
    ba_f, ba_b = vec(w["gla_ba_f"]), vec(w["gla_ba_b"])

    (h1,) = rowwise_fwd("norm1_fwd", f_norm, [(x, D, 0)], [g1], [(D, MXU_DTYPE)], tr)
    p = matmul("proj_in", h1, w_in_p, "nn", F32, tm, FFP // 2, D)
    s = shift_fwd(p, mu_prev, mu_next, nb, t)
    pre_rows = [(s, 512, 1), (s, LANE, 1536 // LANE), (s, LANE, 1664 // LANE)]
    wf, wb, kp, a_s, b_s, g = rowwise_fwd("rwkv_pre_fwd", f_rwkv_pre, pre_rows, pre_params, [(RW_W, F32)] * 6, tr)
    sc = [to_scan("to_scan_r", s, 0, nb, t, False), to_scan("to_scan_w", wf, 0, nb, t, False, x_bwd=wb),
          to_scan("to_scan_k", kp, 0, nb, t, False), to_scan("to_scan_a", a_s, 0, nb, t, False),
          to_scan("to_scan_b", b_s, 0, nb, t, False), to_scan("to_scan_v", s, 2, nb, t, True)]
    comm = [] if late_blocks is None else [(late_blocks[k], False) for k in LATE]
    y_scf, y_scm, hist_rw, sa_sc, *gathered = rwkv_scan_fwd(*sc, comm=comm)
    for k, g_k in zip(LATE, gathered):
        w[k] = _gathered_to_full(g_k, SHARDED[k])
    ffn_up_p = ff_interleave(w["ffn_up"])
    conv_w_p, conv_b_p = ff_interleave(w["ffn_conv_w"]), ff_interleave(vec(w["ffn_conv_b"]))
    ffn_down_p = jnp.pad(w["ffn_down"], ((0, FFP - D_FF), (0, 0)))
    (y,) = from_scan("from_scan_y", y_scf, y_scm, nb, t, True)
    o_f, hist_f = gla_fwd(p, wa2_f, ba_f, None, nb, t, False)
    o, hist_b = gla_fwd(p, wa2_b, ba_b, o_f, nb, t, True)
    post_rows = [(o, 512, 0), (p, 512, C_OG // 512), (y, 512, 0), (s, 512, 0), (kp, 512, 0), (s, 512, 2), (g, 512, 0)]
    oa, ob = rowwise_fwd("branch_post_fwd", f_branch_post, post_rows, post_params, [(512, MXU_DTYPE)] * 2, tr)
    ya = matmul("gla_proj", oa, w["gla_proj"], "nn", F32, tm, 512, 512)
    yb = matmul("rwkv_proj", ob, w["rwkv_proj"], "nn", F32, tm, 512, 512)
    merge_rows = [(p, D, 0), (p, D, 1), (ya, D, 0), (yb, D, 0)]
    (merged,) = rowwise_fwd("merge_fwd", f_merge, merge_rows, [], [(D, MXU_DTYPE)], tr)
    mo = matmul("w_out", merged, w["w_out"], "nn", F32, tm, 512, D)
    x1, h2 = rowwise_fwd("norm2_fwd", f_norm2, [(x, D, 0), (mo, D, 0)], [g2n], [(D, F32), (D, MXU_DTYPE)], tr)
    u = matmul("ffn_up", h2, ffn_up_p, "nn", F32, tm, FFP // 2, D)
    z = conv_glu_fwd(u, conv_w_p, conv_b_p, nb, t)
    ffo = matmul("ffn_down", z, ffn_down_p, "nn", F32, tm, D, FFP // 2)
    loss, dx2, dgf = loss_head(x1, ffo, tgt, gf, tr)

    dz = matmul("ffn_down_dx", dx2, ffn_down_p, "nt", F32, tm, FFP // 2, D)
    d_ffn_down_p = matmul("ffn_down_dw", z, dx2, "tn", F32, FFP // 2, 512, tkt)
    du, d_conv_w_p, d_conv_b_p = conv_glu_bwd(u, dz, conv_w_p, conv_b_p, nb, t)
    dh2 = matmul("ffn_up_dx", du, ffn_up_p, "nt", F32, tm, D, FFP // 2)
    d_ffn_up_p = matmul("ffn_up_dw", h2, du, "tn", F32, D, 512, tkt)
    (dx1,), (dg2,) = rowwise_bwd("norm2_bwd", f_norm2, [(x, D, 0), (mo, D, 0)], [g2n],
                                 [[(dx2, D, 0)], [(dh2, D, 0)]], tr, grad_rows=[1])
    dmerged = matmul("w_out_dx", dx1, w["w_out"], "nt", F32, tm, D, D)
    d_w_out = matmul("w_out_dw", merged, dx1, "tn", F32, D, 512, tkt)
    (dga, dgb, dya, dyb), _ = rowwise_bwd("merge_bwd", f_merge, merge_rows, [], [[(dmerged, D, 0)]], tr)
    d_oa = matmul("gla_proj_dx", dya, w["gla_proj"], "nt", F32, tm, 512, D)
    d_gla_proj = matmul("gla_proj_dw", oa, dya, "tn", F32, 512, 512, tkt)
    d_ob = matmul("rwkv_proj_dx", dyb, w["rwkv_proj"], "nt", F32, tm, 512, D)
    d_rwkv_proj = matmul("rwkv_proj_dw", ob, dyb, "tn", F32, 512, 512, tkt)
    (d_o, d_og, d_y, d_r_post, d_kp_post, d_v_post, d_g), d_post = rowwise_bwd(
        "branch_post_bwd", f_branch_post, post_rows, post_params, [[(d_oa, 512, 0)], [(d_ob, 512, 0)]], tr)
    late_grads = {"gla_proj": d_gla_proj, "rwkv_proj": d_rwkv_proj, "w_out": d_w_out, "ffn_up": ff_deinterleave(d_ffn_up_p),
                  "ffn_conv_w": ff_deinterleave(d_conv_w_p), "ffn_down": d_ffn_down_p[0:D_FF]}
    comm = [] if late_blocks is None else [(_full_to_slices(late_grads[k], SHARDED[k]), True) for k in LATE]
    dsc = rwkv_scan_bwd(*sc, hist_rw, sa_sc, to_scan("to_scan_dy", d_y, 0, nb, t, True), comm=comm)
    received = dict(zip(LATE, dsc[6:]))
    d_r_scan, d_kp_scan, d_a_scan, d_b_scan = from_scan("from_scan_rkab", dsc[0], dsc[1], nb, t, False)
    d_wf, d_wb = from_scan("from_scan_w", dsc[2], dsc[3], nb, t, False, groups=(0, 1))
    (d_v_scan,) = from_scan("from_scan_dv", dsc[4], dsc[5], nb, t, True)
    (d_k, d_wlal, d_gl), d_pre = rowwise_bwd(
        "rwkv_pre_bwd", f_rwkv_pre, pre_rows, pre_params,
        [[(d_wf, 512, 0)], [(d_wb, 512, 0)], [(d_kp_scan, 512, 0), (d_kp_post, 512, 0)],
         [(d_a_scan, 512, 0)], [(d_b_scan, 512, 0)], [(d_g, 512, 0)]], tr)
    ds = jnp.concatenate([d_r_scan + d_r_post, d_k, d_v_scan + d_v_post, d_wlal, d_gl], axis=1)
    dp_rw, d_mu_prev, d_mu_next = shift_bwd(p, ds, mu_prev, mu_next, nb, t)
    dqkv_f, d_wa2_f, d_ba_f = gla_bwd(p, wa2_f, ba_f, hist_f, d_o, None, nb, t, False)
    dqkv, d_wa2_b, d_ba_b = gla_bwd(p, wa2_b, ba_b, hist_b, d_o, dqkv_f, nb, t, True)
    dp = jnp.concatenate([dga, dgb, dqkv[:, 0:1024], d_og, dp_rw, dqkv[:, 1024:1152],
                          jnp.zeros((n, NP - C_AFAB - LANE), F32)], axis=1).astype(MXU_DTYPE)
    d_w_in_p = matmul("proj_in_dw", h1, dp, "tn", F32, D, 512, tkt)
    grads = {
        "w_in": w_in_from_padded(d_w_in_p),
        "gla_wa2_f": d_wa2_f[0:GLA_RANK], "gla_ba_f": d_ba_f, "gla_wa2_b": d_wa2_b[GLA_RANK:2 * GLA_RANK], "gla_ba_b": d_ba_b,
        "gla_norm_g": d_post[0], "rwkv_mu_prev": d_mu_prev, "rwkv_mu_next": d_mu_next,
        "rwkv_w0_f": d_pre[0], "rwkv_w2_f": d_pre[1][0:64], "rwkv_w0_b": d_pre[2], "rwkv_w2_b": d_pre[3][0:64],
        "rwkv_a0": d_pre[4], "rwkv_a2": d_pre[5][64:128], "rwkv_g2": d_pre[6], "rwkv_k_k": d_pre[7], "rwkv_k_a": d_pre[8],
        "rwkv_r_k": d_post[3], "rwkv_ln_w": d_post[1], "rwkv_ln_b": d_post[2],
        "norm2_g": dg2, "ffn_conv_b": ff_deinterleave(d_conv_b_p), "norm_f_g": dgf, **late_grads,
    }
    early = [k for k in SHARDED if k not in LATE]
    payload = lambda k: _full_to_slices(grads[k], SHARDED[k]).astype(MXU_DTYPE if k == "w_in" else F32)
    comm = [] if late_blocks is None else [(payload(k), True) for k in early]
    dh1, *got = matmul("proj_in_dx", dp, w_in_p, "nt", F32, tm, D, FFP // 2, comm=comm) if comm else \
        [matmul("proj_in_dx", dp, w_in_p, "nt", F32, tm, D, FFP // 2)]
    received.update(zip(early, got))
    (grad_x,), (grads["norm1_g"],) = rowwise_bwd("norm1_bwd", f_norm, [(x, D, 0)], [g1], [[(dh1, D, 0)]], tr,
                                                 adds=[(0, (dx1, D, 0))])
    return loss, grad_x, grads, received


MESH = pl.DeviceIdType.MESH


def remote_exchange(name, items):
    n = len(items)

    def body(*refs):
        start, wait = _exchange_plan([sc for _, sc in items], refs[:n], refs[n:2 * n], *refs[2 * n:])
        start()
        wait()

    args, specs, shapes, sems = _comm_specs(items)
    return pl.pallas_call(body, name=name, in_specs=specs, out_specs=specs, out_shape=shapes, scratch_shapes=sems)(*args)


def gather_two_level(name, blocks):
    n = len(blocks)

    def body(*refs):
        in_refs, out_refs = refs[:n], refs[n:2 * n]
        send_sems, recv_sems, local_sems = refs[2 * n:]
        x, y, c = lax.axis_index("x"), lax.axis_index("y"), lax.axis_index("c")
        me, sibling = (x, y, c), (x, y, 1 - c)
        chips = [(1 - x, y), (x, 1 - y), (1 - x, 1 - y)]

        def copy(i, k, block, to, src=None):
            rows = out_refs[i].at[4 * block[0] + 2 * block[1] + block[2]]
            return pltpu.make_async_remote_copy(src_ref=rows if src is None else src, dst_ref=rows, send_sem=send_sems.at[i, k],
                                                recv_sem=recv_sems.at[i, k], device_id=to, device_id_type=MESH)

        own = [pltpu.make_async_copy(in_refs[i], out_refs[i].at[4 * x + 2 * y + c], local_sems.at[i]) for i in range(n)]
        first = [copy(i, 0, me, sibling, src=in_refs[i]) for i in range(n)]
        first += [copy(i, 1 + j, me, (*chip, c), src=in_refs[i]) for j, chip in enumerate(chips) for i in range(n)]
        for cp in own + first:
            cp.start()
        passed = []
        for j, chip in enumerate(chips):
            for i in range(n):
                copy(i, 1 + j, (*chip, c), me).wait_recv()
                onward = copy(i, 4 + j, (*chip, c), sibling)
                onward.start()
                passed.append(onward)
        for i in range(n):
            copy(i, 0, sibling, me).wait_recv()
        for j, chip in enumerate(chips):
            for i in range(n):
                copy(i, 4 + j, (*chip, 1 - c), me).wait_recv()
        for cp in first + passed:
            cp.wait_send()
        for cp in own:
            cp.wait()

    args, specs, shapes, sems = _comm_specs([(b, False) for b in blocks])
    return pl.pallas_call(body, name=name, in_specs=specs, out_specs=specs, out_shape=shapes, scratch_shapes=sems)(*args)


def _exchange_plan(flags, in_refs, out_refs, send_sems, recv_sems, local_sems):
    x, y, c = lax.axis_index("x"), lax.axis_index("y"), lax.axis_index("c")
    me = 4 * x + 2 * y + c

    def peer(k):
        px = 1 - x if (k >> 2) & 1 else x
        py = 1 - y if (k >> 1) & 1 else y
        pc = 1 - c if k & 1 else c
        return (px, py, pc), 4 * px + 2 * py + pc

    def copies(with_arrivals):
        own, sends, recvs = [], [], []
        for i, scatter in enumerate(flags):
            src = in_refs[i].at[me] if scatter else in_refs[i]
            own.append(pltpu.make_async_copy(src, out_refs[i].at[me], local_sems.at[i]))
        for k in range(1, N_DEV):
            dev, slot = peer(k)
            for i, scatter in enumerate(flags):
                src = in_refs[i].at[slot] if scatter else in_refs[i]
                pair = dict(send_sem=send_sems.at[i, k - 1], recv_sem=recv_sems.at[i, k - 1], device_id=dev, device_id_type=MESH)
                sends.append(pltpu.make_async_remote_copy(src_ref=src, dst_ref=out_refs[i].at[me], **pair))
                if with_arrivals:
                    recvs.append(pltpu.make_async_remote_copy(src_ref=out_refs[i].at[slot], dst_ref=out_refs[i].at[slot], **pair))
        return own, sends, recvs

    def start():
        own, sends, _ = copies(False)
        for cp in own + sends:
            cp.start()

    def wait():
        own, sends, recvs = copies(True)
        for send, recv in zip(sends, recvs):
            recv.wait_recv()
            send.wait_send()
        for cp in own:
            cp.wait()

    return start, wait


def _adam_tiles(r, c):
    tc = 256 if (c % 256 == 0 and r * c > 128 * 1024) else c
    tr = 128 if (r % 128 == 0 and r > 128) else r
    return tr, tc


def adamw_reduce(name, parts, w, m, v):
    lead = w.ndim - 2
    r, c = w.shape[lead:]
    tr, tc = _adam_tiles(r, c)

    def body(p_ref, w_ref, m_ref, v_ref, g_ref, d_ref, nm_ref, nv_ref):
        g = p_ref[0].astype(F32)
        for d in range(1, N_DEV):
            g = g + p_ref[d].astype(F32)
        at = (0,) * lead + (slice(None), slice(None))
        nm = ADAM_B1 * m_ref[at] + (1.0 - ADAM_B1) * g
        nv = ADAM_B2 * v_ref[at] + (1.0 - ADAM_B2) * (g * g)
        m_hat = nm / (1.0 - ADAM_B1 ** ADAM_STEP)
        v_hat = nv / (1.0 - ADAM_B2 ** ADAM_STEP)
        g_ref[at] = g
        d_ref[at] = -ADAM_LR * (m_hat / (jnp.sqrt(v_hat) + ADAM_EPS) + ADAM_WD * w_ref[at])
        nm_ref[at] = nm
        nv_ref[at] = nv

    spec = pl.BlockSpec((1,) * lead + (tr, tc), lambda i, j: (0,) * lead + (i, j))
    return pl.pallas_call(
        body, name=name, grid=(r // tr, c // tc),
        in_specs=[pl.BlockSpec((N_DEV, tr, tc), lambda i, j: (0, i, j)), spec, spec, spec],
        out_specs=[spec] * 4, out_shape=[jax.ShapeDtypeStruct(w.shape, F32)] * 4,
        compiler_params=_cparams(("arbitrary", "arbitrary")),
    )(parts, w, m, v)


SHARDED = {"w_in": 1, "gla_wa2_f": 1, "gla_wa2_b": 1, "gla_proj": 1, "rwkv_w2_f": 1, "rwkv_w2_b": 1, "rwkv_a2": 1,
           "rwkv_g2": 1, "rwkv_proj": 1, "w_out": 0, "ffn_up": 1, "ffn_conv_w": 1, "ffn_down": 0}
BF16_GATHER = ("w_in", "gla_proj", "rwkv_proj", "w_out", "ffn_up", "ffn_down")
REPLICATED = ("norm1_g", "gla_ba_f", "gla_ba_b", "gla_norm_g", "rwkv_mu_prev", "rwkv_mu_next", "rwkv_w0_f", "rwkv_w0_b",
              "rwkv_a0", "rwkv_k_k", "rwkv_k_a", "rwkv_r_k", "rwkv_ln_w", "rwkv_ln_b", "norm2_g", "ffn_conv_b", "norm_f_g")
WEIGHTS = ("norm1_g", "w_in", "gla_wa2_f", "gla_ba_f", "gla_wa2_b", "gla_ba_b", "gla_norm_g", "gla_proj", "rwkv_mu_prev",
           "rwkv_mu_next", "rwkv_w0_f", "rwkv_w2_f", "rwkv_w0_b", "rwkv_w2_b", "rwkv_a0", "rwkv_a2", "rwkv_g2", "rwkv_k_k",
           "rwkv_k_a", "rwkv_r_k", "rwkv_ln_w", "rwkv_ln_b", "rwkv_proj", "w_out", "norm2_g", "ffn_up", "ffn_conv_w",
           "ffn_conv_b", "ffn_down", "norm_f_g")


def _gathered_to_full(g, axis):
    if axis == 0:
        return g.reshape(N_DEV * g.shape[1], g.shape[2])
    return g.transpose(1, 0, 2).reshape(g.shape[1], N_DEV * g.shape[2])


def _full_to_slices(a, axis):
    if axis == 0:
        return a.reshape(N_DEV, a.shape[0] // N_DEV, a.shape[1])
    return a.reshape(a.shape[0], N_DEV, a.shape[1] // N_DEV).transpose(1, 0, 2)


def _pack_rows(size):
    return -(-size // (8 * LANE)) * 8


def _pack(d):
    parts = []
    for k in REPLICATED:
        rows = d[k].reshape(-1, LANE).astype(F32)
        parts.append(jnp.pad(rows, ((0, _pack_rows(rows.size) - rows.shape[0]), (0, 0))))
    return jnp.concatenate(parts, axis=0)


def _unpack(packed, shapes):
    out, pos = {}, 0
    for k in REPLICATED:
        size = int(np.prod(shapes[k]))
        out[k] = packed[pos:pos + size // LANE].reshape(shapes[k])
        pos += _pack_rows(size)
    return out


def kernel(x, norm1_g, w_in, gla_wa2_f, gla_ba_f, gla_wa2_b, gla_ba_b, gla_norm_g, gla_proj, rwkv_mu_prev, rwkv_mu_next, rwkv_w0_f, rwkv_w2_f, rwkv_w0_b, rwkv_w2_b, rwkv_a0, rwkv_a2, rwkv_g2, rwkv_k_k, rwkv_k_a, rwkv_r_k, rwkv_ln_w, rwkv_ln_b, rwkv_proj, w_out, norm2_g, ffn_up, ffn_conv_w, ffn_conv_b, ffn_down, norm_f_g, loss_target, m_norm1_g, m_w_in, m_gla_wa2_f, m_gla_ba_f, m_gla_wa2_b, m_gla_ba_b, m_gla_norm_g, m_gla_proj, m_rwkv_mu_prev, m_rwkv_mu_next, m_rwkv_w0_f, m_rwkv_w2_f, m_rwkv_w0_b, m_rwkv_w2_b, m_rwkv_a0, m_rwkv_a2, m_rwkv_g2, m_rwkv_k_k, m_rwkv_k_a, m_rwkv_r_k, m_rwkv_ln_w, m_rwkv_ln_b, m_rwkv_proj, m_w_out, m_norm2_g, m_ffn_up, m_ffn_conv_w, m_ffn_conv_b, m_ffn_down, m_norm_f_g, v_norm1_g, v_w_in, v_gla_wa2_f, v_gla_ba_f, v_gla_wa2_b, v_gla_ba_b, v_gla_norm_g, v_gla_proj, v_rwkv_mu_prev, v_rwkv_mu_next, v_rwkv_w0_f, v_rwkv_w2_f, v_rwkv_w0_b, v_rwkv_w2_b, v_rwkv_a0, v_rwkv_a2, v_rwkv_g2, v_rwkv_k_k, v_rwkv_k_a, v_rwkv_r_k, v_rwkv_ln_w, v_rwkv_ln_b, v_rwkv_proj, v_w_out, v_norm2_g, v_ffn_up, v_ffn_conv_w, v_ffn_conv_b, v_ffn_down, v_norm_f_g):
    args = locals()
    wts = {k: args[k] for k in WEIGHTS}
    mom = {k: args["m_" + k] for k in WEIGHTS}
    var = {k: args["v_" + k] for k in WEIGHTS}
    shapes = {k: wts[k].shape for k in WEIGHTS}
    nb, t = x.shape[0], x.shape[1]
    mat = lambda a: a.reshape(a.shape[-2], a.shape[-1])

    block = lambda k: mat(wts[k]).astype(MXU_DTYPE) if k in BF16_GATHER else mat(wts[k])
    early = [k for k in SHARDED if k not in LATE]
    gathered = gather_two_level("gather_weights", [block(k) for k in early])
    full = {k: _gathered_to_full(g, SHARDED[k]) for k, g in zip(early, gathered)}
    for k in REPLICATED:
        full[k] = wts[k].reshape(-1) if k in ("norm_f_g", "rwkv_r_k") else wts[k][0]

    loss, grad_x, grads, received = local_step(x.reshape(nb * t, D), loss_target.reshape(nb * t, D), full, nb, t,
                                               late_blocks={k: block(k) for k in LATE})

    (rep_parts,) = remote_exchange("exchange_replicated", [(_pack(grads), False)])

    res = {}
    for k in SHARDED:
        res[k] = adamw_reduce("adamw_" + k, received[k], wts[k], mom[k], var[k])
    packed = adamw_reduce("adamw_replicated", rep_parts, _pack(wts), _pack(mom), _pack(var))
    unpacked = [_unpack(p, shapes) for p in packed]
    for k in REPLICATED:
        res[k] = [u[k] for u in unpacked]

    total = lax.psum(loss[0, 0], ("x", "y", "c"))
    out = [total, grad_x.reshape(x.shape)]
    for j in range(4):
        out += [res[k][j] for k in WEIGHTS]
    return tuple(out)
```

```python
import functools

import jax
import jax.numpy as jnp
import numpy as np
from jax import lax
from jax.experimental import pallas as pl
from jax.experimental.pallas import tpu as pltpu

F32 = jnp.float32
MXU_DTYPE = jnp.bfloat16

D = 1024
SEQ = 2048
GLA_H, GLA_DK, GLA_DV, GLA_CHUNK = 4, 64, 128, 64
GLA_RANK = 16
GLA_LOGIT_NORM = 16.0
RW_H, RW_N = 8, 64
RW_W = 512
D_FF = 2752
NORM_EPS = 1e-6
HEAD_NORM_EPS = 1e-5
RW_GN_EPS = RW_N * 1e-5
N_DEV = 8
ADAM_LR, ADAM_B1, ADAM_B2, ADAM_EPS, ADAM_WD, ADAM_STEP = 0.001, 0.9, 0.999, 1e-08, 0.01, 10

C_GA, C_GB, C_Q, C_K, C_V, C_OG = 0, 1024, 2048, 2304, 2560, 3072
C_RW = 3584
C_R, C_RK, C_RV, C_WLAL, C_GL = 3584, 4096, 4608, 5120, 5248
C_AFAB = 5376
NP = 5632
RW_PW = 1792
FFP = 2816
LANE = 128
VMEM_LIMIT = 56 * 1024 * 1024


def _cparams(sem):
    return pltpu.CompilerParams(dimension_semantics=sem, vmem_limit_bytes=VMEM_LIMIT)


@jax.custom_vjp
def mm(a, b):
    return jnp.dot(a.astype(MXU_DTYPE), b.astype(MXU_DTYPE), preferred_element_type=F32)


def _mm_fwd(a, b):
    return mm(a, b), (a, b)


def _mm_bwd(res, g):
    a, b = res
    gb = g.astype(MXU_DTYPE)
    da = lax.dot_general(gb, b.astype(MXU_DTYPE), (((1,), (1,)), ((), ())), preferred_element_type=F32)
    db = lax.dot_general(a.astype(MXU_DTYPE), gb, (((0,), (0,)), ((), ())), preferred_element_type=F32)
    return da.astype(a.dtype), db.astype(b.dtype)


mm.defvjp(_mm_fwd, _mm_bwd)


@jax.custom_vjp
def mm_nt(a, b):
    return lax.dot_general(a.astype(MXU_DTYPE), b.astype(MXU_DTYPE), (((1,), (1,)), ((), ())), preferred_element_type=F32)


def _mm_nt_fwd(a, b):
    return mm_nt(a, b), (a, b)


def _mm_nt_bwd(res, g):
    a, b = res
    gb = g.astype(MXU_DTYPE)
    da = jnp.dot(gb, b.astype(MXU_DTYPE), preferred_element_type=F32)
    db = lax.dot_general(gb, a.astype(MXU_DTYPE), (((0,), (0,)), ((), ())), preferred_element_type=F32)
    return da.astype(a.dtype), db.astype(b.dtype)


mm_nt.defvjp(_mm_nt_fwd, _mm_nt_bwd)


@jax.custom_vjp
def mm_tn(a, b):
    return lax.dot_general(a.astype(MXU_DTYPE), b.astype(MXU_DTYPE), (((0,), (0,)), ((), ())), preferred_element_type=F32)


def _mm_tn_fwd(a, b):
    return mm_tn(a, b), (a, b)


def _mm_tn_bwd(res, g):
    a, b = res
    gb = g.astype(MXU_DTYPE)
    da = lax.dot_general(b.astype(MXU_DTYPE), gb, (((1,), (1,)), ((), ())), preferred_element_type=F32)
    db = jnp.dot(a.astype(MXU_DTYPE), gb, preferred_element_type=F32)
    return da.astype(a.dtype), db.astype(b.dtype)


mm_tn.defvjp(_mm_tn_fwd, _mm_tn_bwd)


@functools.partial(jax.custom_vjp, nondiff_argnums=(2, 3))
def sel_dot(x, s, dims, x_first):
    sb = s.astype(MXU_DTYPE)
    hi = x.astype(MXU_DTYPE)
    r1 = x - hi.astype(F32)
    mid = r1.astype(MXU_DTYPE)
    lo = (r1 - mid.astype(F32)).astype(MXU_DTYPE)
    out = None
    for part in (hi, mid, lo):
        ops = (part, sb) if x_first else (sb, part)
        d = lax.dot_general(*ops, (dims, ((), ())), preferred_element_type=F32)
        out = d if out is None else out + d
    return out


def _sel_dot_fwd(x, s, dims, x_first):
    return sel_dot(x, s, dims, x_first), s


def _sel_dot_bwd(dims, x_first, s, g):
    if x_first:
        (cx,), (cs,) = dims
        dx = sel_dot(g, s, ((1,), (1 - cs,)), True) if cx == 1 else sel_dot(g, s, ((1 - cs,), (1,)), False)
    else:
        (cs,), (cx,) = dims
        dx = sel_dot(g, s, ((1 - cs,), (0,)), False) if cx == 0 else sel_dot(g, s, ((0,), (1 - cs,)), True)
    return dx, jnp.zeros_like(s)


sel_dot.defvjp(_sel_dot_fwd, _sel_dot_bwd)


def mm_exact(a, b, b_is_01=True):
    return sel_dot(a, b, ((1,), (0,)), True) if b_is_01 else sel_dot(b, a, ((1,), (0,)), False)


def mm_tn_exact(a, b):
    return sel_dot(a, b, ((0,), (0,)), True)


def _softplus(x):
    return jnp.maximum(x, 0.0) + jnp.log(1.0 + jnp.exp(-jnp.abs(x)))


def _sigmoid(x):
    return jax.nn.sigmoid(x)


def _silu(x):
    return x * _sigmoid(x)


def _rmsnorm(x, g):
    return x * lax.rsqrt(jnp.mean(x * x, axis=-1, keepdims=True) + NORM_EPS) * g


def _segment_sum(x, seg):
    width = x.shape[1]
    i = lax.broadcasted_iota(jnp.int32, (width, width), 0) // seg
    j = lax.broadcasted_iota(jnp.int32, (width, width), 1) // seg
    return mm_exact(x, (i == j).astype(F32))


def _row_spec(tm, width, cb):
    return pl.BlockSpec((tm, width), lambda i: (i, cb))


def _full_spec(shape):
    nd = len(shape)
    return pl.BlockSpec(tuple(shape), lambda i: (0,) * nd)


def rowwise_fwd(name, f, rows, params, outs, tm):
    n = rows[0][0].shape[0]
    nr, npar = len(rows), len(params)

    def body(*refs):
        rv = [r[...] for r in refs[:nr]]
        pv = [r[...] for r in refs[nr:nr + npar]]
        res = f(rv, pv)
        for o_ref, val in zip(refs[nr + npar:], res):
            o_ref[...] = val.astype(o_ref.dtype)

    return pl.pallas_call(
        body, name=name, grid=(n // tm,),
        in_specs=[_row_spec(tm, w, cb) for _, w, cb in rows] + [_full_spec(p.shape) for p in params],
        out_specs=[_row_spec(tm, w, 0) for w, _ in outs],
        out_shape=[jax.ShapeDtypeStruct((n, w), dt) for w, dt in outs],
        compiler_params=_cparams(("arbitrary",)),
    )(*[a for a, _, _ in rows], *params)


def rowwise_bwd(name, f, rows, params, douts, tm, adds=(), grad_rows=None):
    n = rows[0][0].shape[0]
    nr, npar = len(rows), len(params)
    grad_rows = list(range(nr)) if grad_rows is None else list(grad_rows)
    flat_d = [d for group in douts for d in group]
    nd, na, ng = len(flat_d), len(adds), len(grad_rows)

    def body(*refs):
        rv = [r[...] for r in refs[:nr]]
        pv = [r[...] for r in refs[nr:nr + npar]]
        dflat = [r[...].astype(F32) for r in refs[nr + npar:nr + npar + nd]]
        av = [r[...] for r in refs[nr + npar + nd:nr + npar + nd + na]]
        o = nr + npar + nd + na
        drow_refs, dpar_refs = refs[o:o + ng], refs[o + ng:o + ng + npar]
        dv, pos = [], 0
        for group in douts:
            dv.append(sum(dflat[pos + 1:pos + len(group)], dflat[pos]))
            pos += len(group)

        @pl.when(pl.program_id(0) == 0)
        def _():
            for r in dpar_refs:
                r[...] = jnp.zeros_like(r)

        def g(grows, pars):
            full = list(rv)
            for i, val in zip(grad_rows, grows):
                full[i] = val
            return f(full, pars)

        res, vjp = jax.vjp(g, [rv[i] for i in grad_rows], pv)
        drows, dpars = vjp([d.astype(r.dtype) for d, r in zip(dv, res)])
        drows = [d.astype(F32) for d in drows]
        for (idx, _), a in zip(adds, av):
            drows[idx] = drows[idx] + a.astype(F32)
        for r, d in zip(drow_refs, drows):
            r[...] = d
        for r, d in zip(dpar_refs, dpars):
            r[...] += d.astype(F32)

    res = pl.pallas_call(
        body, name=name, grid=(n // tm,),
        in_specs=[_row_spec(tm, w, cb) for _, w, cb in rows] + [_full_spec(p.shape) for p in params]
        + [_row_spec(tm, w, cb) for _, w, cb in flat_d] + [_row_spec(tm, w, cb) for _, (_, w, cb) in adds],
        out_specs=[_row_spec(tm, rows[i][1], 0) for i in grad_rows] + [_full_spec(p.shape) for p in params],
        out_shape=[jax.ShapeDtypeStruct((n, rows[i][1]), F32) for i in grad_rows]
        + [jax.ShapeDtypeStruct(p.shape, F32) for p in params],
        compiler_params=_cparams(("arbitrary",)),
    )(*[a for a, _, _ in rows], *params, *[a for a, _, _ in flat_d], *[a for _, (a, _, _) in adds])
    return res[:ng], res[ng:]


def matmul(name, a, b, mode, out_dtype, tm, tn, tk, comm=()):
    nc = len(comm)
    flags = [sc for _, sc in comm]
    if mode == "nn":
        (m, k), n = a.shape, b.shape[1]
        a_spec = pl.BlockSpec((tm, tk), lambda i, j, kk: (i, kk))
        b_spec = pl.BlockSpec((tk, tn), lambda i, j, kk: (kk, j))
        dims = (((1,), (0,)), ((), ()))
    elif mode == "nt":
        (m, k), n = a.shape, b.shape[0]
        a_spec = pl.BlockSpec((tm, tk), lambda i, j, kk: (i, kk))
        b_spec = pl.BlockSpec((tn, tk), lambda i, j, kk: (j, kk))
        dims = (((1,), (1,)), ((), ()))
    else:
        (k, m), n = a.shape, b.shape[1]
        a_spec = pl.BlockSpec((tk, tm), lambda i, j, kk: (kk, i))
        b_spec = pl.BlockSpec((tk, tn), lambda i, j, kk: (kk, j))
        dims = (((0,), (0,)), ((), ()))
    assert m % tm == 0 and n % tn == 0 and k % tk == 0, (name, a.shape, b.shape, tm, tn, tk)
    nk = k // tk
    grid = (m // tm, n // tn, nk)

    def body(*refs):
        a_ref, b_ref, c_in, o_ref = refs[0], refs[1], refs[2:2 + nc], refs[2 + nc]
        c_out, acc_ref, sems = refs[3 + nc:3 + 2 * nc], refs[3 + 2 * nc], refs[4 + 2 * nc:]
        kk = pl.program_id(2)
        step = (pl.program_id(0) * grid[1] + pl.program_id(1)) * nk + kk
        if nc:
            start, wait = _exchange_plan(flags, c_in, c_out, *sems)

            @pl.when(step == 0)
            def _():
                start()

        part = lax.dot_general(a_ref[...].astype(MXU_DTYPE), b_ref[...].astype(MXU_DTYPE), dims, preferred_element_type=F32)
        if nk == 1:
            o_ref[...] = part.astype(o_ref.dtype)
        else:
            @pl.when(kk == 0)
            def _():
                acc_ref[...] = part

            @pl.when((kk > 0) & (kk < nk - 1))
            def _():
                acc_ref[...] += part

            @pl.when(kk == nk - 1)
            def _():
                o_ref[...] = (acc_ref[...] + part).astype(o_ref.dtype)

        if nc:
            @pl.when(step == grid[0] * grid[1] * nk - 1)
            def _():
                wait()

    c_args, c_specs, c_shapes, c_sems = _comm_specs(comm)
    res = pl.pallas_call(
        body, name=name, grid=grid,
        in_specs=[a_spec, b_spec] + c_specs,
        out_specs=[pl.BlockSpec((tm, tn), lambda i, j, kk: (i, j))] + c_specs,
        out_shape=[jax.ShapeDtypeStruct((m, n), out_dtype)] + c_shapes,
        scratch_shapes=[pltpu.VMEM((tm, tn) if nk > 1 else (8, LANE), F32)] + c_sems,
        compiler_params=_cparams(("arbitrary", "arbitrary", "arbitrary")),
    )(a, b, *c_args)
    return res if nc else res[0]


def _prev(u, first):
    return jnp.where(first, 0.0, pltpu.roll(u, 1, 0))


def _next(u, last):
    return jnp.where(last, 0.0, pltpu.roll(u, u.shape[0] - 1, 0))


def _edge_masks(t, w):
    row = lax.broadcasted_iota(jnp.int32, (t, w), 0)
    return row == 0, row == t - 1


WIN, HALO = 128, 8
MID = slice(HALO, HALO + WIN)


def _window(ref, i, t):
    r0 = pl.multiple_of(i * WIN, WIN)
    before = ref[pl.ds(pl.multiple_of(jnp.maximum(r0 - HALO, 0), HALO), HALO), :]
    after = ref[pl.ds(pl.multiple_of(jnp.minimum(r0 + WIN, t - HALO), HALO), HALO), :]
    before = jnp.where(i == 0, 0.0, before.astype(F32))
    after = jnp.where(i == t // WIN - 1, 0.0, after.astype(F32))
    return jnp.concatenate([before, ref[pl.ds(r0, WIN), :].astype(F32), after], axis=0)


def _wprev(u):
    return pltpu.roll(u, 1, 0)


def _wnext(u):
    return pltpu.roll(u, u.shape[0] - 1, 0)


def _mid_rows(i):
    return pl.ds(pl.multiple_of(i * WIN, WIN), WIN)


def _colsum(x):
    return jnp.sum(x[MID], axis=0, keepdims=True)


SHIFT_CW = 256


def shift_fwd(p, mu_prev, mu_next, nb, t):
    cw, c0 = SHIFT_CW, C_RW // SHIFT_CW

    def body(p_ref, mp_ref, mn_ref, s_ref):
        x = p_ref[...]
        first, last = _edge_masks(t, cw)
        s_ref[...] = x + mp_ref[...] * (_prev(x, first) - x) + mn_ref[...] * (_next(x, last) - x)

    return pl.pallas_call(
        body, name="rwkv_shift_fwd", grid=(nb, RW_PW // cw),
        in_specs=[pl.BlockSpec((t, cw), lambda b, j: (b, c0 + j)), pl.BlockSpec((1, cw), lambda b, j: (0, j)),
                  pl.BlockSpec((1, cw), lambda b, j: (0, j))],
        out_specs=pl.BlockSpec((t, cw), lambda b, j: (b, j)),
        out_shape=jax.ShapeDtypeStruct((nb * t, RW_PW), F32),
        compiler_params=_cparams(("arbitrary", "arbitrary")),
    )(p, mu_prev, mu_next)


def shift_bwd(p, ds, mu_prev, mu_next, nb, t):
    cw, c0 = SHIFT_CW, C_RW // SHIFT_CW

    def body(p_ref, ds_ref, mp_ref, mn_ref, dp_ref, dmp_ref, dmn_ref):
        @pl.when(pl.program_id(1) == 0)
        def _():
            dmp_ref[...] = jnp.zeros_like(dmp_ref)
            dmn_ref[...] = jnp.zeros_like(dmn_ref)

        mp, mn = mp_ref[...], mn_ref[...]

        def step(i, carry):
            dmp, dmn = carry
            x, g = _window(p_ref, i, t), _window(ds_ref, i, t)
            dp = g * (1.0 - mp - mn) + _wnext(mp * g) + _wprev(mn * g)
            dp_ref[_mid_rows(i), :] = dp[MID]
            return dmp + _colsum(g * (_wprev(x) - x)), dmn + _colsum(g * (_wnext(x) - x))

        zero = jnp.zeros((1, cw), F32)
        dmp, dmn = lax.fori_loop(0, t // WIN, step, (zero, zero))
        dmp_ref[...] += dmp
        dmn_ref[...] += dmn

    return pl.pallas_call(
        body, name="rwkv_shift_bwd", grid=(RW_PW // cw, nb),
        in_specs=[pl.BlockSpec((t, cw), lambda j, b: (b, c0 + j)), pl.BlockSpec((t, cw), lambda j, b: (b, j)),
                  pl.BlockSpec((1, cw), lambda j, b: (0, j)), pl.BlockSpec((1, cw), lambda j, b: (0, j))],
        out_specs=[pl.BlockSpec((t, cw), lambda j, b: (b, j)), pl.BlockSpec((1, cw), lambda j, b: (0, j)),
                   pl.BlockSpec((1, cw), lambda j, b: (0, j))],
        out_shape=[jax.ShapeDtypeStruct((nb * t, RW_PW), F32), jax.ShapeDtypeStruct((1, RW_PW), F32),
                   jax.ShapeDtypeStruct((1, RW_PW), F32)],
        compiler_params=_cparams(("arbitrary", "arbitrary")),
    )(p, ds, mu_prev, mu_next)


def conv_glu_fwd(u, cw, cb, nb, t):
    def body(u_ref, w_ref, b_ref, z_ref):
        x, w = u_ref[...], w_ref[...]
        first, last = _edge_masks(t, 2 * LANE)
        c = w[0:1] * _prev(x, first) + w[1:2] * x + w[2:3] * _next(x, last) + b_ref[...]
        z_ref[...] = (_silu(c[:, :LANE]) * c[:, LANE:]).astype(z_ref.dtype)

    return pl.pallas_call(
        body, name="conv_glu_fwd", grid=(nb, FFP // LANE),
        in_specs=[pl.BlockSpec((t, 2 * LANE), lambda b, j: (b, j)), pl.BlockSpec((3, 2 * LANE), lambda b, j: (0, j)),
                  pl.BlockSpec((1, 2 * LANE), lambda b, j: (0, j))],
        out_specs=pl.BlockSpec((t, LANE), lambda b, j: (b, j)),
        out_shape=jax.ShapeDtypeStruct((nb * t, FFP), MXU_DTYPE),
        compiler_params=_cparams(("arbitrary", "arbitrary")),
    )(u, cw, cb)


def conv_glu_bwd(u, dz, cw, cb, nb, t):
    def body(u_ref, dz_ref, w_ref, b_ref, du_ref, dw_ref, db_ref):
        @pl.when(pl.program_id(1) == 0)
        def _():
            dw_ref[...] = jnp.zeros_like(dw_ref)
            db_ref[...] = jnp.zeros_like(db_ref)

        w, bias = w_ref[...], b_ref[...]

        def step(i, carry):
            x, g = _window(u_ref, i, t), _window(dz_ref, i, t)
            xp, xn = _wprev(x), _wnext(x)
            c = w[0:1] * xp + w[1:2] * x + w[2:3] * xn + bias
            cg, cv = c[:, :LANE], c[:, LANE:]
            sg = _sigmoid(cg)
            dcg = g * cv * (sg * (1.0 + cg * (1.0 - sg)))
            dcv = g * (cg * sg)
            dc = jnp.concatenate([dcg, dcv], axis=1)
            du = w[1:2] * dc + _wnext(w[0:1] * dc) + _wprev(w[2:3] * dc)
            du_ref[_mid_rows(i), :] = du[MID].astype(du_ref.dtype)
            return tuple(acc + _colsum(val) for acc, val in zip(carry, (dc * xp, dc * x, dc * xn, dc)))

        zero = jnp.zeros((1, 2 * LANE), F32)
        sums = lax.fori_loop(0, t // WIN, step, (zero, zero, zero, zero))
        for row in range(3):
            dw_ref[row:row + 1, :] += sums[row]
        db_ref[...] += sums[3]

    return pl.pallas_call(
        body, name="conv_glu_bwd", grid=(FFP // LANE, nb),
        in_specs=[pl.BlockSpec((t, 2 * LANE), lambda j, b: (b, j)), pl.BlockSpec((t, LANE), lambda j, b: (b, j)),
                  pl.BlockSpec((3, 2 * LANE), lambda j, b: (0, j)), pl.BlockSpec((1, 2 * LANE), lambda j, b: (0, j))],
        out_specs=[pl.BlockSpec((t, 2 * LANE), lambda j, b: (b, j)), pl.BlockSpec((3, 2 * LANE), lambda j, b: (0, j)),
                   pl.BlockSpec((1, 2 * LANE), lambda j, b: (0, j))],
        out_shape=[jax.ShapeDtypeStruct((nb * t, 2 * FFP), MXU_DTYPE), jax.ShapeDtypeStruct((3, 2 * FFP), F32),
                   jax.ShapeDtypeStruct((1, 2 * FFP), F32)],
        compiler_params=_cparams(("arbitrary", "arbitrary")),
    )(u, dz, cw, cb)


def _gla_chunk(q, k, v, afab, wa2p, ba, s_in, reverse, sb):
    c = GLA_CHUNK
    r = sb * c
    ri = lax.broadcasted_iota(jnp.int32, (r, r), 0)
    ci = lax.broadcasted_iota(jnp.int32, (r, r), 1)
    same = (ri // c) == (ci // c)
    keep = same & ((ci >= ri) if reverse else (ci <= ri))
    i_ref = (c - 1 - c // 2) if reverse else (c // 2)
    pick_ref = (ci == (ri // c) * c + i_ref).astype(F32)
    seq_cols = (lax.broadcasted_iota(jnp.int32, (r, sb * LANE), 0) // c) == (lax.broadcasted_iota(jnp.int32, (r, sb * LANE), 1) // LANE)
    expand = lambda x: jnp.where(seq_cols, jnp.concatenate([x] * sb, axis=1), 0.0)
    lane = lax.broadcasted_iota(jnp.int32, (1, LANE), 1)
    outs, states = [None] * GLA_H, [None] * GLA_H
    for pr in range(GLA_H // 2):
        la = -_softplus(-(mm(afab, wa2p[pr]) + ba[pr])) * (1.0 / GLA_LOGIT_NORM)
        b = mm_exact(keep.astype(F32), la, b_is_01=False)
        b_ref = mm_exact(pick_ref, b, b_is_01=False)
        b_last = mm_exact(same.astype(F32), la, b_is_01=False)
        qs = q[pr] * (GLA_DK ** -0.5)
        qi = qs * jnp.exp(b - b_ref)
        ki = k[pr] * jnp.exp(b_ref - b)
        kd = k[pr] * jnp.exp(b_last - b)
        qb = qs * jnp.exp(b)
        dec = jnp.exp(mm_tn_exact(expand(la), jnp.ones((r, LANE), F32)))
        for h in (2 * pr, 2 * pr + 1):
            m = ((lane // GLA_DK) == (h % 2)).astype(F32)
            a = jnp.where(keep, mm_nt(qi * m, ki), 0.0)
            o_intra = mm(a, v[h])
            kv = mm_tn(expand(kd * m), v[h])
            o_inter = mm(expand(qb * m), s_in[h])
            outs[h] = o_intra + o_inter
            states[h] = s_in[h] * dec + kv
    return outs, states


def _gla_load(q_ref, k_ref, v_ref, af_ref, w_ref, ba_ref, sb, rows):
    stack = lambda ref, c0: jnp.concatenate([ref[s, rows, c0:c0 + LANE] for s in range(sb)], axis=0)
    q = [stack(q_ref, pr * LANE) for pr in range(GLA_H // 2)]
    k = [stack(k_ref, pr * LANE) for pr in range(GLA_H // 2)]
    v = [stack(v_ref, h * GLA_DV) for h in range(GLA_H)]
    w = [w_ref[:, pr * LANE:(pr + 1) * LANE] for pr in range(GLA_H // 2)]
    ba = [ba_ref[:, pr * LANE:(pr + 1) * LANE] for pr in range(GLA_H // 2)]
    return q, k, v, stack(af_ref, 0), w, ba


GLA_TILE = 256
GLA_SB = 4


def _gla_specs(nb, t, reverse):
    tile = min(GLA_TILE, t)
    nt = t // tile
    sb = GLA_SB if nb % GLA_SB == 0 else 1
    return tile, tile // GLA_CHUNK, nt, sb, ((lambda j: nt - 1 - j) if reverse else (lambda j: j))


def gla_fwd(p, wa2p, ba, o_add, nb, t, reverse):
    tile, cpt, nt, sb, tj = _gla_specs(nb, t, reverse)
    has_add = o_add is not None

    def body(*refs):
        if has_add:
            q_ref, k_ref, v_ref, af_ref, w_ref, ba_ref, add_ref, o_ref, hist_ref, s_ref = refs
        else:
            q_ref, k_ref, v_ref, af_ref, w_ref, ba_ref, o_ref, hist_ref, s_ref = refs

        @pl.when(pl.program_id(1) == 0)
        def _():
            s_ref[...] = jnp.zeros_like(s_ref)

        def step(i, carry):
            ci = (cpt - 1 - i) if reverse else i
            rows = pl.ds(pl.multiple_of(ci * GLA_CHUNK, GLA_CHUNK), GLA_CHUNK)
            s_in = [s_ref[h] for h in range(GLA_H)]
            for h in range(GLA_H):
                for s in range(sb):
                    hist_ref[s, ci, h] = s_in[h][s * LANE:(s + 1) * LANE]
            q, k, v, af, w, ba = _gla_load(q_ref, k_ref, v_ref, af_ref, w_ref, ba_ref, sb, rows)
            outs, states = _gla_chunk(q, k, v, af, w, ba, s_in, reverse, sb)
            for h in range(GLA_H):
                for s in range(sb):
                    oh = outs[h][s * GLA_CHUNK:(s + 1) * GLA_CHUNK]
                    if has_add:
                        oh = oh + add_ref[s, rows, h * GLA_DV:(h + 1) * GLA_DV]
                    o_ref[s, rows, h * GLA_DV:(h + 1) * GLA_DV] = oh
                s_ref[h] = states[h]
            return carry

        lax.fori_loop(0, cpt, step, 0)

    col = lambda width, c0: pl.BlockSpec((sb, tile, width), lambda b, j: (b, tj(j), c0 // width))
    in_specs = [col(256, C_Q), col(256, C_K), col(512, C_V), col(LANE, C_AFAB),
                pl.BlockSpec((LANE, 256), lambda b, j: (0, 0)), pl.BlockSpec((1, 256), lambda b, j: (0, 0))]
    p3 = p.reshape(nb, t, p.shape[1])
    args = [p3, p3, p3, p3, wa2p, ba]
    if has_add:
        in_specs.append(col(512, 0))
        args.append(o_add.reshape(nb, t, 512))
    o, hist = pl.pallas_call(
        body, name="gla_fwd_rev" if reverse else "gla_fwd", grid=(nb // sb, nt),
        in_specs=in_specs,
        out_specs=[col(512, 0), pl.BlockSpec((sb, cpt, GLA_H, LANE, LANE), lambda b, j: (b, tj(j), 0, 0, 0))],
        out_shape=[jax.ShapeDtypeStruct((nb, t, 512), F32),
                   jax.ShapeDtypeStruct((nb, t // GLA_CHUNK, GLA_H, LANE, LANE), F32)],
        scratch_shapes=[pltpu.VMEM((GLA_H, sb * LANE, LANE), F32)],
        compiler_params=_cparams(("arbitrary", "arbitrary")),
    )(*args)
    return o.reshape(nb * t, 512), hist


def gla_bwd(p, wa2p, ba, hist, do, dprev, nb, t, reverse):
    tile, cpt, nt, sb, tj_f = _gla_specs(nb, t, reverse)
    tj = lambda j: tj_f(nt - 1 - j)
    has_prev = dprev is not None

    def body(*refs):
        if has_prev:
            q_ref, k_ref, v_ref, af_ref, w_ref, ba_ref, hist_ref, do_ref, prev_ref, dqkv_ref, dw_ref, dba_ref, ds_ref = refs
        else:
            q_ref, k_ref, v_ref, af_ref, w_ref, ba_ref, hist_ref, do_ref, dqkv_ref, dw_ref, dba_ref, ds_ref = refs

        @pl.when((pl.program_id(0) == 0) & (pl.program_id(1) == 0))
        def _():
            dw_ref[...] = jnp.zeros_like(dw_ref)
            dba_ref[...] = jnp.zeros_like(dba_ref)

        @pl.when(pl.program_id(1) == 0)
        def _():
            ds_ref[...] = jnp.zeros_like(ds_ref)

        def step(i, carry):
            ci = i if reverse else (cpt - 1 - i)
            rows = pl.ds(pl.multiple_of(ci * GLA_CHUNK, GLA_CHUNK), GLA_CHUNK)
            fn = functools.partial(_gla_chunk, reverse=reverse, sb=sb)
            seqs = lambda get: jnp.concatenate([get(s) for s in range(sb)], axis=0)
            s_in = [seqs(lambda s: hist_ref[s, ci, h]) for h in range(GLA_H)]
            q, k, v, af, w, ba = _gla_load(q_ref, k_ref, v_ref, af_ref, w_ref, ba_ref, sb, rows)
            _, vjp = jax.vjp(fn, q, k, v, af, w, ba, s_in)
            d_o = [seqs(lambda s: do_ref[s, rows, h * GLA_DV:(h + 1) * GLA_DV]) for h in range(GLA_H)]
            d_s = [ds_ref[h] for h in range(GLA_H)]
            dq, dk, dv, daf, dw, dba, ds_in = vjp((d_o, d_s))
            pieces = [(pr * LANE, dq[pr]) for pr in range(2)] + [(256 + pr * LANE, dk[pr]) for pr in range(2)]
            pieces += [(512 + h * GLA_DV, dv[h]) for h in range(GLA_H)] + [(1024, daf)]
            for c0, val in pieces:
                for s in range(sb):
                    part = val[s * GLA_CHUNK:(s + 1) * GLA_CHUNK]
                    if has_prev:
                        part = part + prev_ref[s, rows, c0:c0 + LANE]
                    dqkv_ref[s, rows, c0:c0 + LANE] = part
            for pr in range(2):
                dw_ref[:, pr * LANE:(pr + 1) * LANE] += dw[pr]
                dba_ref[:, pr * LANE:(pr + 1) * LANE] += dba[pr]
            for h in range(GLA_H):
                ds_ref[h] = ds_in[h]
            return carry

        lax.fori_loop(0, cpt, step, 0)

    col = lambda width, c0: pl.BlockSpec((sb, tile, width), lambda b, j: (b, tj(j), c0 // width))
    in_specs = [col(256, C_Q), col(256, C_K), col(512, C_V), col(LANE, C_AFAB),
                pl.BlockSpec((LANE, 256), lambda b, j: (0, 0)), pl.BlockSpec((1, 256), lambda b, j: (0, 0)),
                pl.BlockSpec((sb, cpt, GLA_H, LANE, LANE), lambda b, j: (b, tj(j), 0, 0, 0)), col(512, 0)]
    p3 = p.reshape(nb, t, p.shape[1])
    args = [p3, p3, p3, p3, wa2p, ba, hist, do.reshape(nb, t, 512)]
    if has_prev:
        in_specs.append(col(1152, 0))
        args.append(dprev.reshape(nb, t, 1152))
    dqkv, dw, dba = pl.pallas_call(
        body, name="gla_bwd_rev" if reverse else "gla_bwd", grid=(nb // sb, nt),
        in_specs=in_specs,
        out_specs=[col(1152, 0), pl.BlockSpec((LANE, 256), lambda b, j: (0, 0)), pl.BlockSpec((1, 256), lambda b, j: (0, 0))],
        out_shape=[jax.ShapeDtypeStruct((nb, t, 1152), F32), jax.ShapeDtypeStruct((LANE, 256), F32),
                   jax.ShapeDtypeStruct((1, 256), F32)],
        scratch_shapes=[pltpu.VMEM((GLA_H, sb * LANE, LANE), F32)],
        compiler_params=_cparams(("arbitrary", "arbitrary")),
    )(*args)
    return dqkv.reshape(nb * t, 1152), dw, dba


SCAN_TB = 16
RW_VH = RW_N // 2


def _bwd_lanes():
    lane = lax.broadcasted_iota(jnp.int32, (1, LANE), 1)
    return ((lane // (LANE // 4)) % 2) == 1


def _comm_specs(comm):
    anyspec = pl.BlockSpec(memory_space=pl.ANY)
    n = len(comm)
    shapes = [jax.ShapeDtypeStruct((N_DEV,) + (a.shape[1:] if sc else a.shape), a.dtype) for a, sc in comm]
    sems = [pltpu.SemaphoreType.DMA((n, N_DEV - 1)), pltpu.SemaphoreType.DMA((n, N_DEV - 1)), pltpu.SemaphoreType.DMA((n,))] if n else []
    return [a for a, _ in comm], [anyspec] * n, shapes, sems


def rwkv_scan_fwd(r, w, k, a, b, v, comm=()):
    t = r.shape[0]
    nt = t // SCAN_TB
    nc = len(comm)
    flags = [sc for _, sc in comm]

    def body(*refs):
        (rf, rm, kf, km, af, am, bf, bm, wf_ref, wm_ref, vf, vm), refs = refs[:12], refs[12:]
        c_in, refs = refs[:nc], refs[nc:]
        (yf_ref, ym_ref, hist_ref, sa_ref), refs = refs[:4], refs[4:]
        c_out, refs = refs[:nc], refs[nc:]
        s_ref, sems = refs[0], refs[1:]
        i = pl.program_id(0)
        if nc:
            start, wait = _exchange_plan(flags, c_in, c_out, *sems)

        @pl.when(i == 0)
        def _():
            s_ref[...] = jnp.zeros_like(s_ref)
            if nc:
                start()

        bwd = _bwd_lanes()

        for tt in range(SCAN_TB):
            mt = SCAN_TB - 1 - tt
            pick = lambda f_ref, m_ref: jnp.where(bwd, m_ref[mt], f_ref[tt])
            rt, kt, at, bt, wt = pick(rf, rm), pick(kf, km), pick(af, am), pick(bf, bm), pick(wf_ref, wm_ref)
            for vi in range(RW_VH):
                sv = s_ref[vi] if tt == 0 else hist_ref[tt - 1, vi]
                sa = jnp.sum(sv * at, axis=0, keepdims=True)
                v_row = jnp.where(bwd, vm[mt, vi:vi + 1, :], vf[tt, vi:vi + 1, :])
                sn = sv * wt + sa * bt + v_row * kt
                hist_ref[tt, vi] = sn
                y_row = jnp.sum(sn * rt, axis=0, keepdims=True)
                yf_ref[tt, vi:vi + 1, :] = y_row
                ym_ref[mt, vi:vi + 1, :] = y_row
                sa_ref[tt, vi:vi + 1, :] = sa
        s_ref[...] = hist_ref[SCAN_TB - 1]

        if nc:
            @pl.when(i == nt - 1)
            def _():
                wait()

    fwd_map, mir_map = (lambda i: (i, 0, 0)), (lambda i: (nt - 1 - i, 0, 0))
    kf_spec, km_spec = pl.BlockSpec((SCAN_TB, RW_N, LANE), fwd_map), pl.BlockSpec((SCAN_TB, RW_N, LANE), mir_map)
    vf_spec, vm_spec = pl.BlockSpec((SCAN_TB, RW_VH, LANE), fwd_map), pl.BlockSpec((SCAN_TB, RW_VH, LANE), mir_map)
    c_args, c_specs, c_shapes, c_sems = _comm_specs(comm)
    vshape = jax.ShapeDtypeStruct((t, RW_VH, LANE), F32)
    return pl.pallas_call(
        body, name="rwkv_scan_fwd", grid=(nt,),
        in_specs=[kf_spec, km_spec] * 5 + [vf_spec, vm_spec] + c_specs,
        out_specs=[vf_spec, vm_spec, pl.BlockSpec((SCAN_TB, RW_VH, RW_N, LANE), lambda i: (i, 0, 0, 0)), vf_spec] + c_specs,
        out_shape=[vshape, vshape, jax.ShapeDtypeStruct((t, RW_VH, RW_N, LANE), F32), vshape] + c_shapes,
        scratch_shapes=[pltpu.VMEM((RW_VH, RW_N, LANE), F32)] + c_sems,
        compiler_params=_cparams(("arbitrary",)),
    )(r, r, k, k, a, a, b, b, w, w, v, v, *c_args)


def rwkv_scan_bwd(r, w, k, a, b, v, hist, sa, dy, comm=()):
    t = r.shape[0]
    nt = t // SCAN_TB
    nc = len(comm)
    flags = [sc for _, sc in comm]

    def body(*refs):
        (rf, rm, kf, km, af, am, bf, bm, wf_ref, wm_ref, vf, vm, hist_ref, prev_ref, sa_ref, dyf, dym), refs = refs[:17], refs[17:]
        c_in, refs = refs[:nc], refs[nc:]
        k_outs, (dvf_ref, dvm_ref), refs = refs[:4], refs[4:6], refs[6:]
        c_out, refs = refs[:nc], refs[nc:]
        ds_ref, sems = refs[0], refs[1:]
        i = pl.program_id(0)
        if nc:
            start, wait = _exchange_plan(flags, c_in, c_out, *sems)

        @pl.when(i == 0)
        def _():
            ds_ref[...] = jnp.zeros_like(ds_ref)
            if nc:
                start()

        bwd = _bwd_lanes()
        group = lax.broadcasted_iota(jnp.int32, (1, LANE), 1) // RW_Q
        first_block = i == nt - 1

        for tt in range(SCAN_TB - 1, -1, -1):
            mt = SCAN_TB - 1 - tt
            pick = lambda f_ref, m_ref: jnp.where(bwd, m_ref[mt], f_ref[tt])
            rt, kt, at, bt, wt = pick(rf, rm), pick(kf, km), pick(af, am), pick(bf, bm), pick(wf_ref, wm_ref)
            zero = jnp.zeros((RW_N, LANE), F32)
            dr, dw, dk, da, db = zero, zero, zero, zero, zero
            for vi in range(RW_VH):
                sn = hist_ref[tt, vi]
                sv = hist_ref[tt - 1, vi] if tt > 0 else jnp.where(first_block, 0.0, prev_ref[0, vi])
                sa_row = sa_ref[tt, vi:vi + 1, :]
                v_row = jnp.where(bwd, vm[mt, vi:vi + 1, :], vf[tt, vi:vi + 1, :])
                dy_row = jnp.where(bwd, dym[mt, vi:vi + 1, :], dyf[tt, vi:vi + 1, :])
                dsv = ds_ref[vi] + dy_row * rt
                dr = dr + sn * dy_row
                dsa = jnp.sum(dsv * bt, axis=0, keepdims=True)
                dw = dw + sv * dsv
                db = db + dsv * sa_row
                dk = dk + dsv * v_row
                dv_row = jnp.sum(dsv * kt, axis=0, keepdims=True)
                dvf_ref[tt, vi:vi + 1, :] = dv_row
                dvm_ref[mt, vi:vi + 1, :] = dv_row
                da = da + sv * dsa
                ds_ref[vi] = dsv * wt + dsa * at
            dr, dw, dk, da, db = [val + pltpu.roll(val, LANE // 2, 1) for val in (dr, dw, dk, da, db)]
            up, down = (lambda val: pltpu.roll(val, RW_Q, 1)), (lambda val: pltpu.roll(val, LANE - RW_Q, 1))
            packed_f = jnp.where(group == 0, dr, jnp.where(group == 1, up(dk), jnp.where(group == 2, da, up(db))))
            packed_m = jnp.where(group == 0, down(dr), jnp.where(group == 1, dk, jnp.where(group == 2, down(da), db)))
            k_outs[0][tt] = packed_f
            k_outs[1][mt] = packed_m
            k_outs[2][tt] = dw
            k_outs[3][mt] = dw

        if nc:
            @pl.when(i == nt - 1)
            def _():
                wait()

    fwd_map, mir_map = (lambda i: (nt - 1 - i, 0, 0)), (lambda i: (i, 0, 0))
    kf_spec, km_spec = pl.BlockSpec((SCAN_TB, RW_N, LANE), fwd_map), pl.BlockSpec((SCAN_TB, RW_N, LANE), mir_map)
    vf_spec, vm_spec = pl.BlockSpec((SCAN_TB, RW_VH, LANE), fwd_map), pl.BlockSpec((SCAN_TB, RW_VH, LANE), mir_map)
    prev_spec = pl.BlockSpec((1, RW_VH, RW_N, LANE), lambda i: (jnp.maximum((nt - 1 - i) * SCAN_TB - 1, 0), 0, 0, 0))
    c_args, c_specs, c_shapes, c_sems = _comm_specs(comm)
    kshape, vshape = jax.ShapeDtypeStruct((t, RW_N, LANE), F32), jax.ShapeDtypeStruct((t, RW_VH, LANE), F32)
    return pl.pallas_call(
        body, name="rwkv_scan_bwd", grid=(nt,),
        in_specs=[kf_spec, km_spec] * 5 + [vf_spec, vm_spec,
                                           pl.BlockSpec((SCAN_TB, RW_VH, RW_N, LANE), lambda i: (nt - 1 - i, 0, 0, 0)),
                                           prev_spec, vf_spec, vf_spec, vm_spec] + c_specs,
        out_specs=[kf_spec, km_spec] * 2 + [vf_spec, vm_spec] + c_specs,
        out_shape=[kshape] * 4 + [vshape] * 2 + c_shapes,
        scratch_shapes=[pltpu.VMEM((RW_VH, RW_N, LANE), F32)] + c_sems,
        compiler_params=_cparams(("arbitrary",)),
    )(r, r, k, k, a, a, b, b, w, w, v, v, hist, hist, sa, dy, dy, *c_args)


RELAYOUT_TB = 128
RW_Q = LANE // 4


def to_scan(name, x, cb, nb, t, value, x_bwd=None):
    tb = min(RELAYOUT_TB, t)
    rows_out = RW_VH if value else RW_N
    ins = [x] if x_bwd is None else [x, x_bwd]

    def body(*refs):
        x_refs, o_ref, scrs = refs[:len(ins)], refs[len(ins)], refs[len(ins) + 1:]
        for x_ref, scr in zip(x_refs, scrs):
            for b in range(nb):
                scr[b * RW_H:(b + 1) * RW_H] = x_ref[b].T.reshape(RW_H, RW_N, tb)
        for j in range(rows_out):
            lo = scrs[0][:, j, :]
            if value:
                hi = scrs[0][:, j + RW_VH, :]
                blk = [lo, lo, hi, hi]
            else:
                other = lo if x_bwd is None else scrs[1][:, j, :]
                blk = [lo, other, lo, other]
            o_ref[:, j, :] = jnp.concatenate(blk, axis=0).T

    return pl.pallas_call(
        body, name=name, grid=(t // tb,),
        in_specs=[pl.BlockSpec((nb, tb, RW_W), lambda i: (0, i, cb))] + [pl.BlockSpec((nb, tb, RW_W), lambda i: (0, i, 0))] * (len(ins) - 1),
        out_specs=pl.BlockSpec((tb, rows_out, LANE), lambda i: (i, 0, 0)),
        out_shape=jax.ShapeDtypeStruct((t, rows_out, LANE), F32),
        scratch_shapes=[pltpu.VMEM((nb * RW_H, RW_N, tb), F32)] * len(ins),
        compiler_params=_cparams(("arbitrary",)),
    )(*[a.reshape(nb, t, a.shape[1]) for a in ins])


def from_scan(name, xf, xm, nb, t, value, groups=None):
    tb = min(RELAYOUT_TB, t)
    rows_in = RW_VH if value else RW_N
    n_out = 1 if value else (4 if groups is None else 2)
    grp = lambda a, g: a[g * RW_Q:(g + 1) * RW_Q]

    def body(f_ref, m_ref, *rest):
        outs, scrs = rest[:n_out], rest[n_out:]
        lane_group = lax.broadcasted_iota(jnp.int32, (1, LANE), 1) // RW_Q
        for j in range(rows_in):
            f, m = f_ref[:, j, :], m_ref[:, j, :]
            if value:
                c = jnp.where(_bwd_lanes(), m, f).T
                scrs[0][:, j, :] = grp(c, 0) + grp(c, 1)
                scrs[0][:, j + RW_VH, :] = grp(c, 2) + grp(c, 3)
            elif groups is None:
                c = (f + m).T
                for q, scr in enumerate(scrs):
                    scr[:, j, :] = grp(c, q)
            else:
                c = jnp.where(lane_group == groups[1], m, f).T
                scrs[0][:, j, :] = grp(c, groups[0])
                scrs[1][:, j, :] = grp(c, groups[1])
        for o_ref, scr in zip(outs, scrs):
            for b in range(nb):
                o_ref[b] = scr[b * RW_H:(b + 1) * RW_H].reshape(RW_W, tb).T

    res = pl.pallas_call(
        body, name=name, grid=(t // tb,),
        in_specs=[pl.BlockSpec((tb, rows_in, LANE), lambda i: (i, 0, 0))] * 2,
        out_specs=[pl.BlockSpec((nb, tb, RW_W), lambda i: (0, i, 0))] * n_out,
        out_shape=[jax.ShapeDtypeStruct((nb, t, RW_W), F32)] * n_out,
        scratch_shapes=[pltpu.VMEM((nb * RW_H, RW_N, tb), F32)] * n_out,
        compiler_params=_cparams(("arbitrary",)),
    )(xf, xm)
    return [r.reshape(nb * t, RW_W) for r in res]


def f_norm(rows, params):
    (x,), (g,) = rows, params
    return [_rmsnorm(x, g)]


def f_rwkv_pre(rows, params):
    k, wlal, gl = rows
    w0f, w2f, w0b, w2b, a0, a2, g2, k_k, k_a = params
    tw = jnp.tanh(wlal)

    def decay(w0, w2):
        return jnp.exp(-jnp.exp(-_softplus(-(w0 + mm(tw, w2))) - 0.5))

    lr = _sigmoid(a0 + mm(wlal, a2))
    gate = mm(_sigmoid(gl), g2)
    kk = k * k_k
    kk = kk / jnp.maximum(jnp.sqrt(_segment_sum(kk * kk, RW_N)), 1e-12)
    kp = k * (1.0 + (lr - 1.0) * k_a)
    return [decay(w0f, w2f), decay(w0b, w2b), kp, -kk, kk * lr, gate]


def f_branch_post(rows, params):
    o, og, y, r, kp, v, g = rows
    gla_g, ln_w, ln_b, r_k = params
    on = o * lax.rsqrt(_segment_sum(o * o, GLA_DV) * (1.0 / GLA_DV) + HEAD_NORM_EPS)
    oa = on * gla_g * _silu(og)
    mu = _segment_sum(y, RW_N) * (1.0 / RW_N)
    yc = y - mu
    var = _segment_sum(yc * yc, RW_N) * (1.0 / RW_N)
    yn = yc * lax.rsqrt(var + RW_GN_EPS) * ln_w + ln_b
    bonus = _segment_sum(r * kp * r_k, RW_N) * v
    return [oa, (yn + bonus) * g]


def f_merge(rows, params):
    ga, gb, ya, yb = rows
    return [_sigmoid(ga) * ya + _sigmoid(gb) * yb]


def f_norm2(rows, params):
    (x, mo), (g,) = rows, params
    x1 = x + mo
    return [x1, _rmsnorm(x1, g)]


def loss_head(x1, ffo, tgt, gf, tm):
    n = x1.shape[0]

    def body(x1_ref, f_ref, t_ref, g_ref, loss_ref, dx_ref, dg_ref):
        @pl.when(pl.program_id(0) == 0)
        def _():
            loss_ref[...] = jnp.zeros_like(loss_ref)
            dg_ref[...] = jnp.zeros_like(dg_ref)

        tgt_v = t_ref[...]

        def f(x2, g):
            err = _rmsnorm(x2, g) - tgt_v
            return jnp.sum(jnp.sum(err * err, axis=-1, keepdims=True), axis=0, keepdims=True) * (0.5 / D)

        val, vjp = jax.vjp(f, x1_ref[...] + f_ref[...], g_ref[...])
        dx, dg = vjp(jnp.ones((1, 1), F32))
        loss_ref[...] += val
        dx_ref[...] = dx
        dg_ref[...] += dg

    return pl.pallas_call(
        body, name="loss_head", grid=(n // tm,),
        in_specs=[_row_spec(tm, D, 0)] * 3 + [_full_spec((1, D))],
        out_specs=[_full_spec((1, 1)), _row_spec(tm, D, 0), _full_spec((1, D))],
        out_shape=[jax.ShapeDtypeStruct((1, 1), F32), jax.ShapeDtypeStruct((n, D), F32), jax.ShapeDtypeStruct((1, D), F32)],
        compiler_params=_cparams(("arbitrary",)),
    )(x1, ffo, tgt, gf)


def _pad_cols(a, width):
    return jnp.pad(a, ((0, 0), (0, width - a.shape[1])))


def w_in_to_padded(w):
    return _pad_cols(jnp.concatenate([w[:, 3360:5408], w[:, 0:1536], w[:, 1568:3360], w[:, 1536:1568]], axis=1), NP)


def w_in_from_padded(wp):
    return jnp.concatenate([wp[:, 2048:3584], wp[:, 5376:5408], wp[:, 3584:5376], wp[:, 0:2048]], axis=1)


def ff_interleave(a):
    r = a.shape[0]
    halves = jnp.stack([_pad_cols(a[:, :D_FF], FFP), _pad_cols(a[:, D_FF:], FFP)], axis=1)
    return halves.reshape(r, 2, FFP // LANE, LANE).transpose(0, 2, 1, 3).reshape(r, 2 * FFP)


def ff_deinterleave(a):
    r = a.shape[0]
    halves = a.reshape(r, FFP // LANE, 2, LANE).transpose(0, 2, 1, 3).reshape(r, 2, FFP)
    return halves[:, :, :D_FF].reshape(r, 2 * D_FF)


def _rows_into(w, rows, off):
    return jnp.zeros((rows, w.shape[1]), w.dtype).at[off:off + w.shape[0]].set(w)


LATE = ("gla_proj", "rwkv_proj", "w_out", "ffn_up", "ffn_conv_w", "ffn_down")


def local_step(x, tgt, w, nb, t, late_blocks=None):
    n = nb * t
    tm = min(n, 1024)
    tkt = min(n, 2048)
    tr = min(n, 256)
    vec = lambda a: a.reshape(1, -1)
    w = dict(w)

    w_in_p = w_in_to_padded(w["w_in"])
    wa2_f, wa2_b = _rows_into(w["gla_wa2_f"], LANE, 0), _rows_into(w["gla_wa2_b"], LANE, GLA_RANK)
    w2f, w2b = _rows_into(w["rwkv_w2_f"], LANE, 0), _rows_into(w["rwkv_w2_b"], LANE, 0)
    a2 = _rows_into(w["rwkv_a2"], LANE, 64)
    g1, g2n, gf = vec(w["norm1_g"]), vec(w["norm2_g"]), vec(w["norm_f_g"])
    mu_prev, mu_next = vec(w["rwkv_mu_prev"]), vec(w["rwkv_mu_next"])
    pre_params = [vec(w["rwkv_w0_f"]), w2f, vec(w["rwkv_w0_b"]), w2b, vec(w["rwkv_a0"]), a2, w["rwkv_g2"],
                  vec(w["rwkv_k_k"]), vec(w["rwkv_k_a"])]
    post_params = [vec(w["gla_norm_g"]), vec(w["rwkv_ln_w"]), vec(w["rwkv_ln_b"]), vec(w["rwkv_r_k"])]
    ba_f, ba_b = vec(w["gla_ba_f"]), vec(w["gla_ba_b"])

    (h1,) = rowwise_fwd("norm1_fwd", f_norm, [(x, D, 0)], [g1], [(D, MXU_DTYPE)], tr)
    p = matmul("proj_in", h1, w_in_p, "nn", F32, tm, FFP // 2, D)
    s = shift_fwd(p, mu_prev, mu_next, nb, t)
    pre_rows = [(s, 512, 1), (s, LANE, 1536 // LANE), (s, LANE, 1664 // LANE)]
    wf, wb, kp, a_s, b_s, g = rowwise_fwd("rwkv_pre_fwd", f_rwkv_pre, pre_rows, pre_params, [(RW_W, F32)] * 6, tr)
    sc = [to_scan("to_scan_r", s, 0, nb, t, False), to_scan("to_scan_w", wf, 0, nb, t, False, x_bwd=wb),
          to_scan("to_scan_k", kp, 0, nb, t, False), to_scan("to_scan_a", a_s, 0, nb, t, False),
          to_scan("to_scan_b", b_s, 0, nb, t, False), to_scan("to_scan_v", s, 2, nb, t, True)]
    comm = [] if late_blocks is None else [(late_blocks[k], False) for k in LATE]
    y_scf, y_scm, hist_rw, sa_sc, *gathered = rwkv_scan_fwd(*sc, comm=comm)
    for k, g_k in zip(LATE, gathered):
        w[k] = _gathered_to_full(g_k, SHARDED[k])
    ffn_up_p = ff_interleave(w["ffn_up"])
    conv_w_p, conv_b_p = ff_interleave(w["ffn_conv_w"]), ff_interleave(vec(w["ffn_conv_b"]))
    ffn_down_p = jnp.pad(w["ffn_down"], ((0, FFP - D_FF), (0, 0)))
    (y,) = from_scan("from_scan_y", y_scf, y_scm, nb, t, True)
    o_f, hist_f = gla_fwd(p, wa2_f, ba_f, None, nb, t, False)
    o, hist_b = gla_fwd(p, wa2_b, ba_b, o_f, nb, t, True)
    post_rows = [(o, 512, 0), (p, 512, C_OG // 512), (y, 512, 0), (s, 512, 0), (kp, 512, 0), (s, 512, 2), (g, 512, 0)]
    oa, ob = rowwise_fwd("branch_post_fwd", f_branch_post, post_rows, post_params, [(512, MXU_DTYPE)] * 2, tr)
    ya = matmul("gla_proj", oa, w["gla_proj"], "nn", F32, tm, 512, 512)
    yb = matmul("rwkv_proj", ob, w["rwkv_proj"], "nn", F32, tm, 512, 512)
    merge_rows = [(p, D, 0), (p, D, 1), (ya, D, 0), (yb, D, 0)]
    (merged,) = rowwise_fwd("merge_fwd", f_merge, merge_rows, [], [(D, MXU_DTYPE)], tr)
    mo = matmul("w_out", merged, w["w_out"], "nn", F32, tm, 512, D)
    x1, h2 = rowwise_fwd("norm2_fwd", f_norm2, [(x, D, 0), (mo, D, 0)], [g2n], [(D, F32), (D, MXU_DTYPE)], tr)
    u = matmul("ffn_up", h2, ffn_up_p, "nn", F32, tm, FFP // 2, D)
    z = conv_glu_fwd(u, conv_w_p, conv_b_p, nb, t)
    ffo = matmul("ffn_down", z, ffn_down_p, "nn", F32, tm, D, FFP // 2)
    loss, dx2, dgf = loss_head(x1, ffo, tgt, gf, tr)

    dz = matmul("ffn_down_dx", dx2, ffn_down_p, "nt", F32, tm, FFP // 2, D)
    d_ffn_down_p = matmul("ffn_down_dw", z, dx2, "tn", F32, FFP // 2, 512, tkt)
    du, d_conv_w_p, d_conv_b_p = conv_glu_bwd(u, dz, conv_w_p, conv_b_p, nb, t)
    dh2 = matmul("ffn_up_dx", du, ffn_up_p, "nt", F32, tm, D, FFP // 2)
    d_ffn_up_p = matmul("ffn_up_dw", h2, du, "tn", F32, D, 512, tkt)
    (dx1,), (dg2,) = rowwise_bwd("norm2_bwd", f_norm2, [(x, D, 0), (mo, D, 0)], [g2n],
                                 [[(dx2, D, 0)], [(dh2, D, 0)]], tr, grad_rows=[1])
    dmerged = matmul("w_out_dx", dx1, w["w_out"], "nt", F32, tm, D, D)
    d_w_out = matmul("w_out_dw", merged, dx1, "tn", F32, D, 512, tkt)
    (dga, dgb, dya, dyb), _ = rowwise_bwd("merge_bwd", f_merge, merge_rows, [], [[(dmerged, D, 0)]], tr)
    d_oa = matmul("gla_proj_dx", dya, w["gla_proj"], "nt", F32, tm, 512, D)
    d_gla_proj = matmul("gla_proj_dw", oa, dya, "tn", F32, 512, 512, tkt)
    d_ob = matmul("rwkv_proj_dx", dyb, w["rwkv_proj"], "nt", F32, tm, 512, D)
    d_rwkv_proj = matmul("rwkv_proj_dw", ob, dyb, "tn", F32, 512, 512, tkt)
    (d_o, d_og, d_y, d_r_post, d_kp_post, d_v_post, d_g), d_post = rowwise_bwd(
        "branch_post_bwd", f_branch_post, post_rows, post_params, [[(d_oa, 512, 0)], [(d_ob, 512, 0)]], tr)
    late_grads = {"gla_proj": d_gla_proj, "rwkv_proj": d_rwkv_proj, "w_out": d_w_out, "ffn_up": ff_deinterleave(d_ffn_up_p),
                  "ffn_conv_w": ff_deinterleave(d_conv_w_p), "ffn_down": d_ffn_down_p[0:D_FF]}
    comm = [] if late_blocks is None else [(_full_to_slices(late_grads[k], SHARDED[k]), True) for k in LATE]
    dsc = rwkv_scan_bwd(*sc, hist_rw, sa_sc, to_scan("to_scan_dy", d_y, 0, nb, t, True), comm=comm)
    received = dict(zip(LATE, dsc[6:]))
    d_r_scan, d_kp_scan, d_a_scan, d_b_scan = from_scan("from_scan_rkab", dsc[0], dsc[1], nb, t, False)
    d_wf, d_wb = from_scan("from_scan_w", dsc[2], dsc[3], nb, t, False, groups=(0, 1))
    (d_v_scan,) = from_scan("from_scan_dv", dsc[4], dsc[5], nb, t, True)
    (d_k, d_wlal, d_gl), d_pre = rowwise_bwd(
        "rwkv_pre_bwd", f_rwkv_pre, pre_rows, pre_params,
        [[(d_wf, 512, 0)], [(d_wb, 512, 0)], [(d_kp_scan, 512, 0), (d_kp_post, 512, 0)],
         [(d_a_scan, 512, 0)], [(d_b_scan, 512, 0)], [(d_g, 512, 0)]], tr)
    ds = jnp.concatenate([d_r_scan + d_r_post, d_k, d_v_scan + d_v_post, d_wlal, d_gl], axis=1)
    dp_rw, d_mu_prev, d_mu_next = shift_bwd(p, ds, mu_prev, mu_next, nb, t)
    dqkv_f, d_wa2_f, d_ba_f = gla_bwd(p, wa2_f, ba_f, hist_f, d_o, None, nb, t, False)
    dqkv, d_wa2_b, d_ba_b = gla_bwd(p, wa2_b, ba_b, hist_b, d_o, dqkv_f, nb, t, True)
    dp = jnp.concatenate([dga, dgb, dqkv[:, 0:1024], d_og, dp_rw, dqkv[:, 1024:1152],
                          jnp.zeros((n, NP - C_AFAB - LANE), F32)], axis=1).astype(MXU_DTYPE)
    d_w_in_p = matmul("proj_in_dw", h1, dp, "tn", F32, D, 512, tkt)
    grads = {
        "w_in": w_in_from_padded(d_w_in_p),
        "gla_wa2_f": d_wa2_f[0:GLA_RANK], "gla_ba_f": d_ba_f, "gla_wa2_b": d_wa2_b[GLA_RANK:2 * GLA_RANK], "gla_ba_b": d_ba_b,
        "gla_norm_g": d_post[0], "rwkv_mu_prev": d_mu_prev, "rwkv_mu_next": d_mu_next,
        "rwkv_w0_f": d_pre[0], "rwkv_w2_f": d_pre[1][0:64], "rwkv_w0_b": d_pre[2], "rwkv_w2_b": d_pre[3][0:64],
        "rwkv_a0": d_pre[4], "rwkv_a2": d_pre[5][64:128], "rwkv_g2": d_pre[6], "rwkv_k_k": d_pre[7], "rwkv_k_a": d_pre[8],
        "rwkv_r_k": d_post[3], "rwkv_ln_w": d_post[1], "rwkv_ln_b": d_post[2],
        "norm2_g": dg2, "ffn_conv_b": ff_deinterleave(d_conv_b_p), "norm_f_g": dgf, **late_grads,
    }
    early = [k for k in SHARDED if k not in LATE]
    payload = lambda k: _full_to_slices(grads[k], SHARDED[k]).astype(MXU_DTYPE if k == "w_in" else F32)
    comm = [] if late_blocks is None else [(payload(k), True) for k in early]
    dh1, *got = matmul("proj_in_dx", dp, w_in_p, "nt", F32, tm, D, FFP // 2, comm=comm) if comm else \
        [matmul("proj_in_dx", dp, w_in_p, "nt", F32, tm, D, FFP // 2)]
    received.update(zip(early, got))
    (grad_x,), (grads["norm1_g"],) = rowwise_bwd("norm1_bwd", f_norm, [(x, D, 0)], [g1], [[(dh1, D, 0)]], tr,
                                                 adds=[(0, (dx1, D, 0))])
    return loss, grad_x, grads, received


MESH = pl.DeviceIdType.MESH


def remote_exchange(name, items):
    n = len(items)

    def body(*refs):
        start, wait = _exchange_plan([sc for _, sc in items], refs[:n], refs[n:2 * n], *refs[2 * n:])
        start()
        wait()

    args, specs, shapes, sems = _comm_specs(items)
    return pl.pallas_call(body, name=name, in_specs=specs, out_specs=specs, out_shape=shapes, scratch_shapes=sems)(*args)


def gather_two_level(name, blocks):
    n = len(blocks)

    def body(*refs):
        in_refs, out_refs = refs[:n], refs[n:2 * n]
        send_sems, recv_sems, local_sems = refs[2 * n:]
        x, y, c = lax.axis_index("x"), lax.axis_index("y"), lax.axis_index("c")
        me, sibling = (x, y, c), (x, y, 1 - c)
        chips = [(1 - x, y), (x, 1 - y), (1 - x, 1 - y)]

        def copy(i, k, block, to, src=None):
            rows = out_refs[i].at[4 * block[0] + 2 * block[1] + block[2]]
            return pltpu.make_async_remote_copy(src_ref=rows if src is None else src, dst_ref=rows, send_sem=send_sems.at[i, k],
                                                recv_sem=recv_sems.at[i, k], device_id=to, device_id_type=MESH)

        own = [pltpu.make_async_copy(in_refs[i], out_refs[i].at[4 * x + 2 * y + c], local_sems.at[i]) for i in range(n)]
        first = [copy(i, 0, me, sibling, src=in_refs[i]) for i in range(n)]
        first += [copy(i, 1 + j, me, (*chip, c), src=in_refs[i]) for j, chip in enumerate(chips) for i in range(n)]
        for cp in own + first:
            cp.start()
        passed = []
        for j, chip in enumerate(chips):
            for i in range(n):
                copy(i, 1 + j, (*chip, c), me).wait_recv()
                onward = copy(i, 4 + j, (*chip, c), sibling)
                onward.start()
                passed.append(onward)
        for i in range(n):
            copy(i, 0, sibling, me).wait_recv()
        for j, chip in enumerate(chips):
            for i in range(n):
                copy(i, 4 + j, (*chip, 1 - c), me).wait_recv()
        for cp in first + passed:
            cp.wait_send()
        for cp in own:
            cp.wait()

    args, specs, shapes, sems = _comm_specs([(b, False) for b in blocks])
    return pl.pallas_call(body, name=name, in_specs=specs, out_specs=specs, out_shape=shapes, scratch_shapes=sems)(*args)


def _exchange_plan(flags, in_refs, out_refs, send_sems, recv_sems, local_sems):
    x, y, c = lax.axis_index("x"), lax.axis_index("y"), lax.axis_index("c")
    me = 4 * x + 2 * y + c

    def peer(k):
        px = 1 - x if (k >> 2) & 1 else x
        py = 1 - y if (k >> 1) & 1 else y
        pc = 1 - c if k & 1 else c
        return (px, py, pc), 4 * px + 2 * py + pc

    def copies(with_arrivals):
        own, sends, recvs = [], [], []
        for i, scatter in enumerate(flags):
            src = in_refs[i].at[me] if scatter else in_refs[i]
            own.append(pltpu.make_async_copy(src, out_refs[i].at[me], local_sems.at[i]))
        for k in range(1, N_DEV):
            dev, slot = peer(k)
            for i, scatter in enumerate(flags):
                src = in_refs[i].at[slot] if scatter else in_refs[i]
                pair = dict(send_sem=send_sems.at[i, k - 1], recv_sem=recv_sems.at[i, k - 1], device_id=dev, device_id_type=MESH)
                sends.append(pltpu.make_async_remote_copy(src_ref=src, dst_ref=out_refs[i].at[me], **pair))
                if with_arrivals:
                    recvs.append(pltpu.make_async_remote_copy(src_ref=out_refs[i].at[slot], dst_ref=out_refs[i].at[slot], **pair))
        return own, sends, recvs

    def start():
        own, sends, _ = copies(False)
        for cp in own + sends:
            cp.start()

    def wait():
        own, sends, recvs = copies(True)
        for send, recv in zip(sends, recvs):
            recv.wait_recv()
            send.wait_send()
        for cp in own:
            cp.wait()

    return start, wait


def _adam_tiles(r, c):
    tc = 256 if (c % 256 == 0 and r * c > 128 * 1024) else c
    tr = 128 if (r % 128 == 0 and r > 128) else r
    return tr, tc


def adamw_reduce(name, parts, w, m, v):
    lead = w.ndim - 2
    r, c = w.shape[lead:]
    tr, tc = _adam_tiles(r, c)

    def body(p_ref, w_ref, m_ref, v_ref, g_ref, d_ref, nm_ref, nv_ref):
        g = p_ref[0].astype(F32)
        for d in range(1, N_DEV):
            g = g + p_ref[d].astype(F32)
        at = (0,) * lead + (slice(None), slice(None))
        nm = ADAM_B1 * m_ref[at] + (1.0 - ADAM_B1) * g
        nv = ADAM_B2 * v_ref[at] + (1.0 - ADAM_B2) * (g * g)
        m_hat = nm / (1.0 - ADAM_B1 ** ADAM_STEP)
        v_hat = nv / (1.0 - ADAM_B2 ** ADAM_STEP)
        g_ref[at] = g
        d_ref[at] = -ADAM_LR * (m_hat / (jnp.sqrt(v_hat) + ADAM_EPS) + ADAM_WD * w_ref[at])
        nm_ref[at] = nm
        nv_ref[at] = nv

    spec = pl.BlockSpec((1,) * lead + (tr, tc), lambda i, j: (0,) * lead + (i, j))
    return pl.pallas_call(
        body, name=name, grid=(r // tr, c // tc),
        in_specs=[pl.BlockSpec((N_DEV, tr, tc), lambda i, j: (0, i, j)), spec, spec, spec],
        out_specs=[spec] * 4, out_shape=[jax.ShapeDtypeStruct(w.shape, F32)] * 4,
        compiler_params=_cparams(("arbitrary", "arbitrary")),
    )(parts, w, m, v)


SHARDED = {"w_in": 1, "gla_wa2_f": 1, "gla_wa2_b": 1, "gla_proj": 1, "rwkv_w2_f": 1, "rwkv_w2_b": 1, "rwkv_a2": 1,
           "rwkv_g2": 1, "rwkv_proj": 1, "w_out": 0, "ffn_up": 1, "ffn_conv_w": 1, "ffn_down": 0}
BF16_GATHER = ("w_in", "gla_proj", "rwkv_proj", "w_out", "ffn_up", "ffn_down")
REPLICATED = ("norm1_g", "gla_ba_f", "gla_ba_b", "gla_norm_g", "rwkv_mu_prev", "rwkv_mu_next", "rwkv_w0_f", "rwkv_w0_b",
              "rwkv_a0", "rwkv_k_k", "rwkv_k_a", "rwkv_r_k", "rwkv_ln_w", "rwkv_ln_b", "norm2_g", "ffn_conv_b", "norm_f_g")
WEIGHTS = ("norm1_g", "w_in", "gla_wa2_f", "gla_ba_f", "gla_wa2_b", "gla_ba_b", "gla_norm_g", "gla_proj", "rwkv_mu_prev",
           "rwkv_mu_next", "rwkv_w0_f", "rwkv_w2_f", "rwkv_w0_b", "rwkv_w2_b", "rwkv_a0", "rwkv_a2", "rwkv_g2", "rwkv_k_k",
           "rwkv_k_a", "rwkv_r_k", "rwkv_ln_w", "rwkv_ln_b", "rwkv_proj", "w_out", "norm2_g", "ffn_up", "ffn_conv_w",
           "ffn_conv_b", "ffn_down", "norm_f_g")


def _gathered_to_full(g, axis):
    if axis == 0:
        return g.reshape(N_DEV * g.shape[1], g.shape[2])
    return g.transpose(1, 0, 2).reshape(g.shape[1], N_DEV * g.shape[2])


def _full_to_slices(a, axis):
    if axis == 0:
        return a.reshape(N_DEV, a.shape[0] // N_DEV, a.shape[1])
    return a.reshape(a.shape[0], N_DEV, a.shape[1] // N_DEV).transpose(1, 0, 2)


def _pack_rows(size):
    return -(-size // (8 * LANE)) * 8


def _pack(d):
    parts = []
    for k in REPLICATED:
        rows = d[k].reshape(-1, LANE).astype(F32)
        parts.append(jnp.pad(rows, ((0, _pack_rows(rows.size) - rows.shape[0]), (0, 0))))
    return jnp.concatenate(parts, axis=0)


def _unpack(packed, shapes):
    out, pos = {}, 0
    for k in REPLICATED:
        size = int(np.prod(shapes[k]))
        out[k] = packed[pos:pos + size // LANE].reshape(shapes[k])
        pos += _pack_rows(size)
    return out


def kernel(x, norm1_g, w_in, gla_wa2_f, gla_ba_f, gla_wa2_b, gla_ba_b, gla_norm_g, gla_proj, rwkv_mu_prev, rwkv_mu_next, rwkv_w0_f, rwkv_w2_f, rwkv_w0_b, rwkv_w2_b, rwkv_a0, rwkv_a2, rwkv_g2, rwkv_k_k, rwkv_k_a, rwkv_r_k, rwkv_ln_w, rwkv_ln_b, rwkv_proj, w_out, norm2_g, ffn_up, ffn_conv_w, ffn_conv_b, ffn_down, norm_f_g, loss_target, m_norm1_g, m_w_in, m_gla_wa2_f, m_gla_ba_f, m_gla_wa2_b, m_gla_ba_b, m_gla_norm_g, m_gla_proj, m_rwkv_mu_prev, m_rwkv_mu_next, m_rwkv_w0_f, m_rwkv_w2_f, m_rwkv_w0_b, m_rwkv_w2_b, m_rwkv_a0, m_rwkv_a2, m_rwkv_g2, m_rwkv_k_k, m_rwkv_k_a, m_rwkv_r_k, m_rwkv_ln_w, m_rwkv_ln_b, m_rwkv_proj, m_w_out, m_norm2_g, m_ffn_up, m_ffn_conv_w, m_ffn_conv_b, m_ffn_down, m_norm_f_g, v_norm1_g, v_w_in, v_gla_wa2_f, v_gla_ba_f, v_gla_wa2_b, v_gla_ba_b, v_gla_norm_g, v_gla_proj, v_rwkv_mu_prev, v_rwkv_mu_next, v_rwkv_w0_f, v_rwkv_w2_f, v_rwkv_w0_b, v_rwkv_w2_b, v_rwkv_a0, v_rwkv_a2, v_rwkv_g2, v_rwkv_k_k, v_rwkv_k_a, v_rwkv_r_k, v_rwkv_ln_w, v_rwkv_ln_b, v_rwkv_proj, v_w_out, v_norm2_g, v_ffn_up, v_ffn_conv_w, v_ffn_conv_b, v_ffn_down, v_norm_f_g):
    args = locals()
    wts = {k: args[k] for k in WEIGHTS}
    mom = {k: args["m_" + k] for k in WEIGHTS}
    var = {k: args["v_" + k] for k in WEIGHTS}
    shapes = {k: wts[k].shape for k in WEIGHTS}
    nb, t = x.shape[0], x.shape[1]
    mat = lambda a: a.reshape(a.shape[-2], a.shape[-1])

    block = lambda k: mat(wts[k]).astype(MXU_DTYPE) if k in BF16_GATHER else mat(wts[k])
    early = [k for k in SHARDED if k not in LATE]
    gathered = gather_two_level("gather_weights", [block(k) for k in early])
    full = {k: _gathered_to_full(g, SHARDED[k]) for k, g in zip(early, gathered)}
    for k in REPLICATED:
        full[k] = wts[k].reshape(-1) if k in ("norm_f_g", "rwkv_r_k") else wts[k][0]

    loss, grad_x, grads, received = local_step(x.reshape(nb * t, D), loss_target.reshape(nb * t, D), full, nb, t,
                                               late_blocks={k: block(k) for k in LATE})

    (rep_parts,) = remote_exchange("exchange_replicated", [(_pack(grads), False)])

    res = {}
    for k in SHARDED:
        res[k] = adamw_reduce("adamw_" + k, received[k], wts[k], mom[k], var[k])
    packed = adamw_reduce("adamw_replicated", rep_parts, _pack(wts), _pack(mom), _pack(var))
    unpacked = [_unpack(p, shapes) for p in packed]
    for k in REPLICATED:
        res[k] = [u[k] for u in unpacked]

    total = lax.psum(loss[0, 0], ("x", "y", "c"))
    out = [total, grad_x.reshape(x.shape)]
    for j in range(4):
        out += [res[k][j] for k in WEIGHTS]
    return tuple(out)
```

```python
import functools

import jax
import jax.numpy as jnp
import numpy as np
from jax import lax
from jax.experimental import pallas as pl
from jax.experimental.pallas import tpu as pltpu

F32 = jnp.float32
MXU_DTYPE = jnp.bfloat16

D = 1024
SEQ = 2048
GLA_H, GLA_DK, GLA_DV, GLA_CHUNK = 4, 64, 128, 64
GLA_RANK = 16
GLA_LOGIT_NORM = 16.0
RW_H, RW_N = 8, 64
RW_W = 512
D_FF = 2752
NORM_EPS = 1e-6
HEAD_NORM_EPS = 1e-5
RW_GN_EPS = RW_N * 1e-5
N_DEV = 8
ADAM_LR, ADAM_B1, ADAM_B2, ADAM_EPS, ADAM_WD, ADAM_STEP = 0.001, 0.9, 0.999, 1e-08, 0.01, 10

C_GA, C_GB, C_Q, C_K, C_V, C_OG = 0, 1024, 2048, 2304, 2560, 3072
C_RW = 3584
C_R, C_RK, C_RV, C_WLAL, C_GL = 3584, 4096, 4608, 5120, 5248
C_AFAB = 5376
NP = 5632
RW_PW = 1792
FFP = 2816
LANE = 128
VMEM_LIMIT = 56 * 1024 * 1024


def _cparams(sem):
    return pltpu.CompilerParams(dimension_semantics=sem, vmem_limit_bytes=VMEM_LIMIT)


@jax.custom_vjp
def mm(a, b):
    return jnp.dot(a.astype(MXU_DTYPE), b.astype(MXU_DTYPE), preferred_element_type=F32)


def _mm_fwd(a, b):
    return mm(a, b), (a, b)


def _mm_bwd(res, g):
    a, b = res
    gb = g.astype(MXU_DTYPE)
    da = lax.dot_general(gb, b.astype(MXU_DTYPE), (((1,), (1,)), ((), ())), preferred_element_type=F32)
    db = lax.dot_general(a.astype(MXU_DTYPE), gb, (((0,), (0,)), ((), ())), preferred_element_type=F32)
    return da.astype(a.dtype), db.astype(b.dtype)


mm.defvjp(_mm_fwd, _mm_bwd)


@jax.custom_vjp
def mm_nt(a, b):
    return lax.dot_general(a.astype(MXU_DTYPE), b.astype(MXU_DTYPE), (((1,), (1,)), ((), ())), preferred_element_type=F32)


def _mm_nt_fwd(a, b):
    return mm_nt(a, b), (a, b)


def _mm_nt_bwd(res, g):
    a, b = res
    gb = g.astype(MXU_DTYPE)
    da = jnp.dot(gb, b.astype(MXU_DTYPE), preferred_element_type=F32)
    db = lax.dot_general(gb, a.astype(MXU_DTYPE), (((0,), (0,)), ((), ())), preferred_element_type=F32)
    return da.astype(a.dtype), db.astype(b.dtype)


mm_nt.defvjp(_mm_nt_fwd, _mm_nt_bwd)


@jax.custom_vjp
def mm_tn(a, b):
    return lax.dot_general(a.astype(MXU_DTYPE), b.astype(MXU_DTYPE), (((0,), (0,)), ((), ())), preferred_element_type=F32)


def _mm_tn_fwd(a, b):
    return mm_tn(a, b), (a, b)


def _mm_tn_bwd(res, g):
    a, b = res
    gb = g.astype(MXU_DTYPE)
    da = lax.dot_general(b.astype(MXU_DTYPE), gb, (((1,), (1,)), ((), ())), preferred_element_type=F32)
    db = jnp.dot(a.astype(MXU_DTYPE), gb, preferred_element_type=F32)
    return da.astype(a.dtype), db.astype(b.dtype)


mm_tn.defvjp(_mm_tn_fwd, _mm_tn_bwd)


@functools.partial(jax.custom_vjp, nondiff_argnums=(2, 3))
def sel_dot(x, s, dims, x_first):
    sb = s.astype(MXU_DTYPE)
    hi = x.astype(MXU_DTYPE)
    r1 = x - hi.astype(F32)
    mid = r1.astype(MXU_DTYPE)
    lo = (r1 - mid.astype(F32)).astype(MXU_DTYPE)
    out = None
    for part in (hi, mid, lo):
        ops = (part, sb) if x_first else (sb, part)
        d = lax.dot_general(*ops, (dims, ((), ())), preferred_element_type=F32)
        out = d if out is None else out + d
    return out


def _sel_dot_fwd(x, s, dims, x_first):
    return sel_dot(x, s, dims, x_first), s


def _sel_dot_bwd(dims, x_first, s, g):
    if x_first:
        (cx,), (cs,) = dims
        dx = sel_dot(g, s, ((1,), (1 - cs,)), True) if cx == 1 else sel_dot(g, s, ((1 - cs,), (1,)), False)
    else:
        (cs,), (cx,) = dims
        dx = sel_dot(g, s, ((1 - cs,), (0,)), False) if cx == 0 else sel_dot(g, s, ((0,), (1 - cs,)), True)
    return dx, jnp.zeros_like(s)


sel_dot.defvjp(_sel_dot_fwd, _sel_dot_bwd)


def mm_exact(a, b, b_is_01=True):
    return sel_dot(a, b, ((1,), (0,)), True) if b_is_01 else sel_dot(b, a, ((1,), (0,)), False)


def mm_tn_exact(a, b):
    return sel_dot(a, b, ((0,), (0,)), True)


def _softplus(x):
    return jnp.maximum(x, 0.0) + jnp.log(1.0 + jnp.exp(-jnp.abs(x)))


def _sigmoid(x):
    return jax.nn.sigmoid(x)


def _silu(x):
    return x * _sigmoid(x)


def _rmsnorm(x, g):
    return x * lax.rsqrt(jnp.mean(x * x, axis=-1, keepdims=True) + NORM_EPS) * g


def _segment_sum(x, seg):
    width = x.shape[1]
    i = lax.broadcasted_iota(jnp.int32, (width, width), 0) // seg
    j = lax.broadcasted_iota(jnp.int32, (width, width), 1) // seg
    return mm_exact(x, (i == j).astype(F32))


def _row_spec(tm, width, cb):
    return pl.BlockSpec((tm, width), lambda i: (i, cb))


def _full_spec(shape):
    nd = len(shape)
    return pl.BlockSpec(tuple(shape), lambda i: (0,) * nd)


def rowwise_fwd(name, f, rows, params, outs, tm):
    n = rows[0][0].shape[0]
    nr, npar = len(rows), len(params)

    def body(*refs):
        rv = [r[...] for r in refs[:nr]]
        pv = [r[...] for r in refs[nr:nr + npar]]
        res = f(rv, pv)
        for o_ref, val in zip(refs[nr + npar:], res):
            o_ref[...] = val.astype(o_ref.dtype)

    return pl.pallas_call(
        body, name=name, grid=(n // tm,),
        in_specs=[_row_spec(tm, w, cb) for _, w, cb in rows] + [_full_spec(p.shape) for p in params],
        out_specs=[_row_spec(tm, w, 0) for w, _ in outs],
        out_shape=[jax.ShapeDtypeStruct((n, w), dt) for w, dt in outs],
        compiler_params=_cparams(("arbitrary",)),
    )(*[a for a, _, _ in rows], *params)


def rowwise_bwd(name, f, rows, params, douts, tm, adds=(), grad_rows=None):
    n = rows[0][0].shape[0]
    nr, npar = len(rows), len(params)
    grad_rows = list(range(nr)) if grad_rows is None else list(grad_rows)
    flat_d = [d for group in douts for d in group]
    nd, na, ng = len(flat_d), len(adds), len(grad_rows)

    def body(*refs):
        rv = [r[...] for r in refs[:nr]]
        pv = [r[...] for r in refs[nr:nr + npar]]
        dflat = [r[...].astype(F32) for r in refs[nr + npar:nr + npar + nd]]
        av = [r[...] for r in refs[nr + npar + nd:nr + npar + nd + na]]
        o = nr + npar + nd + na
        drow_refs, dpar_refs = refs[o:o + ng], refs[o + ng:o + ng + npar]
        dv, pos = [], 0
        for group in douts:
            dv.append(sum(dflat[pos + 1:pos + len(group)], dflat[pos]))
            pos += len(group)

        @pl.when(pl.program_id(0) == 0)
        def _():
            for r in dpar_refs:
                r[...] = jnp.zeros_like(r)

        def g(grows, pars):
            full = list(rv)
            for i, val in zip(grad_rows, grows):
                full[i] = val
            return f(full, pars)

        res, vjp = jax.vjp(g, [rv[i] for i in grad_rows], pv)
        drows, dpars = vjp([d.astype(r.dtype) for d, r in zip(dv, res)])
        drows = [d.astype(F32) for d in drows]
        for (idx, _), a in zip(adds, av):
            drows[idx] = drows[idx] + a.astype(F32)
        for r, d in zip(drow_refs, drows):
            r[...] = d
        for r, d in zip(dpar_refs, dpars):
            r[...] += d.astype(F32)

    res = pl.pallas_call(
        body, name=name, grid=(n // tm,),
        in_specs=[_row_spec(tm, w, cb) for _, w, cb in rows] + [_full_spec(p.shape) for p in params]
        + [_row_spec(tm, w, cb) for _, w, cb in flat_d] + [_row_spec(tm, w, cb) for _, (_, w, cb) in adds],
        out_specs=[_row_spec(tm, rows[i][1], 0) for i in grad_rows] + [_full_spec(p.shape) for p in params],
        out_shape=[jax.ShapeDtypeStruct((n, rows[i][1]), F32) for i in grad_rows]
        + [jax.ShapeDtypeStruct(p.shape, F32) for p in params],
        compiler_params=_cparams(("arbitrary",)),
    )(*[a for a, _, _ in rows], *params, *[a for a, _, _ in flat_d], *[a for _, (a, _, _) in adds])
    return res[:ng], res[ng:]


def matmul(name, a, b, mode, out_dtype, tm, tn, tk, comm=()):
    nc = len(comm)
    flags = [sc for _, sc in comm]
    if mode == "nn":
        (m, k), n = a.shape, b.shape[1]
        a_spec = pl.BlockSpec((tm, tk), lambda i, j, kk: (i, kk))
        b_spec = pl.BlockSpec((tk, tn), lambda i, j, kk: (kk, j))
        dims = (((1,), (0,)), ((), ()))
    elif mode == "nt":
        (m, k), n = a.shape, b.shape[0]
        a_spec = pl.BlockSpec((tm, tk), lambda i, j, kk: (i, kk))
        b_spec = pl.BlockSpec((tn, tk), lambda i, j, kk: (j, kk))
        dims = (((1,), (1,)), ((), ()))
    else:
        (k, m), n = a.shape, b.shape[1]
        a_spec = pl.BlockSpec((tk, tm), lambda i, j, kk: (kk, i))
        b_spec = pl.BlockSpec((tk, tn), lambda i, j, kk: (kk, j))
        dims = (((0,), (0,)), ((), ()))
    assert m % tm == 0 and n % tn == 0 and k % tk == 0, (name, a.shape, b.shape, tm, tn, tk)
    nk = k // tk
    grid = (m // tm, n // tn, nk)

    def body(*refs):
        a_ref, b_ref, c_in, o_ref = refs[0], refs[1], refs[2:2 + nc], refs[2 + nc]
        c_out, acc_ref, sems = refs[3 + nc:3 + 2 * nc], refs[3 + 2 * nc], refs[4 + 2 * nc:]
        kk = pl.program_id(2)
        step = (pl.program_id(0) * grid[1] + pl.program_id(1)) * nk + kk
        if nc:
            start, wait = _exchange_plan(flags, c_in, c_out, *sems)

            @pl.when(step == 0)
            def _():
                start()

        part = lax.dot_general(a_ref[...].astype(MXU_DTYPE), b_ref[...].astype(MXU_DTYPE), dims, preferred_element_type=F32)
        if nk == 1:
            o_ref[...] = part.astype(o_ref.dtype)
        else:
            @pl.when(kk == 0)
            def _():
                acc_ref[...] = part

            @pl.when((kk > 0) & (kk < nk - 1))
            def _():
                acc_ref[...] += part

            @pl.when(kk == nk - 1)
            def _():
                o_ref[...] = (acc_ref[...] + part).astype(o_ref.dtype)

        if nc:
            @pl.when(step == grid[0] * grid[1] * nk - 1)
            def _():
                wait()

    c_args, c_specs, c_shapes, c_sems = _comm_specs(comm)
    res = pl.pallas_call(
        body, name=name, grid=grid,
        in_specs=[a_spec, b_spec] + c_specs,
        out_specs=[pl.BlockSpec((tm, tn), lambda i, j, kk: (i, j))] + c_specs,
        out_shape=[jax.ShapeDtypeStruct((m, n), out_dtype)] + c_shapes,
        scratch_shapes=[pltpu.VMEM((tm, tn) if nk > 1 else (8, LANE), F32)] + c_sems,
        compiler_params=_cparams(("arbitrary", "arbitrary", "arbitrary")),
    )(a, b, *c_args)
    return res if nc else res[0]


def _prev(u, first):
    return jnp.where(first, 0.0, pltpu.roll(u, 1, 0))


def _next(u, last):
    return jnp.where(last, 0.0, pltpu.roll(u, u.shape[0] - 1, 0))


def _edge_masks(t, w):
    row = lax.broadcasted_iota(jnp.int32, (t, w), 0)
    return row == 0, row == t - 1


WIN, HALO = 128, 8
MID = slice(HALO, HALO + WIN)


def _window(ref, i, t):
    r0 = pl.multiple_of(i * WIN, WIN)
    before = ref[pl.ds(pl.multiple_of(jnp.maximum(r0 - HALO, 0), HALO), HALO), :]
    after = ref[pl.ds(pl.multiple_of(jnp.minimum(r0 + WIN, t - HALO), HALO), HALO), :]
    before = jnp.where(i == 0, 0.0, before.astype(F32))
    after = jnp.where(i == t // WIN - 1, 0.0, after.astype(F32))
    return jnp.concatenate([before, ref[pl.ds(r0, WIN), :].astype(F32), after], axis=0)


def _wprev(u):
    return pltpu.roll(u, 1, 0)


def _wnext(u):
    return pltpu.roll(u, u.shape[0] - 1, 0)


def _mid_rows(i):
    return pl.ds(pl.multiple_of(i * WIN, WIN), WIN)


def _colsum(x):
    return jnp.sum(x[MID], axis=0, keepdims=True)


SHIFT_CW = 256


def shift_fwd(p, mu_prev, mu_next, nb, t):
    cw, c0 = SHIFT_CW, C_RW // SHIFT_CW

    def body(p_ref, mp_ref, mn_ref, s_ref):
        x = p_ref[...]
        first, last = _edge_masks(t, cw)
        s_ref[...] = x + mp_ref[...] * (_prev(x, first) - x) + mn_ref[...] * (_next(x, last) - x)

    return pl.pallas_call(
        body, name="rwkv_shift_fwd", grid=(nb, RW_PW // cw),
        in_specs=[pl.BlockSpec((t, cw), lambda b, j: (b, c0 + j)), pl.BlockSpec((1, cw), lambda b, j: (0, j)),
                  pl.BlockSpec((1, cw), lambda b, j: (0, j))],
        out_specs=pl.BlockSpec((t, cw), lambda b, j: (b, j)),
        out_shape=jax.ShapeDtypeStruct((nb * t, RW_PW), F32),
        compiler_params=_cparams(("arbitrary", "arbitrary")),
    )(p, mu_prev, mu_next)


def shift_bwd(p, ds, mu_prev, mu_next, nb, t):
    cw, c0 = SHIFT_CW, C_RW // SHIFT_CW

    def body(p_ref, ds_ref, mp_ref, mn_ref, dp_ref, dmp_ref, dmn_ref):
        @pl.when(pl.program_id(1) == 0)
        def _():
            dmp_ref[...] = jnp.zeros_like(dmp_ref)
            dmn_ref[...] = jnp.zeros_like(dmn_ref)

        mp, mn = mp_ref[...], mn_ref[...]

        def step(i, carry):
            dmp, dmn = carry
            x, g = _window(p_ref, i, t), _window(ds_ref, i, t)
            dp = g * (1.0 - mp - mn) + _wnext(mp * g) + _wprev(mn * g)
            dp_ref[_mid_rows(i), :] = dp[MID]
            return dmp + _colsum(g * (_wprev(x) - x)), dmn + _colsum(g * (_wnext(x) - x))

        zero = jnp.zeros((1, cw), F32)
        dmp, dmn = lax.fori_loop(0, t // WIN, step, (zero, zero))
        dmp_ref[...] += dmp
        dmn_ref[...] += dmn

    return pl.pallas_call(
        body, name="rwkv_shift_bwd", grid=(RW_PW // cw, nb),
        in_specs=[pl.BlockSpec((t, cw), lambda j, b: (b, c0 + j)), pl.BlockSpec((t, cw), lambda j, b: (b, j)),
                  pl.BlockSpec((1, cw), lambda j, b: (0, j)), pl.BlockSpec((1, cw), lambda j, b: (0, j))],
        out_specs=[pl.BlockSpec((t, cw), lambda j, b: (b, j)), pl.BlockSpec((1, cw), lambda j, b: (0, j)),
                   pl.BlockSpec((1, cw), lambda j, b: (0, j))],
        out_shape=[jax.ShapeDtypeStruct((nb * t, RW_PW), F32), jax.ShapeDtypeStruct((1, RW_PW), F32),
                   jax.ShapeDtypeStruct((1, RW_PW), F32)],
        compiler_params=_cparams(("arbitrary", "arbitrary")),
    )(p, ds, mu_prev, mu_next)


def conv_glu_fwd(u, cw, cb, nb, t):
    def body(u_ref, w_ref, b_ref, z_ref):
        x, w = u_ref[...], w_ref[...]
        first, last = _edge_masks(t, 2 * LANE)
        c = w[0:1] * _prev(x, first) + w[1:2] * x + w[2:3] * _next(x, last) + b_ref[...]
        z_ref[...] = (_silu(c[:, :LANE]) * c[:, LANE:]).astype(z_ref.dtype)

    return pl.pallas_call(
        body, name="conv_glu_fwd", grid=(nb, FFP // LANE),
        in_specs=[pl.BlockSpec((t, 2 * LANE), lambda b, j: (b, j)), pl.BlockSpec((3, 2 * LANE), lambda b, j: (0, j)),
                  pl.BlockSpec((1, 2 * LANE), lambda b, j: (0, j))],
        out_specs=pl.BlockSpec((t, LANE), lambda b, j: (b, j)),
        out_shape=jax.ShapeDtypeStruct((nb * t, FFP), MXU_DTYPE),
        compiler_params=_cparams(("arbitrary", "arbitrary")),
    )(u, cw, cb)


def conv_glu_bwd(u, dz, cw, cb, nb, t):
    def body(u_ref, dz_ref, w_ref, b_ref, du_ref, dw_ref, db_ref):
        @pl.when(pl.program_id(1) == 0)
        def _():
            dw_ref[...] = jnp.zeros_like(dw_ref)
            db_ref[...] = jnp.zeros_like(db_ref)

        w, bias = w_ref[...], b_ref[...]

        def step(i, carry):
            x, g = _window(u_ref, i, t), _window(dz_ref, i, t)
            xp, xn = _wprev(x), _wnext(x)
            c = w[0:1] * xp + w[1:2] * x + w[2:3] * xn + bias
            cg, cv = c[:, :LANE], c[:, LANE:]
            sg = _sigmoid(cg)
            dcg = g * cv * (sg * (1.0 + cg * (1.0 - sg)))
            dcv = g * (cg * sg)
            dc = jnp.concatenate([dcg, dcv], axis=1)
            du = w[1:2] * dc + _wnext(w[0:1] * dc) + _wprev(w[2:3] * dc)
            du_ref[_mid_rows(i), :] = du[MID].astype(du_ref.dtype)
            return tuple(acc + _colsum(val) for acc, val in zip(carry, (dc * xp, dc * x, dc * xn, dc)))

        zero = jnp.zeros((1, 2 * LANE), F32)
        sums = lax.fori_loop(0, t // WIN, step, (zero, zero, zero, zero))
        for row in range(3):
            dw_ref[row:row + 1, :] += sums[row]
        db_ref[...] += sums[3]

    return pl.pallas_call(
        body, name="conv_glu_bwd", grid=(FFP // LANE, nb),
        in_specs=[pl.BlockSpec((t, 2 * LANE), lambda j, b: (b, j)), pl.BlockSpec((t, LANE), lambda j, b: (b, j)),
                  pl.BlockSpec((3, 2 * LANE), lambda j, b: (0, j)), pl.BlockSpec((1, 2 * LANE), lambda j, b: (0, j))],
        out_specs=[pl.BlockSpec((t, 2 * LANE), lambda j, b: (b, j)), pl.BlockSpec((3, 2 * LANE), lambda j, b: (0, j)),
                   pl.BlockSpec((1, 2 * LANE), lambda j, b: (0, j))],
        out_shape=[jax.ShapeDtypeStruct((nb * t, 2 * FFP), MXU_DTYPE), jax.ShapeDtypeStruct((3, 2 * FFP), F32),
                   jax.ShapeDtypeStruct((1, 2 * FFP), F32)],
        compiler_params=_cparams(("arbitrary", "arbitrary")),
    )(u, dz, cw, cb)


def _gla_chunk(q, k, v, afab, wa2p, ba, s_in, reverse, sb):
    c = GLA_CHUNK
    r = sb * c
    ri = lax.broadcasted_iota(jnp.int32, (r, r), 0)
    ci = lax.broadcasted_iota(jnp.int32, (r, r), 1)
    same = (ri // c) == (ci // c)
    keep = same & ((ci >= ri) if reverse else (ci <= ri))
    i_ref = (c - 1 - c // 2) if reverse else (c // 2)
    pick_ref = (ci == (ri // c) * c + i_ref).astype(F32)
    seq_cols = (lax.broadcasted_iota(jnp.int32, (r, sb * LANE), 0) // c) == (lax.broadcasted_iota(jnp.int32, (r, sb * LANE), 1) // LANE)
    expand = lambda x: jnp.where(seq_cols, jnp.concatenate([x] * sb, axis=1), 0.0)
    lane = lax.broadcasted_iota(jnp.int32, (1, LANE), 1)
    outs, states = [None] * GLA_H, [None] * GLA_H
    for pr in range(GLA_H // 2):
        la = -_softplus(-(mm(afab, wa2p[pr]) + ba[pr])) * (1.0 / GLA_LOGIT_NORM)
        b = mm_exact(keep.astype(F32), la, b_is_01=False)
        b_ref = mm_exact(pick_ref, b, b_is_01=False)
        b_last = mm_exact(same.astype(F32), la, b_is_01=False)
        qs = q[pr] * (GLA_DK ** -0.5)
        qi = qs * jnp.exp(b - b_ref)
        ki = k[pr] * jnp.exp(b_ref - b)
        kd = k[pr] * jnp.exp(b_last - b)
        qb = qs * jnp.exp(b)
        dec = jnp.exp(mm_tn_exact(expand(la), jnp.ones((r, LANE), F32)))
        for h in (2 * pr, 2 * pr + 1):
            m = ((lane // GLA_DK) == (h % 2)).astype(F32)
            a = jnp.where(keep, mm_nt(qi * m, ki), 0.0)
            o_intra = mm(a, v[h])
            kv = mm_tn(expand(kd * m), v[h])
            o_inter = mm(expand(qb * m), s_in[h])
            outs[h] = o_intra + o_inter
            states[h] = s_in[h] * dec + kv
    return outs, states


def _gla_load(q_ref, k_ref, v_ref, af_ref, w_ref, ba_ref, sb, rows):
    stack = lambda ref, c0: jnp.concatenate([ref[s, rows, c0:c0 + LANE] for s in range(sb)], axis=0)
    q = [stack(q_ref, pr * LANE) for pr in range(GLA_H // 2)]
    k = [stack(k_ref, pr * LANE) for pr in range(GLA_H // 2)]
    v = [stack(v_ref, h * GLA_DV) for h in range(GLA_H)]
    w = [w_ref[:, pr * LANE:(pr + 1) * LANE] for pr in range(GLA_H // 2)]
    ba = [ba_ref[:, pr * LANE:(pr + 1) * LANE] for pr in range(GLA_H // 2)]
    return q, k, v, stack(af_ref, 0), w, ba


GLA_TILE = 256
GLA_SB = 4


def _gla_specs(nb, t, reverse):
    tile = min(GLA_TILE, t)
    nt = t // tile
    sb = GLA_SB if nb % GLA_SB == 0 else 1
    return tile, tile // GLA_CHUNK, nt, sb, ((lambda j: nt - 1 - j) if reverse else (lambda j: j))


def gla_fwd(p, wa2p, ba, o_add, nb, t, reverse):
    tile, cpt, nt, sb, tj = _gla_specs(nb, t, reverse)
    has_add = o_add is not None

    def body(*refs):
        if has_add:
            q_ref, k_ref, v_ref, af_ref, w_ref, ba_ref, add_ref, o_ref, hist_ref, s_ref = refs
        else:
            q_ref, k_ref, v_ref, af_ref, w_ref, ba_ref, o_ref, hist_ref, s_ref = refs

        @pl.when(pl.program_id(1) == 0)
        def _():
            s_ref[...] = jnp.zeros_like(s_ref)

        def step(i, carry):
            ci = (cpt - 1 - i) if reverse else i
            rows = pl.ds(pl.multiple_of(ci * GLA_CHUNK, GLA_CHUNK), GLA_CHUNK)
            s_in = [s_ref[h] for h in range(GLA_H)]
            for h in range(GLA_H):
                for s in range(sb):
                    hist_ref[s, ci, h] = s_in[h][s * LANE:(s + 1) * LANE]
            q, k, v, af, w, ba = _gla_load(q_ref, k_ref, v_ref, af_ref, w_ref, ba_ref, sb, rows)
            outs, states = _gla_chunk(q, k, v, af, w, ba, s_in, reverse, sb)
            for h in range(GLA_H):
                for s in range(sb):
                    oh = outs[h][s * GLA_CHUNK:(s + 1) * GLA_CHUNK]
                    if has_add:
                        oh = oh + add_ref[s, rows, h * GLA_DV:(h + 1) * GLA_DV]
                    o_ref[s, rows, h * GLA_DV:(h + 1) * GLA_DV] = oh
                s_ref[h] = states[h]
            return carry

        lax.fori_loop(0, cpt, step, 0)

    col = lambda width, c0: pl.BlockSpec((sb, tile, width), lambda b, j: (b, tj(j), c0 // width))
    in_specs = [col(256, C_Q), col(256, C_K), col(512, C_V), col(LANE, C_AFAB),
                pl.BlockSpec((LANE, 256), lambda b, j: (0, 0)), pl.BlockSpec((1, 256), lambda b, j: (0, 0))]
    p3 = p.reshape(nb, t, p.shape[1])
    args = [p3, p3, p3, p3, wa2p, ba]
    if has_add:
        in_specs.append(col(512, 0))
        args.append(o_add.reshape(nb, t, 512))
    o, hist = pl.pallas_call(
        body, name="gla_fwd_rev" if reverse else "gla_fwd", grid=(nb // sb, nt),
        in_specs=in_specs,
        out_specs=[col(512, 0), pl.BlockSpec((sb, cpt, GLA_H, LANE, LANE), lambda b, j: (b, tj(j), 0, 0, 0))],
        out_shape=[jax.ShapeDtypeStruct((nb, t, 512), F32),
                   jax.ShapeDtypeStruct((nb, t // GLA_CHUNK, GLA_H, LANE, LANE), F32)],
        scratch_shapes=[pltpu.VMEM((GLA_H, sb * LANE, LANE), F32)],
        compiler_params=_cparams(("arbitrary", "arbitrary")),
    )(*args)
    return o.reshape(nb * t, 512), hist


def gla_bwd(p, wa2p, ba, hist, do, dprev, nb, t, reverse):
    tile, cpt, nt, sb, tj_f = _gla_specs(nb, t, reverse)
    tj = lambda j: tj_f(nt - 1 - j)
    has_prev = dprev is not None

    def body(*refs):
        if has_prev:
            q_ref, k_ref, v_ref, af_ref, w_ref, ba_ref, hist_ref, do_ref, prev_ref, dqkv_ref, dw_ref, dba_ref, ds_ref = refs
        else:
            q_ref, k_ref, v_ref, af_ref, w_ref, ba_ref, hist_ref, do_ref, dqkv_ref, dw_ref, dba_ref, ds_ref = refs

        @pl.when((pl.program_id(0) == 0) & (pl.program_id(1) == 0))
        def _():
            dw_ref[...] = jnp.zeros_like(dw_ref)
            dba_ref[...] = jnp.zeros_like(dba_ref)

        @pl.when(pl.program_id(1) == 0)
        def _():
            ds_ref[...] = jnp.zeros_like(ds_ref)

        def step(i, carry):
            ci = i if reverse else (cpt - 1 - i)
            rows = pl.ds(pl.multiple_of(ci * GLA_CHUNK, GLA_CHUNK), GLA_CHUNK)
            fn = functools.partial(_gla_chunk, reverse=reverse, sb=sb)
            seqs = lambda get: jnp.concatenate([get(s) for s in range(sb)], axis=0)
            s_in = [seqs(lambda s: hist_ref[s, ci, h]) for h in range(GLA_H)]
            q, k, v, af, w, ba = _gla_load(q_ref, k_ref, v_ref, af_ref, w_ref, ba_ref, sb, rows)
            _, vjp = jax.vjp(fn, q, k, v, af, w, ba, s_in)
            d_o = [seqs(lambda s: do_ref[s, rows, h * GLA_DV:(h + 1) * GLA_DV]) for h in range(GLA_H)]
            d_s = [ds_ref[h] for h in range(GLA_H)]
            dq, dk, dv, daf, dw, dba, ds_in = vjp((d_o, d_s))
            pieces = [(pr * LANE, dq[pr]) for pr in range(2)] + [(256 + pr * LANE, dk[pr]) for pr in range(2)]
            pieces += [(512 + h * GLA_DV, dv[h]) for h in range(GLA_H)] + [(1024, daf)]
            for c0, val in pieces:
                for s in range(sb):
                    part = val[s * GLA_CHUNK:(s + 1) * GLA_CHUNK]
                    if has_prev:
                        part = part + prev_ref[s, rows, c0:c0 + LANE]
                    dqkv_ref[s, rows, c0:c0 + LANE] = part
            for pr in range(2):
                dw_ref[:, pr * LANE:(pr + 1) * LANE] += dw[pr]
                dba_ref[:, pr * LANE:(pr + 1) * LANE] += dba[pr]
            for h in range(GLA_H):
                ds_ref[h] = ds_in[h]
            return carry

        lax.fori_loop(0, cpt, step, 0)

    col = lambda width, c0: pl.BlockSpec((sb, tile, width), lambda b, j: (b, tj(j), c0 // width))
    in_specs = [col(256, C_Q), col(256, C_K), col(512, C_V), col(LANE, C_AFAB),
                pl.BlockSpec((LANE, 256), lambda b, j: (0, 0)), pl.BlockSpec((1, 256), lambda b, j: (0, 0)),
                pl.BlockSpec((sb, cpt, GLA_H, LANE, LANE), lambda b, j: (b, tj(j), 0, 0, 0)), col(512, 0)]
    p3 = p.reshape(nb, t, p.shape[1])
    args = [p3, p3, p3, p3, wa2p, ba, hist, do.reshape(nb, t, 512)]
    if has_prev:
        in_specs.append(col(1152, 0))
        args.append(dprev.reshape(nb, t, 1152))
    dqkv, dw, dba = pl.pallas_call(
        body, name="gla_bwd_rev" if reverse else "gla_bwd", grid=(nb // sb, nt),
        in_specs=in_specs,
        out_specs=[col(1152, 0), pl.BlockSpec((LANE, 256), lambda b, j: (0, 0)), pl.BlockSpec((1, 256), lambda b, j: (0, 0))],
        out_shape=[jax.ShapeDtypeStruct((nb, t, 1152), F32), jax.ShapeDtypeStruct((LANE, 256), F32),
                   jax.ShapeDtypeStruct((1, 256), F32)],
        scratch_shapes=[pltpu.VMEM((GLA_H, sb * LANE, LANE), F32)],
        compiler_params=_cparams(("arbitrary", "arbitrary")),
    )(*args)
    return dqkv.reshape(nb * t, 1152), dw, dba


SCAN_TB = 16
RW_VH = RW_N // 2


def _bwd_lanes():
    lane = lax.broadcasted_iota(jnp.int32, (1, LANE), 1)
    return ((lane // (LANE // 4)) % 2) == 1


def _comm_specs(comm):
    anyspec = pl.BlockSpec(memory_space=pl.ANY)
    n = len(comm)
    shapes = [jax.ShapeDtypeStruct((N_DEV,) + (a.shape[1:] if sc else a.shape), a.dtype) for a, sc in comm]
    sems = [pltpu.SemaphoreType.DMA((n, N_DEV - 1)), pltpu.SemaphoreType.DMA((n, N_DEV - 1)), pltpu.SemaphoreType.DMA((n,))] if n else []
    return [a for a, _ in comm], [anyspec] * n, shapes, sems


def rwkv_scan_fwd(r, w, k, a, b, v, comm=()):
    t = r.shape[0]
    nt = t // SCAN_TB
    nc = len(comm)
    flags = [sc for _, sc in comm]

    def body(*refs):
        (rf, rm, kf, km, af, am, bf, bm, wf_ref, wm_ref, vf, vm), refs = refs[:12], refs[12:]
        c_in, refs = refs[:nc], refs[nc:]
        (yf_ref, ym_ref, hist_ref, sa_ref), refs = refs[:4], refs[4:]
        c_out, refs = refs[:nc], refs[nc:]
        s_ref, sems = refs[0], refs[1:]
        i = pl.program_id(0)
        if nc:
            start, wait = _exchange_plan(flags, c_in, c_out, *sems)

        @pl.when(i == 0)
        def _():
            s_ref[...] = jnp.zeros_like(s_ref)
            if nc:
                start()

        bwd = _bwd_lanes()

        for tt in range(SCAN_TB):
            mt = SCAN_TB - 1 - tt
            pick = lambda f_ref, m_ref: jnp.where(bwd, m_ref[mt], f_ref[tt])
            rt, kt, at, bt, wt = pick(rf, rm), pick(kf, km), pick(af, am), pick(bf, bm), pick(wf_ref, wm_ref)
            for vi in range(RW_VH):
                sv = s_ref[vi] if tt == 0 else hist_ref[tt - 1, vi]
                sa = jnp.sum(sv * at, axis=0, keepdims=True)
                v_row = jnp.where(bwd, vm[mt, vi:vi + 1, :], vf[tt, vi:vi + 1, :])
                sn = sv * wt + sa * bt + v_row * kt
                hist_ref[tt, vi] = sn
                y_row = jnp.sum(sn * rt, axis=0, keepdims=True)
                yf_ref[tt, vi:vi + 1, :] = y_row
                ym_ref[mt, vi:vi + 1, :] = y_row
                sa_ref[tt, vi:vi + 1, :] = sa
        s_ref[...] = hist_ref[SCAN_TB - 1]

        if nc:
            @pl.when(i == nt - 1)
            def _():
                wait()

    fwd_map, mir_map = (lambda i: (i, 0, 0)), (lambda i: (nt - 1 - i, 0, 0))
    kf_spec, km_spec = pl.BlockSpec((SCAN_TB, RW_N, LANE), fwd_map), pl.BlockSpec((SCAN_TB, RW_N, LANE), mir_map)
    vf_spec, vm_spec = pl.BlockSpec((SCAN_TB, RW_VH, LANE), fwd_map), pl.BlockSpec((SCAN_TB, RW_VH, LANE), mir_map)
    c_args, c_specs, c_shapes, c_sems = _comm_specs(comm)
    vshape = jax.ShapeDtypeStruct((t, RW_VH, LANE), F32)
    return pl.pallas_call(
        body, name="rwkv_scan_fwd", grid=(nt,),
        in_specs=[kf_spec, km_spec] * 5 + [vf_spec, vm_spec] + c_specs,
        out_specs=[vf_spec, vm_spec, pl.BlockSpec((SCAN_TB, RW_VH, RW_N, LANE), lambda i: (i, 0, 0, 0)), vf_spec] + c_specs,
        out_shape=[vshape, vshape, jax.ShapeDtypeStruct((t, RW_VH, RW_N, LANE), F32), vshape] + c_shapes,
        scratch_shapes=[pltpu.VMEM((RW_VH, RW_N, LANE), F32)] + c_sems,
        compiler_params=_cparams(("arbitrary",)),
    )(r, r, k, k, a, a, b, b, w, w, v, v, *c_args)


def rwkv_scan_bwd(r, w, k, a, b, v, hist, sa, dy, comm=()):
    t = r.shape[0]
    nt = t // SCAN_TB
    nc = len(comm)
    flags = [sc for _, sc in comm]

    def body(*refs):
        (rf, rm, kf, km, af, am, bf, bm, wf_ref, wm_ref, vf, vm, hist_ref, prev_ref, sa_ref, dyf, dym), refs = refs[:17], refs[17:]
        c_in, refs = refs[:nc], refs[nc:]
        k_outs, (dvf_ref, dvm_ref), refs = refs[:4], refs[4:6], refs[6:]
        c_out, refs = refs[:nc], refs[nc:]
        ds_ref, sems = refs[0], refs[1:]
        i = pl.program_id(0)
        if nc:
            start, wait = _exchange_plan(flags, c_in, c_out, *sems)

        @pl.when(i == 0)
        def _():
            ds_ref[...] = jnp.zeros_like(ds_ref)
            if nc:
                start()

        bwd = _bwd_lanes()
        group = lax.broadcasted_iota(jnp.int32, (1, LANE), 1) // RW_Q
        first_block = i == nt - 1

        for tt in range(SCAN_TB - 1, -1, -1):
            mt = SCAN_TB - 1 - tt
            pick = lambda f_ref, m_ref: jnp.where(bwd, m_ref[mt], f_ref[tt])
            rt, kt, at, bt, wt = pick(rf, rm), pick(kf, km), pick(af, am), pick(bf, bm), pick(wf_ref, wm_ref)
            zero = jnp.zeros((RW_N, LANE), F32)
            dr, dw, dk, da, db = zero, zero, zero, zero, zero
            for vi in range(RW_VH):
                sn = hist_ref[tt, vi]
                sv = hist_ref[tt - 1, vi] if tt > 0 else jnp.where(first_block, 0.0, prev_ref[0, vi])
                sa_row = sa_ref[tt, vi:vi + 1, :]
                v_row = jnp.where(bwd, vm[mt, vi:vi + 1, :], vf[tt, vi:vi + 1, :])
                dy_row = jnp.where(bwd, dym[mt, vi:vi + 1, :], dyf[tt, vi:vi + 1, :])
                dsv = ds_ref[vi] + dy_row * rt
                dr = dr + sn * dy_row
                dsa = jnp.sum(dsv * bt, axis=0, keepdims=True)
                dw = dw + sv * dsv
                db = db + dsv * sa_row
                dk = dk + dsv * v_row
                dv_row = jnp.sum(dsv * kt, axis=0, keepdims=True)
                dvf_ref[tt, vi:vi + 1, :] = dv_row
                dvm_ref[mt, vi:vi + 1, :] = dv_row
                da = da + sv * dsa
                ds_ref[vi] = dsv * wt + dsa * at
            dr, dw, dk, da, db = [val + pltpu.roll(val, LANE // 2, 1) for val in (dr, dw, dk, da, db)]
            up, down = (lambda val: pltpu.roll(val, RW_Q, 1)), (lambda val: pltpu.roll(val, LANE - RW_Q, 1))
            packed_f = jnp.where(group == 0, dr, jnp.where(group == 1, up(dk), jnp.where(group == 2, da, up(db))))
            packed_m = jnp.where(group == 0, down(dr), jnp.where(group == 1, dk, jnp.where(group == 2, down(da), db)))
            k_outs[0][tt] = packed_f
            k_outs[1][mt] = packed_m
            k_outs[2][tt] = dw
            k_outs[3][mt] = dw

        if nc:
            @pl.when(i == nt - 1)
            def _():
                wait()

    fwd_map, mir_map = (lambda i: (nt - 1 - i, 0, 0)), (lambda i: (i, 0, 0))
    kf_spec, km_spec = pl.BlockSpec((SCAN_TB, RW_N, LANE), fwd_map), pl.BlockSpec((SCAN_TB, RW_N, LANE), mir_map)
    vf_spec, vm_spec = pl.BlockSpec((SCAN_TB, RW_VH, LANE), fwd_map), pl.BlockSpec((SCAN_TB, RW_VH, LANE), mir_map)
    prev_spec = pl.BlockSpec((1, RW_VH, RW_N, LANE), lambda i: (jnp.maximum((nt - 1 - i) * SCAN_TB - 1, 0), 0, 0, 0))
    c_args, c_specs, c_shapes, c_sems = _comm_specs(comm)
    kshape, vshape = jax.ShapeDtypeStruct((t, RW_N, LANE), F32), jax.ShapeDtypeStruct((t, RW_VH, LANE), F32)
    return pl.pallas_call(
        body, name="rwkv_scan_bwd", grid=(nt,),
        in_specs=[kf_spec, km_spec] * 5 + [vf_spec, vm_spec,
                                           pl.BlockSpec((SCAN_TB, RW_VH, RW_N, LANE), lambda i: (nt - 1 - i, 0, 0, 0)),
                                           prev_spec, vf_spec, vf_spec, vm_spec] + c_specs,
        out_specs=[kf_spec, km_spec] * 2 + [vf_spec, vm_spec] + c_specs,
        out_shape=[kshape] * 4 + [vshape] * 2 + c_shapes,
        scratch_shapes=[pltpu.VMEM((RW_VH, RW_N, LANE), F32)] + c_sems,
        compiler_params=_cparams(("arbitrary",)),
    )(r, r, k, k, a, a, b, b, w, w, v, v, hist, hist, sa, dy, dy, *c_args)


RELAYOUT_TB = 128
RW_Q = LANE // 4


def to_scan(name, x, cb, nb, t, value, x_bwd=None):
    tb = min(2 * RELAYOUT_TB, t)
    rows_out = RW_VH if value else RW_N
    ins = [x] if x_bwd is None else [x, x_bwd]

    def body(*refs):
        x_refs, o_ref, scrs = refs[:len(ins)], refs[len(ins)], refs[len(ins) + 1:]
        for x_ref, scr in zip(x_refs, scrs):
            for b in range(nb):
                scr[b * RW_H:(b + 1) * RW_H] = x_ref[b].T.reshape(RW_H, RW_N, tb)
        for j in range(rows_out):
            lo = scrs[0][:, j, :]
            if value:
                hi = scrs[0][:, j + RW_VH, :]
                blk = [lo, lo, hi, hi]
            else:
                other = lo if x_bwd is None else scrs[1][:, j, :]
                blk = [lo, other, lo, other]
            o_ref[:, j, :] = jnp.concatenate(blk, axis=0).T

    return pl.pallas_call(
        body, name=name, grid=(t // tb,),
        in_specs=[pl.BlockSpec((nb, tb, RW_W), lambda i: (0, i, cb))] + [pl.BlockSpec((nb, tb, RW_W), lambda i: (0, i, 0))] * (len(ins) - 1),
        out_specs=pl.BlockSpec((tb, rows_out, LANE), lambda i: (i, 0, 0)),
        out_shape=jax.ShapeDtypeStruct((t, rows_out, LANE), F32),
        scratch_shapes=[pltpu.VMEM((nb * RW_H, RW_N, tb), F32)] * len(ins),
        compiler_params=_cparams(("arbitrary",)),
    )(*[a.reshape(nb, t, a.shape[1]) for a in ins])


def from_scan(name, xf, xm, nb, t, value, groups=None):
    tb = min(RELAYOUT_TB, t)
    rows_in = RW_VH if value else RW_N
    n_out = 1 if value else (4 if groups is None else 2)
    grp = lambda a, g: a[g * RW_Q:(g + 1) * RW_Q]

    def body(f_ref, m_ref, *rest):
        outs, scrs = rest[:n_out], rest[n_out:]
        lane_group = lax.broadcasted_iota(jnp.int32, (1, LANE), 1) // RW_Q
        for j in range(rows_in):
            f, m = f_ref[:, j, :], m_ref[:, j, :]
            if value:
                c = jnp.where(_bwd_lanes(), m, f).T
                scrs[0][:, j, :] = grp(c, 0) + grp(c, 1)
                scrs[0][:, j + RW_VH, :] = grp(c, 2) + grp(c, 3)
            elif groups is None:
                c = (f + m).T
                for q, scr in enumerate(scrs):
                    scr[:, j, :] = grp(c, q)
            else:
                c = jnp.where(lane_group == groups[1], m, f).T
                scrs[0][:, j, :] = grp(c, groups[0])
                scrs[1][:, j, :] = grp(c, groups[1])
        for o_ref, scr in zip(outs, scrs):
            for b in range(nb):
                o_ref[b] = scr[b * RW_H:(b + 1) * RW_H].reshape(RW_W, tb).T

    res = pl.pallas_call(
        body, name=name, grid=(t // tb,),
        in_specs=[pl.BlockSpec((tb, rows_in, LANE), lambda i: (i, 0, 0))] * 2,
        out_specs=[pl.BlockSpec((nb, tb, RW_W), lambda i: (0, i, 0))] * n_out,
        out_shape=[jax.ShapeDtypeStruct((nb, t, RW_W), F32)] * n_out,
        scratch_shapes=[pltpu.VMEM((nb * RW_H, RW_N, tb), F32)] * n_out,
        compiler_params=_cparams(("arbitrary",)),
    )(xf, xm)
    return [r.reshape(nb * t, RW_W) for r in res]


def f_norm(rows, params):
    (x,), (g,) = rows, params
    return [_rmsnorm(x, g)]


def f_rwkv_pre(rows, params):
    k, wlal, gl = rows
    w0f, w2f, w0b, w2b, a0, a2, g2, k_k, k_a = params
    tw = jnp.tanh(wlal)

    def decay(w0, w2):
        return jnp.exp(-jnp.exp(-_softplus(-(w0 + mm(tw, w2))) - 0.5))

    lr = _sigmoid(a0 + mm(wlal, a2))
    gate = mm(_sigmoid(gl), g2)
    kk = k * k_k
    kk = kk / jnp.maximum(jnp.sqrt(_segment_sum(kk * kk, RW_N)), 1e-12)
    kp = k * (1.0 + (lr - 1.0) * k_a)
    return [decay(w0f, w2f), decay(w0b, w2b), kp, -kk, kk * lr, gate]


def f_branch_post(rows, params):
    o, og, y, r, kp, v, g = rows
    gla_g, ln_w, ln_b, r_k = params
    on = o * lax.rsqrt(_segment_sum(o * o, GLA_DV) * (1.0 / GLA_DV) + HEAD_NORM_EPS)
    oa = on * gla_g * _silu(og)
    mu = _segment_sum(y, RW_N) * (1.0 / RW_N)
    yc = y - mu
    var = _segment_sum(yc * yc, RW_N) * (1.0 / RW_N)
    yn = yc * lax.rsqrt(var + RW_GN_EPS) * ln_w + ln_b
    bonus = _segment_sum(r * kp * r_k, RW_N) * v
    return [oa, (yn + bonus) * g]


def f_merge(rows, params):
    ga, gb, ya, yb = rows
    return [_sigmoid(ga) * ya + _sigmoid(gb) * yb]


def f_norm2(rows, params):
    (x, mo), (g,) = rows, params
    x1 = x + mo
    return [x1, _rmsnorm(x1, g)]


def loss_head(x1, ffo, tgt, gf, tm):
    n = x1.shape[0]

    def body(x1_ref, f_ref, t_ref, g_ref, loss_ref, dx_ref, dg_ref):
        @pl.when(pl.program_id(0) == 0)
        def _():
            loss_ref[...] = jnp.zeros_like(loss_ref)
            dg_ref[...] = jnp.zeros_like(dg_ref)

        tgt_v = t_ref[...]

        def f(x2, g):
            err = _rmsnorm(x2, g) - tgt_v
            return jnp.sum(jnp.sum(err * err, axis=-1, keepdims=True), axis=0, keepdims=True) * (0.5 / D)

        val, vjp = jax.vjp(f, x1_ref[...] + f_ref[...], g_ref[...])
        dx, dg = vjp(jnp.ones((1, 1), F32))
        loss_ref[...] += val
        dx_ref[...] = dx
        dg_ref[...] += dg

    return pl.pallas_call(
        body, name="loss_head", grid=(n // tm,),
        in_specs=[_row_spec(tm, D, 0)] * 3 + [_full_spec((1, D))],
        out_specs=[_full_spec((1, 1)), _row_spec(tm, D, 0), _full_spec((1, D))],
        out_shape=[jax.ShapeDtypeStruct((1, 1), F32), jax.ShapeDtypeStruct((n, D), F32), jax.ShapeDtypeStruct((1, D), F32)],
        compiler_params=_cparams(("arbitrary",)),
    )(x1, ffo, tgt, gf)


def _pad_cols(a, width):
    return jnp.pad(a, ((0, 0), (0, width - a.shape[1])))


def w_in_to_padded(w):
    return _pad_cols(jnp.concatenate([w[:, 3360:5408], w[:, 0:1536], w[:, 1568:3360], w[:, 1536:1568]], axis=1), NP)


def w_in_from_padded(wp):
    return jnp.concatenate([wp[:, 2048:3584], wp[:, 5376:5408], wp[:, 3584:5376], wp[:, 0:2048]], axis=1)


def ff_interleave(a):
    r = a.shape[0]
    halves = jnp.stack([_pad_cols(a[:, :D_FF], FFP), _pad_cols(a[:, D_FF:], FFP)], axis=1)
    return halves.reshape(r, 2, FFP // LANE, LANE).transpose(0, 2, 1, 3).reshape(r, 2 * FFP)


def ff_deinterleave(a):
    r = a.shape[0]
    halves = a.reshape(r, FFP // LANE, 2, LANE).transpose(0, 2, 1, 3).reshape(r, 2, FFP)
    return halves[:, :, :D_FF].reshape(r, 2 * D_FF)


def _rows_into(w, rows, off):
    return jnp.zeros((rows, w.shape[1]), w.dtype).at[off:off + w.shape[0]].set(w)


LATE = ("gla_proj", "rwkv_proj", "w_out", "ffn_up", "ffn_conv_w", "ffn_down")


def local_step(x, tgt, w, nb, t, late_blocks=None):
    n = nb * t
    tm = min(n, 1024)
    tkt = min(n, 2048)
    tr = min(n, 256)
    trl = min(n, 512)
    vec = lambda a: a.reshape(1, -1)
    w = dict(w)

    w_in_p = w_in_to_padded(w["w_in"])
    wa2_f, wa2_b = _rows_into(w["gla_wa2_f"], LANE, 0), _rows_into(w["gla_wa2_b"], LANE, GLA_RANK)
    w2f, w2b = _rows_into(w["rwkv_w2_f"], LANE, 0), _rows_into(w["rwkv_w2_b"], LANE, 0)
    a2 = _rows_into(w["rwkv_a2"], LANE, 64)
    g1, g2n, gf = vec(w["norm1_g"]), vec(w["norm2_g"]), vec(w["norm_f_g"])
    mu_prev, mu_next = vec(w["rwkv_mu_prev"]), vec(w["rwkv_mu_next"])
    pre_params = [vec(w["rwkv_w0_f"]), w2f, vec(w["rwkv_w0_b"]), w2b, vec(w["rwkv_a0"]), a2, w["rwkv_g2"],
                  vec(w["rwkv_k_k"]), vec(w["rwkv_k_a"])]
    post_params = [vec(w["gla_norm_g"]), vec(w["rwkv_ln_w"]), vec(w["rwkv_ln_b"]), vec(w["rwkv_r_k"])]
    ba_f, ba_b = vec(w["gla_ba_f"]), vec(w["gla_ba_b"])

    (h1,) = rowwise_fwd("norm1_fwd", f_norm, [(x, D, 0)], [g1], [(D, MXU_DTYPE)], trl)
    p = matmul("proj_in", h1, w_in_p, "nn", F32, tm, FFP // 2, D)
    s = shift_fwd(p, mu_prev, mu_next, nb, t)
    pre_rows = [(s, 512, 1), (s, LANE, 1536 // LANE), (s, LANE, 1664 // LANE)]
    wf, wb, kp, a_s, b_s, g = rowwise_fwd("rwkv_pre_fwd", f_rwkv_pre, pre_rows, pre_params, [(RW_W, F32)] * 6, tr)
    sc = [to_scan("to_scan_r", s, 0, nb, t, False), to_scan("to_scan_w", wf, 0, nb, t, False, x_bwd=wb),
          to_scan("to_scan_k", kp, 0, nb, t, False), to_scan("to_scan_a", a_s, 0, nb, t, False),
          to_scan("to_scan_b", b_s, 0, nb, t, False), to_scan("to_scan_v", s, 2, nb, t, True)]
    comm = [] if late_blocks is None else [(late_blocks[k], False) for k in LATE]
    y_scf, y_scm, hist_rw, sa_sc, *gathered = rwkv_scan_fwd(*sc, comm=comm)
    for k, g_k in zip(LATE, gathered):
        w[k] = _gathered_to_full(g_k, SHARDED[k])
    ffn_up_p = ff_interleave(w["ffn_up"])
    conv_w_p, conv_b_p = ff_interleave(w["ffn_conv_w"]), ff_interleave(vec(w["ffn_conv_b"]))
    ffn_down_p = jnp.pad(w["ffn_down"], ((0, FFP - D_FF), (0, 0)))
    (y,) = from_scan("from_scan_y", y_scf, y_scm, nb, t, True)
    o_f, hist_f = gla_fwd(p, wa2_f, ba_f, None, nb, t, False)
    o, hist_b = gla_fwd(p, wa2_b, ba_b, o_f, nb, t, True)
    post_rows = [(o, 512, 0), (p, 512, C_OG // 512), (y, 512, 0), (s, 512, 0), (kp, 512, 0), (s, 512, 2), (g, 512, 0)]
    oa, ob = rowwise_fwd("branch_post_fwd", f_branch_post, post_rows, post_params, [(512, MXU_DTYPE)] * 2, tr)
    ya = matmul("gla_proj", oa, w["gla_proj"], "nn", F32, tm, 512, 512)
    yb = matmul("rwkv_proj", ob, w["rwkv_proj"], "nn", F32, tm, 512, 512)
    merge_rows = [(p, D, 0), (p, D, 1), (ya, D, 0), (yb, D, 0)]
    (merged,) = rowwise_fwd("merge_fwd", f_merge, merge_rows, [], [(D, MXU_DTYPE)], trl)
    mo = matmul("w_out", merged, w["w_out"], "nn", F32, tm, 512, D)
    x1, h2 = rowwise_fwd("norm2_fwd", f_norm2, [(x, D, 0), (mo, D, 0)], [g2n], [(D, F32), (D, MXU_DTYPE)], trl)
    u = matmul("ffn_up", h2, ffn_up_p, "nn", F32, tm, FFP // 2, D)
    z = conv_glu_fwd(u, conv_w_p, conv_b_p, nb, t)
    ffo = matmul("ffn_down", z, ffn_down_p, "nn", F32, tm, D, FFP // 2)
    loss, dx2, dgf = loss_head(x1, ffo, tgt, gf, trl)

    dz = matmul("ffn_down_dx", dx2, ffn_down_p, "nt", F32, tm, FFP // 2, D)
    d_ffn_down_p = matmul("ffn_down_dw", z, dx2, "tn", F32, FFP // 2, 512, tkt)
    du, d_conv_w_p, d_conv_b_p = conv_glu_bwd(u, dz, conv_w_p, conv_b_p, nb, t)
    dh2 = matmul("ffn_up_dx", du, ffn_up_p, "nt", F32, tm, D, FFP // 2)
    d_ffn_up_p = matmul("ffn_up_dw", h2, du, "tn", F32, D, 512, tkt)
    (dx1,), (dg2,) = rowwise_bwd("norm2_bwd", f_norm2, [(x, D, 0), (mo, D, 0)], [g2n],
                                 [[(dx2, D, 0)], [(dh2, D, 0)]], tr, grad_rows=[1])
    dmerged = matmul("w_out_dx", dx1, w["w_out"], "nt", F32, tm, D, D)
    d_w_out = matmul("w_out_dw", merged, dx1, "tn", F32, D, 512, tkt)
    (dga, dgb, dya, dyb), _ = rowwise_bwd("merge_bwd", f_merge, merge_rows, [], [[(dmerged, D, 0)]], tr)
    d_oa = matmul("gla_proj_dx", dya, w["gla_proj"], "nt", F32, tm, 512, D)
    d_gla_proj = matmul("gla_proj_dw", oa, dya, "tn", F32, 512, 512, tkt)
    d_ob = matmul("rwkv_proj_dx", dyb, w["rwkv_proj"], "nt", F32, tm, 512, D)
    d_rwkv_proj = matmul("rwkv_proj_dw", ob, dyb, "tn", F32, 512, 512, tkt)
    (d_o, d_og, d_y, d_r_post, d_kp_post, d_v_post, d_g), d_post = rowwise_bwd(
        "branch_post_bwd", f_branch_post, post_rows, post_params, [[(d_oa, 512, 0)], [(d_ob, 512, 0)]], tr)
    late_grads = {"gla_proj": d_gla_proj, "rwkv_proj": d_rwkv_proj, "w_out": d_w_out, "ffn_up": ff_deinterleave(d_ffn_up_p),
                  "ffn_conv_w": ff_deinterleave(d_conv_w_p), "ffn_down": d_ffn_down_p[0:D_FF]}
    comm = [] if late_blocks is None else [(_full_to_slices(late_grads[k], SHARDED[k]), True) for k in LATE]
    dsc = rwkv_scan_bwd(*sc, hist_rw, sa_sc, to_scan("to_scan_dy", d_y, 0, nb, t, True), comm=comm)
    received = dict(zip(LATE, dsc[6:]))
    d_r_scan, d_kp_scan, d_a_scan, d_b_scan = from_scan("from_scan_rkab", dsc[0], dsc[1], nb, t, False)
    d_wf, d_wb = from_scan("from_scan_w", dsc[2], dsc[3], nb, t, False, groups=(0, 1))
    (d_v_scan,) = from_scan("from_scan_dv", dsc[4], dsc[5], nb, t, True)
    (d_k, d_wlal, d_gl), d_pre = rowwise_bwd(
        "rwkv_pre_bwd", f_rwkv_pre, pre_rows, pre_params,
        [[(d_wf, 512, 0)], [(d_wb, 512, 0)], [(d_kp_scan, 512, 0), (d_kp_post, 512, 0)],
         [(d_a_scan, 512, 0)], [(d_b_scan, 512, 0)], [(d_g, 512, 0)]], tr)
    ds = jnp.concatenate([d_r_scan + d_r_post, d_k, d_v_scan + d_v_post, d_wlal, d_gl], axis=1)
    dp_rw, d_mu_prev, d_mu_next = shift_bwd(p, ds, mu_prev, mu_next, nb, t)
    dqkv_f, d_wa2_f, d_ba_f = gla_bwd(p, wa2_f, ba_f, hist_f, d_o, None, nb, t, False)
    dqkv, d_wa2_b, d_ba_b = gla_bwd(p, wa2_b, ba_b, hist_b, d_o, dqkv_f, nb, t, True)
    dp = jnp.concatenate([dga, dgb, dqkv[:, 0:1024], d_og, dp_rw, dqkv[:, 1024:1152],
                          jnp.zeros((n, NP - C_AFAB - LANE), F32)], axis=1).astype(MXU_DTYPE)
    d_w_in_p = matmul("proj_in_dw", h1, dp, "tn", F32, D, 512, tkt)
    grads = {
        "w_in": w_in_from_padded(d_w_in_p),
        "gla_wa2_f": d_wa2_f[0:GLA_RANK], "gla_ba_f": d_ba_f, "gla_wa2_b": d_wa2_b[GLA_RANK:2 * GLA_RANK], "gla_ba_b": d_ba_b,
        "gla_norm_g": d_post[0], "rwkv_mu_prev": d_mu_prev, "rwkv_mu_next": d_mu_next,
        "rwkv_w0_f": d_pre[0], "rwkv_w2_f": d_pre[1][0:64], "rwkv_w0_b": d_pre[2], "rwkv_w2_b": d_pre[3][0:64],
        "rwkv_a0": d_pre[4], "rwkv_a2": d_pre[5][64:128], "rwkv_g2": d_pre[6], "rwkv_k_k": d_pre[7], "rwkv_k_a": d_pre[8],
        "rwkv_r_k": d_post[3], "rwkv_ln_w": d_post[1], "rwkv_ln_b": d_post[2],
        "norm2_g": dg2, "ffn_conv_b": ff_deinterleave(d_conv_b_p), "norm_f_g": dgf, **late_grads,
    }
    early = [k for k in SHARDED if k not in LATE]
    payload = lambda k: _full_to_slices(grads[k], SHARDED[k]).astype(MXU_DTYPE if k == "w_in" else F32)
    comm = [] if late_blocks is None else [(payload(k), True) for k in early]
    dh1, *got = matmul("proj_in_dx", dp, w_in_p, "nt", F32, tm, D, FFP // 2, comm=comm) if comm else \
        [matmul("proj_in_dx", dp, w_in_p, "nt", F32, tm, D, FFP // 2)]
    received.update(zip(early, got))
    (grad_x,), (grads["norm1_g"],) = rowwise_bwd("norm1_bwd", f_norm, [(x, D, 0)], [g1], [[(dh1, D, 0)]], tr,
                                                 adds=[(0, (dx1, D, 0))])
    return loss, grad_x, grads, received


MESH = pl.DeviceIdType.MESH


def remote_exchange(name, items):
    n = len(items)

    def body(*refs):
        start, wait = _exchange_plan([sc for _, sc in items], refs[:n], refs[n:2 * n], *refs[2 * n:])
        start()
        wait()

    args, specs, shapes, sems = _comm_specs(items)
    return pl.pallas_call(body, name=name, in_specs=specs, out_specs=specs, out_shape=shapes, scratch_shapes=sems)(*args)


def gather_two_level(name, blocks):
    n = len(blocks)

    def body(*refs):
        in_refs, out_refs = refs[:n], refs[n:2 * n]
        send_sems, recv_sems, local_sems = refs[2 * n:]
        x, y, c = lax.axis_index("x"), lax.axis_index("y"), lax.axis_index("c")
        me, sibling = (x, y, c), (x, y, 1 - c)
        chips = [(1 - x, y), (x, 1 - y), (1 - x, 1 - y)]

        def copy(i, k, block, to, src=None):
            rows = out_refs[i].at[4 * block[0] + 2 * block[1] + block[2]]
            return pltpu.make_async_remote_copy(src_ref=rows if src is None else src, dst_ref=rows, send_sem=send_sems.at[i, k],
                                                recv_sem=recv_sems.at[i, k], device_id=to, device_id_type=MESH)

        own = [pltpu.make_async_copy(in_refs[i], out_refs[i].at[4 * x + 2 * y + c], local_sems.at[i]) for i in range(n)]
        first = [copy(i, 0, me, sibling, src=in_refs[i]) for i in range(n)]
        first += [copy(i, 1 + j, me, (*chip, c), src=in_refs[i]) for j, chip in enumerate(chips) for i in range(n)]
        for cp in own + first:
            cp.start()
        passed = []
        for j, chip in enumerate(chips):
            for i in range(n):
                copy(i, 1 + j, (*chip, c), me).wait_recv()
                onward = copy(i, 4 + j, (*chip, c), sibling)
                onward.start()
                passed.append(onward)
        for i in range(n):
            copy(i, 0, sibling, me).wait_recv()
        for j, chip in enumerate(chips):
            for i in range(n):
                copy(i, 4 + j, (*chip, 1 - c), me).wait_recv()
        for cp in first + passed:
            cp.wait_send()
        for cp in own:
            cp.wait()

    args, specs, shapes, sems = _comm_specs([(b, False) for b in blocks])
    return pl.pallas_call(body, name=name, in_specs=specs, out_specs=specs, out_shape=shapes, scratch_shapes=sems)(*args)


def _exchange_plan(flags, in_refs, out_refs, send_sems, recv_sems, local_sems):
    x, y, c = lax.axis_index("x"), lax.axis_index("y"), lax.axis_index("c")
    me = 4 * x + 2 * y + c

    def peer(k):
        px = 1 - x if (k >> 2) & 1 else x
        py = 1 - y if (k >> 1) & 1 else y
        pc = 1 - c if k & 1 else c
        return (px, py, pc), 4 * px + 2 * py + pc

    def copies(with_arrivals):
        own, sends, recvs = [], [], []
        for i, scatter in enumerate(flags):
            src = in_refs[i].at[me] if scatter else in_refs[i]
            own.append(pltpu.make_async_copy(src, out_refs[i].at[me], local_sems.at[i]))
        for k in range(1, N_DEV):
            dev, slot = peer(k)
            for i, scatter in enumerate(flags):
                src = in_refs[i].at[slot] if scatter else in_refs[i]
                pair = dict(send_sem=send_sems.at[i, k - 1], recv_sem=recv_sems.at[i, k - 1], device_id=dev, device_id_type=MESH)
                sends.append(pltpu.make_async_remote_copy(src_ref=src, dst_ref=out_refs[i].at[me], **pair))
                if with_arrivals:
                    recvs.append(pltpu.make_async_remote_copy(src_ref=out_refs[i].at[slot], dst_ref=out_refs[i].at[slot], **pair))
        return own, sends, recvs

    def start():
        own, sends, _ = copies(False)
        for cp in own + sends:
            cp.start()

    def wait():
        own, sends, recvs = copies(True)
        for send, recv in zip(sends, recvs):
            recv.wait_recv()
            send.wait_send()
        for cp in own:
            cp.wait()

    return start, wait


def _adam_tiles(r, c):
    tc = 256 if (c % 256 == 0 and r * c > 128 * 1024) else c
    tr = 128 if (r % 128 == 0 and r > 128) else r
    return tr, tc


def adamw_reduce(name, parts, w, m, v):
    lead = w.ndim - 2
    r, c = w.shape[lead:]
    tr, tc = _adam_tiles(r, c)

    def body(p_ref, w_ref, m_ref, v_ref, g_ref, d_ref, nm_ref, nv_ref):
        g = p_ref[0].astype(F32)
        for d in range(1, N_DEV):
            g = g + p_ref[d].astype(F32)
        at = (0,) * lead + (slice(None), slice(None))
        nm = ADAM_B1 * m_ref[at] + (1.0 - ADAM_B1) * g
        nv = ADAM_B2 * v_ref[at] + (1.0 - ADAM_B2) * (g * g)
        m_hat = nm / (1.0 - ADAM_B1 ** ADAM_STEP)
        v_hat = nv / (1.0 - ADAM_B2 ** ADAM_STEP)
        g_ref[at] = g
        d_ref[at] = -ADAM_LR * (m_hat / (jnp.sqrt(v_hat) + ADAM_EPS) + ADAM_WD * w_ref[at])
        nm_ref[at] = nm
        nv_ref[at] = nv

    spec = pl.BlockSpec((1,) * lead + (tr, tc), lambda i, j: (0,) * lead + (i, j))
    return pl.pallas_call(
        body, name=name, grid=(r // tr, c // tc),
        in_specs=[pl.BlockSpec((N_DEV, tr, tc), lambda i, j: (0, i, j)), spec, spec, spec],
        out_specs=[spec] * 4, out_shape=[jax.ShapeDtypeStruct(w.shape, F32)] * 4,
        compiler_params=_cparams(("arbitrary", "arbitrary")),
    )(parts, w, m, v)


SHARDED = {"w_in": 1, "gla_wa2_f": 1, "gla_wa2_b": 1, "gla_proj": 1, "rwkv_w2_f": 1, "rwkv_w2_b": 1, "rwkv_a2": 1,
           "rwkv_g2": 1, "rwkv_proj": 1, "w_out": 0, "ffn_up": 1, "ffn_conv_w": 1, "ffn_down": 0}
BF16_GATHER = ("w_in", "gla_proj", "rwkv_proj", "w_out", "ffn_up", "ffn_down")
REPLICATED = ("norm1_g", "gla_ba_f", "gla_ba_b", "gla_norm_g", "rwkv_mu_prev", "rwkv_mu_next", "rwkv_w0_f", "rwkv_w0_b",
              "rwkv_a0", "rwkv_k_k", "rwkv_k_a", "rwkv_r_k", "rwkv_ln_w", "rwkv_ln_b", "norm2_g", "ffn_conv_b", "norm_f_g")
WEIGHTS = ("norm1_g", "w_in", "gla_wa2_f", "gla_ba_f", "gla_wa2_b", "gla_ba_b", "gla_norm_g", "gla_proj", "rwkv_mu_prev",
           "rwkv_mu_next", "rwkv_w0_f", "rwkv_w2_f", "rwkv_w0_b", "rwkv_w2_b", "rwkv_a0", "rwkv_a2", "rwkv_g2", "rwkv_k_k",
           "rwkv_k_a", "rwkv_r_k", "rwkv_ln_w", "rwkv_ln_b", "rwkv_proj", "w_out", "norm2_g", "ffn_up", "ffn_conv_w",
           "ffn_conv_b", "ffn_down", "norm_f_g")


def _gathered_to_full(g, axis):
    if axis == 0:
        return g.reshape(N_DEV * g.shape[1], g.shape[2])
    return g.transpose(1, 0, 2).reshape(g.shape[1], N_DEV * g.shape[2])


def _full_to_slices(a, axis):
    if axis == 0:
        return a.reshape(N_DEV, a.shape[0] // N_DEV, a.shape[1])
    return a.reshape(a.shape[0], N_DEV, a.shape[1] // N_DEV).transpose(1, 0, 2)


def _pack_rows(size):
    return -(-size // (8 * LANE)) * 8


def _pack(d):
    parts = []
    for k in REPLICATED:
        rows = d[k].reshape(-1, LANE).astype(F32)
        parts.append(jnp.pad(rows, ((0, _pack_rows(rows.size) - rows.shape[0]), (0, 0))))
    return jnp.concatenate(parts, axis=0)


def _unpack(packed, shapes):
    out, pos = {}, 0
    for k in REPLICATED:
        size = int(np.prod(shapes[k]))
        out[k] = packed[pos:pos + size // LANE].reshape(shapes[k])
        pos += _pack_rows(size)
    return out


def kernel(x, norm1_g, w_in, gla_wa2_f, gla_ba_f, gla_wa2_b, gla_ba_b, gla_norm_g, gla_proj, rwkv_mu_prev, rwkv_mu_next, rwkv_w0_f, rwkv_w2_f, rwkv_w0_b, rwkv_w2_b, rwkv_a0, rwkv_a2, rwkv_g2, rwkv_k_k, rwkv_k_a, rwkv_r_k, rwkv_ln_w, rwkv_ln_b, rwkv_proj, w_out, norm2_g, ffn_up, ffn_conv_w, ffn_conv_b, ffn_down, norm_f_g, loss_target, m_norm1_g, m_w_in, m_gla_wa2_f, m_gla_ba_f, m_gla_wa2_b, m_gla_ba_b, m_gla_norm_g, m_gla_proj, m_rwkv_mu_prev, m_rwkv_mu_next, m_rwkv_w0_f, m_rwkv_w2_f, m_rwkv_w0_b, m_rwkv_w2_b, m_rwkv_a0, m_rwkv_a2, m_rwkv_g2, m_rwkv_k_k, m_rwkv_k_a, m_rwkv_r_k, m_rwkv_ln_w, m_rwkv_ln_b, m_rwkv_proj, m_w_out, m_norm2_g, m_ffn_up, m_ffn_conv_w, m_ffn_conv_b, m_ffn_down, m_norm_f_g, v_norm1_g, v_w_in, v_gla_wa2_f, v_gla_ba_f, v_gla_wa2_b, v_gla_ba_b, v_gla_norm_g, v_gla_proj, v_rwkv_mu_prev, v_rwkv_mu_next, v_rwkv_w0_f, v_rwkv_w2_f, v_rwkv_w0_b, v_rwkv_w2_b, v_rwkv_a0, v_rwkv_a2, v_rwkv_g2, v_rwkv_k_k, v_rwkv_k_a, v_rwkv_r_k, v_rwkv_ln_w, v_rwkv_ln_b, v_rwkv_proj, v_w_out, v_norm2_g, v_ffn_up, v_ffn_conv_w, v_ffn_conv_b, v_ffn_down, v_norm_f_g):
    args = locals()
    wts = {k: args[k] for k in WEIGHTS}
    mom = {k: args["m_" + k] for k in WEIGHTS}
    var = {k: args["v_" + k] for k in WEIGHTS}
    shapes = {k: wts[k].shape for k in WEIGHTS}
    nb, t = x.shape[0], x.shape[1]
    mat = lambda a: a.reshape(a.shape[-2], a.shape[-1])

    block = lambda k: mat(wts[k]).astype(MXU_DTYPE) if k in BF16_GATHER else mat(wts[k])
    early = [k for k in SHARDED if k not in LATE]
    gathered = gather_two_level("gather_weights", [block(k) for k in early])
    full = {k: _gathered_to_full(g, SHARDED[k]) for k, g in zip(early, gathered)}
    for k in REPLICATED:
        full[k] = wts[k].reshape(-1) if k in ("norm_f_g", "rwkv_r_k") else wts[k][0]

    loss, grad_x, grads, received = local_step(x.reshape(nb * t, D), loss_target.reshape(nb * t, D), full, nb, t,
                                               late_blocks={k: block(k) for k in LATE})

    (rep_parts,) = remote_exchange("exchange_replicated", [(_pack(grads), False)])

    res = {}
    for k in SHARDED:
        res[k] = adamw_reduce("adamw_" + k, received[k], wts[k], mom[k], var[k])
    packed = adamw_reduce("adamw_replicated", rep_parts, _pack(wts), _pack(mom), _pack(var))
    unpacked = [_unpack(p, shapes) for p in packed]
    for k in REPLICATED:
        res[k] = [u[k] for u in unpacked]

    total = lax.psum(loss[0, 0], ("x", "y", "c"))
    out = [total, grad_x.reshape(x.shape)]
    for j in range(4):
        out += [res[k][j] for k in WEIGHTS]
    return tuple(out)
```
